```python
import math
import jax, jax.numpy as jnp
from jax import lax
import numpy as np

D_MODEL = 2048
BATCH = 8
SEQ = 8192
DEPTH = 4

N_MIXERS = 3
N_A = (DEPTH + 2) // 3
N_B = (DEPTH + 1) // 3
N_C = DEPTH // 3
HEAD_DIM = 128
ROPE_THETA = 500000.0
PARTIAL_ROT = HEAD_DIM // 4
NORM_EPS = 1e-6
NEG = -1e30
A_HEADS = D_MODEL // HEAD_DIM
A_KV_HEADS = A_HEADS // 4
A_HALF_WINDOW = 128
B_HEADS = D_MODEL // HEAD_DIM
B_Q_RANK = 512
B_KV_RANK = 512
B_NOPE = 128
B_ROPE = 64
B_V = 128
B_QK = B_NOPE + B_ROPE
C_PATTERNS = ((128, 1), (512, 4), (2048, 16))
C_GROUPS = len(C_PATTERNS)
C_HEADS = D_MODEL // HEAD_DIM
D_FF = ((8 * D_MODEL // 3 + 255) // 256) * 256
PLE_DIM = 256
Q_BLOCK = 128

kernel_name = "interleaved_hybrid_encoder_swa_mla_dilated"


def rms_norm(x, g):
    xf = x.astype(jnp.float32)
    y = xf * lax.rsqrt(jnp.mean(xf * xf, axis=-1, keepdims=True) + NORM_EPS)
    return (y * g.astype(jnp.float32)).astype(x.dtype)


def rope(x, pos, rot_dim):
    half = rot_dim // 2
    inv = ROPE_THETA ** (-jnp.arange(half, dtype=jnp.float32) * 2.0 / rot_dim)
    ang = pos.astype(jnp.float32)[..., None] * inv
    cos = jnp.cos(ang)[:, :, None, :]
    sin = jnp.sin(ang)[:, :, None, :]
    xr = x[..., :rot_dim].astype(jnp.float32)
    x1, x2 = xr[..., :half], xr[..., half:]
    rot = jnp.concatenate([x1 * cos - x2 * sin, x2 * cos + x1 * sin], axis=-1).astype(x.dtype)
    return jnp.concatenate([rot, x[..., rot_dim:]], axis=-1)


def banded_attention(q, k, v, half_w, sink=None):
    N, L, Hq, hd = q.shape
    Hkv = k.shape[2]
    G = Hq // Hkv
    blk = half_w
    nb = -(-L // blk)
    Lp = nb * blk
    pad = Lp - L
    q = jnp.pad(q, ((0, 0), (0, pad), (0, 0), (0, 0)))
    k = jnp.pad(k, ((0, 0), (blk, pad + blk), (0, 0), (0, 0)))
    v = jnp.pad(v, ((0, 0), (blk, pad + blk), (0, 0), (0, 0)))
    kb = k.reshape(N, nb + 2, blk, Hkv, hd)
    vb = v.reshape(N, nb + 2, blk, Hkv, hd)
    kn = jnp.concatenate([kb[:, :-2], kb[:, 1:-1], kb[:, 2:]], axis=2)
    vn = jnp.concatenate([vb[:, :-2], vb[:, 1:-1], vb[:, 2:]], axis=2)
    qb = q.reshape(N, nb, blk, Hkv, G, hd)
    s = jnp.einsum('nbqkgd,nbjkd->nbkgqj', qb, kn,
                   preferred_element_type=jnp.float32) * (1.0 / math.sqrt(hd))
    qpos = jnp.arange(Lp).reshape(nb, blk)
    kpos = (jnp.arange(nb)[:, None] - 1) * blk + jnp.arange(3 * blk)[None, :]
    valid = ((jnp.abs(qpos[:, :, None] - kpos[:, None, :]) <= half_w)
             & (kpos >= 0)[:, None, :] & (kpos < L)[:, None, :])
    s = jnp.where(valid[None, :, None, None], s, NEG)
    m = jnp.max(s, axis=-1)
    if sink is not None:
        sink_b = sink.astype(jnp.float32).reshape(1, 1, Hkv, G, 1)
        m = jnp.maximum(m, sink_b)
    pr = jnp.exp(s - m[..., None])
    denom = jnp.sum(pr, axis=-1)
    if sink is not None:
        denom = denom + jnp.exp(sink_b - m)
    o = jnp.einsum('nbkgqj,nbjkd->nbqkgd', pr.astype(v.dtype), vn,
                   preferred_element_type=jnp.float32)
    o = o / denom.transpose(0, 1, 4, 2, 3)[..., None]
    lse = (m + jnp.log(denom)).transpose(0, 1, 4, 2, 3).reshape(N, Lp, Hq)[:, :L]
    o = o.reshape(N, Lp, Hq, hd)[:, :L].astype(q.dtype)
    return o, lse


def dense_attention(q, k, v):
    B, S, H, dq = q.shape
    dv = v.shape[-1]
    nq = S // Q_BLOCK
    scale = 1.0 / math.sqrt(dq)
    qb = q.reshape(B, nq, Q_BLOCK, H, dq).transpose(1, 0, 2, 3, 4)

    def one_block(qi):
        s = jnp.einsum('bqhd,bkhd->bhqk', qi, k, preferred_element_type=jnp.float32) * scale
        pr = jax.nn.softmax(s, axis=-1)
        return jnp.einsum('bhqk,bkhd->bqhd', pr.astype(v.dtype), v)

    o = lax.map(one_block, qb)
    return o.transpose(1, 0, 2, 3, 4).reshape(B, S, H, dv)


def mixer_a(hn, pos, w_in, gq, gk, sink, w_o):
    B, S, _ = hn.shape
    qkv = hn @ w_in
    nq = A_HEADS * HEAD_DIM
    nk = A_KV_HEADS * HEAD_DIM
    q = qkv[..., :nq].reshape(B, S, A_HEADS, HEAD_DIM)
    k = qkv[..., nq:nq + nk].reshape(B, S, A_KV_HEADS, HEAD_DIM)
    v = qkv[..., nq + nk:].reshape(B, S, A_KV_HEADS, HEAD_DIM)
    q = rope(rms_norm(q, gq), pos, PARTIAL_ROT)
    k = rope(rms_norm(k, gk), pos, PARTIAL_ROT)
    o, _ = banded_attention(q, k, v, A_HALF_WINDOW, sink)
    return o.reshape(B, S, nq) @ w_o


def mixer_b(hn, pos, w_in, g_qlat, g_kvlat, w_q_up, w_kv_up, gq, gk, w_o):
    B, S, _ = hn.shape
    lat = hn @ w_in
    q_lat = lat[..., :B_Q_RANK]
    kv_lat = lat[..., B_Q_RANK:B_Q_RANK + B_KV_RANK]
    k_rope = lat[..., B_Q_RANK + B_KV_RANK:]
    q = (rms_norm(q_lat, g_qlat) @ w_q_up).reshape(B, S, B_HEADS, B_QK)
    kv = (rms_norm(kv_lat, g_kvlat) @ w_kv_up).reshape(B, S, B_HEADS, B_NOPE + B_V)
    k_nope, v = kv[..., :B_NOPE], kv[..., B_NOPE:]
    k = jnp.concatenate(
        [k_nope, jnp.broadcast_to(k_rope[:, :, None, :], (B, S, B_HEADS, B_ROPE))], axis=-1)
    q = rms_norm(q, gq)
    k = rms_norm(k, gk)
    q = jnp.concatenate([q[..., :B_NOPE], rope(q[..., B_NOPE:], pos, B_ROPE)], axis=-1)
    k = jnp.concatenate([k[..., :B_NOPE], rope(k[..., B_NOPE:], pos, B_ROPE)], axis=-1)
    o = dense_attention(q, k, v)
    return o.reshape(B, S, B_HEADS * B_V) @ w_o


def to_chains(x, dil):
    B, S, H, d = x.shape
    return x.reshape(B, S // dil, dil, H, d).transpose(0, 2, 1, 3, 4).reshape(B * dil, S // dil, H, d)


def from_chains(x, B, dil):
    N, Lc = x.shape[0], x.shape[1]
    rest = x.shape[2:]
    x = x.reshape((B, dil, Lc) + rest)
    x = jnp.moveaxis(x, 1, 2)
    return x.reshape((B, Lc * dil) + rest)


def mixer_c(hn, pos, w_in, gq, gk, w_o):
    B, S, _ = hn.shape
    nq = C_GROUPS * C_HEADS * HEAD_DIM
    nkv = C_HEADS * HEAD_DIM
    qkv = hn @ w_in
    q = qkv[..., :nq].reshape(B, S, C_GROUPS * C_HEADS, HEAD_DIM)
    k = qkv[..., nq:nq + nkv].reshape(B, S, C_HEADS, HEAD_DIM)
    v = qkv[..., nq + nkv:].reshape(B, S, C_HEADS, HEAD_DIM)
    q = rope(rms_norm(q, gq), pos, PARTIAL_ROT)
    k = rope(rms_norm(k, gk), pos, PARTIAL_ROT)
    outs, lses = [], []
    for g, (window, dil) in enumerate(C_PATTERNS):
        qg = q[:, :, g * C_HEADS:(g + 1) * C_HEADS]
        half_steps = window // 2 // dil
        o, lse = banded_attention(to_chains(qg, dil), to_chains(k, dil), to_chains(v, dil), half_steps)
        outs.append(from_chains(o, B, dil))
        lses.append(from_chains(lse, B, dil))
    w = jax.nn.softmax(jnp.stack(lses, axis=0), axis=0)
    o = jnp.sum(w[..., None] * jnp.stack(outs, axis=0).astype(jnp.float32), axis=0).astype(hn.dtype)
    return o.reshape(B, S, nkv) @ w_o


def swiglu(hn, wg, wu, wd):
    return (jax.nn.silu(hn @ wg) * (hn @ wu)) @ wd


def _fwd_setup_inputs(seed: int = 0) -> dict:
    key = jax.random.key(seed)
    ks = iter(jax.random.split(key, 40))
    f32 = jnp.float32

    def nrm(shape, fan_in):
        return jax.random.normal(next(ks), shape, f32) * (fan_in ** -0.5)

    def gain(shape):
        return 1.0 + 0.05 * jax.random.normal(next(ks), shape, f32)

    x = jax.random.normal(next(ks), (BATCH, SEQ, D_MODEL), f32)
    p = jax.random.normal(next(ks), (DEPTH, BATCH, SEQ, PLE_DIM), f32)
    offs = jax.random.randint(next(ks), (BATCH, 1), 0, 4096, dtype=jnp.int32)
    positions = jnp.arange(SEQ, dtype=jnp.int32)[None, :] + offs
    a_in = A_HEADS * HEAD_DIM + 2 * A_KV_HEADS * HEAD_DIM
    b_in = B_Q_RANK + B_KV_RANK + B_ROPE
    c_in = C_GROUPS * C_HEADS * HEAD_DIM + 2 * C_HEADS * HEAD_DIM
    return {
        "x": x,
        "p": p,
        "positions": positions,
        "g_mix": gain((DEPTH, D_MODEL)),
        "g_ffn": gain((DEPTH, D_MODEL)),
        "g_ple": gain((DEPTH, D_MODEL)),
        "w_ple_gate": nrm((DEPTH, D_MODEL, D_MODEL), D_MODEL),
        "w_ple_proj": nrm((DEPTH, PLE_DIM, D_MODEL), PLE_DIM),
        "w_ffn_gate": nrm((DEPTH, D_MODEL, D_FF), D_MODEL),
        "w_ffn_up": nrm((DEPTH, D_MODEL, D_FF), D_MODEL),
        "w_ffn_down": nrm((DEPTH, D_FF, D_MODEL), D_FF),
        "a_w_in": nrm((N_A, D_MODEL, a_in), D_MODEL),
        "a_q_norm": gain((N_A, HEAD_DIM)),
        "a_k_norm": gain((N_A, HEAD_DIM)),
        "a_sink": 0.5 * jax.random.normal(next(ks), (N_A, A_HEADS), f32),
        "a_w_o": nrm((N_A, A_HEADS * HEAD_DIM, D_MODEL), A_HEADS * HEAD_DIM),
        "b_w_in": nrm((N_B, D_MODEL, b_in), D_MODEL),
        "b_q_lat_norm": gain((N_B, B_Q_RANK)),
        "b_kv_lat_norm": gain((N_B, B_KV_RANK)),
        "b_w_q_up": nrm((N_B, B_Q_RANK, B_HEADS * B_QK), B_Q_RANK),
        "b_w_kv_up": nrm((N_B, B_KV_RANK, B_HEADS * (B_NOPE + B_V)), B_KV_RANK),
        "b_q_norm": gain((N_B, B_QK)),
        "b_k_norm": gain((N_B, B_QK)),
        "b_w_o": nrm((N_B, B_HEADS * B_V, D_MODEL), B_HEADS * B_V),
        "c_w_in": nrm((N_C, D_MODEL, c_in), D_MODEL),
        "c_q_norm": gain((N_C, HEAD_DIM)),
        "c_k_norm": gain((N_C, HEAD_DIM)),
        "c_w_o": nrm((N_C, C_HEADS * HEAD_DIM, D_MODEL), C_HEADS * HEAD_DIM),
    }


def _fwd_reference(x, p, positions, g_mix, g_ffn, g_ple, w_ple_gate, w_ple_proj,
              w_ffn_gate, w_ffn_up, w_ffn_down,
              a_w_in, a_q_norm, a_k_norm, a_sink, a_w_o,
              b_w_in, b_q_lat_norm, b_kv_lat_norm, b_w_q_up, b_w_kv_up, b_q_norm, b_k_norm, b_w_o,
              c_w_in, c_q_norm, c_k_norm, c_w_o):
    h = x
    for i in range(DEPTH):
        kind, slot = i % N_MIXERS, i // N_MIXERS
        hn = rms_norm(h, g_mix[i])
        if kind == 0:
            mix = mixer_a(hn, positions, a_w_in[slot], a_q_norm[slot], a_k_norm[slot],
                          a_sink[slot], a_w_o[slot])
        elif kind == 1:
            mix = mixer_b(hn, positions, b_w_in[slot], b_q_lat_norm[slot], b_kv_lat_norm[slot],
                          b_w_q_up[slot], b_w_kv_up[slot], b_q_norm[slot], b_k_norm[slot],
                          b_w_o[slot])
        else:
            mix = mixer_c(hn, positions, c_w_in[slot], c_q_norm[slot], c_k_norm[slot], c_w_o[slot])
        h = h + mix
        h = h + swiglu(rms_norm(h, g_ffn[i]), w_ffn_gate[i], w_ffn_up[i], w_ffn_down[i])
        gate = jax.nn.sigmoid(rms_norm(h, g_ple[i]) @ w_ple_gate[i])
        h = h + gate * (p[i] @ w_ple_proj[i])
    return h


import jax as _jax
import jax.numpy as _jnp

TWIN_FORMAT = 'train_step'
FWD_PARAMS = ['x', 'p', 'positions', 'g_mix', 'g_ffn', 'g_ple', 'w_ple_gate', 'w_ple_proj', 'w_ffn_gate', 'w_ffn_up', 'w_ffn_down', 'a_w_in', 'a_q_norm', 'a_k_norm', 'a_sink', 'a_w_o', 'b_w_in', 'b_q_lat_norm', 'b_kv_lat_norm', 'b_w_q_up', 'b_w_kv_up', 'b_q_norm', 'b_k_norm', 'b_w_o', 'c_w_in', 'c_q_norm', 'c_k_norm', 'c_w_o']
TWIN_WEIGHTS = ['g_mix', 'g_ffn', 'g_ple', 'w_ple_gate', 'w_ple_proj', 'w_ffn_gate', 'w_ffn_up', 'w_ffn_down', 'a_w_in', 'a_q_norm', 'a_k_norm', 'a_sink', 'a_w_o', 'b_w_in', 'b_q_lat_norm', 'b_kv_lat_norm', 'b_w_q_up', 'b_w_kv_up', 'b_q_norm', 'b_k_norm', 'b_w_o', 'c_w_in', 'c_q_norm', 'c_k_norm', 'c_w_o']
TWIN_DIFF_INPUT = 'x'
TWIN_INPUTS = ['x', 'p', 'positions', 'g_mix', 'g_ffn', 'g_ple', 'w_ple_gate', 'w_ple_proj', 'w_ffn_gate', 'w_ffn_up', 'w_ffn_down', 'a_w_in', 'a_q_norm', 'a_k_norm', 'a_sink', 'a_w_o', 'b_w_in', 'b_q_lat_norm', 'b_kv_lat_norm', 'b_w_q_up', 'b_w_kv_up', 'b_q_norm', 'b_k_norm', 'b_w_o', 'c_w_in', 'c_q_norm', 'c_k_norm', 'c_w_o', 'loss_target', 'm_g_mix', 'm_g_ffn', 'm_g_ple', 'm_w_ple_gate', 'm_w_ple_proj', 'm_w_ffn_gate', 'm_w_ffn_up', 'm_w_ffn_down', 'm_a_w_in', 'm_a_q_norm', 'm_a_k_norm', 'm_a_sink', 'm_a_w_o', 'm_b_w_in', 'm_b_q_lat_norm', 'm_b_kv_lat_norm', 'm_b_w_q_up', 'm_b_w_kv_up', 'm_b_q_norm', 'm_b_k_norm', 'm_b_w_o', 'm_c_w_in', 'm_c_q_norm', 'm_c_k_norm', 'm_c_w_o', 'v_g_mix', 'v_g_ffn', 'v_g_ple', 'v_w_ple_gate', 'v_w_ple_proj', 'v_w_ffn_gate', 'v_w_ffn_up', 'v_w_ffn_down', 'v_a_w_in', 'v_a_q_norm', 'v_a_k_norm', 'v_a_sink', 'v_a_w_o', 'v_b_w_in', 'v_b_q_lat_norm', 'v_b_kv_lat_norm', 'v_b_w_q_up', 'v_b_w_kv_up', 'v_b_q_norm', 'v_b_k_norm', 'v_b_w_o', 'v_c_w_in', 'v_c_q_norm', 'v_c_k_norm', 'v_c_w_o']
TWIN_OUTPUTS = ['loss', 'grad_x', 'grad_g_mix', 'grad_g_ffn', 'grad_g_ple', 'grad_w_ple_gate', 'grad_w_ple_proj', 'grad_w_ffn_gate', 'grad_w_ffn_up', 'grad_w_ffn_down', 'grad_a_w_in', 'grad_a_q_norm', 'grad_a_k_norm', 'grad_a_sink', 'grad_a_w_o', 'grad_b_w_in', 'grad_b_q_lat_norm', 'grad_b_kv_lat_norm', 'grad_b_w_q_up', 'grad_b_w_kv_up', 'grad_b_q_norm', 'grad_b_k_norm', 'grad_b_w_o', 'grad_c_w_in', 'grad_c_q_norm', 'grad_c_k_norm', 'grad_c_w_o', 'delta_g_mix', 'delta_g_ffn', 'delta_g_ple', 'delta_w_ple_gate', 'delta_w_ple_proj', 'delta_w_ffn_gate', 'delta_w_ffn_up', 'delta_w_ffn_down', 'delta_a_w_in', 'delta_a_q_norm', 'delta_a_k_norm', 'delta_a_sink', 'delta_a_w_o', 'delta_b_w_in', 'delta_b_q_lat_norm', 'delta_b_kv_lat_norm', 'delta_b_w_q_up', 'delta_b_w_kv_up', 'delta_b_q_norm', 'delta_b_k_norm', 'delta_b_w_o', 'delta_c_w_in', 'delta_c_q_norm', 'delta_c_k_norm', 'delta_c_w_o', 'new_m_g_mix', 'new_m_g_ffn', 'new_m_g_ple', 'new_m_w_ple_gate', 'new_m_w_ple_proj', 'new_m_w_ffn_gate', 'new_m_w_ffn_up', 'new_m_w_ffn_down', 'new_m_a_w_in', 'new_m_a_q_norm', 'new_m_a_k_norm', 'new_m_a_sink', 'new_m_a_w_o', 'new_m_b_w_in', 'new_m_b_q_lat_norm', 'new_m_b_kv_lat_norm', 'new_m_b_w_q_up', 'new_m_b_w_kv_up', 'new_m_b_q_norm', 'new_m_b_k_norm', 'new_m_b_w_o', 'new_m_c_w_in', 'new_m_c_q_norm', 'new_m_c_k_norm', 'new_m_c_w_o', 'new_v_g_mix', 'new_v_g_ffn', 'new_v_g_ple', 'new_v_w_ple_gate', 'new_v_w_ple_proj', 'new_v_w_ffn_gate', 'new_v_w_ffn_up', 'new_v_w_ffn_down', 'new_v_a_w_in', 'new_v_a_q_norm', 'new_v_a_k_norm', 'new_v_a_sink', 'new_v_a_w_o', 'new_v_b_w_in', 'new_v_b_q_lat_norm', 'new_v_b_kv_lat_norm', 'new_v_b_w_q_up', 'new_v_b_w_kv_up', 'new_v_b_q_norm', 'new_v_b_k_norm', 'new_v_b_w_o', 'new_v_c_w_in', 'new_v_c_q_norm', 'new_v_c_k_norm', 'new_v_c_w_o']
TWIN_LEAF_KINDS = {'loss': 'loss', 'grad_x': 'grad_x', 'grad_g_mix': 'grad_w', 'grad_g_ffn': 'grad_w', 'grad_g_ple': 'grad_w', 'grad_w_ple_gate': 'grad_w', 'grad_w_ple_proj': 'grad_w', 'grad_w_ffn_gate': 'grad_w', 'grad_w_ffn_up': 'grad_w', 'grad_w_ffn_down': 'grad_w', 'grad_a_w_in': 'grad_w', 'grad_a_q_norm': 'grad_w', 'grad_a_k_norm': 'grad_w', 'grad_a_sink': 'grad_w', 'grad_a_w_o': 'grad_w', 'grad_b_w_in': 'grad_w', 'grad_b_q_lat_norm': 'grad_w', 'grad_b_kv_lat_norm': 'grad_w', 'grad_b_w_q_up': 'grad_w', 'grad_b_w_kv_up': 'grad_w', 'grad_b_q_norm': 'grad_w', 'grad_b_k_norm': 'grad_w', 'grad_b_w_o': 'grad_w', 'grad_c_w_in': 'grad_w', 'grad_c_q_norm': 'grad_w', 'grad_c_k_norm': 'grad_w', 'grad_c_w_o': 'grad_w', 'delta_g_mix': 'delta_w', 'delta_g_ffn': 'delta_w', 'delta_g_ple': 'delta_w', 'delta_w_ple_gate': 'delta_w', 'delta_w_ple_proj': 'delta_w', 'delta_w_ffn_gate': 'delta_w', 'delta_w_ffn_up': 'delta_w', 'delta_w_ffn_down': 'delta_w', 'delta_a_w_in': 'delta_w', 'delta_a_q_norm': 'delta_w', 'delta_a_k_norm': 'delta_w', 'delta_a_sink': 'delta_w', 'delta_a_w_o': 'delta_w', 'delta_b_w_in': 'delta_w', 'delta_b_q_lat_norm': 'delta_w', 'delta_b_kv_lat_norm': 'delta_w', 'delta_b_w_q_up': 'delta_w', 'delta_b_w_kv_up': 'delta_w', 'delta_b_q_norm': 'delta_w', 'delta_b_k_norm': 'delta_w', 'delta_b_w_o': 'delta_w', 'delta_c_w_in': 'delta_w', 'delta_c_q_norm': 'delta_w', 'delta_c_k_norm': 'delta_w', 'delta_c_w_o': 'delta_w', 'new_m_g_mix': 'new_m', 'new_m_g_ffn': 'new_m', 'new_m_g_ple': 'new_m', 'new_m_w_ple_gate': 'new_m', 'new_m_w_ple_proj': 'new_m', 'new_m_w_ffn_gate': 'new_m', 'new_m_w_ffn_up': 'new_m', 'new_m_w_ffn_down': 'new_m', 'new_m_a_w_in': 'new_m', 'new_m_a_q_norm': 'new_m', 'new_m_a_k_norm': 'new_m', 'new_m_a_sink': 'new_m', 'new_m_a_w_o': 'new_m', 'new_m_b_w_in': 'new_m', 'new_m_b_q_lat_norm': 'new_m', 'new_m_b_kv_lat_norm': 'new_m', 'new_m_b_w_q_up': 'new_m', 'new_m_b_w_kv_up': 'new_m', 'new_m_b_q_norm': 'new_m', 'new_m_b_k_norm': 'new_m', 'new_m_b_w_o': 'new_m', 'new_m_c_w_in': 'new_m', 'new_m_c_q_norm': 'new_m', 'new_m_c_k_norm': 'new_m', 'new_m_c_w_o': 'new_m', 'new_v_g_mix': 'new_v', 'new_v_g_ffn': 'new_v', 'new_v_g_ple': 'new_v', 'new_v_w_ple_gate': 'new_v', 'new_v_w_ple_proj': 'new_v', 'new_v_w_ffn_gate': 'new_v', 'new_v_w_ffn_up': 'new_v', 'new_v_w_ffn_down': 'new_v', 'new_v_a_w_in': 'new_v', 'new_v_a_q_norm': 'new_v', 'new_v_a_k_norm': 'new_v', 'new_v_a_sink': 'new_v', 'new_v_a_w_o': 'new_v', 'new_v_b_w_in': 'new_v', 'new_v_b_q_lat_norm': 'new_v', 'new_v_b_kv_lat_norm': 'new_v', 'new_v_b_w_q_up': 'new_v', 'new_v_b_w_kv_up': 'new_v', 'new_v_b_q_norm': 'new_v', 'new_v_b_k_norm': 'new_v', 'new_v_b_w_o': 'new_v', 'new_v_c_w_in': 'new_v', 'new_v_c_q_norm': 'new_v', 'new_v_c_k_norm': 'new_v', 'new_v_c_w_o': 'new_v'}


def _forward(args):
    return _fwd_reference(*[args[k] for k in FWD_PARAMS])


def _output_shape():
    def fwd():
        inp = _fwd_setup_inputs(0)
        return _fwd_reference(*[inp[k] for k in FWD_PARAMS])
    out = _jax.eval_shape(fwd)
    return out.shape, out.dtype

N_MICROBATCH = 1
ADAM_LR = 0.001
ADAM_B1 = 0.9
ADAM_B2 = 0.999
ADAM_EPS = 1e-08
ADAM_WD = 0.01
ADAM_STEP = 10
PER_EXAMPLE_BATCH_AXIS = {'x': 0, 'p': 1, 'positions': 0, 'loss_target': 0}
SHARED_INPUTS = []
_WEIGHT_DTYPES = {'g_mix': _jnp.float32, 'g_ffn': _jnp.float32, 'g_ple': _jnp.float32, 'w_ple_gate': _jnp.float32, 'w_ple_proj': _jnp.float32, 'w_ffn_gate': _jnp.float32, 'w_ffn_up': _jnp.float32, 'w_ffn_down': _jnp.float32, 'a_w_in': _jnp.float32, 'a_q_norm': _jnp.float32, 'a_k_norm': _jnp.float32, 'a_sink': _jnp.float32, 'a_w_o': _jnp.float32, 'b_w_in': _jnp.float32, 'b_q_lat_norm': _jnp.float32, 'b_kv_lat_norm': _jnp.float32, 'b_w_q_up': _jnp.float32, 'b_w_kv_up': _jnp.float32, 'b_q_norm': _jnp.float32, 'b_k_norm': _jnp.float32, 'b_w_o': _jnp.float32, 'c_w_in': _jnp.float32, 'c_q_norm': _jnp.float32, 'c_k_norm': _jnp.float32, 'c_w_o': _jnp.float32}
MOMENT_SCALE = {'g_mix': 1.793081e-01, 'g_ffn': 2.483339e+01, 'g_ple': 9.246475e-01, 'w_ple_gate': 8.318237e-02, 'w_ple_proj': 4.258860e-01, 'w_ffn_gate': 1.595062e-01, 'w_ffn_up': 1.684850e-01, 'w_ffn_down': 2.688418e-01, 'a_w_in': 1.162349e-01, 'a_q_norm': 3.846001e+00, 'a_k_norm': 3.868676e+00, 'a_sink': 8.013582e-02, 'a_w_o': 1.034562e-01, 'b_w_in': 1.867144e-01, 'b_q_lat_norm': 9.733359e-02, 'b_kv_lat_norm': 5.397454e-01, 'b_w_q_up': 4.104301e-02, 'b_w_kv_up': 8.093599e-02, 'b_q_norm': 6.752546e-01, 'b_k_norm': 6.771504e-01, 'b_w_o': 1.017753e-01, 'c_w_in': 4.968264e-02, 'c_q_norm': 2.110516e+00, 'c_k_norm': 2.127791e+00, 'c_w_o': 8.983725e-02}


def _to_microbatches(a, axis):
    t = _jnp.moveaxis(a, axis, 0)
    t = t.reshape((N_MICROBATCH, t.shape[0] // N_MICROBATCH) + t.shape[1:])
    return _jnp.moveaxis(t, 1, axis + 1)


def setup_inputs(seed: int = 0) -> dict:
    inp = _fwd_setup_inputs(seed)
    key = _jax.random.fold_in(_jax.random.key(seed), 7919)
    shape, _ = _output_shape()
    out = dict(inp)
    out["loss_target"] = _jax.random.normal(_jax.random.fold_in(key, 0), shape, _jnp.float32)
    for i, name in enumerate(TWIN_WEIGHTS):
        w = inp[name].astype(_jnp.float32)
        if MOMENT_SCALE is None:
            s = _jnp.sqrt(_jnp.mean(_jnp.square(w)) + 1e-30)
        else:
            s = MOMENT_SCALE[name]
        km, kv = _jax.random.split(_jax.random.fold_in(key, i + 1))
        out[name] = w
        out["m_" + name] = s * _jax.random.normal(km, w.shape, _jnp.float32)
        out["v_" + name] = (s * s) * _jax.random.uniform(kv, w.shape, _jnp.float32, 0.5, 1.5)
    if N_MICROBATCH > 1:
        for name, axis in PER_EXAMPLE_BATCH_AXIS.items():
            out[name] = _to_microbatches(out[name], axis)
    return {'x': out['x'], 'p': out['p'], 'positions': out['positions'], 'g_mix': out['g_mix'], 'g_ffn': out['g_ffn'], 'g_ple': out['g_ple'], 'w_ple_gate': out['w_ple_gate'], 'w_ple_proj': out['w_ple_proj'], 'w_ffn_gate': out['w_ffn_gate'], 'w_ffn_up': out['w_ffn_up'], 'w_ffn_down': out['w_ffn_down'], 'a_w_in': out['a_w_in'], 'a_q_norm': out['a_q_norm'], 'a_k_norm': out['a_k_norm'], 'a_sink': out['a_sink'], 'a_w_o': out['a_w_o'], 'b_w_in': out['b_w_in'], 'b_q_lat_norm': out['b_q_lat_norm'], 'b_kv_lat_norm': out['b_kv_lat_norm'], 'b_w_q_up': out['b_w_q_up'], 'b_w_kv_up': out['b_w_kv_up'], 'b_q_norm': out['b_q_norm'], 'b_k_norm': out['b_k_norm'], 'b_w_o': out['b_w_o'], 'c_w_in': out['c_w_in'], 'c_q_norm': out['c_q_norm'], 'c_k_norm': out['c_k_norm'], 'c_w_o': out['c_w_o'], 'loss_target': out['loss_target'], 'm_g_mix': out['m_g_mix'], 'm_g_ffn': out['m_g_ffn'], 'm_g_ple': out['m_g_ple'], 'm_w_ple_gate': out['m_w_ple_gate'], 'm_w_ple_proj': out['m_w_ple_proj'], 'm_w_ffn_gate': out['m_w_ffn_gate'], 'm_w_ffn_up': out['m_w_ffn_up'], 'm_w_ffn_down': out['m_w_ffn_down'], 'm_a_w_in': out['m_a_w_in'], 'm_a_q_norm': out['m_a_q_norm'], 'm_a_k_norm': out['m_a_k_norm'], 'm_a_sink': out['m_a_sink'], 'm_a_w_o': out['m_a_w_o'], 'm_b_w_in': out['m_b_w_in'], 'm_b_q_lat_norm': out['m_b_q_lat_norm'], 'm_b_kv_lat_norm': out['m_b_kv_lat_norm'], 'm_b_w_q_up': out['m_b_w_q_up'], 'm_b_w_kv_up': out['m_b_w_kv_up'], 'm_b_q_norm': out['m_b_q_norm'], 'm_b_k_norm': out['m_b_k_norm'], 'm_b_w_o': out['m_b_w_o'], 'm_c_w_in': out['m_c_w_in'], 'm_c_q_norm': out['m_c_q_norm'], 'm_c_k_norm': out['m_c_k_norm'], 'm_c_w_o': out['m_c_w_o'], 'v_g_mix': out['v_g_mix'], 'v_g_ffn': out['v_g_ffn'], 'v_g_ple': out['v_g_ple'], 'v_w_ple_gate': out['v_w_ple_gate'], 'v_w_ple_proj': out['v_w_ple_proj'], 'v_w_ffn_gate': out['v_w_ffn_gate'], 'v_w_ffn_up': out['v_w_ffn_up'], 'v_w_ffn_down': out['v_w_ffn_down'], 'v_a_w_in': out['v_a_w_in'], 'v_a_q_norm': out['v_a_q_norm'], 'v_a_k_norm': out['v_a_k_norm'], 'v_a_sink': out['v_a_sink'], 'v_a_w_o': out['v_a_w_o'], 'v_b_w_in': out['v_b_w_in'], 'v_b_q_lat_norm': out['v_b_q_lat_norm'], 'v_b_kv_lat_norm': out['v_b_kv_lat_norm'], 'v_b_w_q_up': out['v_b_w_q_up'], 'v_b_w_kv_up': out['v_b_w_kv_up'], 'v_b_q_norm': out['v_b_q_norm'], 'v_b_k_norm': out['v_b_k_norm'], 'v_b_w_o': out['v_b_w_o'], 'v_c_w_in': out['v_c_w_in'], 'v_c_q_norm': out['v_c_q_norm'], 'v_c_k_norm': out['v_c_k_norm'], 'v_c_w_o': out['v_c_w_o']}


def _loss(weights, diff, rest, loss_target):
    with _jax.named_scope("forward"):
        args = {**rest, TWIN_DIFF_INPUT: diff, **{k: w.astype(_WEIGHT_DTYPES[k]) for k, w in weights.items()}}
        y = _forward(args)
    with _jax.named_scope("loss_head"):
        err = _jnp.square(y.astype(_jnp.float32) - loss_target)
        return 0.5 * _jnp.sum(_jnp.mean(err, axis=-1)) if err.ndim else 0.5 * err


def _adamw(w, g, m, v):
    m = ADAM_B1 * m + (1.0 - ADAM_B1) * g
    v = ADAM_B2 * v + (1.0 - ADAM_B2) * _jnp.square(g)
    m_hat = m / (1.0 - ADAM_B1 ** ADAM_STEP)
    v_hat = v / (1.0 - ADAM_B2 ** ADAM_STEP)
    delta = -ADAM_LR * (m_hat / (_jnp.sqrt(v_hat) + ADAM_EPS) + ADAM_WD * w)
    return delta, m, v


def reference(x, p, positions, g_mix, g_ffn, g_ple, w_ple_gate, w_ple_proj, w_ffn_gate, w_ffn_up, w_ffn_down, a_w_in, a_q_norm, a_k_norm, a_sink, a_w_o, b_w_in, b_q_lat_norm, b_kv_lat_norm, b_w_q_up, b_w_kv_up, b_q_norm, b_k_norm, b_w_o, c_w_in, c_q_norm, c_k_norm, c_w_o, loss_target, m_g_mix, m_g_ffn, m_g_ple, m_w_ple_gate, m_w_ple_proj, m_w_ffn_gate, m_w_ffn_up, m_w_ffn_down, m_a_w_in, m_a_q_norm, m_a_k_norm, m_a_sink, m_a_w_o, m_b_w_in, m_b_q_lat_norm, m_b_kv_lat_norm, m_b_w_q_up, m_b_w_kv_up, m_b_q_norm, m_b_k_norm, m_b_w_o, m_c_w_in, m_c_q_norm, m_c_k_norm, m_c_w_o, v_g_mix, v_g_ffn, v_g_ple, v_w_ple_gate, v_w_ple_proj, v_w_ffn_gate, v_w_ffn_up, v_w_ffn_down, v_a_w_in, v_a_q_norm, v_a_k_norm, v_a_sink, v_a_w_o, v_b_w_in, v_b_q_lat_norm, v_b_kv_lat_norm, v_b_w_q_up, v_b_w_kv_up, v_b_q_norm, v_b_k_norm, v_b_w_o, v_c_w_in, v_c_q_norm, v_c_k_norm, v_c_w_o):
    given = dict(x=x, p=p, positions=positions, g_mix=g_mix, g_ffn=g_ffn, g_ple=g_ple, w_ple_gate=w_ple_gate, w_ple_proj=w_ple_proj, w_ffn_gate=w_ffn_gate, w_ffn_up=w_ffn_up, w_ffn_down=w_ffn_down, a_w_in=a_w_in, a_q_norm=a_q_norm, a_k_norm=a_k_norm, a_sink=a_sink, a_w_o=a_w_o, b_w_in=b_w_in, b_q_lat_norm=b_q_lat_norm, b_kv_lat_norm=b_kv_lat_norm, b_w_q_up=b_w_q_up, b_w_kv_up=b_w_kv_up, b_q_norm=b_q_norm, b_k_norm=b_k_norm, b_w_o=b_w_o, c_w_in=c_w_in, c_q_norm=c_q_norm, c_k_norm=c_k_norm, c_w_o=c_w_o, loss_target=loss_target, m_g_mix=m_g_mix, m_g_ffn=m_g_ffn, m_g_ple=m_g_ple, m_w_ple_gate=m_w_ple_gate, m_w_ple_proj=m_w_ple_proj, m_w_ffn_gate=m_w_ffn_gate, m_w_ffn_up=m_w_ffn_up, m_w_ffn_down=m_w_ffn_down, m_a_w_in=m_a_w_in, m_a_q_norm=m_a_q_norm, m_a_k_norm=m_a_k_norm, m_a_sink=m_a_sink, m_a_w_o=m_a_w_o, m_b_w_in=m_b_w_in, m_b_q_lat_norm=m_b_q_lat_norm, m_b_kv_lat_norm=m_b_kv_lat_norm, m_b_w_q_up=m_b_w_q_up, m_b_w_kv_up=m_b_w_kv_up, m_b_q_norm=m_b_q_norm, m_b_k_norm=m_b_k_norm, m_b_w_o=m_b_w_o, m_c_w_in=m_c_w_in, m_c_q_norm=m_c_q_norm, m_c_k_norm=m_c_k_norm, m_c_w_o=m_c_w_o, v_g_mix=v_g_mix, v_g_ffn=v_g_ffn, v_g_ple=v_g_ple, v_w_ple_gate=v_w_ple_gate, v_w_ple_proj=v_w_ple_proj, v_w_ffn_gate=v_w_ffn_gate, v_w_ffn_up=v_w_ffn_up, v_w_ffn_down=v_w_ffn_down, v_a_w_in=v_a_w_in, v_a_q_norm=v_a_q_norm, v_a_k_norm=v_a_k_norm, v_a_sink=v_a_sink, v_a_w_o=v_a_w_o, v_b_w_in=v_b_w_in, v_b_q_lat_norm=v_b_q_lat_norm, v_b_kv_lat_norm=v_b_kv_lat_norm, v_b_w_q_up=v_b_w_q_up, v_b_w_kv_up=v_b_w_kv_up, v_b_q_norm=v_b_q_norm, v_b_k_norm=v_b_k_norm, v_b_w_o=v_b_w_o, v_c_w_in=v_c_w_in, v_c_q_norm=v_c_q_norm, v_c_k_norm=v_c_k_norm, v_c_w_o=v_c_w_o)
    weights = {n: given[n] for n in TWIN_WEIGHTS}
    shared = {n: given[n] for n in SHARED_INPUTS}
    per_example = {n: given[n] for n in ['x', 'p', 'positions']}
    grad_fn = _jax.value_and_grad(_loss, argnums=(0, 1))

    def one_microbatch(ex, loss_target):
        ex = dict(ex)
        diff = ex.pop(TWIN_DIFF_INPUT)
        return grad_fn(weights, diff, {**shared, **ex}, loss_target)

    if N_MICROBATCH == 1:
        loss, (grad_w, grad_x) = one_microbatch(per_example, given["loss_target"])
    else:
        def body(carry, xs):
            loss_sum, grad_sum = carry
            l_k, (gw_k, gx_k) = one_microbatch(xs[0], xs[1])
            with _jax.named_scope("update"):
                return (loss_sum + l_k, _jax.tree.map(_jnp.add, grad_sum, gw_k)), gx_k

        init = (_jnp.zeros((), _jnp.float32), _jax.tree.map(_jnp.zeros_like, weights))
        (loss, grad_w), grad_x = _jax.lax.scan(body, init, (per_example, given["loss_target"]))
    with _jax.named_scope("update"):
        delta_w, new_m, new_v = {}, {}, {}
        for n in TWIN_WEIGHTS:
            delta_w[n], new_m[n], new_v[n] = _adamw(weights[n], grad_w[n], given["m_" + n], given["v_" + n])
    return (loss, grad_x, *[grad_w[n] for n in TWIN_WEIGHTS], *[delta_w[n] for n in TWIN_WEIGHTS],
            *[new_m[n] for n in TWIN_WEIGHTS], *[new_v[n] for n in TWIN_WEIGHTS])
```

```python
import functools
import math

import numpy as np
import jax
import jax.numpy as jnp
from jax import lax
from jax.experimental import pallas as pl
from jax.experimental.pallas import tpu as pltpu

F32 = jnp.float32
BF = jnp.bfloat16

D_MODEL = 2048
DEPTH = 4
HEAD_DIM = 128
ROPE_THETA = 500000.0
PARTIAL_ROT = HEAD_DIM // 4
NORM_EPS = 1e-6
NEG = -1e30
A_HEADS = 16
A_KV_HEADS = 4
A_HALF_WINDOW = 128
B_HEADS = 16
B_Q_RANK = 512
B_KV_RANK = 512
B_NOPE = 128
B_ROPE = 64
B_QK = B_NOPE + B_ROPE
B_PAD = 256
C_PATTERNS = ((128, 1), (512, 4), (2048, 16))
C_HEADS = 16
C_GROUPS = 3
ADAM_LR = 0.001
ADAM_B1 = 0.9
ADAM_B2 = 0.999
ADAM_EPS = 1e-08
ADAM_WD = 0.01
ADAM_STEP = 10

LANES = 128
SUBLANES = 8
VMEM_LIMIT_BYTES = 56 * 1024 * 1024
BAND_BLOCK = 256
DENSE_BLOCK = 512
MESH = pl.DeviceIdType.MESH

BIG = ("w_ple_gate", "w_ple_proj", "w_ffn_gate", "w_ffn_up", "w_ffn_down", "a_w_in", "a_w_o",
       "b_w_in", "b_w_q_up", "b_w_kv_up", "b_w_o", "c_w_in", "c_w_o")
SHARD_AXIS = {"w_ple_gate": 1, "w_ple_proj": 2, "w_ffn_gate": 2, "w_ffn_up": 2, "w_ffn_down": 1,
              "a_w_in": 2, "a_w_o": 1, "b_w_in": 1, "b_w_q_up": 2, "b_w_kv_up": 2, "b_w_o": 1,
              "c_w_in": 2, "c_w_o": 1}
SMALL = ("g_mix", "g_ffn", "g_ple", "a_q_norm", "a_k_norm", "a_sink", "b_q_lat_norm",
         "b_kv_lat_norm", "b_q_norm", "b_k_norm", "c_q_norm", "c_k_norm")
WEIGHTS = ("g_mix", "g_ffn", "g_ple", "w_ple_gate", "w_ple_proj", "w_ffn_gate", "w_ffn_up",
           "w_ffn_down", "a_w_in", "a_q_norm", "a_k_norm", "a_sink", "a_w_o", "b_w_in",
           "b_q_lat_norm", "b_kv_lat_norm", "b_w_q_up", "b_w_kv_up", "b_q_norm", "b_k_norm",
           "b_w_o", "c_w_in", "c_q_norm", "c_k_norm", "c_w_o")
SMALL_COLS = 1024


def _params(semantics):
    return pltpu.CompilerParams(dimension_semantics=semantics, vmem_limit_bytes=VMEM_LIMIT_BYTES)


def _tile(dim, cands=(1024, 1408, 512, 256, 128)):
    for c in cands:
        if dim % c == 0:
            return c
    return dim


def _row_tile(rows, cols, target_elems=1 << 19):
    best = None
    for t in range(16, rows + 1, 16):
        if rows % t == 0 and t * cols <= target_elems:
            best = t
    return best if best is not None else rows


def _sigmoid(x):
    return 1.0 / (1.0 + jnp.exp(-x))


def matmul(pairs, mode, out_dtype, name, res=None, swiglu=None):
    a0, b0 = pairs[0]
    if mode == "nn":
        (M, K), N = a0.shape, b0.shape[1]
    elif mode == "nt":
        (M, K), N = a0.shape, b0.shape[0]
    else:
        (K, M), N = a0.shape, b0.shape[1]
    tm, tn, tk = _tile(M), _tile(N), _tile(K, (1024, 512, 256, 128))
    nk = K // tk
    if mode == "nn":
        a_spec = pl.BlockSpec((tm, tk), lambda i, j, k: (i, k))
        b_spec = pl.BlockSpec((tk, tn), lambda i, j, k: (k, j))
        dims = (((1,), (0,)), ((), ()))
    elif mode == "nt":
        a_spec = pl.BlockSpec((tm, tk), lambda i, j, k: (i, k))
        b_spec = pl.BlockSpec((tn, tk), lambda i, j, k: (j, k))
        dims = (((1,), (1,)), ((), ()))
    else:
        a_spec = pl.BlockSpec((tk, tm), lambda i, j, k: (k, i))
        b_spec = pl.BlockSpec((tk, tn), lambda i, j, k: (k, j))
        dims = (((0,), (0,)), ((), ()))
    mn_spec = pl.BlockSpec((tm, tn), lambda i, j, k: (i, j))
    npairs = len(pairs)
    extras = [] if res is None else [res]
    if swiglu is not None:
        extras = list(swiglu)
    nex = len(extras)
    nout = 2 if swiglu is not None else 1

    def body(*refs):
        ins = refs[:2 * npairs]
        ex = refs[2 * npairs:2 * npairs + nex]
        outs = refs[2 * npairs + nex:2 * npairs + nex + nout]
        acc = refs[-1]
        k = pl.program_id(2)

        @pl.when(k == 0)
        def _():
            acc[...] = jnp.zeros_like(acc)

        part = None
        for p in range(npairs):
            d = lax.dot_general(ins[2 * p][...].astype(BF), ins[2 * p + 1][...].astype(BF), dims,
                                preferred_element_type=F32)
            part = d if part is None else part + d
        acc[...] += part

        @pl.when(k == nk - 1)
        def _():
            r = acc[...]
            if swiglu is not None:
                a = ex[0][...].astype(F32)
                b = ex[1][...].astype(F32)
                sg = _sigmoid(a)
                outs[0][...] = (r * b * (sg * (1.0 + a * (1.0 - sg)))).astype(out_dtype)
                outs[1][...] = (r * (a * sg)).astype(out_dtype)
            elif res is not None:
                outs[0][...] = (ex[0][...] + r).astype(out_dtype)
            else:
                outs[0][...] = r.astype(out_dtype)

    in_specs = []
    operands = []
    for a, b in pairs:
        in_specs += [a_spec, b_spec]
        operands += [a, b]
    in_specs += [mn_spec] * nex
    operands += extras
    out_shape = [jax.ShapeDtypeStruct((M, N), out_dtype)] * nout
    outs = pl.pallas_call(
        body, out_shape=out_shape, grid=(M // tm, N // tn, nk), in_specs=in_specs,
        out_specs=[mn_spec] * nout, scratch_shapes=[pltpu.VMEM((tm, tn), F32)],
        compiler_params=_params(("parallel", "parallel", "arbitrary")), name=name,
    )(*operands)
    return outs if nout > 1 else outs[0]


def rowwise(fn, ins, out_dtypes, name):
    rows, cols = ins[0].shape
    tr = _row_tile(rows, cols)
    nin = len(ins)

    def body(*refs):
        vals = fn(*[r[...] for r in refs[:nin]])
        for o, v in zip(refs[nin:], vals):
            o[...] = v.astype(o.dtype)

    spec = pl.BlockSpec((tr, cols), lambda i: (i, 0))
    outs = pl.pallas_call(
        body, out_shape=[jax.ShapeDtypeStruct((rows, cols), d) for d in out_dtypes],
        grid=(rows // tr,), in_specs=[spec] * nin, out_specs=[spec] * len(out_dtypes),
        compiler_params=_params(("parallel",)), name=name,
    )(*ins)
    return outs


def _swiglu_fn(a, b):
    a = a.astype(F32)
    return ((a * _sigmoid(a)) * b.astype(F32),)


def _ple_fn(h, z, pp):
    return (h + _sigmoid(z.astype(F32)) * pp.astype(F32),)


def _ple_bwd_fn(dh, z, pp):
    gate = _sigmoid(z.astype(F32))
    return (dh * pp.astype(F32) * gate * (1.0 - gate), dh * gate)


def _merge_fn(o0, o1, o2, l0, l1, l2):
    m = jnp.maximum(jnp.maximum(l0, l1), l2)
    e0, e1, e2 = jnp.exp(l0 - m), jnp.exp(l1 - m), jnp.exp(l2 - m)
    den = e0 + e1 + e2
    return ((e0 * o0 + e1 * o1 + e2 * o2) / den, m + jnp.log(den))


def _add_fn(a, b):
    return (a + b,)


def rmsnorm_fwd(x, g, name, col_block=0, width=None):
    T = x.shape[0]
    W = x.shape[1] if width is None else width
    tt = _row_tile(T, W)

    def body(x_ref, g_ref, y_ref):
        xf = x_ref[...].astype(F32)
        ms = jnp.mean(xf * xf, axis=-1, keepdims=True)
        y_ref[...] = (xf * lax.rsqrt(ms + NORM_EPS) * g_ref[...]).astype(y_ref.dtype)

    return pl.pallas_call(
        body, out_shape=jax.ShapeDtypeStruct((T, W), BF), grid=(T // tt,),
        in_specs=[pl.BlockSpec((tt, W), lambda i: (i, col_block)), pl.BlockSpec((1, W), lambda i: (0, 0))],
        out_specs=pl.BlockSpec((tt, W), lambda i: (i, 0)),
        compiler_params=_params(("parallel",)), name=name,
    )(x, g.reshape(1, W).astype(F32))


def rmsnorm_bwd(x, g, dy, name, out_dtypes, dres=None, col_block=0, width=None):
    T = x.shape[0]
    W = x.shape[1] if width is None else width
    tt = _row_tile(T, W, 1 << 18)
    nout = len(out_dtypes)
    has_res = dres is not None

    def body(*refs):
        x_ref, g_ref, dy_ref = refs[:3]
        pos = 3
        res_ref = None
        if has_res:
            res_ref = refs[3]
            pos = 4
        dx_refs = refs[pos:pos + nout]
        dg_ref = refs[pos + nout]
        xf = x_ref[...].astype(F32)
        rstd = lax.rsqrt(jnp.mean(xf * xf, axis=-1, keepdims=True) + NORM_EPS)
        xhat = xf * rstd
        dyf = dy_ref[...].astype(F32)
        dn = dyf * g_ref[...]
        dx = rstd * (dn - xhat * jnp.mean(dn * xhat, axis=-1, keepdims=True))
        if has_res:
            dx = dx + res_ref[...]
        for o in dx_refs:
            o[...] = dx.astype(o.dtype)

        @pl.when(pl.program_id(0) == 0)
        def _():
            dg_ref[...] = jnp.zeros_like(dg_ref)

        dg_ref[...] += jnp.broadcast_to(jnp.sum(dyf * xhat, axis=0, keepdims=True), dg_ref.shape)

    row = pl.BlockSpec((tt, W), lambda i: (i, 0))
    in_specs = [pl.BlockSpec((tt, W), lambda i: (i, col_block)), pl.BlockSpec((1, W), lambda i: (0, 0)), row]
    operands = [x, g.reshape(1, W).astype(F32), dy]
    if has_res:
        in_specs.append(row)
        operands.append(dres)
    outs = pl.pallas_call(
        body,
        out_shape=[jax.ShapeDtypeStruct((T, W), d) for d in out_dtypes] + [jax.ShapeDtypeStruct((SUBLANES, W), F32)],
        grid=(T // tt,), in_specs=in_specs,
        out_specs=[row] * nout + [pl.BlockSpec((SUBLANES, W), lambda i: (0, 0))],
        compiler_params=_params(("arbitrary",)), name=name,
    )(*operands)
    return tuple(outs[:nout]) + (outs[nout][0],)


def loss_and_grad(y, target, name):
    T, D = y.shape
    tt = _row_tile(T, D)

    def body(y_ref, t_ref, loss_ref, dy_ref):
        d = y_ref[...] - t_ref[...]
        dy_ref[...] = d * (1.0 / D)

        @pl.when(pl.program_id(0) == 0)
        def _():
            loss_ref[...] = jnp.zeros_like(loss_ref)

        loss_ref[...] += jnp.full(loss_ref.shape, 0.5 / D, F32) * jnp.sum(d * d)

    row = pl.BlockSpec((tt, D), lambda i: (i, 0))
    loss, dy = pl.pallas_call(
        body, out_shape=[jax.ShapeDtypeStruct((SUBLANES, LANES), F32), jax.ShapeDtypeStruct((T, D), F32)],
        grid=(T // tt,), in_specs=[row, row],
        out_specs=[pl.BlockSpec((SUBLANES, LANES), lambda i: (0, 0)), row],
        compiler_params=_params(("arbitrary",)), name=name,
    )(y, target)
    return loss[0, 0], dy


def rope_tables(pos, width, r0, rot_dim):
    half = rot_dim // 2
    inv = ROPE_THETA ** (-jnp.arange(half, dtype=F32) * 2.0 / rot_dim)
    ang = pos.astype(F32)[:, None] * inv
    cos, sin = jnp.cos(ang), jnp.sin(ang)
    T = pos.shape[0]
    ones_l, ones_r = jnp.ones((T, r0), F32), jnp.ones((T, width - r0 - rot_dim), F32)
    c_tab = jnp.concatenate([ones_l, cos, cos, ones_r], axis=1)
    s_tab = jnp.concatenate([0 * ones_l, -sin, sin, 0 * ones_r], axis=1)
    perm = np.zeros((width, width), np.float32)
    for j in range(half):
        perm[r0 + j + half, r0 + j] = 1.0
        perm[r0 + j, r0 + j + half] = 1.0
    return c_tab, s_tab, jnp.asarray(perm, BF)


def _lane_permute(v, perm):
    hi = v.astype(BF)
    lo = (v - hi.astype(F32)).astype(BF)
    return (jnp.dot(hi, perm, preferred_element_type=F32) + jnp.dot(lo, perm, preferred_element_type=F32))


def headnorm_fwd(x, g, tabs, name, heads, col0, width, n_true):
    c_tab, s_tab, perm = tabs
    T = x.shape[0]
    tt = _tile(T, (1024, 512, 256, 128))
    inv_n = 1.0 / n_true

    def body(x_ref, g_ref, c_ref, s_ref, p_ref, y_ref):
        xf = x_ref[...].astype(F32)
        rstd = lax.rsqrt(jnp.sum(xf * xf, axis=-1, keepdims=True) * inv_n + NORM_EPS)
        n = xf * rstd * g_ref[...]
        y_ref[...] = (n * c_ref[...] + _lane_permute(n, p_ref[...]) * s_ref[...]).astype(y_ref.dtype)

    tab = pl.BlockSpec((tt, width), lambda i, h: (i, 0))
    return pl.pallas_call(
        body, out_shape=jax.ShapeDtypeStruct((T, heads * width), BF), grid=(T // tt, heads),
        in_specs=[pl.BlockSpec((tt, width), lambda i, h: (i, col0 + h)),
                  pl.BlockSpec((1, width), lambda i, h: (0, 0)), tab, tab,
                  pl.BlockSpec((width, width), lambda i, h: (0, 0))],
        out_specs=pl.BlockSpec((tt, width), lambda i, h: (i, h)),
        compiler_params=_params(("parallel", "parallel")), name=name,
    )(x, g.reshape(1, width).astype(F32), c_tab, s_tab, perm)


def headnorm_bwd(x, g, tabs, dy, name, heads, col0, width, n_true, head_sum=False):
    c_tab, s_tab, perm = tabs
    T = x.shape[0]
    tt = _tile(T, (1024, 512, 256, 128))
    inv_n = 1.0 / n_true

    def body(x_ref, g_ref, c_ref, s_ref, p_ref, dy_ref, dx_ref, dg_ref, *rest):
        i, h = pl.program_id(0), pl.program_id(1)
        xf = x_ref[...].astype(F32)
        rstd = lax.rsqrt(jnp.sum(xf * xf, axis=-1, keepdims=True) * inv_n + NORM_EPS)
        xhat = xf * rstd
        dyf = dy_ref[...].astype(F32)
        dn = dyf * c_ref[...] + _lane_permute(dyf * s_ref[...], p_ref[...])
        dxh = dn * g_ref[...]
        dx = rstd * (dxh - xhat * (jnp.sum(dxh * xhat, axis=-1, keepdims=True) * inv_n))
        dx_ref[...] = dx.astype(dx_ref.dtype)

        @pl.when(jnp.logical_and(i == 0, h == 0))
        def _():
            dg_ref[...] = jnp.zeros_like(dg_ref)

        dg_ref[...] += jnp.broadcast_to(jnp.sum(dn * xhat, axis=0, keepdims=True), dg_ref.shape)
        if head_sum:
            sum_ref = rest[0]

            @pl.when(h == 0)
            def _():
                sum_ref[...] = jnp.zeros_like(sum_ref)

            sum_ref[...] += dx

    tab = pl.BlockSpec((tt, width), lambda i, h: (i, 0))
    out_shape = [jax.ShapeDtypeStruct((T, heads * width), BF), jax.ShapeDtypeStruct((SUBLANES, width), F32)]
    out_specs = [pl.BlockSpec((tt, width), lambda i, h: (i, h)), pl.BlockSpec((SUBLANES, width), lambda i, h: (0, 0))]
    if head_sum:
        out_shape.append(jax.ShapeDtypeStruct((T, width), F32))
        out_specs.append(tab)
    outs = pl.pallas_call(
        body, out_shape=out_shape, grid=(T // tt, heads),
        in_specs=[pl.BlockSpec((tt, width), lambda i, h: (i, col0 + h)),
                  pl.BlockSpec((1, width), lambda i, h: (0, 0)), tab, tab,
                  pl.BlockSpec((width, width), lambda i, h: (0, 0)),
                  pl.BlockSpec((tt, width), lambda i, h: (i, h))],
        out_specs=out_specs, compiler_params=_params(("arbitrary", "arbitrary")), name=name,
    )(x, g.reshape(1, width).astype(F32), c_tab, s_tab, perm, dy)
    return (outs[0], outs[1][0]) + ((outs[2],) if head_sum else ())


class Attn:
    def __init__(self, T, dil, hq, group, qc, q0, kc, k0, vc, v0, vstride, dqk, scale, half_window, blk):
        self.T, self.dil, self.hq, self.group = T, dil, hq, group
        self.hkv = hq // group
        self.qc, self.q0, self.kc, self.k0, self.vc, self.v0, self.vstride = qc, q0, kc, k0, vc, v0, vstride
        self.dqk, self.scale, self.hw = dqk, scale, half_window
        self.len = T // dil
        self.blk = min(blk, self.len)
        self.nb = self.len // self.blk
        self.band = half_window is not None
        self.steps = 3 if self.band else self.nb

    def other(self, i, s):
        if self.band:
            nom = i - 1 + s
            return jnp.minimum(jnp.maximum(nom, 0), self.nb - 1), nom
        return s, s

    def chains(self, a):
        return a.reshape(self.len, self.dil * a.shape[1])

    def unchain(self, a, cols):
        return a.reshape(self.T, cols)

    def mask(self, q_nom, k_nom):
        if not self.band:
            return None
        qpos = q_nom * self.blk + lax.broadcasted_iota(jnp.int32, (self.blk, self.blk), 0)
        kpos = k_nom * self.blk + lax.broadcasted_iota(jnp.int32, (self.blk, self.blk), 1)
        ok = jnp.abs(qpos - kpos) <= self.hw
        for pos in (qpos, kpos):
            ok = jnp.logical_and(ok, jnp.logical_and(pos >= 0, pos < self.len))
        return ok


def _scores(cfg, q, k, q_nom, k_nom):
    s = lax.dot_general(q, k, (((1,), (1,)), ((), ())), preferred_element_type=F32) * cfg.scale
    ok = cfg.mask(q_nom, k_nom)
    return s if ok is None else jnp.where(ok, s, NEG)


def flash_fwd(cfg, q, k, v, name, out_dtype, sink=None):
    blk, dqk = cfg.blk, cfg.dqk
    has_sink = sink is not None

    def body(*refs):
        if has_sink:
            sink_ref, refs = refs[0], refs[1:]
        q_ref, k_ref, v_ref, o_ref, lse_ref, m_sc, l_sc, acc_sc = refs
        i, s = pl.program_id(2), pl.program_id(3)

        @pl.when(s == 0)
        def _():
            if has_sink:
                m_sc[...] = jnp.broadcast_to(sink_ref[0, :1, :], m_sc.shape)
                l_sc[...] = jnp.ones_like(l_sc)
            else:
                m_sc[...] = jnp.full(m_sc.shape, NEG, F32)
                l_sc[...] = jnp.zeros_like(l_sc)
            acc_sc[...] = jnp.zeros_like(acc_sc)

        _, k_nom = cfg.other(i, s)
        sc = _scores(cfg, q_ref[...], k_ref[...], i, k_nom)
        m_prev = m_sc[...]
        m_new = jnp.maximum(m_prev, jnp.max(sc, axis=-1, keepdims=True))
        p = jnp.exp(sc - m_new[:, :1])
        alpha = jnp.exp(m_prev - m_new)
        l_sc[...] = alpha * l_sc[...] + jnp.sum(p, axis=-1, keepdims=True)
        acc_sc[...] = alpha * acc_sc[...] + jnp.dot(p.astype(BF), v_ref[...], preferred_element_type=F32)
        m_sc[...] = m_new

        @pl.when(s == cfg.steps - 1)
        def _():
            o_ref[...] = (acc_sc[...] / l_sc[...]).astype(o_ref.dtype)
            lse_ref[...] = m_sc[...] + jnp.log(l_sc[...])

    g = cfg.group
    q_spec = pl.BlockSpec((blk, dqk), lambda r, h, i, s: (i, r * cfg.qc + cfg.q0 + h))
    k_spec = pl.BlockSpec((blk, dqk), lambda r, h, i, s: (cfg.other(i, s)[0], r * cfg.kc + cfg.k0 + h // g))
    v_spec = pl.BlockSpec((blk, LANES),
                          lambda r, h, i, s: (cfg.other(i, s)[0], r * cfg.vc + cfg.v0 + cfg.vstride * (h // g)))
    o_spec = pl.BlockSpec((blk, LANES), lambda r, h, i, s: (i, r * cfg.hq + h))
    in_specs = [q_spec, k_spec, v_spec]
    operands = [cfg.chains(q), cfg.chains(k), cfg.chains(v)]
    if has_sink:
        in_specs.insert(0, pl.BlockSpec((1, SUBLANES, LANES), lambda r, h, i, s: (h, 0, 0)))
        operands.insert(0, sink)
    cols = cfg.dil * cfg.hq * LANES
    o, lse = pl.pallas_call(
        body, out_shape=[jax.ShapeDtypeStruct((cfg.len, cols), out_dtype), jax.ShapeDtypeStruct((cfg.len, cols), F32)],
        grid=(cfg.dil, cfg.hq, cfg.nb, cfg.steps), in_specs=in_specs, out_specs=[o_spec, o_spec],
        scratch_shapes=[pltpu.VMEM((blk, LANES), F32)] * 3,
        compiler_params=_params(("parallel", "parallel", "parallel", "arbitrary")), name=name,
    )(*operands)
    return cfg.unchain(o, cfg.hq * LANES), cfg.unchain(lse, cfg.hq * LANES)


def flash_dq(cfg, q, k, v, do, o, lse, name, sink=None):
    blk, dqk = cfg.blk, cfg.dqk
    has_sink = sink is not None

    def body(*refs):
        if has_sink:
            sink_ref, refs = refs[0], refs[1:]
        q_ref, k_ref, v_ref, do_ref, o_ref, lse_ref = refs[:6]
        dq_ref = refs[6]
        dq_sc, delta_sc = refs[-2:]
        i, s = pl.program_id(2), pl.program_id(3)

        @pl.when(s == 0)
        def _():
            dq_sc[...] = jnp.zeros_like(dq_sc)
            delta = jnp.sum(do_ref[...].astype(F32) * o_ref[...].astype(F32), axis=-1, keepdims=True)
            delta_sc[...] = jnp.broadcast_to(delta, delta_sc.shape)

        _, k_nom = cfg.other(i, s)
        k = k_ref[...]
        sc = _scores(cfg, q_ref[...], k, i, k_nom)
        p = jnp.exp(sc - lse_ref[:, :1])
        dp = lax.dot_general(do_ref[...], v_ref[...], (((1,), (1,)), ((), ())), preferred_element_type=F32)
        ds = p * (dp - delta_sc[:, :1]) * cfg.scale
        dq_sc[...] += jnp.dot(ds.astype(BF), k, preferred_element_type=F32)

        @pl.when(s == cfg.steps - 1)
        def _():
            dq_ref[...] = dq_sc[...].astype(dq_ref.dtype)
            if has_sink:
                ps = jnp.exp(sink_ref[0, :1, :] - lse_ref[...])
                part = -jnp.sum(ps * delta_sc[...], axis=0, keepdims=True)
                refs[7][...] = jnp.broadcast_to(part, refs[7].shape)

    g = cfg.group
    q_spec = pl.BlockSpec((blk, dqk), lambda r, h, i, s: (i, r * cfg.qc + cfg.q0 + h))
    k_spec = pl.BlockSpec((blk, dqk), lambda r, h, i, s: (cfg.other(i, s)[0], r * cfg.kc + cfg.k0 + h // g))
    v_spec = pl.BlockSpec((blk, LANES),
                          lambda r, h, i, s: (cfg.other(i, s)[0], r * cfg.vc + cfg.v0 + cfg.vstride * (h // g)))
    o_spec = pl.BlockSpec((blk, LANES), lambda r, h, i, s: (i, r * cfg.hq + h))
    dq_spec = pl.BlockSpec((blk, dqk), lambda r, h, i, s: (i, r * cfg.hq + h))
    in_specs = [q_spec, k_spec, v_spec, o_spec, o_spec, o_spec]
    operands = [cfg.chains(q), cfg.chains(k), cfg.chains(v), cfg.chains(do), cfg.chains(o), cfg.chains(lse)]
    out_shape = [jax.ShapeDtypeStruct((cfg.len, cfg.dil * cfg.hq * dqk), BF)]
    out_specs = [dq_spec]
    if has_sink:
        in_specs.insert(0, pl.BlockSpec((1, SUBLANES, LANES), lambda r, h, i, s: (h, 0, 0)))
        operands.insert(0, sink)
        out_shape.append(jax.ShapeDtypeStruct((cfg.hq * cfg.nb * SUBLANES, LANES), F32))
        out_specs.append(pl.BlockSpec((SUBLANES, LANES), lambda r, h, i, s: (h * cfg.nb + i, 0)))
    outs = pl.pallas_call(
        body, out_shape=out_shape, grid=(cfg.dil, cfg.hq, cfg.nb, cfg.steps), in_specs=in_specs,
        out_specs=out_specs, scratch_shapes=[pltpu.VMEM((blk, dqk), F32), pltpu.VMEM((blk, LANES), F32)],
        compiler_params=_params(("parallel", "parallel", "parallel", "arbitrary")), name=name,
    )(*operands)
    dq = cfg.unchain(outs[0], cfg.hq * dqk)
    if has_sink:
        return dq, outs[1].reshape(cfg.hq, cfg.nb, SUBLANES, LANES)[:, :, 0, :]
    return dq


def flash_dkv(cfg, q, k, v, do, o, lse, name, out_dtype, add=None):
    blk, dqk, g, nw = cfg.blk, cfg.dqk, cfg.group, cfg.steps
    has_add = add is not None

    def body(*refs):
        k_ref, v_ref, q_ref, do_ref, o_ref, lse_ref = refs[:6]
        pos = 8 if has_add else 6
        dk_ref, dv_ref = refs[pos:pos + 2]
        dk_sc, dv_sc = refs[-2:]
        i, j = pl.program_id(2), pl.program_id(3)

        @pl.when(j == 0)
        def _():
            dk_sc[...] = jnp.zeros_like(dk_sc)
            dv_sc[...] = jnp.zeros_like(dv_sc)

        _, q_nom = cfg.other(i, j % nw)
        q = q_ref[...]
        do = do_ref[...]
        sc = _scores(cfg, q, k_ref[...], q_nom, i)
        p = jnp.exp(sc - lse_ref[:, :1])
        delta = jnp.sum(do.astype(F32) * o_ref[...].astype(F32), axis=-1, keepdims=True)
        dv_sc[...] += lax.dot_general(p.astype(BF), do, (((0,), (0,)), ((), ())), preferred_element_type=F32)
        dp = lax.dot_general(do, v_ref[...], (((1,), (1,)), ((), ())), preferred_element_type=F32)
        ds = p * (dp - delta) * cfg.scale
        dk_sc[...] += lax.dot_general(ds.astype(BF), q, (((0,), (0,)), ((), ())), preferred_element_type=F32)

        @pl.when(j == g * nw - 1)
        def _():
            dk, dv = dk_sc[...], dv_sc[...]
            if has_add:
                dk, dv = dk + refs[6][...].astype(F32), dv + refs[7][...].astype(F32)
            dk_ref[...] = dk.astype(dk_ref.dtype)
            dv_ref[...] = dv.astype(dv_ref.dtype)

    def qrow(i, j):
        return cfg.other(i, j % nw)[0]

    k_spec = pl.BlockSpec((blk, dqk), lambda r, h, i, j: (i, r * cfg.kc + cfg.k0 + h))
    v_spec = pl.BlockSpec((blk, LANES), lambda r, h, i, j: (i, r * cfg.vc + cfg.v0 + cfg.vstride * h))
    q_spec = pl.BlockSpec((blk, dqk), lambda r, h, i, j: (qrow(i, j), r * cfg.qc + cfg.q0 + h * g + j // nw))
    o_spec = pl.BlockSpec((blk, LANES), lambda r, h, i, j: (qrow(i, j), r * cfg.hq + h * g + j // nw))
    dk_spec = pl.BlockSpec((blk, dqk), lambda r, h, i, j: (i, r * cfg.hkv + h))
    dv_spec = pl.BlockSpec((blk, LANES), lambda r, h, i, j: (i, r * cfg.hkv + h))
    in_specs = [k_spec, v_spec, q_spec, o_spec, o_spec, o_spec]
    operands = [cfg.chains(k), cfg.chains(v), cfg.chains(q), cfg.chains(do), cfg.chains(o), cfg.chains(lse)]
    if has_add:
        in_specs += [dk_spec, dv_spec]
        operands += [cfg.chains(add[0]), cfg.chains(add[1])]
    dk, dv = pl.pallas_call(
        body,
        out_shape=[jax.ShapeDtypeStruct((cfg.len, cfg.dil * cfg.hkv * dqk), out_dtype),
                   jax.ShapeDtypeStruct((cfg.len, cfg.dil * cfg.hkv * LANES), out_dtype)],
        grid=(cfg.dil, cfg.hkv, cfg.nb, g * nw), in_specs=in_specs, out_specs=[dk_spec, dv_spec],
        scratch_shapes=[pltpu.VMEM((blk, dqk), F32), pltpu.VMEM((blk, LANES), F32)],
        compiler_params=_params(("parallel", "parallel", "parallel", "arbitrary")), name=name,
    )(*operands)
    return cfg.unchain(dk, cfg.hkv * dqk), cfg.unchain(dv, cfg.hkv * LANES)


HBM_SPEC = pl.BlockSpec(memory_space=pltpu.HBM)


def _place():
    x, y, c = lax.axis_index("x"), lax.axis_index("y"), lax.axis_index("c")
    chips = [(1 - x, y), (x, 1 - y), (1 - x, 1 - y)]
    return x, y, c, chips


def gather_weights(shards):
    n = len(shards)

    def body(*refs):
        ins, outs = refs[:n], refs[n:2 * n]
        send_sems, recv_sems, local_sems = refs[2 * n:]
        x, y, c, chips = _place()
        me = 2 * x + y
        sibling = (x, y, 1 - c)

        def copy(w, k, src, chip_of_block, half, to):
            return pltpu.make_async_remote_copy(
                src_ref=src, dst_ref=outs[w].at[chip_of_block, half], send_sem=send_sems.at[6 * w + k],
                recv_sem=recv_sems.at[6 * w + k], device_id=to, device_id_type=MESH)

        started = []
        local = []
        for w in range(n):
            own = pltpu.make_async_copy(ins[w], outs[w].at[me], local_sems.at[w])
            own.start()
            local.append(own)
            for j, chip in enumerate(chips):
                cp = copy(w, j, ins[w].at[c], me, c, (*chip, c))
                cp.start()
                started.append(cp)
        for w in range(n):
            for j, (cx, cy) in enumerate(chips):
                them = 2 * cx + cy
                copy(w, j, ins[w].at[c], them, c, (cx, cy, c)).wait_recv()
                fwd = copy(w, 3 + j, outs[w].at[them, c], them, c, sibling)
                fwd.start()
                started.append(fwd)
        for w in range(n):
            for j, (cx, cy) in enumerate(chips):
                copy(w, 3 + j, ins[w].at[c], 2 * cx + cy, 1 - c, sibling).wait_recv()
        for cp in started:
            cp.wait_send()
        for own in local:
            own.wait()

    return pl.pallas_call(
        body, out_shape=[jax.ShapeDtypeStruct((4,) + s.shape, s.dtype) for s in shards],
        in_specs=[HBM_SPEC] * n, out_specs=[HBM_SPEC] * n,
        scratch_shapes=[pltpu.SemaphoreType.DMA((6 * n,)), pltpu.SemaphoreType.DMA((6 * n,)),
                        pltpu.SemaphoreType.DMA((n,))],
        name="gather_weights",
    )(*shards)


def exchange_core_halves(grads):
    n = len(grads)

    def body(*refs):
        ins, mine, theirs = refs[:n], refs[n:2 * n], refs[2 * n:3 * n]
        send_sems, recv_sems, local_sems = refs[3 * n:]
        x, y, c, _ = _place()
        copies = []
        for w in range(n):
            keep = pltpu.make_async_copy(ins[w].at[c], mine[w], local_sems.at[w])
            keep.start()
            give = pltpu.make_async_remote_copy(
                src_ref=ins[w].at[1 - c], dst_ref=theirs[w], send_sem=send_sems.at[w], recv_sem=recv_sems.at[w],
                device_id=(x, y, 1 - c), device_id_type=MESH)
            give.start()
            copies.append((keep, give))
        for keep, give in copies:
            give.wait()
            keep.wait()

    half = [jax.ShapeDtypeStruct(g.shape[1:], g.dtype) for g in grads]
    outs = pl.pallas_call(
        body, out_shape=half + half, in_specs=[HBM_SPEC] * n, out_specs=[HBM_SPEC] * (2 * n),
        scratch_shapes=[pltpu.SemaphoreType.DMA((n,))] * 3, name="exchange_core_halves",
    )(*grads)
    return outs[:n], outs[n:]


def scatter_partials(parts):
    n = len(parts)

    def body(*refs):
        ins, outs = refs[:n], refs[n:2 * n]
        send_sems, recv_sems, local_sems = refs[2 * n:]
        x, y, c, chips = _place()
        me = 2 * x + y
        started = []
        for w in range(n):
            own = pltpu.make_async_copy(ins[w].at[me], outs[w].at[me], local_sems.at[w])
            own.start()
            started.append(own)
        sends = []
        for w in range(n):
            for j, (cx, cy) in enumerate(chips):
                cp = pltpu.make_async_remote_copy(
                    src_ref=ins[w].at[2 * cx + cy], dst_ref=outs[w].at[me], send_sem=send_sems.at[3 * w + j],
                    recv_sem=recv_sems.at[3 * w + j], device_id=(cx, cy, c), device_id_type=MESH)
                cp.start()
                sends.append(cp)
        for w in range(n):
            for j, (cx, cy) in enumerate(chips):
                pltpu.make_async_remote_copy(
                    src_ref=ins[w].at[me], dst_ref=outs[w].at[2 * cx + cy], send_sem=send_sems.at[3 * w + j],
                    recv_sem=recv_sems.at[3 * w + j], device_id=(cx, cy, c), device_id_type=MESH).wait_recv()
        for cp in sends:
            cp.wait_send()
        for own in started:
            own.wait()

    return pl.pallas_call(
        body, out_shape=[jax.ShapeDtypeStruct(p.shape, p.dtype) for p in parts],
        in_specs=[HBM_SPEC] * n, out_specs=[HBM_SPEC] * n,
        scratch_shapes=[pltpu.SemaphoreType.DMA((3 * n,)), pltpu.SemaphoreType.DMA((3 * n,)),
                        pltpu.SemaphoreType.DMA((n,))],
        name="scatter_partials",
    )(*parts)


def join_core_halves(halves):
    n = len(halves)

    def body(*refs):
        ins, outs = refs[:n], refs[n:2 * n]
        send_sems, recv_sems, local_sems = refs[2 * n:]
        x, y, c, _ = _place()
        copies = []
        for w in range(n):
            keep = pltpu.make_async_copy(ins[w], outs[w].at[c], local_sems.at[w])
            keep.start()
            give = pltpu.make_async_remote_copy(
                src_ref=ins[w], dst_ref=outs[w].at[c], send_sem=send_sems.at[w], recv_sem=recv_sems.at[w],
                device_id=(x, y, 1 - c), device_id_type=MESH)
            give.start()
            copies.append((keep, give))
        for w, (keep, give) in enumerate(copies):
            give.wait_send()
            pltpu.make_async_remote_copy(
                src_ref=ins[w], dst_ref=outs[w].at[1 - c], send_sem=send_sems.at[w], recv_sem=recv_sems.at[w],
                device_id=(x, y, 1 - c), device_id_type=MESH).wait_recv()
            keep.wait()

    return pl.pallas_call(
        body, out_shape=[jax.ShapeDtypeStruct((2,) + h.shape, h.dtype) for h in halves],
        in_specs=[HBM_SPEC] * n, out_specs=[HBM_SPEC] * n,
        scratch_shapes=[pltpu.SemaphoreType.DMA((n,))] * 3, name="join_core_halves",
    )(*halves)


def gather_small(vec):
    rows = vec.shape[0]

    def body(v_ref, out_ref, send_sems, recv_sems):
        x, y, c, _ = _place()
        me = 4 * x + 2 * y + c
        out_ref[me] = v_ref[...]
        flips = [(dx, dy, dc) for dx in (0, 1) for dy in (0, 1) for dc in (0, 1)][1:]

        def peer(f):
            return tuple(1 - a if d else a for a, d in zip((x, y, c), f))

        def copy(k, block, to):
            return pltpu.make_async_remote_copy(
                src_ref=v_ref, dst_ref=out_ref.at[block], send_sem=send_sems.at[k], recv_sem=recv_sems.at[k],
                device_id=to, device_id_type=MESH)

        sends = [copy(k, me, peer(f)) for k, f in enumerate(flips)]
        for cp in sends:
            cp.start()
        for k, f in enumerate(flips):
            px, py, pc = peer(f)
            copy(k, 4 * px + 2 * py + pc, peer(f)).wait_recv()
        for cp in sends:
            cp.wait_send()

    vm = pl.BlockSpec(memory_space=pltpu.VMEM)
    return pl.pallas_call(
        body, out_shape=jax.ShapeDtypeStruct((8, rows, SMALL_COLS), F32), in_specs=[vm], out_specs=vm,
        scratch_shapes=[pltpu.SemaphoreType.DMA((7,)), pltpu.SemaphoreType.DMA((7,))], name="gather_small",
    )(vec)


def sum_slots(a, out_dtype, name):
    n, rows, cols = a.shape
    tr = _row_tile(rows, cols)

    def body(*refs):
        tot = refs[0][...].astype(F32)
        for r in refs[1:n]:
            tot = tot + r[...].astype(F32)
        refs[n][...] = tot.astype(out_dtype)

    specs = [pl.BlockSpec((None, tr, cols), functools.partial(lambda s, i: (s, i, 0), s)) for s in range(n)]
    return pl.pallas_call(
        body, out_shape=jax.ShapeDtypeStruct((rows, cols), out_dtype), grid=(rows // tr,), in_specs=specs,
        out_specs=pl.BlockSpec((tr, cols), lambda i: (i, 0)), compiler_params=_params(("parallel",)), name=name,
    )(*([a] * n))


def _adam_fn(w, g, m, v):
    m = ADAM_B1 * m + (1.0 - ADAM_B1) * g
    v = ADAM_B2 * v + (1.0 - ADAM_B2) * (g * g)
    m_hat = m / (1.0 - ADAM_B1 ** ADAM_STEP)
    v_hat = v / (1.0 - ADAM_B2 ** ADAM_STEP)
    delta = -ADAM_LR * (m_hat / (jnp.sqrt(v_hat) + ADAM_EPS) + ADAM_WD * w)
    return delta, m, v


def adamw(w, g, m, v, name):
    return rowwise(_adam_fn, [w, g, m, v], [F32, F32, F32], name)


def _full_weight(name, gathered, local_shape):
    L, a, b = local_shape
    g = gathered.reshape((4, L, a, b))
    if SHARD_AXIS[name] == 1:
        return g.transpose(1, 0, 2, 3).reshape(L, 4 * a, b)
    return g.transpose(1, 2, 0, 3).reshape(L, a, 4 * b)


def _grad_slots(name, dw):
    L, a, b = dw.shape
    if SHARD_AXIS[name] == 1:
        s = dw.reshape(L, 4, a // 4, b).transpose(1, 0, 2, 3)
        rows, cols = L * (a // 4), b
    else:
        s = dw.reshape(L, a, 4, b // 4).transpose(2, 0, 1, 3)
        rows, cols = L * a, b // 4
    return s.reshape(4, 2, rows // 2, cols).transpose(1, 0, 2, 3)


def _attn_a(T):
    return Attn(T, 1, A_HEADS, A_HEADS // A_KV_HEADS, A_HEADS, 0, A_KV_HEADS, 0, A_HEADS + 2 * A_KV_HEADS,
                A_HEADS + A_KV_HEADS, 1, HEAD_DIM, 1.0 / math.sqrt(HEAD_DIM), A_HALF_WINDOW, BAND_BLOCK)


def _attn_b(T):
    return Attn(T, 1, B_HEADS, 1, B_HEADS, 0, B_HEADS, 0, 2 * B_HEADS, 1, 2, B_PAD, 1.0 / math.sqrt(B_QK), None,
                DENSE_BLOCK)


def _attn_c(T, group):
    window, dil = C_PATTERNS[group]
    nblk = (C_GROUPS + 2) * C_HEADS
    return Attn(T, dil, C_HEADS, 1, C_GROUPS * C_HEADS, group * C_HEADS, C_HEADS, 0, nblk,
                (C_GROUPS + 1) * C_HEADS, 1, HEAD_DIM, 1.0 / math.sqrt(HEAD_DIM), window // 2 // dil, BAND_BLOCK)


def _pad_heads(a, axis_len_true, axis_len_pad):
    lead = a.shape[:-1]
    h = a.shape[-1] // axis_len_true
    a = a.reshape(lead + (h, axis_len_true))
    a = jnp.pad(a, [(0, 0)] * len(lead) + [(0, 0), (0, axis_len_pad - axis_len_true)])
    return a.reshape(lead + (h * axis_len_pad,))


def _unpad_heads(a, axis_len_true, axis_len_pad):
    lead = a.shape[:-1]
    h = a.shape[-1] // axis_len_pad
    return a.reshape(lead + (h, axis_len_pad))[..., :axis_len_true].reshape(lead + (h * axis_len_true,))


def _mixer_fwd(kind, slot, hn, W, S, tabs, tag):
    T = hn.shape[0]
    if kind == 0:
        cfg = _attn_a(T)
        qkv = matmul([(hn, W["a_w_in"][slot])], "nn", BF, tag + "_a_in")
        q = headnorm_fwd(qkv, W["a_q_norm"][slot], tabs["hd"], tag + "_a_qn", A_HEADS, 0, HEAD_DIM, HEAD_DIM)
        k = headnorm_fwd(qkv, W["a_k_norm"][slot], tabs["hd"], tag + "_a_kn", A_KV_HEADS, A_HEADS, HEAD_DIM, HEAD_DIM)
        sink = jnp.broadcast_to(W["a_sink"][slot][:, None, None], (A_HEADS, SUBLANES, LANES)).astype(F32)
        o, lse = flash_fwd(cfg, q, k, qkv, tag + "_a_att", BF, sink=sink)
        S.update(qkv=qkv, q=q, k=k, o=o, lse=lse, sink=sink)
        return o
    if kind == 1:
        cfg = _attn_b(T)
        lat = matmul([(hn, W["b_w_in"][slot])], "nn", BF, tag + "_b_in")
        qn = rmsnorm_fwd(lat, W["b_q_lat_norm"][slot], tag + "_b_qlat", 0, B_Q_RANK)
        kvn = rmsnorm_fwd(lat, W["b_kv_lat_norm"][slot], tag + "_b_kvlat", 1, B_KV_RANK)
        qp = matmul([(qn, W["b_w_q_up_pad"][slot])], "nn", BF, tag + "_b_qup")
        kv = matmul([(kvn, W["b_w_kv_up"][slot])], "nn", BF, tag + "_b_kvup")
        k_rope = lat[:, B_Q_RANK + B_KV_RANK:]
        kpre = jnp.concatenate(
            [kv.reshape(T, B_HEADS, 2 * B_NOPE)[:, :, :B_NOPE],
             jnp.broadcast_to(k_rope[:, None, :], (T, B_HEADS, B_ROPE)),
             jnp.zeros((T, B_HEADS, B_PAD - B_QK), BF)], axis=-1).reshape(T, B_HEADS * B_PAD)
        q = headnorm_fwd(qp, W["b_q_norm_pad"][slot], tabs["b"], tag + "_b_qn", B_HEADS, 0, B_PAD, B_QK)
        k = headnorm_fwd(kpre, W["b_k_norm_pad"][slot], tabs["b"], tag + "_b_kn", B_HEADS, 0, B_PAD, B_QK)
        o, lse = flash_fwd(cfg, q, k, kv, tag + "_b_att", BF)
        S.update(lat=lat, qn=qn, kvn=kvn, qp=qp, kv=kv, kpre=kpre, q=q, k=k, o=o, lse=lse)
        return o
    qkv = matmul([(hn, W["c_w_in"][slot])], "nn", BF, tag + "_c_in")
    nq = C_GROUPS * C_HEADS
    q = headnorm_fwd(qkv, W["c_q_norm"][slot], tabs["hd"], tag + "_c_qn", nq, 0, HEAD_DIM, HEAD_DIM)
    k = headnorm_fwd(qkv, W["c_k_norm"][slot], tabs["hd"], tag + "_c_kn", C_HEADS, nq, HEAD_DIM, HEAD_DIM)
    outs, lses = [], []
    for g in range(C_GROUPS):
        og, lg = flash_fwd(_attn_c(T, g), q, k, qkv, f"{tag}_c_att{g}", F32)
        outs.append(og)
        lses.append(lg)
    o, lse = rowwise(_merge_fn, outs + lses, [BF, F32], tag + "_c_merge")
    S.update(qkv=qkv, q=q, k=k, o=o, lse=lse)
    return o


def _mixer_bwd(kind, slot, hn, do, W, S, tabs, tag, G):
    T = hn.shape[0]
    if kind == 0:
        cfg = _attn_a(T)
        qkv = S["qkv"]
        dq, dsink = flash_dq(cfg, S["q"], S["k"], qkv, do, S["o"], S["lse"], tag + "_a_dq", sink=S["sink"])
        dk, dv = flash_dkv(cfg, S["q"], S["k"], qkv, do, S["o"], S["lse"], tag + "_a_dkv", BF)
        dqp, dgq = headnorm_bwd(qkv, W["a_q_norm"][slot], tabs["hd"], dq, tag + "_a_dqn", A_HEADS, 0, HEAD_DIM, HEAD_DIM)
        dkp, dgk = headnorm_bwd(qkv, W["a_k_norm"][slot], tabs["hd"], dk, tag + "_a_dkn", A_KV_HEADS, A_HEADS,
                                HEAD_DIM, HEAD_DIM)
        dqkv = jnp.concatenate([dqp, dkp, dv], axis=1)
        G["a_w_in"][slot] = matmul([(hn, dqkv)], "tn", F32, tag + "_a_dwin")
        G["a_q_norm"][slot], G["a_k_norm"][slot] = dgq, dgk
        G["a_sink"][slot] = jnp.sum(dsink[:, :, 0], axis=1)
        return matmul([(dqkv, W["a_w_in"][slot])], "nt", F32, tag + "_a_dhn")
    if kind == 1:
        cfg = _attn_b(T)
        kv = S["kv"]
        dq = flash_dq(cfg, S["q"], S["k"], kv, do, S["o"], S["lse"], tag + "_b_dq")
        dk, dv = flash_dkv(cfg, S["q"], S["k"], kv, do, S["o"], S["lse"], tag + "_b_dkv", BF)
        dqp, dgq = headnorm_bwd(S["qp"], W["b_q_norm_pad"][slot], tabs["b"], dq, tag + "_b_dqn", B_HEADS, 0, B_PAD, B_QK)
        dkp, dgk, dksum = headnorm_bwd(S["kpre"], W["b_k_norm_pad"][slot], tabs["b"], dk, tag + "_b_dkn", B_HEADS, 0,
                                       B_PAD, B_QK, head_sum=True)
        dkv = jnp.concatenate([dkp.reshape(T, B_HEADS, B_PAD)[:, :, :B_NOPE], dv.reshape(T, B_HEADS, LANES)],
                              axis=-1).reshape(T, B_HEADS * 2 * B_NOPE)
        G["b_w_kv_up"][slot] = matmul([(S["kvn"], dkv)], "tn", F32, tag + "_b_dwkv")
        G["b_w_q_up"][slot] = _unpad_heads(matmul([(S["qn"], dqp)], "tn", F32, tag + "_b_dwq"), B_QK, B_PAD)
        dqn = matmul([(dqp, W["b_w_q_up_pad"][slot])], "nt", F32, tag + "_b_dqnorm")
        dkvn = matmul([(dkv, W["b_w_kv_up"][slot])], "nt", F32, tag + "_b_dkvnorm")
        dql, dg_q = rmsnorm_bwd(S["lat"], W["b_q_lat_norm"][slot], dqn, tag + "_b_dqlat", [BF], None, 0, B_Q_RANK)
        dkvl, dg_kv = rmsnorm_bwd(S["lat"], W["b_kv_lat_norm"][slot], dkvn, tag + "_b_dkvlat", [BF], None, 1, B_KV_RANK)
        dlat = jnp.concatenate([dql, dkvl, dksum[:, B_NOPE:B_QK].astype(BF)], axis=1)
        G["b_w_in"][slot] = matmul([(hn, dlat)], "tn", F32, tag + "_b_dwin")
        G["b_q_norm"][slot], G["b_k_norm"][slot] = dgq[:B_QK], dgk[:B_QK]
        G["b_q_lat_norm"][slot], G["b_kv_lat_norm"][slot] = dg_q, dg_kv
        return matmul([(dlat, W["b_w_in"][slot])], "nt", F32, tag + "_b_dhn")
    qkv = S["qkv"]
    nq = C_GROUPS * C_HEADS
    dqs, acc = [], None
    for g in range(C_GROUPS):
        cfg = _attn_c(T, g)
        dqs.append(flash_dq(cfg, S["q"], S["k"], qkv, do, S["o"], S["lse"], f"{tag}_c_dq{g}"))
        acc = flash_dkv(cfg, S["q"], S["k"], qkv, do, S["o"], S["lse"], f"{tag}_c_dkv{g}", F32, add=acc)
    dk, dv = acc
    dq = jnp.concatenate(dqs, axis=1)
    dqp, dgq = headnorm_bwd(qkv, W["c_q_norm"][slot], tabs["hd"], dq, tag + "_c_dqn", nq, 0, HEAD_DIM, HEAD_DIM)
    dkp, dgk = headnorm_bwd(qkv, W["c_k_norm"][slot], tabs["hd"], dk, tag + "_c_dkn", C_HEADS, nq, HEAD_DIM, HEAD_DIM)
    dqkv = jnp.concatenate([dqp, dkp, dv.astype(BF)], axis=1)
    G["c_w_in"][slot] = matmul([(hn, dqkv)], "tn", F32, tag + "_c_dwin")
    G["c_q_norm"][slot], G["c_k_norm"][slot] = dgq, dgk
    return matmul([(dqkv, W["c_w_in"][slot])], "nt", F32, tag + "_c_dhn")


MIXER_OUT = ("a_w_o", "b_w_o", "c_w_o")


def local_step(x, p, positions, loss_target, W):
    T = x.shape[0]
    tabs = {"hd": rope_tables(positions, HEAD_DIM, 0, PARTIAL_ROT), "b": rope_tables(positions, B_PAD, B_NOPE, B_ROPE)}
    W = dict(W)
    W["b_w_q_up_pad"] = _pad_heads(W["b_w_q_up"], B_QK, B_PAD)
    W["b_q_norm_pad"] = _pad_heads(W["b_q_norm"], B_QK, B_PAD)
    W["b_k_norm_pad"] = _pad_heads(W["b_k_norm"], B_QK, B_PAD)
    saved = []
    h = x
    for i in range(DEPTH):
        kind, slot = i % 3, i // 3
        tag = f"l{i}"
        S = {"h0": h}
        hn = rmsnorm_fwd(h, W["g_mix"][i], tag + "_mixnorm")
        o = _mixer_fwd(kind, slot, hn, W, S, tabs, tag)
        h1 = matmul([(o, W[MIXER_OUT[kind]][slot])], "nn", F32, tag + "_mixout", res=h)
        hn2 = rmsnorm_fwd(h1, W["g_ffn"][i], tag + "_ffnnorm")
        a = matmul([(hn2, W["w_ffn_gate"][i])], "nn", BF, tag + "_gate")
        b = matmul([(hn2, W["w_ffn_up"][i])], "nn", BF, tag + "_up")
        (c,) = rowwise(_swiglu_fn, [a, b], [BF], tag + "_swiglu")
        h2 = matmul([(c, W["w_ffn_down"][i])], "nn", F32, tag + "_down", res=h1)
        hn3 = rmsnorm_fwd(h2, W["g_ple"][i], tag + "_plenorm")
        z = matmul([(hn3, W["w_ple_gate"][i])], "nn", BF, tag + "_plegate")
        p_i = p[i].astype(BF)
        pp = matmul([(p_i, W["w_ple_proj"][i])], "nn", BF, tag + "_pleproj")
        (h3,) = rowwise(_ple_fn, [h2, z, pp], [F32], tag + "_ple")
        S.update(hn=hn, h1=h1, hn2=hn2, a=a, b=b, c=c, h2=h2, hn3=hn3, z=z, pp=pp, p=p_i)
        saved.append(S)
        h = h3

    loss, dh = loss_and_grad(h, loss_target, "loss")
    G = {n: [None] * W[n].shape[0] for n in WEIGHTS}
    for i in reversed(range(DEPTH)):
        kind, slot = i % 3, i // 3
        tag = f"l{i}"
        S = saved[i]
        dz, dpp = rowwise(_ple_bwd_fn, [dh, S["z"], S["pp"]], [BF, BF], tag + "_dple")
        G["w_ple_proj"][i] = matmul([(S["p"], dpp)], "tn", F32, tag + "_dwpleproj")
        G["w_ple_gate"][i] = matmul([(S["hn3"], dz)], "tn", F32, tag + "_dwplegate")
        dhn3 = matmul([(dz, W["w_ple_gate"][i])], "nt", F32, tag + "_dplenorm")
        dh2, dh2b, G["g_ple"][i] = rmsnorm_bwd(S["h2"], W["g_ple"][i], dhn3, tag + "_dple_norm", [F32, BF], dres=dh)
        da, db = matmul([(dh2b, W["w_ffn_down"][i])], "nt", BF, tag + "_dswiglu", swiglu=(S["a"], S["b"]))
        G["w_ffn_down"][i] = matmul([(S["c"], dh2b)], "tn", F32, tag + "_dwdown")
        G["w_ffn_gate"][i] = matmul([(S["hn2"], da)], "tn", F32, tag + "_dwgate")
        G["w_ffn_up"][i] = matmul([(S["hn2"], db)], "tn", F32, tag + "_dwup")
        dhn2 = matmul([(da, W["w_ffn_gate"][i]), (db, W["w_ffn_up"][i])], "nt", F32, tag + "_dffnnorm")
        dh1, dh1b, G["g_ffn"][i] = rmsnorm_bwd(S["h1"], W["g_ffn"][i], dhn2, tag + "_dffn_norm", [F32, BF], dres=dh2)
        wo = W[MIXER_OUT[kind]][slot]
        do = matmul([(dh1b, wo)], "nt", BF, tag + "_dmixout")
        G[MIXER_OUT[kind]][slot] = matmul([(S["o"], dh1b)], "tn", F32, tag + "_dwmixout")
        dhn = _mixer_bwd(kind, slot, S["hn"], do, W, S, tabs, tag, G)
        dh, G["g_mix"][i] = rmsnorm_bwd(S["h0"], W["g_mix"][i], dhn, tag + "_dmix_norm", [F32], dres=dh1)
    return loss, dh, G


def _pack_small(vals):
    flat = jnp.concatenate([vals[n].reshape(-1).astype(F32) for n in SMALL])
    rows = -(-flat.shape[0] // SMALL_COLS)
    rows = -(-rows // SUBLANES) * SUBLANES
    return jnp.pad(flat, (0, rows * SMALL_COLS - flat.shape[0])).reshape(rows, SMALL_COLS)


def _unpack_small(packed, like):
    flat = packed.reshape(-1)
    out, off = {}, 0
    for n in SMALL:
        size = like[n].size
        out[n] = flat[off:off + size].reshape(like[n].shape)
        off += size
    return out


def kernel(x, p, positions, g_mix, g_ffn, g_ple, w_ple_gate, w_ple_proj, w_ffn_gate, w_ffn_up, w_ffn_down, a_w_in, a_q_norm, a_k_norm, a_sink, a_w_o, b_w_in, b_q_lat_norm, b_kv_lat_norm, b_w_q_up, b_w_kv_up, b_q_norm, b_k_norm, b_w_o, c_w_in, c_q_norm, c_k_norm, c_w_o, loss_target, m_g_mix, m_g_ffn, m_g_ple, m_w_ple_gate, m_w_ple_proj, m_w_ffn_gate, m_w_ffn_up, m_w_ffn_down, m_a_w_in, m_a_q_norm, m_a_k_norm, m_a_sink, m_a_w_o, m_b_w_in, m_b_q_lat_norm, m_b_kv_lat_norm, m_b_w_q_up, m_b_w_kv_up, m_b_q_norm, m_b_k_norm, m_b_w_o, m_c_w_in, m_c_q_norm, m_c_k_norm, m_c_w_o, v_g_mix, v_g_ffn, v_g_ple, v_w_ple_gate, v_w_ple_proj, v_w_ffn_gate, v_w_ffn_up, v_w_ffn_down, v_a_w_in, v_a_q_norm, v_a_k_norm, v_a_sink, v_a_w_o, v_b_w_in, v_b_q_lat_norm, v_b_kv_lat_norm, v_b_w_q_up, v_b_w_kv_up, v_b_q_norm, v_b_k_norm, v_b_w_o, v_c_w_in, v_c_q_norm, v_c_k_norm, v_c_w_o):
    args = dict(locals())
    w_loc = {n: args[n] for n in WEIGHTS}
    m_loc = {n: args["m_" + n] for n in WEIGHTS}
    v_loc = {n: args["v_" + n] for n in WEIGHTS}

    def halves(a):
        rows = a.shape[0] * a.shape[1]
        return a.reshape(2, rows // 2, a.shape[2])

    gathered = gather_weights([halves(w_loc[n].astype(BF)) for n in BIG])
    W = {n: _full_weight(n, g, w_loc[n].shape) for n, g in zip(BIG, gathered)}
    for n in SMALL:
        W[n] = w_loc[n]

    loss, dx, G = local_step(x[0], p[:, 0], positions[0], loss_target[0], W)
    loss = lax.psum(loss, ("x", "y", "c"))

    slots = [_grad_slots(n, jnp.stack(G[n])) for n in BIG]
    mine, theirs = exchange_core_halves(slots)
    parts = []
    for n, a, b in zip(BIG, mine, theirs):
        s = a.shape
        (part,) = rowwise(_add_fn, [a.reshape(s[0] * s[1], s[2]), b.reshape(s[0] * s[1], s[2])], [BF], "presum_" + n)
        parts.append(part.reshape(s))
    landed = scatter_partials(parts)
    sums = [sum_slots(a, F32, "sum_" + n) for n, a in zip(BIG, landed)]
    joined = join_core_halves(sums)
    grads = {n: j.reshape(w_loc[n].shape) for n, j in zip(BIG, joined)}

    small = gather_small(_pack_small({n: jnp.stack(G[n]) for n in SMALL}))
    small_sum = sum_slots(small, F32, "sum_small")
    grads.update(_unpack_small(small_sum, w_loc))

    delta, new_m, new_v = {}, {}, {}
    for n in BIG:
        shape = w_loc[n].shape
        two_d = (shape[0] * shape[1], shape[2])
        d, m, v = adamw(w_loc[n].reshape(two_d), grads[n].reshape(two_d), m_loc[n].reshape(two_d),
                        v_loc[n].reshape(two_d), "adamw_" + n)
        delta[n], new_m[n], new_v[n] = d.reshape(shape), m.reshape(shape), v.reshape(shape)
    d, m, v = adamw(_pack_small(w_loc), small_sum, _pack_small(m_loc), _pack_small(v_loc), "adamw_small")
    delta.update(_unpack_small(d, w_loc))
    new_m.update(_unpack_small(m, w_loc))
    new_v.update(_unpack_small(v, w_loc))

    return (loss, dx[None], *[grads[n] for n in WEIGHTS], *[delta[n] for n in WEIGHTS],
            *[new_m[n] for n in WEIGHTS], *[new_v[n] for n in WEIGHTS])
```

```python
import functools
import math

import numpy as np
import jax
import jax.numpy as jnp
from jax import lax
from jax.experimental import pallas as pl
from jax.experimental.pallas import tpu as pltpu

F32 = jnp.float32
BF = jnp.bfloat16

D_MODEL = 2048
DEPTH = 4
HEAD_DIM = 128
ROPE_THETA = 500000.0
PARTIAL_ROT = HEAD_DIM // 4
NORM_EPS = 1e-6
NEG = -1e30
A_HEADS = 16
A_KV_HEADS = 4
A_HALF_WINDOW = 128
B_HEADS = 16
B_Q_RANK = 512
B_KV_RANK = 512
B_NOPE = 128
B_ROPE = 64
B_QK = B_NOPE + B_ROPE
B_PAD = 256
C_PATTERNS = ((128, 1), (512, 4), (2048, 16))
C_HEADS = 16
C_GROUPS = 3
ADAM_LR = 0.001
ADAM_B1 = 0.9
ADAM_B2 = 0.999
ADAM_EPS = 1e-08
ADAM_WD = 0.01
ADAM_STEP = 10

LANES = 128
SUBLANES = 8
VMEM_LIMIT_BYTES = 56 * 1024 * 1024
BAND_BLOCK = 256
BAND_HEADS_PER_STEP = 4
DENSE_BLOCK = 1024
MESH = pl.DeviceIdType.MESH

BIG = ("w_ple_gate", "w_ple_proj", "w_ffn_gate", "w_ffn_up", "w_ffn_down", "a_w_in", "a_w_o",
       "b_w_in", "b_w_q_up", "b_w_kv_up", "b_w_o", "c_w_in", "c_w_o")
SHARD_AXIS = {"w_ple_gate": 1, "w_ple_proj": 2, "w_ffn_gate": 2, "w_ffn_up": 2, "w_ffn_down": 1,
              "a_w_in": 2, "a_w_o": 1, "b_w_in": 1, "b_w_q_up": 2, "b_w_kv_up": 2, "b_w_o": 1,
              "c_w_in": 2, "c_w_o": 1}
SMALL = ("g_mix", "g_ffn", "g_ple", "a_q_norm", "a_k_norm", "a_sink", "b_q_lat_norm",
         "b_kv_lat_norm", "b_q_norm", "b_k_norm", "c_q_norm", "c_k_norm")
WEIGHTS = ("g_mix", "g_ffn", "g_ple", "w_ple_gate", "w_ple_proj", "w_ffn_gate", "w_ffn_up",
           "w_ffn_down", "a_w_in", "a_q_norm", "a_k_norm", "a_sink", "a_w_o", "b_w_in",
           "b_q_lat_norm", "b_kv_lat_norm", "b_w_q_up", "b_w_kv_up", "b_q_norm", "b_k_norm",
           "b_w_o", "c_w_in", "c_q_norm", "c_k_norm", "c_w_o")
SMALL_COLS = 1024


def _params(semantics):
    return pltpu.CompilerParams(dimension_semantics=semantics, vmem_limit_bytes=VMEM_LIMIT_BYTES)


def _tile(dim, cands=(1024, 1408, 512, 256, 128)):
    for c in cands:
        if dim % c == 0:
            return c
    return dim


def _row_tile(rows, cols, target_elems=1 << 19):
    best = None
    for t in range(16, rows + 1, 16):
        if rows % t == 0 and t * cols <= target_elems:
            best = t
    return best if best is not None else rows


def _sigmoid(x):
    return 1.0 / (1.0 + jnp.exp(-x))


def matmul(pairs, mode, out_dtype, name, res=None, swiglu=None):
    a0, b0 = pairs[0]
    if mode == "nn":
        (M, K), N = a0.shape, b0.shape[1]
    elif mode == "nt":
        (M, K), N = a0.shape, b0.shape[0]
    else:
        (K, M), N = a0.shape, b0.shape[1]
    tm, tn, tk = _tile(M), _tile(N), _tile(K, (1024, 512, 256, 128))
    nk = K // tk
    if mode == "nn":
        a_spec = pl.BlockSpec((tm, tk), lambda i, j, k: (i, k))
        b_spec = pl.BlockSpec((tk, tn), lambda i, j, k: (k, j))
        dims = (((1,), (0,)), ((), ()))
    elif mode == "nt":
        a_spec = pl.BlockSpec((tm, tk), lambda i, j, k: (i, k))
        b_spec = pl.BlockSpec((tn, tk), lambda i, j, k: (j, k))
        dims = (((1,), (1,)), ((), ()))
    else:
        a_spec = pl.BlockSpec((tk, tm), lambda i, j, k: (k, i))
        b_spec = pl.BlockSpec((tk, tn), lambda i, j, k: (k, j))
        dims = (((0,), (0,)), ((), ()))
    mn_spec = pl.BlockSpec((tm, tn), lambda i, j, k: (i, j))
    npairs = len(pairs)
    extras = [] if res is None else [res]
    if swiglu is not None:
        extras = list(swiglu)
    nex = len(extras)
    nout = 2 if swiglu is not None else 1

    def body(*refs):
        ins = refs[:2 * npairs]
        ex = refs[2 * npairs:2 * npairs + nex]
        outs = refs[2 * npairs + nex:2 * npairs + nex + nout]
        acc = refs[-1]
        k = pl.program_id(2)

        @pl.when(k == 0)
        def _():
            acc[...] = jnp.zeros_like(acc)

        part = None
        for p in range(npairs):
            d = lax.dot_general(ins[2 * p][...].astype(BF), ins[2 * p + 1][...].astype(BF), dims,
                                preferred_element_type=F32)
            part = d if part is None else part + d
        acc[...] += part

        @pl.when(k == nk - 1)
        def _():
            r = acc[...]
            if swiglu is not None:
                a = ex[0][...].astype(F32)
                b = ex[1][...].astype(F32)
                sg = _sigmoid(a)
                outs[0][...] = (r * b * (sg * (1.0 + a * (1.0 - sg)))).astype(out_dtype)
                outs[1][...] = (r * (a * sg)).astype(out_dtype)
            elif res is not None:
                outs[0][...] = (ex[0][...] + r).astype(out_dtype)
            else:
                outs[0][...] = r.astype(out_dtype)

    in_specs = []
    operands = []
    for a, b in pairs:
        in_specs += [a_spec, b_spec]
        operands += [a, b]
    in_specs += [mn_spec] * nex
    operands += extras
    out_shape = [jax.ShapeDtypeStruct((M, N), out_dtype)] * nout
    outs = pl.pallas_call(
        body, out_shape=out_shape, grid=(M // tm, N // tn, nk), in_specs=in_specs,
        out_specs=[mn_spec] * nout, scratch_shapes=[pltpu.VMEM((tm, tn), F32)],
        compiler_params=_params(("parallel", "parallel", "arbitrary")), name=name,
    )(*operands)
    return outs if nout > 1 else outs[0]


def rowwise(fn, ins, out_dtypes, name):
    rows, cols = ins[0].shape
    tr = _row_tile(rows, cols)
    nin = len(ins)

    def body(*refs):
        vals = fn(*[r[...] for r in refs[:nin]])
        for o, v in zip(refs[nin:], vals):
            o[...] = v.astype(o.dtype)

    spec = pl.BlockSpec((tr, cols), lambda i: (i, 0))
    outs = pl.pallas_call(
        body, out_shape=[jax.ShapeDtypeStruct((rows, cols), d) for d in out_dtypes],
        grid=(rows // tr,), in_specs=[spec] * nin, out_specs=[spec] * len(out_dtypes),
        compiler_params=_params(("parallel",)), name=name,
    )(*ins)
    return outs


def _swiglu_fn(a, b):
    a = a.astype(F32)
    return ((a * _sigmoid(a)) * b.astype(F32),)


def _ple_fn(h, z, pp):
    return (h + _sigmoid(z.astype(F32)) * pp.astype(F32),)


def _ple_bwd_fn(dh, z, pp):
    gate = _sigmoid(z.astype(F32))
    return (dh * pp.astype(F32) * gate * (1.0 - gate), dh * gate)


def _merge_fn(o0, o1, o2, l0, l1, l2):
    m = jnp.maximum(jnp.maximum(l0, l1), l2)
    e0, e1, e2 = jnp.exp(l0 - m), jnp.exp(l1 - m), jnp.exp(l2 - m)
    den = e0 + e1 + e2
    return ((e0 * o0 + e1 * o1 + e2 * o2) / den, m + jnp.log(den))


def rmsnorm_fwd(x, g, name, col_block=0, width=None):
    T = x.shape[0]
    W = x.shape[1] if width is None else width
    tt = _row_tile(T, W)

    def body(x_ref, g_ref, y_ref):
        xf = x_ref[...].astype(F32)
        ms = jnp.mean(xf * xf, axis=-1, keepdims=True)
        y_ref[...] = (xf * lax.rsqrt(ms + NORM_EPS) * g_ref[...]).astype(y_ref.dtype)

    return pl.pallas_call(
        body, out_shape=jax.ShapeDtypeStruct((T, W), BF), grid=(T // tt,),
        in_specs=[pl.BlockSpec((tt, W), lambda i: (i, col_block)), pl.BlockSpec((1, W), lambda i: (0, 0))],
        out_specs=pl.BlockSpec((tt, W), lambda i: (i, 0)),
        compiler_params=_params(("parallel",)), name=name,
    )(x, g.reshape(1, W).astype(F32))


def rmsnorm_bwd(x, g, dy, name, out_dtypes, dres=None, col_block=0, width=None):
    T = x.shape[0]
    W = x.shape[1] if width is None else width
    tt = _row_tile(T, W, 1 << 18)
    nout = len(out_dtypes)
    has_res = dres is not None

    def body(*refs):
        x_ref, g_ref, dy_ref = refs[:3]
        pos = 3
        res_ref = None
        if has_res:
            res_ref = refs[3]
            pos = 4
        dx_refs = refs[pos:pos + nout]
        dg_ref = refs[pos + nout]
        xf = x_ref[...].astype(F32)
        rstd = lax.rsqrt(jnp.mean(xf * xf, axis=-1, keepdims=True) + NORM_EPS)
        xhat = xf * rstd
        dyf = dy_ref[...].astype(F32)
        dn = dyf * g_ref[...]
        dx = rstd * (dn - xhat * jnp.mean(dn * xhat, axis=-1, keepdims=True))
        if has_res:
            dx = dx + res_ref[...]
        for o in dx_refs:
            o[...] = dx.astype(o.dtype)

        @pl.when(pl.program_id(0) == 0)
        def _():
            dg_ref[...] = jnp.zeros_like(dg_ref)

        dg_ref[...] += jnp.broadcast_to(jnp.sum(dyf * xhat, axis=0, keepdims=True), dg_ref.shape)

    row = pl.BlockSpec((tt, W), lambda i: (i, 0))
    in_specs = [pl.BlockSpec((tt, W), lambda i: (i, col_block)), pl.BlockSpec((1, W), lambda i: (0, 0)), row]
    operands = [x, g.reshape(1, W).astype(F32), dy]
    if has_res:
        in_specs.append(row)
        operands.append(dres)
    outs = pl.pallas_call(
        body,
        out_shape=[jax.ShapeDtypeStruct((T, W), d) for d in out_dtypes] + [jax.ShapeDtypeStruct((SUBLANES, W), F32)],
        grid=(T // tt,), in_specs=in_specs,
        out_specs=[row] * nout + [pl.BlockSpec((SUBLANES, W), lambda i: (0, 0))],
        compiler_params=_params(("arbitrary",)), name=name,
    )(*operands)
    return tuple(outs[:nout]) + (outs[nout][0],)


def loss_and_grad(y, target, name):
    T, D = y.shape
    tt = _row_tile(T, D)

    def body(y_ref, t_ref, loss_ref, dy_ref):
        d = y_ref[...] - t_ref[...]
        dy_ref[...] = d * (1.0 / D)

        @pl.when(pl.program_id(0) == 0)
        def _():
            loss_ref[...] = jnp.zeros_like(loss_ref)

        loss_ref[...] += jnp.full(loss_ref.shape, 0.5 / D, F32) * jnp.sum(d * d)

    row = pl.BlockSpec((tt, D), lambda i: (i, 0))
    loss, dy = pl.pallas_call(
        body, out_shape=[jax.ShapeDtypeStruct((SUBLANES, LANES), F32), jax.ShapeDtypeStruct((T, D), F32)],
        grid=(T // tt,), in_specs=[row, row],
        out_specs=[pl.BlockSpec((SUBLANES, LANES), lambda i: (0, 0)), row],
        compiler_params=_params(("arbitrary",)), name=name,
    )(y, target)
    return loss[0, 0], dy


def rope_tables(pos, width, r0, rot_dim):
    half = rot_dim // 2
    inv = ROPE_THETA ** (-jnp.arange(half, dtype=F32) * 2.0 / rot_dim)
    ang = pos.astype(F32)[:, None] * inv
    cos, sin = jnp.cos(ang), jnp.sin(ang)
    T = pos.shape[0]
    ones_l, ones_r = jnp.ones((T, r0), F32), jnp.ones((T, width - r0 - rot_dim), F32)
    c_tab = jnp.concatenate([ones_l, cos, cos, ones_r], axis=1)
    s_tab = jnp.concatenate([0 * ones_l, -sin, sin, 0 * ones_r], axis=1)
    perm = np.zeros((width, width), np.float32)
    for j in range(half):
        perm[r0 + j + half, r0 + j] = 1.0
        perm[r0 + j, r0 + j + half] = 1.0
    return c_tab, s_tab, jnp.asarray(perm, BF)


def _lane_permute(v, perm):
    hi = v.astype(BF)
    lo = (v - hi.astype(F32)).astype(BF)
    return (jnp.dot(hi, perm, preferred_element_type=F32) + jnp.dot(lo, perm, preferred_element_type=F32))


def headnorm_fwd(x, g, tabs, name, heads, col0, width, n_true):
    c_tab, s_tab, perm = tabs
    T = x.shape[0]
    tt = _tile(T, (1024, 512, 256, 128))
    inv_n = 1.0 / n_true

    def body(x_ref, g_ref, c_ref, s_ref, p_ref, y_ref):
        xf = x_ref[...].astype(F32)
        rstd = lax.rsqrt(jnp.sum(xf * xf, axis=-1, keepdims=True) * inv_n + NORM_EPS)
        n = xf * rstd * g_ref[...]
        y_ref[...] = (n * c_ref[...] + _lane_permute(n, p_ref[...]) * s_ref[...]).astype(y_ref.dtype)

    tab = pl.BlockSpec((tt, width), lambda i, h: (i, 0))
    return pl.pallas_call(
        body, out_shape=jax.ShapeDtypeStruct((T, heads * width), BF), grid=(T // tt, heads),
        in_specs=[pl.BlockSpec((tt, width), lambda i, h: (i, col0 + h)),
                  pl.BlockSpec((1, width), lambda i, h: (0, 0)), tab, tab,
                  pl.BlockSpec((width, width), lambda i, h: (0, 0))],
        out_specs=pl.BlockSpec((tt, width), lambda i, h: (i, h)),
        compiler_params=_params(("parallel", "parallel")), name=name,
    )(x, g.reshape(1, width).astype(F32), c_tab, s_tab, perm)


def headnorm_bwd(x, g, tabs, dy, name, heads, col0, width, n_true, head_sum=False):
    c_tab, s_tab, perm = tabs
    T = x.shape[0]
    tt = _tile(T, (1024, 512, 256, 128))
    inv_n = 1.0 / n_true

    def body(x_ref, g_ref, c_ref, s_ref, p_ref, dy_ref, dx_ref, dg_ref, *rest):
        i, h = pl.program_id(0), pl.program_id(1)
        xf = x_ref[...].astype(F32)
        rstd = lax.rsqrt(jnp.sum(xf * xf, axis=-1, keepdims=True) * inv_n + NORM_EPS)
        xhat = xf * rstd
        dyf = dy_ref[...].astype(F32)
        dn = dyf * c_ref[...] + _lane_permute(dyf * s_ref[...], p_ref[...])
        dxh = dn * g_ref[...]
        dx = rstd * (dxh - xhat * (jnp.sum(dxh * xhat, axis=-1, keepdims=True) * inv_n))
        dx_ref[...] = dx.astype(dx_ref.dtype)

        @pl.when(jnp.logical_and(i == 0, h == 0))
        def _():
            dg_ref[...] = jnp.zeros_like(dg_ref)

        dg_ref[...] += jnp.broadcast_to(jnp.sum(dn * xhat, axis=0, keepdims=True), dg_ref.shape)
        if head_sum:
            sum_ref = rest[0]

            @pl.when(h == 0)
            def _():
                sum_ref[...] = jnp.zeros_like(sum_ref)

            sum_ref[...] += dx

    tab = pl.BlockSpec((tt, width), lambda i, h: (i, 0))
    out_shape = [jax.ShapeDtypeStruct((T, heads * width), BF), jax.ShapeDtypeStruct((SUBLANES, width), F32)]
    out_specs = [pl.BlockSpec((tt, width), lambda i, h: (i, h)), pl.BlockSpec((SUBLANES, width), lambda i, h: (0, 0))]
    if head_sum:
        out_shape.append(jax.ShapeDtypeStruct((T, width), F32))
        out_specs.append(tab)
    outs = pl.pallas_call(
        body, out_shape=out_shape, grid=(T // tt, heads),
        in_specs=[pl.BlockSpec((tt, width), lambda i, h: (i, col0 + h)),
                  pl.BlockSpec((1, width), lambda i, h: (0, 0)), tab, tab,
                  pl.BlockSpec((width, width), lambda i, h: (0, 0)),
                  pl.BlockSpec((tt, width), lambda i, h: (i, h))],
        out_specs=out_specs, compiler_params=_params(("arbitrary", "arbitrary")), name=name,
    )(x, g.reshape(1, width).astype(F32), c_tab, s_tab, perm, dy)
    return (outs[0], outs[1][0]) + ((outs[2],) if head_sum else ())


class Attn:
    def __init__(self, T, dil, hq, group, qc, q0, kc, k0, vc, v0, vstride, dqk, scale, half_window, blk):
        self.T, self.dil, self.hq, self.group = T, dil, hq, group
        self.hkv = hq // group
        self.qc, self.q0, self.kc, self.k0, self.vc, self.v0, self.vstride = qc, q0, kc, k0, vc, v0, vstride
        self.dqk, self.scale, self.hw = dqk, scale, half_window
        self.len = T // dil
        self.blk = min(blk, self.len)
        self.nb = self.len // self.blk
        self.band = half_window is not None
        self.steps = 3 if self.band else self.nb

    def other(self, i, s):
        if self.band:
            nom = i - 1 + s
            return jnp.minimum(jnp.maximum(nom, 0), self.nb - 1), nom
        return s, s

    def chains(self, a):
        return a.reshape(self.len, self.dil * a.shape[1])

    def unchain(self, a, cols):
        return a.reshape(self.T, cols)

    def mask(self, q_nom, k_nom):
        if not self.band:
            return None
        qpos = q_nom * self.blk + lax.broadcasted_iota(jnp.int32, (self.blk, self.blk), 0)
        kpos = k_nom * self.blk + lax.broadcasted_iota(jnp.int32, (self.blk, self.blk), 1)
        ok = jnp.abs(qpos - kpos) <= self.hw
        for pos in (qpos, kpos):
            ok = jnp.logical_and(ok, jnp.logical_and(pos >= 0, pos < self.len))
        return ok


def _scores(cfg, q, k, q_nom, k_nom):
    s = lax.dot_general(q, k, (((1,), (1,)), ((), ())), preferred_element_type=F32) * cfg.scale
    ok = cfg.mask(q_nom, k_nom)
    return s if ok is None else jnp.where(ok, s, NEG)


def flash_fwd(cfg, q, k, v, name, out_dtype, sink=None):
    blk, dqk = cfg.blk, cfg.dqk
    has_sink = sink is not None

    def body(*refs):
        if has_sink:
            sink_ref, refs = refs[0], refs[1:]
        q_ref, k_ref, v_ref, o_ref, lse_ref, m_sc, l_sc, acc_sc = refs
        i, s = pl.program_id(2), pl.program_id(3)

        @pl.when(s == 0)
        def _():
            if has_sink:
                m_sc[...] = jnp.broadcast_to(sink_ref[0, :1, :], m_sc.shape)
                l_sc[...] = jnp.ones_like(l_sc)
            else:
                m_sc[...] = jnp.full(m_sc.shape, NEG, F32)
                l_sc[...] = jnp.zeros_like(l_sc)
            acc_sc[...] = jnp.zeros_like(acc_sc)

        _, k_nom = cfg.other(i, s)
        sc = _scores(cfg, q_ref[...], k_ref[...], i, k_nom)
        m_prev = m_sc[...]
        m_new = jnp.maximum(m_prev, jnp.max(sc, axis=-1, keepdims=True))
        p = jnp.exp(sc - m_new[:, :1])
        alpha = jnp.exp(m_prev - m_new)
        l_sc[...] = alpha * l_sc[...] + jnp.sum(p, axis=-1, keepdims=True)
        acc_sc[...] = alpha * acc_sc[...] + jnp.dot(p.astype(BF), v_ref[...], preferred_element_type=F32)
        m_sc[...] = m_new

        @pl.when(s == cfg.steps - 1)
        def _():
            o_ref[...] = (acc_sc[...] / l_sc[...]).astype(o_ref.dtype)
            lse_ref[...] = m_sc[...] + jnp.log(l_sc[...])

    g = cfg.group
    q_spec = pl.BlockSpec((blk, dqk), lambda r, h, i, s: (i, r * cfg.qc + cfg.q0 + h))
    k_spec = pl.BlockSpec((blk, dqk), lambda r, h, i, s: (cfg.other(i, s)[0], r * cfg.kc + cfg.k0 + h // g))
    v_spec = pl.BlockSpec((blk, LANES),
                          lambda r, h, i, s: (cfg.other(i, s)[0], r * cfg.vc + cfg.v0 + cfg.vstride * (h // g)))
    o_spec = pl.BlockSpec((blk, LANES), lambda r, h, i, s: (i, r * cfg.hq + h))
    in_specs = [q_spec, k_spec, v_spec]
    operands = [cfg.chains(q), cfg.chains(k), cfg.chains(v)]
    if has_sink:
        in_specs.insert(0, pl.BlockSpec((1, SUBLANES, LANES), lambda r, h, i, s: (h, 0, 0)))
        operands.insert(0, sink)
    cols = cfg.dil * cfg.hq * LANES
    o, lse = pl.pallas_call(
        body, out_shape=[jax.ShapeDtypeStruct((cfg.len, cols), out_dtype), jax.ShapeDtypeStruct((cfg.len, cols), F32)],
        grid=(cfg.dil, cfg.hq, cfg.nb, cfg.steps), in_specs=in_specs, out_specs=[o_spec, o_spec],
        scratch_shapes=[pltpu.VMEM((blk, LANES), F32)] * 3,
        compiler_params=_params(("parallel", "parallel", "parallel", "arbitrary")), name=name,
    )(*operands)
    return cfg.unchain(o, cfg.hq * LANES), cfg.unchain(lse, cfg.hq * LANES)


def flash_dq(cfg, q, k, v, do, o, lse, name, sink=None):
    blk, dqk = cfg.blk, cfg.dqk
    has_sink = sink is not None

    def body(*refs):
        if has_sink:
            sink_ref, refs = refs[0], refs[1:]
        q_ref, k_ref, v_ref, do_ref, o_ref, lse_ref = refs[:6]
        dq_ref = refs[6]
        dq_sc, delta_sc = refs[-2:]
        i, s = pl.program_id(2), pl.program_id(3)

        @pl.when(s == 0)
        def _():
            dq_sc[...] = jnp.zeros_like(dq_sc)
            delta = jnp.sum(do_ref[...].astype(F32) * o_ref[...].astype(F32), axis=-1, keepdims=True)
            delta_sc[...] = jnp.broadcast_to(delta, delta_sc.shape)

        _, k_nom = cfg.other(i, s)
        k = k_ref[...]
        sc = _scores(cfg, q_ref[...], k, i, k_nom)
        p = jnp.exp(sc - lse_ref[:, :1])
        dp = lax.dot_general(do_ref[...], v_ref[...], (((1,), (1,)), ((), ())), preferred_element_type=F32)
        ds = p * (dp - delta_sc[:, :1]) * cfg.scale
        dq_sc[...] += jnp.dot(ds.astype(BF), k, preferred_element_type=F32)

        @pl.when(s == cfg.steps - 1)
        def _():
            dq_ref[...] = dq_sc[...].astype(dq_ref.dtype)
            if has_sink:
                ps = jnp.exp(sink_ref[0, :1, :] - lse_ref[...])
                part = -jnp.sum(ps * delta_sc[...], axis=0, keepdims=True)
                refs[7][...] = jnp.broadcast_to(part, refs[7].shape)

    g = cfg.group
    q_spec = pl.BlockSpec((blk, dqk), lambda r, h, i, s: (i, r * cfg.qc + cfg.q0 + h))
    k_spec = pl.BlockSpec((blk, dqk), lambda r, h, i, s: (cfg.other(i, s)[0], r * cfg.kc + cfg.k0 + h // g))
    v_spec = pl.BlockSpec((blk, LANES),
                          lambda r, h, i, s: (cfg.other(i, s)[0], r * cfg.vc + cfg.v0 + cfg.vstride * (h // g)))
    o_spec = pl.BlockSpec((blk, LANES), lambda r, h, i, s: (i, r * cfg.hq + h))
    dq_spec = pl.BlockSpec((blk, dqk), lambda r, h, i, s: (i, r * cfg.hq + h))
    in_specs = [q_spec, k_spec, v_spec, o_spec, o_spec, o_spec]
    operands = [cfg.chains(q), cfg.chains(k), cfg.chains(v), cfg.chains(do), cfg.chains(o), cfg.chains(lse)]
    out_shape = [jax.ShapeDtypeStruct((cfg.len, cfg.dil * cfg.hq * dqk), BF)]
    out_specs = [dq_spec]
    if has_sink:
        in_specs.insert(0, pl.BlockSpec((1, SUBLANES, LANES), lambda r, h, i, s: (h, 0, 0)))
        operands.insert(0, sink)
        out_shape.append(jax.ShapeDtypeStruct((cfg.hq * cfg.nb * SUBLANES, LANES), F32))
        out_specs.append(pl.BlockSpec((SUBLANES, LANES), lambda r, h, i, s: (h * cfg.nb + i, 0)))
    outs = pl.pallas_call(
        body, out_shape=out_shape, grid=(cfg.dil, cfg.hq, cfg.nb, cfg.steps), in_specs=in_specs,
        out_specs=out_specs, scratch_shapes=[pltpu.VMEM((blk, dqk), F32), pltpu.VMEM((blk, LANES), F32)],
        compiler_params=_params(("parallel", "parallel", "parallel", "arbitrary")), name=name,
    )(*operands)
    dq = cfg.unchain(outs[0], cfg.hq * dqk)
    if has_sink:
        return dq, outs[1].reshape(cfg.hq, cfg.nb, SUBLANES, LANES)[:, :, 0, :]
    return dq


def flash_dkv(cfg, q, k, v, do, o, lse, name, out_dtype, add=None):
    blk, dqk, g, nw = cfg.blk, cfg.dqk, cfg.group, cfg.steps
    has_add = add is not None

    def body(*refs):
        k_ref, v_ref, q_ref, do_ref, o_ref, lse_ref = refs[:6]
        pos = 8 if has_add else 6
        dk_ref, dv_ref = refs[pos:pos + 2]
        dk_sc, dv_sc = refs[-2:]
        i, j = pl.program_id(2), pl.program_id(3)

        @pl.when(j == 0)
        def _():
            dk_sc[...] = jnp.zeros_like(dk_sc)
            dv_sc[...] = jnp.zeros_like(dv_sc)

        _, q_nom = cfg.other(i, j % nw)
        q = q_ref[...]
        do = do_ref[...]
        sc = _scores(cfg, q, k_ref[...], q_nom, i)
        p = jnp.exp(sc - lse_ref[:, :1])
        delta = jnp.sum(do.astype(F32) * o_ref[...].astype(F32), axis=-1, keepdims=True)
        dv_sc[...] += lax.dot_general(p.astype(BF), do, (((0,), (0,)), ((), ())), preferred_element_type=F32)
        dp = lax.dot_general(do, v_ref[...], (((1,), (1,)), ((), ())), preferred_element_type=F32)
        ds = p * (dp - delta) * cfg.scale
        dk_sc[...] += lax.dot_general(ds.astype(BF), q, (((0,), (0,)), ((), ())), preferred_element_type=F32)

        @pl.when(j == g * nw - 1)
        def _():
            dk, dv = dk_sc[...], dv_sc[...]
            if has_add:
                dk, dv = dk + refs[6][...].astype(F32), dv + refs[7][...].astype(F32)
            dk_ref[...] = dk.astype(dk_ref.dtype)
            dv_ref[...] = dv.astype(dv_ref.dtype)

    def qrow(i, j):
        return cfg.other(i, j % nw)[0]

    k_spec = pl.BlockSpec((blk, dqk), lambda r, h, i, j: (i, r * cfg.kc + cfg.k0 + h))
    v_spec = pl.BlockSpec((blk, LANES), lambda r, h, i, j: (i, r * cfg.vc + cfg.v0 + cfg.vstride * h))
    q_spec = pl.BlockSpec((blk, dqk), lambda r, h, i, j: (qrow(i, j), r * cfg.qc + cfg.q0 + h * g + j // nw))
    o_spec = pl.BlockSpec((blk, LANES), lambda r, h, i, j: (qrow(i, j), r * cfg.hq + h * g + j // nw))
    dk_spec = pl.BlockSpec((blk, dqk), lambda r, h, i, j: (i, r * cfg.hkv + h))
    dv_spec = pl.BlockSpec((blk, LANES), lambda r, h, i, j: (i, r * cfg.hkv + h))
    in_specs = [k_spec, v_spec, q_spec, o_spec, o_spec, o_spec]
    operands = [cfg.chains(k), cfg.chains(v), cfg.chains(q), cfg.chains(do), cfg.chains(o), cfg.chains(lse)]
    if has_add:
        in_specs += [dk_spec, dv_spec]
        operands += [cfg.chains(add[0]), cfg.chains(add[1])]
    dk, dv = pl.pallas_call(
        body,
        out_shape=[jax.ShapeDtypeStruct((cfg.len, cfg.dil * cfg.hkv * dqk), out_dtype),
                   jax.ShapeDtypeStruct((cfg.len, cfg.dil * cfg.hkv * LANES), out_dtype)],
        grid=(cfg.dil, cfg.hkv, cfg.nb, g * nw), in_specs=in_specs, out_specs=[dk_spec, dv_spec],
        scratch_shapes=[pltpu.VMEM((blk, dqk), F32), pltpu.VMEM((blk, LANES), F32)],
        compiler_params=_params(("parallel", "parallel", "parallel", "arbitrary")), name=name,
    )(*operands)
    return cfg.unchain(dk, cfg.hkv * dqk), cfg.unchain(dv, cfg.hkv * LANES)


class Band:
    def __init__(self, T, dil, hq, group, per, qc, q0, kc, k0, vc, v0, scale, hw, blk):
        self.T, self.dil, self.hq, self.group, self.per = T, dil, hq, group, per
        self.pk = per // group
        self.hkv = hq // group
        self.scale, self.hw = scale, hw
        self.len = T // dil
        self.blk = min(blk, self.len)
        self.nb = self.len // self.blk
        self.win = self.blk + 2 * hw
        self.qcol = lambda r: (r * qc + q0) // per
        self.kcol = lambda r: (r * kc + k0) // self.pk
        self.vcol = lambda r: (r * vc + v0) // self.pk
        self.ocol = lambda r: (r * hq) // per
        self.dkcol = lambda r: (r * self.hkv) // self.pk
        assert hw <= self.blk and qc % per == 0 and q0 % per == 0 and kc % self.pk == 0 and k0 % self.pk == 0
        assert vc % self.pk == 0 and v0 % self.pk == 0

    def chains(self, a):
        return a.reshape(self.len, self.dil * a.shape[1])

    def rows3(self, width, col):
        nb = self.nb
        return [pl.BlockSpec((self.blk, width), lambda r, h, i: (jnp.maximum(i - 1, 0), col(r) + h)),
                pl.BlockSpec((self.blk, width), lambda r, h, i: (i, col(r) + h)),
                pl.BlockSpec((self.blk, width), lambda r, h, i: (jnp.minimum(i + 1, nb - 1), col(r) + h))]

    def window(self, prev, cur, nxt, j):
        cols = slice(j * LANES, (j + 1) * LANES)
        return jnp.concatenate([prev[self.blk - self.hw:, cols], cur[:, cols], nxt[:self.hw, cols]], axis=0)

    def valid(self, i, window_is_rows):
        shape = (self.win, self.blk) if window_is_rows else (self.blk, self.win)
        wdim = 0 if window_is_rows else 1
        bpos = i * self.blk + lax.broadcasted_iota(jnp.int32, shape, 1 - wdim)
        wpos = i * self.blk - self.hw + lax.broadcasted_iota(jnp.int32, shape, wdim)
        ok = jnp.abs(bpos - wpos) <= self.hw
        return jnp.logical_and(ok, jnp.logical_and(wpos >= 0, wpos < self.len))


def band_fwd(cfg, q, k, v, name, out_dtype, sink=None):
    blk, per, pk = cfg.blk, cfg.per, cfg.pk
    has_sink = sink is not None

    def body(*refs):
        if has_sink:
            sink_ref, refs = refs[0], refs[1:]
        q_ref, kp, kc, kn, vp, vc, vn, o_ref, lse_ref = refs
        ok = cfg.valid(pl.program_id(2), False)
        for j in range(per):
            jk = j // cfg.group
            if j % cfg.group == 0:
                kw = cfg.window(kp, kc, kn, jk)
                vw = cfg.window(vp, vc, vn, jk)
            cols = slice(j * LANES, (j + 1) * LANES)
            s = lax.dot_general(q_ref[:, cols], kw, (((1,), (1,)), ((), ())), preferred_element_type=F32) * cfg.scale
            s = jnp.where(ok, s, NEG)
            m = jnp.max(s, axis=-1, keepdims=True)
            if has_sink:
                sk = sink_ref[j, :1, :1]
                m = jnp.maximum(m, sk)
            e = jnp.exp(s - m)
            den = jnp.sum(e, axis=-1, keepdims=True)
            if has_sink:
                den = den + jnp.exp(sk - m)
            o = jnp.dot(e.astype(BF), vw, preferred_element_type=F32) / den
            o_ref[:, cols] = o.astype(o_ref.dtype)
            lse_ref[:, cols] = jnp.broadcast_to(m + jnp.log(den), (blk, LANES))

    q_spec = pl.BlockSpec((blk, per * LANES), lambda r, h, i: (i, cfg.qcol(r) + h))
    o_spec = pl.BlockSpec((blk, per * LANES), lambda r, h, i: (i, cfg.ocol(r) + h))
    in_specs = [q_spec] + cfg.rows3(pk * LANES, cfg.kcol) + cfg.rows3(pk * LANES, cfg.vcol)
    kc_, vc_ = cfg.chains(k), cfg.chains(v)
    operands = [cfg.chains(q), kc_, kc_, kc_, vc_, vc_, vc_]
    if has_sink:
        in_specs.insert(0, pl.BlockSpec((per, SUBLANES, LANES), lambda r, h, i: (h, 0, 0)))
        operands.insert(0, sink)
    cols = cfg.dil * cfg.hq * LANES
    o, lse = pl.pallas_call(
        body, out_shape=[jax.ShapeDtypeStruct((cfg.len, cols), out_dtype), jax.ShapeDtypeStruct((cfg.len, cols), F32)],
        grid=(cfg.dil, cfg.hq // per, cfg.nb), in_specs=in_specs, out_specs=[o_spec, o_spec],
        compiler_params=_params(("parallel", "parallel", "parallel")), name=name,
    )(*operands)
    return o.reshape(cfg.T, cfg.hq * LANES), lse.reshape(cfg.T, cfg.hq * LANES)


def band_dq(cfg, q, k, v, do, o, lse, name, sink=None):
    blk, per, pk = cfg.blk, cfg.per, cfg.pk
    has_sink = sink is not None

    def body(*refs):
        if has_sink:
            sink_ref, refs = refs[0], refs[1:]
        q_ref, kp, kc, kn, vp, vc, vn, do_ref, o_ref, lse_ref, dq_ref = refs[:11]
        ok = cfg.valid(pl.program_id(2), False)
        for j in range(per):
            jk = j // cfg.group
            if j % cfg.group == 0:
                kw = cfg.window(kp, kc, kn, jk)
                vw = cfg.window(vp, vc, vn, jk)
            cols = slice(j * LANES, (j + 1) * LANES)
            do = do_ref[:, cols]
            lse = lse_ref[:, j * LANES:j * LANES + 1]
            delta = jnp.sum(do.astype(F32) * o_ref[:, cols].astype(F32), axis=-1, keepdims=True)
            s = lax.dot_general(q_ref[:, cols], kw, (((1,), (1,)), ((), ())), preferred_element_type=F32) * cfg.scale
            p = jnp.exp(jnp.where(ok, s, NEG) - lse)
            dp = lax.dot_general(do, vw, (((1,), (1,)), ((), ())), preferred_element_type=F32)
            ds = p * (dp - delta) * cfg.scale
            dq_ref[:, cols] = jnp.dot(ds.astype(BF), kw, preferred_element_type=F32).astype(dq_ref.dtype)
            if has_sink:
                part = -jnp.sum(jnp.exp(sink_ref[j, :1, :1] - lse) * delta, axis=0, keepdims=True)
                refs[11][j * SUBLANES:(j + 1) * SUBLANES, :] = jnp.broadcast_to(part, (SUBLANES, LANES))

    q_spec = pl.BlockSpec((blk, per * LANES), lambda r, h, i: (i, cfg.qcol(r) + h))
    o_spec = pl.BlockSpec((blk, per * LANES), lambda r, h, i: (i, cfg.ocol(r) + h))
    in_specs = [q_spec] + cfg.rows3(pk * LANES, cfg.kcol) + cfg.rows3(pk * LANES, cfg.vcol) + [o_spec] * 3
    kc_, vc_ = cfg.chains(k), cfg.chains(v)
    operands = [cfg.chains(q), kc_, kc_, kc_, vc_, vc_, vc_, cfg.chains(do), cfg.chains(o), cfg.chains(lse)]
    out_shape = [jax.ShapeDtypeStruct((cfg.len, cfg.dil * cfg.hq * LANES), BF)]
    out_specs = [o_spec]
    if has_sink:
        in_specs.insert(0, pl.BlockSpec((per, SUBLANES, LANES), lambda r, h, i: (h, 0, 0)))
        operands.insert(0, sink)
        out_shape.append(jax.ShapeDtypeStruct((cfg.hq // per, cfg.nb, per * SUBLANES, LANES), F32))
        out_specs.append(pl.BlockSpec((None, None, per * SUBLANES, LANES), lambda r, h, i: (h, i, 0, 0)))
    outs = pl.pallas_call(
        body, out_shape=out_shape, grid=(cfg.dil, cfg.hq // per, cfg.nb), in_specs=in_specs, out_specs=out_specs,
        compiler_params=_params(("parallel", "parallel", "parallel")), name=name,
    )(*operands)
    dq = outs[0].reshape(cfg.T, cfg.hq * LANES)
    return (dq, outs[1]) if has_sink else dq


def band_dkv(cfg, q, k, v, do, o, lse, name, out_dtype, add=None):
    blk, per, pk, group = cfg.blk, cfg.per, cfg.pk, cfg.group
    has_add = add is not None

    def body(*refs):
        k_ref, v_ref = refs[:2]
        qs, dos, os_, lses = refs[2:5], refs[5:8], refs[8:11], refs[11:14]
        pos = 16 if has_add else 14
        dk_ref, dv_ref = refs[pos:pos + 2]
        ok = cfg.valid(pl.program_id(2), True)
        for jk in range(pk):
            kcols = slice(jk * LANES, (jk + 1) * LANES)
            kt, vt = k_ref[:, kcols], v_ref[:, kcols]
            dk = jnp.zeros((blk, LANES), F32)
            dv = jnp.zeros((blk, LANES), F32)
            for g in range(group):
                j = jk * group + g
                qw = cfg.window(*qs, j)
                dow = cfg.window(*dos, j)
                lse = cfg.window(*lses, j)[:, :1]
                delta = jnp.sum(dow.astype(F32) * cfg.window(*os_, j).astype(F32), axis=-1, keepdims=True)
                s = lax.dot_general(qw, kt, (((1,), (1,)), ((), ())), preferred_element_type=F32) * cfg.scale
                p = jnp.exp(jnp.where(ok, s, NEG) - lse)
                dv = dv + lax.dot_general(p.astype(BF), dow, (((0,), (0,)), ((), ())), preferred_element_type=F32)
                dp = lax.dot_general(dow, vt, (((1,), (1,)), ((), ())), preferred_element_type=F32)
                ds = p * (dp - delta) * cfg.scale
                dk = dk + lax.dot_general(ds.astype(BF), qw, (((0,), (0,)), ((), ())), preferred_element_type=F32)
            if has_add:
                dk, dv = dk + refs[14][:, kcols].astype(F32), dv + refs[15][:, kcols].astype(F32)
            dk_ref[:, kcols] = dk.astype(dk_ref.dtype)
            dv_ref[:, kcols] = dv.astype(dv_ref.dtype)

    k_spec = pl.BlockSpec((blk, pk * LANES), lambda r, h, i: (i, cfg.kcol(r) + h))
    v_spec = pl.BlockSpec((blk, pk * LANES), lambda r, h, i: (i, cfg.vcol(r) + h))
    d_spec = pl.BlockSpec((blk, pk * LANES), lambda r, h, i: (i, cfg.dkcol(r) + h))
    in_specs = [k_spec, v_spec] + cfg.rows3(per * LANES, cfg.qcol) + cfg.rows3(per * LANES, cfg.ocol) * 3
    qc_, doc, oc, lc = cfg.chains(q), cfg.chains(do), cfg.chains(o), cfg.chains(lse)
    operands = [cfg.chains(k), cfg.chains(v), qc_, qc_, qc_, doc, doc, doc, oc, oc, oc, lc, lc, lc]
    if has_add:
        in_specs += [d_spec, d_spec]
        operands += [cfg.chains(add[0]), cfg.chains(add[1])]
    cols = cfg.dil * cfg.hkv * LANES
    dk, dv = pl.pallas_call(
        body, out_shape=[jax.ShapeDtypeStruct((cfg.len, cols), out_dtype)] * 2,
        grid=(cfg.dil, cfg.hq // per, cfg.nb), in_specs=in_specs, out_specs=[d_spec, d_spec],
        compiler_params=_params(("parallel", "parallel", "parallel")), name=name,
    )(*operands)
    return dk.reshape(cfg.T, cfg.hkv * LANES), dv.reshape(cfg.T, cfg.hkv * LANES)


HBM_SPEC = pl.BlockSpec(memory_space=pltpu.HBM)


def _place():
    x, y, c = lax.axis_index("x"), lax.axis_index("y"), lax.axis_index("c")
    chips = [(1 - x, y), (x, 1 - y), (1 - x, 1 - y)]
    return x, y, c, chips


def gather_weights(shards):
    n = len(shards)

    def body(*refs):
        ins, outs = refs[:n], refs[n:2 * n]
        send_sems, recv_sems, local_sems = refs[2 * n:]
        x, y, c, chips = _place()
        me = 2 * x + y
        sibling = (x, y, 1 - c)

        def copy(w, k, src, chip_of_block, half, to):
            return pltpu.make_async_remote_copy(
                src_ref=src, dst_ref=outs[w].at[chip_of_block, half], send_sem=send_sems.at[6 * w + k],
                recv_sem=recv_sems.at[6 * w + k], device_id=to, device_id_type=MESH)

        started = []
        local = []
        for w in range(n):
            own = pltpu.make_async_copy(ins[w], outs[w].at[me], local_sems.at[w])
            own.start()
            local.append(own)
            for j, chip in enumerate(chips):
                cp = copy(w, j, ins[w].at[c], me, c, (*chip, c))
                cp.start()
                started.append(cp)
        for w in range(n):
            for j, (cx, cy) in enumerate(chips):
                them = 2 * cx + cy
                copy(w, j, ins[w].at[c], them, c, (cx, cy, c)).wait_recv()
                fwd = copy(w, 3 + j, outs[w].at[them, c], them, c, sibling)
                fwd.start()
                started.append(fwd)
        for w in range(n):
            for j, (cx, cy) in enumerate(chips):
                copy(w, 3 + j, ins[w].at[c], 2 * cx + cy, 1 - c, sibling).wait_recv()
        for cp in started:
            cp.wait_send()
        for own in local:
            own.wait()

    return pl.pallas_call(
        body, out_shape=[jax.ShapeDtypeStruct((4,) + s.shape, s.dtype) for s in shards],
        in_specs=[HBM_SPEC] * n, out_specs=[HBM_SPEC] * n,
        scratch_shapes=[pltpu.SemaphoreType.DMA((6 * n,)), pltpu.SemaphoreType.DMA((6 * n,)),
                        pltpu.SemaphoreType.DMA((n,))],
        name="gather_weights",
    )(*shards)


def _core_index():
    return lax.axis_index("c").astype(jnp.int32).reshape(1)


def presum_core_halves(g2, core, name):
    _, rows, cols = g2.shape
    tr = _row_tile(rows, cols)
    nb = rows // tr
    g2 = g2.reshape(2 * rows, cols)

    def body(core_ref, mine_ref, other_ref, out_ref, land, send_sems, recv_sems):
        x, y, c, _ = _place()
        slot = pl.program_id(0) % 2
        cp = pltpu.make_async_remote_copy(
            src_ref=other_ref, dst_ref=land.at[slot], send_sem=send_sems.at[slot], recv_sem=recv_sems.at[slot],
            device_id=(x, y, 1 - c), device_id_type=MESH)
        cp.start()
        cp.wait_recv()
        out_ref[...] = (mine_ref[...] + land[slot]).astype(out_ref.dtype)
        cp.wait_send()

    grid_spec = pltpu.PrefetchScalarGridSpec(
        num_scalar_prefetch=1, grid=(nb,),
        in_specs=[pl.BlockSpec((tr, cols), lambda i, core: (core[0] * nb + i, 0)),
                  pl.BlockSpec((tr, cols), lambda i, core: ((1 - core[0]) * nb + i, 0))],
        out_specs=pl.BlockSpec((tr, cols), lambda i, core: (i, 0)),
        scratch_shapes=[pltpu.VMEM((2, tr, cols), F32), pltpu.SemaphoreType.DMA((2,)), pltpu.SemaphoreType.DMA((2,))])
    return pl.pallas_call(
        body, out_shape=jax.ShapeDtypeStruct((rows, cols), BF), grid_spec=grid_spec,
        compiler_params=_params(("arbitrary",)), name=name,
    )(core, g2, g2)


def sum_and_swap(landed, name):
    n, rows, cols = landed.shape
    tr = _row_tile(rows, cols)

    def body(*refs):
        slots = refs[:n]
        mine_ref, theirs_ref, out_buf, land, send_sems, recv_sems = refs[n:]
        x, y, c, _ = _place()
        slot = pl.program_id(0) % 2
        tot = slots[0][...].astype(F32)
        for r in slots[1:]:
            tot = tot + r[...].astype(F32)
        mine_ref[...] = tot
        out_buf[slot] = tot
        cp = pltpu.make_async_remote_copy(
            src_ref=out_buf.at[slot], dst_ref=land.at[slot], send_sem=send_sems.at[slot], recv_sem=recv_sems.at[slot],
            device_id=(x, y, 1 - c), device_id_type=MESH)
        cp.start()
        cp.wait_recv()
        theirs_ref[...] = land[slot]
        cp.wait_send()

    specs = [pl.BlockSpec((None, tr, cols), functools.partial(lambda s, i: (s, i, 0), s)) for s in range(n)]
    row = pl.BlockSpec((tr, cols), lambda i: (i, 0))
    return pl.pallas_call(
        body, out_shape=[jax.ShapeDtypeStruct((rows, cols), F32)] * 2, grid=(rows // tr,), in_specs=specs,
        out_specs=[row, row],
        scratch_shapes=[pltpu.VMEM((2, tr, cols), F32), pltpu.VMEM((2, tr, cols), F32),
                        pltpu.SemaphoreType.DMA((2,)), pltpu.SemaphoreType.DMA((2,))],
        compiler_params=_params(("arbitrary",)), name=name,
    )(*([landed] * n))


def scatter_partials(parts):
    n = len(parts)

    def body(*refs):
        ins, outs = refs[:n], refs[n:2 * n]
        send_sems, recv_sems, local_sems = refs[2 * n:]
        x, y, c, chips = _place()
        me = 2 * x + y
        started = []
        for w in range(n):
            own = pltpu.make_async_copy(ins[w].at[me], outs[w].at[me], local_sems.at[w])
            own.start()
            started.append(own)
        sends = []
        for w in range(n):
            for j, (cx, cy) in enumerate(chips):
                cp = pltpu.make_async_remote_copy(
                    src_ref=ins[w].at[2 * cx + cy], dst_ref=outs[w].at[me], send_sem=send_sems.at[3 * w + j],
                    recv_sem=recv_sems.at[3 * w + j], device_id=(cx, cy, c), device_id_type=MESH)
                cp.start()
                sends.append(cp)
        for w in range(n):
            for j, (cx, cy) in enumerate(chips):
                pltpu.make_async_remote_copy(
                    src_ref=ins[w].at[me], dst_ref=outs[w].at[2 * cx + cy], send_sem=send_sems.at[3 * w + j],
                    recv_sem=recv_sems.at[3 * w + j], device_id=(cx, cy, c), device_id_type=MESH).wait_recv()
        for cp in sends:
            cp.wait_send()
        for own in started:
            own.wait()

    return pl.pallas_call(
        body, out_shape=[jax.ShapeDtypeStruct(p.shape, p.dtype) for p in parts],
        in_specs=[HBM_SPEC] * n, out_specs=[HBM_SPEC] * n,
        scratch_shapes=[pltpu.SemaphoreType.DMA((3 * n,)), pltpu.SemaphoreType.DMA((3 * n,)),
                        pltpu.SemaphoreType.DMA((n,))],
        name="scatter_partials",
    )(*parts)


def adamw_halves(w, mine, theirs, m, v, core, name):
    rows, cols = w.shape
    tr = _row_tile(rows // 2, cols, 1 << 18)
    nh = rows // 2 // tr

    def body(core_ref, w_ref, a_ref, b_ref, m_ref, v_ref, g_out, d_out, m_out, v_out):
        g = jnp.where(pl.program_id(0) // nh == core_ref[0], a_ref[...], b_ref[...])
        d_out[...], m_out[...], v_out[...] = _adam_fn(w_ref[...], g, m_ref[...], v_ref[...])
        g_out[...] = g

    full = pl.BlockSpec((tr, cols), lambda i, core: (i, 0))
    half = pl.BlockSpec((tr, cols), lambda i, core: (i % nh, 0))
    grid_spec = pltpu.PrefetchScalarGridSpec(
        num_scalar_prefetch=1, grid=(rows // tr,), in_specs=[full, half, half, full, full], out_specs=[full] * 4)
    return pl.pallas_call(
        body, out_shape=[jax.ShapeDtypeStruct((rows, cols), F32)] * 4, grid_spec=grid_spec,
        compiler_params=_params(("parallel",)), name=name,
    )(core, w, mine, theirs, m, v)


def gather_small(vec):
    rows = vec.shape[0]

    def body(v_ref, out_ref, send_sems, recv_sems):
        x, y, c, _ = _place()
        me = 4 * x + 2 * y + c
        out_ref[me] = v_ref[...]
        flips = [(dx, dy, dc) for dx in (0, 1) for dy in (0, 1) for dc in (0, 1)][1:]

        def peer(f):
            return tuple(1 - a if d else a for a, d in zip((x, y, c), f))

        def copy(k, block, to):
            return pltpu.make_async_remote_copy(
                src_ref=v_ref, dst_ref=out_ref.at[block], send_sem=send_sems.at[k], recv_sem=recv_sems.at[k],
                device_id=to, device_id_type=MESH)

        sends = [copy(k, me, peer(f)) for k, f in enumerate(flips)]
        for cp in sends:
            cp.start()
        for k, f in enumerate(flips):
            px, py, pc = peer(f)
            copy(k, 4 * px + 2 * py + pc, peer(f)).wait_recv()
        for cp in sends:
            cp.wait_send()

    vm = pl.BlockSpec(memory_space=pltpu.VMEM)
    return pl.pallas_call(
        body, out_shape=jax.ShapeDtypeStruct((8, rows, SMALL_COLS), F32), in_specs=[vm], out_specs=vm,
        scratch_shapes=[pltpu.SemaphoreType.DMA((7,)), pltpu.SemaphoreType.DMA((7,))], name="gather_small",
    )(vec)


def sum_slots(a, out_dtype, name):
    n, rows, cols = a.shape
    tr = _row_tile(rows, cols)

    def body(*refs):
        tot = refs[0][...].astype(F32)
        for r in refs[1:n]:
            tot = tot + r[...].astype(F32)
        refs[n][...] = tot.astype(out_dtype)

    specs = [pl.BlockSpec((None, tr, cols), functools.partial(lambda s, i: (s, i, 0), s)) for s in range(n)]
    return pl.pallas_call(
        body, out_shape=jax.ShapeDtypeStruct((rows, cols), out_dtype), grid=(rows // tr,), in_specs=specs,
        out_specs=pl.BlockSpec((tr, cols), lambda i: (i, 0)), compiler_params=_params(("parallel",)), name=name,
    )(*([a] * n))


def _adam_fn(w, g, m, v):
    m = ADAM_B1 * m + (1.0 - ADAM_B1) * g
    v = ADAM_B2 * v + (1.0 - ADAM_B2) * (g * g)
    m_hat = m / (1.0 - ADAM_B1 ** ADAM_STEP)
    v_hat = v / (1.0 - ADAM_B2 ** ADAM_STEP)
    delta = -ADAM_LR * (m_hat / (jnp.sqrt(v_hat) + ADAM_EPS) + ADAM_WD * w)
    return delta, m, v


def adamw(w, g, m, v, name):
    return rowwise(_adam_fn, [w, g, m, v], [F32, F32, F32], name)


def _full_weight(name, gathered, local_shape):
    L, a, b = local_shape
    g = gathered.reshape((4, L, a, b))
    if SHARD_AXIS[name] == 1:
        return g.transpose(1, 0, 2, 3).reshape(L, 4 * a, b)
    return g.transpose(1, 2, 0, 3).reshape(L, a, 4 * b)


def _grad_slots(name, dw):
    L, a, b = dw.shape
    if SHARD_AXIS[name] == 1:
        s = dw.reshape(L, 4, a // 4, b).transpose(1, 0, 2, 3)
        rows, cols = L * (a // 4), b
    else:
        s = dw.reshape(L, a, 4, b // 4).transpose(2, 0, 1, 3)
        rows, cols = L * a, b // 4
    return s.reshape(4, 2, rows // 2, cols).transpose(1, 0, 2, 3)


def _attn_a(T):
    group = A_HEADS // A_KV_HEADS
    return Band(T, 1, A_HEADS, group, group, A_HEADS, 0, A_KV_HEADS, 0, A_HEADS + 2 * A_KV_HEADS,
                A_HEADS + A_KV_HEADS, 1.0 / math.sqrt(HEAD_DIM), A_HALF_WINDOW, BAND_BLOCK)


def _attn_b(T):
    return Attn(T, 1, B_HEADS, 1, B_HEADS, 0, B_HEADS, 0, 2 * B_HEADS, 1, 2, B_PAD, 1.0 / math.sqrt(B_QK), None,
                DENSE_BLOCK)


def _attn_c(T, group):
    window, dil = C_PATTERNS[group]
    nblk = (C_GROUPS + 2) * C_HEADS
    return Band(T, dil, C_HEADS, 1, BAND_HEADS_PER_STEP, C_GROUPS * C_HEADS, group * C_HEADS, C_HEADS, 0, nblk,
                (C_GROUPS + 1) * C_HEADS, 1.0 / math.sqrt(HEAD_DIM), window // 2 // dil, BAND_BLOCK)


def _pad_heads(a, axis_len_true, axis_len_pad):
    lead = a.shape[:-1]
    h = a.shape[-1] // axis_len_true
    a = a.reshape(lead + (h, axis_len_true))
    a = jnp.pad(a, [(0, 0)] * len(lead) + [(0, 0), (0, axis_len_pad - axis_len_true)])
    return a.reshape(lead + (h * axis_len_pad,))


def _unpad_heads(a, axis_len_true, axis_len_pad):
    lead = a.shape[:-1]
    h = a.shape[-1] // axis_len_pad
    return a.reshape(lead + (h, axis_len_pad))[..., :axis_len_true].reshape(lead + (h * axis_len_true,))


def _mixer_fwd(kind, slot, hn, W, S, tabs, tag):
    T = hn.shape[0]
    if kind == 0:
        cfg = _attn_a(T)
        qkv = matmul([(hn, W["a_w_in"][slot])], "nn", BF, tag + "_a_in")
        q = headnorm_fwd(qkv, W["a_q_norm"][slot], tabs["hd"], tag + "_a_qn", A_HEADS, 0, HEAD_DIM, HEAD_DIM)
        k = headnorm_fwd(qkv, W["a_k_norm"][slot], tabs["hd"], tag + "_a_kn", A_KV_HEADS, A_HEADS, HEAD_DIM, HEAD_DIM)
        sink = jnp.broadcast_to(W["a_sink"][slot][:, None, None], (A_HEADS, SUBLANES, LANES)).astype(F32)
        o, lse = band_fwd(cfg, q, k, qkv, tag + "_a_att", BF, sink=sink)
        S.update(qkv=qkv, q=q, k=k, o=o, lse=lse, sink=sink)
        return o
    if kind == 1:
        cfg = _attn_b(T)
        lat = matmul([(hn, W["b_w_in"][slot])], "nn", BF, tag + "_b_in")
        qn = rmsnorm_fwd(lat, W["b_q_lat_norm"][slot], tag + "_b_qlat", 0, B_Q_RANK)
        kvn = rmsnorm_fwd(lat, W["b_kv_lat_norm"][slot], tag + "_b_kvlat", 1, B_KV_RANK)
        qp = matmul([(qn, W["b_w_q_up_pad"][slot])], "nn", BF, tag + "_b_qup")
        kv = matmul([(kvn, W["b_w_kv_up"][slot])], "nn", BF, tag + "_b_kvup")
        k_rope = lat[:, B_Q_RANK + B_KV_RANK:]
        kpre = jnp.concatenate(
            [kv.reshape(T, B_HEADS, 2 * B_NOPE)[:, :, :B_NOPE],
             jnp.broadcast_to(k_rope[:, None, :], (T, B_HEADS, B_ROPE)),
             jnp.zeros((T, B_HEADS, B_PAD - B_QK), BF)], axis=-1).reshape(T, B_HEADS * B_PAD)
        q = headnorm_fwd(qp, W["b_q_norm_pad"][slot], tabs["b"], tag + "_b_qn", B_HEADS, 0, B_PAD, B_QK)
        k = headnorm_fwd(kpre, W["b_k_norm_pad"][slot], tabs["b"], tag + "_b_kn", B_HEADS, 0, B_PAD, B_QK)
        o, lse = flash_fwd(cfg, q, k, kv, tag + "_b_att", BF)
        S.update(lat=lat, qn=qn, kvn=kvn, qp=qp, kv=kv, kpre=kpre, q=q, k=k, o=o, lse=lse)
        return o
    qkv = matmul([(hn, W["c_w_in"][slot])], "nn", BF, tag + "_c_in")
    nq = C_GROUPS * C_HEADS
    q = headnorm_fwd(qkv, W["c_q_norm"][slot], tabs["hd"], tag + "_c_qn", nq, 0, HEAD_DIM, HEAD_DIM)
    k = headnorm_fwd(qkv, W["c_k_norm"][slot], tabs["hd"], tag + "_c_kn", C_HEADS, nq, HEAD_DIM, HEAD_DIM)
    outs, lses = [], []
    for g in range(C_GROUPS):
        og, lg = band_fwd(_attn_c(T, g), q, k, qkv, f"{tag}_c_att{g}", F32)
        outs.append(og)
        lses.append(lg)
    o, lse = rowwise(_merge_fn, outs + lses, [BF, F32], tag + "_c_merge")
    S.update(qkv=qkv, q=q, k=k, o=o, lse=lse)
    return o


def _mixer_bwd(kind, slot, hn, do, W, S, tabs, tag, G):
    T = hn.shape[0]
    if kind == 0:
        cfg = _attn_a(T)
        qkv = S["qkv"]
        dq, dsink = band_dq(cfg, S["q"], S["k"], qkv, do, S["o"], S["lse"], tag + "_a_dq", sink=S["sink"])
        dk, dv = band_dkv(cfg, S["q"], S["k"], qkv, do, S["o"], S["lse"], tag + "_a_dkv", BF)
        dqp, dgq = headnorm_bwd(qkv, W["a_q_norm"][slot], tabs["hd"], dq, tag + "_a_dqn", A_HEADS, 0, HEAD_DIM, HEAD_DIM)
        dkp, dgk = headnorm_bwd(qkv, W["a_k_norm"][slot], tabs["hd"], dk, tag + "_a_dkn", A_KV_HEADS, A_HEADS,
                                HEAD_DIM, HEAD_DIM)
        dqkv = jnp.concatenate([dqp, dkp, dv], axis=1)
        G["a_w_in"][slot] = matmul([(hn, dqkv)], "tn", F32, tag + "_a_dwin")
        G["a_q_norm"][slot], G["a_k_norm"][slot] = dgq, dgk
        parts = dsink.reshape(A_HEADS // cfg.per, cfg.nb, cfg.per, SUBLANES, LANES)[:, :, :, 0, 0]
        G["a_sink"][slot] = jnp.sum(parts, axis=1).reshape(A_HEADS)
        return matmul([(dqkv, W["a_w_in"][slot])], "nt", F32, tag + "_a_dhn")
    if kind == 1:
        cfg = _attn_b(T)
        kv = S["kv"]
        dq = flash_dq(cfg, S["q"], S["k"], kv, do, S["o"], S["lse"], tag + "_b_dq")
        dk, dv = flash_dkv(cfg, S["q"], S["k"], kv, do, S["o"], S["lse"], tag + "_b_dkv", BF)
        dqp, dgq = headnorm_bwd(S["qp"], W["b_q_norm_pad"][slot], tabs["b"], dq, tag + "_b_dqn", B_HEADS, 0, B_PAD, B_QK)
        dkp, dgk, dksum = headnorm_bwd(S["kpre"], W["b_k_norm_pad"][slot], tabs["b"], dk, tag + "_b_dkn", B_HEADS, 0,
                                       B_PAD, B_QK, head_sum=True)
        dkv = jnp.concatenate([dkp.reshape(T, B_HEADS, B_PAD)[:, :, :B_NOPE], dv.reshape(T, B_HEADS, LANES)],
                              axis=-1).reshape(T, B_HEADS * 2 * B_NOPE)
        G["b_w_kv_up"][slot] = matmul([(S["kvn"], dkv)], "tn", F32, tag + "_b_dwkv")
        G["b_w_q_up"][slot] = _unpad_heads(matmul([(S["qn"], dqp)], "tn", F32, tag + "_b_dwq"), B_QK, B_PAD)
        dqn = matmul([(dqp, W["b_w_q_up_pad"][slot])], "nt", F32, tag + "_b_dqnorm")
        dkvn = matmul([(dkv, W["b_w_kv_up"][slot])], "nt", F32, tag + "_b_dkvnorm")
        dql, dg_q = rmsnorm_bwd(S["lat"], W["b_q_lat_norm"][slot], dqn, tag + "_b_dqlat", [BF], None, 0, B_Q_RANK)
        dkvl, dg_kv = rmsnorm_bwd(S["lat"], W["b_kv_lat_norm"][slot], dkvn, tag + "_b_dkvlat", [BF], None, 1, B_KV_RANK)
        dlat = jnp.concatenate([dql, dkvl, dksum[:, B_NOPE:B_QK].astype(BF)], axis=1)
        G["b_w_in"][slot] = matmul([(hn, dlat)], "tn", F32, tag + "_b_dwin")
        G["b_q_norm"][slot], G["b_k_norm"][slot] = dgq[:B_QK], dgk[:B_QK]
        G["b_q_lat_norm"][slot], G["b_kv_lat_norm"][slot] = dg_q, dg_kv
        return matmul([(dlat, W["b_w_in"][slot])], "nt", F32, tag + "_b_dhn")
    qkv = S["qkv"]
    nq = C_GROUPS * C_HEADS
    dqs, acc = [], None
    for g in range(C_GROUPS):
        cfg = _attn_c(T, g)
        dqs.append(band_dq(cfg, S["q"], S["k"], qkv, do, S["o"], S["lse"], f"{tag}_c_dq{g}"))
        acc = band_dkv(cfg, S["q"], S["k"], qkv, do, S["o"], S["lse"], f"{tag}_c_dkv{g}", F32, add=acc)
    dk, dv = acc
    dq = jnp.concatenate(dqs, axis=1)
    dqp, dgq = headnorm_bwd(qkv, W["c_q_norm"][slot], tabs["hd"], dq, tag + "_c_dqn", nq, 0, HEAD_DIM, HEAD_DIM)
    dkp, dgk = headnorm_bwd(qkv, W["c_k_norm"][slot], tabs["hd"], dk, tag + "_c_dkn", C_HEADS, nq, HEAD_DIM, HEAD_DIM)
    dqkv = jnp.concatenate([dqp, dkp, dv.astype(BF)], axis=1)
    G["c_w_in"][slot] = matmul([(hn, dqkv)], "tn", F32, tag + "_c_dwin")
    G["c_q_norm"][slot], G["c_k_norm"][slot] = dgq, dgk
    return matmul([(dqkv, W["c_w_in"][slot])], "nt", F32, tag + "_c_dhn")


MIXER_OUT = ("a_w_o", "b_w_o", "c_w_o")


def local_step(x, p, positions, loss_target, W):
    T = x.shape[0]
    tabs = {"hd": rope_tables(positions, HEAD_DIM, 0, PARTIAL_ROT), "b": rope_tables(positions, B_PAD, B_NOPE, B_ROPE)}
    W = dict(W)
    W["b_w_q_up_pad"] = _pad_heads(W["b_w_q_up"], B_QK, B_PAD)
    W["b_q_norm_pad"] = _pad_heads(W["b_q_norm"], B_QK, B_PAD)
    W["b_k_norm_pad"] = _pad_heads(W["b_k_norm"], B_QK, B_PAD)
    saved = []
    h = x
    for i in range(DEPTH):
        kind, slot = i % 3, i // 3
        tag = f"l{i}"
        S = {"h0": h}
        hn = rmsnorm_fwd(h, W["g_mix"][i], tag + "_mixnorm")
        o = _mixer_fwd(kind, slot, hn, W, S, tabs, tag)
        h1 = matmul([(o, W[MIXER_OUT[kind]][slot])], "nn", F32, tag + "_mixout", res=h)
        hn2 = rmsnorm_fwd(h1, W["g_ffn"][i], tag + "_ffnnorm")
        a = matmul([(hn2, W["w_ffn_gate"][i])], "nn", BF, tag + "_gate")
        b = matmul([(hn2, W["w_ffn_up"][i])], "nn", BF, tag + "_up")
        (c,) = rowwise(_swiglu_fn, [a, b], [BF], tag + "_swiglu")
        h2 = matmul([(c, W["w_ffn_down"][i])], "nn", F32, tag + "_down", res=h1)
        hn3 = rmsnorm_fwd(h2, W["g_ple"][i], tag + "_plenorm")
        z = matmul([(hn3, W["w_ple_gate"][i])], "nn", BF, tag + "_plegate")
        p_i = p[i].astype(BF)
        pp = matmul([(p_i, W["w_ple_proj"][i])], "nn", BF, tag + "_pleproj")
        (h3,) = rowwise(_ple_fn, [h2, z, pp], [F32], tag + "_ple")
        S.update(hn=hn, h1=h1, hn2=hn2, a=a, b=b, c=c, h2=h2, hn3=hn3, z=z, pp=pp, p=p_i)
        saved.append(S)
        h = h3

    loss, dh = loss_and_grad(h, loss_target, "loss")
    G = {n: [None] * W[n].shape[0] for n in WEIGHTS}
    for i in reversed(range(DEPTH)):
        kind, slot = i % 3, i // 3
        tag = f"l{i}"
        S = saved[i]
        dz, dpp = rowwise(_ple_bwd_fn, [dh, S["z"], S["pp"]], [BF, BF], tag + "_dple")
        G["w_ple_proj"][i] = matmul([(S["p"], dpp)], "tn", F32, tag + "_dwpleproj")
        G["w_ple_gate"][i] = matmul([(S["hn3"], dz)], "tn", F32, tag + "_dwplegate")
        dhn3 = matmul([(dz, W["w_ple_gate"][i])], "nt", F32, tag + "_dplenorm")
        dh2, dh2b, G["g_ple"][i] = rmsnorm_bwd(S["h2"], W["g_ple"][i], dhn3, tag + "_dple_norm", [F32, BF], dres=dh)
        da, db = matmul([(dh2b, W["w_ffn_down"][i])], "nt", BF, tag + "_dswiglu", swiglu=(S["a"], S["b"]))
        G["w_ffn_down"][i] = matmul([(S["c"], dh2b)], "tn", F32, tag + "_dwdown")
        G["w_ffn_gate"][i] = matmul([(S["hn2"], da)], "tn", F32, tag + "_dwgate")
        G["w_ffn_up"][i] = matmul([(S["hn2"], db)], "tn", F32, tag + "_dwup")
        dhn2 = matmul([(da, W["w_ffn_gate"][i]), (db, W["w_ffn_up"][i])], "nt", F32, tag + "_dffnnorm")
        dh1, dh1b, G["g_ffn"][i] = rmsnorm_bwd(S["h1"], W["g_ffn"][i], dhn2, tag + "_dffn_norm", [F32, BF], dres=dh2)
        wo = W[MIXER_OUT[kind]][slot]
        do = matmul([(dh1b, wo)], "nt", BF, tag + "_dmixout")
        G[MIXER_OUT[kind]][slot] = matmul([(S["o"], dh1b)], "tn", F32, tag + "_dwmixout")
        dhn = _mixer_bwd(kind, slot, S["hn"], do, W, S, tabs, tag, G)
        dh, G["g_mix"][i] = rmsnorm_bwd(S["h0"], W["g_mix"][i], dhn, tag + "_dmix_norm", [F32], dres=dh1)
    return loss, dh, G


def _pack_small(vals):
    flat = jnp.concatenate([vals[n].reshape(-1).astype(F32) for n in SMALL])
    rows = -(-flat.shape[0] // SMALL_COLS)
    rows = -(-rows // SUBLANES) * SUBLANES
    return jnp.pad(flat, (0, rows * SMALL_COLS - flat.shape[0])).reshape(rows, SMALL_COLS)


def _unpack_small(packed, like):
    flat = packed.reshape(-1)
    out, off = {}, 0
    for n in SMALL:
        size = like[n].size
        out[n] = flat[off:off + size].reshape(like[n].shape)
        off += size
    return out


def kernel(x, p, positions, g_mix, g_ffn, g_ple, w_ple_gate, w_ple_proj, w_ffn_gate, w_ffn_up, w_ffn_down, a_w_in, a_q_norm, a_k_norm, a_sink, a_w_o, b_w_in, b_q_lat_norm, b_kv_lat_norm, b_w_q_up, b_w_kv_up, b_q_norm, b_k_norm, b_w_o, c_w_in, c_q_norm, c_k_norm, c_w_o, loss_target, m_g_mix, m_g_ffn, m_g_ple, m_w_ple_gate, m_w_ple_proj, m_w_ffn_gate, m_w_ffn_up, m_w_ffn_down, m_a_w_in, m_a_q_norm, m_a_k_norm, m_a_sink, m_a_w_o, m_b_w_in, m_b_q_lat_norm, m_b_kv_lat_norm, m_b_w_q_up, m_b_w_kv_up, m_b_q_norm, m_b_k_norm, m_b_w_o, m_c_w_in, m_c_q_norm, m_c_k_norm, m_c_w_o, v_g_mix, v_g_ffn, v_g_ple, v_w_ple_gate, v_w_ple_proj, v_w_ffn_gate, v_w_ffn_up, v_w_ffn_down, v_a_w_in, v_a_q_norm, v_a_k_norm, v_a_sink, v_a_w_o, v_b_w_in, v_b_q_lat_norm, v_b_kv_lat_norm, v_b_w_q_up, v_b_w_kv_up, v_b_q_norm, v_b_k_norm, v_b_w_o, v_c_w_in, v_c_q_norm, v_c_k_norm, v_c_w_o):
    args = dict(locals())
    w_loc = {n: args[n] for n in WEIGHTS}
    m_loc = {n: args["m_" + n] for n in WEIGHTS}
    v_loc = {n: args["v_" + n] for n in WEIGHTS}

    def halves(a):
        rows = a.shape[0] * a.shape[1]
        return a.reshape(2, rows // 2, a.shape[2])

    gathered = gather_weights([halves(w_loc[n].astype(BF)) for n in BIG])
    W = {n: _full_weight(n, g, w_loc[n].shape) for n, g in zip(BIG, gathered)}
    for n in SMALL:
        W[n] = w_loc[n]

    loss, dx, G = local_step(x[0], p[:, 0], positions[0], loss_target[0], W)
    loss = lax.psum(loss, ("x", "y", "c"))

    core = _core_index()
    parts = []
    for n in BIG:
        s = _grad_slots(n, jnp.stack(G[n]))
        part = presum_core_halves(s.reshape(2, 4 * s.shape[2], s.shape[3]), core, "presum_" + n)
        parts.append(part.reshape(s.shape[1:]))
    landed = scatter_partials(parts)
    halves = [sum_and_swap(a, "sum_" + n) for n, a in zip(BIG, landed)]

    small = gather_small(_pack_small({n: jnp.stack(G[n]) for n in SMALL}))
    small_sum = sum_slots(small, F32, "sum_small")
    grads = _unpack_small(small_sum, w_loc)

    delta, new_m, new_v = {}, {}, {}
    for n, (mine, theirs) in zip(BIG, halves):
        shape = w_loc[n].shape
        two_d = (shape[0] * shape[1], shape[2])
        g, d, m, v = adamw_halves(w_loc[n].reshape(two_d), mine, theirs, m_loc[n].reshape(two_d),
                                  v_loc[n].reshape(two_d), core, "adamw_" + n)
        grads[n], delta[n], new_m[n], new_v[n] = g.reshape(shape), d.reshape(shape), m.reshape(shape), v.reshape(shape)
    d, m, v = adamw(_pack_small(w_loc), small_sum, _pack_small(m_loc), _pack_small(v_loc), "adamw_small")
    delta.update(_unpack_small(d, w_loc))
    new_m.update(_unpack_small(m, w_loc))
    new_v.update(_unpack_small(v, w_loc))

    return (loss, dx[None], *[grads[n] for n in WEIGHTS], *[delta[n] for n in WEIGHTS],
            *[new_m[n] for n in WEIGHTS], *[new_v[n] for n in WEIGHTS])
```

```python
import functools
import math

import numpy as np
import jax
import jax.numpy as jnp
from jax import lax
from jax.experimental import pallas as pl
from jax.experimental.pallas import tpu as pltpu

F32 = jnp.float32
BF = jnp.bfloat16

D_MODEL = 2048
DEPTH = 4
HEAD_DIM = 128
ROPE_THETA = 500000.0
PARTIAL_ROT = HEAD_DIM // 4
NORM_EPS = 1e-6
NEG = -1e30
A_HEADS = 16
A_KV_HEADS = 4
A_HALF_WINDOW = 128
B_HEADS = 16
B_Q_RANK = 512
B_KV_RANK = 512
B_NOPE = 128
B_ROPE = 64
B_QK = B_NOPE + B_ROPE
B_PAD = 256
C_PATTERNS = ((128, 1), (512, 4), (2048, 16))
C_HEADS = 16
C_GROUPS = 3
ADAM_LR = 0.001
ADAM_B1 = 0.9
ADAM_B2 = 0.999
ADAM_EPS = 1e-08
ADAM_WD = 0.01
ADAM_STEP = 10

LANES = 128
SUBLANES = 8
VMEM_LIMIT_BYTES = 56 * 1024 * 1024
BAND_BLOCK = 256
BAND_HEADS_PER_STEP = 4
DENSE_BLOCK = 1024
MESH = pl.DeviceIdType.MESH

BIG = ("w_ple_gate", "w_ple_proj", "w_ffn_gate", "w_ffn_up", "w_ffn_down", "a_w_in", "a_w_o",
       "b_w_in", "b_w_q_up", "b_w_kv_up", "b_w_o", "c_w_in", "c_w_o")
SHARD_AXIS = {"w_ple_gate": 1, "w_ple_proj": 2, "w_ffn_gate": 2, "w_ffn_up": 2, "w_ffn_down": 1,
              "a_w_in": 2, "a_w_o": 1, "b_w_in": 1, "b_w_q_up": 2, "b_w_kv_up": 2, "b_w_o": 1,
              "c_w_in": 2, "c_w_o": 1}
SMALL = ("g_mix", "g_ffn", "g_ple", "a_q_norm", "a_k_norm", "a_sink", "b_q_lat_norm",
         "b_kv_lat_norm", "b_q_norm", "b_k_norm", "c_q_norm", "c_k_norm")
WEIGHTS = ("g_mix", "g_ffn", "g_ple", "w_ple_gate", "w_ple_proj", "w_ffn_gate", "w_ffn_up",
           "w_ffn_down", "a_w_in", "a_q_norm", "a_k_norm", "a_sink", "a_w_o", "b_w_in",
           "b_q_lat_norm", "b_kv_lat_norm", "b_w_q_up", "b_w_kv_up", "b_q_norm", "b_k_norm",
           "b_w_o", "c_w_in", "c_q_norm", "c_k_norm", "c_w_o")
SMALL_COLS = 1024


def _params(semantics):
    return pltpu.CompilerParams(dimension_semantics=semantics, vmem_limit_bytes=VMEM_LIMIT_BYTES)


def _tile(dim, cands=(1024, 1408, 512, 256, 128)):
    for c in cands:
        if dim % c == 0:
            return c
    return dim


def _row_tile(rows, cols, target_elems=1 << 19):
    best = None
    for t in range(16, rows + 1, 16):
        if rows % t == 0 and t * cols <= target_elems:
            best = t
    return best if best is not None else rows


def _sigmoid(x):
    return 1.0 / (1.0 + jnp.exp(-x))


class Slot:
    def __init__(self, name, layers, layer, rows, cols, col0=0, buf=None):
        self.axis, self.layers, self.layer, self.rows, self.cols, self.col0, self.buf = (
            SHARD_AXIS[name], layers, layer, rows, cols, col0, buf)
        self.srows = rows // 4 if self.axis == 1 else rows
        self.scols = cols if self.axis == 1 else cols // 4
        self.half = layers * self.srows // 2

    def tiles(self, ncols):
        tm = _tile(math.gcd(self.srows, self.half))
        tn = _tile(math.gcd(self.scols, math.gcd(self.col0, ncols)))
        return tm, tn

    def spec(self, tm, tn):
        def index(i, j, k):
            row, col = i * tm, self.col0 + j * tn
            chip = row // self.srows if self.axis == 1 else col // self.scols
            flat = self.layer * self.srows + (row % self.srows if self.axis == 1 else row)
            cb = col // tn if self.axis == 1 else (col % self.scols) // tn
            return flat // self.half, chip, (flat % self.half) // tm, cb

        return pl.BlockSpec((None, None, tm, tn), index)

    def shape(self):
        return jax.ShapeDtypeStruct((2, 4, self.half, self.scols), F32)


def matmul(pairs, mode, out_dtype, name, res=None, swiglu=None, ple=None, slot=None):
    a0, b0 = pairs[0]
    if mode == "nn":
        (M, K), N = a0.shape, b0.shape[1]
    elif mode == "nt":
        (M, K), N = a0.shape, b0.shape[0]
    else:
        (K, M), N = a0.shape, b0.shape[1]
    tm, tn, tk = _tile(M), _tile(N), _tile(K, (1024, 512, 256, 128))
    if slot is not None:
        tm, tn = slot.tiles(N)
    nk = K // tk
    if mode == "nn":
        a_spec = pl.BlockSpec((tm, tk), lambda i, j, k: (i, k))
        b_spec = pl.BlockSpec((tk, tn), lambda i, j, k: (k, j))
        dims = (((1,), (0,)), ((), ()))
    elif mode == "nt":
        a_spec = pl.BlockSpec((tm, tk), lambda i, j, k: (i, k))
        b_spec = pl.BlockSpec((tn, tk), lambda i, j, k: (j, k))
        dims = (((1,), (1,)), ((), ()))
    else:
        a_spec = pl.BlockSpec((tk, tm), lambda i, j, k: (k, i))
        b_spec = pl.BlockSpec((tk, tn), lambda i, j, k: (k, j))
        dims = (((0,), (0,)), ((), ()))
    mn_spec = pl.BlockSpec((tm, tn), lambda i, j, k: (i, j))
    npairs = len(pairs)
    extras = [] if res is None else [res]
    if swiglu is not None:
        extras = list(swiglu)
    if ple is not None:
        extras = list(ple)
    nex = len(extras)
    nout = 2 if (swiglu is not None or ple is not None) else 1
    carried = slot is not None and slot.buf is not None

    def body(*refs):
        ins = refs[:2 * npairs]
        ex = refs[2 * npairs:2 * npairs + nex]
        first_out = 2 * npairs + nex + (1 if carried else 0)
        outs = refs[first_out:first_out + nout]
        acc = refs[-1]
        k = pl.program_id(2)

        @pl.when(k == 0)
        def _():
            acc[...] = jnp.zeros_like(acc)

        part = None
        for p in range(npairs):
            d = lax.dot_general(ins[2 * p][...].astype(BF), ins[2 * p + 1][...].astype(BF), dims,
                                preferred_element_type=F32)
            part = d if part is None else part + d
        acc[...] += part

        @pl.when(k == nk - 1)
        def _():
            r = acc[...]
            if swiglu is not None:
                a = ex[0][...].astype(F32)
                b = ex[1][...].astype(F32)
                sg = _sigmoid(a)
                outs[0][...] = (r * b * (sg * (1.0 + a * (1.0 - sg)))).astype(out_dtype)
                outs[1][...] = (r * (a * sg)).astype(out_dtype)
            elif ple is not None:
                outs[0][...] = r.astype(out_dtype)
                outs[1][...] = ex[0][...] + _sigmoid(r) * ex[1][...].astype(F32)
            elif res is not None:
                outs[0][...] = (ex[0][...] + r).astype(out_dtype)
            else:
                outs[0][...] = r.astype(outs[0].dtype)

    in_specs = []
    operands = []
    for a, b in pairs:
        in_specs += [a_spec, b_spec]
        operands += [a, b]
    in_specs += [mn_spec] * nex
    operands += extras
    out_shape = [jax.ShapeDtypeStruct((M, N), out_dtype)] * nout
    out_specs = [mn_spec] * nout
    aliases = {}
    if ple is not None:
        out_shape[1] = jax.ShapeDtypeStruct((M, N), F32)
    if slot is not None:
        out_shape, out_specs = [slot.shape()], [slot.spec(tm, tn)]
        if carried:
            aliases = {len(operands): 0}
            in_specs.append(pl.BlockSpec(memory_space=pl.ANY))
            operands.append(slot.buf)
    outs = pl.pallas_call(
        body, out_shape=out_shape, grid=(M // tm, N // tn, nk), in_specs=in_specs,
        out_specs=out_specs, scratch_shapes=[pltpu.VMEM((tm, tn), F32)], input_output_aliases=aliases,
        compiler_params=_params(("parallel", "parallel", "arbitrary")), name=name,
    )(*operands)
    return outs if nout > 1 else outs[0]


def matmul_swiglu(x, wg, wu, name):
    (M, K), N = x.shape, wg.shape[1]
    tm, tn, tk = _tile(M), _tile(N), _tile(K, (1024, 512, 256, 128))
    nk = K // tk

    def body(x_ref, g_ref, u_ref, a_ref, b_ref, c_ref, acc_g, acc_u):
        k = pl.program_id(2)

        @pl.when(k == 0)
        def _():
            acc_g[...] = jnp.zeros_like(acc_g)
            acc_u[...] = jnp.zeros_like(acc_u)

        xv = x_ref[...].astype(BF)
        acc_g[...] += jnp.dot(xv, g_ref[...].astype(BF), preferred_element_type=F32)
        acc_u[...] += jnp.dot(xv, u_ref[...].astype(BF), preferred_element_type=F32)

        @pl.when(k == nk - 1)
        def _():
            a, b = acc_g[...], acc_u[...]
            a_ref[...] = a.astype(a_ref.dtype)
            b_ref[...] = b.astype(b_ref.dtype)
            c_ref[...] = (a * _sigmoid(a) * b).astype(c_ref.dtype)

    w_spec = pl.BlockSpec((tk, tn), lambda i, j, k: (k, j))
    mn_spec = pl.BlockSpec((tm, tn), lambda i, j, k: (i, j))
    return pl.pallas_call(
        body, out_shape=[jax.ShapeDtypeStruct((M, N), BF)] * 3, grid=(M // tm, N // tn, nk),
        in_specs=[pl.BlockSpec((tm, tk), lambda i, j, k: (i, k)), w_spec, w_spec], out_specs=[mn_spec] * 3,
        scratch_shapes=[pltpu.VMEM((tm, tn), F32)] * 2,
        compiler_params=_params(("parallel", "parallel", "arbitrary")), name=name,
    )(x, wg, wu)


def rowwise(fn, ins, out_dtypes, name):
    rows, cols = ins[0].shape
    tr = _row_tile(rows, cols)
    nin = len(ins)

    def body(*refs):
        vals = fn(*[r[...] for r in refs[:nin]])
        for o, v in zip(refs[nin:], vals):
            o[...] = v.astype(o.dtype)

    spec = pl.BlockSpec((tr, cols), lambda i: (i, 0))
    outs = pl.pallas_call(
        body, out_shape=[jax.ShapeDtypeStruct((rows, cols), d) for d in out_dtypes],
        grid=(rows // tr,), in_specs=[spec] * nin, out_specs=[spec] * len(out_dtypes),
        compiler_params=_params(("parallel",)), name=name,
    )(*ins)
    return outs


def _ple_bwd_fn(dh, z, pp):
    gate = _sigmoid(z.astype(F32))
    return (dh * pp.astype(F32) * gate * (1.0 - gate), dh * gate)


def _merge_fn(o0, o1, o2, l0, l1, l2):
    m = jnp.maximum(jnp.maximum(l0, l1), l2)
    e0, e1, e2 = jnp.exp(l0 - m), jnp.exp(l1 - m), jnp.exp(l2 - m)
    den = e0 + e1 + e2
    return ((e0 * o0 + e1 * o1 + e2 * o2) / den, m + jnp.log(den))


def rmsnorm_fwd(x, g, name, col_block=0, width=None):
    T = x.shape[0]
    W = x.shape[1] if width is None else width
    tt = _row_tile(T, W)

    def body(x_ref, g_ref, y_ref):
        xf = x_ref[...].astype(F32)
        ms = jnp.mean(xf * xf, axis=-1, keepdims=True)
        y_ref[...] = (xf * lax.rsqrt(ms + NORM_EPS) * g_ref[...]).astype(y_ref.dtype)

    return pl.pallas_call(
        body, out_shape=jax.ShapeDtypeStruct((T, W), BF), grid=(T // tt,),
        in_specs=[pl.BlockSpec((tt, W), lambda i: (i, col_block)), pl.BlockSpec((1, W), lambda i: (0, 0))],
        out_specs=pl.BlockSpec((tt, W), lambda i: (i, 0)),
        compiler_params=_params(("parallel",)), name=name,
    )(x, g.reshape(1, W).astype(F32))


def rmsnorm_bwd(x, g, dy, name, out_dtypes, dres=None, col_block=0, width=None):
    T = x.shape[0]
    W = x.shape[1] if width is None else width
    tt = _row_tile(T, W, 1 << 18)
    nout = len(out_dtypes)
    has_res = dres is not None

    def body(*refs):
        x_ref, g_ref, dy_ref = refs[:3]
        pos = 3
        res_ref = None
        if has_res:
            res_ref = refs[3]
            pos = 4
        dx_refs = refs[pos:pos + nout]
        dg_ref = refs[pos + nout]
        xf = x_ref[...].astype(F32)
        rstd = lax.rsqrt(jnp.mean(xf * xf, axis=-1, keepdims=True) + NORM_EPS)
        xhat = xf * rstd
        dyf = dy_ref[...].astype(F32)
        dn = dyf * g_ref[...]
        dx = rstd * (dn - xhat * jnp.mean(dn * xhat, axis=-1, keepdims=True))
        if has_res:
            dx = dx + res_ref[...]
        for o in dx_refs:
            o[...] = dx.astype(o.dtype)

        @pl.when(pl.program_id(0) == 0)
        def _():
            dg_ref[...] = jnp.zeros_like(dg_ref)

        dg_ref[...] += jnp.broadcast_to(jnp.sum(dyf * xhat, axis=0, keepdims=True), dg_ref.shape)

    row = pl.BlockSpec((tt, W), lambda i: (i, 0))
    in_specs = [pl.BlockSpec((tt, W), lambda i: (i, col_block)), pl.BlockSpec((1, W), lambda i: (0, 0)), row]
    operands = [x, g.reshape(1, W).astype(F32), dy]
    if has_res:
        in_specs.append(row)
        operands.append(dres)
    outs = pl.pallas_call(
        body,
        out_shape=[jax.ShapeDtypeStruct((T, W), d) for d in out_dtypes] + [jax.ShapeDtypeStruct((SUBLANES, W), F32)],
        grid=(T // tt,), in_specs=in_specs,
        out_specs=[row] * nout + [pl.BlockSpec((SUBLANES, W), lambda i: (0, 0))],
        compiler_params=_params(("arbitrary",)), name=name,
    )(*operands)
    return tuple(outs[:nout]) + (outs[nout][0],)


def loss_and_grad(y, target, name):
    T, D = y.shape
    tt = _row_tile(T, D)

    def body(y_ref, t_ref, loss_ref, dy_ref):
        d = y_ref[...] - t_ref[...]
        dy_ref[...] = d * (1.0 / D)

        @pl.when(pl.program_id(0) == 0)
        def _():
            loss_ref[...] = jnp.zeros_like(loss_ref)

        loss_ref[...] += jnp.full(loss_ref.shape, 0.5 / D, F32) * jnp.sum(d * d)

    row = pl.BlockSpec((tt, D), lambda i: (i, 0))
    loss, dy = pl.pallas_call(
        body, out_shape=[jax.ShapeDtypeStruct((SUBLANES, LANES), F32), jax.ShapeDtypeStruct((T, D), F32)],
        grid=(T // tt,), in_specs=[row, row],
        out_specs=[pl.BlockSpec((SUBLANES, LANES), lambda i: (0, 0)), row],
        compiler_params=_params(("arbitrary",)), name=name,
    )(y, target)
    return loss[0, 0], dy


def rope_tables(pos, width, r0, rot_dim):
    half = rot_dim // 2
    inv = ROPE_THETA ** (-jnp.arange(half, dtype=F32) * 2.0 / rot_dim)
    ang = pos.astype(F32)[:, None] * inv
    cos, sin = jnp.cos(ang), jnp.sin(ang)
    T = pos.shape[0]
    ones_l, ones_r = jnp.ones((T, r0), F32), jnp.ones((T, width - r0 - rot_dim), F32)
    c_tab = jnp.concatenate([ones_l, cos, cos, ones_r], axis=1)
    s_tab = jnp.concatenate([0 * ones_l, -sin, sin, 0 * ones_r], axis=1)
    perm = np.zeros((width, width), np.float32)
    for j in range(half):
        perm[r0 + j + half, r0 + j] = 1.0
        perm[r0 + j, r0 + j + half] = 1.0
    return c_tab, s_tab, jnp.asarray(perm, BF)


def _lane_permute(v, perm):
    hi = v.astype(BF)
    lo = (v - hi.astype(F32)).astype(BF)
    return (jnp.dot(hi, perm, preferred_element_type=F32) + jnp.dot(lo, perm, preferred_element_type=F32))


def headnorm_fwd(x, g, tabs, name, heads, col0, width, n_true):
    c_tab, s_tab, perm = tabs
    T = x.shape[0]
    tt = _tile(T, (1024, 512, 256, 128))
    inv_n = 1.0 / n_true

    def body(x_ref, g_ref, c_ref, s_ref, p_ref, y_ref):
        xf = x_ref[...].astype(F32)
        rstd = lax.rsqrt(jnp.sum(xf * xf, axis=-1, keepdims=True) * inv_n + NORM_EPS)
        n = xf * rstd * g_ref[...]
        y_ref[...] = (n * c_ref[...] + _lane_permute(n, p_ref[...]) * s_ref[...]).astype(y_ref.dtype)

    tab = pl.BlockSpec((tt, width), lambda i, h: (i, 0))
    return pl.pallas_call(
        body, out_shape=jax.ShapeDtypeStruct((T, heads * width), BF), grid=(T // tt, heads),
        in_specs=[pl.BlockSpec((tt, width), lambda i, h: (i, col0 + h)),
                  pl.BlockSpec((1, width), lambda i, h: (0, 0)), tab, tab,
                  pl.BlockSpec((width, width), lambda i, h: (0, 0))],
        out_specs=pl.BlockSpec((tt, width), lambda i, h: (i, h)),
        compiler_params=_params(("parallel", "parallel")), name=name,
    )(x, g.reshape(1, width).astype(F32), c_tab, s_tab, perm)


def headnorm_bwd(x, g, tabs, dy, name, heads, col0, width, n_true, head_sum=False):
    c_tab, s_tab, perm = tabs
    T = x.shape[0]
    tt = _tile(T, (1024, 512, 256, 128))
    inv_n = 1.0 / n_true

    def body(x_ref, g_ref, c_ref, s_ref, p_ref, dy_ref, dx_ref, dg_ref, *rest):
        i, h = pl.program_id(0), pl.program_id(1)
        xf = x_ref[...].astype(F32)
        rstd = lax.rsqrt(jnp.sum(xf * xf, axis=-1, keepdims=True) * inv_n + NORM_EPS)
        xhat = xf * rstd
        dyf = dy_ref[...].astype(F32)
        dn = dyf * c_ref[...] + _lane_permute(dyf * s_ref[...], p_ref[...])
        dxh = dn * g_ref[...]
        dx = rstd * (dxh - xhat * (jnp.sum(dxh * xhat, axis=-1, keepdims=True) * inv_n))
        dx_ref[...] = dx.astype(dx_ref.dtype)

        @pl.when(jnp.logical_and(i == 0, h == 0))
        def _():
            dg_ref[...] = jnp.zeros_like(dg_ref)

        dg_ref[...] += jnp.broadcast_to(jnp.sum(dn * xhat, axis=0, keepdims=True), dg_ref.shape)
        if head_sum:
            sum_ref = rest[0]

            @pl.when(h == 0)
            def _():
                sum_ref[...] = jnp.zeros_like(sum_ref)

            sum_ref[...] += dx

    tab = pl.BlockSpec((tt, width), lambda i, h: (i, 0))
    out_shape = [jax.ShapeDtypeStruct((T, heads * width), BF), jax.ShapeDtypeStruct((SUBLANES, width), F32)]
    out_specs = [pl.BlockSpec((tt, width), lambda i, h: (i, h)), pl.BlockSpec((SUBLANES, width), lambda i, h: (0, 0))]
    if head_sum:
        out_shape.append(jax.ShapeDtypeStruct((T, width), F32))
        out_specs.append(tab)
    outs = pl.pallas_call(
        body, out_shape=out_shape, grid=(T // tt, heads),
        in_specs=[pl.BlockSpec((tt, width), lambda i, h: (i, col0 + h)),
                  pl.BlockSpec((1, width), lambda i, h: (0, 0)), tab, tab,
                  pl.BlockSpec((width, width), lambda i, h: (0, 0)),
                  pl.BlockSpec((tt, width), lambda i, h: (i, h))],
        out_specs=out_specs, compiler_params=_params(("arbitrary", "arbitrary")), name=name,
    )(x, g.reshape(1, width).astype(F32), c_tab, s_tab, perm, dy)
    return (outs[0], outs[1][0]) + ((outs[2],) if head_sum else ())


class Attn:
    def __init__(self, T, dil, hq, group, qc, q0, kc, k0, vc, v0, vstride, dqk, scale, half_window, blk):
        self.T, self.dil, self.hq, self.group = T, dil, hq, group
        self.hkv = hq // group
        self.qc, self.q0, self.kc, self.k0, self.vc, self.v0, self.vstride = qc, q0, kc, k0, vc, v0, vstride
        self.dqk, self.scale, self.hw = dqk, scale, half_window
        self.len = T // dil
        self.blk = min(blk, self.len)
        self.nb = self.len // self.blk
        self.band = half_window is not None
        self.steps = 3 if self.band else self.nb

    def other(self, i, s):
        if self.band:
            nom = i - 1 + s
            return jnp.minimum(jnp.maximum(nom, 0), self.nb - 1), nom
        return s, s

    def chains(self, a):
        return a.reshape(self.len, self.dil * a.shape[1])

    def unchain(self, a, cols):
        return a.reshape(self.T, cols)

    def mask(self, q_nom, k_nom):
        if not self.band:
            return None
        qpos = q_nom * self.blk + lax.broadcasted_iota(jnp.int32, (self.blk, self.blk), 0)
        kpos = k_nom * self.blk + lax.broadcasted_iota(jnp.int32, (self.blk, self.blk), 1)
        ok = jnp.abs(qpos - kpos) <= self.hw
        for pos in (qpos, kpos):
            ok = jnp.logical_and(ok, jnp.logical_and(pos >= 0, pos < self.len))
        return ok


def _scores(cfg, q, k, q_nom, k_nom):
    s = lax.dot_general(q, k, (((1,), (1,)), ((), ())), preferred_element_type=F32) * cfg.scale
    ok = cfg.mask(q_nom, k_nom)
    return s if ok is None else jnp.where(ok, s, NEG)


def flash_fwd(cfg, q, k, v, name, out_dtype, sink=None):
    blk, dqk = cfg.blk, cfg.dqk
    has_sink = sink is not None

    def body(*refs):
        if has_sink:
            sink_ref, refs = refs[0], refs[1:]
        q_ref, k_ref, v_ref, o_ref, lse_ref, m_sc, l_sc, acc_sc = refs
        i, s = pl.program_id(2), pl.program_id(3)

        @pl.when(s == 0)
        def _():
            if has_sink:
                m_sc[...] = jnp.broadcast_to(sink_ref[0, :1, :], m_sc.shape)
                l_sc[...] = jnp.ones_like(l_sc)
            else:
                m_sc[...] = jnp.full(m_sc.shape, NEG, F32)
                l_sc[...] = jnp.zeros_like(l_sc)
            acc_sc[...] = jnp.zeros_like(acc_sc)

        _, k_nom = cfg.other(i, s)
        sc = _scores(cfg, q_ref[...], k_ref[...], i, k_nom)
        m_prev = m_sc[...]
        m_new = jnp.maximum(m_prev, jnp.max(sc, axis=-1, keepdims=True))
        p = jnp.exp(sc - m_new[:, :1])
        alpha = jnp.exp(m_prev - m_new)
        l_sc[...] = alpha * l_sc[...] + jnp.sum(p, axis=-1, keepdims=True)
        acc_sc[...] = alpha * acc_sc[...] + jnp.dot(p.astype(BF), v_ref[...], preferred_element_type=F32)
        m_sc[...] = m_new

        @pl.when(s == cfg.steps - 1)
        def _():
            o_ref[...] = (acc_sc[...] / l_sc[...]).astype(o_ref.dtype)
            lse_ref[...] = m_sc[...] + jnp.log(l_sc[...])

    g = cfg.group
    q_spec = pl.BlockSpec((blk, dqk), lambda r, h, i, s: (i, r * cfg.qc + cfg.q0 + h))
    k_spec = pl.BlockSpec((blk, dqk), lambda r, h, i, s: (cfg.other(i, s)[0], r * cfg.kc + cfg.k0 + h // g))
    v_spec = pl.BlockSpec((blk, LANES),
                          lambda r, h, i, s: (cfg.other(i, s)[0], r * cfg.vc + cfg.v0 + cfg.vstride * (h // g)))
    o_spec = pl.BlockSpec((blk, LANES), lambda r, h, i, s: (i, r * cfg.hq + h))
    in_specs = [q_spec, k_spec, v_spec]
    operands = [cfg.chains(q), cfg.chains(k), cfg.chains(v)]
    if has_sink:
        in_specs.insert(0, pl.BlockSpec((1, SUBLANES, LANES), lambda r, h, i, s: (h, 0, 0)))
        operands.insert(0, sink)
    cols = cfg.dil * cfg.hq * LANES
    o, lse = pl.pallas_call(
        body, out_shape=[jax.ShapeDtypeStruct((cfg.len, cols), out_dtype), jax.ShapeDtypeStruct((cfg.len, cols), F32)],
        grid=(cfg.dil, cfg.hq, cfg.nb, cfg.steps), in_specs=in_specs, out_specs=[o_spec, o_spec],
        scratch_shapes=[pltpu.VMEM((blk, LANES), F32)] * 3,
        compiler_params=_params(("parallel", "parallel", "parallel", "arbitrary")), name=name,
    )(*operands)
    return cfg.unchain(o, cfg.hq * LANES), cfg.unchain(lse, cfg.hq * LANES)


def flash_dq(cfg, q, k, v, do, o, lse, name, sink=None):
    blk, dqk = cfg.blk, cfg.dqk
    has_sink = sink is not None

    def body(*refs):
        if has_sink:
            sink_ref, refs = refs[0], refs[1:]
        q_ref, k_ref, v_ref, do_ref, o_ref, lse_ref = refs[:6]
        dq_ref = refs[6]
        dq_sc, delta_sc = refs[-2:]
        i, s = pl.program_id(2), pl.program_id(3)

        @pl.when(s == 0)
        def _():
            dq_sc[...] = jnp.zeros_like(dq_sc)
            delta = jnp.sum(do_ref[...].astype(F32) * o_ref[...].astype(F32), axis=-1, keepdims=True)
            delta_sc[...] = jnp.broadcast_to(delta, delta_sc.shape)

        _, k_nom = cfg.other(i, s)
        k = k_ref[...]
        sc = _scores(cfg, q_ref[...], k, i, k_nom)
        p = jnp.exp(sc - lse_ref[:, :1])
        dp = lax.dot_general(do_ref[...], v_ref[...], (((1,), (1,)), ((), ())), preferred_element_type=F32)
        ds = p * (dp - delta_sc[:, :1]) * cfg.scale
        dq_sc[...] += jnp.dot(ds.astype(BF), k, preferred_element_type=F32)

        @pl.when(s == cfg.steps - 1)
        def _():
            dq_ref[...] = dq_sc[...].astype(dq_ref.dtype)
            if has_sink:
                ps = jnp.exp(sink_ref[0, :1, :] - lse_ref[...])
                part = -jnp.sum(ps * delta_sc[...], axis=0, keepdims=True)
                refs[7][...] = jnp.broadcast_to(part, refs[7].shape)

    g = cfg.group
    q_spec = pl.BlockSpec((blk, dqk), lambda r, h, i, s: (i, r * cfg.qc + cfg.q0 + h))
    k_spec = pl.BlockSpec((blk, dqk), lambda r, h, i, s: (cfg.other(i, s)[0], r * cfg.kc + cfg.k0 + h // g))
    v_spec = pl.BlockSpec((blk, LANES),
                          lambda r, h, i, s: (cfg.other(i, s)[0], r * cfg.vc + cfg.v0 + cfg.vstride * (h // g)))
    o_spec = pl.BlockSpec((blk, LANES), lambda r, h, i, s: (i, r * cfg.hq + h))
    dq_spec = pl.BlockSpec((blk, dqk), lambda r, h, i, s: (i, r * cfg.hq + h))
    in_specs = [q_spec, k_spec, v_spec, o_spec, o_spec, o_spec]
    operands = [cfg.chains(q), cfg.chains(k), cfg.chains(v), cfg.chains(do), cfg.chains(o), cfg.chains(lse)]
    out_shape = [jax.ShapeDtypeStruct((cfg.len, cfg.dil * cfg.hq * dqk), BF)]
    out_specs = [dq_spec]
    if has_sink:
        in_specs.insert(0, pl.BlockSpec((1, SUBLANES, LANES), lambda r, h, i, s: (h, 0, 0)))
        operands.insert(0, sink)
        out_shape.append(jax.ShapeDtypeStruct((cfg.hq * cfg.nb * SUBLANES, LANES), F32))
        out_specs.append(pl.BlockSpec((SUBLANES, LANES), lambda r, h, i, s: (h * cfg.nb + i, 0)))
    outs = pl.pallas_call(
        body, out_shape=out_shape, grid=(cfg.dil, cfg.hq, cfg.nb, cfg.steps), in_specs=in_specs,
        out_specs=out_specs, scratch_shapes=[pltpu.VMEM((blk, dqk), F32), pltpu.VMEM((blk, LANES), F32)],
        compiler_params=_params(("parallel", "parallel", "parallel", "arbitrary")), name=name,
    )(*operands)
    dq = cfg.unchain(outs[0], cfg.hq * dqk)
    if has_sink:
        return dq, outs[1].reshape(cfg.hq, cfg.nb, SUBLANES, LANES)[:, :, 0, :]
    return dq


def flash_dkv(cfg, q, k, v, do, o, lse, name, out_dtype, add=None):
    blk, dqk, g, nw = cfg.blk, cfg.dqk, cfg.group, cfg.steps
    has_add = add is not None

    def body(*refs):
        k_ref, v_ref, q_ref, do_ref, o_ref, lse_ref = refs[:6]
        pos = 8 if has_add else 6
        dk_ref, dv_ref = refs[pos:pos + 2]
        dk_sc, dv_sc = refs[-2:]
        i, j = pl.program_id(2), pl.program_id(3)

        @pl.when(j == 0)
        def _():
            dk_sc[...] = jnp.zeros_like(dk_sc)
            dv_sc[...] = jnp.zeros_like(dv_sc)

        _, q_nom = cfg.other(i, j % nw)
        q = q_ref[...]
        do = do_ref[...]
        sc = _scores(cfg, q, k_ref[...], q_nom, i)
        p = jnp.exp(sc - lse_ref[:, :1])
        delta = jnp.sum(do.astype(F32) * o_ref[...].astype(F32), axis=-1, keepdims=True)
        dv_sc[...] += lax.dot_general(p.astype(BF), do, (((0,), (0,)), ((), ())), preferred_element_type=F32)
        dp = lax.dot_general(do, v_ref[...], (((1,), (1,)), ((), ())), preferred_element_type=F32)
        ds = p * (dp - delta) * cfg.scale
        dk_sc[...] += lax.dot_general(ds.astype(BF), q, (((0,), (0,)), ((), ())), preferred_element_type=F32)

        @pl.when(j == g * nw - 1)
        def _():
            dk, dv = dk_sc[...], dv_sc[...]
            if has_add:
                dk, dv = dk + refs[6][...].astype(F32), dv + refs[7][...].astype(F32)
            dk_ref[...] = dk.astype(dk_ref.dtype)
            dv_ref[...] = dv.astype(dv_ref.dtype)

    def qrow(i, j):
        return cfg.other(i, j % nw)[0]

    k_spec = pl.BlockSpec((blk, dqk), lambda r, h, i, j: (i, r * cfg.kc + cfg.k0 + h))
    v_spec = pl.BlockSpec((blk, LANES), lambda r, h, i, j: (i, r * cfg.vc + cfg.v0 + cfg.vstride * h))
    q_spec = pl.BlockSpec((blk, dqk), lambda r, h, i, j: (qrow(i, j), r * cfg.qc + cfg.q0 + h * g + j // nw))
    o_spec = pl.BlockSpec((blk, LANES), lambda r, h, i, j: (qrow(i, j), r * cfg.hq + h * g + j // nw))
    dk_spec = pl.BlockSpec((blk, dqk), lambda r, h, i, j: (i, r * cfg.hkv + h))
    dv_spec = pl.BlockSpec((blk, LANES), lambda r, h, i, j: (i, r * cfg.hkv + h))
    in_specs = [k_spec, v_spec, q_spec, o_spec, o_spec, o_spec]
    operands = [cfg.chains(k), cfg.chains(v), cfg.chains(q), cfg.chains(do), cfg.chains(o), cfg.chains(lse)]
    if has_add:
        in_specs += [dk_spec, dv_spec]
        operands += [cfg.chains(add[0]), cfg.chains(add[1])]
    dk, dv = pl.pallas_call(
        body,
        out_shape=[jax.ShapeDtypeStruct((cfg.len, cfg.dil * cfg.hkv * dqk), out_dtype),
                   jax.ShapeDtypeStruct((cfg.len, cfg.dil * cfg.hkv * LANES), out_dtype)],
        grid=(cfg.dil, cfg.hkv, cfg.nb, g * nw), in_specs=in_specs, out_specs=[dk_spec, dv_spec],
        scratch_shapes=[pltpu.VMEM((blk, dqk), F32), pltpu.VMEM((blk, LANES), F32)],
        compiler_params=_params(("parallel", "parallel", "parallel", "arbitrary")), name=name,
    )(*operands)
    return cfg.unchain(dk, cfg.hkv * dqk), cfg.unchain(dv, cfg.hkv * LANES)


class Band:
    def __init__(self, T, dil, hq, group, per, qc, q0, kc, k0, vc, v0, scale, hw, blk):
        self.T, self.dil, self.hq, self.group, self.per = T, dil, hq, group, per
        self.pk = per // group
        self.hkv = hq // group
        self.scale, self.hw = scale, hw
        self.len = T // dil
        self.blk = min(blk, self.len)
        self.nb = self.len // self.blk
        self.win = self.blk + 2 * hw
        self.qcol = lambda r: (r * qc + q0) // per
        self.kcol = lambda r: (r * kc + k0) // self.pk
        self.vcol = lambda r: (r * vc + v0) // self.pk
        self.ocol = lambda r: (r * hq) // per
        self.dkcol = lambda r: (r * self.hkv) // self.pk
        assert hw <= self.blk and qc % per == 0 and q0 % per == 0 and kc % self.pk == 0 and k0 % self.pk == 0
        assert vc % self.pk == 0 and v0 % self.pk == 0

    def chains(self, a):
        return a.reshape(self.len, self.dil * a.shape[1])

    def rows3(self, width, col):
        nb = self.nb
        return [pl.BlockSpec((self.blk, width), lambda r, h, i: (jnp.maximum(i - 1, 0), col(r) + h)),
                pl.BlockSpec((self.blk, width), lambda r, h, i: (i, col(r) + h)),
                pl.BlockSpec((self.blk, width), lambda r, h, i: (jnp.minimum(i + 1, nb - 1), col(r) + h))]

    def window(self, prev, cur, nxt, j):
        cols = slice(j * LANES, (j + 1) * LANES)
        return jnp.concatenate([prev[self.blk - self.hw:, cols], cur[:, cols], nxt[:self.hw, cols]], axis=0)

    def valid(self, i, window_is_rows):
        shape = (self.win, self.blk) if window_is_rows else (self.blk, self.win)
        wdim = 0 if window_is_rows else 1
        bpos = i * self.blk + lax.broadcasted_iota(jnp.int32, shape, 1 - wdim)
        wpos = i * self.blk - self.hw + lax.broadcasted_iota(jnp.int32, shape, wdim)
        ok = jnp.abs(bpos - wpos) <= self.hw
        return jnp.logical_and(ok, jnp.logical_and(wpos >= 0, wpos < self.len))


def band_fwd(cfg, q, k, v, name, out_dtype, sink=None):
    blk, per, pk = cfg.blk, cfg.per, cfg.pk
    has_sink = sink is not None

    def body(*refs):
        if has_sink:
            sink_ref, refs = refs[0], refs[1:]
        q_ref, kp, kc, kn, vp, vc, vn, o_ref, lse_ref = refs
        ok = cfg.valid(pl.program_id(2), False)
        for j in range(per):
            jk = j // cfg.group
            if j % cfg.group == 0:
                kw = cfg.window(kp, kc, kn, jk)
                vw = cfg.window(vp, vc, vn, jk)
            cols = slice(j * LANES, (j + 1) * LANES)
            s = lax.dot_general(q_ref[:, cols], kw, (((1,), (1,)), ((), ())), preferred_element_type=F32) * cfg.scale
            s = jnp.where(ok, s, NEG)
            m = jnp.max(s, axis=-1, keepdims=True)
            if has_sink:
                sk = sink_ref[j, :1, :1]
                m = jnp.maximum(m, sk)
            e = jnp.exp(s - m)
            den = jnp.sum(e, axis=-1, keepdims=True)
            if has_sink:
                den = den + jnp.exp(sk - m)
            o = jnp.dot(e.astype(BF), vw, preferred_element_type=F32) / den
            o_ref[:, cols] = o.astype(o_ref.dtype)
            lse_ref[:, cols] = jnp.broadcast_to(m + jnp.log(den), (blk, LANES))

    q_spec = pl.BlockSpec((blk, per * LANES), lambda r, h, i: (i, cfg.qcol(r) + h))
    o_spec = pl.BlockSpec((blk, per * LANES), lambda r, h, i: (i, cfg.ocol(r) + h))
    in_specs = [q_spec] + cfg.rows3(pk * LANES, cfg.kcol) + cfg.rows3(pk * LANES, cfg.vcol)
    kc_, vc_ = cfg.chains(k), cfg.chains(v)
    operands = [cfg.chains(q), kc_, kc_, kc_, vc_, vc_, vc_]
    if has_sink:
        in_specs.insert(0, pl.BlockSpec((per, SUBLANES, LANES), lambda r, h, i: (h, 0, 0)))
        operands.insert(0, sink)
    cols = cfg.dil * cfg.hq * LANES
    o, lse = pl.pallas_call(
        body, out_shape=[jax.ShapeDtypeStruct((cfg.len, cols), out_dtype), jax.ShapeDtypeStruct((cfg.len, cols), F32)],
        grid=(cfg.dil, cfg.hq // per, cfg.nb), in_specs=in_specs, out_specs=[o_spec, o_spec],
        compiler_params=_params(("parallel", "parallel", "parallel")), name=name,
    )(*operands)
    return o.reshape(cfg.T, cfg.hq * LANES), lse.reshape(cfg.T, cfg.hq * LANES)


def band_dq(cfg, q, k, v, do, o, lse, name, sink=None):
    blk, per, pk = cfg.blk, cfg.per, cfg.pk
    has_sink = sink is not None

    def body(*refs):
        if has_sink:
            sink_ref, refs = refs[0], refs[1:]
        q_ref, kp, kc, kn, vp, vc, vn, do_ref, o_ref, lse_ref, dq_ref = refs[:11]
        ok = cfg.valid(pl.program_id(2), False)
        for j in range(per):
            jk = j // cfg.group
            if j % cfg.group == 0:
                kw = cfg.window(kp, kc, kn, jk)
                vw = cfg.window(vp, vc, vn, jk)
            cols = slice(j * LANES, (j + 1) * LANES)
            do = do_ref[:, cols]
            lse = lse_ref[:, j * LANES:j * LANES + 1]
            delta = jnp.sum(do.astype(F32) * o_ref[:, cols].astype(F32), axis=-1, keepdims=True)
            s = lax.dot_general(q_ref[:, cols], kw, (((1,), (1,)), ((), ())), preferred_element_type=F32) * cfg.scale
            p = jnp.exp(jnp.where(ok, s, NEG) - lse)
            dp = lax.dot_general(do, vw, (((1,), (1,)), ((), ())), preferred_element_type=F32)
            ds = p * (dp - delta) * cfg.scale
            dq_ref[:, cols] = jnp.dot(ds.astype(BF), kw, preferred_element_type=F32).astype(dq_ref.dtype)
            if has_sink:
                part = -jnp.sum(jnp.exp(sink_ref[j, :1, :1] - lse) * delta, axis=0, keepdims=True)
                refs[11][j * SUBLANES:(j + 1) * SUBLANES, :] = jnp.broadcast_to(part, (SUBLANES, LANES))

    q_spec = pl.BlockSpec((blk, per * LANES), lambda r, h, i: (i, cfg.qcol(r) + h))
    o_spec = pl.BlockSpec((blk, per * LANES), lambda r, h, i: (i, cfg.ocol(r) + h))
    in_specs = [q_spec] + cfg.rows3(pk * LANES, cfg.kcol) + cfg.rows3(pk * LANES, cfg.vcol) + [o_spec] * 3
    kc_, vc_ = cfg.chains(k), cfg.chains(v)
    operands = [cfg.chains(q), kc_, kc_, kc_, vc_, vc_, vc_, cfg.chains(do), cfg.chains(o), cfg.chains(lse)]
    out_shape = [jax.ShapeDtypeStruct((cfg.len, cfg.dil * cfg.hq * LANES), BF)]
    out_specs = [o_spec]
    if has_sink:
        in_specs.insert(0, pl.BlockSpec((per, SUBLANES, LANES), lambda r, h, i: (h, 0, 0)))
        operands.insert(0, sink)
        out_shape.append(jax.ShapeDtypeStruct((cfg.hq // per, cfg.nb, per * SUBLANES, LANES), F32))
        out_specs.append(pl.BlockSpec((None, None, per * SUBLANES, LANES), lambda r, h, i: (h, i, 0, 0)))
    outs = pl.pallas_call(
        body, out_shape=out_shape, grid=(cfg.dil, cfg.hq // per, cfg.nb), in_specs=in_specs, out_specs=out_specs,
        compiler_params=_params(("parallel", "parallel", "parallel")), name=name,
    )(*operands)
    dq = outs[0].reshape(cfg.T, cfg.hq * LANES)
    return (dq, outs[1]) if has_sink else dq


def band_dkv(cfg, q, k, v, do, o, lse, name, out_dtype, add=None):
    blk, per, pk, group = cfg.blk, cfg.per, cfg.pk, cfg.group
    has_add = add is not None

    def body(*refs):
        k_ref, v_ref = refs[:2]
        qs, dos, os_, lses = refs[2:5], refs[5:8], refs[8:11], refs[11:14]
        pos = 16 if has_add else 14
        dk_ref, dv_ref = refs[pos:pos + 2]
        ok = cfg.valid(pl.program_id(2), True)
        for jk in range(pk):
            kcols = slice(jk * LANES, (jk + 1) * LANES)
            kt, vt = k_ref[:, kcols], v_ref[:, kcols]
            dk = jnp.zeros((blk, LANES), F32)
            dv = jnp.zeros((blk, LANES), F32)
            for g in range(group):
                j = jk * group + g
                qw = cfg.window(*qs, j)
                dow = cfg.window(*dos, j)
                lse = cfg.window(*lses, j)[:, :1]
                delta = jnp.sum(dow.astype(F32) * cfg.window(*os_, j).astype(F32), axis=-1, keepdims=True)
                s = lax.dot_general(qw, kt, (((1,), (1,)), ((), ())), preferred_element_type=F32) * cfg.scale
                p = jnp.exp(jnp.where(ok, s, NEG) - lse)
                dv = dv + lax.dot_general(p.astype(BF), dow, (((0,), (0,)), ((), ())), preferred_element_type=F32)
                dp = lax.dot_general(dow, vt, (((1,), (1,)), ((), ())), preferred_element_type=F32)
                ds = p * (dp - delta) * cfg.scale
                dk = dk + lax.dot_general(ds.astype(BF), qw, (((0,), (0,)), ((), ())), preferred_element_type=F32)
            if has_add:
                dk, dv = dk + refs[14][:, kcols].astype(F32), dv + refs[15][:, kcols].astype(F32)
            dk_ref[:, kcols] = dk.astype(dk_ref.dtype)
            dv_ref[:, kcols] = dv.astype(dv_ref.dtype)

    k_spec = pl.BlockSpec((blk, pk * LANES), lambda r, h, i: (i, cfg.kcol(r) + h))
    v_spec = pl.BlockSpec((blk, pk * LANES), lambda r, h, i: (i, cfg.vcol(r) + h))
    d_spec = pl.BlockSpec((blk, pk * LANES), lambda r, h, i: (i, cfg.dkcol(r) + h))
    in_specs = [k_spec, v_spec] + cfg.rows3(per * LANES, cfg.qcol) + cfg.rows3(per * LANES, cfg.ocol) * 3
    qc_, doc, oc, lc = cfg.chains(q), cfg.chains(do), cfg.chains(o), cfg.chains(lse)
    operands = [cfg.chains(k), cfg.chains(v), qc_, qc_, qc_, doc, doc, doc, oc, oc, oc, lc, lc, lc]
    if has_add:
        in_specs += [d_spec, d_spec]
        operands += [cfg.chains(add[0]), cfg.chains(add[1])]
    cols = cfg.dil * cfg.hkv * LANES
    dk, dv = pl.pallas_call(
        body, out_shape=[jax.ShapeDtypeStruct((cfg.len, cols), out_dtype)] * 2,
        grid=(cfg.dil, cfg.hq // per, cfg.nb), in_specs=in_specs, out_specs=[d_spec, d_spec],
        compiler_params=_params(("parallel", "parallel", "parallel")), name=name,
    )(*operands)
    return dk.reshape(cfg.T, cfg.hkv * LANES), dv.reshape(cfg.T, cfg.hkv * LANES)


HBM_SPEC = pl.BlockSpec(memory_space=pltpu.HBM)


def _place():
    x, y, c = lax.axis_index("x"), lax.axis_index("y"), lax.axis_index("c")
    chips = [(1 - x, y), (x, 1 - y), (1 - x, 1 - y)]
    return x, y, c, chips


def gather_weights(shards):
    n = len(shards)

    def body(*refs):
        ins, outs = refs[:n], refs[n:2 * n]
        send_sems, recv_sems, local_sems = refs[2 * n:]
        x, y, c, chips = _place()
        me = 2 * x + y
        sibling = (x, y, 1 - c)

        def copy(w, k, src, chip_of_block, half, to):
            return pltpu.make_async_remote_copy(
                src_ref=src, dst_ref=outs[w].at[chip_of_block, half], send_sem=send_sems.at[6 * w + k],
                recv_sem=recv_sems.at[6 * w + k], device_id=to, device_id_type=MESH)

        started = []
        local = []
        for w in range(n):
            own = pltpu.make_async_copy(ins[w], outs[w].at[me], local_sems.at[w])
            own.start()
            local.append(own)
            for j, chip in enumerate(chips):
                cp = copy(w, j, ins[w].at[c], me, c, (*chip, c))
                cp.start()
                started.append(cp)
        for w in range(n):
            for j, (cx, cy) in enumerate(chips):
                them = 2 * cx + cy
                copy(w, j, ins[w].at[c], them, c, (cx, cy, c)).wait_recv()
                fwd = copy(w, 3 + j, outs[w].at[them, c], them, c, sibling)
                fwd.start()
                started.append(fwd)
        for w in range(n):
            for j, (cx, cy) in enumerate(chips):
                copy(w, 3 + j, ins[w].at[c], 2 * cx + cy, 1 - c, sibling).wait_recv()
        for cp in started:
            cp.wait_send()
        for own in local:
            own.wait()

    return pl.pallas_call(
        body, out_shape=[jax.ShapeDtypeStruct((4,) + s.shape, s.dtype) for s in shards],
        in_specs=[HBM_SPEC] * n, out_specs=[HBM_SPEC] * n,
        scratch_shapes=[pltpu.SemaphoreType.DMA((6 * n,)), pltpu.SemaphoreType.DMA((6 * n,)),
                        pltpu.SemaphoreType.DMA((n,))],
        name="gather_weights",
    )(*shards)


def _core_index():
    return lax.axis_index("c").astype(jnp.int32).reshape(1)


def presum_core_halves(g2, core, name):
    _, rows, cols = g2.shape
    tr = _row_tile(rows, cols)
    nb = rows // tr
    g2 = g2.reshape(2 * rows, cols)

    def body(core_ref, mine_ref, other_ref, out_ref, land, send_sems, recv_sems):
        x, y, c, _ = _place()
        slot = pl.program_id(0) % 2
        cp = pltpu.make_async_remote_copy(
            src_ref=other_ref, dst_ref=land.at[slot], send_sem=send_sems.at[slot], recv_sem=recv_sems.at[slot],
            device_id=(x, y, 1 - c), device_id_type=MESH)
        cp.start()
        cp.wait_recv()
        out_ref[...] = (mine_ref[...] + land[slot]).astype(out_ref.dtype)
        cp.wait_send()

    grid_spec = pltpu.PrefetchScalarGridSpec(
        num_scalar_prefetch=1, grid=(nb,),
        in_specs=[pl.BlockSpec((tr, cols), lambda i, core: (core[0] * nb + i, 0)),
                  pl.BlockSpec((tr, cols), lambda i, core: ((1 - core[0]) * nb + i, 0))],
        out_specs=pl.BlockSpec((tr, cols), lambda i, core: (i, 0)),
        scratch_shapes=[pltpu.VMEM((2, tr, cols), F32), pltpu.SemaphoreType.DMA((2,)), pltpu.SemaphoreType.DMA((2,))])
    return pl.pallas_call(
        body, out_shape=jax.ShapeDtypeStruct((rows, cols), BF), grid_spec=grid_spec,
        compiler_params=_params(("arbitrary",)), name=name,
    )(core, g2, g2)


def sum_and_swap(landed, name):
    n, rows, cols = landed.shape
    tr = _row_tile(rows, cols)

    def body(*refs):
        slots = refs[:n]
        mine_ref, theirs_ref, out_buf, land, send_sems, recv_sems = refs[n:]
        x, y, c, _ = _place()
        slot = pl.program_id(0) % 2
        tot = slots[0][...].astype(F32)
        for r in slots[1:]:
            tot = tot + r[...].astype(F32)
        mine_ref[...] = tot
        out_buf[slot] = tot
        cp = pltpu.make_async_remote_copy(
            src_ref=out_buf.at[slot], dst_ref=land.at[slot], send_sem=send_sems.at[slot], recv_sem=recv_sems.at[slot],
            device_id=(x, y, 1 - c), device_id_type=MESH)
        cp.start()
        cp.wait_recv()
        theirs_ref[...] = land[slot]
        cp.wait_send()

    specs = [pl.BlockSpec((None, tr, cols), functools.partial(lambda s, i: (s, i, 0), s)) for s in range(n)]
    row = pl.BlockSpec((tr, cols), lambda i: (i, 0))
    return pl.pallas_call(
        body, out_shape=[jax.ShapeDtypeStruct((rows, cols), F32)] * 2, grid=(rows // tr,), in_specs=specs,
        out_specs=[row, row],
        scratch_shapes=[pltpu.VMEM((2, tr, cols), F32), pltpu.VMEM((2, tr, cols), F32),
                        pltpu.SemaphoreType.DMA((2,)), pltpu.SemaphoreType.DMA((2,))],
        compiler_params=_params(("arbitrary",)), name=name,
    )(*([landed] * n))


def scatter_partials(parts):
    n = len(parts)

    def body(*refs):
        ins, outs = refs[:n], refs[n:2 * n]
        send_sems, recv_sems, local_sems = refs[2 * n:]
        x, y, c, chips = _place()
        me = 2 * x + y
        started = []
        for w in range(n):
            own = pltpu.make_async_copy(ins[w].at[me], outs[w].at[me], local_sems.at[w])
            own.start()
            started.append(own)
        sends = []
        for w in range(n):
            for j, (cx, cy) in enumerate(chips):
                cp = pltpu.make_async_remote_copy(
                    src_ref=ins[w].at[2 * cx + cy], dst_ref=outs[w].at[me], send_sem=send_sems.at[3 * w + j],
                    recv_sem=recv_sems.at[3 * w + j], device_id=(cx, cy, c), device_id_type=MESH)
                cp.start()
                sends.append(cp)
        for w in range(n):
            for j, (cx, cy) in enumerate(chips):
                pltpu.make_async_remote_copy(
                    src_ref=ins[w].at[me], dst_ref=outs[w].at[2 * cx + cy], send_sem=send_sems.at[3 * w + j],
                    recv_sem=recv_sems.at[3 * w + j], device_id=(cx, cy, c), device_id_type=MESH).wait_recv()
        for cp in sends:
            cp.wait_send()
        for own in started:
            own.wait()

    return pl.pallas_call(
        body, out_shape=[jax.ShapeDtypeStruct(p.shape, p.dtype) for p in parts],
        in_specs=[HBM_SPEC] * n, out_specs=[HBM_SPEC] * n,
        scratch_shapes=[pltpu.SemaphoreType.DMA((3 * n,)), pltpu.SemaphoreType.DMA((3 * n,)),
                        pltpu.SemaphoreType.DMA((n,))],
        name="scatter_partials",
    )(*parts)


def adamw_halves(w, mine, theirs, m, v, core, name):
    rows, cols = w.shape
    tr = _row_tile(rows // 2, cols, 1 << 18)
    nh = rows // 2 // tr

    def body(core_ref, w_ref, a_ref, b_ref, m_ref, v_ref, g_out, d_out, m_out, v_out):
        g = jnp.where(pl.program_id(0) // nh == core_ref[0], a_ref[...], b_ref[...])
        d_out[...], m_out[...], v_out[...] = _adam_fn(w_ref[...], g, m_ref[...], v_ref[...])
        g_out[...] = g

    full = pl.BlockSpec((tr, cols), lambda i, core: (i, 0))
    half = pl.BlockSpec((tr, cols), lambda i, core: (i % nh, 0))
    grid_spec = pltpu.PrefetchScalarGridSpec(
        num_scalar_prefetch=1, grid=(rows // tr,), in_specs=[full, half, half, full, full], out_specs=[full] * 4)
    return pl.pallas_call(
        body, out_shape=[jax.ShapeDtypeStruct((rows, cols), F32)] * 4, grid_spec=grid_spec,
        compiler_params=_params(("parallel",)), name=name,
    )(core, w, mine, theirs, m, v)


def gather_small(vec):
    rows = vec.shape[0]

    def body(v_ref, out_ref, send_sems, recv_sems):
        x, y, c, _ = _place()
        me = 4 * x + 2 * y + c
        out_ref[me] = v_ref[...]
        flips = [(dx, dy, dc) for dx in (0, 1) for dy in (0, 1) for dc in (0, 1)][1:]

        def peer(f):
            return tuple(1 - a if d else a for a, d in zip((x, y, c), f))

        def copy(k, block, to):
            return pltpu.make_async_remote_copy(
                src_ref=v_ref, dst_ref=out_ref.at[block], send_sem=send_sems.at[k], recv_sem=recv_sems.at[k],
                device_id=to, device_id_type=MESH)

        sends = [copy(k, me, peer(f)) for k, f in enumerate(flips)]
        for cp in sends:
            cp.start()
        for k, f in enumerate(flips):
            px, py, pc = peer(f)
            copy(k, 4 * px + 2 * py + pc, peer(f)).wait_recv()
        for cp in sends:
            cp.wait_send()

    vm = pl.BlockSpec(memory_space=pltpu.VMEM)
    return pl.pallas_call(
        body, out_shape=jax.ShapeDtypeStruct((8, rows, SMALL_COLS), F32), in_specs=[vm], out_specs=vm,
        scratch_shapes=[pltpu.SemaphoreType.DMA((7,)), pltpu.SemaphoreType.DMA((7,))], name="gather_small",
    )(vec)


def sum_slots(a, out_dtype, name):
    n, rows, cols = a.shape
    tr = _row_tile(rows, cols)

    def body(*refs):
        tot = refs[0][...].astype(F32)
        for r in refs[1:n]:
            tot = tot + r[...].astype(F32)
        refs[n][...] = tot.astype(out_dtype)

    specs = [pl.BlockSpec((None, tr, cols), functools.partial(lambda s, i: (s, i, 0), s)) for s in range(n)]
    return pl.pallas_call(
        body, out_shape=jax.ShapeDtypeStruct((rows, cols), out_dtype), grid=(rows // tr,), in_specs=specs,
        out_specs=pl.BlockSpec((tr, cols), lambda i: (i, 0)), compiler_params=_params(("parallel",)), name=name,
    )(*([a] * n))


def _adam_fn(w, g, m, v):
    m = ADAM_B1 * m + (1.0 - ADAM_B1) * g
    v = ADAM_B2 * v + (1.0 - ADAM_B2) * (g * g)
    m_hat = m / (1.0 - ADAM_B1 ** ADAM_STEP)
    v_hat = v / (1.0 - ADAM_B2 ** ADAM_STEP)
    delta = -ADAM_LR * (m_hat / (jnp.sqrt(v_hat) + ADAM_EPS) + ADAM_WD * w)
    return delta, m, v


def adamw(w, g, m, v, name):
    return rowwise(_adam_fn, [w, g, m, v], [F32, F32, F32], name)


def _full_weight(name, gathered, local_shape):
    L, a, b = local_shape
    g = gathered.reshape((4, L, a, b))
    if SHARD_AXIS[name] == 1:
        return g.transpose(1, 0, 2, 3).reshape(L, 4 * a, b)
    return g.transpose(1, 2, 0, 3).reshape(L, a, 4 * b)


def _grad_slots(name, dw):
    L, a, b = dw.shape
    if SHARD_AXIS[name] == 1:
        s = dw.reshape(L, 4, a // 4, b).transpose(1, 0, 2, 3)
        rows, cols = L * (a // 4), b
    else:
        s = dw.reshape(L, a, 4, b // 4).transpose(2, 0, 1, 3)
        rows, cols = L * a, b // 4
    return s.reshape(4, 2, rows // 2, cols).transpose(1, 0, 2, 3)


def _attn_a(T):
    group = A_HEADS // A_KV_HEADS
    return Band(T, 1, A_HEADS, group, group, A_HEADS, 0, A_KV_HEADS, 0, A_HEADS + 2 * A_KV_HEADS,
                A_HEADS + A_KV_HEADS, 1.0 / math.sqrt(HEAD_DIM), A_HALF_WINDOW, BAND_BLOCK)


def _attn_b(T):
    return Attn(T, 1, B_HEADS, 1, B_HEADS, 0, B_HEADS, 0, 2 * B_HEADS, 1, 2, B_PAD, 1.0 / math.sqrt(B_QK), None,
                DENSE_BLOCK)


def _attn_c(T, group):
    window, dil = C_PATTERNS[group]
    nblk = (C_GROUPS + 2) * C_HEADS
    return Band(T, dil, C_HEADS, 1, BAND_HEADS_PER_STEP, C_GROUPS * C_HEADS, group * C_HEADS, C_HEADS, 0, nblk,
                (C_GROUPS + 1) * C_HEADS, 1.0 / math.sqrt(HEAD_DIM), window // 2 // dil, BAND_BLOCK)


def _pad_heads(a, axis_len_true, axis_len_pad):
    lead = a.shape[:-1]
    h = a.shape[-1] // axis_len_true
    a = a.reshape(lead + (h, axis_len_true))
    a = jnp.pad(a, [(0, 0)] * len(lead) + [(0, 0), (0, axis_len_pad - axis_len_true)])
    return a.reshape(lead + (h * axis_len_pad,))


def _unpad_heads(a, axis_len_true, axis_len_pad):
    lead = a.shape[:-1]
    h = a.shape[-1] // axis_len_pad
    return a.reshape(lead + (h, axis_len_pad))[..., :axis_len_true].reshape(lead + (h * axis_len_true,))


def _weight_grad(G, name, layer, a, dy, W, tag):
    layers, rows, cols = W[name].shape
    G[name] = matmul([(a, dy)], "tn", F32, tag, slot=Slot(name, layers, layer, rows, cols, 0, G.get(name)))


def _mixer_fwd(kind, slot, hn, W, S, tabs, tag):
    T = hn.shape[0]
    if kind == 0:
        cfg = _attn_a(T)
        qkv = matmul([(hn, W["a_w_in"][slot])], "nn", BF, tag + "_a_in")
        q = headnorm_fwd(qkv, W["a_q_norm"][slot], tabs["hd"], tag + "_a_qn", A_HEADS, 0, HEAD_DIM, HEAD_DIM)
        k = headnorm_fwd(qkv, W["a_k_norm"][slot], tabs["hd"], tag + "_a_kn", A_KV_HEADS, A_HEADS, HEAD_DIM, HEAD_DIM)
        sink = jnp.broadcast_to(W["a_sink"][slot][:, None, None], (A_HEADS, SUBLANES, LANES)).astype(F32)
        o, lse = band_fwd(cfg, q, k, qkv, tag + "_a_att", BF, sink=sink)
        S.update(qkv=qkv, q=q, k=k, o=o, lse=lse, sink=sink)
        return o
    if kind == 1:
        cfg = _attn_b(T)
        lat = matmul([(hn, W["b_w_in"][slot])], "nn", BF, tag + "_b_in")
        qn = rmsnorm_fwd(lat, W["b_q_lat_norm"][slot], tag + "_b_qlat", 0, B_Q_RANK)
        kvn = rmsnorm_fwd(lat, W["b_kv_lat_norm"][slot], tag + "_b_kvlat", 1, B_KV_RANK)
        qp = matmul([(qn, W["b_w_q_up_pad"][slot])], "nn", BF, tag + "_b_qup")
        kv = matmul([(kvn, W["b_w_kv_up"][slot])], "nn", BF, tag + "_b_kvup")
        k_rope = lat[:, B_Q_RANK + B_KV_RANK:]
        kpre = jnp.concatenate(
            [kv.reshape(T, B_HEADS, 2 * B_NOPE)[:, :, :B_NOPE],
             jnp.broadcast_to(k_rope[:, None, :], (T, B_HEADS, B_ROPE)),
             jnp.zeros((T, B_HEADS, B_PAD - B_QK), BF)], axis=-1).reshape(T, B_HEADS * B_PAD)
        q = headnorm_fwd(qp, W["b_q_norm_pad"][slot], tabs["b"], tag + "_b_qn", B_HEADS, 0, B_PAD, B_QK)
        k = headnorm_fwd(kpre, W["b_k_norm_pad"][slot], tabs["b"], tag + "_b_kn", B_HEADS, 0, B_PAD, B_QK)
        o, lse = flash_fwd(cfg, q, k, kv, tag + "_b_att", BF)
        S.update(lat=lat, qn=qn, kvn=kvn, qp=qp, kv=kv, kpre=kpre, q=q, k=k, o=o, lse=lse)
        return o
    qkv = matmul([(hn, W["c_w_in"][slot])], "nn", BF, tag + "_c_in")
    nq = C_GROUPS * C_HEADS
    q = headnorm_fwd(qkv, W["c_q_norm"][slot], tabs["hd"], tag + "_c_qn", nq, 0, HEAD_DIM, HEAD_DIM)
    k = headnorm_fwd(qkv, W["c_k_norm"][slot], tabs["hd"], tag + "_c_kn", C_HEADS, nq, HEAD_DIM, HEAD_DIM)
    outs, lses = [], []
    for g in range(C_GROUPS):
        og, lg = band_fwd(_attn_c(T, g), q, k, qkv, f"{tag}_c_att{g}", F32)
        outs.append(og)
        lses.append(lg)
    o, lse = rowwise(_merge_fn, outs + lses, [BF, F32], tag + "_c_merge")
    S.update(qkv=qkv, q=q, k=k, o=o, lse=lse)
    return o


def _mixer_bwd(kind, slot, hn, do, W, S, tabs, tag, G):
    T = hn.shape[0]
    if kind == 0:
        cfg = _attn_a(T)
        qkv = S["qkv"]
        dq, dsink = band_dq(cfg, S["q"], S["k"], qkv, do, S["o"], S["lse"], tag + "_a_dq", sink=S["sink"])
        dk, dv = band_dkv(cfg, S["q"], S["k"], qkv, do, S["o"], S["lse"], tag + "_a_dkv", BF)
        dqp, dgq = headnorm_bwd(qkv, W["a_q_norm"][slot], tabs["hd"], dq, tag + "_a_dqn", A_HEADS, 0, HEAD_DIM, HEAD_DIM)
        dkp, dgk = headnorm_bwd(qkv, W["a_k_norm"][slot], tabs["hd"], dk, tag + "_a_dkn", A_KV_HEADS, A_HEADS,
                                HEAD_DIM, HEAD_DIM)
        dqkv = jnp.concatenate([dqp, dkp, dv], axis=1)
        _weight_grad(G, "a_w_in", slot, hn, dqkv, W, tag + "_a_dwin")
        G["a_q_norm"][slot], G["a_k_norm"][slot] = dgq, dgk
        parts = dsink.reshape(A_HEADS // cfg.per, cfg.nb, cfg.per, SUBLANES, LANES)[:, :, :, 0, 0]
        G["a_sink"][slot] = jnp.sum(parts, axis=1).reshape(A_HEADS)
        return matmul([(dqkv, W["a_w_in"][slot])], "nt", F32, tag + "_a_dhn")
    if kind == 1:
        cfg = _attn_b(T)
        kv = S["kv"]
        dq = flash_dq(cfg, S["q"], S["k"], kv, do, S["o"], S["lse"], tag + "_b_dq")
        dk, dv = flash_dkv(cfg, S["q"], S["k"], kv, do, S["o"], S["lse"], tag + "_b_dkv", BF)
        dqp, dgq = headnorm_bwd(S["qp"], W["b_q_norm_pad"][slot], tabs["b"], dq, tag + "_b_dqn", B_HEADS, 0, B_PAD, B_QK)
        dkp, dgk, dksum = headnorm_bwd(S["kpre"], W["b_k_norm_pad"][slot], tabs["b"], dk, tag + "_b_dkn", B_HEADS, 0,
                                       B_PAD, B_QK, head_sum=True)
        dkv = jnp.concatenate([dkp.reshape(T, B_HEADS, B_PAD)[:, :, :B_NOPE], dv.reshape(T, B_HEADS, LANES)],
                              axis=-1).reshape(T, B_HEADS * 2 * B_NOPE)
        _weight_grad(G, "b_w_kv_up", slot, S["kvn"], dkv, W, tag + "_b_dwkv")
        G["b_w_q_up"][slot] = _unpad_heads(matmul([(S["qn"], dqp)], "tn", F32, tag + "_b_dwq"), B_QK, B_PAD)
        dqn = matmul([(dqp, W["b_w_q_up_pad"][slot])], "nt", F32, tag + "_b_dqnorm")
        dkvn = matmul([(dkv, W["b_w_kv_up"][slot])], "nt", F32, tag + "_b_dkvnorm")
        dql, dg_q = rmsnorm_bwd(S["lat"], W["b_q_lat_norm"][slot], dqn, tag + "_b_dqlat", [BF], None, 0, B_Q_RANK)
        dkvl, dg_kv = rmsnorm_bwd(S["lat"], W["b_kv_lat_norm"][slot], dkvn, tag + "_b_dkvlat", [BF], None, 1, B_KV_RANK)
        dlat = jnp.concatenate([dql, dkvl, dksum[:, B_NOPE:B_QK].astype(BF)], axis=1)
        _weight_grad(G, "b_w_in", slot, hn, dlat, W, tag + "_b_dwin")
        G["b_q_norm"][slot], G["b_k_norm"][slot] = dgq[:B_QK], dgk[:B_QK]
        G["b_q_lat_norm"][slot], G["b_kv_lat_norm"][slot] = dg_q, dg_kv
        return matmul([(dlat, W["b_w_in"][slot])], "nt", F32, tag + "_b_dhn")
    qkv = S["qkv"]
    nq = C_GROUPS * C_HEADS
    dqs, acc = [], None
    for g in range(C_GROUPS):
        cfg = _attn_c(T, g)
        dqs.append(band_dq(cfg, S["q"], S["k"], qkv, do, S["o"], S["lse"], f"{tag}_c_dq{g}"))
        acc = band_dkv(cfg, S["q"], S["k"], qkv, do, S["o"], S["lse"], f"{tag}_c_dkv{g}", F32, add=acc)
    dk, dv = acc
    dq = jnp.concatenate(dqs, axis=1)
    dqp, dgq = headnorm_bwd(qkv, W["c_q_norm"][slot], tabs["hd"], dq, tag + "_c_dqn", nq, 0, HEAD_DIM, HEAD_DIM)
    dkp, dgk = headnorm_bwd(qkv, W["c_k_norm"][slot], tabs["hd"], dk, tag + "_c_dkn", C_HEADS, nq, HEAD_DIM, HEAD_DIM)
    dqkv = jnp.concatenate([dqp, dkp, dv.astype(BF)], axis=1)
    _weight_grad(G, "c_w_in", slot, hn, dqkv, W, tag + "_c_dwin")
    G["c_q_norm"][slot], G["c_k_norm"][slot] = dgq, dgk
    return matmul([(dqkv, W["c_w_in"][slot])], "nt", F32, tag + "_c_dhn")


MIXER_OUT = ("a_w_o", "b_w_o", "c_w_o")


def local_step(x, p, positions, loss_target, W):
    T = x.shape[0]
    tabs = {"hd": rope_tables(positions, HEAD_DIM, 0, PARTIAL_ROT), "b": rope_tables(positions, B_PAD, B_NOPE, B_ROPE)}
    W = dict(W)
    W["b_w_q_up_pad"] = _pad_heads(W["b_w_q_up"], B_QK, B_PAD)
    W["b_q_norm_pad"] = _pad_heads(W["b_q_norm"], B_QK, B_PAD)
    W["b_k_norm_pad"] = _pad_heads(W["b_k_norm"], B_QK, B_PAD)
    saved = []
    h = x
    for i in range(DEPTH):
        kind, slot = i % 3, i // 3
        tag = f"l{i}"
        S = {"h0": h}
        hn = rmsnorm_fwd(h, W["g_mix"][i], tag + "_mixnorm")
        o = _mixer_fwd(kind, slot, hn, W, S, tabs, tag)
        h1 = matmul([(o, W[MIXER_OUT[kind]][slot])], "nn", F32, tag + "_mixout", res=h)
        hn2 = rmsnorm_fwd(h1, W["g_ffn"][i], tag + "_ffnnorm")
        a, b, c = matmul_swiglu(hn2, W["w_ffn_gate"][i], W["w_ffn_up"][i], tag + "_gateup")
        h2 = matmul([(c, W["w_ffn_down"][i])], "nn", F32, tag + "_down", res=h1)
        hn3 = rmsnorm_fwd(h2, W["g_ple"][i], tag + "_plenorm")
        p_i = p[i].astype(BF)
        pp = matmul([(p_i, W["w_ple_proj"][i])], "nn", BF, tag + "_pleproj")
        z, h3 = matmul([(hn3, W["w_ple_gate"][i])], "nn", BF, tag + "_plegate", ple=(h2, pp))
        S.update(hn=hn, h1=h1, hn2=hn2, a=a, b=b, c=c, h2=h2, hn3=hn3, z=z, pp=pp, p=p_i)
        saved.append(S)
        h = h3

    loss, dh = loss_and_grad(h, loss_target, "loss")
    G = {n: [None] * W[n].shape[0] for n in SMALL + ("b_w_q_up",)}
    for i in reversed(range(DEPTH)):
        kind, slot = i % 3, i // 3
        tag = f"l{i}"
        S = saved[i]
        dz, dpp = rowwise(_ple_bwd_fn, [dh, S["z"], S["pp"]], [BF, BF], tag + "_dple")
        _weight_grad(G, "w_ple_proj", i, S["p"], dpp, W, tag + "_dwpleproj")
        _weight_grad(G, "w_ple_gate", i, S["hn3"], dz, W, tag + "_dwplegate")
        dhn3 = matmul([(dz, W["w_ple_gate"][i])], "nt", F32, tag + "_dplenorm")
        dh2, dh2b, G["g_ple"][i] = rmsnorm_bwd(S["h2"], W["g_ple"][i], dhn3, tag + "_dple_norm", [F32, BF], dres=dh)
        da, db = matmul([(dh2b, W["w_ffn_down"][i])], "nt", BF, tag + "_dswiglu", swiglu=(S["a"], S["b"]))
        _weight_grad(G, "w_ffn_down", i, S["c"], dh2b, W, tag + "_dwdown")
        _weight_grad(G, "w_ffn_gate", i, S["hn2"], da, W, tag + "_dwgate")
        _weight_grad(G, "w_ffn_up", i, S["hn2"], db, W, tag + "_dwup")
        dhn2 = matmul([(da, W["w_ffn_gate"][i]), (db, W["w_ffn_up"][i])], "nt", F32, tag + "_dffnnorm")
        dh1, dh1b, G["g_ffn"][i] = rmsnorm_bwd(S["h1"], W["g_ffn"][i], dhn2, tag + "_dffn_norm", [F32, BF], dres=dh2)
        wo = W[MIXER_OUT[kind]][slot]
        do = matmul([(dh1b, wo)], "nt", BF, tag + "_dmixout")
        _weight_grad(G, MIXER_OUT[kind], slot, S["o"], dh1b, W, tag + "_dwmixout")
        dhn = _mixer_bwd(kind, slot, S["hn"], do, W, S, tabs, tag, G)
        dh, G["g_mix"][i] = rmsnorm_bwd(S["h0"], W["g_mix"][i], dhn, tag + "_dmix_norm", [F32], dres=dh1)
    return loss, dh, G


def _pack_small(vals):
    flat = jnp.concatenate([vals[n].reshape(-1).astype(F32) for n in SMALL])
    rows = -(-flat.shape[0] // SMALL_COLS)
    rows = -(-rows // SUBLANES) * SUBLANES
    return jnp.pad(flat, (0, rows * SMALL_COLS - flat.shape[0])).reshape(rows, SMALL_COLS)


def _unpack_small(packed, like):
    flat = packed.reshape(-1)
    out, off = {}, 0
    for n in SMALL:
        size = like[n].size
        out[n] = flat[off:off + size].reshape(like[n].shape)
        off += size
    return out


def kernel(x, p, positions, g_mix, g_ffn, g_ple, w_ple_gate, w_ple_proj, w_ffn_gate, w_ffn_up, w_ffn_down, a_w_in, a_q_norm, a_k_norm, a_sink, a_w_o, b_w_in, b_q_lat_norm, b_kv_lat_norm, b_w_q_up, b_w_kv_up, b_q_norm, b_k_norm, b_w_o, c_w_in, c_q_norm, c_k_norm, c_w_o, loss_target, m_g_mix, m_g_ffn, m_g_ple, m_w_ple_gate, m_w_ple_proj, m_w_ffn_gate, m_w_ffn_up, m_w_ffn_down, m_a_w_in, m_a_q_norm, m_a_k_norm, m_a_sink, m_a_w_o, m_b_w_in, m_b_q_lat_norm, m_b_kv_lat_norm, m_b_w_q_up, m_b_w_kv_up, m_b_q_norm, m_b_k_norm, m_b_w_o, m_c_w_in, m_c_q_norm, m_c_k_norm, m_c_w_o, v_g_mix, v_g_ffn, v_g_ple, v_w_ple_gate, v_w_ple_proj, v_w_ffn_gate, v_w_ffn_up, v_w_ffn_down, v_a_w_in, v_a_q_norm, v_a_k_norm, v_a_sink, v_a_w_o, v_b_w_in, v_b_q_lat_norm, v_b_kv_lat_norm, v_b_w_q_up, v_b_w_kv_up, v_b_q_norm, v_b_k_norm, v_b_w_o, v_c_w_in, v_c_q_norm, v_c_k_norm, v_c_w_o):
    args = dict(locals())
    w_loc = {n: args[n] for n in WEIGHTS}
    m_loc = {n: args["m_" + n] for n in WEIGHTS}
    v_loc = {n: args["v_" + n] for n in WEIGHTS}

    def halves(a):
        rows = a.shape[0] * a.shape[1]
        return a.reshape(2, rows // 2, a.shape[2])

    gathered = gather_weights([halves(w_loc[n].astype(BF)) for n in BIG])
    W = {n: _full_weight(n, g, w_loc[n].shape) for n, g in zip(BIG, gathered)}
    for n in SMALL:
        W[n] = w_loc[n]

    loss, dx, G = local_step(x[0], p[:, 0], positions[0], loss_target[0], W)
    loss = lax.psum(loss, ("x", "y", "c"))

    core = _core_index()
    parts = []
    for n in BIG:
        s = _grad_slots(n, jnp.stack(G[n])) if isinstance(G[n], list) else G[n]
        part = presum_core_halves(s.reshape(2, 4 * s.shape[2], s.shape[3]), core, "presum_" + n)
        parts.append(part.reshape(s.shape[1:]))
    landed = scatter_partials(parts)
    halves = [sum_and_swap(a, "sum_" + n) for n, a in zip(BIG, landed)]

    small = gather_small(_pack_small({n: jnp.stack(G[n]) for n in SMALL}))
    small_sum = sum_slots(small, F32, "sum_small")
    grads = _unpack_small(small_sum, w_loc)

    delta, new_m, new_v = {}, {}, {}
    for n, (mine, theirs) in zip(BIG, halves):
        shape = w_loc[n].shape
        two_d = (shape[0] * shape[1], shape[2])
        g, d, m, v = adamw_halves(w_loc[n].reshape(two_d), mine, theirs, m_loc[n].reshape(two_d),
                                  v_loc[n].reshape(two_d), core, "adamw_" + n)
        grads[n], delta[n], new_m[n], new_v[n] = g.reshape(shape), d.reshape(shape), m.reshape(shape), v.reshape(shape)
    d, m, v = adamw(_pack_small(w_loc), small_sum, _pack_small(m_loc), _pack_small(v_loc), "adamw_small")
    delta.update(_unpack_small(d, w_loc))
    new_m.update(_unpack_small(m, w_loc))
    new_v.update(_unpack_small(v, w_loc))

    return (loss, dx[None], *[grads[n] for n in WEIGHTS], *[delta[n] for n in WEIGHTS],
            *[new_m[n] for n in WEIGHTS], *[new_v[n] for n in WEIGHTS])
```

```python
import functools
import math

import numpy as np
import jax
import jax.numpy as jnp
from jax import lax
from jax.experimental import pallas as pl
from jax.experimental.pallas import tpu as pltpu

F32 = jnp.float32
BF = jnp.bfloat16

D_MODEL = 2048
DEPTH = 4
HEAD_DIM = 128
ROPE_THETA = 500000.0
PARTIAL_ROT = HEAD_DIM // 4
NORM_EPS = 1e-6
NEG = -1e30
A_HEADS = 16
A_KV_HEADS = 4
A_HALF_WINDOW = 128
B_HEADS = 16
B_Q_RANK = 512
B_KV_RANK = 512
B_NOPE = 128
B_ROPE = 64
B_QK = B_NOPE + B_ROPE
B_PAD = 256
C_PATTERNS = ((128, 1), (512, 4), (2048, 16))
C_HEADS = 16
C_GROUPS = 3
ADAM_LR = 0.001
ADAM_B1 = 0.9
ADAM_B2 = 0.999
ADAM_EPS = 1e-08
ADAM_WD = 0.01
ADAM_STEP = 10

LANES = 128
SUBLANES = 8
VMEM_LIMIT_BYTES = 56 * 1024 * 1024
BAND_BLOCK = 256
BAND_HEADS_PER_STEP = 4
DENSE_BLOCK = 1024
DENSE_SUB = 256
MESH = pl.DeviceIdType.MESH

BIG = ("w_ple_gate", "w_ple_proj", "w_ffn_gate", "w_ffn_up", "w_ffn_down", "a_w_in", "a_w_o",
       "b_w_in", "b_w_q_up", "b_w_kv_up", "b_w_o", "c_w_in", "c_w_o")
SHARD_AXIS = {"w_ple_gate": 1, "w_ple_proj": 2, "w_ffn_gate": 2, "w_ffn_up": 2, "w_ffn_down": 1,
              "a_w_in": 2, "a_w_o": 1, "b_w_in": 1, "b_w_q_up": 2, "b_w_kv_up": 2, "b_w_o": 1,
              "c_w_in": 2, "c_w_o": 1}
SMALL = ("g_mix", "g_ffn", "g_ple", "a_q_norm", "a_k_norm", "a_sink", "b_q_lat_norm",
         "b_kv_lat_norm", "b_q_norm", "b_k_norm", "c_q_norm", "c_k_norm")
WEIGHTS = ("g_mix", "g_ffn", "g_ple", "w_ple_gate", "w_ple_proj", "w_ffn_gate", "w_ffn_up",
           "w_ffn_down", "a_w_in", "a_q_norm", "a_k_norm", "a_sink", "a_w_o", "b_w_in",
           "b_q_lat_norm", "b_kv_lat_norm", "b_w_q_up", "b_w_kv_up", "b_q_norm", "b_k_norm",
           "b_w_o", "c_w_in", "c_q_norm", "c_k_norm", "c_w_o")
SMALL_COLS = 1024
SLOT_DIRECT = ("w_ple_gate", "w_ple_proj", "w_ffn_gate", "w_ffn_up", "w_ffn_down")


def _params(semantics):
    return pltpu.CompilerParams(dimension_semantics=semantics, vmem_limit_bytes=VMEM_LIMIT_BYTES)


def _tile(dim, cands=(1024, 1408, 512, 256, 128)):
    for c in cands:
        if dim % c == 0:
            return c
    return dim


def _row_tile(rows, cols, target_elems=1 << 19):
    best = None
    for t in range(16, rows + 1, 16):
        if rows % t == 0 and t * cols <= target_elems:
            best = t
    return best if best is not None else rows


def _sigmoid(x):
    return 1.0 / (1.0 + jnp.exp(-x))


class Slot:
    def __init__(self, name, layers, layer, rows, cols, col0=0, buf=None):
        self.axis, self.layers, self.layer, self.rows, self.cols, self.col0, self.buf = (
            SHARD_AXIS[name], layers, layer, rows, cols, col0, buf)
        self.srows = rows // 4 if self.axis == 1 else rows
        self.scols = cols if self.axis == 1 else cols // 4
        self.half = layers * self.srows // 2

    def tiles(self, ncols):
        tm = _tile(math.gcd(self.srows, self.half))
        tn = _tile(math.gcd(self.scols, math.gcd(self.col0, ncols)))
        return tm, tn

    def spec(self, tm, tn):
        def index(i, j, k):
            row, col = i * tm, self.col0 + j * tn
            chip = row // self.srows if self.axis == 1 else col // self.scols
            flat = self.layer * self.srows + (row % self.srows if self.axis == 1 else row)
            cb = col // tn if self.axis == 1 else (col % self.scols) // tn
            return flat // self.half, chip, (flat % self.half) // tm, cb

        return pl.BlockSpec((None, None, tm, tn), index)

    def shape(self):
        return jax.ShapeDtypeStruct((2, 4, self.half, self.scols), F32)


def matmul(pairs, mode, out_dtype, name, res=None, swiglu=None, ple=None, slot=None):
    a0, b0 = pairs[0]
    if mode == "nn":
        (M, K), N = a0.shape, b0.shape[1]
    elif mode == "nt":
        (M, K), N = a0.shape, b0.shape[0]
    else:
        (K, M), N = a0.shape, b0.shape[1]
    tm, tn, tk = _tile(M), _tile(N), _tile(K, (1024, 512, 256, 128))
    if slot is not None:
        tm, tn = slot.tiles(N)
    nk = K // tk
    if mode == "nn":
        a_spec = pl.BlockSpec((tm, tk), lambda i, j, k: (i, k))
        b_spec = pl.BlockSpec((tk, tn), lambda i, j, k: (k, j))
        dims = (((1,), (0,)), ((), ()))
    elif mode == "nt":
        a_spec = pl.BlockSpec((tm, tk), lambda i, j, k: (i, k))
        b_spec = pl.BlockSpec((tn, tk), lambda i, j, k: (j, k))
        dims = (((1,), (1,)), ((), ()))
    else:
        a_spec = pl.BlockSpec((tk, tm), lambda i, j, k: (k, i))
        b_spec = pl.BlockSpec((tk, tn), lambda i, j, k: (k, j))
        dims = (((0,), (0,)), ((), ()))
    mn_spec = pl.BlockSpec((tm, tn), lambda i, j, k: (i, j))
    npairs = len(pairs)
    extras = [] if res is None else [res]
    if swiglu is not None:
        extras = list(swiglu)
    if ple is not None:
        extras = list(ple)
    nex = len(extras)
    nout = 2 if (swiglu is not None or ple is not None) else 1
    carried = slot is not None and slot.buf is not None

    def body(*refs):
        ins = refs[:2 * npairs]
        ex = refs[2 * npairs:2 * npairs + nex]
        first_out = 2 * npairs + nex + (1 if carried else 0)
        outs = refs[first_out:first_out + nout]
        acc = refs[-1]
        k = pl.program_id(2)

        @pl.when(k == 0)
        def _():
            acc[...] = jnp.zeros_like(acc)

        part = None
        for p in range(npairs):
            d = lax.dot_general(ins[2 * p][...].astype(BF), ins[2 * p + 1][...].astype(BF), dims,
                                preferred_element_type=F32)
            part = d if part is None else part + d
        acc[...] += part

        @pl.when(k == nk - 1)
        def _():
            r = acc[...]
            if swiglu is not None:
                a = ex[0][...].astype(F32)
                b = ex[1][...].astype(F32)
                sg = _sigmoid(a)
                outs[0][...] = (r * b * (sg * (1.0 + a * (1.0 - sg)))).astype(out_dtype)
                outs[1][...] = (r * (a * sg)).astype(out_dtype)
            elif ple is not None:
                outs[0][...] = r.astype(out_dtype)
                outs[1][...] = ex[0][...] + _sigmoid(r) * ex[1][...].astype(F32)
            elif res is not None:
                outs[0][...] = (ex[0][...] + r).astype(out_dtype)
            else:
                outs[0][...] = r.astype(outs[0].dtype)

    in_specs = []
    operands = []
    for a, b in pairs:
        in_specs += [a_spec, b_spec]
        operands += [a, b]
    in_specs += [mn_spec] * nex
    operands += extras
    out_shape = [jax.ShapeDtypeStruct((M, N), out_dtype)] * nout
    out_specs = [mn_spec] * nout
    aliases = {}
    if ple is not None:
        out_shape[1] = jax.ShapeDtypeStruct((M, N), F32)
    if slot is not None:
        out_shape, out_specs = [slot.shape()], [slot.spec(tm, tn)]
        if carried:
            aliases = {len(operands): 0}
            in_specs.append(pl.BlockSpec(memory_space=pl.ANY))
            operands.append(slot.buf)
    outs = pl.pallas_call(
        body, out_shape=out_shape, grid=(M // tm, N // tn, nk), in_specs=in_specs,
        out_specs=out_specs, scratch_shapes=[pltpu.VMEM((tm, tn), F32)], input_output_aliases=aliases,
        compiler_params=_params(("parallel", "parallel", "arbitrary")), name=name,
    )(*operands)
    return outs if nout > 1 else outs[0]


def matmul_swiglu(x, wg, wu, name):
    (M, K), N = x.shape, wg.shape[1]
    tm, tn, tk = _tile(M), _tile(N), _tile(K, (1024, 512, 256, 128))
    nk = K // tk

    def body(x_ref, g_ref, u_ref, a_ref, b_ref, c_ref, acc_g, acc_u):
        k = pl.program_id(2)

        @pl.when(k == 0)
        def _():
            acc_g[...] = jnp.zeros_like(acc_g)
            acc_u[...] = jnp.zeros_like(acc_u)

        xv = x_ref[...].astype(BF)
        acc_g[...] += jnp.dot(xv, g_ref[...].astype(BF), preferred_element_type=F32)
        acc_u[...] += jnp.dot(xv, u_ref[...].astype(BF), preferred_element_type=F32)

        @pl.when(k == nk - 1)
        def _():
            a, b = acc_g[...], acc_u[...]
            a_ref[...] = a.astype(a_ref.dtype)
            b_ref[...] = b.astype(b_ref.dtype)
            c_ref[...] = (a * _sigmoid(a) * b).astype(c_ref.dtype)

    w_spec = pl.BlockSpec((tk, tn), lambda i, j, k: (k, j))
    mn_spec = pl.BlockSpec((tm, tn), lambda i, j, k: (i, j))
    return pl.pallas_call(
        body, out_shape=[jax.ShapeDtypeStruct((M, N), BF)] * 3, grid=(M // tm, N // tn, nk),
        in_specs=[pl.BlockSpec((tm, tk), lambda i, j, k: (i, k)), w_spec, w_spec], out_specs=[mn_spec] * 3,
        scratch_shapes=[pltpu.VMEM((tm, tn), F32)] * 2,
        compiler_params=_params(("parallel", "parallel", "arbitrary")), name=name,
    )(x, wg, wu)


def rowwise(fn, ins, out_dtypes, name):
    rows, cols = ins[0].shape
    tr = _row_tile(rows, cols)
    nin = len(ins)

    def body(*refs):
        vals = fn(*[r[...] for r in refs[:nin]])
        for o, v in zip(refs[nin:], vals):
            o[...] = v.astype(o.dtype)

    spec = pl.BlockSpec((tr, cols), lambda i: (i, 0))
    outs = pl.pallas_call(
        body, out_shape=[jax.ShapeDtypeStruct((rows, cols), d) for d in out_dtypes],
        grid=(rows // tr,), in_specs=[spec] * nin, out_specs=[spec] * len(out_dtypes),
        compiler_params=_params(("parallel",)), name=name,
    )(*ins)
    return outs


def _ple_bwd_fn(dh, z, pp):
    gate = _sigmoid(z.astype(F32))
    return (dh * pp.astype(F32) * gate * (1.0 - gate), dh * gate)


def _merge_fn(o0, o1, o2, l0, l1, l2):
    m = jnp.maximum(jnp.maximum(l0, l1), l2)
    e0, e1, e2 = jnp.exp(l0 - m), jnp.exp(l1 - m), jnp.exp(l2 - m)
    den = e0 + e1 + e2
    return ((e0 * o0 + e1 * o1 + e2 * o2) / den, m + jnp.log(den))


def rmsnorm_fwd(x, g, name, col_block=0, width=None):
    T = x.shape[0]
    W = x.shape[1] if width is None else width
    tt = _row_tile(T, W)

    def body(x_ref, g_ref, y_ref):
        xf = x_ref[...].astype(F32)
        ms = jnp.mean(xf * xf, axis=-1, keepdims=True)
        y_ref[...] = (xf * lax.rsqrt(ms + NORM_EPS) * g_ref[...]).astype(y_ref.dtype)

    return pl.pallas_call(
        body, out_shape=jax.ShapeDtypeStruct((T, W), BF), grid=(T // tt,),
        in_specs=[pl.BlockSpec((tt, W), lambda i: (i, col_block)), pl.BlockSpec((1, W), lambda i: (0, 0))],
        out_specs=pl.BlockSpec((tt, W), lambda i: (i, 0)),
        compiler_params=_params(("parallel",)), name=name,
    )(x, g.reshape(1, W).astype(F32))


def rmsnorm_bwd(x, g, dy, name, out_dtypes, dres=None, col_block=0, width=None):
    T = x.shape[0]
    W = x.shape[1] if width is None else width
    tt = _row_tile(T, W, 1 << 18)
    nout = len(out_dtypes)
    has_res = dres is not None

    def body(*refs):
        x_ref, g_ref, dy_ref = refs[:3]
        pos = 3
        res_ref = None
        if has_res:
            res_ref = refs[3]
            pos = 4
        dx_refs = refs[pos:pos + nout]
        dg_ref = refs[pos + nout]
        xf = x_ref[...].astype(F32)
        rstd = lax.rsqrt(jnp.mean(xf * xf, axis=-1, keepdims=True) + NORM_EPS)
        xhat = xf * rstd
        dyf = dy_ref[...].astype(F32)
        dn = dyf * g_ref[...]
        dx = rstd * (dn - xhat * jnp.mean(dn * xhat, axis=-1, keepdims=True))
        if has_res:
            dx = dx + res_ref[...]
        for o in dx_refs:
            o[...] = dx.astype(o.dtype)

        @pl.when(pl.program_id(0) == 0)
        def _():
            dg_ref[...] = jnp.zeros_like(dg_ref)

        dg_ref[...] += jnp.broadcast_to(jnp.sum(dyf * xhat, axis=0, keepdims=True), dg_ref.shape)

    row = pl.BlockSpec((tt, W), lambda i: (i, 0))
    in_specs = [pl.BlockSpec((tt, W), lambda i: (i, col_block)), pl.BlockSpec((1, W), lambda i: (0, 0)), row]
    operands = [x, g.reshape(1, W).astype(F32), dy]
    if has_res:
        in_specs.append(row)
        operands.append(dres)
    outs = pl.pallas_call(
        body,
        out_shape=[jax.ShapeDtypeStruct((T, W), d) for d in out_dtypes] + [jax.ShapeDtypeStruct((SUBLANES, W), F32)],
        grid=(T // tt,), in_specs=in_specs,
        out_specs=[row] * nout + [pl.BlockSpec((SUBLANES, W), lambda i: (0, 0))],
        compiler_params=_params(("arbitrary",)), name=name,
    )(*operands)
    return tuple(outs[:nout]) + (outs[nout][0],)


def loss_and_grad(y, target, name):
    T, D = y.shape
    tt = _row_tile(T, D)

    def body(y_ref, t_ref, loss_ref, dy_ref):
        d = y_ref[...] - t_ref[...]
        dy_ref[...] = d * (1.0 / D)

        @pl.when(pl.program_id(0) == 0)
        def _():
            loss_ref[...] = jnp.zeros_like(loss_ref)

        loss_ref[...] += jnp.full(loss_ref.shape, 0.5 / D, F32) * jnp.sum(d * d)

    row = pl.BlockSpec((tt, D), lambda i: (i, 0))
    loss, dy = pl.pallas_call(
        body, out_shape=[jax.ShapeDtypeStruct((SUBLANES, LANES), F32), jax.ShapeDtypeStruct((T, D), F32)],
        grid=(T // tt,), in_specs=[row, row],
        out_specs=[pl.BlockSpec((SUBLANES, LANES), lambda i: (0, 0)), row],
        compiler_params=_params(("arbitrary",)), name=name,
    )(y, target)
    return loss[0, 0], dy


def rope_tables(pos, width, r0, rot_dim):
    half = rot_dim // 2
    inv = ROPE_THETA ** (-jnp.arange(half, dtype=F32) * 2.0 / rot_dim)
    ang = pos.astype(F32)[:, None] * inv
    cos, sin = jnp.cos(ang), jnp.sin(ang)
    T = pos.shape[0]
    ones_l, ones_r = jnp.ones((T, r0), F32), jnp.ones((T, width - r0 - rot_dim), F32)
    c_tab = jnp.concatenate([ones_l, cos, cos, ones_r], axis=1)
    s_tab = jnp.concatenate([0 * ones_l, -sin, sin, 0 * ones_r], axis=1)
    perm = np.zeros((width, width), np.float32)
    for j in range(half):
        perm[r0 + j + half, r0 + j] = 1.0
        perm[r0 + j, r0 + j + half] = 1.0
    return c_tab, s_tab, jnp.asarray(perm, BF)


def _lane_permute(v, perm):
    hi = v.astype(BF)
    lo = (v - hi.astype(F32)).astype(BF)
    return (jnp.dot(hi, perm, preferred_element_type=F32) + jnp.dot(lo, perm, preferred_element_type=F32))


def headnorm_fwd(x, g, tabs, name, heads, col0, width, n_true):
    c_tab, s_tab, perm = tabs
    T = x.shape[0]
    tt = _tile(T, (1024, 512, 256, 128))
    inv_n = 1.0 / n_true

    def body(x_ref, g_ref, c_ref, s_ref, p_ref, y_ref):
        xf = x_ref[...].astype(F32)
        rstd = lax.rsqrt(jnp.sum(xf * xf, axis=-1, keepdims=True) * inv_n + NORM_EPS)
        n = xf * rstd * g_ref[...]
        y_ref[...] = (n * c_ref[...] + _lane_permute(n, p_ref[...]) * s_ref[...]).astype(y_ref.dtype)

    tab = pl.BlockSpec((tt, width), lambda i, h: (i, 0))
    return pl.pallas_call(
        body, out_shape=jax.ShapeDtypeStruct((T, heads * width), BF), grid=(T // tt, heads),
        in_specs=[pl.BlockSpec((tt, width), lambda i, h: (i, col0 + h)),
                  pl.BlockSpec((1, width), lambda i, h: (0, 0)), tab, tab,
                  pl.BlockSpec((width, width), lambda i, h: (0, 0))],
        out_specs=pl.BlockSpec((tt, width), lambda i, h: (i, h)),
        compiler_params=_params(("parallel", "parallel")), name=name,
    )(x, g.reshape(1, width).astype(F32), c_tab, s_tab, perm)


def headnorm_bwd(x, g, tabs, dy, name, heads, col0, width, n_true, head_sum=False, into=None):
    c_tab, s_tab, perm = tabs
    T = x.shape[0]
    tt = _tile(T, (1024, 512, 256, 128))
    inv_n = 1.0 / n_true
    buf, blocks, block0 = into if into is not None else (None, heads, 0)
    carried = buf is not None

    def body(*refs):
        x_ref, g_ref, c_ref, s_ref, p_ref, dy_ref = refs[:6]
        dx_ref, dg_ref = refs[7:9] if carried else refs[6:8]
        i, h = pl.program_id(0), pl.program_id(1)
        xf = x_ref[...].astype(F32)
        rstd = lax.rsqrt(jnp.sum(xf * xf, axis=-1, keepdims=True) * inv_n + NORM_EPS)
        xhat = xf * rstd
        dyf = dy_ref[...].astype(F32)
        dn = dyf * c_ref[...] + _lane_permute(dyf * s_ref[...], p_ref[...])
        dxh = dn * g_ref[...]
        dx = rstd * (dxh - xhat * (jnp.sum(dxh * xhat, axis=-1, keepdims=True) * inv_n))
        dx_ref[...] = dx.astype(dx_ref.dtype)

        @pl.when(jnp.logical_and(i == 0, h == 0))
        def _():
            dg_ref[...] = jnp.zeros_like(dg_ref)

        dg_ref[...] += jnp.broadcast_to(jnp.sum(dn * xhat, axis=0, keepdims=True), dg_ref.shape)
        if head_sum:
            sum_ref = refs[-1]

            @pl.when(h == 0)
            def _():
                sum_ref[...] = jnp.zeros_like(sum_ref)

            sum_ref[...] += dx

    tab = pl.BlockSpec((tt, width), lambda i, h: (i, 0))
    out_shape = [jax.ShapeDtypeStruct((T, blocks * width), BF), jax.ShapeDtypeStruct((SUBLANES, width), F32)]
    out_specs = [pl.BlockSpec((tt, width), lambda i, h: (i, block0 + h)),
                 pl.BlockSpec((SUBLANES, width), lambda i, h: (0, 0))]
    if head_sum:
        out_shape.append(jax.ShapeDtypeStruct((T, width), F32))
        out_specs.append(tab)
    in_specs = [pl.BlockSpec((tt, width), lambda i, h: (i, col0 + h)),
                pl.BlockSpec((1, width), lambda i, h: (0, 0)), tab, tab,
                pl.BlockSpec((width, width), lambda i, h: (0, 0)),
                pl.BlockSpec((tt, width), lambda i, h: (i, h))]
    operands = [x, g.reshape(1, width).astype(F32), c_tab, s_tab, perm, dy]
    if carried:
        in_specs.append(pl.BlockSpec(memory_space=pl.ANY))
        operands.append(buf)
    outs = pl.pallas_call(
        body, out_shape=out_shape, grid=(T // tt, heads), in_specs=in_specs, out_specs=out_specs,
        input_output_aliases={6: 0} if carried else {},
        compiler_params=_params(("arbitrary", "arbitrary")), name=name,
    )(*operands)
    return (outs[0], outs[1][0]) + ((outs[2],) if head_sum else ())


class Attn:
    def __init__(self, T, dil, hq, group, qc, q0, kc, k0, vc, v0, vstride, dqk, scale, half_window, blk):
        self.T, self.dil, self.hq, self.group = T, dil, hq, group
        self.hkv = hq // group
        self.qc, self.q0, self.kc, self.k0, self.vc, self.v0, self.vstride = qc, q0, kc, k0, vc, v0, vstride
        self.dqk, self.scale, self.hw = dqk, scale, half_window
        self.len = T // dil
        self.blk = min(blk, self.len)
        self.nb = self.len // self.blk
        self.band = half_window is not None
        self.steps = 3 if self.band else self.nb

    def other(self, i, s):
        if self.band:
            nom = i - 1 + s
            return jnp.minimum(jnp.maximum(nom, 0), self.nb - 1), nom
        return s, s

    def chains(self, a):
        return a.reshape(self.len, self.dil * a.shape[1])

    def row_chunks(self):
        assert not self.band
        sub = min(DENSE_SUB, self.blk)
        return [slice(c * sub, (c + 1) * sub) for c in range(self.blk // sub)]

    def unchain(self, a, cols):
        return a.reshape(self.T, cols)

    def mask(self, q_nom, k_nom):
        if not self.band:
            return None
        qpos = q_nom * self.blk + lax.broadcasted_iota(jnp.int32, (self.blk, self.blk), 0)
        kpos = k_nom * self.blk + lax.broadcasted_iota(jnp.int32, (self.blk, self.blk), 1)
        ok = jnp.abs(qpos - kpos) <= self.hw
        for pos in (qpos, kpos):
            ok = jnp.logical_and(ok, jnp.logical_and(pos >= 0, pos < self.len))
        return ok


def _scores(cfg, q, k, q_nom, k_nom):
    s = lax.dot_general(q, k, (((1,), (1,)), ((), ())), preferred_element_type=F32) * cfg.scale
    ok = cfg.mask(q_nom, k_nom)
    return s if ok is None else jnp.where(ok, s, NEG)


def flash_fwd(cfg, q, k, v, name, out_dtype, sink=None):
    blk, dqk = cfg.blk, cfg.dqk
    has_sink = sink is not None

    def body(*refs):
        if has_sink:
            sink_ref, refs = refs[0], refs[1:]
        q_ref, k_ref, v_ref, o_ref, lse_ref, m_sc, l_sc, acc_sc = refs
        i, s = pl.program_id(2), pl.program_id(3)

        @pl.when(s == 0)
        def _():
            if has_sink:
                m_sc[...] = jnp.broadcast_to(sink_ref[0, :1, :], m_sc.shape)
                l_sc[...] = jnp.ones_like(l_sc)
            else:
                m_sc[...] = jnp.full(m_sc.shape, NEG, F32)
                l_sc[...] = jnp.zeros_like(l_sc)
            acc_sc[...] = jnp.zeros_like(acc_sc)

        _, k_nom = cfg.other(i, s)
        k, v = k_ref[...], v_ref[...]
        for rows in cfg.row_chunks():
            sc = _scores(cfg, q_ref[rows, :], k, i, k_nom)
            m_prev = m_sc[rows, :]
            m_new = jnp.maximum(m_prev, jnp.max(sc, axis=-1, keepdims=True))
            p = jnp.exp(sc - m_new[:, :1])
            alpha = jnp.exp(m_prev - m_new)
            l_sc[rows, :] = alpha * l_sc[rows, :] + jnp.sum(p, axis=-1, keepdims=True)
            acc_sc[rows, :] = alpha * acc_sc[rows, :] + jnp.dot(p.astype(BF), v, preferred_element_type=F32)
            m_sc[rows, :] = m_new

        @pl.when(s == cfg.steps - 1)
        def _():
            o_ref[...] = (acc_sc[...] / l_sc[...]).astype(o_ref.dtype)
            lse_ref[...] = m_sc[...] + jnp.log(l_sc[...])

    g = cfg.group
    q_spec = pl.BlockSpec((blk, dqk), lambda r, h, i, s: (i, r * cfg.qc + cfg.q0 + h))
    k_spec = pl.BlockSpec((blk, dqk), lambda r, h, i, s: (cfg.other(i, s)[0], r * cfg.kc + cfg.k0 + h // g))
    v_spec = pl.BlockSpec((blk, LANES),
                          lambda r, h, i, s: (cfg.other(i, s)[0], r * cfg.vc + cfg.v0 + cfg.vstride * (h // g)))
    o_spec = pl.BlockSpec((blk, LANES), lambda r, h, i, s: (i, r * cfg.hq + h))
    in_specs = [q_spec, k_spec, v_spec]
    operands = [cfg.chains(q), cfg.chains(k), cfg.chains(v)]
    if has_sink:
        in_specs.insert(0, pl.BlockSpec((1, SUBLANES, LANES), lambda r, h, i, s: (h, 0, 0)))
        operands.insert(0, sink)
    cols = cfg.dil * cfg.hq * LANES
    o, lse = pl.pallas_call(
        body, out_shape=[jax.ShapeDtypeStruct((cfg.len, cols), out_dtype), jax.ShapeDtypeStruct((cfg.len, cols), F32)],
        grid=(cfg.dil, cfg.hq, cfg.nb, cfg.steps), in_specs=in_specs, out_specs=[o_spec, o_spec],
        scratch_shapes=[pltpu.VMEM((blk, LANES), F32)] * 3,
        compiler_params=_params(("parallel", "parallel", "parallel", "arbitrary")), name=name,
    )(*operands)
    return cfg.unchain(o, cfg.hq * LANES), cfg.unchain(lse, cfg.hq * LANES)


def flash_dq(cfg, q, k, v, do, o, lse, name, sink=None):
    blk, dqk = cfg.blk, cfg.dqk
    has_sink = sink is not None

    def body(*refs):
        if has_sink:
            sink_ref, refs = refs[0], refs[1:]
        q_ref, k_ref, v_ref, do_ref, o_ref, lse_ref = refs[:6]
        dq_ref = refs[6]
        dq_sc, delta_sc = refs[-2:]
        i, s = pl.program_id(2), pl.program_id(3)

        @pl.when(s == 0)
        def _():
            dq_sc[...] = jnp.zeros_like(dq_sc)
            delta = jnp.sum(do_ref[...].astype(F32) * o_ref[...].astype(F32), axis=-1, keepdims=True)
            delta_sc[...] = jnp.broadcast_to(delta, delta_sc.shape)

        _, k_nom = cfg.other(i, s)
        k, v = k_ref[...], v_ref[...]
        for rows in cfg.row_chunks():
            sc = _scores(cfg, q_ref[rows, :], k, i, k_nom)
            p = jnp.exp(sc - lse_ref[rows, :1])
            dp = lax.dot_general(do_ref[rows, :], v, (((1,), (1,)), ((), ())), preferred_element_type=F32)
            ds = p * (dp - delta_sc[rows, :1]) * cfg.scale
            dq_sc[rows, :] += jnp.dot(ds.astype(BF), k, preferred_element_type=F32)

        @pl.when(s == cfg.steps - 1)
        def _():
            dq_ref[...] = dq_sc[...].astype(dq_ref.dtype)
            if has_sink:
                ps = jnp.exp(sink_ref[0, :1, :] - lse_ref[...])
                part = -jnp.sum(ps * delta_sc[...], axis=0, keepdims=True)
                refs[7][...] = jnp.broadcast_to(part, refs[7].shape)

    g = cfg.group
    q_spec = pl.BlockSpec((blk, dqk), lambda r, h, i, s: (i, r * cfg.qc + cfg.q0 + h))
    k_spec = pl.BlockSpec((blk, dqk), lambda r, h, i, s: (cfg.other(i, s)[0], r * cfg.kc + cfg.k0 + h // g))
    v_spec = pl.BlockSpec((blk, LANES),
                          lambda r, h, i, s: (cfg.other(i, s)[0], r * cfg.vc + cfg.v0 + cfg.vstride * (h // g)))
    o_spec = pl.BlockSpec((blk, LANES), lambda r, h, i, s: (i, r * cfg.hq + h))
    dq_spec = pl.BlockSpec((blk, dqk), lambda r, h, i, s: (i, r * cfg.hq + h))
    in_specs = [q_spec, k_spec, v_spec, o_spec, o_spec, o_spec]
    operands = [cfg.chains(q), cfg.chains(k), cfg.chains(v), cfg.chains(do), cfg.chains(o), cfg.chains(lse)]
    out_shape = [jax.ShapeDtypeStruct((cfg.len, cfg.dil * cfg.hq * dqk), BF)]
    out_specs = [dq_spec]
    if has_sink:
        in_specs.insert(0, pl.BlockSpec((1, SUBLANES, LANES), lambda r, h, i, s: (h, 0, 0)))
        operands.insert(0, sink)
        out_shape.append(jax.ShapeDtypeStruct((cfg.hq * cfg.nb * SUBLANES, LANES), F32))
        out_specs.append(pl.BlockSpec((SUBLANES, LANES), lambda r, h, i, s: (h * cfg.nb + i, 0)))
    outs = pl.pallas_call(
        body, out_shape=out_shape, grid=(cfg.dil, cfg.hq, cfg.nb, cfg.steps), in_specs=in_specs,
        out_specs=out_specs, scratch_shapes=[pltpu.VMEM((blk, dqk), F32), pltpu.VMEM((blk, LANES), F32)],
        compiler_params=_params(("parallel", "parallel", "parallel", "arbitrary")), name=name,
    )(*operands)
    dq = cfg.unchain(outs[0], cfg.hq * dqk)
    if has_sink:
        return dq, outs[1].reshape(cfg.hq, cfg.nb, SUBLANES, LANES)[:, :, 0, :]
    return dq


def flash_dkv(cfg, q, k, v, do, o, lse, name, out_dtype, add=None):
    blk, dqk, g, nw = cfg.blk, cfg.dqk, cfg.group, cfg.steps
    has_add = add is not None

    def body(*refs):
        k_ref, v_ref, q_ref, do_ref, o_ref, lse_ref = refs[:6]
        pos = 8 if has_add else 6
        dk_ref, dv_ref = refs[pos:pos + 2]
        dk_sc, dv_sc = refs[-2:]
        i, j = pl.program_id(2), pl.program_id(3)

        @pl.when(j == 0)
        def _():
            dk_sc[...] = jnp.zeros_like(dk_sc)
            dv_sc[...] = jnp.zeros_like(dv_sc)

        _, q_nom = cfg.other(i, j % nw)
        k, v = k_ref[...], v_ref[...]
        for rows in cfg.row_chunks():
            q = q_ref[rows, :]
            do = do_ref[rows, :]
            sc = _scores(cfg, q, k, q_nom, i)
            p = jnp.exp(sc - lse_ref[rows, :1])
            delta = jnp.sum(do.astype(F32) * o_ref[rows, :].astype(F32), axis=-1, keepdims=True)
            dv_sc[...] += lax.dot_general(p.astype(BF), do, (((0,), (0,)), ((), ())), preferred_element_type=F32)
            dp = lax.dot_general(do, v, (((1,), (1,)), ((), ())), preferred_element_type=F32)
            ds = p * (dp - delta) * cfg.scale
            dk_sc[...] += lax.dot_general(ds.astype(BF), q, (((0,), (0,)), ((), ())), preferred_element_type=F32)

        @pl.when(j == g * nw - 1)
        def _():
            dk, dv = dk_sc[...], dv_sc[...]
            if has_add:
                dk, dv = dk + refs[6][...].astype(F32), dv + refs[7][...].astype(F32)
            dk_ref[...] = dk.astype(dk_ref.dtype)
            dv_ref[...] = dv.astype(dv_ref.dtype)

    def qrow(i, j):
        return cfg.other(i, j % nw)[0]

    k_spec = pl.BlockSpec((blk, dqk), lambda r, h, i, j: (i, r * cfg.kc + cfg.k0 + h))
    v_spec = pl.BlockSpec((blk, LANES), lambda r, h, i, j: (i, r * cfg.vc + cfg.v0 + cfg.vstride * h))
    q_spec = pl.BlockSpec((blk, dqk), lambda r, h, i, j: (qrow(i, j), r * cfg.qc + cfg.q0 + h * g + j // nw))
    o_spec = pl.BlockSpec((blk, LANES), lambda r, h, i, j: (qrow(i, j), r * cfg.hq + h * g + j // nw))
    dk_spec = pl.BlockSpec((blk, dqk), lambda r, h, i, j: (i, r * cfg.hkv + h))
    dv_spec = pl.BlockSpec((blk, LANES), lambda r, h, i, j: (i, r * cfg.hkv + h))
    in_specs = [k_spec, v_spec, q_spec, o_spec, o_spec, o_spec]
    operands = [cfg.chains(k), cfg.chains(v), cfg.chains(q), cfg.chains(do), cfg.chains(o), cfg.chains(lse)]
    if has_add:
        in_specs += [dk_spec, dv_spec]
        operands += [cfg.chains(add[0]), cfg.chains(add[1])]
    dk, dv = pl.pallas_call(
        body,
        out_shape=[jax.ShapeDtypeStruct((cfg.len, cfg.dil * cfg.hkv * dqk), out_dtype),
                   jax.ShapeDtypeStruct((cfg.len, cfg.dil * cfg.hkv * LANES), out_dtype)],
        grid=(cfg.dil, cfg.hkv, cfg.nb, g * nw), in_specs=in_specs, out_specs=[dk_spec, dv_spec],
        scratch_shapes=[pltpu.VMEM((blk, dqk), F32), pltpu.VMEM((blk, LANES), F32)],
        compiler_params=_params(("parallel", "parallel", "parallel", "arbitrary")), name=name,
    )(*operands)
    return cfg.unchain(dk, cfg.hkv * dqk), cfg.unchain(dv, cfg.hkv * LANES)


class Band:
    def __init__(self, T, dil, hq, group, per, qc, q0, kc, k0, vc, v0, scale, hw, blk):
        self.T, self.dil, self.hq, self.group, self.per = T, dil, hq, group, per
        self.pk = per // group
        self.hkv = hq // group
        self.scale, self.hw = scale, hw
        self.len = T // dil
        self.blk = min(blk, self.len)
        self.nb = self.len // self.blk
        self.win = self.blk + 2 * hw
        self.qcol = lambda r: (r * qc + q0) // per
        self.kcol = lambda r: (r * kc + k0) // self.pk
        self.vcol = lambda r: (r * vc + v0) // self.pk
        self.ocol = lambda r: (r * hq) // per
        self.dkcol = lambda r: (r * self.hkv) // self.pk
        assert hw <= self.blk and qc % per == 0 and q0 % per == 0 and kc % self.pk == 0 and k0 % self.pk == 0
        assert vc % self.pk == 0 and v0 % self.pk == 0

    def chains(self, a):
        return a.reshape(self.len, self.dil * a.shape[1])

    def rows3(self, width, col):
        nb = self.nb
        return [pl.BlockSpec((self.blk, width), lambda r, h, i: (jnp.maximum(i - 1, 0), col(r) + h)),
                pl.BlockSpec((self.blk, width), lambda r, h, i: (i, col(r) + h)),
                pl.BlockSpec((self.blk, width), lambda r, h, i: (jnp.minimum(i + 1, nb - 1), col(r) + h))]

    def window(self, prev, cur, nxt, j):
        cols = slice(j * LANES, (j + 1) * LANES)
        return jnp.concatenate([prev[self.blk - self.hw:, cols], cur[:, cols], nxt[:self.hw, cols]], axis=0)

    def valid(self, i, window_is_rows):
        shape = (self.win, self.blk) if window_is_rows else (self.blk, self.win)
        wdim = 0 if window_is_rows else 1
        bpos = i * self.blk + lax.broadcasted_iota(jnp.int32, shape, 1 - wdim)
        wpos = i * self.blk - self.hw + lax.broadcasted_iota(jnp.int32, shape, wdim)
        ok = jnp.abs(bpos - wpos) <= self.hw
        return jnp.logical_and(ok, jnp.logical_and(wpos >= 0, wpos < self.len))


def band_fwd(cfg, q, k, v, name, out_dtype, sink=None):
    blk, per, pk = cfg.blk, cfg.per, cfg.pk
    has_sink = sink is not None

    def body(*refs):
        if has_sink:
            sink_ref, refs = refs[0], refs[1:]
        q_ref, kp, kc, kn, vp, vc, vn, o_ref, lse_ref = refs
        ok = cfg.valid(pl.program_id(2), False)
        for j in range(per):
            jk = j // cfg.group
            if j % cfg.group == 0:
                kw = cfg.window(kp, kc, kn, jk)
                vw = cfg.window(vp, vc, vn, jk)
            cols = slice(j * LANES, (j + 1) * LANES)
            s = lax.dot_general(q_ref[:, cols], kw, (((1,), (1,)), ((), ())), preferred_element_type=F32) * cfg.scale
            s = jnp.where(ok, s, NEG)
            m = jnp.max(s, axis=-1, keepdims=True)
            if has_sink:
                sk = sink_ref[j, :1, :1]
                m = jnp.maximum(m, sk)
            e = jnp.exp(s - m)
            den = jnp.sum(e, axis=-1, keepdims=True)
            if has_sink:
                den = den + jnp.exp(sk - m)
            o = jnp.dot(e.astype(BF), vw, preferred_element_type=F32) / den
            o_ref[:, cols] = o.astype(o_ref.dtype)
            lse_ref[:, cols] = jnp.broadcast_to(m + jnp.log(den), (blk, LANES))

    q_spec = pl.BlockSpec((blk, per * LANES), lambda r, h, i: (i, cfg.qcol(r) + h))
    o_spec = pl.BlockSpec((blk, per * LANES), lambda r, h, i: (i, cfg.ocol(r) + h))
    in_specs = [q_spec] + cfg.rows3(pk * LANES, cfg.kcol) + cfg.rows3(pk * LANES, cfg.vcol)
    kc_, vc_ = cfg.chains(k), cfg.chains(v)
    operands = [cfg.chains(q), kc_, kc_, kc_, vc_, vc_, vc_]
    if has_sink:
        in_specs.insert(0, pl.BlockSpec((per, SUBLANES, LANES), lambda r, h, i: (h, 0, 0)))
        operands.insert(0, sink)
    cols = cfg.dil * cfg.hq * LANES
    o, lse = pl.pallas_call(
        body, out_shape=[jax.ShapeDtypeStruct((cfg.len, cols), out_dtype), jax.ShapeDtypeStruct((cfg.len, cols), F32)],
        grid=(cfg.dil, cfg.hq // per, cfg.nb), in_specs=in_specs, out_specs=[o_spec, o_spec],
        compiler_params=_params(("parallel", "parallel", "parallel")), name=name,
    )(*operands)
    return o.reshape(cfg.T, cfg.hq * LANES), lse.reshape(cfg.T, cfg.hq * LANES)


def band_dq(cfg, q, k, v, do, o, lse, name, sink=None):
    blk, per, pk = cfg.blk, cfg.per, cfg.pk
    has_sink = sink is not None

    def body(*refs):
        if has_sink:
            sink_ref, refs = refs[0], refs[1:]
        q_ref, kp, kc, kn, vp, vc, vn, do_ref, o_ref, lse_ref, dq_ref = refs[:11]
        ok = cfg.valid(pl.program_id(2), False)
        for j in range(per):
            jk = j // cfg.group
            if j % cfg.group == 0:
                kw = cfg.window(kp, kc, kn, jk)
                vw = cfg.window(vp, vc, vn, jk)
            cols = slice(j * LANES, (j + 1) * LANES)
            do = do_ref[:, cols]
            lse = lse_ref[:, j * LANES:j * LANES + 1]
            delta = jnp.sum(do.astype(F32) * o_ref[:, cols].astype(F32), axis=-1, keepdims=True)
            s = lax.dot_general(q_ref[:, cols], kw, (((1,), (1,)), ((), ())), preferred_element_type=F32) * cfg.scale
            p = jnp.exp(jnp.where(ok, s, NEG) - lse)
            dp = lax.dot_general(do, vw, (((1,), (1,)), ((), ())), preferred_element_type=F32)
            ds = p * (dp - delta) * cfg.scale
            dq_ref[:, cols] = jnp.dot(ds.astype(BF), kw, preferred_element_type=F32).astype(dq_ref.dtype)
            if has_sink:
                part = -jnp.sum(jnp.exp(sink_ref[j, :1, :1] - lse) * delta, axis=0, keepdims=True)
                refs[11][j * SUBLANES:(j + 1) * SUBLANES, :] = jnp.broadcast_to(part, (SUBLANES, LANES))

    q_spec = pl.BlockSpec((blk, per * LANES), lambda r, h, i: (i, cfg.qcol(r) + h))
    o_spec = pl.BlockSpec((blk, per * LANES), lambda r, h, i: (i, cfg.ocol(r) + h))
    in_specs = [q_spec] + cfg.rows3(pk * LANES, cfg.kcol) + cfg.rows3(pk * LANES, cfg.vcol) + [o_spec] * 3
    kc_, vc_ = cfg.chains(k), cfg.chains(v)
    operands = [cfg.chains(q), kc_, kc_, kc_, vc_, vc_, vc_, cfg.chains(do), cfg.chains(o), cfg.chains(lse)]
    out_shape = [jax.ShapeDtypeStruct((cfg.len, cfg.dil * cfg.hq * LANES), BF)]
    out_specs = [o_spec]
    if has_sink:
        in_specs.insert(0, pl.BlockSpec((per, SUBLANES, LANES), lambda r, h, i: (h, 0, 0)))
        operands.insert(0, sink)
        out_shape.append(jax.ShapeDtypeStruct((cfg.hq // per, cfg.nb, per * SUBLANES, LANES), F32))
        out_specs.append(pl.BlockSpec((None, None, per * SUBLANES, LANES), lambda r, h, i: (h, i, 0, 0)))
    outs = pl.pallas_call(
        body, out_shape=out_shape, grid=(cfg.dil, cfg.hq // per, cfg.nb), in_specs=in_specs, out_specs=out_specs,
        compiler_params=_params(("parallel", "parallel", "parallel")), name=name,
    )(*operands)
    dq = outs[0].reshape(cfg.T, cfg.hq * LANES)
    return (dq, outs[1]) if has_sink else dq


def band_dkv(cfg, q, k, v, do, o, lse, name, out_dtype, add=None, dv_into=None):
    blk, per, pk, group = cfg.blk, cfg.per, cfg.pk, cfg.group
    has_add = add is not None
    carried = dv_into is not None
    assert not carried or cfg.dil == 1

    def body(*refs):
        k_ref, v_ref = refs[:2]
        qs, dos, os_, lses = refs[2:5], refs[5:8], refs[8:11], refs[11:14]
        pos = 14 + (2 if has_add else 0) + (1 if carried else 0)
        dk_ref, dv_ref = refs[pos:pos + 2]
        ok = cfg.valid(pl.program_id(2), True)
        for jk in range(pk):
            kcols = slice(jk * LANES, (jk + 1) * LANES)
            kt, vt = k_ref[:, kcols], v_ref[:, kcols]
            dk = jnp.zeros((blk, LANES), F32)
            dv = jnp.zeros((blk, LANES), F32)
            for g in range(group):
                j = jk * group + g
                qw = cfg.window(*qs, j)
                dow = cfg.window(*dos, j)
                lse = cfg.window(*lses, j)[:, :1]
                delta = jnp.sum(dow.astype(F32) * cfg.window(*os_, j).astype(F32), axis=-1, keepdims=True)
                s = lax.dot_general(qw, kt, (((1,), (1,)), ((), ())), preferred_element_type=F32) * cfg.scale
                p = jnp.exp(jnp.where(ok, s, NEG) - lse)
                dv = dv + lax.dot_general(p.astype(BF), dow, (((0,), (0,)), ((), ())), preferred_element_type=F32)
                dp = lax.dot_general(dow, vt, (((1,), (1,)), ((), ())), preferred_element_type=F32)
                ds = p * (dp - delta) * cfg.scale
                dk = dk + lax.dot_general(ds.astype(BF), qw, (((0,), (0,)), ((), ())), preferred_element_type=F32)
            if has_add:
                dk, dv = dk + refs[14][:, kcols].astype(F32), dv + refs[15][:, kcols].astype(F32)
            dk_ref[:, kcols] = dk.astype(dk_ref.dtype)
            dv_ref[:, kcols] = dv.astype(dv_ref.dtype)

    k_spec = pl.BlockSpec((blk, pk * LANES), lambda r, h, i: (i, cfg.kcol(r) + h))
    v_spec = pl.BlockSpec((blk, pk * LANES), lambda r, h, i: (i, cfg.vcol(r) + h))
    d_spec = pl.BlockSpec((blk, pk * LANES), lambda r, h, i: (i, cfg.dkcol(r) + h))
    in_specs = [k_spec, v_spec] + cfg.rows3(per * LANES, cfg.qcol) + cfg.rows3(per * LANES, cfg.ocol) * 3
    qc_, doc, oc, lc = cfg.chains(q), cfg.chains(do), cfg.chains(o), cfg.chains(lse)
    operands = [cfg.chains(k), cfg.chains(v), qc_, qc_, qc_, doc, doc, doc, oc, oc, oc, lc, lc, lc]
    if has_add:
        in_specs += [d_spec, d_spec]
        operands += [cfg.chains(add[0]), cfg.chains(add[1])]
    cols = cfg.dil * cfg.hkv * LANES
    out_shape = [jax.ShapeDtypeStruct((cfg.len, cols), out_dtype)] * 2
    out_specs = [d_spec, d_spec]
    aliases = {}
    if carried:
        buf, blocks, block0 = dv_into
        out_shape[1] = jax.ShapeDtypeStruct((cfg.T, blocks * LANES), BF)
        out_specs[1] = pl.BlockSpec((blk, pk * LANES), lambda r, h, i: (i, block0 // pk + h))
        aliases = {len(operands): 1}
        in_specs.append(pl.BlockSpec(memory_space=pl.ANY))
        operands.append(buf)
    dk, dv = pl.pallas_call(
        body, out_shape=out_shape, grid=(cfg.dil, cfg.hq // per, cfg.nb), in_specs=in_specs, out_specs=out_specs,
        input_output_aliases=aliases, compiler_params=_params(("parallel", "parallel", "parallel")), name=name,
    )(*operands)
    return dk.reshape(cfg.T, cfg.hkv * LANES), (dv if carried else dv.reshape(cfg.T, cfg.hkv * LANES))


HBM_SPEC = pl.BlockSpec(memory_space=pltpu.HBM)


def _place():
    x, y, c = lax.axis_index("x"), lax.axis_index("y"), lax.axis_index("c")
    chips = [(1 - x, y), (x, 1 - y), (1 - x, 1 - y)]
    return x, y, c, chips


def gather_weights(shards):
    n = len(shards)

    def body(*refs):
        ins, outs = refs[:n], refs[n:2 * n]
        send_sems, recv_sems, local_sems = refs[2 * n:]
        x, y, c, chips = _place()
        me = 2 * x + y
        sibling = (x, y, 1 - c)

        def copy(w, k, src, chip_of_block, half, to):
            return pltpu.make_async_remote_copy(
                src_ref=src, dst_ref=outs[w].at[chip_of_block, half], send_sem=send_sems.at[6 * w + k],
                recv_sem=recv_sems.at[6 * w + k], device_id=to, device_id_type=MESH)

        started = []
        local = []
        for w in range(n):
            own = pltpu.make_async_copy(ins[w], outs[w].at[me], local_sems.at[w])
            own.start()
            local.append(own)
            for j, chip in enumerate(chips):
                cp = copy(w, j, ins[w].at[c], me, c, (*chip, c))
                cp.start()
                started.append(cp)
        for w in range(n):
            for j, (cx, cy) in enumerate(chips):
                them = 2 * cx + cy
                copy(w, j, ins[w].at[c], them, c, (cx, cy, c)).wait_recv()
                fwd = copy(w, 3 + j, outs[w].at[them, c], them, c, sibling)
                fwd.start()
                started.append(fwd)
        for w in range(n):
            for j, (cx, cy) in enumerate(chips):
                copy(w, 3 + j, ins[w].at[c], 2 * cx + cy, 1 - c, sibling).wait_recv()
        for cp in started:
            cp.wait_send()
        for own in local:
            own.wait()

    return pl.pallas_call(
        body, out_shape=[jax.ShapeDtypeStruct((4,) + s.shape, s.dtype) for s in shards],
        in_specs=[HBM_SPEC] * n, out_specs=[HBM_SPEC] * n,
        scratch_shapes=[pltpu.SemaphoreType.DMA((6 * n,)), pltpu.SemaphoreType.DMA((6 * n,)),
                        pltpu.SemaphoreType.DMA((n,))],
        name="gather_weights",
    )(*shards)


def _core_index():
    return lax.axis_index("c").astype(jnp.int32).reshape(1)


def presum_core_halves(g2, core, name):
    _, rows, cols = g2.shape
    tr = _row_tile(rows, cols)
    nb = rows // tr
    g2 = g2.reshape(2 * rows, cols)

    def body(core_ref, mine_ref, other_ref, out_ref, land, send_sems, recv_sems):
        x, y, c, _ = _place()
        slot = pl.program_id(0) % 2
        cp = pltpu.make_async_remote_copy(
            src_ref=other_ref, dst_ref=land.at[slot], send_sem=send_sems.at[slot], recv_sem=recv_sems.at[slot],
            device_id=(x, y, 1 - c), device_id_type=MESH)
        cp.start()
        cp.wait_recv()
        out_ref[...] = (mine_ref[...] + land[slot]).astype(out_ref.dtype)
        cp.wait_send()

    grid_spec = pltpu.PrefetchScalarGridSpec(
        num_scalar_prefetch=1, grid=(nb,),
        in_specs=[pl.BlockSpec((tr, cols), lambda i, core: (core[0] * nb + i, 0)),
                  pl.BlockSpec((tr, cols), lambda i, core: ((1 - core[0]) * nb + i, 0))],
        out_specs=pl.BlockSpec((tr, cols), lambda i, core: (i, 0)),
        scratch_shapes=[pltpu.VMEM((2, tr, cols), F32), pltpu.SemaphoreType.DMA((2,)), pltpu.SemaphoreType.DMA((2,))])
    return pl.pallas_call(
        body, out_shape=jax.ShapeDtypeStruct((rows, cols), BF), grid_spec=grid_spec,
        compiler_params=_params(("arbitrary",)), name=name,
    )(core, g2, g2)


def sum_and_swap(landed, name):
    n, rows, cols = landed.shape
    tr = _row_tile(rows, cols)

    def body(*refs):
        slots = refs[:n]
        mine_ref, theirs_ref, out_buf, land, send_sems, recv_sems = refs[n:]
        x, y, c, _ = _place()
        slot = pl.program_id(0) % 2
        tot = slots[0][...].astype(F32)
        for r in slots[1:]:
            tot = tot + r[...].astype(F32)
        mine_ref[...] = tot
        out_buf[slot] = tot
        cp = pltpu.make_async_remote_copy(
            src_ref=out_buf.at[slot], dst_ref=land.at[slot], send_sem=send_sems.at[slot], recv_sem=recv_sems.at[slot],
            device_id=(x, y, 1 - c), device_id_type=MESH)
        cp.start()
        cp.wait_recv()
        theirs_ref[...] = land[slot]
        cp.wait_send()

    specs = [pl.BlockSpec((None, tr, cols), functools.partial(lambda s, i: (s, i, 0), s)) for s in range(n)]
    row = pl.BlockSpec((tr, cols), lambda i: (i, 0))
    return pl.pallas_call(
        body, out_shape=[jax.ShapeDtypeStruct((rows, cols), F32)] * 2, grid=(rows // tr,), in_specs=specs,
        out_specs=[row, row],
        scratch_shapes=[pltpu.VMEM((2, tr, cols), F32), pltpu.VMEM((2, tr, cols), F32),
                        pltpu.SemaphoreType.DMA((2,)), pltpu.SemaphoreType.DMA((2,))],
        compiler_params=_params(("arbitrary",)), name=name,
    )(*([landed] * n))


def scatter_partials(parts):
    n = len(parts)

    def body(*refs):
        ins, outs = refs[:n], refs[n:2 * n]
        send_sems, recv_sems, local_sems = refs[2 * n:]
        x, y, c, chips = _place()
        me = 2 * x + y
        started = []
        for w in range(n):
            own = pltpu.make_async_copy(ins[w].at[me], outs[w].at[me], local_sems.at[w])
            own.start()
            started.append(own)
        sends = []
        for w in range(n):
            for j, (cx, cy) in enumerate(chips):
                cp = pltpu.make_async_remote_copy(
                    src_ref=ins[w].at[2 * cx + cy], dst_ref=outs[w].at[me], send_sem=send_sems.at[3 * w + j],
                    recv_sem=recv_sems.at[3 * w + j], device_id=(cx, cy, c), device_id_type=MESH)
                cp.start()
                sends.append(cp)
        for w in range(n):
            for j, (cx, cy) in enumerate(chips):
                pltpu.make_async_remote_copy(
                    src_ref=ins[w].at[me], dst_ref=outs[w].at[2 * cx + cy], send_sem=send_sems.at[3 * w + j],
                    recv_sem=recv_sems.at[3 * w + j], device_id=(cx, cy, c), device_id_type=MESH).wait_recv()
        for cp in sends:
            cp.wait_send()
        for own in started:
            own.wait()

    return pl.pallas_call(
        body, out_shape=[jax.ShapeDtypeStruct(p.shape, p.dtype) for p in parts],
        in_specs=[HBM_SPEC] * n, out_specs=[HBM_SPEC] * n,
        scratch_shapes=[pltpu.SemaphoreType.DMA((3 * n,)), pltpu.SemaphoreType.DMA((3 * n,)),
                        pltpu.SemaphoreType.DMA((n,))],
        name="scatter_partials",
    )(*parts)


def adamw_halves(w, mine, theirs, m, v, core, name):
    rows, cols = w.shape
    tr = _row_tile(rows // 2, cols, 1 << 18)
    nh = rows // 2 // tr

    def body(core_ref, w_ref, a_ref, b_ref, m_ref, v_ref, g_out, d_out, m_out, v_out):
        g = jnp.where(pl.program_id(0) // nh == core_ref[0], a_ref[...], b_ref[...])
        d_out[...], m_out[...], v_out[...] = _adam_fn(w_ref[...], g, m_ref[...], v_ref[...])
        g_out[...] = g

    full = pl.BlockSpec((tr, cols), lambda i, core: (i, 0))
    half = pl.BlockSpec((tr, cols), lambda i, core: (i % nh, 0))
    grid_spec = pltpu.PrefetchScalarGridSpec(
        num_scalar_prefetch=1, grid=(rows // tr,), in_specs=[full, half, half, full, full], out_specs=[full] * 4)
    return pl.pallas_call(
        body, out_shape=[jax.ShapeDtypeStruct((rows, cols), F32)] * 4, grid_spec=grid_spec,
        compiler_params=_params(("parallel",)), name=name,
    )(core, w, mine, theirs, m, v)


def gather_small(vec):
    rows = vec.shape[0]

    def body(v_ref, out_ref, send_sems, recv_sems):
        x, y, c, _ = _place()
        me = 4 * x + 2 * y + c
        out_ref[me] = v_ref[...]
        flips = [(dx, dy, dc) for dx in (0, 1) for dy in (0, 1) for dc in (0, 1)][1:]

        def peer(f):
            return tuple(1 - a if d else a for a, d in zip((x, y, c), f))

        def copy(k, block, to):
            return pltpu.make_async_remote_copy(
                src_ref=v_ref, dst_ref=out_ref.at[block], send_sem=send_sems.at[k], recv_sem=recv_sems.at[k],
                device_id=to, device_id_type=MESH)

        sends = [copy(k, me, peer(f)) for k, f in enumerate(flips)]
        for cp in sends:
            cp.start()
        for k, f in enumerate(flips):
            px, py, pc = peer(f)
            copy(k, 4 * px + 2 * py + pc, peer(f)).wait_recv()
        for cp in sends:
            cp.wait_send()

    vm = pl.BlockSpec(memory_space=pltpu.VMEM)
    return pl.pallas_call(
        body, out_shape=jax.ShapeDtypeStruct((8, rows, SMALL_COLS), F32), in_specs=[vm], out_specs=vm,
        scratch_shapes=[pltpu.SemaphoreType.DMA((7,)), pltpu.SemaphoreType.DMA((7,))], name="gather_small",
    )(vec)


def sum_slots(a, out_dtype, name):
    n, rows, cols = a.shape
    tr = _row_tile(rows, cols)

    def body(*refs):
        tot = refs[0][...].astype(F32)
        for r in refs[1:n]:
            tot = tot + r[...].astype(F32)
        refs[n][...] = tot.astype(out_dtype)

    specs = [pl.BlockSpec((None, tr, cols), functools.partial(lambda s, i: (s, i, 0), s)) for s in range(n)]
    return pl.pallas_call(
        body, out_shape=jax.ShapeDtypeStruct((rows, cols), out_dtype), grid=(rows // tr,), in_specs=specs,
        out_specs=pl.BlockSpec((tr, cols), lambda i: (i, 0)), compiler_params=_params(("parallel",)), name=name,
    )(*([a] * n))


def _adam_fn(w, g, m, v):
    m = ADAM_B1 * m + (1.0 - ADAM_B1) * g
    v = ADAM_B2 * v + (1.0 - ADAM_B2) * (g * g)
    m_hat = m / (1.0 - ADAM_B1 ** ADAM_STEP)
    v_hat = v / (1.0 - ADAM_B2 ** ADAM_STEP)
    delta = -ADAM_LR * (m_hat / (jnp.sqrt(v_hat) + ADAM_EPS) + ADAM_WD * w)
    return delta, m, v


def adamw(w, g, m, v, name):
    return rowwise(_adam_fn, [w, g, m, v], [F32, F32, F32], name)


def _full_weight(name, gathered, local_shape):
    L, a, b = local_shape
    g = gathered.reshape((4, L, a, b))
    if SHARD_AXIS[name] == 1:
        return g.transpose(1, 0, 2, 3).reshape(L, 4 * a, b)
    return g.transpose(1, 2, 0, 3).reshape(L, a, 4 * b)


def _grad_slots(name, dw):
    L, a, b = dw.shape
    if SHARD_AXIS[name] == 1:
        s = dw.reshape(L, 4, a // 4, b).transpose(1, 0, 2, 3)
        rows, cols = L * (a // 4), b
    else:
        s = dw.reshape(L, a, 4, b // 4).transpose(2, 0, 1, 3)
        rows, cols = L * a, b // 4
    return s.reshape(4, 2, rows // 2, cols).transpose(1, 0, 2, 3)


def _attn_a(T):
    group = A_HEADS // A_KV_HEADS
    return Band(T, 1, A_HEADS, group, group, A_HEADS, 0, A_KV_HEADS, 0, A_HEADS + 2 * A_KV_HEADS,
                A_HEADS + A_KV_HEADS, 1.0 / math.sqrt(HEAD_DIM), A_HALF_WINDOW, BAND_BLOCK)


def _attn_b(T):
    return Attn(T, 1, B_HEADS, 1, B_HEADS, 0, B_HEADS, 0, 2 * B_HEADS, 1, 2, B_PAD, 1.0 / math.sqrt(B_QK), None,
                DENSE_BLOCK)


def _attn_c(T, group):
    window, dil = C_PATTERNS[group]
    return Band(T, dil, C_HEADS, 1, BAND_HEADS_PER_STEP, C_HEADS, 0, C_HEADS, 0, C_HEADS, 0,
                1.0 / math.sqrt(HEAD_DIM), window // 2 // dil, BAND_BLOCK)


def _pad_heads(a, axis_len_true, axis_len_pad):
    lead = a.shape[:-1]
    h = a.shape[-1] // axis_len_true
    a = a.reshape(lead + (h, axis_len_true))
    a = jnp.pad(a, [(0, 0)] * len(lead) + [(0, 0), (0, axis_len_pad - axis_len_true)])
    return a.reshape(lead + (h * axis_len_pad,))


def _unpad_heads(a, axis_len_true, axis_len_pad):
    lead = a.shape[:-1]
    h = a.shape[-1] // axis_len_pad
    return a.reshape(lead + (h, axis_len_pad))[..., :axis_len_true].reshape(lead + (h * axis_len_true,))


def _weight_grad(G, name, layer, a, dy, W, tag):
    layers, rows, cols = W[name].shape
    if name in SLOT_DIRECT:
        G[name] = matmul([(a, dy)], "tn", F32, tag, slot=Slot(name, layers, layer, rows, cols, 0, G.get(name)))
    else:
        G.setdefault(name, [None] * layers)[layer] = matmul([(a, dy)], "tn", F32, tag)


def _mixer_fwd(kind, slot, hn, W, S, tabs, tag):
    T = hn.shape[0]
    if kind == 0:
        cfg = _attn_a(T)
        qkv = matmul([(hn, W["a_w_in"][slot])], "nn", BF, tag + "_a_in")
        q = headnorm_fwd(qkv, W["a_q_norm"][slot], tabs["hd"], tag + "_a_qn", A_HEADS, 0, HEAD_DIM, HEAD_DIM)
        k = headnorm_fwd(qkv, W["a_k_norm"][slot], tabs["hd"], tag + "_a_kn", A_KV_HEADS, A_HEADS, HEAD_DIM, HEAD_DIM)
        sink = jnp.broadcast_to(W["a_sink"][slot][:, None, None], (A_HEADS, SUBLANES, LANES)).astype(F32)
        o, lse = band_fwd(cfg, q, k, qkv, tag + "_a_att", BF, sink=sink)
        S.update(qkv=qkv, q=q, k=k, o=o, lse=lse, sink=sink)
        return o
    if kind == 1:
        cfg = _attn_b(T)
        lat = matmul([(hn, W["b_w_in"][slot])], "nn", BF, tag + "_b_in")
        qn = rmsnorm_fwd(lat, W["b_q_lat_norm"][slot], tag + "_b_qlat", 0, B_Q_RANK)
        kvn = rmsnorm_fwd(lat, W["b_kv_lat_norm"][slot], tag + "_b_kvlat", 1, B_KV_RANK)
        qp = matmul([(qn, W["b_w_q_up_pad"][slot])], "nn", BF, tag + "_b_qup")
        kv = matmul([(kvn, W["b_w_kv_up"][slot])], "nn", BF, tag + "_b_kvup")
        k_rope = lat[:, B_Q_RANK + B_KV_RANK:]
        kpre = jnp.concatenate(
            [kv.reshape(T, B_HEADS, 2 * B_NOPE)[:, :, :B_NOPE],
             jnp.broadcast_to(k_rope[:, None, :], (T, B_HEADS, B_ROPE)),
             jnp.zeros((T, B_HEADS, B_PAD - B_QK), BF)], axis=-1).reshape(T, B_HEADS * B_PAD)
        q = headnorm_fwd(qp, W["b_q_norm_pad"][slot], tabs["b"], tag + "_b_qn", B_HEADS, 0, B_PAD, B_QK)
        k = headnorm_fwd(kpre, W["b_k_norm_pad"][slot], tabs["b"], tag + "_b_kn", B_HEADS, 0, B_PAD, B_QK)
        o, lse = flash_fwd(cfg, q, k, kv, tag + "_b_att", BF)
        S.update(lat=lat, qn=qn, kvn=kvn, qp=qp, kv=kv, kpre=kpre, q=q, k=k, o=o, lse=lse)
        return o
    qkv = matmul([(hn, W["c_w_in"][slot])], "nn", BF, tag + "_c_in")
    nq = C_GROUPS * C_HEADS
    qs = [headnorm_fwd(qkv, W["c_q_norm"][slot], tabs["hd"], f"{tag}_c_qn{g}", C_HEADS, g * C_HEADS, HEAD_DIM, HEAD_DIM)
          for g in range(C_GROUPS)]
    k = headnorm_fwd(qkv, W["c_k_norm"][slot], tabs["hd"], tag + "_c_kn", C_HEADS, nq, HEAD_DIM, HEAD_DIM)
    outs, lses = [], []
    v = qkv[:, (C_GROUPS + 1) * C_HEADS * HEAD_DIM:]
    for g in range(C_GROUPS):
        og, lg = band_fwd(_attn_c(T, g), qs[g], k, v, f"{tag}_c_att{g}", F32)
        outs.append(og)
        lses.append(lg)
    o, lse = rowwise(_merge_fn, outs + lses, [BF, F32], tag + "_c_merge")
    S.update(qkv=qkv, qs=qs, v=v, k=k, o=o, lse=lse)
    return o


def _mixer_bwd(kind, slot, hn, do, W, S, tabs, tag, G):
    T = hn.shape[0]
    if kind == 0:
        cfg = _attn_a(T)
        qkv = S["qkv"]
        dq, dsink = band_dq(cfg, S["q"], S["k"], qkv, do, S["o"], S["lse"], tag + "_a_dq", sink=S["sink"])
        blocks = A_HEADS + 2 * A_KV_HEADS
        dqkv, dgq = headnorm_bwd(qkv, W["a_q_norm"][slot], tabs["hd"], dq, tag + "_a_dqn", A_HEADS, 0, HEAD_DIM, HEAD_DIM,
                                 into=(None, blocks, 0))
        dk, dqkv = band_dkv(cfg, S["q"], S["k"], qkv, do, S["o"], S["lse"], tag + "_a_dkv", BF,
                            dv_into=(dqkv, blocks, A_HEADS + A_KV_HEADS))
        dqkv, dgk = headnorm_bwd(qkv, W["a_k_norm"][slot], tabs["hd"], dk, tag + "_a_dkn", A_KV_HEADS, A_HEADS,
                                 HEAD_DIM, HEAD_DIM, into=(dqkv, blocks, A_HEADS))
        _weight_grad(G, "a_w_in", slot, hn, dqkv, W, tag + "_a_dwin")
        G["a_q_norm"][slot], G["a_k_norm"][slot] = dgq, dgk
        parts = dsink.reshape(A_HEADS // cfg.per, cfg.nb, cfg.per, SUBLANES, LANES)[:, :, :, 0, 0]
        G["a_sink"][slot] = jnp.sum(parts, axis=1).reshape(A_HEADS)
        return matmul([(dqkv, W["a_w_in"][slot])], "nt", F32, tag + "_a_dhn")
    if kind == 1:
        cfg = _attn_b(T)
        kv = S["kv"]
        dq = flash_dq(cfg, S["q"], S["k"], kv, do, S["o"], S["lse"], tag + "_b_dq")
        dk, dv = flash_dkv(cfg, S["q"], S["k"], kv, do, S["o"], S["lse"], tag + "_b_dkv", BF)
        dqp, dgq = headnorm_bwd(S["qp"], W["b_q_norm_pad"][slot], tabs["b"], dq, tag + "_b_dqn", B_HEADS, 0, B_PAD, B_QK)
        dkp, dgk, dksum = headnorm_bwd(S["kpre"], W["b_k_norm_pad"][slot], tabs["b"], dk, tag + "_b_dkn", B_HEADS, 0,
                                       B_PAD, B_QK, head_sum=True)
        dkv = jnp.concatenate([dkp.reshape(T, B_HEADS, B_PAD)[:, :, :B_NOPE], dv.reshape(T, B_HEADS, LANES)],
                              axis=-1).reshape(T, B_HEADS * 2 * B_NOPE)
        _weight_grad(G, "b_w_kv_up", slot, S["kvn"], dkv, W, tag + "_b_dwkv")
        G["b_w_q_up"][slot] = _unpad_heads(matmul([(S["qn"], dqp)], "tn", F32, tag + "_b_dwq"), B_QK, B_PAD)
        dqn = matmul([(dqp, W["b_w_q_up_pad"][slot])], "nt", F32, tag + "_b_dqnorm")
        dkvn = matmul([(dkv, W["b_w_kv_up"][slot])], "nt", F32, tag + "_b_dkvnorm")
        dql, dg_q = rmsnorm_bwd(S["lat"], W["b_q_lat_norm"][slot], dqn, tag + "_b_dqlat", [BF], None, 0, B_Q_RANK)
        dkvl, dg_kv = rmsnorm_bwd(S["lat"], W["b_kv_lat_norm"][slot], dkvn, tag + "_b_dkvlat", [BF], None, 1, B_KV_RANK)
        dlat = jnp.concatenate([dql, dkvl, dksum[:, B_NOPE:B_QK].astype(BF)], axis=1)
        _weight_grad(G, "b_w_in", slot, hn, dlat, W, tag + "_b_dwin")
        G["b_q_norm"][slot], G["b_k_norm"][slot] = dgq[:B_QK], dgk[:B_QK]
        G["b_q_lat_norm"][slot], G["b_kv_lat_norm"][slot] = dg_q, dg_kv
        return matmul([(dlat, W["b_w_in"][slot])], "nt", F32, tag + "_b_dhn")
    qkv = S["qkv"]
    nq = C_GROUPS * C_HEADS
    blocks = (C_GROUPS + 2) * C_HEADS
    dqkv, dgq = None, 0.0
    for g in range(C_GROUPS):
        dq = band_dq(_attn_c(T, g), S["qs"][g], S["k"], S["v"], do, S["o"], S["lse"], f"{tag}_c_dq{g}")
        dqkv, dg = headnorm_bwd(qkv, W["c_q_norm"][slot], tabs["hd"], dq, f"{tag}_c_dqn{g}", C_HEADS, g * C_HEADS,
                                HEAD_DIM, HEAD_DIM, into=(dqkv, blocks, g * C_HEADS))
        dgq = dgq + dg
    acc = None
    for g in reversed(range(C_GROUPS)):
        into = (dqkv, blocks, (C_GROUPS + 1) * C_HEADS) if g == 0 else None
        acc = band_dkv(_attn_c(T, g), S["qs"][g], S["k"], S["v"], do, S["o"], S["lse"], f"{tag}_c_dkv{g}", F32,
                       add=acc, dv_into=into)
    dk, dqkv = acc
    dqkv, dgk = headnorm_bwd(qkv, W["c_k_norm"][slot], tabs["hd"], dk, tag + "_c_dkn", C_HEADS, nq, HEAD_DIM, HEAD_DIM,
                             into=(dqkv, blocks, nq))
    _weight_grad(G, "c_w_in", slot, hn, dqkv, W, tag + "_c_dwin")
    G["c_q_norm"][slot], G["c_k_norm"][slot] = dgq, dgk
    return matmul([(dqkv, W["c_w_in"][slot])], "nt", F32, tag + "_c_dhn")


MIXER_OUT = ("a_w_o", "b_w_o", "c_w_o")


def local_step(x, p, positions, loss_target, W):
    T = x.shape[0]
    tabs = {"hd": rope_tables(positions, HEAD_DIM, 0, PARTIAL_ROT), "b": rope_tables(positions, B_PAD, B_NOPE, B_ROPE)}
    W = dict(W)
    W["b_w_q_up_pad"] = _pad_heads(W["b_w_q_up"], B_QK, B_PAD)
    W["b_q_norm_pad"] = _pad_heads(W["b_q_norm"], B_QK, B_PAD)
    W["b_k_norm_pad"] = _pad_heads(W["b_k_norm"], B_QK, B_PAD)
    saved = []
    h = x
    for i in range(DEPTH):
        kind, slot = i % 3, i // 3
        tag = f"l{i}"
        S = {"h0": h}
        hn = rmsnorm_fwd(h, W["g_mix"][i], tag + "_mixnorm")
        o = _mixer_fwd(kind, slot, hn, W, S, tabs, tag)
        h1 = matmul([(o, W[MIXER_OUT[kind]][slot])], "nn", F32, tag + "_mixout", res=h)
        hn2 = rmsnorm_fwd(h1, W["g_ffn"][i], tag + "_ffnnorm")
        a, b, c = matmul_swiglu(hn2, W["w_ffn_gate"][i], W["w_ffn_up"][i], tag + "_gateup")
        h2 = matmul([(c, W["w_ffn_down"][i])], "nn", F32, tag + "_down", res=h1)
        hn3 = rmsnorm_fwd(h2, W["g_ple"][i], tag + "_plenorm")
        p_i = p[i].astype(BF)
        pp = matmul([(p_i, W["w_ple_proj"][i])], "nn", BF, tag + "_pleproj")
        z, h3 = matmul([(hn3, W["w_ple_gate"][i])], "nn", BF, tag + "_plegate", ple=(h2, pp))
        S.update(hn=hn, h1=h1, hn2=hn2, a=a, b=b, c=c, h2=h2, hn3=hn3, z=z, pp=pp, p=p_i)
        saved.append(S)
        h = h3

    loss, dh = loss_and_grad(h, loss_target, "loss")
    G = {n: [None] * W[n].shape[0] for n in SMALL + ("b_w_q_up",)}
    for i in reversed(range(DEPTH)):
        kind, slot = i % 3, i // 3
        tag = f"l{i}"
        S = saved[i]
        dz, dpp = rowwise(_ple_bwd_fn, [dh, S["z"], S["pp"]], [BF, BF], tag + "_dple")
        _weight_grad(G, "w_ple_proj", i, S["p"], dpp, W, tag + "_dwpleproj")
        _weight_grad(G, "w_ple_gate", i, S["hn3"], dz, W, tag + "_dwplegate")
        dhn3 = matmul([(dz, W["w_ple_gate"][i])], "nt", F32, tag + "_dplenorm")
        dh2, dh2b, G["g_ple"][i] = rmsnorm_bwd(S["h2"], W["g_ple"][i], dhn3, tag + "_dple_norm", [F32, BF], dres=dh)
        da, db = matmul([(dh2b, W["w_ffn_down"][i])], "nt", BF, tag + "_dswiglu", swiglu=(S["a"], S["b"]))
        _weight_grad(G, "w_ffn_down", i, S["c"], dh2b, W, tag + "_dwdown")
        _weight_grad(G, "w_ffn_gate", i, S["hn2"], da, W, tag + "_dwgate")
        _weight_grad(G, "w_ffn_up", i, S["hn2"], db, W, tag + "_dwup")
        dhn2 = matmul([(da, W["w_ffn_gate"][i]), (db, W["w_ffn_up"][i])], "nt", F32, tag + "_dffnnorm")
        dh1, dh1b, G["g_ffn"][i] = rmsnorm_bwd(S["h1"], W["g_ffn"][i], dhn2, tag + "_dffn_norm", [F32, BF], dres=dh2)
        wo = W[MIXER_OUT[kind]][slot]
        do = matmul([(dh1b, wo)], "nt", BF, tag + "_dmixout")
        _weight_grad(G, MIXER_OUT[kind], slot, S["o"], dh1b, W, tag + "_dwmixout")
        dhn = _mixer_bwd(kind, slot, S["hn"], do, W, S, tabs, tag, G)
        dh, G["g_mix"][i] = rmsnorm_bwd(S["h0"], W["g_mix"][i], dhn, tag + "_dmix_norm", [F32], dres=dh1)
    return loss, dh, G


def _pack_small(vals):
    flat = jnp.concatenate([vals[n].reshape(-1).astype(F32) for n in SMALL])
    rows = -(-flat.shape[0] // SMALL_COLS)
    rows = -(-rows // SUBLANES) * SUBLANES
    return jnp.pad(flat, (0, rows * SMALL_COLS - flat.shape[0])).reshape(rows, SMALL_COLS)


def _unpack_small(packed, like):
    flat = packed.reshape(-1)
    out, off = {}, 0
    for n in SMALL:
        size = like[n].size
        out[n] = flat[off:off + size].reshape(like[n].shape)
        off += size
    return out


def kernel(x, p, positions, g_mix, g_ffn, g_ple, w_ple_gate, w_ple_proj, w_ffn_gate, w_ffn_up, w_ffn_down, a_w_in, a_q_norm, a_k_norm, a_sink, a_w_o, b_w_in, b_q_lat_norm, b_kv_lat_norm, b_w_q_up, b_w_kv_up, b_q_norm, b_k_norm, b_w_o, c_w_in, c_q_norm, c_k_norm, c_w_o, loss_target, m_g_mix, m_g_ffn, m_g_ple, m_w_ple_gate, m_w_ple_proj, m_w_ffn_gate, m_w_ffn_up, m_w_ffn_down, m_a_w_in, m_a_q_norm, m_a_k_norm, m_a_sink, m_a_w_o, m_b_w_in, m_b_q_lat_norm, m_b_kv_lat_norm, m_b_w_q_up, m_b_w_kv_up, m_b_q_norm, m_b_k_norm, m_b_w_o, m_c_w_in, m_c_q_norm, m_c_k_norm, m_c_w_o, v_g_mix, v_g_ffn, v_g_ple, v_w_ple_gate, v_w_ple_proj, v_w_ffn_gate, v_w_ffn_up, v_w_ffn_down, v_a_w_in, v_a_q_norm, v_a_k_norm, v_a_sink, v_a_w_o, v_b_w_in, v_b_q_lat_norm, v_b_kv_lat_norm, v_b_w_q_up, v_b_w_kv_up, v_b_q_norm, v_b_k_norm, v_b_w_o, v_c_w_in, v_c_q_norm, v_c_k_norm, v_c_w_o):
    args = dict(locals())
    w_loc = {n: args[n] for n in WEIGHTS}
    m_loc = {n: args["m_" + n] for n in WEIGHTS}
    v_loc = {n: args["v_" + n] for n in WEIGHTS}

    def halves(a):
        rows = a.shape[0] * a.shape[1]
        return a.reshape(2, rows // 2, a.shape[2])

    gathered = gather_weights([halves(w_loc[n].astype(BF)) for n in BIG])
    W = {n: _full_weight(n, g, w_loc[n].shape) for n, g in zip(BIG, gathered)}
    for n in SMALL:
        W[n] = w_loc[n]

    loss, dx, G = local_step(x[0], p[:, 0], positions[0], loss_target[0], W)
    loss = lax.psum(loss, ("x", "y", "c"))

    core = _core_index()
    parts = []
    for n in BIG:
        s = _grad_slots(n, jnp.stack(G[n])) if isinstance(G[n], list) else G[n]
        part = presum_core_halves(s.reshape(2, 4 * s.shape[2], s.shape[3]), core, "presum_" + n)
        parts.append(part.reshape(s.shape[1:]))
    landed = scatter_partials(parts)
    halves = [sum_and_swap(a, "sum_" + n) for n, a in zip(BIG, landed)]

    small = gather_small(_pack_small({n: jnp.stack(G[n]) for n in SMALL}))
    small_sum = sum_slots(small, F32, "sum_small")
    grads = _unpack_small(small_sum, w_loc)

    delta, new_m, new_v = {}, {}, {}
    for n, (mine, theirs) in zip(BIG, halves):
        shape = w_loc[n].shape
        two_d = (shape[0] * shape[1], shape[2])
        g, d, m, v = adamw_halves(w_loc[n].reshape(two_d), mine, theirs, m_loc[n].reshape(two_d),
                                  v_loc[n].reshape(two_d), core, "adamw_" + n)
        grads[n], delta[n], new_m[n], new_v[n] = g.reshape(shape), d.reshape(shape), m.reshape(shape), v.reshape(shape)
    d, m, v = adamw(_pack_small(w_loc), small_sum, _pack_small(m_loc), _pack_small(v_loc), "adamw_small")
    delta.update(_unpack_small(d, w_loc))
    new_m.update(_unpack_small(m, w_loc))
    new_v.update(_unpack_small(v, w_loc))

    return (loss, dx[None], *[grads[n] for n in WEIGHTS], *[delta[n] for n in WEIGHTS],
            *[new_m[n] for n in WEIGHTS], *[new_v[n] for n in WEIGHTS])
```

```python
import functools
import math

import numpy as np
import jax
import jax.numpy as jnp
from jax import lax
from jax.experimental import pallas as pl
from jax.experimental.pallas import tpu as pltpu

F32 = jnp.float32
BF = jnp.bfloat16

D_MODEL = 2048
DEPTH = 4
HEAD_DIM = 128
ROPE_THETA = 500000.0
PARTIAL_ROT = HEAD_DIM // 4
NORM_EPS = 1e-6
NEG = -1e30
A_HEADS = 16
A_KV_HEADS = 4
A_HALF_WINDOW = 128
B_HEADS = 16
B_Q_RANK = 512
B_KV_RANK = 512
B_NOPE = 128
B_ROPE = 64
B_QK = B_NOPE + B_ROPE
B_PAD = 256
C_PATTERNS = ((128, 1), (512, 4), (2048, 16))
C_HEADS = 16
C_GROUPS = 3
ADAM_LR = 0.001
ADAM_B1 = 0.9
ADAM_B2 = 0.999
ADAM_EPS = 1e-08
ADAM_WD = 0.01
ADAM_STEP = 10

LANES = 128
SUBLANES = 8
VMEM_LIMIT_BYTES = 56 * 1024 * 1024
MATMUL_VMEM_BYTES = 46 * 1024 * 1024
BAND_BLOCK = 256
BAND_HEADS_PER_STEP = 4
DENSE_BLOCK = 1024
DENSE_SUB = 256
MESH = pl.DeviceIdType.MESH

BIG = ("w_ple_gate", "w_ple_proj", "w_ffn_gate", "w_ffn_up", "w_ffn_down", "a_w_in", "a_w_o",
       "b_w_in", "b_w_q_up", "b_w_kv_up", "b_w_o", "c_w_in", "c_w_o")
SHARD_AXIS = {"w_ple_gate": 1, "w_ple_proj": 2, "w_ffn_gate": 2, "w_ffn_up": 2, "w_ffn_down": 1,
              "a_w_in": 2, "a_w_o": 1, "b_w_in": 1, "b_w_q_up": 2, "b_w_kv_up": 2, "b_w_o": 1,
              "c_w_in": 2, "c_w_o": 1}
SMALL = ("g_mix", "g_ffn", "g_ple", "a_q_norm", "a_k_norm", "a_sink", "b_q_lat_norm",
         "b_kv_lat_norm", "b_q_norm", "b_k_norm", "c_q_norm", "c_k_norm")
WEIGHTS = ("g_mix", "g_ffn", "g_ple", "w_ple_gate", "w_ple_proj", "w_ffn_gate", "w_ffn_up",
           "w_ffn_down", "a_w_in", "a_q_norm", "a_k_norm", "a_sink", "a_w_o", "b_w_in",
           "b_q_lat_norm", "b_kv_lat_norm", "b_w_q_up", "b_w_kv_up", "b_q_norm", "b_k_norm",
           "b_w_o", "c_w_in", "c_q_norm", "c_k_norm", "c_w_o")
SMALL_COLS = 1024
SLOT_DIRECT = ("w_ple_gate", "w_ple_proj", "w_ffn_gate", "w_ffn_up", "w_ffn_down")


def _params(semantics):
    return pltpu.CompilerParams(dimension_semantics=semantics, vmem_limit_bytes=VMEM_LIMIT_BYTES)


def _tile(dim, cands=(1024, 1408, 512, 256, 128)):
    for c in cands:
        if dim % c == 0:
            return c
    return dim


def _k_tile(K, bytes_per_k, fixed_bytes):
    for t in (4096, 2816, 2048, 1408, 1024, 512, 256, 128):
        if K % t == 0 and 2 * bytes_per_k * t + fixed_bytes <= MATMUL_VMEM_BYTES:
            return t
    return _tile(K, (128,))


def _row_tile(rows, cols, target_elems=1 << 19):
    best = None
    for t in range(16, rows + 1, 16):
        if rows % t == 0 and t * cols <= target_elems:
            best = t
    return best if best is not None else rows


def _sigmoid(x):
    return 1.0 / (1.0 + jnp.exp(-x))


class Slot:
    def __init__(self, name, layers, layer, rows, cols, col0=0, buf=None):
        self.axis, self.layers, self.layer, self.rows, self.cols, self.col0, self.buf = (
            SHARD_AXIS[name], layers, layer, rows, cols, col0, buf)
        self.srows = rows // 4 if self.axis == 1 else rows
        self.scols = cols if self.axis == 1 else cols // 4
        self.half = layers * self.srows // 2

    def tiles(self, ncols):
        tm = _tile(math.gcd(self.srows, self.half))
        tn = _tile(math.gcd(self.scols, math.gcd(self.col0, ncols)))
        return tm, tn

    def spec(self, tm, tn):
        def index(i, j, k):
            row, col = i * tm, self.col0 + j * tn
            chip = row // self.srows if self.axis == 1 else col // self.scols
            flat = self.layer * self.srows + (row % self.srows if self.axis == 1 else row)
            cb = col // tn if self.axis == 1 else (col % self.scols) // tn
            return flat // self.half, chip, (flat % self.half) // tm, cb

        return pl.BlockSpec((None, None, tm, tn), index)

    def shape(self):
        return jax.ShapeDtypeStruct((2, 4, self.half, self.scols), F32)


def matmul(pairs, mode, out_dtype, name, res=None, swiglu=None, ple=None, slot=None):
    a0, b0 = pairs[0]
    if mode == "nn":
        (M, K), N = a0.shape, b0.shape[1]
    elif mode == "nt":
        (M, K), N = a0.shape, b0.shape[0]
    else:
        (K, M), N = a0.shape, b0.shape[1]
    tm, tn = (_tile(M), _tile(N)) if slot is None else slot.tiles(N)
    n_mn = 2 + (0 if res is None else 2) + (0 if swiglu is None else 2) + (0 if ple is None else 4)
    tk = _k_tile(K, sum(tm * a.dtype.itemsize + tn * b.dtype.itemsize for a, b in pairs), 4 * tm * tn * (1 + n_mn))
    nk = K // tk
    if mode == "nn":
        a_spec = pl.BlockSpec((tm, tk), lambda i, j, k: (i, k))
        b_spec = pl.BlockSpec((tk, tn), lambda i, j, k: (k, j))
        dims = (((1,), (0,)), ((), ()))
    elif mode == "nt":
        a_spec = pl.BlockSpec((tm, tk), lambda i, j, k: (i, k))
        b_spec = pl.BlockSpec((tn, tk), lambda i, j, k: (j, k))
        dims = (((1,), (1,)), ((), ()))
    else:
        a_spec = pl.BlockSpec((tk, tm), lambda i, j, k: (k, i))
        b_spec = pl.BlockSpec((tk, tn), lambda i, j, k: (k, j))
        dims = (((0,), (0,)), ((), ()))
    mn_spec = pl.BlockSpec((tm, tn), lambda i, j, k: (i, j))
    npairs = len(pairs)
    extras = [] if res is None else [res]
    if swiglu is not None:
        extras = list(swiglu)
    if ple is not None:
        extras = list(ple)
    nex = len(extras)
    nout = 2 if (swiglu is not None or ple is not None) else 1
    carried = slot is not None and slot.buf is not None

    def body(*refs):
        ins = refs[:2 * npairs]
        ex = refs[2 * npairs:2 * npairs + nex]
        first_out = 2 * npairs + nex + (1 if carried else 0)
        outs = refs[first_out:first_out + nout]
        acc = refs[-1]
        k = pl.program_id(2)

        @pl.when(k == 0)
        def _():
            acc[...] = jnp.zeros_like(acc)

        part = None
        for p in range(npairs):
            d = lax.dot_general(ins[2 * p][...].astype(BF), ins[2 * p + 1][...].astype(BF), dims,
                                preferred_element_type=F32)
            part = d if part is None else part + d
        acc[...] += part

        @pl.when(k == nk - 1)
        def _():
            r = acc[...]
            if swiglu is not None:
                a = ex[0][...].astype(F32)
                b = ex[1][...].astype(F32)
                sg = _sigmoid(a)
                outs[0][...] = (r * b * (sg * (1.0 + a * (1.0 - sg)))).astype(out_dtype)
                outs[1][...] = (r * (a * sg)).astype(out_dtype)
            elif ple is not None:
                outs[0][...] = r.astype(out_dtype)
                outs[1][...] = ex[0][...] + _sigmoid(r) * ex[1][...].astype(F32)
            elif res is not None:
                outs[0][...] = (ex[0][...] + r).astype(out_dtype)
            else:
                outs[0][...] = r.astype(outs[0].dtype)

    in_specs = []
    operands = []
    for a, b in pairs:
        in_specs += [a_spec, b_spec]
        operands += [a, b]
    in_specs += [mn_spec] * nex
    operands += extras
    out_shape = [jax.ShapeDtypeStruct((M, N), out_dtype)] * nout
    out_specs = [mn_spec] * nout
    aliases = {}
    if ple is not None:
        out_shape[1] = jax.ShapeDtypeStruct((M, N), F32)
    if slot is not None:
        out_shape, out_specs = [slot.shape()], [slot.spec(tm, tn)]
        if carried:
            aliases = {len(operands): 0}
            in_specs.append(pl.BlockSpec(memory_space=pl.ANY))
            operands.append(slot.buf)
    outs = pl.pallas_call(
        body, out_shape=out_shape, grid=(M // tm, N // tn, nk), in_specs=in_specs,
        out_specs=out_specs, scratch_shapes=[pltpu.VMEM((tm, tn), F32)], input_output_aliases=aliases,
        compiler_params=_params(("parallel", "parallel", "arbitrary")), name=name,
    )(*operands)
    return outs if nout > 1 else outs[0]


def matmul_swiglu(x, wg, wu, name):
    (M, K), N = x.shape, wg.shape[1]
    tm, tn, tk = _tile(M), _tile(N), _tile(K, (1024, 512, 256, 128))
    nk = K // tk

    def body(x_ref, g_ref, u_ref, a_ref, b_ref, c_ref, acc_g, acc_u):
        k = pl.program_id(2)

        @pl.when(k == 0)
        def _():
            acc_g[...] = jnp.zeros_like(acc_g)
            acc_u[...] = jnp.zeros_like(acc_u)

        xv = x_ref[...].astype(BF)
        acc_g[...] += jnp.dot(xv, g_ref[...].astype(BF), preferred_element_type=F32)
        acc_u[...] += jnp.dot(xv, u_ref[...].astype(BF), preferred_element_type=F32)

        @pl.when(k == nk - 1)
        def _():
            a, b = acc_g[...], acc_u[...]
            a_ref[...] = a.astype(a_ref.dtype)
            b_ref[...] = b.astype(b_ref.dtype)
            c_ref[...] = (a * _sigmoid(a) * b).astype(c_ref.dtype)

    w_spec = pl.BlockSpec((tk, tn), lambda i, j, k: (k, j))
    mn_spec = pl.BlockSpec((tm, tn), lambda i, j, k: (i, j))
    return pl.pallas_call(
        body, out_shape=[jax.ShapeDtypeStruct((M, N), BF)] * 3, grid=(M // tm, N // tn, nk),
        in_specs=[pl.BlockSpec((tm, tk), lambda i, j, k: (i, k)), w_spec, w_spec], out_specs=[mn_spec] * 3,
        scratch_shapes=[pltpu.VMEM((tm, tn), F32)] * 2,
        compiler_params=_params(("parallel", "parallel", "arbitrary")), name=name,
    )(x, wg, wu)


def rowwise(fn, ins, out_dtypes, name):
    rows, cols = ins[0].shape
    tr = _row_tile(rows, cols)
    nin = len(ins)

    def body(*refs):
        vals = fn(*[r[...] for r in refs[:nin]])
        for o, v in zip(refs[nin:], vals):
            o[...] = v.astype(o.dtype)

    spec = pl.BlockSpec((tr, cols), lambda i: (i, 0))
    outs = pl.pallas_call(
        body, out_shape=[jax.ShapeDtypeStruct((rows, cols), d) for d in out_dtypes],
        grid=(rows // tr,), in_specs=[spec] * nin, out_specs=[spec] * len(out_dtypes),
        compiler_params=_params(("parallel",)), name=name,
    )(*ins)
    return outs


def _ple_bwd_fn(dh, z, pp):
    gate = _sigmoid(z.astype(F32))
    return (dh * pp.astype(F32) * gate * (1.0 - gate), dh * gate)


def _merge_fn(o0, o1, o2, l0, l1, l2):
    m = jnp.maximum(jnp.maximum(l0, l1), l2)
    e0, e1, e2 = jnp.exp(l0 - m), jnp.exp(l1 - m), jnp.exp(l2 - m)
    den = e0 + e1 + e2
    return ((e0 * o0 + e1 * o1 + e2 * o2) / den, m + jnp.log(den))


def rmsnorm_fwd(x, g, name, col_block=0, width=None):
    T = x.shape[0]
    W = x.shape[1] if width is None else width
    tt = _row_tile(T, W)

    def body(x_ref, g_ref, y_ref):
        xf = x_ref[...].astype(F32)
        ms = jnp.mean(xf * xf, axis=-1, keepdims=True)
        y_ref[...] = (xf * lax.rsqrt(ms + NORM_EPS) * g_ref[...]).astype(y_ref.dtype)

    return pl.pallas_call(
        body, out_shape=jax.ShapeDtypeStruct((T, W), BF), grid=(T // tt,),
        in_specs=[pl.BlockSpec((tt, W), lambda i: (i, col_block)), pl.BlockSpec((1, W), lambda i: (0, 0))],
        out_specs=pl.BlockSpec((tt, W), lambda i: (i, 0)),
        compiler_params=_params(("parallel",)), name=name,
    )(x, g.reshape(1, W).astype(F32))


def rmsnorm_bwd(x, g, dy, name, out_dtypes, dres=None, col_block=0, width=None):
    T = x.shape[0]
    W = x.shape[1] if width is None else width
    tt = _row_tile(T, W, 1 << 18)
    nout = len(out_dtypes)
    has_res = dres is not None

    def body(*refs):
        x_ref, g_ref, dy_ref = refs[:3]
        pos = 3
        res_ref = None
        if has_res:
            res_ref = refs[3]
            pos = 4
        dx_refs = refs[pos:pos + nout]
        dg_ref = refs[pos + nout]
        xf = x_ref[...].astype(F32)
        rstd = lax.rsqrt(jnp.mean(xf * xf, axis=-1, keepdims=True) + NORM_EPS)
        xhat = xf * rstd
        dyf = dy_ref[...].astype(F32)
        dn = dyf * g_ref[...]
        dx = rstd * (dn - xhat * jnp.mean(dn * xhat, axis=-1, keepdims=True))
        if has_res:
            dx = dx + res_ref[...]
        for o in dx_refs:
            o[...] = dx.astype(o.dtype)

        @pl.when(pl.program_id(0) == 0)
        def _():
            dg_ref[...] = jnp.zeros_like(dg_ref)

        dg_ref[...] += jnp.broadcast_to(jnp.sum(dyf * xhat, axis=0, keepdims=True), dg_ref.shape)

    row = pl.BlockSpec((tt, W), lambda i: (i, 0))
    in_specs = [pl.BlockSpec((tt, W), lambda i: (i, col_block)), pl.BlockSpec((1, W), lambda i: (0, 0)), row]
    operands = [x, g.reshape(1, W).astype(F32), dy]
    if has_res:
        in_specs.append(row)
        operands.append(dres)
    outs = pl.pallas_call(
        body,
        out_shape=[jax.ShapeDtypeStruct((T, W), d) for d in out_dtypes] + [jax.ShapeDtypeStruct((SUBLANES, W), F32)],
        grid=(T // tt,), in_specs=in_specs,
        out_specs=[row] * nout + [pl.BlockSpec((SUBLANES, W), lambda i: (0, 0))],
        compiler_params=_params(("arbitrary",)), name=name,
    )(*operands)
    return tuple(outs[:nout]) + (outs[nout][0],)


def loss_and_grad(y, target, name):
    T, D = y.shape
    tt = _row_tile(T, D)

    def body(y_ref, t_ref, loss_ref, dy_ref):
        d = y_ref[...] - t_ref[...]
        dy_ref[...] = d * (1.0 / D)

        @pl.when(pl.program_id(0) == 0)
        def _():
            loss_ref[...] = jnp.zeros_like(loss_ref)

        loss_ref[...] += jnp.full(loss_ref.shape, 0.5 / D, F32) * jnp.sum(d * d)

    row = pl.BlockSpec((tt, D), lambda i: (i, 0))
    loss, dy = pl.pallas_call(
        body, out_shape=[jax.ShapeDtypeStruct((SUBLANES, LANES), F32), jax.ShapeDtypeStruct((T, D), F32)],
        grid=(T // tt,), in_specs=[row, row],
        out_specs=[pl.BlockSpec((SUBLANES, LANES), lambda i: (0, 0)), row],
        compiler_params=_params(("arbitrary",)), name=name,
    )(y, target)
    return loss[0, 0], dy


def rope_tables(pos, width, r0, rot_dim):
    half = rot_dim // 2
    inv = ROPE_THETA ** (-jnp.arange(half, dtype=F32) * 2.0 / rot_dim)
    ang = pos.astype(F32)[:, None] * inv
    cos, sin = jnp.cos(ang), jnp.sin(ang)
    T = pos.shape[0]
    ones_l, ones_r = jnp.ones((T, r0), F32), jnp.ones((T, width - r0 - rot_dim), F32)
    c_tab = jnp.concatenate([ones_l, cos, cos, ones_r], axis=1)
    s_tab = jnp.concatenate([0 * ones_l, -sin, sin, 0 * ones_r], axis=1)
    perm = np.zeros((width, width), np.float32)
    for j in range(half):
        perm[r0 + j + half, r0 + j] = 1.0
        perm[r0 + j, r0 + j + half] = 1.0
    return c_tab, s_tab, jnp.asarray(perm, BF)


def _lane_permute(v, perm):
    hi = v.astype(BF)
    lo = (v - hi.astype(F32)).astype(BF)
    return (jnp.dot(hi, perm, preferred_element_type=F32) + jnp.dot(lo, perm, preferred_element_type=F32))


def headnorm_fwd(x, g, tabs, name, heads, col0, width, n_true):
    c_tab, s_tab, perm = tabs
    T = x.shape[0]
    tt = _tile(T, (1024, 512, 256, 128))
    inv_n = 1.0 / n_true

    def body(x_ref, g_ref, c_ref, s_ref, p_ref, y_ref):
        xf = x_ref[...].astype(F32)
        rstd = lax.rsqrt(jnp.sum(xf * xf, axis=-1, keepdims=True) * inv_n + NORM_EPS)
        n = xf * rstd * g_ref[...]
        y_ref[...] = (n * c_ref[...] + _lane_permute(n, p_ref[...]) * s_ref[...]).astype(y_ref.dtype)

    tab = pl.BlockSpec((tt, width), lambda i, h: (i, 0))
    return pl.pallas_call(
        body, out_shape=jax.ShapeDtypeStruct((T, heads * width), BF), grid=(T // tt, heads),
        in_specs=[pl.BlockSpec((tt, width), lambda i, h: (i, col0 + h)),
                  pl.BlockSpec((1, width), lambda i, h: (0, 0)), tab, tab,
                  pl.BlockSpec((width, width), lambda i, h: (0, 0))],
        out_specs=pl.BlockSpec((tt, width), lambda i, h: (i, h)),
        compiler_params=_params(("parallel", "parallel")), name=name,
    )(x, g.reshape(1, width).astype(F32), c_tab, s_tab, perm)


def headnorm_bwd(x, g, tabs, dy, name, heads, col0, width, n_true, head_sum=False, into=None):
    c_tab, s_tab, perm = tabs
    T = x.shape[0]
    tt = _tile(T, (1024, 512, 256, 128))
    inv_n = 1.0 / n_true
    buf, blocks, block0 = into if into is not None else (None, heads, 0)
    carried = buf is not None

    def body(*refs):
        x_ref, g_ref, c_ref, s_ref, p_ref, dy_ref = refs[:6]
        dx_ref, dg_ref = refs[7:9] if carried else refs[6:8]
        i, h = pl.program_id(0), pl.program_id(1)
        xf = x_ref[...].astype(F32)
        rstd = lax.rsqrt(jnp.sum(xf * xf, axis=-1, keepdims=True) * inv_n + NORM_EPS)
        xhat = xf * rstd
        dyf = dy_ref[...].astype(F32)
        dn = dyf * c_ref[...] + _lane_permute(dyf * s_ref[...], p_ref[...])
        dxh = dn * g_ref[...]
        dx = rstd * (dxh - xhat * (jnp.sum(dxh * xhat, axis=-1, keepdims=True) * inv_n))
        dx_ref[...] = dx.astype(dx_ref.dtype)

        @pl.when(jnp.logical_and(i == 0, h == 0))
        def _():
            dg_ref[...] = jnp.zeros_like(dg_ref)

        dg_ref[...] += jnp.broadcast_to(jnp.sum(dn * xhat, axis=0, keepdims=True), dg_ref.shape)
        if head_sum:
            sum_ref = refs[-1]

            @pl.when(h == 0)
            def _():
                sum_ref[...] = jnp.zeros_like(sum_ref)

            sum_ref[...] += dx

    tab = pl.BlockSpec((tt, width), lambda i, h: (i, 0))
    out_shape = [jax.ShapeDtypeStruct((T, blocks * width), BF), jax.ShapeDtypeStruct((SUBLANES, width), F32)]
    out_specs = [pl.BlockSpec((tt, width), lambda i, h: (i, block0 + h)),
                 pl.BlockSpec((SUBLANES, width), lambda i, h: (0, 0))]
    if head_sum:
        out_shape.append(jax.ShapeDtypeStruct((T, width), F32))
        out_specs.append(tab)
    in_specs = [pl.BlockSpec((tt, width), lambda i, h: (i, col0 + h)),
                pl.BlockSpec((1, width), lambda i, h: (0, 0)), tab, tab,
                pl.BlockSpec((width, width), lambda i, h: (0, 0)),
                pl.BlockSpec((tt, width), lambda i, h: (i, h))]
    operands = [x, g.reshape(1, width).astype(F32), c_tab, s_tab, perm, dy]
    if carried:
        in_specs.append(pl.BlockSpec(memory_space=pl.ANY))
        operands.append(buf)
    outs = pl.pallas_call(
        body, out_shape=out_shape, grid=(T // tt, heads), in_specs=in_specs, out_specs=out_specs,
        input_output_aliases={6: 0} if carried else {},
        compiler_params=_params(("arbitrary", "arbitrary")), name=name,
    )(*operands)
    return (outs[0], outs[1][0]) + ((outs[2],) if head_sum else ())


class Attn:
    def __init__(self, T, dil, hq, group, qc, q0, kc, k0, vc, v0, vstride, dqk, scale, half_window, blk):
        self.T, self.dil, self.hq, self.group = T, dil, hq, group
        self.hkv = hq // group
        self.qc, self.q0, self.kc, self.k0, self.vc, self.v0, self.vstride = qc, q0, kc, k0, vc, v0, vstride
        self.dqk, self.scale, self.hw = dqk, scale, half_window
        self.len = T // dil
        self.blk = min(blk, self.len)
        self.nb = self.len // self.blk
        self.band = half_window is not None
        self.steps = 3 if self.band else self.nb

    def other(self, i, s):
        if self.band:
            nom = i - 1 + s
            return jnp.minimum(jnp.maximum(nom, 0), self.nb - 1), nom
        return s, s

    def chains(self, a):
        return a.reshape(self.len, self.dil * a.shape[1])

    def row_chunks(self):
        assert not self.band
        sub = min(DENSE_SUB, self.blk)
        return [slice(c * sub, (c + 1) * sub) for c in range(self.blk // sub)]

    def unchain(self, a, cols):
        return a.reshape(self.T, cols)

    def mask(self, q_nom, k_nom):
        if not self.band:
            return None
        qpos = q_nom * self.blk + lax.broadcasted_iota(jnp.int32, (self.blk, self.blk), 0)
        kpos = k_nom * self.blk + lax.broadcasted_iota(jnp.int32, (self.blk, self.blk), 1)
        ok = jnp.abs(qpos - kpos) <= self.hw
        for pos in (qpos, kpos):
            ok = jnp.logical_and(ok, jnp.logical_and(pos >= 0, pos < self.len))
        return ok


def _scores(cfg, q, k, q_nom, k_nom):
    s = lax.dot_general(q, k, (((1,), (1,)), ((), ())), preferred_element_type=F32) * cfg.scale
    ok = cfg.mask(q_nom, k_nom)
    return s if ok is None else jnp.where(ok, s, NEG)


def flash_fwd(cfg, q, k, v, name, out_dtype, sink=None):
    blk, dqk = cfg.blk, cfg.dqk
    has_sink = sink is not None

    def body(*refs):
        if has_sink:
            sink_ref, refs = refs[0], refs[1:]
        q_ref, k_ref, v_ref, o_ref, lse_ref, m_sc, l_sc, acc_sc = refs
        i, s = pl.program_id(2), pl.program_id(3)

        @pl.when(s == 0)
        def _():
            if has_sink:
                m_sc[...] = jnp.broadcast_to(sink_ref[0, :1, :], m_sc.shape)
                l_sc[...] = jnp.ones_like(l_sc)
            else:
                m_sc[...] = jnp.full(m_sc.shape, NEG, F32)
                l_sc[...] = jnp.zeros_like(l_sc)
            acc_sc[...] = jnp.zeros_like(acc_sc)

        _, k_nom = cfg.other(i, s)
        k, v = k_ref[...], v_ref[...]
        for rows in cfg.row_chunks():
            sc = _scores(cfg, q_ref[rows, :], k, i, k_nom)
            m_prev = m_sc[rows, :]
            m_new = jnp.maximum(m_prev, jnp.max(sc, axis=-1, keepdims=True))
            p = jnp.exp(sc - m_new[:, :1])
            alpha = jnp.exp(m_prev - m_new)
            l_sc[rows, :] = alpha * l_sc[rows, :] + jnp.sum(p, axis=-1, keepdims=True)
            acc_sc[rows, :] = alpha * acc_sc[rows, :] + jnp.dot(p.astype(BF), v, preferred_element_type=F32)
            m_sc[rows, :] = m_new

        @pl.when(s == cfg.steps - 1)
        def _():
            o_ref[...] = (acc_sc[...] / l_sc[...]).astype(o_ref.dtype)
            lse_ref[...] = m_sc[...] + jnp.log(l_sc[...])

    g = cfg.group
    q_spec = pl.BlockSpec((blk, dqk), lambda r, h, i, s: (i, r * cfg.qc + cfg.q0 + h))
    k_spec = pl.BlockSpec((blk, dqk), lambda r, h, i, s: (cfg.other(i, s)[0], r * cfg.kc + cfg.k0 + h // g))
    v_spec = pl.BlockSpec((blk, LANES),
                          lambda r, h, i, s: (cfg.other(i, s)[0], r * cfg.vc + cfg.v0 + cfg.vstride * (h // g)))
    o_spec = pl.BlockSpec((blk, LANES), lambda r, h, i, s: (i, r * cfg.hq + h))
    in_specs = [q_spec, k_spec, v_spec]
    operands = [cfg.chains(q), cfg.chains(k), cfg.chains(v)]
    if has_sink:
        in_specs.insert(0, pl.BlockSpec((1, SUBLANES, LANES), lambda r, h, i, s: (h, 0, 0)))
        operands.insert(0, sink)
    cols = cfg.dil * cfg.hq * LANES
    o, lse = pl.pallas_call(
        body, out_shape=[jax.ShapeDtypeStruct((cfg.len, cols), out_dtype), jax.ShapeDtypeStruct((cfg.len, cols), F32)],
        grid=(cfg.dil, cfg.hq, cfg.nb, cfg.steps), in_specs=in_specs, out_specs=[o_spec, o_spec],
        scratch_shapes=[pltpu.VMEM((blk, LANES), F32)] * 3,
        compiler_params=_params(("parallel", "parallel", "parallel", "arbitrary")), name=name,
    )(*operands)
    return cfg.unchain(o, cfg.hq * LANES), cfg.unchain(lse, cfg.hq * LANES)


def flash_dq(cfg, q, k, v, do, o, lse, name, sink=None):
    blk, dqk = cfg.blk, cfg.dqk
    has_sink = sink is not None

    def body(*refs):
        if has_sink:
            sink_ref, refs = refs[0], refs[1:]
        q_ref, k_ref, v_ref, do_ref, o_ref, lse_ref = refs[:6]
        dq_ref = refs[6]
        dq_sc, delta_sc = refs[-2:]
        i, s = pl.program_id(2), pl.program_id(3)

        @pl.when(s == 0)
        def _():
            dq_sc[...] = jnp.zeros_like(dq_sc)
            delta = jnp.sum(do_ref[...].astype(F32) * o_ref[...].astype(F32), axis=-1, keepdims=True)
            delta_sc[...] = jnp.broadcast_to(delta, delta_sc.shape)

        _, k_nom = cfg.other(i, s)
        k, v = k_ref[...], v_ref[...]
        for rows in cfg.row_chunks():
            sc = _scores(cfg, q_ref[rows, :], k, i, k_nom)
            p = jnp.exp(sc - lse_ref[rows, :1])
            dp = lax.dot_general(do_ref[rows, :], v, (((1,), (1,)), ((), ())), preferred_element_type=F32)
            ds = p * (dp - delta_sc[rows, :1]) * cfg.scale
            dq_sc[rows, :] += jnp.dot(ds.astype(BF), k, preferred_element_type=F32)

        @pl.when(s == cfg.steps - 1)
        def _():
            dq_ref[...] = dq_sc[...].astype(dq_ref.dtype)
            if has_sink:
                ps = jnp.exp(sink_ref[0, :1, :] - lse_ref[...])
                part = -jnp.sum(ps * delta_sc[...], axis=0, keepdims=True)
                refs[7][...] = jnp.broadcast_to(part, refs[7].shape)

    g = cfg.group
    q_spec = pl.BlockSpec((blk, dqk), lambda r, h, i, s: (i, r * cfg.qc + cfg.q0 + h))
    k_spec = pl.BlockSpec((blk, dqk), lambda r, h, i, s: (cfg.other(i, s)[0], r * cfg.kc + cfg.k0 + h // g))
    v_spec = pl.BlockSpec((blk, LANES),
                          lambda r, h, i, s: (cfg.other(i, s)[0], r * cfg.vc + cfg.v0 + cfg.vstride * (h // g)))
    o_spec = pl.BlockSpec((blk, LANES), lambda r, h, i, s: (i, r * cfg.hq + h))
    dq_spec = pl.BlockSpec((blk, dqk), lambda r, h, i, s: (i, r * cfg.hq + h))
    in_specs = [q_spec, k_spec, v_spec, o_spec, o_spec, o_spec]
    operands = [cfg.chains(q), cfg.chains(k), cfg.chains(v), cfg.chains(do), cfg.chains(o), cfg.chains(lse)]
    out_shape = [jax.ShapeDtypeStruct((cfg.len, cfg.dil * cfg.hq * dqk), BF)]
    out_specs = [dq_spec]
    if has_sink:
        in_specs.insert(0, pl.BlockSpec((1, SUBLANES, LANES), lambda r, h, i, s: (h, 0, 0)))
        operands.insert(0, sink)
        out_shape.append(jax.ShapeDtypeStruct((cfg.hq * cfg.nb * SUBLANES, LANES), F32))
        out_specs.append(pl.BlockSpec((SUBLANES, LANES), lambda r, h, i, s: (h * cfg.nb + i, 0)))
    outs = pl.pallas_call(
        body, out_shape=out_shape, grid=(cfg.dil, cfg.hq, cfg.nb, cfg.steps), in_specs=in_specs,
        out_specs=out_specs, scratch_shapes=[pltpu.VMEM((blk, dqk), F32), pltpu.VMEM((blk, LANES), F32)],
        compiler_params=_params(("parallel", "parallel", "parallel", "arbitrary")), name=name,
    )(*operands)
    dq = cfg.unchain(outs[0], cfg.hq * dqk)
    if has_sink:
        return dq, outs[1].reshape(cfg.hq, cfg.nb, SUBLANES, LANES)[:, :, 0, :]
    return dq


def flash_dkv(cfg, q, k, v, do, o, lse, name, out_dtype, add=None):
    blk, dqk, g, nw = cfg.blk, cfg.dqk, cfg.group, cfg.steps
    has_add = add is not None

    def body(*refs):
        k_ref, v_ref, q_ref, do_ref, o_ref, lse_ref = refs[:6]
        pos = 8 if has_add else 6
        dk_ref, dv_ref = refs[pos:pos + 2]
        dk_sc, dv_sc = refs[-2:]
        i, j = pl.program_id(2), pl.program_id(3)

        @pl.when(j == 0)
        def _():
            dk_sc[...] = jnp.zeros_like(dk_sc)
            dv_sc[...] = jnp.zeros_like(dv_sc)

        _, q_nom = cfg.other(i, j % nw)
        k, v = k_ref[...], v_ref[...]
        for rows in cfg.row_chunks():
            q = q_ref[rows, :]
            do = do_ref[rows, :]
            sc = _scores(cfg, q, k, q_nom, i)
            p = jnp.exp(sc - lse_ref[rows, :1])
            delta = jnp.sum(do.astype(F32) * o_ref[rows, :].astype(F32), axis=-1, keepdims=True)
            dv_sc[...] += lax.dot_general(p.astype(BF), do, (((0,), (0,)), ((), ())), preferred_element_type=F32)
            dp = lax.dot_general(do, v, (((1,), (1,)), ((), ())), preferred_element_type=F32)
            ds = p * (dp - delta) * cfg.scale
            dk_sc[...] += lax.dot_general(ds.astype(BF), q, (((0,), (0,)), ((), ())), preferred_element_type=F32)

        @pl.when(j == g * nw - 1)
        def _():
            dk, dv = dk_sc[...], dv_sc[...]
            if has_add:
                dk, dv = dk + refs[6][...].astype(F32), dv + refs[7][...].astype(F32)
            dk_ref[...] = dk.astype(dk_ref.dtype)
            dv_ref[...] = dv.astype(dv_ref.dtype)

    def qrow(i, j):
        return cfg.other(i, j % nw)[0]

    k_spec = pl.BlockSpec((blk, dqk), lambda r, h, i, j: (i, r * cfg.kc + cfg.k0 + h))
    v_spec = pl.BlockSpec((blk, LANES), lambda r, h, i, j: (i, r * cfg.vc + cfg.v0 + cfg.vstride * h))
    q_spec = pl.BlockSpec((blk, dqk), lambda r, h, i, j: (qrow(i, j), r * cfg.qc + cfg.q0 + h * g + j // nw))
    o_spec = pl.BlockSpec((blk, LANES), lambda r, h, i, j: (qrow(i, j), r * cfg.hq + h * g + j // nw))
    dk_spec = pl.BlockSpec((blk, dqk), lambda r, h, i, j: (i, r * cfg.hkv + h))
    dv_spec = pl.BlockSpec((blk, LANES), lambda r, h, i, j: (i, r * cfg.hkv + h))
    in_specs = [k_spec, v_spec, q_spec, o_spec, o_spec, o_spec]
    operands = [cfg.chains(k), cfg.chains(v), cfg.chains(q), cfg.chains(do), cfg.chains(o), cfg.chains(lse)]
    if has_add:
        in_specs += [dk_spec, dv_spec]
        operands += [cfg.chains(add[0]), cfg.chains(add[1])]
    dk, dv = pl.pallas_call(
        body,
        out_shape=[jax.ShapeDtypeStruct((cfg.len, cfg.dil * cfg.hkv * dqk), out_dtype),
                   jax.ShapeDtypeStruct((cfg.len, cfg.dil * cfg.hkv * LANES), out_dtype)],
        grid=(cfg.dil, cfg.hkv, cfg.nb, g * nw), in_specs=in_specs, out_specs=[dk_spec, dv_spec],
        scratch_shapes=[pltpu.VMEM((blk, dqk), F32), pltpu.VMEM((blk, LANES), F32)],
        compiler_params=_params(("parallel", "parallel", "parallel", "arbitrary")), name=name,
    )(*operands)
    return cfg.unchain(dk, cfg.hkv * dqk), cfg.unchain(dv, cfg.hkv * LANES)


class Band:
    def __init__(self, T, dil, hq, group, per, qc, q0, kc, k0, vc, v0, scale, hw, blk):
        self.T, self.dil, self.hq, self.group, self.per = T, dil, hq, group, per
        self.pk = per // group
        self.hkv = hq // group
        self.scale, self.hw = scale, hw
        self.len = T // dil
        self.blk = min(blk, self.len)
        self.nb = self.len // self.blk
        self.win = self.blk + 2 * hw
        self.qcol = lambda r: (r * qc + q0) // per
        self.kcol = lambda r: (r * kc + k0) // self.pk
        self.vcol = lambda r: (r * vc + v0) // self.pk
        self.ocol = lambda r: (r * hq) // per
        self.dkcol = lambda r: (r * self.hkv) // self.pk
        assert hw <= self.blk and qc % per == 0 and q0 % per == 0 and kc % self.pk == 0 and k0 % self.pk == 0
        assert vc % self.pk == 0 and v0 % self.pk == 0

    def chains(self, a):
        return a.reshape(self.len, self.dil * a.shape[1])

    def rows3(self, width, col):
        nb = self.nb
        return [pl.BlockSpec((self.blk, width), lambda r, h, i: (jnp.maximum(i - 1, 0), col(r) + h)),
                pl.BlockSpec((self.blk, width), lambda r, h, i: (i, col(r) + h)),
                pl.BlockSpec((self.blk, width), lambda r, h, i: (jnp.minimum(i + 1, nb - 1), col(r) + h))]

    def window(self, prev, cur, nxt, j):
        cols = slice(j * LANES, (j + 1) * LANES)
        return jnp.concatenate([prev[self.blk - self.hw:, cols], cur[:, cols], nxt[:self.hw, cols]], axis=0)

    def valid(self, i, window_is_rows):
        shape = (self.win, self.blk) if window_is_rows else (self.blk, self.win)
        wdim = 0 if window_is_rows else 1
        bpos = i * self.blk + lax.broadcasted_iota(jnp.int32, shape, 1 - wdim)
        wpos = i * self.blk - self.hw + lax.broadcasted_iota(jnp.int32, shape, wdim)
        ok = jnp.abs(bpos - wpos) <= self.hw
        return jnp.logical_and(ok, jnp.logical_and(wpos >= 0, wpos < self.len))


def band_fwd(cfg, q, k, v, name, out_dtype, sink=None):
    blk, per, pk = cfg.blk, cfg.per, cfg.pk
    has_sink = sink is not None

    def body(*refs):
        if has_sink:
            sink_ref, refs = refs[0], refs[1:]
        q_ref, kp, kc, kn, vp, vc, vn, o_ref, lse_ref = refs
        ok = cfg.valid(pl.program_id(2), False)
        for j in range(per):
            jk = j // cfg.group
            if j % cfg.group == 0:
                kw = cfg.window(kp, kc, kn, jk)
                vw = cfg.window(vp, vc, vn, jk)
            cols = slice(j * LANES, (j + 1) * LANES)
            s = lax.dot_general(q_ref[:, cols], kw, (((1,), (1,)), ((), ())), preferred_element_type=F32) * cfg.scale
            s = jnp.where(ok, s, NEG)
            m = jnp.max(s, axis=-1, keepdims=True)
            if has_sink:
                sk = sink_ref[j, :1, :1]
                m = jnp.maximum(m, sk)
            e = jnp.exp(s - m)
            den = jnp.sum(e, axis=-1, keepdims=True)
            if has_sink:
                den = den + jnp.exp(sk - m)
            o = jnp.dot(e.astype(BF), vw, preferred_element_type=F32) / den
            o_ref[:, cols] = o.astype(o_ref.dtype)
            lse_ref[:, cols] = jnp.broadcast_to(m + jnp.log(den), (blk, LANES))

    q_spec = pl.BlockSpec((blk, per * LANES), lambda r, h, i: (i, cfg.qcol(r) + h))
    o_spec = pl.BlockSpec((blk, per * LANES), lambda r, h, i: (i, cfg.ocol(r) + h))
    in_specs = [q_spec] + cfg.rows3(pk * LANES, cfg.kcol) + cfg.rows3(pk * LANES, cfg.vcol)
    kc_, vc_ = cfg.chains(k), cfg.chains(v)
    operands = [cfg.chains(q), kc_, kc_, kc_, vc_, vc_, vc_]
    if has_sink:
        in_specs.insert(0, pl.BlockSpec((per, SUBLANES, LANES), lambda r, h, i: (h, 0, 0)))
        operands.insert(0, sink)
    cols = cfg.dil * cfg.hq * LANES
    o, lse = pl.pallas_call(
        body, out_shape=[jax.ShapeDtypeStruct((cfg.len, cols), out_dtype), jax.ShapeDtypeStruct((cfg.len, cols), F32)],
        grid=(cfg.dil, cfg.hq // per, cfg.nb), in_specs=in_specs, out_specs=[o_spec, o_spec],
        compiler_params=_params(("parallel", "parallel", "parallel")), name=name,
    )(*operands)
    return o.reshape(cfg.T, cfg.hq * LANES), lse.reshape(cfg.T, cfg.hq * LANES)


def band_dq(cfg, q, k, v, do, o, lse, name, sink=None):
    blk, per, pk = cfg.blk, cfg.per, cfg.pk
    has_sink = sink is not None

    def body(*refs):
        if has_sink:
            sink_ref, refs = refs[0], refs[1:]
        q_ref, kp, kc, kn, vp, vc, vn, do_ref, o_ref, lse_ref, dq_ref = refs[:11]
        ok = cfg.valid(pl.program_id(2), False)
        for j in range(per):
            jk = j // cfg.group
            if j % cfg.group == 0:
                kw = cfg.window(kp, kc, kn, jk)
                vw = cfg.window(vp, vc, vn, jk)
            cols = slice(j * LANES, (j + 1) * LANES)
            do = do_ref[:, cols]
            lse = lse_ref[:, j * LANES:j * LANES + 1]
            delta = jnp.sum(do.astype(F32) * o_ref[:, cols].astype(F32), axis=-1, keepdims=True)
            s = lax.dot_general(q_ref[:, cols], kw, (((1,), (1,)), ((), ())), preferred_element_type=F32) * cfg.scale
            p = jnp.exp(jnp.where(ok, s, NEG) - lse)
            dp = lax.dot_general(do, vw, (((1,), (1,)), ((), ())), preferred_element_type=F32)
            ds = p * (dp - delta) * cfg.scale
            dq_ref[:, cols] = jnp.dot(ds.astype(BF), kw, preferred_element_type=F32).astype(dq_ref.dtype)
            if has_sink:
                part = -jnp.sum(jnp.exp(sink_ref[j, :1, :1] - lse) * delta, axis=0, keepdims=True)
                refs[11][j * SUBLANES:(j + 1) * SUBLANES, :] = jnp.broadcast_to(part, (SUBLANES, LANES))

    q_spec = pl.BlockSpec((blk, per * LANES), lambda r, h, i: (i, cfg.qcol(r) + h))
    o_spec = pl.BlockSpec((blk, per * LANES), lambda r, h, i: (i, cfg.ocol(r) + h))
    in_specs = [q_spec] + cfg.rows3(pk * LANES, cfg.kcol) + cfg.rows3(pk * LANES, cfg.vcol) + [o_spec] * 3
    kc_, vc_ = cfg.chains(k), cfg.chains(v)
    operands = [cfg.chains(q), kc_, kc_, kc_, vc_, vc_, vc_, cfg.chains(do), cfg.chains(o), cfg.chains(lse)]
    out_shape = [jax.ShapeDtypeStruct((cfg.len, cfg.dil * cfg.hq * LANES), BF)]
    out_specs = [o_spec]
    if has_sink:
        in_specs.insert(0, pl.BlockSpec((per, SUBLANES, LANES), lambda r, h, i: (h, 0, 0)))
        operands.insert(0, sink)
        out_shape.append(jax.ShapeDtypeStruct((cfg.hq // per, cfg.nb, per * SUBLANES, LANES), F32))
        out_specs.append(pl.BlockSpec((None, None, per * SUBLANES, LANES), lambda r, h, i: (h, i, 0, 0)))
    outs = pl.pallas_call(
        body, out_shape=out_shape, grid=(cfg.dil, cfg.hq // per, cfg.nb), in_specs=in_specs, out_specs=out_specs,
        compiler_params=_params(("parallel", "parallel", "parallel")), name=name,
    )(*operands)
    dq = outs[0].reshape(cfg.T, cfg.hq * LANES)
    return (dq, outs[1]) if has_sink else dq


def band_dkv(cfg, q, k, v, do, o, lse, name, out_dtype, add=None, dv_into=None):
    blk, per, pk, group = cfg.blk, cfg.per, cfg.pk, cfg.group
    has_add = add is not None
    carried = dv_into is not None
    assert not carried or cfg.dil == 1

    def body(*refs):
        k_ref, v_ref = refs[:2]
        qs, dos, os_, lses = refs[2:5], refs[5:8], refs[8:11], refs[11:14]
        pos = 14 + (2 if has_add else 0) + (1 if carried else 0)
        dk_ref, dv_ref = refs[pos:pos + 2]
        ok = cfg.valid(pl.program_id(2), True)
        for jk in range(pk):
            kcols = slice(jk * LANES, (jk + 1) * LANES)
            kt, vt = k_ref[:, kcols], v_ref[:, kcols]
            dk = jnp.zeros((blk, LANES), F32)
            dv = jnp.zeros((blk, LANES), F32)
            for g in range(group):
                j = jk * group + g
                qw = cfg.window(*qs, j)
                dow = cfg.window(*dos, j)
                lse = cfg.window(*lses, j)[:, :1]
                delta = jnp.sum(dow.astype(F32) * cfg.window(*os_, j).astype(F32), axis=-1, keepdims=True)
                s = lax.dot_general(qw, kt, (((1,), (1,)), ((), ())), preferred_element_type=F32) * cfg.scale
                p = jnp.exp(jnp.where(ok, s, NEG) - lse)
                dv = dv + lax.dot_general(p.astype(BF), dow, (((0,), (0,)), ((), ())), preferred_element_type=F32)
                dp = lax.dot_general(dow, vt, (((1,), (1,)), ((), ())), preferred_element_type=F32)
                ds = p * (dp - delta) * cfg.scale
                dk = dk + lax.dot_general(ds.astype(BF), qw, (((0,), (0,)), ((), ())), preferred_element_type=F32)
            if has_add:
                dk, dv = dk + refs[14][:, kcols].astype(F32), dv + refs[15][:, kcols].astype(F32)
            dk_ref[:, kcols] = dk.astype(dk_ref.dtype)
            dv_ref[:, kcols] = dv.astype(dv_ref.dtype)

    k_spec = pl.BlockSpec((blk, pk * LANES), lambda r, h, i: (i, cfg.kcol(r) + h))
    v_spec = pl.BlockSpec((blk, pk * LANES), lambda r, h, i: (i, cfg.vcol(r) + h))
    d_spec = pl.BlockSpec((blk, pk * LANES), lambda r, h, i: (i, cfg.dkcol(r) + h))
    in_specs = [k_spec, v_spec] + cfg.rows3(per * LANES, cfg.qcol) + cfg.rows3(per * LANES, cfg.ocol) * 3
    qc_, doc, oc, lc = cfg.chains(q), cfg.chains(do), cfg.chains(o), cfg.chains(lse)
    operands = [cfg.chains(k), cfg.chains(v), qc_, qc_, qc_, doc, doc, doc, oc, oc, oc, lc, lc, lc]
    if has_add:
        in_specs += [d_spec, d_spec]
        operands += [cfg.chains(add[0]), cfg.chains(add[1])]
    cols = cfg.dil * cfg.hkv * LANES
    out_shape = [jax.ShapeDtypeStruct((cfg.len, cols), out_dtype)] * 2
    out_specs = [d_spec, d_spec]
    aliases = {}
    if carried:
        buf, blocks, block0 = dv_into
        out_shape[1] = jax.ShapeDtypeStruct((cfg.T, blocks * LANES), BF)
        out_specs[1] = pl.BlockSpec((blk, pk * LANES), lambda r, h, i: (i, block0 // pk + h))
        aliases = {len(operands): 1}
        in_specs.append(pl.BlockSpec(memory_space=pl.ANY))
        operands.append(buf)
    dk, dv = pl.pallas_call(
        body, out_shape=out_shape, grid=(cfg.dil, cfg.hq // per, cfg.nb), in_specs=in_specs, out_specs=out_specs,
        input_output_aliases=aliases, compiler_params=_params(("parallel", "parallel", "parallel")), name=name,
    )(*operands)
    return dk.reshape(cfg.T, cfg.hkv * LANES), (dv if carried else dv.reshape(cfg.T, cfg.hkv * LANES))


HBM_SPEC = pl.BlockSpec(memory_space=pltpu.HBM)


def _place():
    x, y, c = lax.axis_index("x"), lax.axis_index("y"), lax.axis_index("c")
    chips = [(1 - x, y), (x, 1 - y), (1 - x, 1 - y)]
    return x, y, c, chips


def gather_weights(shards):
    n = len(shards)

    def body(*refs):
        ins, outs = refs[:n], refs[n:2 * n]
        send_sems, recv_sems, local_sems = refs[2 * n:]
        x, y, c, chips = _place()
        me = 2 * x + y
        sibling = (x, y, 1 - c)

        def copy(w, k, src, chip_of_block, half, to):
            return pltpu.make_async_remote_copy(
                src_ref=src, dst_ref=outs[w].at[chip_of_block, half], send_sem=send_sems.at[6 * w + k],
                recv_sem=recv_sems.at[6 * w + k], device_id=to, device_id_type=MESH)

        started = []
        local = []
        for w in range(n):
            own = pltpu.make_async_copy(ins[w], outs[w].at[me], local_sems.at[w])
            own.start()
            local.append(own)
            for j, chip in enumerate(chips):
                cp = copy(w, j, ins[w].at[c], me, c, (*chip, c))
                cp.start()
                started.append(cp)
        for w in range(n):
            for j, (cx, cy) in enumerate(chips):
                them = 2 * cx + cy
                copy(w, j, ins[w].at[c], them, c, (cx, cy, c)).wait_recv()
                fwd = copy(w, 3 + j, outs[w].at[them, c], them, c, sibling)
                fwd.start()
                started.append(fwd)
        for w in range(n):
            for j, (cx, cy) in enumerate(chips):
                copy(w, 3 + j, ins[w].at[c], 2 * cx + cy, 1 - c, sibling).wait_recv()
        for cp in started:
            cp.wait_send()
        for own in local:
            own.wait()

    return pl.pallas_call(
        body, out_shape=[jax.ShapeDtypeStruct((4,) + s.shape, s.dtype) for s in shards],
        in_specs=[HBM_SPEC] * n, out_specs=[HBM_SPEC] * n,
        scratch_shapes=[pltpu.SemaphoreType.DMA((6 * n,)), pltpu.SemaphoreType.DMA((6 * n,)),
                        pltpu.SemaphoreType.DMA((n,))],
        name="gather_weights",
    )(*shards)


def _core_index():
    return lax.axis_index("c").astype(jnp.int32).reshape(1)


def presum_core_halves(g2, core, name):
    _, rows, cols = g2.shape
    tr = _row_tile(rows, cols, 1 << 20)
    nb = rows // tr
    g2 = g2.reshape(2 * rows, cols)

    def body(core_ref, mine_ref, other_ref, out_ref, land, send_sems, recv_sems):
        x, y, c, _ = _place()
        slot = pl.program_id(0) % 2
        cp = pltpu.make_async_remote_copy(
            src_ref=other_ref, dst_ref=land.at[slot], send_sem=send_sems.at[slot], recv_sem=recv_sems.at[slot],
            device_id=(x, y, 1 - c), device_id_type=MESH)
        cp.start()
        cp.wait_recv()
        out_ref[...] = (mine_ref[...] + land[slot]).astype(out_ref.dtype)
        cp.wait_send()

    grid_spec = pltpu.PrefetchScalarGridSpec(
        num_scalar_prefetch=1, grid=(nb,),
        in_specs=[pl.BlockSpec((tr, cols), lambda i, core: (core[0] * nb + i, 0)),
                  pl.BlockSpec((tr, cols), lambda i, core: ((1 - core[0]) * nb + i, 0))],
        out_specs=pl.BlockSpec((tr, cols), lambda i, core: (i, 0)),
        scratch_shapes=[pltpu.VMEM((2, tr, cols), F32), pltpu.SemaphoreType.DMA((2,)), pltpu.SemaphoreType.DMA((2,))])
    return pl.pallas_call(
        body, out_shape=jax.ShapeDtypeStruct((rows, cols), BF), grid_spec=grid_spec,
        compiler_params=_params(("arbitrary",)), name=name,
    )(core, g2, g2)


def sum_and_swap(landed, name):
    n, rows, cols = landed.shape
    tr = _row_tile(rows, cols)

    def body(*refs):
        slots = refs[:n]
        mine_ref, theirs_ref, out_buf, land, send_sems, recv_sems = refs[n:]
        x, y, c, _ = _place()
        slot = pl.program_id(0) % 2
        tot = slots[0][...].astype(F32)
        for r in slots[1:]:
            tot = tot + r[...].astype(F32)
        mine_ref[...] = tot
        out_buf[slot] = tot
        cp = pltpu.make_async_remote_copy(
            src_ref=out_buf.at[slot], dst_ref=land.at[slot], send_sem=send_sems.at[slot], recv_sem=recv_sems.at[slot],
            device_id=(x, y, 1 - c), device_id_type=MESH)
        cp.start()
        cp.wait_recv()
        theirs_ref[...] = land[slot]
        cp.wait_send()

    specs = [pl.BlockSpec((None, tr, cols), functools.partial(lambda s, i: (s, i, 0), s)) for s in range(n)]
    row = pl.BlockSpec((tr, cols), lambda i: (i, 0))
    return pl.pallas_call(
        body, out_shape=[jax.ShapeDtypeStruct((rows, cols), F32)] * 2, grid=(rows // tr,), in_specs=specs,
        out_specs=[row, row],
        scratch_shapes=[pltpu.VMEM((2, tr, cols), F32), pltpu.VMEM((2, tr, cols), F32),
                        pltpu.SemaphoreType.DMA((2,)), pltpu.SemaphoreType.DMA((2,))],
        compiler_params=_params(("arbitrary",)), name=name,
    )(*([landed] * n))


def scatter_partials(parts):
    n = len(parts)

    def body(*refs):
        ins, outs = refs[:n], refs[n:2 * n]
        send_sems, recv_sems, local_sems = refs[2 * n:]
        x, y, c, chips = _place()
        me = 2 * x + y
        started = []
        for w in range(n):
            own = pltpu.make_async_copy(ins[w].at[me], outs[w].at[me], local_sems.at[w])
            own.start()
            started.append(own)
        sends = []
        for w in range(n):
            for j, (cx, cy) in enumerate(chips):
                cp = pltpu.make_async_remote_copy(
                    src_ref=ins[w].at[2 * cx + cy], dst_ref=outs[w].at[me], send_sem=send_sems.at[3 * w + j],
                    recv_sem=recv_sems.at[3 * w + j], device_id=(cx, cy, c), device_id_type=MESH)
                cp.start()
                sends.append(cp)
        for w in range(n):
            for j, (cx, cy) in enumerate(chips):
                pltpu.make_async_remote_copy(
                    src_ref=ins[w].at[me], dst_ref=outs[w].at[2 * cx + cy], send_sem=send_sems.at[3 * w + j],
                    recv_sem=recv_sems.at[3 * w + j], device_id=(cx, cy, c), device_id_type=MESH).wait_recv()
        for cp in sends:
            cp.wait_send()
        for own in started:
            own.wait()

    return pl.pallas_call(
        body, out_shape=[jax.ShapeDtypeStruct(p.shape, p.dtype) for p in parts],
        in_specs=[HBM_SPEC] * n, out_specs=[HBM_SPEC] * n,
        scratch_shapes=[pltpu.SemaphoreType.DMA((3 * n,)), pltpu.SemaphoreType.DMA((3 * n,)),
                        pltpu.SemaphoreType.DMA((n,))],
        name="scatter_partials",
    )(*parts)


def adamw_halves(w, mine, theirs, m, v, core, name):
    rows, cols = w.shape
    tr = _row_tile(rows // 2, cols, 1 << 18)
    nh = rows // 2 // tr

    def body(core_ref, w_ref, a_ref, b_ref, m_ref, v_ref, g_out, d_out, m_out, v_out):
        g = jnp.where(pl.program_id(0) // nh == core_ref[0], a_ref[...], b_ref[...])
        d_out[...], m_out[...], v_out[...] = _adam_fn(w_ref[...], g, m_ref[...], v_ref[...])
        g_out[...] = g

    full = pl.BlockSpec((tr, cols), lambda i, core: (i, 0))
    half = pl.BlockSpec((tr, cols), lambda i, core: (i % nh, 0))
    grid_spec = pltpu.PrefetchScalarGridSpec(
        num_scalar_prefetch=1, grid=(rows // tr,), in_specs=[full, half, half, full, full], out_specs=[full] * 4)
    return pl.pallas_call(
        body, out_shape=[jax.ShapeDtypeStruct((rows, cols), F32)] * 4, grid_spec=grid_spec,
        compiler_params=_params(("parallel",)), name=name,
    )(core, w, mine, theirs, m, v)


def gather_small(vec):
    rows = vec.shape[0]

    def body(v_ref, out_ref, send_sems, recv_sems):
        x, y, c, _ = _place()
        me = 4 * x + 2 * y + c
        out_ref[me] = v_ref[...]
        flips = [(dx, dy, dc) for dx in (0, 1) for dy in (0, 1) for dc in (0, 1)][1:]

        def peer(f):
            return tuple(1 - a if d else a for a, d in zip((x, y, c), f))

        def copy(k, block, to):
            return pltpu.make_async_remote_copy(
                src_ref=v_ref, dst_ref=out_ref.at[block], send_sem=send_sems.at[k], recv_sem=recv_sems.at[k],
                device_id=to, device_id_type=MESH)

        sends = [copy(k, me, peer(f)) for k, f in enumerate(flips)]
        for cp in sends:
            cp.start()
        for k, f in enumerate(flips):
            px, py, pc = peer(f)
            copy(k, 4 * px + 2 * py + pc, peer(f)).wait_recv()
        for cp in sends:
            cp.wait_send()

    vm = pl.BlockSpec(memory_space=pltpu.VMEM)
    return pl.pallas_call(
        body, out_shape=jax.ShapeDtypeStruct((8, rows, SMALL_COLS), F32), in_specs=[vm], out_specs=vm,
        scratch_shapes=[pltpu.SemaphoreType.DMA((7,)), pltpu.SemaphoreType.DMA((7,))], name="gather_small",
    )(vec)


def sum_slots(a, out_dtype, name):
    n, rows, cols = a.shape
    tr = _row_tile(rows, cols)

    def body(*refs):
        tot = refs[0][...].astype(F32)
        for r in refs[1:n]:
            tot = tot + r[...].astype(F32)
        refs[n][...] = tot.astype(out_dtype)

    specs = [pl.BlockSpec((None, tr, cols), functools.partial(lambda s, i: (s, i, 0), s)) for s in range(n)]
    return pl.pallas_call(
        body, out_shape=jax.ShapeDtypeStruct((rows, cols), out_dtype), grid=(rows // tr,), in_specs=specs,
        out_specs=pl.BlockSpec((tr, cols), lambda i: (i, 0)), compiler_params=_params(("parallel",)), name=name,
    )(*([a] * n))


def _adam_fn(w, g, m, v):
    m = ADAM_B1 * m + (1.0 - ADAM_B1) * g
    v = ADAM_B2 * v + (1.0 - ADAM_B2) * (g * g)
    m_hat = m / (1.0 - ADAM_B1 ** ADAM_STEP)
    v_hat = v / (1.0 - ADAM_B2 ** ADAM_STEP)
    delta = -ADAM_LR * (m_hat / (jnp.sqrt(v_hat) + ADAM_EPS) + ADAM_WD * w)
    return delta, m, v


def adamw(w, g, m, v, name):
    return rowwise(_adam_fn, [w, g, m, v], [F32, F32, F32], name)


def _full_weight(name, gathered, local_shape):
    L, a, b = local_shape
    g = gathered.reshape((4, L, a, b))
    if SHARD_AXIS[name] == 1:
        return g.transpose(1, 0, 2, 3).reshape(L, 4 * a, b)
    return g.transpose(1, 2, 0, 3).reshape(L, a, 4 * b)


def _grad_slots(name, dw):
    L, a, b = dw.shape
    if SHARD_AXIS[name] == 1:
        s = dw.reshape(L, 4, a // 4, b).transpose(1, 0, 2, 3)
        rows, cols = L * (a // 4), b
    else:
        s = dw.reshape(L, a, 4, b // 4).transpose(2, 0, 1, 3)
        rows, cols = L * a, b // 4
    return s.reshape(4, 2, rows // 2, cols).transpose(1, 0, 2, 3)


def _attn_a(T):
    group = A_HEADS // A_KV_HEADS
    return Band(T, 1, A_HEADS, group, group, A_HEADS, 0, A_KV_HEADS, 0, A_HEADS + 2 * A_KV_HEADS,
                A_HEADS + A_KV_HEADS, 1.0 / math.sqrt(HEAD_DIM), A_HALF_WINDOW, BAND_BLOCK)


def _attn_b(T):
    return Attn(T, 1, B_HEADS, 1, B_HEADS, 0, B_HEADS, 0, 2 * B_HEADS, 1, 2, B_PAD, 1.0 / math.sqrt(B_QK), None,
                DENSE_BLOCK)


def _attn_c(T, group):
    window, dil = C_PATTERNS[group]
    return Band(T, dil, C_HEADS, 1, BAND_HEADS_PER_STEP, C_HEADS, 0, C_HEADS, 0, C_HEADS, 0,
                1.0 / math.sqrt(HEAD_DIM), window // 2 // dil, BAND_BLOCK)


def _pad_heads(a, axis_len_true, axis_len_pad):
    lead = a.shape[:-1]
    h = a.shape[-1] // axis_len_true
    a = a.reshape(lead + (h, axis_len_true))
    a = jnp.pad(a, [(0, 0)] * len(lead) + [(0, 0), (0, axis_len_pad - axis_len_true)])
    return a.reshape(lead + (h * axis_len_pad,))


def _unpad_heads(a, axis_len_true, axis_len_pad):
    lead = a.shape[:-1]
    h = a.shape[-1] // axis_len_pad
    return a.reshape(lead + (h, axis_len_pad))[..., :axis_len_true].reshape(lead + (h * axis_len_true,))


def _weight_grad(G, name, layer, a, dy, W, tag):
    layers, rows, cols = W[name].shape
    if name in SLOT_DIRECT:
        G[name] = matmul([(a, dy)], "tn", F32, tag, slot=Slot(name, layers, layer, rows, cols, 0, G.get(name)))
    else:
        G.setdefault(name, [None] * layers)[layer] = matmul([(a, dy)], "tn", F32, tag)


def _mixer_fwd(kind, slot, hn, W, S, tabs, tag):
    T = hn.shape[0]
    if kind == 0:
        cfg = _attn_a(T)
        qkv = matmul([(hn, W["a_w_in"][slot])], "nn", BF, tag + "_a_in")
        q = headnorm_fwd(qkv, W["a_q_norm"][slot], tabs["hd"], tag + "_a_qn", A_HEADS, 0, HEAD_DIM, HEAD_DIM)
        k = headnorm_fwd(qkv, W["a_k_norm"][slot], tabs["hd"], tag + "_a_kn", A_KV_HEADS, A_HEADS, HEAD_DIM, HEAD_DIM)
        sink = jnp.broadcast_to(W["a_sink"][slot][:, None, None], (A_HEADS, SUBLANES, LANES)).astype(F32)
        o, lse = band_fwd(cfg, q, k, qkv, tag + "_a_att", BF, sink=sink)
        S.update(qkv=qkv, q=q, k=k, o=o, lse=lse, sink=sink)
        return o
    if kind == 1:
        cfg = _attn_b(T)
        lat = matmul([(hn, W["b_w_in"][slot])], "nn", BF, tag + "_b_in")
        qn = rmsnorm_fwd(lat, W["b_q_lat_norm"][slot], tag + "_b_qlat", 0, B_Q_RANK)
        kvn = rmsnorm_fwd(lat, W["b_kv_lat_norm"][slot], tag + "_b_kvlat", 1, B_KV_RANK)
        qp = matmul([(qn, W["b_w_q_up_pad"][slot])], "nn", BF, tag + "_b_qup")
        kv = matmul([(kvn, W["b_w_kv_up"][slot])], "nn", BF, tag + "_b_kvup")
        k_rope = lat[:, B_Q_RANK + B_KV_RANK:]
        kpre = jnp.concatenate(
            [kv.reshape(T, B_HEADS, 2 * B_NOPE)[:, :, :B_NOPE],
             jnp.broadcast_to(k_rope[:, None, :], (T, B_HEADS, B_ROPE)),
             jnp.zeros((T, B_HEADS, B_PAD - B_QK), BF)], axis=-1).reshape(T, B_HEADS * B_PAD)
        q = headnorm_fwd(qp, W["b_q_norm_pad"][slot], tabs["b"], tag + "_b_qn", B_HEADS, 0, B_PAD, B_QK)
        k = headnorm_fwd(kpre, W["b_k_norm_pad"][slot], tabs["b"], tag + "_b_kn", B_HEADS, 0, B_PAD, B_QK)
        o, lse = flash_fwd(cfg, q, k, kv, tag + "_b_att", BF)
        S.update(lat=lat, qn=qn, kvn=kvn, qp=qp, kv=kv, kpre=kpre, q=q, k=k, o=o, lse=lse)
        return o
    qkv = matmul([(hn, W["c_w_in"][slot])], "nn", BF, tag + "_c_in")
    nq = C_GROUPS * C_HEADS
    qs = [headnorm_fwd(qkv, W["c_q_norm"][slot], tabs["hd"], f"{tag}_c_qn{g}", C_HEADS, g * C_HEADS, HEAD_DIM, HEAD_DIM)
          for g in range(C_GROUPS)]
    k = headnorm_fwd(qkv, W["c_k_norm"][slot], tabs["hd"], tag + "_c_kn", C_HEADS, nq, HEAD_DIM, HEAD_DIM)
    outs, lses = [], []
    v = qkv[:, (C_GROUPS + 1) * C_HEADS * HEAD_DIM:]
    for g in range(C_GROUPS):
        og, lg = band_fwd(_attn_c(T, g), qs[g], k, v, f"{tag}_c_att{g}", F32)
        outs.append(og)
        lses.append(lg)
    o, lse = rowwise(_merge_fn, outs + lses, [BF, F32], tag + "_c_merge")
    S.update(qkv=qkv, qs=qs, v=v, k=k, o=o, lse=lse)
    return o


def _mixer_bwd(kind, slot, hn, do, W, S, tabs, tag, G):
    T = hn.shape[0]
    if kind == 0:
        cfg = _attn_a(T)
        qkv = S["qkv"]
        dq, dsink = band_dq(cfg, S["q"], S["k"], qkv, do, S["o"], S["lse"], tag + "_a_dq", sink=S["sink"])
        blocks = A_HEADS + 2 * A_KV_HEADS
        dqkv, dgq = headnorm_bwd(qkv, W["a_q_norm"][slot], tabs["hd"], dq, tag + "_a_dqn", A_HEADS, 0, HEAD_DIM, HEAD_DIM,
                                 into=(None, blocks, 0))
        dk, dqkv = band_dkv(cfg, S["q"], S["k"], qkv, do, S["o"], S["lse"], tag + "_a_dkv", BF,
                            dv_into=(dqkv, blocks, A_HEADS + A_KV_HEADS))
        dqkv, dgk = headnorm_bwd(qkv, W["a_k_norm"][slot], tabs["hd"], dk, tag + "_a_dkn", A_KV_HEADS, A_HEADS,
                                 HEAD_DIM, HEAD_DIM, into=(dqkv, blocks, A_HEADS))
        _weight_grad(G, "a_w_in", slot, hn, dqkv, W, tag + "_a_dwin")
        G["a_q_norm"][slot], G["a_k_norm"][slot] = dgq, dgk
        parts = dsink.reshape(A_HEADS // cfg.per, cfg.nb, cfg.per, SUBLANES, LANES)[:, :, :, 0, 0]
        G["a_sink"][slot] = jnp.sum(parts, axis=1).reshape(A_HEADS)
        return matmul([(dqkv, W["a_w_in"][slot])], "nt", F32, tag + "_a_dhn")
    if kind == 1:
        cfg = _attn_b(T)
        kv = S["kv"]
        dq = flash_dq(cfg, S["q"], S["k"], kv, do, S["o"], S["lse"], tag + "_b_dq")
        dk, dv = flash_dkv(cfg, S["q"], S["k"], kv, do, S["o"], S["lse"], tag + "_b_dkv", BF)
        dqp, dgq = headnorm_bwd(S["qp"], W["b_q_norm_pad"][slot], tabs["b"], dq, tag + "_b_dqn", B_HEADS, 0, B_PAD, B_QK)
        dkp, dgk, dksum = headnorm_bwd(S["kpre"], W["b_k_norm_pad"][slot], tabs["b"], dk, tag + "_b_dkn", B_HEADS, 0,
                                       B_PAD, B_QK, head_sum=True)
        dkv = jnp.concatenate([dkp.reshape(T, B_HEADS, B_PAD)[:, :, :B_NOPE], dv.reshape(T, B_HEADS, LANES)],
                              axis=-1).reshape(T, B_HEADS * 2 * B_NOPE)
        _weight_grad(G, "b_w_kv_up", slot, S["kvn"], dkv, W, tag + "_b_dwkv")
        G["b_w_q_up"][slot] = _unpad_heads(matmul([(S["qn"], dqp)], "tn", F32, tag + "_b_dwq"), B_QK, B_PAD)
        dqn = matmul([(dqp, W["b_w_q_up_pad"][slot])], "nt", F32, tag + "_b_dqnorm")
        dkvn = matmul([(dkv, W["b_w_kv_up"][slot])], "nt", F32, tag + "_b_dkvnorm")
        dql, dg_q = rmsnorm_bwd(S["lat"], W["b_q_lat_norm"][slot], dqn, tag + "_b_dqlat", [BF], None, 0, B_Q_RANK)
        dkvl, dg_kv = rmsnorm_bwd(S["lat"], W["b_kv_lat_norm"][slot], dkvn, tag + "_b_dkvlat", [BF], None, 1, B_KV_RANK)
        dlat = jnp.concatenate([dql, dkvl, dksum[:, B_NOPE:B_QK].astype(BF)], axis=1)
        _weight_grad(G, "b_w_in", slot, hn, dlat, W, tag + "_b_dwin")
        G["b_q_norm"][slot], G["b_k_norm"][slot] = dgq[:B_QK], dgk[:B_QK]
        G["b_q_lat_norm"][slot], G["b_kv_lat_norm"][slot] = dg_q, dg_kv
        return matmul([(dlat, W["b_w_in"][slot])], "nt", F32, tag + "_b_dhn")
    qkv = S["qkv"]
    nq = C_GROUPS * C_HEADS
    blocks = (C_GROUPS + 2) * C_HEADS
    dqkv, dgq = None, 0.0
    for g in range(C_GROUPS):
        dq = band_dq(_attn_c(T, g), S["qs"][g], S["k"], S["v"], do, S["o"], S["lse"], f"{tag}_c_dq{g}")
        dqkv, dg = headnorm_bwd(qkv, W["c_q_norm"][slot], tabs["hd"], dq, f"{tag}_c_dqn{g}", C_HEADS, g * C_HEADS,
                                HEAD_DIM, HEAD_DIM, into=(dqkv, blocks, g * C_HEADS))
        dgq = dgq + dg
    acc = None
    for g in reversed(range(C_GROUPS)):
        into = (dqkv, blocks, (C_GROUPS + 1) * C_HEADS) if g == 0 else None
        acc = band_dkv(_attn_c(T, g), S["qs"][g], S["k"], S["v"], do, S["o"], S["lse"], f"{tag}_c_dkv{g}", F32,
                       add=acc, dv_into=into)
    dk, dqkv = acc
    dqkv, dgk = headnorm_bwd(qkv, W["c_k_norm"][slot], tabs["hd"], dk, tag + "_c_dkn", C_HEADS, nq, HEAD_DIM, HEAD_DIM,
                             into=(dqkv, blocks, nq))
    _weight_grad(G, "c_w_in", slot, hn, dqkv, W, tag + "_c_dwin")
    G["c_q_norm"][slot], G["c_k_norm"][slot] = dgq, dgk
    return matmul([(dqkv, W["c_w_in"][slot])], "nt", F32, tag + "_c_dhn")


MIXER_OUT = ("a_w_o", "b_w_o", "c_w_o")


def local_step(x, p, positions, loss_target, W):
    T = x.shape[0]
    tabs = {"hd": rope_tables(positions, HEAD_DIM, 0, PARTIAL_ROT), "b": rope_tables(positions, B_PAD, B_NOPE, B_ROPE)}
    W = dict(W)
    W["b_w_q_up_pad"] = _pad_heads(W["b_w_q_up"], B_QK, B_PAD)
    W["b_q_norm_pad"] = _pad_heads(W["b_q_norm"], B_QK, B_PAD)
    W["b_k_norm_pad"] = _pad_heads(W["b_k_norm"], B_QK, B_PAD)
    saved = []
    h = x
    for i in range(DEPTH):
        kind, slot = i % 3, i // 3
        tag = f"l{i}"
        S = {"h0": h}
        hn = rmsnorm_fwd(h, W["g_mix"][i], tag + "_mixnorm")
        o = _mixer_fwd(kind, slot, hn, W, S, tabs, tag)
        h1 = matmul([(o, W[MIXER_OUT[kind]][slot])], "nn", F32, tag + "_mixout", res=h)
        hn2 = rmsnorm_fwd(h1, W["g_ffn"][i], tag + "_ffnnorm")
        a, b, c = matmul_swiglu(hn2, W["w_ffn_gate"][i], W["w_ffn_up"][i], tag + "_gateup")
        h2 = matmul([(c, W["w_ffn_down"][i])], "nn", F32, tag + "_down", res=h1)
        hn3 = rmsnorm_fwd(h2, W["g_ple"][i], tag + "_plenorm")
        p_i = p[i].astype(BF)
        pp = matmul([(p_i, W["w_ple_proj"][i])], "nn", BF, tag + "_pleproj")
        z, h3 = matmul([(hn3, W["w_ple_gate"][i])], "nn", BF, tag + "_plegate", ple=(h2, pp))
        S.update(hn=hn, h1=h1, hn2=hn2, a=a, b=b, c=c, h2=h2, hn3=hn3, z=z, pp=pp, p=p_i)
        saved.append(S)
        h = h3

    loss, dh = loss_and_grad(h, loss_target, "loss")
    G = {n: [None] * W[n].shape[0] for n in SMALL + ("b_w_q_up",)}
    for i in reversed(range(DEPTH)):
        kind, slot = i % 3, i // 3
        tag = f"l{i}"
        S = saved[i]
        dz, dpp = rowwise(_ple_bwd_fn, [dh, S["z"], S["pp"]], [BF, BF], tag + "_dple")
        _weight_grad(G, "w_ple_proj", i, S["p"], dpp, W, tag + "_dwpleproj")
        _weight_grad(G, "w_ple_gate", i, S["hn3"], dz, W, tag + "_dwplegate")
        dhn3 = matmul([(dz, W["w_ple_gate"][i])], "nt", F32, tag + "_dplenorm")
        dh2, dh2b, G["g_ple"][i] = rmsnorm_bwd(S["h2"], W["g_ple"][i], dhn3, tag + "_dple_norm", [F32, BF], dres=dh)
        da, db = matmul([(dh2b, W["w_ffn_down"][i])], "nt", BF, tag + "_dswiglu", swiglu=(S["a"], S["b"]))
        _weight_grad(G, "w_ffn_down", i, S["c"], dh2b, W, tag + "_dwdown")
        _weight_grad(G, "w_ffn_gate", i, S["hn2"], da, W, tag + "_dwgate")
        _weight_grad(G, "w_ffn_up", i, S["hn2"], db, W, tag + "_dwup")
        dhn2 = matmul([(da, W["w_ffn_gate"][i]), (db, W["w_ffn_up"][i])], "nt", F32, tag + "_dffnnorm")
        dh1, dh1b, G["g_ffn"][i] = rmsnorm_bwd(S["h1"], W["g_ffn"][i], dhn2, tag + "_dffn_norm", [F32, BF], dres=dh2)
        wo = W[MIXER_OUT[kind]][slot]
        do = matmul([(dh1b, wo)], "nt", BF, tag + "_dmixout")
        _weight_grad(G, MIXER_OUT[kind], slot, S["o"], dh1b, W, tag + "_dwmixout")
        dhn = _mixer_bwd(kind, slot, S["hn"], do, W, S, tabs, tag, G)
        dh, G["g_mix"][i] = rmsnorm_bwd(S["h0"], W["g_mix"][i], dhn, tag + "_dmix_norm", [F32], dres=dh1)
    return loss, dh, G


def _pack_small(vals):
    flat = jnp.concatenate([vals[n].reshape(-1).astype(F32) for n in SMALL])
    rows = -(-flat.shape[0] // SMALL_COLS)
    rows = -(-rows // SUBLANES) * SUBLANES
    return jnp.pad(flat, (0, rows * SMALL_COLS - flat.shape[0])).reshape(rows, SMALL_COLS)


def _unpack_small(packed, like):
    flat = packed.reshape(-1)
    out, off = {}, 0
    for n in SMALL:
        size = like[n].size
        out[n] = flat[off:off + size].reshape(like[n].shape)
        off += size
    return out


def kernel(x, p, positions, g_mix, g_ffn, g_ple, w_ple_gate, w_ple_proj, w_ffn_gate, w_ffn_up, w_ffn_down, a_w_in, a_q_norm, a_k_norm, a_sink, a_w_o, b_w_in, b_q_lat_norm, b_kv_lat_norm, b_w_q_up, b_w_kv_up, b_q_norm, b_k_norm, b_w_o, c_w_in, c_q_norm, c_k_norm, c_w_o, loss_target, m_g_mix, m_g_ffn, m_g_ple, m_w_ple_gate, m_w_ple_proj, m_w_ffn_gate, m_w_ffn_up, m_w_ffn_down, m_a_w_in, m_a_q_norm, m_a_k_norm, m_a_sink, m_a_w_o, m_b_w_in, m_b_q_lat_norm, m_b_kv_lat_norm, m_b_w_q_up, m_b_w_kv_up, m_b_q_norm, m_b_k_norm, m_b_w_o, m_c_w_in, m_c_q_norm, m_c_k_norm, m_c_w_o, v_g_mix, v_g_ffn, v_g_ple, v_w_ple_gate, v_w_ple_proj, v_w_ffn_gate, v_w_ffn_up, v_w_ffn_down, v_a_w_in, v_a_q_norm, v_a_k_norm, v_a_sink, v_a_w_o, v_b_w_in, v_b_q_lat_norm, v_b_kv_lat_norm, v_b_w_q_up, v_b_w_kv_up, v_b_q_norm, v_b_k_norm, v_b_w_o, v_c_w_in, v_c_q_norm, v_c_k_norm, v_c_w_o):
    args = dict(locals())
    w_loc = {n: args[n] for n in WEIGHTS}
    m_loc = {n: args["m_" + n] for n in WEIGHTS}
    v_loc = {n: args["v_" + n] for n in WEIGHTS}

    def halves(a):
        rows = a.shape[0] * a.shape[1]
        return a.reshape(2, rows // 2, a.shape[2])

    gathered = gather_weights([halves(w_loc[n].astype(BF)) for n in BIG])
    W = {n: _full_weight(n, g, w_loc[n].shape) for n, g in zip(BIG, gathered)}
    for n in SMALL:
        W[n] = w_loc[n]

    loss, dx, G = local_step(x[0], p[:, 0], positions[0], loss_target[0], W)
    loss = lax.psum(loss, ("x", "y", "c"))

    core = _core_index()
    parts = []
    for n in BIG:
        s = _grad_slots(n, jnp.stack(G[n])) if isinstance(G[n], list) else G[n]
        part = presum_core_halves(s.reshape(2, 4 * s.shape[2], s.shape[3]), core, "presum_" + n)
        parts.append(part.reshape(s.shape[1:]))
    landed = scatter_partials(parts)
    halves = [sum_and_swap(a, "sum_" + n) for n, a in zip(BIG, landed)]

    small = gather_small(_pack_small({n: jnp.stack(G[n]) for n in SMALL}))
    small_sum = sum_slots(small, F32, "sum_small")
    grads = _unpack_small(small_sum, w_loc)

    delta, new_m, new_v = {}, {}, {}
    for n, (mine, theirs) in zip(BIG, halves):
        shape = w_loc[n].shape
        two_d = (shape[0] * shape[1], shape[2])
        g, d, m, v = adamw_halves(w_loc[n].reshape(two_d), mine, theirs, m_loc[n].reshape(two_d),
                                  v_loc[n].reshape(two_d), core, "adamw_" + n)
        grads[n], delta[n], new_m[n], new_v[n] = g.reshape(shape), d.reshape(shape), m.reshape(shape), v.reshape(shape)
    d, m, v = adamw(_pack_small(w_loc), small_sum, _pack_small(m_loc), _pack_small(v_loc), "adamw_small")
    delta.update(_unpack_small(d, w_loc))
    new_m.update(_unpack_small(m, w_loc))
    new_v.update(_unpack_small(v, w_loc))

    return (loss, dx[None], *[grads[n] for n in WEIGHTS], *[delta[n] for n in WEIGHTS],
            *[new_m[n] for n in WEIGHTS], *[new_v[n] for n in WEIGHTS])
```

```python
import functools
import math

import numpy as np
import jax
import jax.numpy as jnp
from jax import lax
from jax.experimental import pallas as pl
from jax.experimental.pallas import tpu as pltpu

F32 = jnp.float32
BF = jnp.bfloat16

D_MODEL = 2048
DEPTH = 4
HEAD_DIM = 128
ROPE_THETA = 500000.0
PARTIAL_ROT = HEAD_DIM // 4
NORM_EPS = 1e-6
NEG = -1e30
A_HEADS = 16
A_KV_HEADS = 4
A_HALF_WINDOW = 128
B_HEADS = 16
B_Q_RANK = 512
B_KV_RANK = 512
B_NOPE = 128
B_ROPE = 64
B_QK = B_NOPE + B_ROPE
B_PAD = 256
C_PATTERNS = ((128, 1), (512, 4), (2048, 16))
C_HEADS = 16
C_GROUPS = 3
ADAM_LR = 0.001
ADAM_B1 = 0.9
ADAM_B2 = 0.999
ADAM_EPS = 1e-08
ADAM_WD = 0.01
ADAM_STEP = 10

LANES = 128
SUBLANES = 8
VMEM_LIMIT_BYTES = 56 * 1024 * 1024
MATMUL_VMEM_BYTES = 46 * 1024 * 1024
MIN_M_TILE = 512
SINGLE_STEP_MAX_K = 2048
BAND_BLOCK = 256
BAND_HEADS_PER_STEP = 4
DENSE_BLOCK = 1024
DENSE_OTHER_BLOCK = 2048
DENSE_SUB = 256
MESH = pl.DeviceIdType.MESH

BIG = ("w_ple_gate", "w_ple_proj", "w_ffn_gate", "w_ffn_up", "w_ffn_down", "a_w_in", "a_w_o",
       "b_w_in", "b_w_q_up", "b_w_kv_up", "b_w_o", "c_w_in", "c_w_o")
SHARD_AXIS = {"w_ple_gate": 1, "w_ple_proj": 2, "w_ffn_gate": 2, "w_ffn_up": 2, "w_ffn_down": 1,
              "a_w_in": 2, "a_w_o": 1, "b_w_in": 1, "b_w_q_up": 2, "b_w_kv_up": 2, "b_w_o": 1,
              "c_w_in": 2, "c_w_o": 1}
SMALL = ("g_mix", "g_ffn", "g_ple", "a_q_norm", "a_k_norm", "a_sink", "b_q_lat_norm",
         "b_kv_lat_norm", "b_q_norm", "b_k_norm", "c_q_norm", "c_k_norm")
WEIGHTS = ("g_mix", "g_ffn", "g_ple", "w_ple_gate", "w_ple_proj", "w_ffn_gate", "w_ffn_up",
           "w_ffn_down", "a_w_in", "a_q_norm", "a_k_norm", "a_sink", "a_w_o", "b_w_in",
           "b_q_lat_norm", "b_kv_lat_norm", "b_w_q_up", "b_w_kv_up", "b_q_norm", "b_k_norm",
           "b_w_o", "c_w_in", "c_q_norm", "c_k_norm", "c_w_o")
SMALL_COLS = 1024
SLOT_DIRECT = ("w_ple_gate", "w_ple_proj", "w_ffn_gate", "w_ffn_up", "w_ffn_down")


def _params(semantics):
    return pltpu.CompilerParams(dimension_semantics=semantics, vmem_limit_bytes=VMEM_LIMIT_BYTES)


def _tile(dim, cands=(1024, 1408, 512, 256, 128)):
    for c in cands:
        if dim % c == 0:
            return c
    return dim


def _k_tile(K, bytes_per_k, fixed_bytes):
    for t in (4096, 2816, 2048, 1408, 1024, 512, 256, 128):
        if K % t == 0 and 2 * bytes_per_k * t + fixed_bytes <= MATMUL_VMEM_BYTES:
            return t
    return _tile(K, (128,))


def _row_tile(rows, cols, target_elems=1 << 19):
    best = None
    for t in range(16, rows + 1, 16):
        if rows % t == 0 and t * cols <= target_elems:
            best = t
    return best if best is not None else rows


def _sigmoid(x):
    return 1.0 / (1.0 + jnp.exp(-x))


class Slot:
    def __init__(self, name, layers, layer, rows, cols, col0=0, buf=None):
        self.axis, self.layers, self.layer, self.rows, self.cols, self.col0, self.buf = (
            SHARD_AXIS[name], layers, layer, rows, cols, col0, buf)
        self.srows = rows // 4 if self.axis == 1 else rows
        self.scols = cols if self.axis == 1 else cols // 4
        self.half = layers * self.srows // 2

    def tiles(self, ncols):
        tm = _tile(math.gcd(self.srows, self.half))
        tn = _tile(math.gcd(self.scols, math.gcd(self.col0, ncols)))
        return tm, tn

    def spec(self, tm, tn):
        def index(i, j, k):
            row, col = i * tm, self.col0 + j * tn
            chip = row // self.srows if self.axis == 1 else col // self.scols
            flat = self.layer * self.srows + (row % self.srows if self.axis == 1 else row)
            cb = col // tn if self.axis == 1 else (col % self.scols) // tn
            return flat // self.half, chip, (flat % self.half) // tm, cb

        return pl.BlockSpec((None, None, tm, tn), index)

    def shape(self):
        return jax.ShapeDtypeStruct((2, 4, self.half, self.scols), F32)


def matmul(pairs, mode, out_dtype, name, res=None, swiglu=None, ple=None, slot=None):
    a0, b0 = pairs[0]
    if mode == "nn":
        (M, K), N = a0.shape, b0.shape[1]
    elif mode == "nt":
        (M, K), N = a0.shape, b0.shape[0]
    else:
        (K, M), N = a0.shape, b0.shape[1]
    tm, tn = (_tile(M), _tile(N)) if slot is None else slot.tiles(N)
    n_mn = 2 + (0 if res is None else 2) + (0 if swiglu is None else 2) + (0 if ple is None else 4)

    def k_tile(rows):
        return _k_tile(K, sum(rows * a.dtype.itemsize + tn * b.dtype.itemsize for a, b in pairs),
                       4 * rows * tn * (1 + n_mn))

    tk = k_tile(tm)
    if (slot is None and tk < K <= SINGLE_STEP_MAX_K and tm > MIN_M_TILE and M % MIN_M_TILE == 0
            and k_tile(MIN_M_TILE) == K):
        tm, tk = MIN_M_TILE, K
    nk = K // tk
    if mode == "nn":
        a_spec = pl.BlockSpec((tm, tk), lambda i, j, k: (i, k))
        b_spec = pl.BlockSpec((tk, tn), lambda i, j, k: (k, j))
        dims = (((1,), (0,)), ((), ()))
    elif mode == "nt":
        a_spec = pl.BlockSpec((tm, tk), lambda i, j, k: (i, k))
        b_spec = pl.BlockSpec((tn, tk), lambda i, j, k: (j, k))
        dims = (((1,), (1,)), ((), ()))
    else:
        a_spec = pl.BlockSpec((tk, tm), lambda i, j, k: (k, i))
        b_spec = pl.BlockSpec((tk, tn), lambda i, j, k: (k, j))
        dims = (((0,), (0,)), ((), ()))
    mn_spec = pl.BlockSpec((tm, tn), lambda i, j, k: (i, j))
    npairs = len(pairs)
    extras = [] if res is None else [res]
    if swiglu is not None:
        extras = list(swiglu)
    if ple is not None:
        extras = list(ple)
    nex = len(extras)
    nout = 2 if (swiglu is not None or ple is not None) else 1
    carried = slot is not None and slot.buf is not None

    def body(*refs):
        ins = refs[:2 * npairs]
        ex = refs[2 * npairs:2 * npairs + nex]
        first_out = 2 * npairs + nex + (1 if carried else 0)
        outs = refs[first_out:first_out + nout]
        k = pl.program_id(2)

        def product():
            part = None
            for p in range(npairs):
                d = lax.dot_general(ins[2 * p][...].astype(BF), ins[2 * p + 1][...].astype(BF), dims,
                                    preferred_element_type=F32)
                part = d if part is None else part + d
            return part

        def finish(r):
            if swiglu is not None:
                a = ex[0][...].astype(F32)
                b = ex[1][...].astype(F32)
                sg = _sigmoid(a)
                outs[0][...] = (r * b * (sg * (1.0 + a * (1.0 - sg)))).astype(out_dtype)
                outs[1][...] = (r * (a * sg)).astype(out_dtype)
            elif ple is not None:
                outs[0][...] = r.astype(out_dtype)
                outs[1][...] = ex[0][...] + _sigmoid(r) * ex[1][...].astype(F32)
            elif res is not None:
                outs[0][...] = (ex[0][...] + r).astype(out_dtype)
            else:
                outs[0][...] = r.astype(outs[0].dtype)

        if nk == 1:
            finish(product())
        else:
            acc = refs[-1]

            @pl.when(k == 0)
            def _():
                acc[...] = jnp.zeros_like(acc)

            acc[...] += product()

            @pl.when(k == nk - 1)
            def _():
                finish(acc[...])

    in_specs = []
    operands = []
    for a, b in pairs:
        in_specs += [a_spec, b_spec]
        operands += [a, b]
    in_specs += [mn_spec] * nex
    operands += extras
    out_shape = [jax.ShapeDtypeStruct((M, N), out_dtype)] * nout
    out_specs = [mn_spec] * nout
    aliases = {}
    if ple is not None:
        out_shape[1] = jax.ShapeDtypeStruct((M, N), F32)
    if slot is not None:
        out_shape, out_specs = [slot.shape()], [slot.spec(tm, tn)]
        if carried:
            aliases = {len(operands): 0}
            in_specs.append(pl.BlockSpec(memory_space=pl.ANY))
            operands.append(slot.buf)
    outs = pl.pallas_call(
        body, out_shape=out_shape, grid=(M // tm, N // tn, nk), in_specs=in_specs,
        out_specs=out_specs, scratch_shapes=[pltpu.VMEM((tm, tn), F32)] if nk > 1 else [],
        input_output_aliases=aliases, compiler_params=_params(("parallel", "parallel", "arbitrary")), name=name,
    )(*operands)
    return outs if nout > 1 else outs[0]


def matmul_swiglu(x, wg, wu, name):
    (M, K), N = x.shape, wg.shape[1]
    tm, tn = _tile(M), _tile(N)

    def k_tile(rows):
        return _k_tile(K, rows * x.dtype.itemsize + 2 * tn * wg.dtype.itemsize, 4 * rows * tn * (2 + 3))

    tk = k_tile(tm)
    if tk < K <= SINGLE_STEP_MAX_K and tm > MIN_M_TILE and M % MIN_M_TILE == 0 and k_tile(MIN_M_TILE) == K:
        tm, tk = MIN_M_TILE, K
    nk = K // tk

    def body(x_ref, g_ref, u_ref, a_ref, b_ref, c_ref, *accs):
        k = pl.program_id(2)
        xv = x_ref[...].astype(BF)

        def products():
            return (jnp.dot(xv, g_ref[...].astype(BF), preferred_element_type=F32),
                    jnp.dot(xv, u_ref[...].astype(BF), preferred_element_type=F32))

        def finish(a, b):
            a_ref[...] = a.astype(a_ref.dtype)
            b_ref[...] = b.astype(b_ref.dtype)
            c_ref[...] = (a * _sigmoid(a) * b).astype(c_ref.dtype)

        if nk == 1:
            finish(*products())
        else:
            acc_g, acc_u = accs

            @pl.when(k == 0)
            def _():
                acc_g[...] = jnp.zeros_like(acc_g)
                acc_u[...] = jnp.zeros_like(acc_u)

            pg, pu = products()
            acc_g[...] += pg
            acc_u[...] += pu

            @pl.when(k == nk - 1)
            def _():
                finish(acc_g[...], acc_u[...])

    w_spec = pl.BlockSpec((tk, tn), lambda i, j, k: (k, j))
    mn_spec = pl.BlockSpec((tm, tn), lambda i, j, k: (i, j))
    return pl.pallas_call(
        body, out_shape=[jax.ShapeDtypeStruct((M, N), BF)] * 3, grid=(M // tm, N // tn, nk),
        in_specs=[pl.BlockSpec((tm, tk), lambda i, j, k: (i, k)), w_spec, w_spec], out_specs=[mn_spec] * 3,
        scratch_shapes=[pltpu.VMEM((tm, tn), F32)] * 2 if nk > 1 else [],
        compiler_params=_params(("parallel", "parallel", "arbitrary")), name=name,
    )(x, wg, wu)


def rowwise(fn, ins, out_dtypes, name):
    rows, cols = ins[0].shape
    tr = _row_tile(rows, cols)
    nin = len(ins)

    def body(*refs):
        vals = fn(*[r[...] for r in refs[:nin]])
        for o, v in zip(refs[nin:], vals):
            o[...] = v.astype(o.dtype)

    spec = pl.BlockSpec((tr, cols), lambda i: (i, 0))
    outs = pl.pallas_call(
        body, out_shape=[jax.ShapeDtypeStruct((rows, cols), d) for d in out_dtypes],
        grid=(rows // tr,), in_specs=[spec] * nin, out_specs=[spec] * len(out_dtypes),
        compiler_params=_params(("parallel",)), name=name,
    )(*ins)
    return outs


def _ple_bwd_fn(dh, z, pp):
    gate = _sigmoid(z.astype(F32))
    return (dh * pp.astype(F32) * gate * (1.0 - gate), dh * gate)


def _merge_fn(o0, o1, o2, l0, l1, l2):
    m = jnp.maximum(jnp.maximum(l0, l1), l2)
    e0, e1, e2 = jnp.exp(l0 - m), jnp.exp(l1 - m), jnp.exp(l2 - m)
    den = e0 + e1 + e2
    return ((e0 * o0 + e1 * o1 + e2 * o2) / den, m + jnp.log(den))


def rmsnorm_fwd(x, g, name, col_block=0, width=None):
    T = x.shape[0]
    W = x.shape[1] if width is None else width
    tt = _row_tile(T, W)

    def body(x_ref, g_ref, y_ref):
        xf = x_ref[...].astype(F32)
        ms = jnp.mean(xf * xf, axis=-1, keepdims=True)
        y_ref[...] = (xf * lax.rsqrt(ms + NORM_EPS) * g_ref[...]).astype(y_ref.dtype)

    return pl.pallas_call(
        body, out_shape=jax.ShapeDtypeStruct((T, W), BF), grid=(T // tt,),
        in_specs=[pl.BlockSpec((tt, W), lambda i: (i, col_block)), pl.BlockSpec((1, W), lambda i: (0, 0))],
        out_specs=pl.BlockSpec((tt, W), lambda i: (i, 0)),
        compiler_params=_params(("parallel",)), name=name,
    )(x, g.reshape(1, W).astype(F32))


def rmsnorm_bwd(x, g, dy, name, out_dtypes, dres=None, col_block=0, width=None):
    T = x.shape[0]
    W = x.shape[1] if width is None else width
    tt = _row_tile(T, W, 1 << 18)
    nout = len(out_dtypes)
    has_res = dres is not None

    def body(*refs):
        x_ref, g_ref, dy_ref = refs[:3]
        pos = 3
        res_ref = None
        if has_res:
            res_ref = refs[3]
            pos = 4
        dx_refs = refs[pos:pos + nout]
        dg_ref = refs[pos + nout]
        xf = x_ref[...].astype(F32)
        rstd = lax.rsqrt(jnp.mean(xf * xf, axis=-1, keepdims=True) + NORM_EPS)
        xhat = xf * rstd
        dyf = dy_ref[...].astype(F32)
        dn = dyf * g_ref[...]
        dx = rstd * (dn - xhat * jnp.mean(dn * xhat, axis=-1, keepdims=True))
        if has_res:
            dx = dx + res_ref[...]
        for o in dx_refs:
            o[...] = dx.astype(o.dtype)

        @pl.when(pl.program_id(0) == 0)
        def _():
            dg_ref[...] = jnp.zeros_like(dg_ref)

        dg_ref[...] += jnp.broadcast_to(jnp.sum(dyf * xhat, axis=0, keepdims=True), dg_ref.shape)

    row = pl.BlockSpec((tt, W), lambda i: (i, 0))
    in_specs = [pl.BlockSpec((tt, W), lambda i: (i, col_block)), pl.BlockSpec((1, W), lambda i: (0, 0)), row]
    operands = [x, g.reshape(1, W).astype(F32), dy]
    if has_res:
        in_specs.append(row)
        operands.append(dres)
    outs = pl.pallas_call(
        body,
        out_shape=[jax.ShapeDtypeStruct((T, W), d) for d in out_dtypes] + [jax.ShapeDtypeStruct((SUBLANES, W), F32)],
        grid=(T // tt,), in_specs=in_specs,
        out_specs=[row] * nout + [pl.BlockSpec((SUBLANES, W), lambda i: (0, 0))],
        compiler_params=_params(("arbitrary",)), name=name,
    )(*operands)
    return tuple(outs[:nout]) + (outs[nout][0],)


def loss_and_grad(y, target, name):
    T, D = y.shape
    tt = _row_tile(T, D)

    def body(y_ref, t_ref, loss_ref, dy_ref):
        d = y_ref[...] - t_ref[...]
        dy_ref[...] = d * (1.0 / D)

        @pl.when(pl.program_id(0) == 0)
        def _():
            loss_ref[...] = jnp.zeros_like(loss_ref)

        loss_ref[...] += jnp.full(loss_ref.shape, 0.5 / D, F32) * jnp.sum(d * d)

    row = pl.BlockSpec((tt, D), lambda i: (i, 0))
    loss, dy = pl.pallas_call(
        body, out_shape=[jax.ShapeDtypeStruct((SUBLANES, LANES), F32), jax.ShapeDtypeStruct((T, D), F32)],
        grid=(T // tt,), in_specs=[row, row],
        out_specs=[pl.BlockSpec((SUBLANES, LANES), lambda i: (0, 0)), row],
        compiler_params=_params(("arbitrary",)), name=name,
    )(y, target)
    return loss[0, 0], dy


def rope_tables(pos, width, r0, rot_dim):
    half = rot_dim // 2
    inv = ROPE_THETA ** (-jnp.arange(half, dtype=F32) * 2.0 / rot_dim)
    ang = pos.astype(F32)[:, None] * inv
    cos, sin = jnp.cos(ang), jnp.sin(ang)
    T = pos.shape[0]
    ones_l, ones_r = jnp.ones((T, r0), F32), jnp.ones((T, width - r0 - rot_dim), F32)
    c_tab = jnp.concatenate([ones_l, cos, cos, ones_r], axis=1)
    s_tab = jnp.concatenate([0 * ones_l, -sin, sin, 0 * ones_r], axis=1)
    perm = np.zeros((width, width), np.float32)
    for j in range(half):
        perm[r0 + j + half, r0 + j] = 1.0
        perm[r0 + j, r0 + j + half] = 1.0
    return c_tab, s_tab, jnp.asarray(perm, BF)


def _lane_permute(v, perm):
    hi = v.astype(BF)
    lo = (v - hi.astype(F32)).astype(BF)
    return (jnp.dot(hi, perm, preferred_element_type=F32) + jnp.dot(lo, perm, preferred_element_type=F32))


def headnorm_fwd(x, g, tabs, name, heads, col0, width, n_true):
    c_tab, s_tab, perm = tabs
    T = x.shape[0]
    tt = _tile(T, (1024, 512, 256, 128))
    inv_n = 1.0 / n_true

    def body(x_ref, g_ref, c_ref, s_ref, p_ref, y_ref):
        xf = x_ref[...].astype(F32)
        rstd = lax.rsqrt(jnp.sum(xf * xf, axis=-1, keepdims=True) * inv_n + NORM_EPS)
        n = xf * rstd * g_ref[...]
        y_ref[...] = (n * c_ref[...] + _lane_permute(n, p_ref[...]) * s_ref[...]).astype(y_ref.dtype)

    tab = pl.BlockSpec((tt, width), lambda i, h: (i, 0))
    return pl.pallas_call(
        body, out_shape=jax.ShapeDtypeStruct((T, heads * width), BF), grid=(T // tt, heads),
        in_specs=[pl.BlockSpec((tt, width), lambda i, h: (i, col0 + h)),
                  pl.BlockSpec((1, width), lambda i, h: (0, 0)), tab, tab,
                  pl.BlockSpec((width, width), lambda i, h: (0, 0))],
        out_specs=pl.BlockSpec((tt, width), lambda i, h: (i, h)),
        compiler_params=_params(("parallel", "parallel")), name=name,
    )(x, g.reshape(1, width).astype(F32), c_tab, s_tab, perm)


def headnorm_bwd(x, g, tabs, dy, name, heads, col0, width, n_true, head_sum=False, into=None):
    c_tab, s_tab, perm = tabs
    T = x.shape[0]
    tt = _tile(T, (1024, 512, 256, 128))
    inv_n = 1.0 / n_true
    buf, blocks, block0 = into if into is not None else (None, heads, 0)
    carried = buf is not None

    def body(*refs):
        x_ref, g_ref, c_ref, s_ref, p_ref, dy_ref = refs[:6]
        dx_ref, dg_ref = refs[7:9] if carried else refs[6:8]
        i, h = pl.program_id(0), pl.program_id(1)
        xf = x_ref[...].astype(F32)
        rstd = lax.rsqrt(jnp.sum(xf * xf, axis=-1, keepdims=True) * inv_n + NORM_EPS)
        xhat = xf * rstd
        dyf = dy_ref[...].astype(F32)
        dn = dyf * c_ref[...] + _lane_permute(dyf * s_ref[...], p_ref[...])
        dxh = dn * g_ref[...]
        dx = rstd * (dxh - xhat * (jnp.sum(dxh * xhat, axis=-1, keepdims=True) * inv_n))
        dx_ref[...] = dx.astype(dx_ref.dtype)

        @pl.when(jnp.logical_and(i == 0, h == 0))
        def _():
            dg_ref[...] = jnp.zeros_like(dg_ref)

        dg_ref[...] += jnp.broadcast_to(jnp.sum(dn * xhat, axis=0, keepdims=True), dg_ref.shape)
        if head_sum:
            sum_ref = refs[-1]

            @pl.when(h == 0)
            def _():
                sum_ref[...] = jnp.zeros_like(sum_ref)

            sum_ref[...] += dx

    tab = pl.BlockSpec((tt, width), lambda i, h: (i, 0))
    out_shape = [jax.ShapeDtypeStruct((T, blocks * width), BF), jax.ShapeDtypeStruct((SUBLANES, width), F32)]
    out_specs = [pl.BlockSpec((tt, width), lambda i, h: (i, block0 + h)),
                 pl.BlockSpec((SUBLANES, width), lambda i, h: (0, 0))]
    if head_sum:
        out_shape.append(jax.ShapeDtypeStruct((T, width), F32))
        out_specs.append(tab)
    in_specs = [pl.BlockSpec((tt, width), lambda i, h: (i, col0 + h)),
                pl.BlockSpec((1, width), lambda i, h: (0, 0)), tab, tab,
                pl.BlockSpec((width, width), lambda i, h: (0, 0)),
                pl.BlockSpec((tt, width), lambda i, h: (i, h))]
    operands = [x, g.reshape(1, width).astype(F32), c_tab, s_tab, perm, dy]
    if carried:
        in_specs.append(pl.BlockSpec(memory_space=pl.ANY))
        operands.append(buf)
    outs = pl.pallas_call(
        body, out_shape=out_shape, grid=(T // tt, heads), in_specs=in_specs, out_specs=out_specs,
        input_output_aliases={6: 0} if carried else {},
        compiler_params=_params(("arbitrary", "arbitrary")), name=name,
    )(*operands)
    return (outs[0], outs[1][0]) + ((outs[2],) if head_sum else ())


class Attn:
    def __init__(self, T, dil, hq, group, qc, q0, kc, k0, vc, v0, vstride, dqk, scale, half_window, blk, oblk=None):
        self.T, self.dil, self.hq, self.group = T, dil, hq, group
        self.hkv = hq // group
        self.qc, self.q0, self.kc, self.k0, self.vc, self.v0, self.vstride = qc, q0, kc, k0, vc, v0, vstride
        self.dqk, self.scale, self.hw = dqk, scale, half_window
        self.len = T // dil
        self.blk = min(blk, self.len)
        self.nb = self.len // self.blk
        self.band = half_window is not None
        self.oblk = self.blk if self.band or oblk is None else min(oblk, self.len)
        self.steps = 3 if self.band else self.len // self.oblk

    def other(self, i, s):
        if self.band:
            nom = i - 1 + s
            return jnp.minimum(jnp.maximum(nom, 0), self.nb - 1), nom
        return s, s

    def chains(self, a):
        return a.reshape(self.len, self.dil * a.shape[1])

    def row_chunks(self, rows):
        assert not self.band
        sub = min(DENSE_SUB, rows)
        return [slice(c * sub, (c + 1) * sub) for c in range(rows // sub)]

    def unchain(self, a, cols):
        return a.reshape(self.T, cols)

    def mask(self, q_nom, k_nom):
        if not self.band:
            return None
        qpos = q_nom * self.blk + lax.broadcasted_iota(jnp.int32, (self.blk, self.blk), 0)
        kpos = k_nom * self.blk + lax.broadcasted_iota(jnp.int32, (self.blk, self.blk), 1)
        ok = jnp.abs(qpos - kpos) <= self.hw
        for pos in (qpos, kpos):
            ok = jnp.logical_and(ok, jnp.logical_and(pos >= 0, pos < self.len))
        return ok


def _scores(cfg, q, k, q_nom, k_nom):
    s = lax.dot_general(q, k, (((1,), (1,)), ((), ())), preferred_element_type=F32) * cfg.scale
    ok = cfg.mask(q_nom, k_nom)
    return s if ok is None else jnp.where(ok, s, NEG)


def flash_fwd(cfg, q, k, v, name, out_dtype, sink=None):
    blk, dqk = cfg.blk, cfg.dqk
    has_sink = sink is not None

    def body(*refs):
        if has_sink:
            sink_ref, refs = refs[0], refs[1:]
        q_ref, k_ref, v_ref, o_ref, lse_ref, m_sc, l_sc, acc_sc = refs
        i, s = pl.program_id(2), pl.program_id(3)

        @pl.when(s == 0)
        def _():
            if has_sink:
                m_sc[...] = jnp.broadcast_to(sink_ref[0, :1, :], m_sc.shape)
                l_sc[...] = jnp.ones_like(l_sc)
            else:
                m_sc[...] = jnp.full(m_sc.shape, NEG, F32)
                l_sc[...] = jnp.zeros_like(l_sc)
            acc_sc[...] = jnp.zeros_like(acc_sc)

        _, k_nom = cfg.other(i, s)
        k, v = k_ref[...], v_ref[...]
        for rows in cfg.row_chunks(blk):
            sc = _scores(cfg, q_ref[rows, :], k, i, k_nom)
            m_prev = m_sc[rows, :]
            m_new = jnp.maximum(m_prev, jnp.max(sc, axis=-1, keepdims=True))
            p = jnp.exp(sc - m_new[:, :1])
            alpha = jnp.exp(m_prev - m_new)
            l_sc[rows, :] = alpha * l_sc[rows, :] + jnp.sum(p, axis=-1, keepdims=True)
            acc_sc[rows, :] = alpha * acc_sc[rows, :] + jnp.dot(p.astype(BF), v, preferred_element_type=F32)
            m_sc[rows, :] = m_new

        @pl.when(s == cfg.steps - 1)
        def _():
            o_ref[...] = (acc_sc[...] / l_sc[...]).astype(o_ref.dtype)
            lse_ref[...] = m_sc[...] + jnp.log(l_sc[...])

    g = cfg.group
    q_spec = pl.BlockSpec((blk, dqk), lambda r, h, i, s: (i, r * cfg.qc + cfg.q0 + h))
    k_spec = pl.BlockSpec((cfg.oblk, dqk), lambda r, h, i, s: (cfg.other(i, s)[0], r * cfg.kc + cfg.k0 + h // g))
    v_spec = pl.BlockSpec((cfg.oblk, LANES),
                          lambda r, h, i, s: (cfg.other(i, s)[0], r * cfg.vc + cfg.v0 + cfg.vstride * (h // g)))
    o_spec = pl.BlockSpec((blk, LANES), lambda r, h, i, s: (i, r * cfg.hq + h))
    in_specs = [q_spec, k_spec, v_spec]
    operands = [cfg.chains(q), cfg.chains(k), cfg.chains(v)]
    if has_sink:
        in_specs.insert(0, pl.BlockSpec((1, SUBLANES, LANES), lambda r, h, i, s: (h, 0, 0)))
        operands.insert(0, sink)
    cols = cfg.dil * cfg.hq * LANES
    o, lse = pl.pallas_call(
        body, out_shape=[jax.ShapeDtypeStruct((cfg.len, cols), out_dtype), jax.ShapeDtypeStruct((cfg.len, cols), F32)],
        grid=(cfg.dil, cfg.hq, cfg.nb, cfg.steps), in_specs=in_specs, out_specs=[o_spec, o_spec],
        scratch_shapes=[pltpu.VMEM((blk, LANES), F32)] * 3,
        compiler_params=_params(("parallel", "parallel", "parallel", "arbitrary")), name=name,
    )(*operands)
    return cfg.unchain(o, cfg.hq * LANES), cfg.unchain(lse, cfg.hq * LANES)


def flash_dq(cfg, q, k, v, do, o, lse, name, sink=None):
    blk, dqk = cfg.blk, cfg.dqk
    has_sink = sink is not None

    def body(*refs):
        if has_sink:
            sink_ref, refs = refs[0], refs[1:]
        q_ref, k_ref, v_ref, do_ref, o_ref, lse_ref = refs[:6]
        dq_ref = refs[6]
        dq_sc, delta_sc = refs[-2:]
        i, s = pl.program_id(2), pl.program_id(3)

        @pl.when(s == 0)
        def _():
            dq_sc[...] = jnp.zeros_like(dq_sc)
            delta = jnp.sum(do_ref[...].astype(F32) * o_ref[...].astype(F32), axis=-1, keepdims=True)
            delta_sc[...] = jnp.broadcast_to(delta, delta_sc.shape)

        _, k_nom = cfg.other(i, s)
        k, v = k_ref[...], v_ref[...]
        for rows in cfg.row_chunks(blk):
            sc = _scores(cfg, q_ref[rows, :], k, i, k_nom)
            p = jnp.exp(sc - lse_ref[rows, :1])
            dp = lax.dot_general(do_ref[rows, :], v, (((1,), (1,)), ((), ())), preferred_element_type=F32)
            ds = p * (dp - delta_sc[rows, :1]) * cfg.scale
            dq_sc[rows, :] += jnp.dot(ds.astype(BF), k, preferred_element_type=F32)

        @pl.when(s == cfg.steps - 1)
        def _():
            dq_ref[...] = dq_sc[...].astype(dq_ref.dtype)
            if has_sink:
                ps = jnp.exp(sink_ref[0, :1, :] - lse_ref[...])
                part = -jnp.sum(ps * delta_sc[...], axis=0, keepdims=True)
                refs[7][...] = jnp.broadcast_to(part, refs[7].shape)

    g = cfg.group
    q_spec = pl.BlockSpec((blk, dqk), lambda r, h, i, s: (i, r * cfg.qc + cfg.q0 + h))
    k_spec = pl.BlockSpec((cfg.oblk, dqk), lambda r, h, i, s: (cfg.other(i, s)[0], r * cfg.kc + cfg.k0 + h // g))
    v_spec = pl.BlockSpec((cfg.oblk, LANES),
                          lambda r, h, i, s: (cfg.other(i, s)[0], r * cfg.vc + cfg.v0 + cfg.vstride * (h // g)))
    o_spec = pl.BlockSpec((blk, LANES), lambda r, h, i, s: (i, r * cfg.hq + h))
    dq_spec = pl.BlockSpec((blk, dqk), lambda r, h, i, s: (i, r * cfg.hq + h))
    in_specs = [q_spec, k_spec, v_spec, o_spec, o_spec, o_spec]
    operands = [cfg.chains(q), cfg.chains(k), cfg.chains(v), cfg.chains(do), cfg.chains(o), cfg.chains(lse)]
    out_shape = [jax.ShapeDtypeStruct((cfg.len, cfg.dil * cfg.hq * dqk), BF)]
    out_specs = [dq_spec]
    if has_sink:
        in_specs.insert(0, pl.BlockSpec((1, SUBLANES, LANES), lambda r, h, i, s: (h, 0, 0)))
        operands.insert(0, sink)
        out_shape.append(jax.ShapeDtypeStruct((cfg.hq * cfg.nb * SUBLANES, LANES), F32))
        out_specs.append(pl.BlockSpec((SUBLANES, LANES), lambda r, h, i, s: (h * cfg.nb + i, 0)))
    outs = pl.pallas_call(
        body, out_shape=out_shape, grid=(cfg.dil, cfg.hq, cfg.nb, cfg.steps), in_specs=in_specs,
        out_specs=out_specs, scratch_shapes=[pltpu.VMEM((blk, dqk), F32), pltpu.VMEM((blk, LANES), F32)],
        compiler_params=_params(("parallel", "parallel", "parallel", "arbitrary")), name=name,
    )(*operands)
    dq = cfg.unchain(outs[0], cfg.hq * dqk)
    if has_sink:
        return dq, outs[1].reshape(cfg.hq, cfg.nb, SUBLANES, LANES)[:, :, 0, :]
    return dq


def flash_dkv(cfg, q, k, v, do, o, lse, name, out_dtype, add=None):
    blk, dqk, g, nw = cfg.blk, cfg.dqk, cfg.group, cfg.steps
    has_add = add is not None

    def body(*refs):
        k_ref, v_ref, q_ref, do_ref, o_ref, lse_ref = refs[:6]
        pos = 8 if has_add else 6
        dk_ref, dv_ref = refs[pos:pos + 2]
        dk_sc, dv_sc = refs[-2:]
        i, j = pl.program_id(2), pl.program_id(3)

        @pl.when(j == 0)
        def _():
            dk_sc[...] = jnp.zeros_like(dk_sc)
            dv_sc[...] = jnp.zeros_like(dv_sc)

        _, q_nom = cfg.other(i, j % nw)
        k, v = k_ref[...], v_ref[...]
        for rows in cfg.row_chunks(cfg.oblk):
            q = q_ref[rows, :]
            do = do_ref[rows, :]
            sc = _scores(cfg, q, k, q_nom, i)
            p = jnp.exp(sc - lse_ref[rows, :1])
            delta = jnp.sum(do.astype(F32) * o_ref[rows, :].astype(F32), axis=-1, keepdims=True)
            dv_sc[...] += lax.dot_general(p.astype(BF), do, (((0,), (0,)), ((), ())), preferred_element_type=F32)
            dp = lax.dot_general(do, v, (((1,), (1,)), ((), ())), preferred_element_type=F32)
            ds = p * (dp - delta) * cfg.scale
            dk_sc[...] += lax.dot_general(ds.astype(BF), q, (((0,), (0,)), ((), ())), preferred_element_type=F32)

        @pl.when(j == g * nw - 1)
        def _():
            dk, dv = dk_sc[...], dv_sc[...]
            if has_add:
                dk, dv = dk + refs[6][...].astype(F32), dv + refs[7][...].astype(F32)
            dk_ref[...] = dk.astype(dk_ref.dtype)
            dv_ref[...] = dv.astype(dv_ref.dtype)

    def qrow(i, j):
        return cfg.other(i, j % nw)[0]

    k_spec = pl.BlockSpec((blk, dqk), lambda r, h, i, j: (i, r * cfg.kc + cfg.k0 + h))
    v_spec = pl.BlockSpec((blk, LANES), lambda r, h, i, j: (i, r * cfg.vc + cfg.v0 + cfg.vstride * h))
    q_spec = pl.BlockSpec((cfg.oblk, dqk), lambda r, h, i, j: (qrow(i, j), r * cfg.qc + cfg.q0 + h * g + j // nw))
    o_spec = pl.BlockSpec((cfg.oblk, LANES), lambda r, h, i, j: (qrow(i, j), r * cfg.hq + h * g + j // nw))
    dk_spec = pl.BlockSpec((blk, dqk), lambda r, h, i, j: (i, r * cfg.hkv + h))
    dv_spec = pl.BlockSpec((blk, LANES), lambda r, h, i, j: (i, r * cfg.hkv + h))
    in_specs = [k_spec, v_spec, q_spec, o_spec, o_spec, o_spec]
    operands = [cfg.chains(k), cfg.chains(v), cfg.chains(q), cfg.chains(do), cfg.chains(o), cfg.chains(lse)]
    if has_add:
        in_specs += [dk_spec, dv_spec]
        operands += [cfg.chains(add[0]), cfg.chains(add[1])]
    dk, dv = pl.pallas_call(
        body,
        out_shape=[jax.ShapeDtypeStruct((cfg.len, cfg.dil * cfg.hkv * dqk), out_dtype),
                   jax.ShapeDtypeStruct((cfg.len, cfg.dil * cfg.hkv * LANES), out_dtype)],
        grid=(cfg.dil, cfg.hkv, cfg.nb, g * nw), in_specs=in_specs, out_specs=[dk_spec, dv_spec],
        scratch_shapes=[pltpu.VMEM((blk, dqk), F32), pltpu.VMEM((blk, LANES), F32)],
        compiler_params=_params(("parallel", "parallel", "parallel", "arbitrary")), name=name,
    )(*operands)
    return cfg.unchain(dk, cfg.hkv * dqk), cfg.unchain(dv, cfg.hkv * LANES)


class Band:
    def __init__(self, T, dil, hq, group, per, qc, q0, kc, k0, vc, v0, scale, hw, blk):
        self.T, self.dil, self.hq, self.group, self.per = T, dil, hq, group, per
        self.pk = per // group
        self.hkv = hq // group
        self.scale, self.hw = scale, hw
        self.len = T // dil
        self.blk = min(blk, self.len)
        self.nb = self.len // self.blk
        self.win = self.blk + 2 * hw
        self.qcol = lambda r: (r * qc + q0) // per
        self.kcol = lambda r: (r * kc + k0) // self.pk
        self.vcol = lambda r: (r * vc + v0) // self.pk
        self.ocol = lambda r: (r * hq) // per
        self.dkcol = lambda r: (r * self.hkv) // self.pk
        assert hw <= self.blk and qc % per == 0 and q0 % per == 0 and kc % self.pk == 0 and k0 % self.pk == 0
        assert vc % self.pk == 0 and v0 % self.pk == 0

    def chains(self, a):
        return a.reshape(self.len, self.dil * a.shape[1])

    def rows3(self, width, col):
        nb = self.nb
        return [pl.BlockSpec((self.blk, width), lambda r, h, i: (jnp.maximum(i - 1, 0), col(r) + h)),
                pl.BlockSpec((self.blk, width), lambda r, h, i: (i, col(r) + h)),
                pl.BlockSpec((self.blk, width), lambda r, h, i: (jnp.minimum(i + 1, nb - 1), col(r) + h))]

    def window(self, prev, cur, nxt, j):
        cols = slice(j * LANES, (j + 1) * LANES)
        return jnp.concatenate([prev[self.blk - self.hw:, cols], cur[:, cols], nxt[:self.hw, cols]], axis=0)

    def valid(self, i, window_is_rows):
        shape = (self.win, self.blk) if window_is_rows else (self.blk, self.win)
        wdim = 0 if window_is_rows else 1
        bpos = i * self.blk + lax.broadcasted_iota(jnp.int32, shape, 1 - wdim)
        wpos = i * self.blk - self.hw + lax.broadcasted_iota(jnp.int32, shape, wdim)
        ok = jnp.abs(bpos - wpos) <= self.hw
        return jnp.logical_and(ok, jnp.logical_and(wpos >= 0, wpos < self.len))


def band_fwd(cfg, q, k, v, name, out_dtype, sink=None):
    blk, per, pk = cfg.blk, cfg.per, cfg.pk
    has_sink = sink is not None

    def body(*refs):
        if has_sink:
            sink_ref, refs = refs[0], refs[1:]
        q_ref, kp, kc, kn, vp, vc, vn, o_ref, lse_ref = refs
        ok = cfg.valid(pl.program_id(2), False)
        for j in range(per):
            jk = j // cfg.group
            if j % cfg.group == 0:
                kw = cfg.window(kp, kc, kn, jk)
                vw = cfg.window(vp, vc, vn, jk)
            cols = slice(j * LANES, (j + 1) * LANES)
            s = lax.dot_general(q_ref[:, cols], kw, (((1,), (1,)), ((), ())), preferred_element_type=F32) * cfg.scale
            s = jnp.where(ok, s, NEG)
            m = jnp.max(s, axis=-1, keepdims=True)
            if has_sink:
                sk = sink_ref[j, :1, :1]
                m = jnp.maximum(m, sk)
            e = jnp.exp(s - m)
            den = jnp.sum(e, axis=-1, keepdims=True)
            if has_sink:
                den = den + jnp.exp(sk - m)
            o = jnp.dot(e.astype(BF), vw, preferred_element_type=F32) / den
            o_ref[:, cols] = o.astype(o_ref.dtype)
            lse_ref[:, cols] = jnp.broadcast_to(m + jnp.log(den), (blk, LANES))

    q_spec = pl.BlockSpec((blk, per * LANES), lambda r, h, i: (i, cfg.qcol(r) + h))
    o_spec = pl.BlockSpec((blk, per * LANES), lambda r, h, i: (i, cfg.ocol(r) + h))
    in_specs = [q_spec] + cfg.rows3(pk * LANES, cfg.kcol) + cfg.rows3(pk * LANES, cfg.vcol)
    kc_, vc_ = cfg.chains(k), cfg.chains(v)
    operands = [cfg.chains(q), kc_, kc_, kc_, vc_, vc_, vc_]
    if has_sink:
        in_specs.insert(0, pl.BlockSpec((per, SUBLANES, LANES), lambda r, h, i: (h, 0, 0)))
        operands.insert(0, sink)
    cols = cfg.dil * cfg.hq * LANES
    o, lse = pl.pallas_call(
        body, out_shape=[jax.ShapeDtypeStruct((cfg.len, cols), out_dtype), jax.ShapeDtypeStruct((cfg.len, cols), F32)],
        grid=(cfg.dil, cfg.hq // per, cfg.nb), in_specs=in_specs, out_specs=[o_spec, o_spec],
        compiler_params=_params(("parallel", "parallel", "parallel")), name=name,
    )(*operands)
    return o.reshape(cfg.T, cfg.hq * LANES), lse.reshape(cfg.T, cfg.hq * LANES)


def band_dq(cfg, q, k, v, do, o, lse, name, sink=None):
    blk, per, pk = cfg.blk, cfg.per, cfg.pk
    has_sink = sink is not None

    def body(*refs):
        if has_sink:
            sink_ref, refs = refs[0], refs[1:]
        q_ref, kp, kc, kn, vp, vc, vn, do_ref, o_ref, lse_ref, dq_ref = refs[:11]
        ok = cfg.valid(pl.program_id(2), False)
        for j in range(per):
            jk = j // cfg.group
            if j % cfg.group == 0:
                kw = cfg.window(kp, kc, kn, jk)
                vw = cfg.window(vp, vc, vn, jk)
            cols = slice(j * LANES, (j + 1) * LANES)
            do = do_ref[:, cols]
            lse = lse_ref[:, j * LANES:j * LANES + 1]
            delta = jnp.sum(do.astype(F32) * o_ref[:, cols].astype(F32), axis=-1, keepdims=True)
            s = lax.dot_general(q_ref[:, cols], kw, (((1,), (1,)), ((), ())), preferred_element_type=F32) * cfg.scale
            p = jnp.exp(jnp.where(ok, s, NEG) - lse)
            dp = lax.dot_general(do, vw, (((1,), (1,)), ((), ())), preferred_element_type=F32)
            ds = p * (dp - delta) * cfg.scale
            dq_ref[:, cols] = jnp.dot(ds.astype(BF), kw, preferred_element_type=F32).astype(dq_ref.dtype)
            if has_sink:
                part = -jnp.sum(jnp.exp(sink_ref[j, :1, :1] - lse) * delta, axis=0, keepdims=True)
                refs[11][j * SUBLANES:(j + 1) * SUBLANES, :] = jnp.broadcast_to(part, (SUBLANES, LANES))

    q_spec = pl.BlockSpec((blk, per * LANES), lambda r, h, i: (i, cfg.qcol(r) + h))
    o_spec = pl.BlockSpec((blk, per * LANES), lambda r, h, i: (i, cfg.ocol(r) + h))
    in_specs = [q_spec] + cfg.rows3(pk * LANES, cfg.kcol) + cfg.rows3(pk * LANES, cfg.vcol) + [o_spec] * 3
    kc_, vc_ = cfg.chains(k), cfg.chains(v)
    operands = [cfg.chains(q), kc_, kc_, kc_, vc_, vc_, vc_, cfg.chains(do), cfg.chains(o), cfg.chains(lse)]
    out_shape = [jax.ShapeDtypeStruct((cfg.len, cfg.dil * cfg.hq * LANES), BF)]
    out_specs = [o_spec]
    if has_sink:
        in_specs.insert(0, pl.BlockSpec((per, SUBLANES, LANES), lambda r, h, i: (h, 0, 0)))
        operands.insert(0, sink)
        out_shape.append(jax.ShapeDtypeStruct((cfg.hq // per, cfg.nb, per * SUBLANES, LANES), F32))
        out_specs.append(pl.BlockSpec((None, None, per * SUBLANES, LANES), lambda r, h, i: (h, i, 0, 0)))
    outs = pl.pallas_call(
        body, out_shape=out_shape, grid=(cfg.dil, cfg.hq // per, cfg.nb), in_specs=in_specs, out_specs=out_specs,
        compiler_params=_params(("parallel", "parallel", "parallel")), name=name,
    )(*operands)
    dq = outs[0].reshape(cfg.T, cfg.hq * LANES)
    return (dq, outs[1]) if has_sink else dq


def band_dkv(cfg, q, k, v, do, o, lse, name, out_dtype, add=None, dv_into=None):
    blk, per, pk, group = cfg.blk, cfg.per, cfg.pk, cfg.group
    has_add = add is not None
    carried = dv_into is not None
    assert not carried or cfg.dil == 1

    def body(*refs):
        k_ref, v_ref = refs[:2]
        qs, dos, os_, lses = refs[2:5], refs[5:8], refs[8:11], refs[11:14]
        pos = 14 + (2 if has_add else 0) + (1 if carried else 0)
        dk_ref, dv_ref = refs[pos:pos + 2]
        ok = cfg.valid(pl.program_id(2), True)
        for jk in range(pk):
            kcols = slice(jk * LANES, (jk + 1) * LANES)
            kt, vt = k_ref[:, kcols], v_ref[:, kcols]
            dk = jnp.zeros((blk, LANES), F32)
            dv = jnp.zeros((blk, LANES), F32)
            for g in range(group):
                j = jk * group + g
                qw = cfg.window(*qs, j)
                dow = cfg.window(*dos, j)
                lse = cfg.window(*lses, j)[:, :1]
                delta = jnp.sum(dow.astype(F32) * cfg.window(*os_, j).astype(F32), axis=-1, keepdims=True)
                s = lax.dot_general(qw, kt, (((1,), (1,)), ((), ())), preferred_element_type=F32) * cfg.scale
                p = jnp.exp(jnp.where(ok, s, NEG) - lse)
                dv = dv + lax.dot_general(p.astype(BF), dow, (((0,), (0,)), ((), ())), preferred_element_type=F32)
                dp = lax.dot_general(dow, vt, (((1,), (1,)), ((), ())), preferred_element_type=F32)
                ds = p * (dp - delta) * cfg.scale
                dk = dk + lax.dot_general(ds.astype(BF), qw, (((0,), (0,)), ((), ())), preferred_element_type=F32)
            if has_add:
                dk, dv = dk + refs[14][:, kcols].astype(F32), dv + refs[15][:, kcols].astype(F32)
            dk_ref[:, kcols] = dk.astype(dk_ref.dtype)
            dv_ref[:, kcols] = dv.astype(dv_ref.dtype)

    k_spec = pl.BlockSpec((blk, pk * LANES), lambda r, h, i: (i, cfg.kcol(r) + h))
    v_spec = pl.BlockSpec((blk, pk * LANES), lambda r, h, i: (i, cfg.vcol(r) + h))
    d_spec = pl.BlockSpec((blk, pk * LANES), lambda r, h, i: (i, cfg.dkcol(r) + h))
    in_specs = [k_spec, v_spec] + cfg.rows3(per * LANES, cfg.qcol) + cfg.rows3(per * LANES, cfg.ocol) * 3
    qc_, doc, oc, lc = cfg.chains(q), cfg.chains(do), cfg.chains(o), cfg.chains(lse)
    operands = [cfg.chains(k), cfg.chains(v), qc_, qc_, qc_, doc, doc, doc, oc, oc, oc, lc, lc, lc]
    if has_add:
        in_specs += [d_spec, d_spec]
        operands += [cfg.chains(add[0]), cfg.chains(add[1])]
    cols = cfg.dil * cfg.hkv * LANES
    out_shape = [jax.ShapeDtypeStruct((cfg.len, cols), out_dtype)] * 2
    out_specs = [d_spec, d_spec]
    aliases = {}
    if carried:
        buf, blocks, block0 = dv_into
        out_shape[1] = jax.ShapeDtypeStruct((cfg.T, blocks * LANES), BF)
        out_specs[1] = pl.BlockSpec((blk, pk * LANES), lambda r, h, i: (i, block0 // pk + h))
        aliases = {len(operands): 1}
        in_specs.append(pl.BlockSpec(memory_space=pl.ANY))
        operands.append(buf)
    dk, dv = pl.pallas_call(
        body, out_shape=out_shape, grid=(cfg.dil, cfg.hq // per, cfg.nb), in_specs=in_specs, out_specs=out_specs,
        input_output_aliases=aliases, compiler_params=_params(("parallel", "parallel", "parallel")), name=name,
    )(*operands)
    return dk.reshape(cfg.T, cfg.hkv * LANES), (dv if carried else dv.reshape(cfg.T, cfg.hkv * LANES))


HBM_SPEC = pl.BlockSpec(memory_space=pltpu.HBM)


def _place():
    x, y, c = lax.axis_index("x"), lax.axis_index("y"), lax.axis_index("c")
    chips = [(1 - x, y), (x, 1 - y), (1 - x, 1 - y)]
    return x, y, c, chips


def gather_weights(shards):
    n = len(shards)

    def body(*refs):
        ins, outs = refs[:n], refs[n:2 * n]
        send_sems, recv_sems, local_sems = refs[2 * n:]
        x, y, c, chips = _place()
        me = 2 * x + y
        sibling = (x, y, 1 - c)

        def copy(w, k, src, chip_of_block, half, to):
            return pltpu.make_async_remote_copy(
                src_ref=src, dst_ref=outs[w].at[chip_of_block, half], send_sem=send_sems.at[6 * w + k],
                recv_sem=recv_sems.at[6 * w + k], device_id=to, device_id_type=MESH)

        started = []
        local = []
        for w in range(n):
            own = pltpu.make_async_copy(ins[w], outs[w].at[me], local_sems.at[w])
            own.start()
            local.append(own)
            for j, chip in enumerate(chips):
                cp = copy(w, j, ins[w].at[c], me, c, (*chip, c))
                cp.start()
                started.append(cp)
        for w in range(n):
            for j, (cx, cy) in enumerate(chips):
                them = 2 * cx + cy
                copy(w, j, ins[w].at[c], them, c, (cx, cy, c)).wait_recv()
                fwd = copy(w, 3 + j, outs[w].at[them, c], them, c, sibling)
                fwd.start()
                started.append(fwd)
        for w in range(n):
            for j, (cx, cy) in enumerate(chips):
                copy(w, 3 + j, ins[w].at[c], 2 * cx + cy, 1 - c, sibling).wait_recv()
        for cp in started:
            cp.wait_send()
        for own in local:
            own.wait()

    return pl.pallas_call(
        body, out_shape=[jax.ShapeDtypeStruct((4,) + s.shape, s.dtype) for s in shards],
        in_specs=[HBM_SPEC] * n, out_specs=[HBM_SPEC] * n,
        scratch_shapes=[pltpu.SemaphoreType.DMA((6 * n,)), pltpu.SemaphoreType.DMA((6 * n,)),
                        pltpu.SemaphoreType.DMA((n,))],
        name="gather_weights",
    )(*shards)


def _core_index():
    return lax.axis_index("c").astype(jnp.int32).reshape(1)


def presum_core_halves(g2, core, name):
    _, rows, cols = g2.shape
    tr = _row_tile(rows, cols, 1 << 20)
    nb = rows // tr
    g2 = g2.reshape(2 * rows, cols)

    def body(core_ref, mine_ref, other_ref, out_ref, land, send_sems, recv_sems):
        x, y, c, _ = _place()
        slot = pl.program_id(0) % 2
        cp = pltpu.make_async_remote_copy(
            src_ref=other_ref, dst_ref=land.at[slot], send_sem=send_sems.at[slot], recv_sem=recv_sems.at[slot],
            device_id=(x, y, 1 - c), device_id_type=MESH)
        cp.start()
        cp.wait_recv()
        out_ref[...] = (mine_ref[...] + land[slot]).astype(out_ref.dtype)
        cp.wait_send()

    grid_spec = pltpu.PrefetchScalarGridSpec(
        num_scalar_prefetch=1, grid=(nb,),
        in_specs=[pl.BlockSpec((tr, cols), lambda i, core: (core[0] * nb + i, 0)),
                  pl.BlockSpec((tr, cols), lambda i, core: ((1 - core[0]) * nb + i, 0))],
        out_specs=pl.BlockSpec((tr, cols), lambda i, core: (i, 0)),
        scratch_shapes=[pltpu.VMEM((2, tr, cols), F32), pltpu.SemaphoreType.DMA((2,)), pltpu.SemaphoreType.DMA((2,))])
    return pl.pallas_call(
        body, out_shape=jax.ShapeDtypeStruct((rows, cols), BF), grid_spec=grid_spec,
        compiler_params=_params(("arbitrary",)), name=name,
    )(core, g2, g2)


def sum_and_swap(landed, name):
    n, rows, cols = landed.shape
    tr = _row_tile(rows, cols)

    def body(*refs):
        slots = refs[:n]
        mine_ref, theirs_ref, out_buf, land, send_sems, recv_sems = refs[n:]
        x, y, c, _ = _place()
        slot = pl.program_id(0) % 2
        tot = slots[0][...].astype(F32)
        for r in slots[1:]:
            tot = tot + r[...].astype(F32)
        mine_ref[...] = tot
        out_buf[slot] = tot
        cp = pltpu.make_async_remote_copy(
            src_ref=out_buf.at[slot], dst_ref=land.at[slot], send_sem=send_sems.at[slot], recv_sem=recv_sems.at[slot],
            device_id=(x, y, 1 - c), device_id_type=MESH)
        cp.start()
        cp.wait_recv()
        theirs_ref[...] = land[slot]
        cp.wait_send()

    specs = [pl.BlockSpec((None, tr, cols), functools.partial(lambda s, i: (s, i, 0), s)) for s in range(n)]
    row = pl.BlockSpec((tr, cols), lambda i: (i, 0))
    return pl.pallas_call(
        body, out_shape=[jax.ShapeDtypeStruct((rows, cols), F32)] * 2, grid=(rows // tr,), in_specs=specs,
        out_specs=[row, row],
        scratch_shapes=[pltpu.VMEM((2, tr, cols), F32), pltpu.VMEM((2, tr, cols), F32),
                        pltpu.SemaphoreType.DMA((2,)), pltpu.SemaphoreType.DMA((2,))],
        compiler_params=_params(("arbitrary",)), name=name,
    )(*([landed] * n))


def scatter_partials(parts):
    n = len(parts)

    def body(*refs):
        ins, outs = refs[:n], refs[n:2 * n]
        send_sems, recv_sems, local_sems = refs[2 * n:]
        x, y, c, chips = _place()
        me = 2 * x + y
        started = []
        for w in range(n):
            own = pltpu.make_async_copy(ins[w].at[me], outs[w].at[me], local_sems.at[w])
            own.start()
            started.append(own)
        sends = []
        for w in range(n):
            for j, (cx, cy) in enumerate(chips):
                cp = pltpu.make_async_remote_copy(
                    src_ref=ins[w].at[2 * cx + cy], dst_ref=outs[w].at[me], send_sem=send_sems.at[3 * w + j],
                    recv_sem=recv_sems.at[3 * w + j], device_id=(cx, cy, c), device_id_type=MESH)
                cp.start()
                sends.append(cp)
        for w in range(n):
            for j, (cx, cy) in enumerate(chips):
                pltpu.make_async_remote_copy(
                    src_ref=ins[w].at[me], dst_ref=outs[w].at[2 * cx + cy], send_sem=send_sems.at[3 * w + j],
                    recv_sem=recv_sems.at[3 * w + j], device_id=(cx, cy, c), device_id_type=MESH).wait_recv()
        for cp in sends:
            cp.wait_send()
        for own in started:
            own.wait()

    return pl.pallas_call(
        body, out_shape=[jax.ShapeDtypeStruct(p.shape, p.dtype) for p in parts],
        in_specs=[HBM_SPEC] * n, out_specs=[HBM_SPEC] * n,
        scratch_shapes=[pltpu.SemaphoreType.DMA((3 * n,)), pltpu.SemaphoreType.DMA((3 * n,)),
                        pltpu.SemaphoreType.DMA((n,))],
        name="scatter_partials",
    )(*parts)


def adamw_halves(w, mine, theirs, m, v, core, name):
    rows, cols = w.shape
    tr = _row_tile(rows // 2, cols, 1 << 18)
    nh = rows // 2 // tr

    def body(core_ref, w_ref, a_ref, b_ref, m_ref, v_ref, g_out, d_out, m_out, v_out):
        g = jnp.where(pl.program_id(0) // nh == core_ref[0], a_ref[...], b_ref[...])
        d_out[...], m_out[...], v_out[...] = _adam_fn(w_ref[...], g, m_ref[...], v_ref[...])
        g_out[...] = g

    full = pl.BlockSpec((tr, cols), lambda i, core: (i, 0))
    half = pl.BlockSpec((tr, cols), lambda i, core: (i % nh, 0))
    grid_spec = pltpu.PrefetchScalarGridSpec(
        num_scalar_prefetch=1, grid=(rows // tr,), in_specs=[full, half, half, full, full], out_specs=[full] * 4)
    return pl.pallas_call(
        body, out_shape=[jax.ShapeDtypeStruct((rows, cols), F32)] * 4, grid_spec=grid_spec,
        compiler_params=_params(("parallel",)), name=name,
    )(core, w, mine, theirs, m, v)


def gather_small(vec):
    rows = vec.shape[0]

    def body(v_ref, out_ref, send_sems, recv_sems):
        x, y, c, _ = _place()
        me = 4 * x + 2 * y + c
        out_ref[me] = v_ref[...]
        flips = [(dx, dy, dc) for dx in (0, 1) for dy in (0, 1) for dc in (0, 1)][1:]

        def peer(f):
            return tuple(1 - a if d else a for a, d in zip((x, y, c), f))

        def copy(k, block, to):
            return pltpu.make_async_remote_copy(
                src_ref=v_ref, dst_ref=out_ref.at[block], send_sem=send_sems.at[k], recv_sem=recv_sems.at[k],
                device_id=to, device_id_type=MESH)

        sends = [copy(k, me, peer(f)) for k, f in enumerate(flips)]
        for cp in sends:
            cp.start()
        for k, f in enumerate(flips):
            px, py, pc = peer(f)
            copy(k, 4 * px + 2 * py + pc, peer(f)).wait_recv()
        for cp in sends:
            cp.wait_send()

    vm = pl.BlockSpec(memory_space=pltpu.VMEM)
    return pl.pallas_call(
        body, out_shape=jax.ShapeDtypeStruct((8, rows, SMALL_COLS), F32), in_specs=[vm], out_specs=vm,
        scratch_shapes=[pltpu.SemaphoreType.DMA((7,)), pltpu.SemaphoreType.DMA((7,))], name="gather_small",
    )(vec)


def sum_slots(a, out_dtype, name):
    n, rows, cols = a.shape
    tr = _row_tile(rows, cols)

    def body(*refs):
        tot = refs[0][...].astype(F32)
        for r in refs[1:n]:
            tot = tot + r[...].astype(F32)
        refs[n][...] = tot.astype(out_dtype)

    specs = [pl.BlockSpec((None, tr, cols), functools.partial(lambda s, i: (s, i, 0), s)) for s in range(n)]
    return pl.pallas_call(
        body, out_shape=jax.ShapeDtypeStruct((rows, cols), out_dtype), grid=(rows // tr,), in_specs=specs,
        out_specs=pl.BlockSpec((tr, cols), lambda i: (i, 0)), compiler_params=_params(("parallel",)), name=name,
    )(*([a] * n))


def _adam_fn(w, g, m, v):
    m = ADAM_B1 * m + (1.0 - ADAM_B1) * g
    v = ADAM_B2 * v + (1.0 - ADAM_B2) * (g * g)
    m_hat = m / (1.0 - ADAM_B1 ** ADAM_STEP)
    v_hat = v / (1.0 - ADAM_B2 ** ADAM_STEP)
    delta = -ADAM_LR * (m_hat / (jnp.sqrt(v_hat) + ADAM_EPS) + ADAM_WD * w)
    return delta, m, v


def adamw(w, g, m, v, name):
    return rowwise(_adam_fn, [w, g, m, v], [F32, F32, F32], name)


def _full_weight(name, gathered, local_shape):
    L, a, b = local_shape
    g = gathered.reshape((4, L, a, b))
    if SHARD_AXIS[name] == 1:
        return g.transpose(1, 0, 2, 3).reshape(L, 4 * a, b)
    return g.transpose(1, 2, 0, 3).reshape(L, a, 4 * b)


def _grad_slots(name, dw):
    L, a, b = dw.shape
    if SHARD_AXIS[name] == 1:
        s = dw.reshape(L, 4, a // 4, b).transpose(1, 0, 2, 3)
        rows, cols = L * (a // 4), b
    else:
        s = dw.reshape(L, a, 4, b // 4).transpose(2, 0, 1, 3)
        rows, cols = L * a, b // 4
    return s.reshape(4, 2, rows // 2, cols).transpose(1, 0, 2, 3)


def _attn_a(T):
    group = A_HEADS // A_KV_HEADS
    return Band(T, 1, A_HEADS, group, group, A_HEADS, 0, A_KV_HEADS, 0, A_HEADS + 2 * A_KV_HEADS,
                A_HEADS + A_KV_HEADS, 1.0 / math.sqrt(HEAD_DIM), A_HALF_WINDOW, BAND_BLOCK)


def _attn_b(T):
    return Attn(T, 1, B_HEADS, 1, B_HEADS, 0, B_HEADS, 0, 2 * B_HEADS, 1, 2, B_PAD, 1.0 / math.sqrt(B_QK), None,
                DENSE_BLOCK, DENSE_OTHER_BLOCK)


def _attn_c(T, group):
    window, dil = C_PATTERNS[group]
    return Band(T, dil, C_HEADS, 1, BAND_HEADS_PER_STEP, C_HEADS, 0, C_HEADS, 0, C_HEADS, 0,
                1.0 / math.sqrt(HEAD_DIM), window // 2 // dil, BAND_BLOCK)


def _pad_heads(a, axis_len_true, axis_len_pad):
    lead = a.shape[:-1]
    h = a.shape[-1] // axis_len_true
    a = a.reshape(lead + (h, axis_len_true))
    a = jnp.pad(a, [(0, 0)] * len(lead) + [(0, 0), (0, axis_len_pad - axis_len_true)])
    return a.reshape(lead + (h * axis_len_pad,))


def _unpad_heads(a, axis_len_true, axis_len_pad):
    lead = a.shape[:-1]
    h = a.shape[-1] // axis_len_pad
    return a.reshape(lead + (h, axis_len_pad))[..., :axis_len_true].reshape(lead + (h * axis_len_true,))


def _weight_grad(G, name, layer, a, dy, W, tag):
    layers, rows, cols = W[name].shape
    if name in SLOT_DIRECT:
        G[name] = matmul([(a, dy)], "tn", F32, tag, slot=Slot(name, layers, layer, rows, cols, 0, G.get(name)))
    else:
        G.setdefault(name, [None] * layers)[layer] = matmul([(a, dy)], "tn", F32, tag)


def _mixer_fwd(kind, slot, hn, W, S, tabs, tag):
    T = hn.shape[0]
    if kind == 0:
        cfg = _attn_a(T)
        qkv = matmul([(hn, W["a_w_in"][slot])], "nn", BF, tag + "_a_in")
        q = headnorm_fwd(qkv, W["a_q_norm"][slot], tabs["hd"], tag + "_a_qn", A_HEADS, 0, HEAD_DIM, HEAD_DIM)
        k = headnorm_fwd(qkv, W["a_k_norm"][slot], tabs["hd"], tag + "_a_kn", A_KV_HEADS, A_HEADS, HEAD_DIM, HEAD_DIM)
        sink = jnp.broadcast_to(W["a_sink"][slot][:, None, None], (A_HEADS, SUBLANES, LANES)).astype(F32)
        o, lse = band_fwd(cfg, q, k, qkv, tag + "_a_att", BF, sink=sink)
        S.update(qkv=qkv, q=q, k=k, o=o, lse=lse, sink=sink)
        return o
    if kind == 1:
        cfg = _attn_b(T)
        lat = matmul([(hn, W["b_w_in"][slot])], "nn", BF, tag + "_b_in")
        qn = rmsnorm_fwd(lat, W["b_q_lat_norm"][slot], tag + "_b_qlat", 0, B_Q_RANK)
        kvn = rmsnorm_fwd(lat, W["b_kv_lat_norm"][slot], tag + "_b_kvlat", 1, B_KV_RANK)
        qp = matmul([(qn, W["b_w_q_up_pad"][slot])], "nn", BF, tag + "_b_qup")
        kv = matmul([(kvn, W["b_w_kv_up"][slot])], "nn", BF, tag + "_b_kvup")
        k_rope = lat[:, B_Q_RANK + B_KV_RANK:]
        kpre = jnp.concatenate(
            [kv.reshape(T, B_HEADS, 2 * B_NOPE)[:, :, :B_NOPE],
             jnp.broadcast_to(k_rope[:, None, :], (T, B_HEADS, B_ROPE)),
             jnp.zeros((T, B_HEADS, B_PAD - B_QK), BF)], axis=-1).reshape(T, B_HEADS * B_PAD)
        q = headnorm_fwd(qp, W["b_q_norm_pad"][slot], tabs["b"], tag + "_b_qn", B_HEADS, 0, B_PAD, B_QK)
        k = headnorm_fwd(kpre, W["b_k_norm_pad"][slot], tabs["b"], tag + "_b_kn", B_HEADS, 0, B_PAD, B_QK)
        o, lse = flash_fwd(cfg, q, k, kv, tag + "_b_att", BF)
        S.update(lat=lat, qn=qn, kvn=kvn, qp=qp, kv=kv, kpre=kpre, q=q, k=k, o=o, lse=lse)
        return o
    qkv = matmul([(hn, W["c_w_in"][slot])], "nn", BF, tag + "_c_in")
    nq = C_GROUPS * C_HEADS
    qs = [headnorm_fwd(qkv, W["c_q_norm"][slot], tabs["hd"], f"{tag}_c_qn{g}", C_HEADS, g * C_HEADS, HEAD_DIM, HEAD_DIM)
          for g in range(C_GROUPS)]
    k = headnorm_fwd(qkv, W["c_k_norm"][slot], tabs["hd"], tag + "_c_kn", C_HEADS, nq, HEAD_DIM, HEAD_DIM)
    outs, lses = [], []
    v = qkv[:, (C_GROUPS + 1) * C_HEADS * HEAD_DIM:]
    for g in range(C_GROUPS):
        og, lg = band_fwd(_attn_c(T, g), qs[g], k, v, f"{tag}_c_att{g}", F32)
        outs.append(og)
        lses.append(lg)
    o, lse = rowwise(_merge_fn, outs + lses, [BF, F32], tag + "_c_merge")
    S.update(qkv=qkv, qs=qs, v=v, k=k, o=o, lse=lse)
    return o


def _mixer_bwd(kind, slot, hn, do, W, S, tabs, tag, G):
    T = hn.shape[0]
    if kind == 0:
        cfg = _attn_a(T)
        qkv = S["qkv"]
        dq, dsink = band_dq(cfg, S["q"], S["k"], qkv, do, S["o"], S["lse"], tag + "_a_dq", sink=S["sink"])
        blocks = A_HEADS + 2 * A_KV_HEADS
        dqkv, dgq = headnorm_bwd(qkv, W["a_q_norm"][slot], tabs["hd"], dq, tag + "_a_dqn", A_HEADS, 0, HEAD_DIM, HEAD_DIM,
                                 into=(None, blocks, 0))
        dk, dqkv = band_dkv(cfg, S["q"], S["k"], qkv, do, S["o"], S["lse"], tag + "_a_dkv", BF,
                            dv_into=(dqkv, blocks, A_HEADS + A_KV_HEADS))
        dqkv, dgk = headnorm_bwd(qkv, W["a_k_norm"][slot], tabs["hd"], dk, tag + "_a_dkn", A_KV_HEADS, A_HEADS,
                                 HEAD_DIM, HEAD_DIM, into=(dqkv, blocks, A_HEADS))
        _weight_grad(G, "a_w_in", slot, hn, dqkv, W, tag + "_a_dwin")
        G["a_q_norm"][slot], G["a_k_norm"][slot] = dgq, dgk
        parts = dsink.reshape(A_HEADS // cfg.per, cfg.nb, cfg.per, SUBLANES, LANES)[:, :, :, 0, 0]
        G["a_sink"][slot] = jnp.sum(parts, axis=1).reshape(A_HEADS)
        return matmul([(dqkv, W["a_w_in"][slot])], "nt", F32, tag + "_a_dhn")
    if kind == 1:
        cfg = _attn_b(T)
        kv = S["kv"]
        dq = flash_dq(cfg, S["q"], S["k"], kv, do, S["o"], S["lse"], tag + "_b_dq")
        dk, dv = flash_dkv(cfg, S["q"], S["k"], kv, do, S["o"], S["lse"], tag + "_b_dkv", BF)
        dqp, dgq = headnorm_bwd(S["qp"], W["b_q_norm_pad"][slot], tabs["b"], dq, tag + "_b_dqn", B_HEADS, 0, B_PAD, B_QK)
        dkp, dgk, dksum = headnorm_bwd(S["kpre"], W["b_k_norm_pad"][slot], tabs["b"], dk, tag + "_b_dkn", B_HEADS, 0,
                                       B_PAD, B_QK, head_sum=True)
        dkv = jnp.concatenate([dkp.reshape(T, B_HEADS, B_PAD)[:, :, :B_NOPE], dv.reshape(T, B_HEADS, LANES)],
                              axis=-1).reshape(T, B_HEADS * 2 * B_NOPE)
        _weight_grad(G, "b_w_kv_up", slot, S["kvn"], dkv, W, tag + "_b_dwkv")
        G["b_w_q_up"][slot] = _unpad_heads(matmul([(S["qn"], dqp)], "tn", F32, tag + "_b_dwq"), B_QK, B_PAD)
        dqn = matmul([(dqp, W["b_w_q_up_pad"][slot])], "nt", F32, tag + "_b_dqnorm")
        dkvn = matmul([(dkv, W["b_w_kv_up"][slot])], "nt", F32, tag + "_b_dkvnorm")
        dql, dg_q = rmsnorm_bwd(S["lat"], W["b_q_lat_norm"][slot], dqn, tag + "_b_dqlat", [BF], None, 0, B_Q_RANK)
        dkvl, dg_kv = rmsnorm_bwd(S["lat"], W["b_kv_lat_norm"][slot], dkvn, tag + "_b_dkvlat", [BF], None, 1, B_KV_RANK)
        dlat = jnp.concatenate([dql, dkvl, dksum[:, B_NOPE:B_QK].astype(BF)], axis=1)
        _weight_grad(G, "b_w_in", slot, hn, dlat, W, tag + "_b_dwin")
        G["b_q_norm"][slot], G["b_k_norm"][slot] = dgq[:B_QK], dgk[:B_QK]
        G["b_q_lat_norm"][slot], G["b_kv_lat_norm"][slot] = dg_q, dg_kv
        return matmul([(dlat, W["b_w_in"][slot])], "nt", F32, tag + "_b_dhn")
    qkv = S["qkv"]
    nq = C_GROUPS * C_HEADS
    blocks = (C_GROUPS + 2) * C_HEADS
    dqkv, dgq = None, 0.0
    for g in range(C_GROUPS):
        dq = band_dq(_attn_c(T, g), S["qs"][g], S["k"], S["v"], do, S["o"], S["lse"], f"{tag}_c_dq{g}")
        dqkv, dg = headnorm_bwd(qkv, W["c_q_norm"][slot], tabs["hd"], dq, f"{tag}_c_dqn{g}", C_HEADS, g * C_HEADS,
                                HEAD_DIM, HEAD_DIM, into=(dqkv, blocks, g * C_HEADS))
        dgq = dgq + dg
    acc = None
    for g in reversed(range(C_GROUPS)):
        into = (dqkv, blocks, (C_GROUPS + 1) * C_HEADS) if g == 0 else None
        acc = band_dkv(_attn_c(T, g), S["qs"][g], S["k"], S["v"], do, S["o"], S["lse"], f"{tag}_c_dkv{g}", F32,
                       add=acc, dv_into=into)
    dk, dqkv = acc
    dqkv, dgk = headnorm_bwd(qkv, W["c_k_norm"][slot], tabs["hd"], dk, tag + "_c_dkn", C_HEADS, nq, HEAD_DIM, HEAD_DIM,
                             into=(dqkv, blocks, nq))
    _weight_grad(G, "c_w_in", slot, hn, dqkv, W, tag + "_c_dwin")
    G["c_q_norm"][slot], G["c_k_norm"][slot] = dgq, dgk
    return matmul([(dqkv, W["c_w_in"][slot])], "nt", F32, tag + "_c_dhn")


MIXER_OUT = ("a_w_o", "b_w_o", "c_w_o")


def local_step(x, p, positions, loss_target, W):
    T = x.shape[0]
    tabs = {"hd": rope_tables(positions, HEAD_DIM, 0, PARTIAL_ROT), "b": rope_tables(positions, B_PAD, B_NOPE, B_ROPE)}
    W = dict(W)
    W["b_w_q_up_pad"] = _pad_heads(W["b_w_q_up"], B_QK, B_PAD)
    W["b_q_norm_pad"] = _pad_heads(W["b_q_norm"], B_QK, B_PAD)
    W["b_k_norm_pad"] = _pad_heads(W["b_k_norm"], B_QK, B_PAD)
    saved = []
    h = x
    for i in range(DEPTH):
        kind, slot = i % 3, i // 3
        tag = f"l{i}"
        S = {"h0": h}
        hn = rmsnorm_fwd(h, W["g_mix"][i], tag + "_mixnorm")
        o = _mixer_fwd(kind, slot, hn, W, S, tabs, tag)
        h1 = matmul([(o, W[MIXER_OUT[kind]][slot])], "nn", F32, tag + "_mixout", res=h)
        hn2 = rmsnorm_fwd(h1, W["g_ffn"][i], tag + "_ffnnorm")
        a, b, c = matmul_swiglu(hn2, W["w_ffn_gate"][i], W["w_ffn_up"][i], tag + "_gateup")
        h2 = matmul([(c, W["w_ffn_down"][i])], "nn", F32, tag + "_down", res=h1)
        hn3 = rmsnorm_fwd(h2, W["g_ple"][i], tag + "_plenorm")
        p_i = p[i].astype(BF)
        pp = matmul([(p_i, W["w_ple_proj"][i])], "nn", BF, tag + "_pleproj")
        z, h3 = matmul([(hn3, W["w_ple_gate"][i])], "nn", BF, tag + "_plegate", ple=(h2, pp))
        S.update(hn=hn, h1=h1, hn2=hn2, a=a, b=b, c=c, h2=h2, hn3=hn3, z=z, pp=pp, p=p_i)
        saved.append(S)
        h = h3

    loss, dh = loss_and_grad(h, loss_target, "loss")
    G = {n: [None] * W[n].shape[0] for n in SMALL + ("b_w_q_up",)}
    for i in reversed(range(DEPTH)):
        kind, slot = i % 3, i // 3
        tag = f"l{i}"
        S = saved[i]
        dz, dpp = rowwise(_ple_bwd_fn, [dh, S["z"], S["pp"]], [BF, BF], tag + "_dple")
        _weight_grad(G, "w_ple_proj", i, S["p"], dpp, W, tag + "_dwpleproj")
        _weight_grad(G, "w_ple_gate", i, S["hn3"], dz, W, tag + "_dwplegate")
        dhn3 = matmul([(dz, W["w_ple_gate"][i])], "nt", F32, tag + "_dplenorm")
        dh2, dh2b, G["g_ple"][i] = rmsnorm_bwd(S["h2"], W["g_ple"][i], dhn3, tag + "_dple_norm", [F32, BF], dres=dh)
        da, db = matmul([(dh2b, W["w_ffn_down"][i])], "nt", BF, tag + "_dswiglu", swiglu=(S["a"], S["b"]))
        _weight_grad(G, "w_ffn_down", i, S["c"], dh2b, W, tag + "_dwdown")
        _weight_grad(G, "w_ffn_gate", i, S["hn2"], da, W, tag + "_dwgate")
        _weight_grad(G, "w_ffn_up", i, S["hn2"], db, W, tag + "_dwup")
        dhn2 = matmul([(da, W["w_ffn_gate"][i]), (db, W["w_ffn_up"][i])], "nt", F32, tag + "_dffnnorm")
        dh1, dh1b, G["g_ffn"][i] = rmsnorm_bwd(S["h1"], W["g_ffn"][i], dhn2, tag + "_dffn_norm", [F32, BF], dres=dh2)
        wo = W[MIXER_OUT[kind]][slot]
        do = matmul([(dh1b, wo)], "nt", BF, tag + "_dmixout")
        _weight_grad(G, MIXER_OUT[kind], slot, S["o"], dh1b, W, tag + "_dwmixout")
        dhn = _mixer_bwd(kind, slot, S["hn"], do, W, S, tabs, tag, G)
        dh, G["g_mix"][i] = rmsnorm_bwd(S["h0"], W["g_mix"][i], dhn, tag + "_dmix_norm", [F32], dres=dh1)
    return loss, dh, G


def _pack_small(vals):
    flat = jnp.concatenate([vals[n].reshape(-1).astype(F32) for n in SMALL])
    rows = -(-flat.shape[0] // SMALL_COLS)
    rows = -(-rows // SUBLANES) * SUBLANES
    return jnp.pad(flat, (0, rows * SMALL_COLS - flat.shape[0])).reshape(rows, SMALL_COLS)


def _unpack_small(packed, like):
    flat = packed.reshape(-1)
    out, off = {}, 0
    for n in SMALL:
        size = like[n].size
        out[n] = flat[off:off + size].reshape(like[n].shape)
        off += size
    return out


def kernel(x, p, positions, g_mix, g_ffn, g_ple, w_ple_gate, w_ple_proj, w_ffn_gate, w_ffn_up, w_ffn_down, a_w_in, a_q_norm, a_k_norm, a_sink, a_w_o, b_w_in, b_q_lat_norm, b_kv_lat_norm, b_w_q_up, b_w_kv_up, b_q_norm, b_k_norm, b_w_o, c_w_in, c_q_norm, c_k_norm, c_w_o, loss_target, m_g_mix, m_g_ffn, m_g_ple, m_w_ple_gate, m_w_ple_proj, m_w_ffn_gate, m_w_ffn_up, m_w_ffn_down, m_a_w_in, m_a_q_norm, m_a_k_norm, m_a_sink, m_a_w_o, m_b_w_in, m_b_q_lat_norm, m_b_kv_lat_norm, m_b_w_q_up, m_b_w_kv_up, m_b_q_norm, m_b_k_norm, m_b_w_o, m_c_w_in, m_c_q_norm, m_c_k_norm, m_c_w_o, v_g_mix, v_g_ffn, v_g_ple, v_w_ple_gate, v_w_ple_proj, v_w_ffn_gate, v_w_ffn_up, v_w_ffn_down, v_a_w_in, v_a_q_norm, v_a_k_norm, v_a_sink, v_a_w_o, v_b_w_in, v_b_q_lat_norm, v_b_kv_lat_norm, v_b_w_q_up, v_b_w_kv_up, v_b_q_norm, v_b_k_norm, v_b_w_o, v_c_w_in, v_c_q_norm, v_c_k_norm, v_c_w_o):
    args = dict(locals())
    w_loc = {n: args[n] for n in WEIGHTS}
    m_loc = {n: args["m_" + n] for n in WEIGHTS}
    v_loc = {n: args["v_" + n] for n in WEIGHTS}

    def halves(a):
        rows = a.shape[0] * a.shape[1]
        return a.reshape(2, rows // 2, a.shape[2])

    gathered = gather_weights([halves(w_loc[n].astype(BF)) for n in BIG])
    W = {n: _full_weight(n, g, w_loc[n].shape) for n, g in zip(BIG, gathered)}
    for n in SMALL:
        W[n] = w_loc[n]

    loss, dx, G = local_step(x[0], p[:, 0], positions[0], loss_target[0], W)
    loss = lax.psum(loss, ("x", "y", "c"))

    core = _core_index()
    parts = []
    for n in BIG:
        s = _grad_slots(n, jnp.stack(G[n])) if isinstance(G[n], list) else G[n]
        part = presum_core_halves(s.reshape(2, 4 * s.shape[2], s.shape[3]), core, "presum_" + n)
        parts.append(part.reshape(s.shape[1:]))
    landed = scatter_partials(parts)
    halves = [sum_and_swap(a, "sum_" + n) for n, a in zip(BIG, landed)]

    small = gather_small(_pack_small({n: jnp.stack(G[n]) for n in SMALL}))
    small_sum = sum_slots(small, F32, "sum_small")
    grads = _unpack_small(small_sum, w_loc)

    delta, new_m, new_v = {}, {}, {}
    for n, (mine, theirs) in zip(BIG, halves):
        shape = w_loc[n].shape
        two_d = (shape[0] * shape[1], shape[2])
        g, d, m, v = adamw_halves(w_loc[n].reshape(two_d), mine, theirs, m_loc[n].reshape(two_d),
                                  v_loc[n].reshape(two_d), core, "adamw_" + n)
        grads[n], delta[n], new_m[n], new_v[n] = g.reshape(shape), d.reshape(shape), m.reshape(shape), v.reshape(shape)
    d, m, v = adamw(_pack_small(w_loc), small_sum, _pack_small(m_loc), _pack_small(v_loc), "adamw_small")
    delta.update(_unpack_small(d, w_loc))
    new_m.update(_unpack_small(m, w_loc))
    new_v.update(_unpack_small(v, w_loc))

    return (loss, dx[None], *[grads[n] for n in WEIGHTS], *[delta[n] for n in WEIGHTS],
            *[new_m[n] for n in WEIGHTS], *[new_v[n] for n in WEIGHTS])
```

```python
import functools
import math

import numpy as np
import jax
import jax.numpy as jnp
from jax import lax
from jax.experimental import pallas as pl
from jax.experimental.pallas import tpu as pltpu

F32 = jnp.float32
BF = jnp.bfloat16

D_MODEL = 2048
DEPTH = 4
HEAD_DIM = 128
ROPE_THETA = 500000.0
PARTIAL_ROT = HEAD_DIM // 4
NORM_EPS = 1e-6
NEG = -1e30
A_HEADS = 16
A_KV_HEADS = 4
A_HALF_WINDOW = 128
B_HEADS = 16
B_Q_RANK = 512
B_KV_RANK = 512
B_NOPE = 128
B_ROPE = 64
B_QK = B_NOPE + B_ROPE
B_PAD = 256
C_PATTERNS = ((128, 1), (512, 4), (2048, 16))
C_HEADS = 16
C_GROUPS = 3
ADAM_LR = 0.001
ADAM_B1 = 0.9
ADAM_B2 = 0.999
ADAM_EPS = 1e-08
ADAM_WD = 0.01
ADAM_STEP = 10

LANES = 128
SUBLANES = 8
VMEM_LIMIT_BYTES = 56 * 1024 * 1024
MATMUL_VMEM_BYTES = 46 * 1024 * 1024
MIN_M_TILE = 512
SINGLE_STEP_MAX_K = 2048
BAND_BLOCK = 256
BAND_HEADS_PER_STEP = 4
DENSE_BLOCK = 1024
DENSE_OTHER_BLOCK = 8192
DENSE_SUB = 256
MESH = pl.DeviceIdType.MESH

BIG = ("w_ple_gate", "w_ple_proj", "w_ffn_gate", "w_ffn_up", "w_ffn_down", "a_w_in", "a_w_o",
       "b_w_in", "b_w_q_up", "b_w_kv_up", "b_w_o", "c_w_in", "c_w_o")
SHARD_AXIS = {"w_ple_gate": 1, "w_ple_proj": 2, "w_ffn_gate": 2, "w_ffn_up": 2, "w_ffn_down": 1,
              "a_w_in": 2, "a_w_o": 1, "b_w_in": 1, "b_w_q_up": 2, "b_w_kv_up": 2, "b_w_o": 1,
              "c_w_in": 2, "c_w_o": 1}
SMALL = ("g_mix", "g_ffn", "g_ple", "a_q_norm", "a_k_norm", "a_sink", "b_q_lat_norm",
         "b_kv_lat_norm", "b_q_norm", "b_k_norm", "c_q_norm", "c_k_norm")
WEIGHTS = ("g_mix", "g_ffn", "g_ple", "w_ple_gate", "w_ple_proj", "w_ffn_gate", "w_ffn_up",
           "w_ffn_down", "a_w_in", "a_q_norm", "a_k_norm", "a_sink", "a_w_o", "b_w_in",
           "b_q_lat_norm", "b_kv_lat_norm", "b_w_q_up", "b_w_kv_up", "b_q_norm", "b_k_norm",
           "b_w_o", "c_w_in", "c_q_norm", "c_k_norm", "c_w_o")
SMALL_COLS = 1024
SLOT_DIRECT = ("w_ple_gate", "w_ple_proj", "w_ffn_gate", "w_ffn_up", "w_ffn_down")


def _params(semantics):
    return pltpu.CompilerParams(dimension_semantics=semantics, vmem_limit_bytes=VMEM_LIMIT_BYTES)


def _tile(dim, cands=(1024, 1408, 512, 256, 128)):
    for c in cands:
        if dim % c == 0:
            return c
    return dim


def _k_tile(K, bytes_per_k, fixed_bytes):
    for t in (4096, 2816, 2048, 1408, 1024, 512, 256, 128):
        if K % t == 0 and 2 * bytes_per_k * t + fixed_bytes <= MATMUL_VMEM_BYTES:
            return t
    return _tile(K, (128,))


def _row_tile(rows, cols, target_elems=1 << 19):
    best = None
    for t in range(16, rows + 1, 16):
        if rows % t == 0 and t * cols <= target_elems:
            best = t
    return best if best is not None else rows


def _sigmoid(x):
    return 1.0 / (1.0 + jnp.exp(-x))


class Slot:
    def __init__(self, name, layers, layer, rows, cols, col0=0, buf=None):
        self.axis, self.layers, self.layer, self.rows, self.cols, self.col0, self.buf = (
            SHARD_AXIS[name], layers, layer, rows, cols, col0, buf)
        self.srows = rows // 4 if self.axis == 1 else rows
        self.scols = cols if self.axis == 1 else cols // 4
        self.half = layers * self.srows // 2

    def tiles(self, ncols):
        tm = _tile(math.gcd(self.srows, self.half))
        tn = _tile(math.gcd(self.scols, math.gcd(self.col0, ncols)))
        return tm, tn

    def spec(self, tm, tn):
        def index(i, j, k):
            row, col = i * tm, self.col0 + j * tn
            chip = row // self.srows if self.axis == 1 else col // self.scols
            flat = self.layer * self.srows + (row % self.srows if self.axis == 1 else row)
            cb = col // tn if self.axis == 1 else (col % self.scols) // tn
            return flat // self.half, chip, (flat % self.half) // tm, cb

        return pl.BlockSpec((None, None, tm, tn), index)

    def shape(self):
        return jax.ShapeDtypeStruct((2, 4, self.half, self.scols), F32)


def matmul(pairs, mode, out_dtype, name, res=None, swiglu=None, ple=None, slot=None):
    a0, b0 = pairs[0]
    if mode == "nn":
        (M, K), N = a0.shape, b0.shape[1]
    elif mode == "nt":
        (M, K), N = a0.shape, b0.shape[0]
    else:
        (K, M), N = a0.shape, b0.shape[1]
    tm, tn = (_tile(M), _tile(N)) if slot is None else slot.tiles(N)
    n_mn = 2 + (0 if res is None else 2) + (0 if swiglu is None else 2) + (0 if ple is None else 4)

    def k_tile(rows):
        return _k_tile(K, sum(rows * a.dtype.itemsize + tn * b.dtype.itemsize for a, b in pairs),
                       4 * rows * tn * (1 + n_mn))

    tk = k_tile(tm)
    if (slot is None and tk < K <= SINGLE_STEP_MAX_K and tm > MIN_M_TILE and M % MIN_M_TILE == 0
            and k_tile(MIN_M_TILE) == K):
        tm, tk = MIN_M_TILE, K
    nk = K // tk
    if mode == "nn":
        a_spec = pl.BlockSpec((tm, tk), lambda i, j, k: (i, k))
        b_spec = pl.BlockSpec((tk, tn), lambda i, j, k: (k, j))
        dims = (((1,), (0,)), ((), ()))
    elif mode == "nt":
        a_spec = pl.BlockSpec((tm, tk), lambda i, j, k: (i, k))
        b_spec = pl.BlockSpec((tn, tk), lambda i, j, k: (j, k))
        dims = (((1,), (1,)), ((), ()))
    else:
        a_spec = pl.BlockSpec((tk, tm), lambda i, j, k: (k, i))
        b_spec = pl.BlockSpec((tk, tn), lambda i, j, k: (k, j))
        dims = (((0,), (0,)), ((), ()))
    mn_spec = pl.BlockSpec((tm, tn), lambda i, j, k: (i, j))
    npairs = len(pairs)
    extras = [] if res is None else [res]
    if swiglu is not None:
        extras = list(swiglu)
    if ple is not None:
        extras = list(ple)
    nex = len(extras)
    nout = 2 if (swiglu is not None or ple is not None) else 1
    carried = slot is not None and slot.buf is not None

    def body(*refs):
        ins = refs[:2 * npairs]
        ex = refs[2 * npairs:2 * npairs + nex]
        first_out = 2 * npairs + nex + (1 if carried else 0)
        outs = refs[first_out:first_out + nout]
        k = pl.program_id(2)

        def product():
            part = None
            for p in range(npairs):
                d = lax.dot_general(ins[2 * p][...].astype(BF), ins[2 * p + 1][...].astype(BF), dims,
                                    preferred_element_type=F32)
                part = d if part is None else part + d
            return part

        def finish(r):
            if swiglu is not None:
                a = ex[0][...].astype(F32)
                b = ex[1][...].astype(F32)
                sg = _sigmoid(a)
                outs[0][...] = (r * b * (sg * (1.0 + a * (1.0 - sg)))).astype(out_dtype)
                outs[1][...] = (r * (a * sg)).astype(out_dtype)
            elif ple is not None:
                outs[0][...] = r.astype(out_dtype)
                outs[1][...] = ex[0][...] + _sigmoid(r) * ex[1][...].astype(F32)
            elif res is not None:
                outs[0][...] = (ex[0][...] + r).astype(out_dtype)
            else:
                outs[0][...] = r.astype(outs[0].dtype)

        if nk == 1:
            finish(product())
        else:
            acc = refs[-1]

            @pl.when(k == 0)
            def _():
                acc[...] = jnp.zeros_like(acc)

            acc[...] += product()

            @pl.when(k == nk - 1)
            def _():
                finish(acc[...])

    in_specs = []
    operands = []
    for a, b in pairs:
        in_specs += [a_spec, b_spec]
        operands += [a, b]
    in_specs += [mn_spec] * nex
    operands += extras
    out_shape = [jax.ShapeDtypeStruct((M, N), out_dtype)] * nout
    out_specs = [mn_spec] * nout
    aliases = {}
    if ple is not None:
        out_shape[1] = jax.ShapeDtypeStruct((M, N), F32)
    if slot is not None:
        out_shape, out_specs = [slot.shape()], [slot.spec(tm, tn)]
        if carried:
            aliases = {len(operands): 0}
            in_specs.append(pl.BlockSpec(memory_space=pl.ANY))
            operands.append(slot.buf)
    outs = pl.pallas_call(
        body, out_shape=out_shape, grid=(M // tm, N // tn, nk), in_specs=in_specs,
        out_specs=out_specs, scratch_shapes=[pltpu.VMEM((tm, tn), F32)] if nk > 1 else [],
        input_output_aliases=aliases, compiler_params=_params(("parallel", "parallel", "arbitrary")), name=name,
    )(*operands)
    return outs if nout > 1 else outs[0]


def matmul_swiglu(x, wg, wu, name):
    (M, K), N = x.shape, wg.shape[1]
    tm, tn = _tile(M), _tile(N)

    def k_tile(rows):
        return _k_tile(K, rows * x.dtype.itemsize + 2 * tn * wg.dtype.itemsize, 4 * rows * tn * (2 + 3))

    tk = k_tile(tm)
    if tk < K <= SINGLE_STEP_MAX_K and tm > MIN_M_TILE and M % MIN_M_TILE == 0 and k_tile(MIN_M_TILE) == K:
        tm, tk = MIN_M_TILE, K
    nk = K // tk

    def body(x_ref, g_ref, u_ref, a_ref, b_ref, c_ref, *accs):
        k = pl.program_id(2)
        xv = x_ref[...].astype(BF)

        def products():
            return (jnp.dot(xv, g_ref[...].astype(BF), preferred_element_type=F32),
                    jnp.dot(xv, u_ref[...].astype(BF), preferred_element_type=F32))

        def finish(a, b):
            a_ref[...] = a.astype(a_ref.dtype)
            b_ref[...] = b.astype(b_ref.dtype)
            c_ref[...] = (a * _sigmoid(a) * b).astype(c_ref.dtype)

        if nk == 1:
            finish(*products())
        else:
            acc_g, acc_u = accs

            @pl.when(k == 0)
            def _():
                acc_g[...] = jnp.zeros_like(acc_g)
                acc_u[...] = jnp.zeros_like(acc_u)

            pg, pu = products()
            acc_g[...] += pg
            acc_u[...] += pu

            @pl.when(k == nk - 1)
            def _():
                finish(acc_g[...], acc_u[...])

    w_spec = pl.BlockSpec((tk, tn), lambda i, j, k: (k, j))
    mn_spec = pl.BlockSpec((tm, tn), lambda i, j, k: (i, j))
    return pl.pallas_call(
        body, out_shape=[jax.ShapeDtypeStruct((M, N), BF)] * 3, grid=(M // tm, N // tn, nk),
        in_specs=[pl.BlockSpec((tm, tk), lambda i, j, k: (i, k)), w_spec, w_spec], out_specs=[mn_spec] * 3,
        scratch_shapes=[pltpu.VMEM((tm, tn), F32)] * 2 if nk > 1 else [],
        compiler_params=_params(("parallel", "parallel", "arbitrary")), name=name,
    )(x, wg, wu)


def rowwise(fn, ins, out_dtypes, name):
    rows, cols = ins[0].shape
    tr = _row_tile(rows, cols)
    nin = len(ins)

    def body(*refs):
        vals = fn(*[r[...] for r in refs[:nin]])
        for o, v in zip(refs[nin:], vals):
            o[...] = v.astype(o.dtype)

    spec = pl.BlockSpec((tr, cols), lambda i: (i, 0))
    outs = pl.pallas_call(
        body, out_shape=[jax.ShapeDtypeStruct((rows, cols), d) for d in out_dtypes],
        grid=(rows // tr,), in_specs=[spec] * nin, out_specs=[spec] * len(out_dtypes),
        compiler_params=_params(("parallel",)), name=name,
    )(*ins)
    return outs


def _ple_bwd_fn(dh, z, pp):
    gate = _sigmoid(z.astype(F32))
    return (dh * pp.astype(F32) * gate * (1.0 - gate), dh * gate)


def _merge_fn(o0, o1, o2, l0, l1, l2):
    m = jnp.maximum(jnp.maximum(l0, l1), l2)
    e0, e1, e2 = jnp.exp(l0 - m), jnp.exp(l1 - m), jnp.exp(l2 - m)
    den = e0 + e1 + e2
    return ((e0 * o0 + e1 * o1 + e2 * o2) / den, m + jnp.log(den))


def rmsnorm_fwd(x, g, name, col_block=0, width=None):
    T = x.shape[0]
    W = x.shape[1] if width is None else width
    tt = _row_tile(T, W)

    def body(x_ref, g_ref, y_ref):
        xf = x_ref[...].astype(F32)
        ms = jnp.mean(xf * xf, axis=-1, keepdims=True)
        y_ref[...] = (xf * lax.rsqrt(ms + NORM_EPS) * g_ref[...]).astype(y_ref.dtype)

    return pl.pallas_call(
        body, out_shape=jax.ShapeDtypeStruct((T, W), BF), grid=(T // tt,),
        in_specs=[pl.BlockSpec((tt, W), lambda i: (i, col_block)), pl.BlockSpec((1, W), lambda i: (0, 0))],
        out_specs=pl.BlockSpec((tt, W), lambda i: (i, 0)),
        compiler_params=_params(("parallel",)), name=name,
    )(x, g.reshape(1, W).astype(F32))


def rmsnorm_bwd(x, g, dy, name, out_dtypes, dres=None, col_block=0, width=None):
    T = x.shape[0]
    W = x.shape[1] if width is None else width
    tt = _row_tile(T, W, 1 << 18)
    nout = len(out_dtypes)
    has_res = dres is not None

    def body(*refs):
        x_ref, g_ref, dy_ref = refs[:3]
        pos = 3
        res_ref = None
        if has_res:
            res_ref = refs[3]
            pos = 4
        dx_refs = refs[pos:pos + nout]
        dg_ref = refs[pos + nout]
        xf = x_ref[...].astype(F32)
        rstd = lax.rsqrt(jnp.mean(xf * xf, axis=-1, keepdims=True) + NORM_EPS)
        xhat = xf * rstd
        dyf = dy_ref[...].astype(F32)
        dn = dyf * g_ref[...]
        dx = rstd * (dn - xhat * jnp.mean(dn * xhat, axis=-1, keepdims=True))
        if has_res:
            dx = dx + res_ref[...]
        for o in dx_refs:
            o[...] = dx.astype(o.dtype)

        @pl.when(pl.program_id(0) == 0)
        def _():
            dg_ref[...] = jnp.zeros_like(dg_ref)

        dg_ref[...] += jnp.broadcast_to(jnp.sum(dyf * xhat, axis=0, keepdims=True), dg_ref.shape)

    row = pl.BlockSpec((tt, W), lambda i: (i, 0))
    in_specs = [pl.BlockSpec((tt, W), lambda i: (i, col_block)), pl.BlockSpec((1, W), lambda i: (0, 0)), row]
    operands = [x, g.reshape(1, W).astype(F32), dy]
    if has_res:
        in_specs.append(row)
        operands.append(dres)
    outs = pl.pallas_call(
        body,
        out_shape=[jax.ShapeDtypeStruct((T, W), d) for d in out_dtypes] + [jax.ShapeDtypeStruct((SUBLANES, W), F32)],
        grid=(T // tt,), in_specs=in_specs,
        out_specs=[row] * nout + [pl.BlockSpec((SUBLANES, W), lambda i: (0, 0))],
        compiler_params=_params(("arbitrary",)), name=name,
    )(*operands)
    return tuple(outs[:nout]) + (outs[nout][0],)


def loss_and_grad(y, target, name):
    T, D = y.shape
    tt = _row_tile(T, D)

    def body(y_ref, t_ref, loss_ref, dy_ref):
        d = y_ref[...] - t_ref[...]
        dy_ref[...] = d * (1.0 / D)

        @pl.when(pl.program_id(0) == 0)
        def _():
            loss_ref[...] = jnp.zeros_like(loss_ref)

        loss_ref[...] += jnp.full(loss_ref.shape, 0.5 / D, F32) * jnp.sum(d * d)

    row = pl.BlockSpec((tt, D), lambda i: (i, 0))
    loss, dy = pl.pallas_call(
        body, out_shape=[jax.ShapeDtypeStruct((SUBLANES, LANES), F32), jax.ShapeDtypeStruct((T, D), F32)],
        grid=(T // tt,), in_specs=[row, row],
        out_specs=[pl.BlockSpec((SUBLANES, LANES), lambda i: (0, 0)), row],
        compiler_params=_params(("arbitrary",)), name=name,
    )(y, target)
    return loss[0, 0], dy


def rope_tables(pos, width, r0, rot_dim):
    half = rot_dim // 2
    inv = ROPE_THETA ** (-jnp.arange(half, dtype=F32) * 2.0 / rot_dim)
    ang = pos.astype(F32)[:, None] * inv
    cos, sin = jnp.cos(ang), jnp.sin(ang)
    T = pos.shape[0]
    ones_l, ones_r = jnp.ones((T, r0), F32), jnp.ones((T, width - r0 - rot_dim), F32)
    c_tab = jnp.concatenate([ones_l, cos, cos, ones_r], axis=1)
    s_tab = jnp.concatenate([0 * ones_l, -sin, sin, 0 * ones_r], axis=1)
    perm = np.zeros((width, width), np.float32)
    for j in range(half):
        perm[r0 + j + half, r0 + j] = 1.0
        perm[r0 + j, r0 + j + half] = 1.0
    return c_tab, s_tab, jnp.asarray(perm, BF)


def _lane_permute(v, perm):
    hi = v.astype(BF)
    lo = (v - hi.astype(F32)).astype(BF)
    return (jnp.dot(hi, perm, preferred_element_type=F32) + jnp.dot(lo, perm, preferred_element_type=F32))


def headnorm_fwd(x, g, tabs, name, heads, col0, width, n_true):
    c_tab, s_tab, perm = tabs
    T = x.shape[0]
    tt = _tile(T, (1024, 512, 256, 128))
    inv_n = 1.0 / n_true

    def body(x_ref, g_ref, c_ref, s_ref, p_ref, y_ref):
        xf = x_ref[...].astype(F32)
        rstd = lax.rsqrt(jnp.sum(xf * xf, axis=-1, keepdims=True) * inv_n + NORM_EPS)
        n = xf * rstd * g_ref[...]
        y_ref[...] = (n * c_ref[...] + _lane_permute(n, p_ref[...]) * s_ref[...]).astype(y_ref.dtype)

    tab = pl.BlockSpec((tt, width), lambda i, h: (i, 0))
    return pl.pallas_call(
        body, out_shape=jax.ShapeDtypeStruct((T, heads * width), BF), grid=(T // tt, heads),
        in_specs=[pl.BlockSpec((tt, width), lambda i, h: (i, col0 + h)),
                  pl.BlockSpec((1, width), lambda i, h: (0, 0)), tab, tab,
                  pl.BlockSpec((width, width), lambda i, h: (0, 0))],
        out_specs=pl.BlockSpec((tt, width), lambda i, h: (i, h)),
        compiler_params=_params(("parallel", "parallel")), name=name,
    )(x, g.reshape(1, width).astype(F32), c_tab, s_tab, perm)


def headnorm_bwd(x, g, tabs, dy, name, heads, col0, width, n_true, head_sum=False, into=None):
    c_tab, s_tab, perm = tabs
    T = x.shape[0]
    tt = _tile(T, (1024, 512, 256, 128))
    inv_n = 1.0 / n_true
    buf, blocks, block0 = into if into is not None else (None, heads, 0)
    carried = buf is not None

    def body(*refs):
        x_ref, g_ref, c_ref, s_ref, p_ref, dy_ref = refs[:6]
        dx_ref, dg_ref = refs[7:9] if carried else refs[6:8]
        i, h = pl.program_id(0), pl.program_id(1)
        xf = x_ref[...].astype(F32)
        rstd = lax.rsqrt(jnp.sum(xf * xf, axis=-1, keepdims=True) * inv_n + NORM_EPS)
        xhat = xf * rstd
        dyf = dy_ref[...].astype(F32)
        dn = dyf * c_ref[...] + _lane_permute(dyf * s_ref[...], p_ref[...])
        dxh = dn * g_ref[...]
        dx = rstd * (dxh - xhat * (jnp.sum(dxh * xhat, axis=-1, keepdims=True) * inv_n))
        dx_ref[...] = dx.astype(dx_ref.dtype)

        @pl.when(jnp.logical_and(i == 0, h == 0))
        def _():
            dg_ref[...] = jnp.zeros_like(dg_ref)

        dg_ref[...] += jnp.broadcast_to(jnp.sum(dn * xhat, axis=0, keepdims=True), dg_ref.shape)
        if head_sum:
            sum_ref = refs[-1]

            @pl.when(h == 0)
            def _():
                sum_ref[...] = jnp.zeros_like(sum_ref)

            sum_ref[...] += dx

    tab = pl.BlockSpec((tt, width), lambda i, h: (i, 0))
    out_shape = [jax.ShapeDtypeStruct((T, blocks * width), BF), jax.ShapeDtypeStruct((SUBLANES, width), F32)]
    out_specs = [pl.BlockSpec((tt, width), lambda i, h: (i, block0 + h)),
                 pl.BlockSpec((SUBLANES, width), lambda i, h: (0, 0))]
    if head_sum:
        out_shape.append(jax.ShapeDtypeStruct((T, width), F32))
        out_specs.append(tab)
    in_specs = [pl.BlockSpec((tt, width), lambda i, h: (i, col0 + h)),
                pl.BlockSpec((1, width), lambda i, h: (0, 0)), tab, tab,
                pl.BlockSpec((width, width), lambda i, h: (0, 0)),
                pl.BlockSpec((tt, width), lambda i, h: (i, h))]
    operands = [x, g.reshape(1, width).astype(F32), c_tab, s_tab, perm, dy]
    if carried:
        in_specs.append(pl.BlockSpec(memory_space=pl.ANY))
        operands.append(buf)
    outs = pl.pallas_call(
        body, out_shape=out_shape, grid=(T // tt, heads), in_specs=in_specs, out_specs=out_specs,
        input_output_aliases={6: 0} if carried else {},
        compiler_params=_params(("arbitrary", "arbitrary")), name=name,
    )(*operands)
    return (outs[0], outs[1][0]) + ((outs[2],) if head_sum else ())


class Attn:
    def __init__(self, T, dil, hq, group, qc, q0, kc, k0, vc, v0, vstride, dqk, scale, half_window, blk, oblk=None):
        self.T, self.dil, self.hq, self.group = T, dil, hq, group
        self.hkv = hq // group
        self.qc, self.q0, self.kc, self.k0, self.vc, self.v0, self.vstride = qc, q0, kc, k0, vc, v0, vstride
        self.dqk, self.scale, self.hw = dqk, scale, half_window
        self.len = T // dil
        self.blk = min(blk, self.len)
        self.nb = self.len // self.blk
        self.band = half_window is not None
        self.oblk = self.blk if self.band or oblk is None else min(oblk, self.len)
        self.steps = 3 if self.band else self.len // self.oblk

    def other(self, i, s):
        if self.band:
            nom = i - 1 + s
            return jnp.minimum(jnp.maximum(nom, 0), self.nb - 1), nom
        return s, s

    def chains(self, a):
        return a.reshape(self.len, self.dil * a.shape[1])

    def row_chunks(self, rows):
        assert not self.band
        sub = min(DENSE_SUB, rows)
        return [slice(c * sub, (c + 1) * sub) for c in range(rows // sub)]

    def unchain(self, a, cols):
        return a.reshape(self.T, cols)

    def mask(self, q_nom, k_nom):
        if not self.band:
            return None
        qpos = q_nom * self.blk + lax.broadcasted_iota(jnp.int32, (self.blk, self.blk), 0)
        kpos = k_nom * self.blk + lax.broadcasted_iota(jnp.int32, (self.blk, self.blk), 1)
        ok = jnp.abs(qpos - kpos) <= self.hw
        for pos in (qpos, kpos):
            ok = jnp.logical_and(ok, jnp.logical_and(pos >= 0, pos < self.len))
        return ok


def _scores(cfg, q, k, q_nom, k_nom):
    s = lax.dot_general(q, k, (((1,), (1,)), ((), ())), preferred_element_type=F32) * cfg.scale
    ok = cfg.mask(q_nom, k_nom)
    return s if ok is None else jnp.where(ok, s, NEG)


def flash_fwd(cfg, q, k, v, name, out_dtype, sink=None):
    blk, dqk = cfg.blk, cfg.dqk
    has_sink = sink is not None

    def body(*refs):
        if has_sink:
            sink_ref, refs = refs[0], refs[1:]
        q_ref, k_ref, v_ref, o_ref, lse_ref, m_sc, l_sc, acc_sc = refs
        i, s = pl.program_id(2), pl.program_id(3)

        @pl.when(s == 0)
        def _():
            if has_sink:
                m_sc[...] = jnp.broadcast_to(sink_ref[0, :1, :], m_sc.shape)
                l_sc[...] = jnp.ones_like(l_sc)
            else:
                m_sc[...] = jnp.full(m_sc.shape, NEG, F32)
                l_sc[...] = jnp.zeros_like(l_sc)
            acc_sc[...] = jnp.zeros_like(acc_sc)

        _, k_nom = cfg.other(i, s)
        k, v = k_ref[...], v_ref[...]
        for rows in cfg.row_chunks(blk):
            sc = _scores(cfg, q_ref[rows, :], k, i, k_nom)
            m_prev = m_sc[rows, :]
            m_new = jnp.maximum(m_prev, jnp.max(sc, axis=-1, keepdims=True))
            p = jnp.exp(sc - m_new[:, :1])
            alpha = jnp.exp(m_prev - m_new)
            l_sc[rows, :] = alpha * l_sc[rows, :] + jnp.sum(p, axis=-1, keepdims=True)
            acc_sc[rows, :] = alpha * acc_sc[rows, :] + jnp.dot(p.astype(BF), v, preferred_element_type=F32)
            m_sc[rows, :] = m_new

        @pl.when(s == cfg.steps - 1)
        def _():
            o_ref[...] = (acc_sc[...] / l_sc[...]).astype(o_ref.dtype)
            lse_ref[...] = m_sc[...] + jnp.log(l_sc[...])

    g = cfg.group
    q_spec = pl.BlockSpec((blk, dqk), lambda r, h, i, s: (i, r * cfg.qc + cfg.q0 + h))
    k_spec = pl.BlockSpec((cfg.oblk, dqk), lambda r, h, i, s: (cfg.other(i, s)[0], r * cfg.kc + cfg.k0 + h // g))
    v_spec = pl.BlockSpec((cfg.oblk, LANES),
                          lambda r, h, i, s: (cfg.other(i, s)[0], r * cfg.vc + cfg.v0 + cfg.vstride * (h // g)))
    o_spec = pl.BlockSpec((blk, LANES), lambda r, h, i, s: (i, r * cfg.hq + h))
    in_specs = [q_spec, k_spec, v_spec]
    operands = [cfg.chains(q), cfg.chains(k), cfg.chains(v)]
    if has_sink:
        in_specs.insert(0, pl.BlockSpec((1, SUBLANES, LANES), lambda r, h, i, s: (h, 0, 0)))
        operands.insert(0, sink)
    cols = cfg.dil * cfg.hq * LANES
    o, lse = pl.pallas_call(
        body, out_shape=[jax.ShapeDtypeStruct((cfg.len, cols), out_dtype), jax.ShapeDtypeStruct((cfg.len, cols), F32)],
        grid=(cfg.dil, cfg.hq, cfg.nb, cfg.steps), in_specs=in_specs, out_specs=[o_spec, o_spec],
        scratch_shapes=[pltpu.VMEM((blk, LANES), F32)] * 3,
        compiler_params=_params(("parallel", "parallel", "parallel", "arbitrary")), name=name,
    )(*operands)
    return cfg.unchain(o, cfg.hq * LANES), cfg.unchain(lse, cfg.hq * LANES)


def flash_dq(cfg, q, k, v, do, o, lse, name, sink=None):
    blk, dqk = cfg.blk, cfg.dqk
    has_sink = sink is not None

    def body(*refs):
        if has_sink:
            sink_ref, refs = refs[0], refs[1:]
        q_ref, k_ref, v_ref, do_ref, o_ref, lse_ref = refs[:6]
        dq_ref = refs[6]
        dq_sc, delta_sc = refs[-2:]
        i, s = pl.program_id(2), pl.program_id(3)

        @pl.when(s == 0)
        def _():
            dq_sc[...] = jnp.zeros_like(dq_sc)
            delta = jnp.sum(do_ref[...].astype(F32) * o_ref[...].astype(F32), axis=-1, keepdims=True)
            delta_sc[...] = jnp.broadcast_to(delta, delta_sc.shape)

        _, k_nom = cfg.other(i, s)
        k, v = k_ref[...], v_ref[...]
        for rows in cfg.row_chunks(blk):
            sc = _scores(cfg, q_ref[rows, :], k, i, k_nom)
            p = jnp.exp(sc - lse_ref[rows, :1])
            dp = lax.dot_general(do_ref[rows, :], v, (((1,), (1,)), ((), ())), preferred_element_type=F32)
            ds = p * (dp - delta_sc[rows, :1]) * cfg.scale
            dq_sc[rows, :] += jnp.dot(ds.astype(BF), k, preferred_element_type=F32)

        @pl.when(s == cfg.steps - 1)
        def _():
            dq_ref[...] = dq_sc[...].astype(dq_ref.dtype)
            if has_sink:
                ps = jnp.exp(sink_ref[0, :1, :] - lse_ref[...])
                part = -jnp.sum(ps * delta_sc[...], axis=0, keepdims=True)
                refs[7][...] = jnp.broadcast_to(part, refs[7].shape)

    g = cfg.group
    q_spec = pl.BlockSpec((blk, dqk), lambda r, h, i, s: (i, r * cfg.qc + cfg.q0 + h))
    k_spec = pl.BlockSpec((cfg.oblk, dqk), lambda r, h, i, s: (cfg.other(i, s)[0], r * cfg.kc + cfg.k0 + h // g))
    v_spec = pl.BlockSpec((cfg.oblk, LANES),
                          lambda r, h, i, s: (cfg.other(i, s)[0], r * cfg.vc + cfg.v0 + cfg.vstride * (h // g)))
    o_spec = pl.BlockSpec((blk, LANES), lambda r, h, i, s: (i, r * cfg.hq + h))
    dq_spec = pl.BlockSpec((blk, dqk), lambda r, h, i, s: (i, r * cfg.hq + h))
    in_specs = [q_spec, k_spec, v_spec, o_spec, o_spec, o_spec]
    operands = [cfg.chains(q), cfg.chains(k), cfg.chains(v), cfg.chains(do), cfg.chains(o), cfg.chains(lse)]
    out_shape = [jax.ShapeDtypeStruct((cfg.len, cfg.dil * cfg.hq * dqk), BF)]
    out_specs = [dq_spec]
    if has_sink:
        in_specs.insert(0, pl.BlockSpec((1, SUBLANES, LANES), lambda r, h, i, s: (h, 0, 0)))
        operands.insert(0, sink)
        out_shape.append(jax.ShapeDtypeStruct((cfg.hq * cfg.nb * SUBLANES, LANES), F32))
        out_specs.append(pl.BlockSpec((SUBLANES, LANES), lambda r, h, i, s: (h * cfg.nb + i, 0)))
    outs = pl.pallas_call(
        body, out_shape=out_shape, grid=(cfg.dil, cfg.hq, cfg.nb, cfg.steps), in_specs=in_specs,
        out_specs=out_specs, scratch_shapes=[pltpu.VMEM((blk, dqk), F32), pltpu.VMEM((blk, LANES), F32)],
        compiler_params=_params(("parallel", "parallel", "parallel", "arbitrary")), name=name,
    )(*operands)
    dq = cfg.unchain(outs[0], cfg.hq * dqk)
    if has_sink:
        return dq, outs[1].reshape(cfg.hq, cfg.nb, SUBLANES, LANES)[:, :, 0, :]
    return dq


def flash_dkv(cfg, q, k, v, do, o, lse, name, out_dtype, add=None):
    blk, dqk, g, nw = cfg.blk, cfg.dqk, cfg.group, cfg.steps
    has_add = add is not None

    def body(*refs):
        k_ref, v_ref, q_ref, do_ref, o_ref, lse_ref = refs[:6]
        pos = 8 if has_add else 6
        dk_ref, dv_ref = refs[pos:pos + 2]
        dk_sc, dv_sc = refs[-2:]
        i, j = pl.program_id(2), pl.program_id(3)

        @pl.when(j == 0)
        def _():
            dk_sc[...] = jnp.zeros_like(dk_sc)
            dv_sc[...] = jnp.zeros_like(dv_sc)

        _, q_nom = cfg.other(i, j % nw)
        q, do = q_ref[...], do_ref[...]
        lse = lse_ref[:, :1]
        delta = jnp.sum(do.astype(F32) * o_ref[...].astype(F32), axis=-1, keepdims=True)
        for rows in cfg.row_chunks(blk):
            sc = _scores(cfg, q, k_ref[rows, :], q_nom, i)
            p = jnp.exp(sc - lse)
            dv_sc[rows, :] += lax.dot_general(p.astype(BF), do, (((0,), (0,)), ((), ())), preferred_element_type=F32)
            dp = lax.dot_general(do, v_ref[rows, :], (((1,), (1,)), ((), ())), preferred_element_type=F32)
            ds = p * (dp - delta) * cfg.scale
            dk_sc[rows, :] += lax.dot_general(ds.astype(BF), q, (((0,), (0,)), ((), ())), preferred_element_type=F32)

        @pl.when(j == g * nw - 1)
        def _():
            dk, dv = dk_sc[...], dv_sc[...]
            if has_add:
                dk, dv = dk + refs[6][...].astype(F32), dv + refs[7][...].astype(F32)
            dk_ref[...] = dk.astype(dk_ref.dtype)
            dv_ref[...] = dv.astype(dv_ref.dtype)

    def qrow(i, j):
        return cfg.other(i, j % nw)[0]

    k_spec = pl.BlockSpec((blk, dqk), lambda r, h, i, j: (i, r * cfg.kc + cfg.k0 + h))
    v_spec = pl.BlockSpec((blk, LANES), lambda r, h, i, j: (i, r * cfg.vc + cfg.v0 + cfg.vstride * h))
    q_spec = pl.BlockSpec((cfg.oblk, dqk), lambda r, h, i, j: (qrow(i, j), r * cfg.qc + cfg.q0 + h * g + j // nw))
    o_spec = pl.BlockSpec((cfg.oblk, LANES), lambda r, h, i, j: (qrow(i, j), r * cfg.hq + h * g + j // nw))
    dk_spec = pl.BlockSpec((blk, dqk), lambda r, h, i, j: (i, r * cfg.hkv + h))
    dv_spec = pl.BlockSpec((blk, LANES), lambda r, h, i, j: (i, r * cfg.hkv + h))
    in_specs = [k_spec, v_spec, q_spec, o_spec, o_spec, o_spec]
    operands = [cfg.chains(k), cfg.chains(v), cfg.chains(q), cfg.chains(do), cfg.chains(o), cfg.chains(lse)]
    if has_add:
        in_specs += [dk_spec, dv_spec]
        operands += [cfg.chains(add[0]), cfg.chains(add[1])]
    dk, dv = pl.pallas_call(
        body,
        out_shape=[jax.ShapeDtypeStruct((cfg.len, cfg.dil * cfg.hkv * dqk), out_dtype),
                   jax.ShapeDtypeStruct((cfg.len, cfg.dil * cfg.hkv * LANES), out_dtype)],
        grid=(cfg.dil, cfg.hkv, cfg.nb, g * nw), in_specs=in_specs, out_specs=[dk_spec, dv_spec],
        scratch_shapes=[pltpu.VMEM((blk, dqk), F32), pltpu.VMEM((blk, LANES), F32)],
        compiler_params=_params(("parallel", "parallel", "parallel", "arbitrary")), name=name,
    )(*operands)
    return cfg.unchain(dk, cfg.hkv * dqk), cfg.unchain(dv, cfg.hkv * LANES)


class Band:
    def __init__(self, T, dil, hq, group, per, qc, q0, kc, k0, vc, v0, scale, hw, blk):
        self.T, self.dil, self.hq, self.group, self.per = T, dil, hq, group, per
        self.pk = per // group
        self.hkv = hq // group
        self.scale, self.hw = scale, hw
        self.len = T // dil
        self.blk = min(blk, self.len)
        self.nb = self.len // self.blk
        self.win = self.blk + 2 * hw
        self.qcol = lambda r: (r * qc + q0) // per
        self.kcol = lambda r: (r * kc + k0) // self.pk
        self.vcol = lambda r: (r * vc + v0) // self.pk
        self.ocol = lambda r: (r * hq) // per
        self.dkcol = lambda r: (r * self.hkv) // self.pk
        assert hw <= self.blk and qc % per == 0 and q0 % per == 0 and kc % self.pk == 0 and k0 % self.pk == 0
        assert vc % self.pk == 0 and v0 % self.pk == 0

    def chains(self, a):
        return a.reshape(self.len, self.dil * a.shape[1])

    def rows3(self, width, col):
        nb = self.nb
        return [pl.BlockSpec((self.blk, width), lambda r, h, i: (jnp.maximum(i - 1, 0), col(r) + h)),
                pl.BlockSpec((self.blk, width), lambda r, h, i: (i, col(r) + h)),
                pl.BlockSpec((self.blk, width), lambda r, h, i: (jnp.minimum(i + 1, nb - 1), col(r) + h))]

    def window(self, prev, cur, nxt, j):
        cols = slice(j * LANES, (j + 1) * LANES)
        return jnp.concatenate([prev[self.blk - self.hw:, cols], cur[:, cols], nxt[:self.hw, cols]], axis=0)

    def valid(self, i, window_is_rows):
        shape = (self.win, self.blk) if window_is_rows else (self.blk, self.win)
        wdim = 0 if window_is_rows else 1
        bpos = i * self.blk + lax.broadcasted_iota(jnp.int32, shape, 1 - wdim)
        wpos = i * self.blk - self.hw + lax.broadcasted_iota(jnp.int32, shape, wdim)
        ok = jnp.abs(bpos - wpos) <= self.hw
        return jnp.logical_and(ok, jnp.logical_and(wpos >= 0, wpos < self.len))


def band_fwd(cfg, q, k, v, name, out_dtype, sink=None):
    blk, per, pk = cfg.blk, cfg.per, cfg.pk
    has_sink = sink is not None

    def body(*refs):
        if has_sink:
            sink_ref, refs = refs[0], refs[1:]
        q_ref, kp, kc, kn, vp, vc, vn, o_ref, lse_ref = refs
        ok = cfg.valid(pl.program_id(2), False)
        for j in range(per):
            jk = j // cfg.group
            if j % cfg.group == 0:
                kw = cfg.window(kp, kc, kn, jk)
                vw = cfg.window(vp, vc, vn, jk)
            cols = slice(j * LANES, (j + 1) * LANES)
            s = lax.dot_general(q_ref[:, cols], kw, (((1,), (1,)), ((), ())), preferred_element_type=F32) * cfg.scale
            s = jnp.where(ok, s, NEG)
            m = jnp.max(s, axis=-1, keepdims=True)
            if has_sink:
                sk = sink_ref[j, :1, :1]
                m = jnp.maximum(m, sk)
            e = jnp.exp(s - m)
            den = jnp.sum(e, axis=-1, keepdims=True)
            if has_sink:
                den = den + jnp.exp(sk - m)
            o = jnp.dot(e.astype(BF), vw, preferred_element_type=F32) / den
            o_ref[:, cols] = o.astype(o_ref.dtype)
            lse_ref[:, cols] = jnp.broadcast_to(m + jnp.log(den), (blk, LANES))

    q_spec = pl.BlockSpec((blk, per * LANES), lambda r, h, i: (i, cfg.qcol(r) + h))
    o_spec = pl.BlockSpec((blk, per * LANES), lambda r, h, i: (i, cfg.ocol(r) + h))
    in_specs = [q_spec] + cfg.rows3(pk * LANES, cfg.kcol) + cfg.rows3(pk * LANES, cfg.vcol)
    kc_, vc_ = cfg.chains(k), cfg.chains(v)
    operands = [cfg.chains(q), kc_, kc_, kc_, vc_, vc_, vc_]
    if has_sink:
        in_specs.insert(0, pl.BlockSpec((per, SUBLANES, LANES), lambda r, h, i: (h, 0, 0)))
        operands.insert(0, sink)
    cols = cfg.dil * cfg.hq * LANES
    o, lse = pl.pallas_call(
        body, out_shape=[jax.ShapeDtypeStruct((cfg.len, cols), out_dtype), jax.ShapeDtypeStruct((cfg.len, cols), F32)],
        grid=(cfg.dil, cfg.hq // per, cfg.nb), in_specs=in_specs, out_specs=[o_spec, o_spec],
        compiler_params=_params(("parallel", "parallel", "parallel")), name=name,
    )(*operands)
    return o.reshape(cfg.T, cfg.hq * LANES), lse.reshape(cfg.T, cfg.hq * LANES)


def band_dq(cfg, q, k, v, do, o, lse, name, sink=None):
    blk, per, pk = cfg.blk, cfg.per, cfg.pk
    has_sink = sink is not None

    def body(*refs):
        if has_sink:
            sink_ref, refs = refs[0], refs[1:]
        q_ref, kp, kc, kn, vp, vc, vn, do_ref, o_ref, lse_ref, dq_ref = refs[:11]
        ok = cfg.valid(pl.program_id(2), False)
        for j in range(per):
            jk = j // cfg.group
            if j % cfg.group == 0:
                kw = cfg.window(kp, kc, kn, jk)
                vw = cfg.window(vp, vc, vn, jk)
            cols = slice(j * LANES, (j + 1) * LANES)
            do = do_ref[:, cols]
            lse = lse_ref[:, j * LANES:j * LANES + 1]
            delta = jnp.sum(do.astype(F32) * o_ref[:, cols].astype(F32), axis=-1, keepdims=True)
            s = lax.dot_general(q_ref[:, cols], kw, (((1,), (1,)), ((), ())), preferred_element_type=F32) * cfg.scale
            p = jnp.exp(jnp.where(ok, s, NEG) - lse)
            dp = lax.dot_general(do, vw, (((1,), (1,)), ((), ())), preferred_element_type=F32)
            ds = p * (dp - delta) * cfg.scale
            dq_ref[:, cols] = jnp.dot(ds.astype(BF), kw, preferred_element_type=F32).astype(dq_ref.dtype)
            if has_sink:
                part = -jnp.sum(jnp.exp(sink_ref[j, :1, :1] - lse) * delta, axis=0, keepdims=True)
                refs[11][j * SUBLANES:(j + 1) * SUBLANES, :] = jnp.broadcast_to(part, (SUBLANES, LANES))

    q_spec = pl.BlockSpec((blk, per * LANES), lambda r, h, i: (i, cfg.qcol(r) + h))
    o_spec = pl.BlockSpec((blk, per * LANES), lambda r, h, i: (i, cfg.ocol(r) + h))
    in_specs = [q_spec] + cfg.rows3(pk * LANES, cfg.kcol) + cfg.rows3(pk * LANES, cfg.vcol) + [o_spec] * 3
    kc_, vc_ = cfg.chains(k), cfg.chains(v)
    operands = [cfg.chains(q), kc_, kc_, kc_, vc_, vc_, vc_, cfg.chains(do), cfg.chains(o), cfg.chains(lse)]
    out_shape = [jax.ShapeDtypeStruct((cfg.len, cfg.dil * cfg.hq * LANES), BF)]
    out_specs = [o_spec]
    if has_sink:
        in_specs.insert(0, pl.BlockSpec((per, SUBLANES, LANES), lambda r, h, i: (h, 0, 0)))
        operands.insert(0, sink)
        out_shape.append(jax.ShapeDtypeStruct((cfg.hq // per, cfg.nb, per * SUBLANES, LANES), F32))
        out_specs.append(pl.BlockSpec((None, None, per * SUBLANES, LANES), lambda r, h, i: (h, i, 0, 0)))
    outs = pl.pallas_call(
        body, out_shape=out_shape, grid=(cfg.dil, cfg.hq // per, cfg.nb), in_specs=in_specs, out_specs=out_specs,
        compiler_params=_params(("parallel", "parallel", "parallel")), name=name,
    )(*operands)
    dq = outs[0].reshape(cfg.T, cfg.hq * LANES)
    return (dq, outs[1]) if has_sink else dq


def band_dkv(cfg, q, k, v, do, o, lse, name, out_dtype, add=None, dv_into=None):
    blk, per, pk, group = cfg.blk, cfg.per, cfg.pk, cfg.group
    has_add = add is not None
    carried = dv_into is not None
    assert not carried or cfg.dil == 1

    def body(*refs):
        k_ref, v_ref = refs[:2]
        qs, dos, os_, lses = refs[2:5], refs[5:8], refs[8:11], refs[11:14]
        pos = 14 + (2 if has_add else 0) + (1 if carried else 0)
        dk_ref, dv_ref = refs[pos:pos + 2]
        ok = cfg.valid(pl.program_id(2), True)
        for jk in range(pk):
            kcols = slice(jk * LANES, (jk + 1) * LANES)
            kt, vt = k_ref[:, kcols], v_ref[:, kcols]
            dk = jnp.zeros((blk, LANES), F32)
            dv = jnp.zeros((blk, LANES), F32)
            for g in range(group):
                j = jk * group + g
                qw = cfg.window(*qs, j)
                dow = cfg.window(*dos, j)
                lse = cfg.window(*lses, j)[:, :1]
                delta = jnp.sum(dow.astype(F32) * cfg.window(*os_, j).astype(F32), axis=-1, keepdims=True)
                s = lax.dot_general(qw, kt, (((1,), (1,)), ((), ())), preferred_element_type=F32) * cfg.scale
                p = jnp.exp(jnp.where(ok, s, NEG) - lse)
                dv = dv + lax.dot_general(p.astype(BF), dow, (((0,), (0,)), ((), ())), preferred_element_type=F32)
                dp = lax.dot_general(dow, vt, (((1,), (1,)), ((), ())), preferred_element_type=F32)
                ds = p * (dp - delta) * cfg.scale
                dk = dk + lax.dot_general(ds.astype(BF), qw, (((0,), (0,)), ((), ())), preferred_element_type=F32)
            if has_add:
                dk, dv = dk + refs[14][:, kcols].astype(F32), dv + refs[15][:, kcols].astype(F32)
            dk_ref[:, kcols] = dk.astype(dk_ref.dtype)
            dv_ref[:, kcols] = dv.astype(dv_ref.dtype)

    k_spec = pl.BlockSpec((blk, pk * LANES), lambda r, h, i: (i, cfg.kcol(r) + h))
    v_spec = pl.BlockSpec((blk, pk * LANES), lambda r, h, i: (i, cfg.vcol(r) + h))
    d_spec = pl.BlockSpec((blk, pk * LANES), lambda r, h, i: (i, cfg.dkcol(r) + h))
    in_specs = [k_spec, v_spec] + cfg.rows3(per * LANES, cfg.qcol) + cfg.rows3(per * LANES, cfg.ocol) * 3
    qc_, doc, oc, lc = cfg.chains(q), cfg.chains(do), cfg.chains(o), cfg.chains(lse)
    operands = [cfg.chains(k), cfg.chains(v), qc_, qc_, qc_, doc, doc, doc, oc, oc, oc, lc, lc, lc]
    if has_add:
        in_specs += [d_spec, d_spec]
        operands += [cfg.chains(add[0]), cfg.chains(add[1])]
    cols = cfg.dil * cfg.hkv * LANES
    out_shape = [jax.ShapeDtypeStruct((cfg.len, cols), out_dtype)] * 2
    out_specs = [d_spec, d_spec]
    aliases = {}
    if carried:
        buf, blocks, block0 = dv_into
        out_shape[1] = jax.ShapeDtypeStruct((cfg.T, blocks * LANES), BF)
        out_specs[1] = pl.BlockSpec((blk, pk * LANES), lambda r, h, i: (i, block0 // pk + h))
        aliases = {len(operands): 1}
        in_specs.append(pl.BlockSpec(memory_space=pl.ANY))
        operands.append(buf)
    dk, dv = pl.pallas_call(
        body, out_shape=out_shape, grid=(cfg.dil, cfg.hq // per, cfg.nb), in_specs=in_specs, out_specs=out_specs,
        input_output_aliases=aliases, compiler_params=_params(("parallel", "parallel", "parallel")), name=name,
    )(*operands)
    return dk.reshape(cfg.T, cfg.hkv * LANES), (dv if carried else dv.reshape(cfg.T, cfg.hkv * LANES))


HBM_SPEC = pl.BlockSpec(memory_space=pltpu.HBM)


def _place():
    x, y, c = lax.axis_index("x"), lax.axis_index("y"), lax.axis_index("c")
    chips = [(1 - x, y), (x, 1 - y), (1 - x, 1 - y)]
    return x, y, c, chips


def gather_weights(shards):
    n = len(shards)

    def body(*refs):
        ins, outs = refs[:n], refs[n:2 * n]
        send_sems, recv_sems, local_sems = refs[2 * n:]
        x, y, c, chips = _place()
        me = 2 * x + y
        sibling = (x, y, 1 - c)

        def copy(w, k, src, chip_of_block, half, to):
            return pltpu.make_async_remote_copy(
                src_ref=src, dst_ref=outs[w].at[chip_of_block, half], send_sem=send_sems.at[6 * w + k],
                recv_sem=recv_sems.at[6 * w + k], device_id=to, device_id_type=MESH)

        started = []
        local = []
        for w in range(n):
            own = pltpu.make_async_copy(ins[w], outs[w].at[me], local_sems.at[w])
            own.start()
            local.append(own)
            for j, chip in enumerate(chips):
                cp = copy(w, j, ins[w].at[c], me, c, (*chip, c))
                cp.start()
                started.append(cp)
        for w in range(n):
            for j, (cx, cy) in enumerate(chips):
                them = 2 * cx + cy
                copy(w, j, ins[w].at[c], them, c, (cx, cy, c)).wait_recv()
                fwd = copy(w, 3 + j, outs[w].at[them, c], them, c, sibling)
                fwd.start()
                started.append(fwd)
        for w in range(n):
            for j, (cx, cy) in enumerate(chips):
                copy(w, 3 + j, ins[w].at[c], 2 * cx + cy, 1 - c, sibling).wait_recv()
        for cp in started:
            cp.wait_send()
        for own in local:
            own.wait()

    return pl.pallas_call(
        body, out_shape=[jax.ShapeDtypeStruct((4,) + s.shape, s.dtype) for s in shards],
        in_specs=[HBM_SPEC] * n, out_specs=[HBM_SPEC] * n,
        scratch_shapes=[pltpu.SemaphoreType.DMA((6 * n,)), pltpu.SemaphoreType.DMA((6 * n,)),
                        pltpu.SemaphoreType.DMA((n,))],
        name="gather_weights",
    )(*shards)


def _core_index():
    return lax.axis_index("c").astype(jnp.int32).reshape(1)


def presum_core_halves(g2, core, name):
    _, rows, cols = g2.shape
    tr = _row_tile(rows, cols, 1 << 20)
    nb = rows // tr
    g2 = g2.reshape(2 * rows, cols)

    def body(core_ref, mine_ref, other_ref, out_ref, land, send_sems, recv_sems):
        x, y, c, _ = _place()
        slot = pl.program_id(0) % 2
        cp = pltpu.make_async_remote_copy(
            src_ref=other_ref, dst_ref=land.at[slot], send_sem=send_sems.at[slot], recv_sem=recv_sems.at[slot],
            device_id=(x, y, 1 - c), device_id_type=MESH)
        cp.start()
        cp.wait_recv()
        out_ref[...] = (mine_ref[...] + land[slot]).astype(out_ref.dtype)
        cp.wait_send()

    grid_spec = pltpu.PrefetchScalarGridSpec(
        num_scalar_prefetch=1, grid=(nb,),
        in_specs=[pl.BlockSpec((tr, cols), lambda i, core: (core[0] * nb + i, 0)),
                  pl.BlockSpec((tr, cols), lambda i, core: ((1 - core[0]) * nb + i, 0))],
        out_specs=pl.BlockSpec((tr, cols), lambda i, core: (i, 0)),
        scratch_shapes=[pltpu.VMEM((2, tr, cols), F32), pltpu.SemaphoreType.DMA((2,)), pltpu.SemaphoreType.DMA((2,))])
    return pl.pallas_call(
        body, out_shape=jax.ShapeDtypeStruct((rows, cols), BF), grid_spec=grid_spec,
        compiler_params=_params(("arbitrary",)), name=name,
    )(core, g2, g2)


def sum_and_swap(landed, name):
    n, rows, cols = landed.shape
    tr = _row_tile(rows, cols)

    def body(*refs):
        slots = refs[:n]
        mine_ref, theirs_ref, out_buf, land, send_sems, recv_sems = refs[n:]
        x, y, c, _ = _place()
        slot = pl.program_id(0) % 2
        tot = slots[0][...].astype(F32)
        for r in slots[1:]:
            tot = tot + r[...].astype(F32)
        mine_ref[...] = tot
        out_buf[slot] = tot
        cp = pltpu.make_async_remote_copy(
            src_ref=out_buf.at[slot], dst_ref=land.at[slot], send_sem=send_sems.at[slot], recv_sem=recv_sems.at[slot],
            device_id=(x, y, 1 - c), device_id_type=MESH)
        cp.start()
        cp.wait_recv()
        theirs_ref[...] = land[slot]
        cp.wait_send()

    specs = [pl.BlockSpec((None, tr, cols), functools.partial(lambda s, i: (s, i, 0), s)) for s in range(n)]
    row = pl.BlockSpec((tr, cols), lambda i: (i, 0))
    return pl.pallas_call(
        body, out_shape=[jax.ShapeDtypeStruct((rows, cols), F32)] * 2, grid=(rows // tr,), in_specs=specs,
        out_specs=[row, row],
        scratch_shapes=[pltpu.VMEM((2, tr, cols), F32), pltpu.VMEM((2, tr, cols), F32),
                        pltpu.SemaphoreType.DMA((2,)), pltpu.SemaphoreType.DMA((2,))],
        compiler_params=_params(("arbitrary",)), name=name,
    )(*([landed] * n))


def scatter_partials(parts):
    n = len(parts)

    def body(*refs):
        ins, outs = refs[:n], refs[n:2 * n]
        send_sems, recv_sems, local_sems = refs[2 * n:]
        x, y, c, chips = _place()
        me = 2 * x + y
        started = []
        for w in range(n):
            own = pltpu.make_async_copy(ins[w].at[me], outs[w].at[me], local_sems.at[w])
            own.start()
            started.append(own)
        sends = []
        for w in range(n):
            for j, (cx, cy) in enumerate(chips):
                cp = pltpu.make_async_remote_copy(
                    src_ref=ins[w].at[2 * cx + cy], dst_ref=outs[w].at[me], send_sem=send_sems.at[3 * w + j],
                    recv_sem=recv_sems.at[3 * w + j], device_id=(cx, cy, c), device_id_type=MESH)
                cp.start()
                sends.append(cp)
        for w in range(n):
            for j, (cx, cy) in enumerate(chips):
                pltpu.make_async_remote_copy(
                    src_ref=ins[w].at[me], dst_ref=outs[w].at[2 * cx + cy], send_sem=send_sems.at[3 * w + j],
                    recv_sem=recv_sems.at[3 * w + j], device_id=(cx, cy, c), device_id_type=MESH).wait_recv()
        for cp in sends:
            cp.wait_send()
        for own in started:
            own.wait()

    return pl.pallas_call(
        body, out_shape=[jax.ShapeDtypeStruct(p.shape, p.dtype) for p in parts],
        in_specs=[HBM_SPEC] * n, out_specs=[HBM_SPEC] * n,
        scratch_shapes=[pltpu.SemaphoreType.DMA((3 * n,)), pltpu.SemaphoreType.DMA((3 * n,)),
                        pltpu.SemaphoreType.DMA((n,))],
        name="scatter_partials",
    )(*parts)


def adamw_halves(w, mine, theirs, m, v, core, name):
    rows, cols = w.shape
    tr = _row_tile(rows // 2, cols, 1 << 18)
    nh = rows // 2 // tr

    def body(core_ref, w_ref, a_ref, b_ref, m_ref, v_ref, g_out, d_out, m_out, v_out):
        g = jnp.where(pl.program_id(0) // nh == core_ref[0], a_ref[...], b_ref[...])
        d_out[...], m_out[...], v_out[...] = _adam_fn(w_ref[...], g, m_ref[...], v_ref[...])
        g_out[...] = g

    full = pl.BlockSpec((tr, cols), lambda i, core: (i, 0))
    half = pl.BlockSpec((tr, cols), lambda i, core: (i % nh, 0))
    grid_spec = pltpu.PrefetchScalarGridSpec(
        num_scalar_prefetch=1, grid=(rows // tr,), in_specs=[full, half, half, full, full], out_specs=[full] * 4)
    return pl.pallas_call(
        body, out_shape=[jax.ShapeDtypeStruct((rows, cols), F32)] * 4, grid_spec=grid_spec,
        compiler_params=_params(("parallel",)), name=name,
    )(core, w, mine, theirs, m, v)


def gather_small(vec):
    rows = vec.shape[0]

    def body(v_ref, out_ref, send_sems, recv_sems):
        x, y, c, _ = _place()
        me = 4 * x + 2 * y + c
        out_ref[me] = v_ref[...]
        flips = [(dx, dy, dc) for dx in (0, 1) for dy in (0, 1) for dc in (0, 1)][1:]

        def peer(f):
            return tuple(1 - a if d else a for a, d in zip((x, y, c), f))

        def copy(k, block, to):
            return pltpu.make_async_remote_copy(
                src_ref=v_ref, dst_ref=out_ref.at[block], send_sem=send_sems.at[k], recv_sem=recv_sems.at[k],
                device_id=to, device_id_type=MESH)

        sends = [copy(k, me, peer(f)) for k, f in enumerate(flips)]
        for cp in sends:
            cp.start()
        for k, f in enumerate(flips):
            px, py, pc = peer(f)
            copy(k, 4 * px + 2 * py + pc, peer(f)).wait_recv()
        for cp in sends:
            cp.wait_send()

    vm = pl.BlockSpec(memory_space=pltpu.VMEM)
    return pl.pallas_call(
        body, out_shape=jax.ShapeDtypeStruct((8, rows, SMALL_COLS), F32), in_specs=[vm], out_specs=vm,
        scratch_shapes=[pltpu.SemaphoreType.DMA((7,)), pltpu.SemaphoreType.DMA((7,))], name="gather_small",
    )(vec)


def sum_slots(a, out_dtype, name):
    n, rows, cols = a.shape
    tr = _row_tile(rows, cols)

    def body(*refs):
        tot = refs[0][...].astype(F32)
        for r in refs[1:n]:
            tot = tot + r[...].astype(F32)
        refs[n][...] = tot.astype(out_dtype)

    specs = [pl.BlockSpec((None, tr, cols), functools.partial(lambda s, i: (s, i, 0), s)) for s in range(n)]
    return pl.pallas_call(
        body, out_shape=jax.ShapeDtypeStruct((rows, cols), out_dtype), grid=(rows // tr,), in_specs=specs,
        out_specs=pl.BlockSpec((tr, cols), lambda i: (i, 0)), compiler_params=_params(("parallel",)), name=name,
    )(*([a] * n))


def _adam_fn(w, g, m, v):
    m = ADAM_B1 * m + (1.0 - ADAM_B1) * g
    v = ADAM_B2 * v + (1.0 - ADAM_B2) * (g * g)
    m_hat = m / (1.0 - ADAM_B1 ** ADAM_STEP)
    v_hat = v / (1.0 - ADAM_B2 ** ADAM_STEP)
    delta = -ADAM_LR * (m_hat / (jnp.sqrt(v_hat) + ADAM_EPS) + ADAM_WD * w)
    return delta, m, v


def adamw(w, g, m, v, name):
    return rowwise(_adam_fn, [w, g, m, v], [F32, F32, F32], name)


def _full_weight(name, gathered, local_shape):
    L, a, b = local_shape
    g = gathered.reshape((4, L, a, b))
    if SHARD_AXIS[name] == 1:
        return g.transpose(1, 0, 2, 3).reshape(L, 4 * a, b)
    return g.transpose(1, 2, 0, 3).reshape(L, a, 4 * b)


def _grad_slots(name, dw):
    L, a, b = dw.shape
    if SHARD_AXIS[name] == 1:
        s = dw.reshape(L, 4, a // 4, b).transpose(1, 0, 2, 3)
        rows, cols = L * (a // 4), b
    else:
        s = dw.reshape(L, a, 4, b // 4).transpose(2, 0, 1, 3)
        rows, cols = L * a, b // 4
    return s.reshape(4, 2, rows // 2, cols).transpose(1, 0, 2, 3)


def _attn_a(T):
    group = A_HEADS // A_KV_HEADS
    return Band(T, 1, A_HEADS, group, group, A_HEADS, 0, A_KV_HEADS, 0, A_HEADS + 2 * A_KV_HEADS,
                A_HEADS + A_KV_HEADS, 1.0 / math.sqrt(HEAD_DIM), A_HALF_WINDOW, BAND_BLOCK)


def _attn_b(T):
    return Attn(T, 1, B_HEADS, 1, B_HEADS, 0, B_HEADS, 0, 2 * B_HEADS, 1, 2, B_PAD, 1.0 / math.sqrt(B_QK), None,
                DENSE_BLOCK, DENSE_OTHER_BLOCK)


def _attn_c(T, group):
    window, dil = C_PATTERNS[group]
    return Band(T, dil, C_HEADS, 1, BAND_HEADS_PER_STEP, C_HEADS, 0, C_HEADS, 0, C_HEADS, 0,
                1.0 / math.sqrt(HEAD_DIM), window // 2 // dil, BAND_BLOCK)


def _pad_heads(a, axis_len_true, axis_len_pad):
    lead = a.shape[:-1]
    h = a.shape[-1] // axis_len_true
    a = a.reshape(lead + (h, axis_len_true))
    a = jnp.pad(a, [(0, 0)] * len(lead) + [(0, 0), (0, axis_len_pad - axis_len_true)])
    return a.reshape(lead + (h * axis_len_pad,))


def _unpad_heads(a, axis_len_true, axis_len_pad):
    lead = a.shape[:-1]
    h = a.shape[-1] // axis_len_pad
    return a.reshape(lead + (h, axis_len_pad))[..., :axis_len_true].reshape(lead + (h * axis_len_true,))


def _weight_grad(G, name, layer, a, dy, W, tag):
    layers, rows, cols = W[name].shape
    if name in SLOT_DIRECT:
        G[name] = matmul([(a, dy)], "tn", F32, tag, slot=Slot(name, layers, layer, rows, cols, 0, G.get(name)))
    else:
        G.setdefault(name, [None] * layers)[layer] = matmul([(a, dy)], "tn", F32, tag)


def _mixer_fwd(kind, slot, hn, W, S, tabs, tag):
    T = hn.shape[0]
    if kind == 0:
        cfg = _attn_a(T)
        qkv = matmul([(hn, W["a_w_in"][slot])], "nn", BF, tag + "_a_in")
        q = headnorm_fwd(qkv, W["a_q_norm"][slot], tabs["hd"], tag + "_a_qn", A_HEADS, 0, HEAD_DIM, HEAD_DIM)
        k = headnorm_fwd(qkv, W["a_k_norm"][slot], tabs["hd"], tag + "_a_kn", A_KV_HEADS, A_HEADS, HEAD_DIM, HEAD_DIM)
        sink = jnp.broadcast_to(W["a_sink"][slot][:, None, None], (A_HEADS, SUBLANES, LANES)).astype(F32)
        o, lse = band_fwd(cfg, q, k, qkv, tag + "_a_att", BF, sink=sink)
        S.update(qkv=qkv, q=q, k=k, o=o, lse=lse, sink=sink)
        return o
    if kind == 1:
        cfg = _attn_b(T)
        lat = matmul([(hn, W["b_w_in"][slot])], "nn", BF, tag + "_b_in")
        qn = rmsnorm_fwd(lat, W["b_q_lat_norm"][slot], tag + "_b_qlat", 0, B_Q_RANK)
        kvn = rmsnorm_fwd(lat, W["b_kv_lat_norm"][slot], tag + "_b_kvlat", 1, B_KV_RANK)
        qp = matmul([(qn, W["b_w_q_up_pad"][slot])], "nn", BF, tag + "_b_qup")
        kv = matmul([(kvn, W["b_w_kv_up"][slot])], "nn", BF, tag + "_b_kvup")
        k_rope = lat[:, B_Q_RANK + B_KV_RANK:]
        kpre = jnp.concatenate(
            [kv.reshape(T, B_HEADS, 2 * B_NOPE)[:, :, :B_NOPE],
             jnp.broadcast_to(k_rope[:, None, :], (T, B_HEADS, B_ROPE)),
             jnp.zeros((T, B_HEADS, B_PAD - B_QK), BF)], axis=-1).reshape(T, B_HEADS * B_PAD)
        q = headnorm_fwd(qp, W["b_q_norm_pad"][slot], tabs["b"], tag + "_b_qn", B_HEADS, 0, B_PAD, B_QK)
        k = headnorm_fwd(kpre, W["b_k_norm_pad"][slot], tabs["b"], tag + "_b_kn", B_HEADS, 0, B_PAD, B_QK)
        o, lse = flash_fwd(cfg, q, k, kv, tag + "_b_att", BF)
        S.update(lat=lat, qn=qn, kvn=kvn, qp=qp, kv=kv, kpre=kpre, q=q, k=k, o=o, lse=lse)
        return o
    qkv = matmul([(hn, W["c_w_in"][slot])], "nn", BF, tag + "_c_in")
    nq = C_GROUPS * C_HEADS
    qs = [headnorm_fwd(qkv, W["c_q_norm"][slot], tabs["hd"], f"{tag}_c_qn{g}", C_HEADS, g * C_HEADS, HEAD_DIM, HEAD_DIM)
          for g in range(C_GROUPS)]
    k = headnorm_fwd(qkv, W["c_k_norm"][slot], tabs["hd"], tag + "_c_kn", C_HEADS, nq, HEAD_DIM, HEAD_DIM)
    outs, lses = [], []
    v = qkv[:, (C_GROUPS + 1) * C_HEADS * HEAD_DIM:]
    for g in range(C_GROUPS):
        og, lg = band_fwd(_attn_c(T, g), qs[g], k, v, f"{tag}_c_att{g}", F32)
        outs.append(og)
        lses.append(lg)
    o, lse = rowwise(_merge_fn, outs + lses, [BF, F32], tag + "_c_merge")
    S.update(qkv=qkv, qs=qs, v=v, k=k, o=o, lse=lse)
    return o


def _mixer_bwd(kind, slot, hn, do, W, S, tabs, tag, G):
    T = hn.shape[0]
    if kind == 0:
        cfg = _attn_a(T)
        qkv = S["qkv"]
        dq, dsink = band_dq(cfg, S["q"], S["k"], qkv, do, S["o"], S["lse"], tag + "_a_dq", sink=S["sink"])
        blocks = A_HEADS + 2 * A_KV_HEADS
        dqkv, dgq = headnorm_bwd(qkv, W["a_q_norm"][slot], tabs["hd"], dq, tag + "_a_dqn", A_HEADS, 0, HEAD_DIM, HEAD_DIM,
                                 into=(None, blocks, 0))
        dk, dqkv = band_dkv(cfg, S["q"], S["k"], qkv, do, S["o"], S["lse"], tag + "_a_dkv", BF,
                            dv_into=(dqkv, blocks, A_HEADS + A_KV_HEADS))
        dqkv, dgk = headnorm_bwd(qkv, W["a_k_norm"][slot], tabs["hd"], dk, tag + "_a_dkn", A_KV_HEADS, A_HEADS,
                                 HEAD_DIM, HEAD_DIM, into=(dqkv, blocks, A_HEADS))
        _weight_grad(G, "a_w_in", slot, hn, dqkv, W, tag + "_a_dwin")
        G["a_q_norm"][slot], G["a_k_norm"][slot] = dgq, dgk
        parts = dsink.reshape(A_HEADS // cfg.per, cfg.nb, cfg.per, SUBLANES, LANES)[:, :, :, 0, 0]
        G["a_sink"][slot] = jnp.sum(parts, axis=1).reshape(A_HEADS)
        return matmul([(dqkv, W["a_w_in"][slot])], "nt", F32, tag + "_a_dhn")
    if kind == 1:
        cfg = _attn_b(T)
        kv = S["kv"]
        dq = flash_dq(cfg, S["q"], S["k"], kv, do, S["o"], S["lse"], tag + "_b_dq")
        dk, dv = flash_dkv(cfg, S["q"], S["k"], kv, do, S["o"], S["lse"], tag + "_b_dkv", BF)
        dqp, dgq = headnorm_bwd(S["qp"], W["b_q_norm_pad"][slot], tabs["b"], dq, tag + "_b_dqn", B_HEADS, 0, B_PAD, B_QK)
        dkp, dgk, dksum = headnorm_bwd(S["kpre"], W["b_k_norm_pad"][slot], tabs["b"], dk, tag + "_b_dkn", B_HEADS, 0,
                                       B_PAD, B_QK, head_sum=True)
        dkv = jnp.concatenate([dkp.reshape(T, B_HEADS, B_PAD)[:, :, :B_NOPE], dv.reshape(T, B_HEADS, LANES)],
                              axis=-1).reshape(T, B_HEADS * 2 * B_NOPE)
        _weight_grad(G, "b_w_kv_up", slot, S["kvn"], dkv, W, tag + "_b_dwkv")
        G["b_w_q_up"][slot] = _unpad_heads(matmul([(S["qn"], dqp)], "tn", F32, tag + "_b_dwq"), B_QK, B_PAD)
        dqn = matmul([(dqp, W["b_w_q_up_pad"][slot])], "nt", F32, tag + "_b_dqnorm")
        dkvn = matmul([(dkv, W["b_w_kv_up"][slot])], "nt", F32, tag + "_b_dkvnorm")
        dql, dg_q = rmsnorm_bwd(S["lat"], W["b_q_lat_norm"][slot], dqn, tag + "_b_dqlat", [BF], None, 0, B_Q_RANK)
        dkvl, dg_kv = rmsnorm_bwd(S["lat"], W["b_kv_lat_norm"][slot], dkvn, tag + "_b_dkvlat", [BF], None, 1, B_KV_RANK)
        dlat = jnp.concatenate([dql, dkvl, dksum[:, B_NOPE:B_QK].astype(BF)], axis=1)
        _weight_grad(G, "b_w_in", slot, hn, dlat, W, tag + "_b_dwin")
        G["b_q_norm"][slot], G["b_k_norm"][slot] = dgq[:B_QK], dgk[:B_QK]
        G["b_q_lat_norm"][slot], G["b_kv_lat_norm"][slot] = dg_q, dg_kv
        return matmul([(dlat, W["b_w_in"][slot])], "nt", F32, tag + "_b_dhn")
    qkv = S["qkv"]
    nq = C_GROUPS * C_HEADS
    blocks = (C_GROUPS + 2) * C_HEADS
    dqkv, dgq = None, 0.0
    for g in range(C_GROUPS):
        dq = band_dq(_attn_c(T, g), S["qs"][g], S["k"], S["v"], do, S["o"], S["lse"], f"{tag}_c_dq{g}")
        dqkv, dg = headnorm_bwd(qkv, W["c_q_norm"][slot], tabs["hd"], dq, f"{tag}_c_dqn{g}", C_HEADS, g * C_HEADS,
                                HEAD_DIM, HEAD_DIM, into=(dqkv, blocks, g * C_HEADS))
        dgq = dgq + dg
    acc = None
    for g in reversed(range(C_GROUPS)):
        into = (dqkv, blocks, (C_GROUPS + 1) * C_HEADS) if g == 0 else None
        acc = band_dkv(_attn_c(T, g), S["qs"][g], S["k"], S["v"], do, S["o"], S["lse"], f"{tag}_c_dkv{g}", F32,
                       add=acc, dv_into=into)
    dk, dqkv = acc
    dqkv, dgk = headnorm_bwd(qkv, W["c_k_norm"][slot], tabs["hd"], dk, tag + "_c_dkn", C_HEADS, nq, HEAD_DIM, HEAD_DIM,
                             into=(dqkv, blocks, nq))
    _weight_grad(G, "c_w_in", slot, hn, dqkv, W, tag + "_c_dwin")
    G["c_q_norm"][slot], G["c_k_norm"][slot] = dgq, dgk
    return matmul([(dqkv, W["c_w_in"][slot])], "nt", F32, tag + "_c_dhn")


MIXER_OUT = ("a_w_o", "b_w_o", "c_w_o")


def local_step(x, p, positions, loss_target, W):
    T = x.shape[0]
    tabs = {"hd": rope_tables(positions, HEAD_DIM, 0, PARTIAL_ROT), "b": rope_tables(positions, B_PAD, B_NOPE, B_ROPE)}
    W = dict(W)
    W["b_w_q_up_pad"] = _pad_heads(W["b_w_q_up"], B_QK, B_PAD)
    W["b_q_norm_pad"] = _pad_heads(W["b_q_norm"], B_QK, B_PAD)
    W["b_k_norm_pad"] = _pad_heads(W["b_k_norm"], B_QK, B_PAD)
    saved = []
    h = x
    for i in range(DEPTH):
        kind, slot = i % 3, i // 3
        tag = f"l{i}"
        S = {"h0": h}
        hn = rmsnorm_fwd(h, W["g_mix"][i], tag + "_mixnorm")
        o = _mixer_fwd(kind, slot, hn, W, S, tabs, tag)
        h1 = matmul([(o, W[MIXER_OUT[kind]][slot])], "nn", F32, tag + "_mixout", res=h)
        hn2 = rmsnorm_fwd(h1, W["g_ffn"][i], tag + "_ffnnorm")
        a, b, c = matmul_swiglu(hn2, W["w_ffn_gate"][i], W["w_ffn_up"][i], tag + "_gateup")
        h2 = matmul([(c, W["w_ffn_down"][i])], "nn", F32, tag + "_down", res=h1)
        hn3 = rmsnorm_fwd(h2, W["g_ple"][i], tag + "_plenorm")
        p_i = p[i].astype(BF)
        pp = matmul([(p_i, W["w_ple_proj"][i])], "nn", BF, tag + "_pleproj")
        z, h3 = matmul([(hn3, W["w_ple_gate"][i])], "nn", BF, tag + "_plegate", ple=(h2, pp))
        S.update(hn=hn, h1=h1, hn2=hn2, a=a, b=b, c=c, h2=h2, hn3=hn3, z=z, pp=pp, p=p_i)
        saved.append(S)
        h = h3

    loss, dh = loss_and_grad(h, loss_target, "loss")
    G = {n: [None] * W[n].shape[0] for n in SMALL + ("b_w_q_up",)}
    for i in reversed(range(DEPTH)):
        kind, slot = i % 3, i // 3
        tag = f"l{i}"
        S = saved[i]
        dz, dpp = rowwise(_ple_bwd_fn, [dh, S["z"], S["pp"]], [BF, BF], tag + "_dple")
        _weight_grad(G, "w_ple_proj", i, S["p"], dpp, W, tag + "_dwpleproj")
        _weight_grad(G, "w_ple_gate", i, S["hn3"], dz, W, tag + "_dwplegate")
        dhn3 = matmul([(dz, W["w_ple_gate"][i])], "nt", F32, tag + "_dplenorm")
        dh2, dh2b, G["g_ple"][i] = rmsnorm_bwd(S["h2"], W["g_ple"][i], dhn3, tag + "_dple_norm", [F32, BF], dres=dh)
        da, db = matmul([(dh2b, W["w_ffn_down"][i])], "nt", BF, tag + "_dswiglu", swiglu=(S["a"], S["b"]))
        _weight_grad(G, "w_ffn_down", i, S["c"], dh2b, W, tag + "_dwdown")
        _weight_grad(G, "w_ffn_gate", i, S["hn2"], da, W, tag + "_dwgate")
        _weight_grad(G, "w_ffn_up", i, S["hn2"], db, W, tag + "_dwup")
        dhn2 = matmul([(da, W["w_ffn_gate"][i]), (db, W["w_ffn_up"][i])], "nt", F32, tag + "_dffnnorm")
        dh1, dh1b, G["g_ffn"][i] = rmsnorm_bwd(S["h1"], W["g_ffn"][i], dhn2, tag + "_dffn_norm", [F32, BF], dres=dh2)
        wo = W[MIXER_OUT[kind]][slot]
        do = matmul([(dh1b, wo)], "nt", BF, tag + "_dmixout")
        _weight_grad(G, MIXER_OUT[kind], slot, S["o"], dh1b, W, tag + "_dwmixout")
        dhn = _mixer_bwd(kind, slot, S["hn"], do, W, S, tabs, tag, G)
        dh, G["g_mix"][i] = rmsnorm_bwd(S["h0"], W["g_mix"][i], dhn, tag + "_dmix_norm", [F32], dres=dh1)
    return loss, dh, G


def _pack_small(vals):
    flat = jnp.concatenate([vals[n].reshape(-1).astype(F32) for n in SMALL])
    rows = -(-flat.shape[0] // SMALL_COLS)
    rows = -(-rows // SUBLANES) * SUBLANES
    return jnp.pad(flat, (0, rows * SMALL_COLS - flat.shape[0])).reshape(rows, SMALL_COLS)


def _unpack_small(packed, like):
    flat = packed.reshape(-1)
    out, off = {}, 0
    for n in SMALL:
        size = like[n].size
        out[n] = flat[off:off + size].reshape(like[n].shape)
        off += size
    return out


def kernel(x, p, positions, g_mix, g_ffn, g_ple, w_ple_gate, w_ple_proj, w_ffn_gate, w_ffn_up, w_ffn_down, a_w_in, a_q_norm, a_k_norm, a_sink, a_w_o, b_w_in, b_q_lat_norm, b_kv_lat_norm, b_w_q_up, b_w_kv_up, b_q_norm, b_k_norm, b_w_o, c_w_in, c_q_norm, c_k_norm, c_w_o, loss_target, m_g_mix, m_g_ffn, m_g_ple, m_w_ple_gate, m_w_ple_proj, m_w_ffn_gate, m_w_ffn_up, m_w_ffn_down, m_a_w_in, m_a_q_norm, m_a_k_norm, m_a_sink, m_a_w_o, m_b_w_in, m_b_q_lat_norm, m_b_kv_lat_norm, m_b_w_q_up, m_b_w_kv_up, m_b_q_norm, m_b_k_norm, m_b_w_o, m_c_w_in, m_c_q_norm, m_c_k_norm, m_c_w_o, v_g_mix, v_g_ffn, v_g_ple, v_w_ple_gate, v_w_ple_proj, v_w_ffn_gate, v_w_ffn_up, v_w_ffn_down, v_a_w_in, v_a_q_norm, v_a_k_norm, v_a_sink, v_a_w_o, v_b_w_in, v_b_q_lat_norm, v_b_kv_lat_norm, v_b_w_q_up, v_b_w_kv_up, v_b_q_norm, v_b_k_norm, v_b_w_o, v_c_w_in, v_c_q_norm, v_c_k_norm, v_c_w_o):
    args = dict(locals())
    w_loc = {n: args[n] for n in WEIGHTS}
    m_loc = {n: args["m_" + n] for n in WEIGHTS}
    v_loc = {n: args["v_" + n] for n in WEIGHTS}

    def halves(a):
        rows = a.shape[0] * a.shape[1]
        return a.reshape(2, rows // 2, a.shape[2])

    gathered = gather_weights([halves(w_loc[n].astype(BF)) for n in BIG])
    W = {n: _full_weight(n, g, w_loc[n].shape) for n, g in zip(BIG, gathered)}
    for n in SMALL:
        W[n] = w_loc[n]

    loss, dx, G = local_step(x[0], p[:, 0], positions[0], loss_target[0], W)
    loss = lax.psum(loss, ("x", "y", "c"))

    core = _core_index()
    parts = []
    for n in BIG:
        s = _grad_slots(n, jnp.stack(G[n])) if isinstance(G[n], list) else G[n]
        part = presum_core_halves(s.reshape(2, 4 * s.shape[2], s.shape[3]), core, "presum_" + n)
        parts.append(part.reshape(s.shape[1:]))
    landed = scatter_partials(parts)
    halves = [sum_and_swap(a, "sum_" + n) for n, a in zip(BIG, landed)]

    small = gather_small(_pack_small({n: jnp.stack(G[n]) for n in SMALL}))
    small_sum = sum_slots(small, F32, "sum_small")
    grads = _unpack_small(small_sum, w_loc)

    delta, new_m, new_v = {}, {}, {}
    for n, (mine, theirs) in zip(BIG, halves):
        shape = w_loc[n].shape
        two_d = (shape[0] * shape[1], shape[2])
        g, d, m, v = adamw_halves(w_loc[n].reshape(two_d), mine, theirs, m_loc[n].reshape(two_d),
                                  v_loc[n].reshape(two_d), core, "adamw_" + n)
        grads[n], delta[n], new_m[n], new_v[n] = g.reshape(shape), d.reshape(shape), m.reshape(shape), v.reshape(shape)
    d, m, v = adamw(_pack_small(w_loc), small_sum, _pack_small(m_loc), _pack_small(v_loc), "adamw_small")
    delta.update(_unpack_small(d, w_loc))
    new_m.update(_unpack_small(m, w_loc))
    new_v.update(_unpack_small(v, w_loc))

    return (loss, dx[None], *[grads[n] for n in WEIGHTS], *[delta[n] for n in WEIGHTS],
            *[new_m[n] for n in WEIGHTS], *[new_v[n] for n in WEIGHTS])
```

```python
import functools
import math

import numpy as np
import jax
import jax.numpy as jnp
from jax import lax
from jax.experimental import pallas as pl
from jax.experimental.pallas import tpu as pltpu

F32 = jnp.float32
BF = jnp.bfloat16

D_MODEL = 2048
DEPTH = 4
HEAD_DIM = 128
ROPE_THETA = 500000.0
PARTIAL_ROT = HEAD_DIM // 4
NORM_EPS = 1e-6
NEG = -1e30
A_HEADS = 16
A_KV_HEADS = 4
A_HALF_WINDOW = 128
B_HEADS = 16
B_Q_RANK = 512
B_KV_RANK = 512
B_NOPE = 128
B_ROPE = 64
B_QK = B_NOPE + B_ROPE
B_PAD = 256
C_PATTERNS = ((128, 1), (512, 4), (2048, 16))
C_HEADS = 16
C_GROUPS = 3
ADAM_LR = 0.001
ADAM_B1 = 0.9
ADAM_B2 = 0.999
ADAM_EPS = 1e-08
ADAM_WD = 0.01
ADAM_STEP = 10

LANES = 128
SUBLANES = 8
VMEM_LIMIT_BYTES = 56 * 1024 * 1024
MATMUL_VMEM_BYTES = 46 * 1024 * 1024
MIN_M_TILE = 512
SINGLE_STEP_MAX_K = 2048
BAND_BLOCK = 256
BAND_HEADS_PER_STEP = 4
DENSE_BLOCK = 1024
DENSE_OTHER_BLOCK = 8192
DENSE_SUB = 256
MESH = pl.DeviceIdType.MESH

BIG = ("w_ple_gate", "w_ple_proj", "w_ffn_gate", "w_ffn_up", "w_ffn_down", "a_w_in", "a_w_o",
       "b_w_in", "b_w_q_up", "b_w_kv_up", "b_w_o", "c_w_in", "c_w_o")
SHARD_AXIS = {"w_ple_gate": 1, "w_ple_proj": 2, "w_ffn_gate": 2, "w_ffn_up": 2, "w_ffn_down": 1,
              "a_w_in": 2, "a_w_o": 1, "b_w_in": 1, "b_w_q_up": 2, "b_w_kv_up": 2, "b_w_o": 1,
              "c_w_in": 2, "c_w_o": 1}
SMALL = ("g_mix", "g_ffn", "g_ple", "a_q_norm", "a_k_norm", "a_sink", "b_q_lat_norm",
         "b_kv_lat_norm", "b_q_norm", "b_k_norm", "c_q_norm", "c_k_norm")
WEIGHTS = ("g_mix", "g_ffn", "g_ple", "w_ple_gate", "w_ple_proj", "w_ffn_gate", "w_ffn_up",
           "w_ffn_down", "a_w_in", "a_q_norm", "a_k_norm", "a_sink", "a_w_o", "b_w_in",
           "b_q_lat_norm", "b_kv_lat_norm", "b_w_q_up", "b_w_kv_up", "b_q_norm", "b_k_norm",
           "b_w_o", "c_w_in", "c_q_norm", "c_k_norm", "c_w_o")
SMALL_COLS = 1024
SLOT_DIRECT = ("w_ple_gate", "w_ple_proj", "w_ffn_gate", "w_ffn_up", "w_ffn_down")


def _params(semantics):
    return pltpu.CompilerParams(dimension_semantics=semantics, vmem_limit_bytes=VMEM_LIMIT_BYTES)


def _tile(dim, cands=(1024, 1408, 512, 256, 128)):
    for c in cands:
        if dim % c == 0:
            return c
    return dim


def _k_tile(K, bytes_per_k, fixed_bytes):
    for t in (4096, 2816, 2048, 1408, 1024, 512, 256, 128):
        if K % t == 0 and 2 * bytes_per_k * t + fixed_bytes <= MATMUL_VMEM_BYTES:
            return t
    return _tile(K, (128,))


def _row_tile(rows, cols, target_elems=1 << 19):
    best = None
    for t in range(16, rows + 1, 16):
        if rows % t == 0 and t * cols <= target_elems:
            best = t
    return best if best is not None else rows


def _sigmoid(x):
    return 1.0 / (1.0 + jnp.exp(-x))


class Slot:
    def __init__(self, name, layers, layer, rows, cols, col0=0, buf=None):
        self.axis, self.layers, self.layer, self.rows, self.cols, self.col0, self.buf = (
            SHARD_AXIS[name], layers, layer, rows, cols, col0, buf)
        self.srows = rows // 4 if self.axis == 1 else rows
        self.scols = cols if self.axis == 1 else cols // 4
        self.half = layers * self.srows // 2

    def tiles(self, ncols):
        tm = _tile(math.gcd(self.srows, self.half))
        tn = _tile(math.gcd(self.scols, math.gcd(self.col0, ncols)))
        return tm, tn

    def spec(self, tm, tn):
        def index(i, j, k):
            row, col = i * tm, self.col0 + j * tn
            chip = row // self.srows if self.axis == 1 else col // self.scols
            flat = self.layer * self.srows + (row % self.srows if self.axis == 1 else row)
            cb = col // tn if self.axis == 1 else (col % self.scols) // tn
            return flat // self.half, chip, (flat % self.half) // tm, cb

        return pl.BlockSpec((None, None, tm, tn), index)

    def shape(self):
        return jax.ShapeDtypeStruct((2, 4, self.half, self.scols), F32)


def matmul(pairs, mode, out_dtype, name, res=None, swiglu=None, ple=None, slot=None):
    a0, b0 = pairs[0]
    if mode == "nn":
        (M, K), N = a0.shape, b0.shape[1]
    elif mode == "nt":
        (M, K), N = a0.shape, b0.shape[0]
    else:
        (K, M), N = a0.shape, b0.shape[1]
    tm, tn = (_tile(M), _tile(N)) if slot is None else slot.tiles(N)
    n_mn = 2 + (0 if res is None else 2) + (0 if swiglu is None else 2) + (0 if ple is None else 4)

    def k_tile(rows):
        return _k_tile(K, sum(rows * a.dtype.itemsize + tn * b.dtype.itemsize for a, b in pairs),
                       4 * rows * tn * (1 + n_mn))

    tk = k_tile(tm)
    if (slot is None and tk < K <= SINGLE_STEP_MAX_K and tm > MIN_M_TILE and M % MIN_M_TILE == 0
            and k_tile(MIN_M_TILE) == K):
        tm, tk = MIN_M_TILE, K
    nk = K // tk
    if mode == "nn":
        a_spec = pl.BlockSpec((tm, tk), lambda i, j, k: (i, k))
        b_spec = pl.BlockSpec((tk, tn), lambda i, j, k: (k, j))
        dims = (((1,), (0,)), ((), ()))
    elif mode == "nt":
        a_spec = pl.BlockSpec((tm, tk), lambda i, j, k: (i, k))
        b_spec = pl.BlockSpec((tn, tk), lambda i, j, k: (j, k))
        dims = (((1,), (1,)), ((), ()))
    else:
        a_spec = pl.BlockSpec((tk, tm), lambda i, j, k: (k, i))
        b_spec = pl.BlockSpec((tk, tn), lambda i, j, k: (k, j))
        dims = (((0,), (0,)), ((), ()))
    mn_spec = pl.BlockSpec((tm, tn), lambda i, j, k: (i, j))
    npairs = len(pairs)
    extras = [] if res is None else [res]
    if swiglu is not None:
        extras = list(swiglu)
    if ple is not None:
        extras = list(ple)
    nex = len(extras)
    nout = 2 if (swiglu is not None or ple is not None) else 1
    carried = slot is not None and slot.buf is not None

    def body(*refs):
        ins = refs[:2 * npairs]
        ex = refs[2 * npairs:2 * npairs + nex]
        first_out = 2 * npairs + nex + (1 if carried else 0)
        outs = refs[first_out:first_out + nout]
        k = pl.program_id(2)

        def product():
            part = None
            for p in range(npairs):
                d = lax.dot_general(ins[2 * p][...].astype(BF), ins[2 * p + 1][...].astype(BF), dims,
                                    preferred_element_type=F32)
                part = d if part is None else part + d
            return part

        def finish(r):
            if swiglu is not None:
                a = ex[0][...].astype(F32)
                b = ex[1][...].astype(F32)
                sg = _sigmoid(a)
                outs[0][...] = (r * b * (sg * (1.0 + a * (1.0 - sg)))).astype(out_dtype)
                outs[1][...] = (r * (a * sg)).astype(out_dtype)
            elif ple is not None:
                outs[0][...] = r.astype(out_dtype)
                outs[1][...] = ex[0][...] + _sigmoid(r) * ex[1][...].astype(F32)
            elif res is not None:
                outs[0][...] = (ex[0][...] + r).astype(out_dtype)
            else:
                outs[0][...] = r.astype(outs[0].dtype)

        if nk == 1:
            finish(product())
        else:
            acc = refs[-1]

            @pl.when(k == 0)
            def _():
                acc[...] = jnp.zeros_like(acc)

            acc[...] += product()

            @pl.when(k == nk - 1)
            def _():
                finish(acc[...])

    in_specs = []
    operands = []
    for a, b in pairs:
        in_specs += [a_spec, b_spec]
        operands += [a, b]
    in_specs += [mn_spec] * nex
    operands += extras
    out_shape = [jax.ShapeDtypeStruct((M, N), out_dtype)] * nout
    out_specs = [mn_spec] * nout
    aliases = {}
    if ple is not None:
        out_shape[1] = jax.ShapeDtypeStruct((M, N), F32)
    if slot is not None:
        out_shape, out_specs = [slot.shape()], [slot.spec(tm, tn)]
        if carried:
            aliases = {len(operands): 0}
            in_specs.append(pl.BlockSpec(memory_space=pl.ANY))
            operands.append(slot.buf)
    outs = pl.pallas_call(
        body, out_shape=out_shape, grid=(M // tm, N // tn, nk), in_specs=in_specs,
        out_specs=out_specs, scratch_shapes=[pltpu.VMEM((tm, tn), F32)] if nk > 1 else [],
        input_output_aliases=aliases, compiler_params=_params(("parallel", "parallel", "arbitrary")), name=name,
    )(*operands)
    return outs if nout > 1 else outs[0]


def matmul_swiglu(x, wg, wu, name):
    (M, K), N = x.shape, wg.shape[1]
    tm, tn = _tile(M), _tile(N)

    def k_tile(rows):
        return _k_tile(K, rows * x.dtype.itemsize + 2 * tn * wg.dtype.itemsize, 4 * rows * tn * (2 + 3))

    tk = k_tile(tm)
    if tk < K <= SINGLE_STEP_MAX_K and tm > MIN_M_TILE and M % MIN_M_TILE == 0 and k_tile(MIN_M_TILE) == K:
        tm, tk = MIN_M_TILE, K
    nk = K // tk

    def body(x_ref, g_ref, u_ref, a_ref, b_ref, c_ref, *accs):
        k = pl.program_id(2)
        xv = x_ref[...].astype(BF)

        def products():
            return (jnp.dot(xv, g_ref[...].astype(BF), preferred_element_type=F32),
                    jnp.dot(xv, u_ref[...].astype(BF), preferred_element_type=F32))

        def finish(a, b):
            a_ref[...] = a.astype(a_ref.dtype)
            b_ref[...] = b.astype(b_ref.dtype)
            c_ref[...] = (a * _sigmoid(a) * b).astype(c_ref.dtype)

        if nk == 1:
            finish(*products())
        else:
            acc_g, acc_u = accs

            @pl.when(k == 0)
            def _():
                acc_g[...] = jnp.zeros_like(acc_g)
                acc_u[...] = jnp.zeros_like(acc_u)

            pg, pu = products()
            acc_g[...] += pg
            acc_u[...] += pu

            @pl.when(k == nk - 1)
            def _():
                finish(acc_g[...], acc_u[...])

    w_spec = pl.BlockSpec((tk, tn), lambda i, j, k: (k, j))
    mn_spec = pl.BlockSpec((tm, tn), lambda i, j, k: (i, j))
    return pl.pallas_call(
        body, out_shape=[jax.ShapeDtypeStruct((M, N), BF)] * 3, grid=(M // tm, N // tn, nk),
        in_specs=[pl.BlockSpec((tm, tk), lambda i, j, k: (i, k)), w_spec, w_spec], out_specs=[mn_spec] * 3,
        scratch_shapes=[pltpu.VMEM((tm, tn), F32)] * 2 if nk > 1 else [],
        compiler_params=_params(("parallel", "parallel", "arbitrary")), name=name,
    )(x, wg, wu)


def rowwise(fn, ins, out_dtypes, name):
    rows, cols = ins[0].shape
    tr = _row_tile(rows, cols)
    nin = len(ins)

    def body(*refs):
        vals = fn(*[r[...] for r in refs[:nin]])
        for o, v in zip(refs[nin:], vals):
            o[...] = v.astype(o.dtype)

    spec = pl.BlockSpec((tr, cols), lambda i: (i, 0))
    outs = pl.pallas_call(
        body, out_shape=[jax.ShapeDtypeStruct((rows, cols), d) for d in out_dtypes],
        grid=(rows // tr,), in_specs=[spec] * nin, out_specs=[spec] * len(out_dtypes),
        compiler_params=_params(("parallel",)), name=name,
    )(*ins)
    return outs


def _ple_bwd_fn(dh, z, pp):
    gate = _sigmoid(z.astype(F32))
    return (dh * pp.astype(F32) * gate * (1.0 - gate), dh * gate)


def _merge_fn(o0, o1, o2, l0, l1, l2):
    m = jnp.maximum(jnp.maximum(l0, l1), l2)
    e0, e1, e2 = jnp.exp(l0 - m), jnp.exp(l1 - m), jnp.exp(l2 - m)
    den = e0 + e1 + e2
    return ((e0 * o0 + e1 * o1 + e2 * o2) / den, m + jnp.log(den))


def rmsnorm_fwd(x, g, name, col_block=0, width=None):
    T = x.shape[0]
    W = x.shape[1] if width is None else width
    tt = _row_tile(T, W)

    def body(x_ref, g_ref, y_ref):
        xf = x_ref[...].astype(F32)
        ms = jnp.mean(xf * xf, axis=-1, keepdims=True)
        y_ref[...] = (xf * lax.rsqrt(ms + NORM_EPS) * g_ref[...]).astype(y_ref.dtype)

    return pl.pallas_call(
        body, out_shape=jax.ShapeDtypeStruct((T, W), BF), grid=(T // tt,),
        in_specs=[pl.BlockSpec((tt, W), lambda i: (i, col_block)), pl.BlockSpec((1, W), lambda i: (0, 0))],
        out_specs=pl.BlockSpec((tt, W), lambda i: (i, 0)),
        compiler_params=_params(("parallel",)), name=name,
    )(x, g.reshape(1, W).astype(F32))


def rmsnorm_bwd(x, g, dy, name, out_dtypes, dres=None, col_block=0, width=None):
    T = x.shape[0]
    W = x.shape[1] if width is None else width
    tt = _row_tile(T, W, 1 << 18)
    nout = len(out_dtypes)
    has_res = dres is not None

    def body(*refs):
        x_ref, g_ref, dy_ref = refs[:3]
        pos = 3
        res_ref = None
        if has_res:
            res_ref = refs[3]
            pos = 4
        dx_refs = refs[pos:pos + nout]
        dg_ref = refs[pos + nout]
        xf = x_ref[...].astype(F32)
        rstd = lax.rsqrt(jnp.mean(xf * xf, axis=-1, keepdims=True) + NORM_EPS)
        xhat = xf * rstd
        dyf = dy_ref[...].astype(F32)
        dn = dyf * g_ref[...]
        dx = rstd * (dn - xhat * jnp.mean(dn * xhat, axis=-1, keepdims=True))
        if has_res:
            dx = dx + res_ref[...]
        for o in dx_refs:
            o[...] = dx.astype(o.dtype)

        @pl.when(pl.program_id(0) == 0)
        def _():
            dg_ref[...] = jnp.zeros_like(dg_ref)

        dg_ref[...] += jnp.broadcast_to(jnp.sum(dyf * xhat, axis=0, keepdims=True), dg_ref.shape)

    row = pl.BlockSpec((tt, W), lambda i: (i, 0))
    in_specs = [pl.BlockSpec((tt, W), lambda i: (i, col_block)), pl.BlockSpec((1, W), lambda i: (0, 0)), row]
    operands = [x, g.reshape(1, W).astype(F32), dy]
    if has_res:
        in_specs.append(row)
        operands.append(dres)
    outs = pl.pallas_call(
        body,
        out_shape=[jax.ShapeDtypeStruct((T, W), d) for d in out_dtypes] + [jax.ShapeDtypeStruct((SUBLANES, W), F32)],
        grid=(T // tt,), in_specs=in_specs,
        out_specs=[row] * nout + [pl.BlockSpec((SUBLANES, W), lambda i: (0, 0))],
        compiler_params=_params(("arbitrary",)), name=name,
    )(*operands)
    return tuple(outs[:nout]) + (outs[nout][0],)


def loss_and_grad(y, target, name):
    T, D = y.shape
    tt = _row_tile(T, D)

    def body(y_ref, t_ref, loss_ref, dy_ref):
        d = y_ref[...] - t_ref[...]
        dy_ref[...] = d * (1.0 / D)

        @pl.when(pl.program_id(0) == 0)
        def _():
            loss_ref[...] = jnp.zeros_like(loss_ref)

        loss_ref[...] += jnp.full(loss_ref.shape, 0.5 / D, F32) * jnp.sum(d * d)

    row = pl.BlockSpec((tt, D), lambda i: (i, 0))
    loss, dy = pl.pallas_call(
        body, out_shape=[jax.ShapeDtypeStruct((SUBLANES, LANES), F32), jax.ShapeDtypeStruct((T, D), F32)],
        grid=(T // tt,), in_specs=[row, row],
        out_specs=[pl.BlockSpec((SUBLANES, LANES), lambda i: (0, 0)), row],
        compiler_params=_params(("arbitrary",)), name=name,
    )(y, target)
    return loss[0, 0], dy


def rope_tables(pos, width, r0, rot_dim):
    half = rot_dim // 2
    inv = ROPE_THETA ** (-jnp.arange(half, dtype=F32) * 2.0 / rot_dim)
    ang = pos.astype(F32)[:, None] * inv
    cos, sin = jnp.cos(ang), jnp.sin(ang)
    T = pos.shape[0]
    ones_l, ones_r = jnp.ones((T, r0), F32), jnp.ones((T, width - r0 - rot_dim), F32)
    c_tab = jnp.concatenate([ones_l, cos, cos, ones_r], axis=1)
    s_tab = jnp.concatenate([0 * ones_l, -sin, sin, 0 * ones_r], axis=1)
    perm = np.zeros((width, width), np.float32)
    for j in range(half):
        perm[r0 + j + half, r0 + j] = 1.0
        perm[r0 + j, r0 + j + half] = 1.0
    return c_tab, s_tab, jnp.asarray(perm, BF)


def _lane_permute(v, perm):
    hi = v.astype(BF)
    lo = (v - hi.astype(F32)).astype(BF)
    return (jnp.dot(hi, perm, preferred_element_type=F32) + jnp.dot(lo, perm, preferred_element_type=F32))


def headnorm_fwd(x, g, tabs, name, heads, col0, width, n_true):
    c_tab, s_tab, perm = tabs
    T = x.shape[0]
    tt = _tile(T, (1024, 512, 256, 128))
    inv_n = 1.0 / n_true

    def body(x_ref, g_ref, c_ref, s_ref, p_ref, y_ref):
        xf = x_ref[...].astype(F32)
        rstd = lax.rsqrt(jnp.sum(xf * xf, axis=-1, keepdims=True) * inv_n + NORM_EPS)
        n = xf * rstd * g_ref[...]
        y_ref[...] = (n * c_ref[...] + _lane_permute(n, p_ref[...]) * s_ref[...]).astype(y_ref.dtype)

    tab = pl.BlockSpec((tt, width), lambda i, h: (i, 0))
    return pl.pallas_call(
        body, out_shape=jax.ShapeDtypeStruct((T, heads * width), BF), grid=(T // tt, heads),
        in_specs=[pl.BlockSpec((tt, width), lambda i, h: (i, col0 + h)),
                  pl.BlockSpec((1, width), lambda i, h: (0, 0)), tab, tab,
                  pl.BlockSpec((width, width), lambda i, h: (0, 0))],
        out_specs=pl.BlockSpec((tt, width), lambda i, h: (i, h)),
        compiler_params=_params(("parallel", "parallel")), name=name,
    )(x, g.reshape(1, width).astype(F32), c_tab, s_tab, perm)


def headnorm_bwd(x, g, tabs, dy, name, heads, col0, width, n_true, head_sum=False, into=None):
    c_tab, s_tab, perm = tabs
    T = x.shape[0]
    tt = _tile(T, (1024, 512, 256, 128))
    inv_n = 1.0 / n_true
    buf, blocks, block0 = into if into is not None else (None, heads, 0)
    carried = buf is not None

    def body(*refs):
        x_ref, g_ref, c_ref, s_ref, p_ref, dy_ref = refs[:6]
        dx_ref, dg_ref = refs[7:9] if carried else refs[6:8]
        i, h = pl.program_id(0), pl.program_id(1)
        xf = x_ref[...].astype(F32)
        rstd = lax.rsqrt(jnp.sum(xf * xf, axis=-1, keepdims=True) * inv_n + NORM_EPS)
        xhat = xf * rstd
        dyf = dy_ref[...].astype(F32)
        dn = dyf * c_ref[...] + _lane_permute(dyf * s_ref[...], p_ref[...])
        dxh = dn * g_ref[...]
        dx = rstd * (dxh - xhat * (jnp.sum(dxh * xhat, axis=-1, keepdims=True) * inv_n))
        dx_ref[...] = dx.astype(dx_ref.dtype)

        @pl.when(jnp.logical_and(i == 0, h == 0))
        def _():
            dg_ref[...] = jnp.zeros_like(dg_ref)

        dg_ref[...] += jnp.broadcast_to(jnp.sum(dn * xhat, axis=0, keepdims=True), dg_ref.shape)
        if head_sum:
            sum_ref = refs[-1]

            @pl.when(h == 0)
            def _():
                sum_ref[...] = jnp.zeros_like(sum_ref)

            sum_ref[...] += dx

    tab = pl.BlockSpec((tt, width), lambda i, h: (i, 0))
    out_shape = [jax.ShapeDtypeStruct((T, blocks * width), BF), jax.ShapeDtypeStruct((SUBLANES, width), F32)]
    out_specs = [pl.BlockSpec((tt, width), lambda i, h: (i, block0 + h)),
                 pl.BlockSpec((SUBLANES, width), lambda i, h: (0, 0))]
    if head_sum:
        out_shape.append(jax.ShapeDtypeStruct((T, width), F32))
        out_specs.append(tab)
    in_specs = [pl.BlockSpec((tt, width), lambda i, h: (i, col0 + h)),
                pl.BlockSpec((1, width), lambda i, h: (0, 0)), tab, tab,
                pl.BlockSpec((width, width), lambda i, h: (0, 0)),
                pl.BlockSpec((tt, width), lambda i, h: (i, h))]
    operands = [x, g.reshape(1, width).astype(F32), c_tab, s_tab, perm, dy]
    if carried:
        in_specs.append(pl.BlockSpec(memory_space=pl.ANY))
        operands.append(buf)
    outs = pl.pallas_call(
        body, out_shape=out_shape, grid=(T // tt, heads), in_specs=in_specs, out_specs=out_specs,
        input_output_aliases={6: 0} if carried else {},
        compiler_params=_params(("arbitrary", "arbitrary")), name=name,
    )(*operands)
    return (outs[0], outs[1][0]) + ((outs[2],) if head_sum else ())


class Attn:
    def __init__(self, T, dil, hq, group, qc, q0, kc, k0, vc, v0, vstride, dqk, scale, half_window, blk, oblk=None):
        self.T, self.dil, self.hq, self.group = T, dil, hq, group
        self.hkv = hq // group
        self.qc, self.q0, self.kc, self.k0, self.vc, self.v0, self.vstride = qc, q0, kc, k0, vc, v0, vstride
        self.dqk, self.scale, self.hw = dqk, scale, half_window
        self.len = T // dil
        self.blk = min(blk, self.len)
        self.nb = self.len // self.blk
        self.band = half_window is not None
        self.oblk = self.blk if self.band or oblk is None else min(oblk, self.len)
        self.steps = 3 if self.band else self.len // self.oblk

    def other(self, i, s):
        if self.band:
            nom = i - 1 + s
            return jnp.minimum(jnp.maximum(nom, 0), self.nb - 1), nom
        return s, s

    def chains(self, a):
        return a.reshape(self.len, self.dil * a.shape[1])

    def row_chunks(self, rows):
        assert not self.band
        sub = min(DENSE_SUB, rows)
        return [slice(c * sub, (c + 1) * sub) for c in range(rows // sub)]

    def unchain(self, a, cols):
        return a.reshape(self.T, cols)

    def mask(self, q_nom, k_nom):
        if not self.band:
            return None
        qpos = q_nom * self.blk + lax.broadcasted_iota(jnp.int32, (self.blk, self.blk), 0)
        kpos = k_nom * self.blk + lax.broadcasted_iota(jnp.int32, (self.blk, self.blk), 1)
        ok = jnp.abs(qpos - kpos) <= self.hw
        for pos in (qpos, kpos):
            ok = jnp.logical_and(ok, jnp.logical_and(pos >= 0, pos < self.len))
        return ok


def _scores(cfg, q, k, q_nom, k_nom):
    s = lax.dot_general(q, k, (((1,), (1,)), ((), ())), preferred_element_type=F32) * cfg.scale
    ok = cfg.mask(q_nom, k_nom)
    return s if ok is None else jnp.where(ok, s, NEG)


def flash_fwd(cfg, q, k, v, name, out_dtype, sink=None):
    blk, dqk = cfg.blk, cfg.dqk
    has_sink = sink is not None

    def body(*refs):
        if has_sink:
            sink_ref, refs = refs[0], refs[1:]
        q_ref, k_ref, v_ref, o_ref, lse_ref, m_sc, l_sc, acc_sc = refs
        i, s = pl.program_id(2), pl.program_id(3)

        @pl.when(s == 0)
        def _():
            if has_sink:
                m_sc[...] = jnp.broadcast_to(sink_ref[0, :1, :], m_sc.shape)
                l_sc[...] = jnp.ones_like(l_sc)
            else:
                m_sc[...] = jnp.full(m_sc.shape, NEG, F32)
                l_sc[...] = jnp.zeros_like(l_sc)
            acc_sc[...] = jnp.zeros_like(acc_sc)

        _, k_nom = cfg.other(i, s)
        k, v = k_ref[...], v_ref[...]
        for rows in cfg.row_chunks(blk):
            sc = _scores(cfg, q_ref[rows, :], k, i, k_nom)
            m_prev = m_sc[rows, :]
            m_new = jnp.maximum(m_prev, jnp.max(sc, axis=-1, keepdims=True))
            p = jnp.exp(sc - m_new[:, :1])
            alpha = jnp.exp(m_prev - m_new)
            l_sc[rows, :] = alpha * l_sc[rows, :] + jnp.sum(p, axis=-1, keepdims=True)
            acc_sc[rows, :] = alpha * acc_sc[rows, :] + jnp.dot(p.astype(BF), v, preferred_element_type=F32)
            m_sc[rows, :] = m_new

        @pl.when(s == cfg.steps - 1)
        def _():
            o_ref[...] = (acc_sc[...] / l_sc[...]).astype(o_ref.dtype)
            lse_ref[...] = m_sc[...] + jnp.log(l_sc[...])

    g = cfg.group
    q_spec = pl.BlockSpec((blk, dqk), lambda r, h, i, s: (i, r * cfg.qc + cfg.q0 + h))
    k_spec = pl.BlockSpec((cfg.oblk, dqk), lambda r, h, i, s: (cfg.other(i, s)[0], r * cfg.kc + cfg.k0 + h // g))
    v_spec = pl.BlockSpec((cfg.oblk, LANES),
                          lambda r, h, i, s: (cfg.other(i, s)[0], r * cfg.vc + cfg.v0 + cfg.vstride * (h // g)))
    o_spec = pl.BlockSpec((blk, LANES), lambda r, h, i, s: (i, r * cfg.hq + h))
    in_specs = [q_spec, k_spec, v_spec]
    operands = [cfg.chains(q), cfg.chains(k), cfg.chains(v)]
    if has_sink:
        in_specs.insert(0, pl.BlockSpec((1, SUBLANES, LANES), lambda r, h, i, s: (h, 0, 0)))
        operands.insert(0, sink)
    cols = cfg.dil * cfg.hq * LANES
    o, lse = pl.pallas_call(
        body, out_shape=[jax.ShapeDtypeStruct((cfg.len, cols), out_dtype), jax.ShapeDtypeStruct((cfg.len, cols), F32)],
        grid=(cfg.dil, cfg.hq, cfg.nb, cfg.steps), in_specs=in_specs, out_specs=[o_spec, o_spec],
        scratch_shapes=[pltpu.VMEM((blk, LANES), F32)] * 3,
        compiler_params=_params(("parallel", "parallel", "parallel", "arbitrary")), name=name,
    )(*operands)
    return cfg.unchain(o, cfg.hq * LANES), cfg.unchain(lse, cfg.hq * LANES)


def flash_dq(cfg, q, k, v, do, o, lse, name, sink=None):
    blk, dqk = cfg.blk, cfg.dqk
    has_sink = sink is not None

    def body(*refs):
        if has_sink:
            sink_ref, refs = refs[0], refs[1:]
        q_ref, k_ref, v_ref, do_ref, o_ref, lse_ref = refs[:6]
        dq_ref = refs[6]
        dq_sc, delta_sc = refs[-2:]
        i, s = pl.program_id(2), pl.program_id(3)

        @pl.when(s == 0)
        def _():
            dq_sc[...] = jnp.zeros_like(dq_sc)
            delta = jnp.sum(do_ref[...].astype(F32) * o_ref[...].astype(F32), axis=-1, keepdims=True)
            delta_sc[...] = jnp.broadcast_to(delta, delta_sc.shape)

        _, k_nom = cfg.other(i, s)
        k, v = k_ref[...], v_ref[...]
        for rows in cfg.row_chunks(blk):
            sc = _scores(cfg, q_ref[rows, :], k, i, k_nom)
            p = jnp.exp(sc - lse_ref[rows, :1])
            dp = lax.dot_general(do_ref[rows, :], v, (((1,), (1,)), ((), ())), preferred_element_type=F32)
            ds = p * (dp - delta_sc[rows, :1]) * cfg.scale
            dq_sc[rows, :] += jnp.dot(ds.astype(BF), k, preferred_element_type=F32)

        @pl.when(s == cfg.steps - 1)
        def _():
            dq_ref[...] = dq_sc[...].astype(dq_ref.dtype)
            if has_sink:
                ps = jnp.exp(sink_ref[0, :1, :] - lse_ref[...])
                part = -jnp.sum(ps * delta_sc[...], axis=0, keepdims=True)
                refs[7][...] = jnp.broadcast_to(part, refs[7].shape)

    g = cfg.group
    q_spec = pl.BlockSpec((blk, dqk), lambda r, h, i, s: (i, r * cfg.qc + cfg.q0 + h))
    k_spec = pl.BlockSpec((cfg.oblk, dqk), lambda r, h, i, s: (cfg.other(i, s)[0], r * cfg.kc + cfg.k0 + h // g))
    v_spec = pl.BlockSpec((cfg.oblk, LANES),
                          lambda r, h, i, s: (cfg.other(i, s)[0], r * cfg.vc + cfg.v0 + cfg.vstride * (h // g)))
    o_spec = pl.BlockSpec((blk, LANES), lambda r, h, i, s: (i, r * cfg.hq + h))
    dq_spec = pl.BlockSpec((blk, dqk), lambda r, h, i, s: (i, r * cfg.hq + h))
    in_specs = [q_spec, k_spec, v_spec, o_spec, o_spec, o_spec]
    operands = [cfg.chains(q), cfg.chains(k), cfg.chains(v), cfg.chains(do), cfg.chains(o), cfg.chains(lse)]
    out_shape = [jax.ShapeDtypeStruct((cfg.len, cfg.dil * cfg.hq * dqk), BF)]
    out_specs = [dq_spec]
    if has_sink:
        in_specs.insert(0, pl.BlockSpec((1, SUBLANES, LANES), lambda r, h, i, s: (h, 0, 0)))
        operands.insert(0, sink)
        out_shape.append(jax.ShapeDtypeStruct((cfg.hq * cfg.nb * SUBLANES, LANES), F32))
        out_specs.append(pl.BlockSpec((SUBLANES, LANES), lambda r, h, i, s: (h * cfg.nb + i, 0)))
    outs = pl.pallas_call(
        body, out_shape=out_shape, grid=(cfg.dil, cfg.hq, cfg.nb, cfg.steps), in_specs=in_specs,
        out_specs=out_specs, scratch_shapes=[pltpu.VMEM((blk, dqk), F32), pltpu.VMEM((blk, LANES), F32)],
        compiler_params=_params(("parallel", "parallel", "parallel", "arbitrary")), name=name,
    )(*operands)
    dq = cfg.unchain(outs[0], cfg.hq * dqk)
    if has_sink:
        return dq, outs[1].reshape(cfg.hq, cfg.nb, SUBLANES, LANES)[:, :, 0, :]
    return dq


def flash_dkv(cfg, q, k, v, do, o, lse, name, out_dtype, add=None):
    blk, dqk, g, nw = cfg.blk, cfg.dqk, cfg.group, cfg.steps
    has_add = add is not None

    def body(*refs):
        k_ref, v_ref, q_ref, do_ref, o_ref, lse_ref = refs[:6]
        pos = 8 if has_add else 6
        dk_ref, dv_ref = refs[pos:pos + 2]
        dk_sc, dv_sc = refs[-2:]
        i, j = pl.program_id(2), pl.program_id(3)

        @pl.when(j == 0)
        def _():
            dk_sc[...] = jnp.zeros_like(dk_sc)
            dv_sc[...] = jnp.zeros_like(dv_sc)

        _, q_nom = cfg.other(i, j % nw)
        q, do = q_ref[...], do_ref[...]
        lse = lse_ref[:, :1]
        delta = jnp.sum(do.astype(F32) * o_ref[...].astype(F32), axis=-1, keepdims=True)
        for rows in cfg.row_chunks(blk):
            sc = _scores(cfg, q, k_ref[rows, :], q_nom, i)
            p = jnp.exp(sc - lse)
            dv_sc[rows, :] += lax.dot_general(p.astype(BF), do, (((0,), (0,)), ((), ())), preferred_element_type=F32)
            dp = lax.dot_general(do, v_ref[rows, :], (((1,), (1,)), ((), ())), preferred_element_type=F32)
            ds = p * (dp - delta) * cfg.scale
            dk_sc[rows, :] += lax.dot_general(ds.astype(BF), q, (((0,), (0,)), ((), ())), preferred_element_type=F32)

        @pl.when(j == g * nw - 1)
        def _():
            dk, dv = dk_sc[...], dv_sc[...]
            if has_add:
                dk, dv = dk + refs[6][...].astype(F32), dv + refs[7][...].astype(F32)
            dk_ref[...] = dk.astype(dk_ref.dtype)
            dv_ref[...] = dv.astype(dv_ref.dtype)

    def qrow(i, j):
        return cfg.other(i, j % nw)[0]

    k_spec = pl.BlockSpec((blk, dqk), lambda r, h, i, j: (i, r * cfg.kc + cfg.k0 + h))
    v_spec = pl.BlockSpec((blk, LANES), lambda r, h, i, j: (i, r * cfg.vc + cfg.v0 + cfg.vstride * h))
    q_spec = pl.BlockSpec((cfg.oblk, dqk), lambda r, h, i, j: (qrow(i, j), r * cfg.qc + cfg.q0 + h * g + j // nw))
    o_spec = pl.BlockSpec((cfg.oblk, LANES), lambda r, h, i, j: (qrow(i, j), r * cfg.hq + h * g + j // nw))
    dk_spec = pl.BlockSpec((blk, dqk), lambda r, h, i, j: (i, r * cfg.hkv + h))
    dv_spec = pl.BlockSpec((blk, LANES), lambda r, h, i, j: (i, r * cfg.hkv + h))
    in_specs = [k_spec, v_spec, q_spec, o_spec, o_spec, o_spec]
    operands = [cfg.chains(k), cfg.chains(v), cfg.chains(q), cfg.chains(do), cfg.chains(o), cfg.chains(lse)]
    if has_add:
        in_specs += [dk_spec, dv_spec]
        operands += [cfg.chains(add[0]), cfg.chains(add[1])]
    dk, dv = pl.pallas_call(
        body,
        out_shape=[jax.ShapeDtypeStruct((cfg.len, cfg.dil * cfg.hkv * dqk), out_dtype),
                   jax.ShapeDtypeStruct((cfg.len, cfg.dil * cfg.hkv * LANES), out_dtype)],
        grid=(cfg.dil, cfg.hkv, cfg.nb, g * nw), in_specs=in_specs, out_specs=[dk_spec, dv_spec],
        scratch_shapes=[pltpu.VMEM((blk, dqk), F32), pltpu.VMEM((blk, LANES), F32)],
        compiler_params=_params(("parallel", "parallel", "parallel", "arbitrary")), name=name,
    )(*operands)
    return cfg.unchain(dk, cfg.hkv * dqk), cfg.unchain(dv, cfg.hkv * LANES)


class Band:
    def __init__(self, T, dil, hq, group, per, qc, q0, kc, k0, vc, v0, scale, hw, blk):
        self.T, self.dil, self.hq, self.group, self.per = T, dil, hq, group, per
        self.pk = per // group
        self.hkv = hq // group
        self.scale, self.hw = scale, hw
        self.len = T // dil
        self.blk = min(blk, self.len)
        self.nb = self.len // self.blk
        self.win = self.blk + 2 * hw
        self.qcol = lambda r: (r * qc + q0) // per
        self.kcol = lambda r: (r * kc + k0) // self.pk
        self.vcol = lambda r: (r * vc + v0) // self.pk
        self.ocol = lambda r: (r * hq) // per
        self.dkcol = lambda r: (r * self.hkv) // self.pk
        assert hw <= self.blk and qc % per == 0 and q0 % per == 0 and kc % self.pk == 0 and k0 % self.pk == 0
        assert vc % self.pk == 0 and v0 % self.pk == 0

    def chains(self, a):
        return a.reshape(self.len, self.dil * a.shape[1])

    def rows3(self, width, col):
        nb = self.nb
        return [pl.BlockSpec((self.blk, width), lambda r, h, i: (jnp.maximum(i - 1, 0), col(r) + h)),
                pl.BlockSpec((self.blk, width), lambda r, h, i: (i, col(r) + h)),
                pl.BlockSpec((self.blk, width), lambda r, h, i: (jnp.minimum(i + 1, nb - 1), col(r) + h))]

    def window(self, prev, cur, nxt, j):
        cols = slice(j * LANES, (j + 1) * LANES)
        return jnp.concatenate([prev[self.blk - self.hw:, cols], cur[:, cols], nxt[:self.hw, cols]], axis=0)

    def valid(self, i, window_is_rows):
        shape = (self.win, self.blk) if window_is_rows else (self.blk, self.win)
        wdim = 0 if window_is_rows else 1
        bpos = i * self.blk + lax.broadcasted_iota(jnp.int32, shape, 1 - wdim)
        wpos = i * self.blk - self.hw + lax.broadcasted_iota(jnp.int32, shape, wdim)
        ok = jnp.abs(bpos - wpos) <= self.hw
        return jnp.logical_and(ok, jnp.logical_and(wpos >= 0, wpos < self.len))


def band_fwd(cfg, q, k, v, name, out_dtype, sink=None):
    blk, per, pk = cfg.blk, cfg.per, cfg.pk
    has_sink = sink is not None

    def body(*refs):
        if has_sink:
            sink_ref, refs = refs[0], refs[1:]
        q_ref, kp, kc, kn, vp, vc, vn, o_ref, lse_ref = refs
        ok = cfg.valid(pl.program_id(2), False)
        for j in range(per):
            jk = j // cfg.group
            if j % cfg.group == 0:
                kw = cfg.window(kp, kc, kn, jk)
                vw = cfg.window(vp, vc, vn, jk)
            cols = slice(j * LANES, (j + 1) * LANES)
            s = lax.dot_general(q_ref[:, cols], kw, (((1,), (1,)), ((), ())), preferred_element_type=F32) * cfg.scale
            s = jnp.where(ok, s, NEG)
            m = jnp.max(s, axis=-1, keepdims=True)
            if has_sink:
                sk = sink_ref[j, :1, :1]
                m = jnp.maximum(m, sk)
            e = jnp.exp(s - m)
            den = jnp.sum(e, axis=-1, keepdims=True)
            if has_sink:
                den = den + jnp.exp(sk - m)
            o = jnp.dot(e.astype(BF), vw, preferred_element_type=F32) / den
            o_ref[:, cols] = o.astype(o_ref.dtype)
            lse_ref[:, cols] = jnp.broadcast_to(m + jnp.log(den), (blk, LANES))

    q_spec = pl.BlockSpec((blk, per * LANES), lambda r, h, i: (i, cfg.qcol(r) + h))
    o_spec = pl.BlockSpec((blk, per * LANES), lambda r, h, i: (i, cfg.ocol(r) + h))
    in_specs = [q_spec] + cfg.rows3(pk * LANES, cfg.kcol) + cfg.rows3(pk * LANES, cfg.vcol)
    kc_, vc_ = cfg.chains(k), cfg.chains(v)
    operands = [cfg.chains(q), kc_, kc_, kc_, vc_, vc_, vc_]
    if has_sink:
        in_specs.insert(0, pl.BlockSpec((per, SUBLANES, LANES), lambda r, h, i: (h, 0, 0)))
        operands.insert(0, sink)
    cols = cfg.dil * cfg.hq * LANES
    o, lse = pl.pallas_call(
        body, out_shape=[jax.ShapeDtypeStruct((cfg.len, cols), out_dtype), jax.ShapeDtypeStruct((cfg.len, cols), F32)],
        grid=(cfg.dil, cfg.hq // per, cfg.nb), in_specs=in_specs, out_specs=[o_spec, o_spec],
        compiler_params=_params(("parallel", "parallel", "parallel")), name=name,
    )(*operands)
    return o.reshape(cfg.T, cfg.hq * LANES), lse.reshape(cfg.T, cfg.hq * LANES)


def band_dq(cfg, q, k, v, do, o, lse, name, sink=None):
    blk, per, pk = cfg.blk, cfg.per, cfg.pk
    has_sink = sink is not None

    def body(*refs):
        if has_sink:
            sink_ref, refs = refs[0], refs[1:]
        q_ref, kp, kc, kn, vp, vc, vn, do_ref, o_ref, lse_ref, dq_ref = refs[:11]
        ok = cfg.valid(pl.program_id(2), False)
        for j in range(per):
            jk = j // cfg.group
            if j % cfg.group == 0:
                kw = cfg.window(kp, kc, kn, jk)
                vw = cfg.window(vp, vc, vn, jk)
            cols = slice(j * LANES, (j + 1) * LANES)
            do = do_ref[:, cols]
            lse = lse_ref[:, j * LANES:j * LANES + 1]
            delta = jnp.sum(do.astype(F32) * o_ref[:, cols].astype(F32), axis=-1, keepdims=True)
            s = lax.dot_general(q_ref[:, cols], kw, (((1,), (1,)), ((), ())), preferred_element_type=F32) * cfg.scale
            p = jnp.exp(jnp.where(ok, s, NEG) - lse)
            dp = lax.dot_general(do, vw, (((1,), (1,)), ((), ())), preferred_element_type=F32)
            ds = p * (dp - delta) * cfg.scale
            dq_ref[:, cols] = jnp.dot(ds.astype(BF), kw, preferred_element_type=F32).astype(dq_ref.dtype)
            if has_sink:
                part = -jnp.sum(jnp.exp(sink_ref[j, :1, :1] - lse) * delta, axis=0, keepdims=True)
                refs[11][j * SUBLANES:(j + 1) * SUBLANES, :] = jnp.broadcast_to(part, (SUBLANES, LANES))

    q_spec = pl.BlockSpec((blk, per * LANES), lambda r, h, i: (i, cfg.qcol(r) + h))
    o_spec = pl.BlockSpec((blk, per * LANES), lambda r, h, i: (i, cfg.ocol(r) + h))
    in_specs = [q_spec] + cfg.rows3(pk * LANES, cfg.kcol) + cfg.rows3(pk * LANES, cfg.vcol) + [o_spec] * 3
    kc_, vc_ = cfg.chains(k), cfg.chains(v)
    operands = [cfg.chains(q), kc_, kc_, kc_, vc_, vc_, vc_, cfg.chains(do), cfg.chains(o), cfg.chains(lse)]
    out_shape = [jax.ShapeDtypeStruct((cfg.len, cfg.dil * cfg.hq * LANES), BF)]
    out_specs = [o_spec]
    if has_sink:
        in_specs.insert(0, pl.BlockSpec((per, SUBLANES, LANES), lambda r, h, i: (h, 0, 0)))
        operands.insert(0, sink)
        out_shape.append(jax.ShapeDtypeStruct((cfg.hq // per, cfg.nb, per * SUBLANES, LANES), F32))
        out_specs.append(pl.BlockSpec((None, None, per * SUBLANES, LANES), lambda r, h, i: (h, i, 0, 0)))
    outs = pl.pallas_call(
        body, out_shape=out_shape, grid=(cfg.dil, cfg.hq // per, cfg.nb), in_specs=in_specs, out_specs=out_specs,
        compiler_params=_params(("parallel", "parallel", "parallel")), name=name,
    )(*operands)
    dq = outs[0].reshape(cfg.T, cfg.hq * LANES)
    return (dq, outs[1]) if has_sink else dq


def band_dkv(cfg, q, k, v, do, o, lse, name, out_dtype, add=None, dv_into=None):
    blk, per, pk, group = cfg.blk, cfg.per, cfg.pk, cfg.group
    has_add = add is not None
    carried = dv_into is not None
    assert not carried or cfg.dil == 1

    def body(*refs):
        k_ref, v_ref = refs[:2]
        qs, dos, os_, lses = refs[2:5], refs[5:8], refs[8:11], refs[11:14]
        pos = 14 + (2 if has_add else 0) + (1 if carried else 0)
        dk_ref, dv_ref = refs[pos:pos + 2]
        ok = cfg.valid(pl.program_id(2), True)
        for jk in range(pk):
            kcols = slice(jk * LANES, (jk + 1) * LANES)
            kt, vt = k_ref[:, kcols], v_ref[:, kcols]
            dk = jnp.zeros((blk, LANES), F32)
            dv = jnp.zeros((blk, LANES), F32)
            for g in range(group):
                j = jk * group + g
                qw = cfg.window(*qs, j)
                dow = cfg.window(*dos, j)
                lse = cfg.window(*lses, j)[:, :1]
                delta = jnp.sum(dow.astype(F32) * cfg.window(*os_, j).astype(F32), axis=-1, keepdims=True)
                s = lax.dot_general(qw, kt, (((1,), (1,)), ((), ())), preferred_element_type=F32) * cfg.scale
                p = jnp.exp(jnp.where(ok, s, NEG) - lse)
                dv = dv + lax.dot_general(p.astype(BF), dow, (((0,), (0,)), ((), ())), preferred_element_type=F32)
                dp = lax.dot_general(dow, vt, (((1,), (1,)), ((), ())), preferred_element_type=F32)
                ds = p * (dp - delta) * cfg.scale
                dk = dk + lax.dot_general(ds.astype(BF), qw, (((0,), (0,)), ((), ())), preferred_element_type=F32)
            if has_add:
                dk, dv = dk + refs[14][:, kcols].astype(F32), dv + refs[15][:, kcols].astype(F32)
            dk_ref[:, kcols] = dk.astype(dk_ref.dtype)
            dv_ref[:, kcols] = dv.astype(dv_ref.dtype)

    k_spec = pl.BlockSpec((blk, pk * LANES), lambda r, h, i: (i, cfg.kcol(r) + h))
    v_spec = pl.BlockSpec((blk, pk * LANES), lambda r, h, i: (i, cfg.vcol(r) + h))
    d_spec = pl.BlockSpec((blk, pk * LANES), lambda r, h, i: (i, cfg.dkcol(r) + h))
    in_specs = [k_spec, v_spec] + cfg.rows3(per * LANES, cfg.qcol) + cfg.rows3(per * LANES, cfg.ocol) * 3
    qc_, doc, oc, lc = cfg.chains(q), cfg.chains(do), cfg.chains(o), cfg.chains(lse)
    operands = [cfg.chains(k), cfg.chains(v), qc_, qc_, qc_, doc, doc, doc, oc, oc, oc, lc, lc, lc]
    if has_add:
        in_specs += [d_spec, d_spec]
        operands += [cfg.chains(add[0]), cfg.chains(add[1])]
    cols = cfg.dil * cfg.hkv * LANES
    out_shape = [jax.ShapeDtypeStruct((cfg.len, cols), out_dtype)] * 2
    out_specs = [d_spec, d_spec]
    aliases = {}
    if carried:
        buf, blocks, block0 = dv_into
        out_shape[1] = jax.ShapeDtypeStruct((cfg.T, blocks * LANES), BF)
        out_specs[1] = pl.BlockSpec((blk, pk * LANES), lambda r, h, i: (i, block0 // pk + h))
        aliases = {len(operands): 1}
        in_specs.append(pl.BlockSpec(memory_space=pl.ANY))
        operands.append(buf)
    dk, dv = pl.pallas_call(
        body, out_shape=out_shape, grid=(cfg.dil, cfg.hq // per, cfg.nb), in_specs=in_specs, out_specs=out_specs,
        input_output_aliases=aliases, compiler_params=_params(("parallel", "parallel", "parallel")), name=name,
    )(*operands)
    return dk.reshape(cfg.T, cfg.hkv * LANES), (dv if carried else dv.reshape(cfg.T, cfg.hkv * LANES))


HBM_SPEC = pl.BlockSpec(memory_space=pltpu.HBM)


def _place():
    x, y, c = lax.axis_index("x"), lax.axis_index("y"), lax.axis_index("c")
    chips = [(1 - x, y), (x, 1 - y), (1 - x, 1 - y)]
    return x, y, c, chips


def gather_weights(shards):
    n = len(shards)

    def body(*refs):
        ins, outs = refs[:n], refs[n:2 * n]
        send_sems, recv_sems, local_sems = refs[2 * n:]
        x, y, c, chips = _place()
        me = 2 * x + y
        sibling = (x, y, 1 - c)

        def copy(w, k, src, chip_of_block, half, to):
            return pltpu.make_async_remote_copy(
                src_ref=src, dst_ref=outs[w].at[chip_of_block, half], send_sem=send_sems.at[6 * w + k],
                recv_sem=recv_sems.at[6 * w + k], device_id=to, device_id_type=MESH)

        started = []
        local = []
        for w in range(n):
            own = pltpu.make_async_copy(ins[w], outs[w].at[me], local_sems.at[w])
            own.start()
            local.append(own)
            for j, chip in enumerate(chips):
                cp = copy(w, j, ins[w].at[c], me, c, (*chip, c))
                cp.start()
                started.append(cp)
        for w in range(n):
            for j, (cx, cy) in enumerate(chips):
                them = 2 * cx + cy
                copy(w, j, ins[w].at[c], them, c, (cx, cy, c)).wait_recv()
                fwd = copy(w, 3 + j, outs[w].at[them, c], them, c, sibling)
                fwd.start()
                started.append(fwd)
        for w in range(n):
            for j, (cx, cy) in enumerate(chips):
                copy(w, 3 + j, ins[w].at[c], 2 * cx + cy, 1 - c, sibling).wait_recv()
        for cp in started:
            cp.wait_send()
        for own in local:
            own.wait()

    return pl.pallas_call(
        body, out_shape=[jax.ShapeDtypeStruct((4,) + s.shape, s.dtype) for s in shards],
        in_specs=[HBM_SPEC] * n, out_specs=[HBM_SPEC] * n,
        scratch_shapes=[pltpu.SemaphoreType.DMA((6 * n,)), pltpu.SemaphoreType.DMA((6 * n,)),
                        pltpu.SemaphoreType.DMA((n,))],
        name="gather_weights",
    )(*shards)


def _core_index():
    return lax.axis_index("c").astype(jnp.int32).reshape(1)


def presum_core_halves(g2, core, name, ship=None):
    _, rows, cols = g2.shape
    tr = _row_tile(rows, cols, 1 << 20)
    nb = rows // tr
    g2 = g2.reshape(2 * rows, cols)
    shipping = ship is not None

    def body(*refs):
        core_ref, mine_ref, other_ref = refs[:3]
        if shipping:
            ship_ref, out_ref, landed_ref, land, send_sems, recv_sems, ici_send, ici_recv, ici_local = refs[3:]
        else:
            out_ref, land, send_sems, recv_sems = refs[3:]
        x, y, c, chips = _place()
        i = pl.program_id(0)
        if shipping:
            me = 2 * x + y

            def own():
                return pltpu.make_async_copy(ship_ref.at[me], landed_ref.at[me], ici_local.at[0])

            def to_chip(j, cx, cy):
                return pltpu.make_async_remote_copy(
                    src_ref=ship_ref.at[2 * cx + cy], dst_ref=landed_ref.at[me], send_sem=ici_send.at[j],
                    recv_sem=ici_recv.at[j], device_id=(cx, cy, c), device_id_type=MESH)

            def from_chip(j, cx, cy):
                return pltpu.make_async_remote_copy(
                    src_ref=ship_ref.at[me], dst_ref=landed_ref.at[2 * cx + cy], send_sem=ici_send.at[j],
                    recv_sem=ici_recv.at[j], device_id=(cx, cy, c), device_id_type=MESH)

            @pl.when(i == 0)
            def _():
                own().start()
                for j, (cx, cy) in enumerate(chips):
                    to_chip(j, cx, cy).start()

        slot = i % 2
        cp = pltpu.make_async_remote_copy(
            src_ref=other_ref, dst_ref=land.at[slot], send_sem=send_sems.at[slot], recv_sem=recv_sems.at[slot],
            device_id=(x, y, 1 - c), device_id_type=MESH)
        cp.start()
        cp.wait_recv()
        out_ref[...] = (mine_ref[...] + land[slot]).astype(out_ref.dtype)
        cp.wait_send()
        if shipping:
            @pl.when(i == nb - 1)
            def _():
                for j, (cx, cy) in enumerate(chips):
                    from_chip(j, cx, cy).wait_recv()
                for j, (cx, cy) in enumerate(chips):
                    to_chip(j, cx, cy).wait_send()
                own().wait()

    in_specs = [pl.BlockSpec((tr, cols), lambda i, core: (core[0] * nb + i, 0)),
                pl.BlockSpec((tr, cols), lambda i, core: ((1 - core[0]) * nb + i, 0))]
    out_specs = [pl.BlockSpec((tr, cols), lambda i, core: (i, 0))]
    out_shape = [jax.ShapeDtypeStruct((rows, cols), BF)]
    scratch = [pltpu.VMEM((2, tr, cols), F32), pltpu.SemaphoreType.DMA((2,)), pltpu.SemaphoreType.DMA((2,))]
    operands = [core, g2, g2]
    if shipping:
        in_specs.append(pl.BlockSpec(memory_space=pl.ANY))
        out_specs.append(pl.BlockSpec(memory_space=pl.ANY))
        out_shape.append(jax.ShapeDtypeStruct(ship.shape, ship.dtype))
        scratch += [pltpu.SemaphoreType.DMA((3,)), pltpu.SemaphoreType.DMA((3,)), pltpu.SemaphoreType.DMA((1,))]
        operands.append(ship)
    grid_spec = pltpu.PrefetchScalarGridSpec(
        num_scalar_prefetch=1, grid=(nb,), in_specs=in_specs, out_specs=out_specs, scratch_shapes=scratch)
    outs = pl.pallas_call(
        body, out_shape=out_shape, grid_spec=grid_spec, compiler_params=_params(("arbitrary",)), name=name,
    )(*operands)
    return (outs[0], outs[1]) if shipping else outs[0]


def sum_and_swap(landed, name):
    n, rows, cols = landed.shape
    tr = _row_tile(rows, cols)

    def body(*refs):
        slots = refs[:n]
        mine_ref, theirs_ref, out_buf, land, send_sems, recv_sems = refs[n:]
        x, y, c, _ = _place()
        slot = pl.program_id(0) % 2
        tot = slots[0][...].astype(F32)
        for r in slots[1:]:
            tot = tot + r[...].astype(F32)
        mine_ref[...] = tot
        out_buf[slot] = tot
        cp = pltpu.make_async_remote_copy(
            src_ref=out_buf.at[slot], dst_ref=land.at[slot], send_sem=send_sems.at[slot], recv_sem=recv_sems.at[slot],
            device_id=(x, y, 1 - c), device_id_type=MESH)
        cp.start()
        cp.wait_recv()
        theirs_ref[...] = land[slot]
        cp.wait_send()

    specs = [pl.BlockSpec((None, tr, cols), functools.partial(lambda s, i: (s, i, 0), s)) for s in range(n)]
    row = pl.BlockSpec((tr, cols), lambda i: (i, 0))
    return pl.pallas_call(
        body, out_shape=[jax.ShapeDtypeStruct((rows, cols), F32)] * 2, grid=(rows // tr,), in_specs=specs,
        out_specs=[row, row],
        scratch_shapes=[pltpu.VMEM((2, tr, cols), F32), pltpu.VMEM((2, tr, cols), F32),
                        pltpu.SemaphoreType.DMA((2,)), pltpu.SemaphoreType.DMA((2,))],
        compiler_params=_params(("arbitrary",)), name=name,
    )(*([landed] * n))


def scatter_partials(parts):
    n = len(parts)

    def body(*refs):
        ins, outs = refs[:n], refs[n:2 * n]
        send_sems, recv_sems, local_sems = refs[2 * n:]
        x, y, c, chips = _place()
        me = 2 * x + y
        started = []
        for w in range(n):
            own = pltpu.make_async_copy(ins[w].at[me], outs[w].at[me], local_sems.at[w])
            own.start()
            started.append(own)
        sends = []
        for w in range(n):
            for j, (cx, cy) in enumerate(chips):
                cp = pltpu.make_async_remote_copy(
                    src_ref=ins[w].at[2 * cx + cy], dst_ref=outs[w].at[me], send_sem=send_sems.at[3 * w + j],
                    recv_sem=recv_sems.at[3 * w + j], device_id=(cx, cy, c), device_id_type=MESH)
                cp.start()
                sends.append(cp)
        for w in range(n):
            for j, (cx, cy) in enumerate(chips):
                pltpu.make_async_remote_copy(
                    src_ref=ins[w].at[me], dst_ref=outs[w].at[2 * cx + cy], send_sem=send_sems.at[3 * w + j],
                    recv_sem=recv_sems.at[3 * w + j], device_id=(cx, cy, c), device_id_type=MESH).wait_recv()
        for cp in sends:
            cp.wait_send()
        for own in started:
            own.wait()

    return pl.pallas_call(
        body, out_shape=[jax.ShapeDtypeStruct(p.shape, p.dtype) for p in parts],
        in_specs=[HBM_SPEC] * n, out_specs=[HBM_SPEC] * n,
        scratch_shapes=[pltpu.SemaphoreType.DMA((3 * n,)), pltpu.SemaphoreType.DMA((3 * n,)),
                        pltpu.SemaphoreType.DMA((n,))],
        name="scatter_partials",
    )(*parts)


def adamw_halves(w, mine, theirs, m, v, core, name):
    rows, cols = w.shape
    tr = _row_tile(rows // 2, cols, 1 << 18)
    nh = rows // 2 // tr

    def body(core_ref, w_ref, a_ref, b_ref, m_ref, v_ref, g_out, d_out, m_out, v_out):
        g = jnp.where(pl.program_id(0) // nh == core_ref[0], a_ref[...], b_ref[...])
        d_out[...], m_out[...], v_out[...] = _adam_fn(w_ref[...], g, m_ref[...], v_ref[...])
        g_out[...] = g

    full = pl.BlockSpec((tr, cols), lambda i, core: (i, 0))
    half = pl.BlockSpec((tr, cols), lambda i, core: (i % nh, 0))
    grid_spec = pltpu.PrefetchScalarGridSpec(
        num_scalar_prefetch=1, grid=(rows // tr,), in_specs=[full, half, half, full, full], out_specs=[full] * 4)
    return pl.pallas_call(
        body, out_shape=[jax.ShapeDtypeStruct((rows, cols), F32)] * 4, grid_spec=grid_spec,
        compiler_params=_params(("parallel",)), name=name,
    )(core, w, mine, theirs, m, v)


def gather_small(vec):
    rows = vec.shape[0]

    def body(v_ref, out_ref, send_sems, recv_sems):
        x, y, c, _ = _place()
        me = 4 * x + 2 * y + c
        out_ref[me] = v_ref[...]
        flips = [(dx, dy, dc) for dx in (0, 1) for dy in (0, 1) for dc in (0, 1)][1:]

        def peer(f):
            return tuple(1 - a if d else a for a, d in zip((x, y, c), f))

        def copy(k, block, to):
            return pltpu.make_async_remote_copy(
                src_ref=v_ref, dst_ref=out_ref.at[block], send_sem=send_sems.at[k], recv_sem=recv_sems.at[k],
                device_id=to, device_id_type=MESH)

        sends = [copy(k, me, peer(f)) for k, f in enumerate(flips)]
        for cp in sends:
            cp.start()
        for k, f in enumerate(flips):
            px, py, pc = peer(f)
            copy(k, 4 * px + 2 * py + pc, peer(f)).wait_recv()
        for cp in sends:
            cp.wait_send()

    vm = pl.BlockSpec(memory_space=pltpu.VMEM)
    return pl.pallas_call(
        body, out_shape=jax.ShapeDtypeStruct((8, rows, SMALL_COLS), F32), in_specs=[vm], out_specs=vm,
        scratch_shapes=[pltpu.SemaphoreType.DMA((7,)), pltpu.SemaphoreType.DMA((7,))], name="gather_small",
    )(vec)


def sum_slots(a, out_dtype, name):
    n, rows, cols = a.shape
    tr = _row_tile(rows, cols)

    def body(*refs):
        tot = refs[0][...].astype(F32)
        for r in refs[1:n]:
            tot = tot + r[...].astype(F32)
        refs[n][...] = tot.astype(out_dtype)

    specs = [pl.BlockSpec((None, tr, cols), functools.partial(lambda s, i: (s, i, 0), s)) for s in range(n)]
    return pl.pallas_call(
        body, out_shape=jax.ShapeDtypeStruct((rows, cols), out_dtype), grid=(rows // tr,), in_specs=specs,
        out_specs=pl.BlockSpec((tr, cols), lambda i: (i, 0)), compiler_params=_params(("parallel",)), name=name,
    )(*([a] * n))


def _adam_fn(w, g, m, v):
    m = ADAM_B1 * m + (1.0 - ADAM_B1) * g
    v = ADAM_B2 * v + (1.0 - ADAM_B2) * (g * g)
    m_hat = m / (1.0 - ADAM_B1 ** ADAM_STEP)
    v_hat = v / (1.0 - ADAM_B2 ** ADAM_STEP)
    delta = -ADAM_LR * (m_hat / (jnp.sqrt(v_hat) + ADAM_EPS) + ADAM_WD * w)
    return delta, m, v


def adamw(w, g, m, v, name):
    return rowwise(_adam_fn, [w, g, m, v], [F32, F32, F32], name)


def _full_weight(name, gathered, local_shape):
    L, a, b = local_shape
    g = gathered.reshape((4, L, a, b))
    if SHARD_AXIS[name] == 1:
        return g.transpose(1, 0, 2, 3).reshape(L, 4 * a, b)
    return g.transpose(1, 2, 0, 3).reshape(L, a, 4 * b)


def _grad_slots(name, dw):
    L, a, b = dw.shape
    if SHARD_AXIS[name] == 1:
        s = dw.reshape(L, 4, a // 4, b).transpose(1, 0, 2, 3)
        rows, cols = L * (a // 4), b
    else:
        s = dw.reshape(L, a, 4, b // 4).transpose(2, 0, 1, 3)
        rows, cols = L * a, b // 4
    return s.reshape(4, 2, rows // 2, cols).transpose(1, 0, 2, 3)


def _attn_a(T):
    group = A_HEADS // A_KV_HEADS
    return Band(T, 1, A_HEADS, group, group, A_HEADS, 0, A_KV_HEADS, 0, A_HEADS + 2 * A_KV_HEADS,
                A_HEADS + A_KV_HEADS, 1.0 / math.sqrt(HEAD_DIM), A_HALF_WINDOW, BAND_BLOCK)


def _attn_b(T):
    return Attn(T, 1, B_HEADS, 1, B_HEADS, 0, B_HEADS, 0, 2 * B_HEADS, 1, 2, B_PAD, 1.0 / math.sqrt(B_QK), None,
                DENSE_BLOCK, DENSE_OTHER_BLOCK)


def _attn_c(T, group):
    window, dil = C_PATTERNS[group]
    return Band(T, dil, C_HEADS, 1, BAND_HEADS_PER_STEP, C_HEADS, 0, C_HEADS, 0, C_HEADS, 0,
                1.0 / math.sqrt(HEAD_DIM), window // 2 // dil, BAND_BLOCK)


def _pad_heads(a, axis_len_true, axis_len_pad):
    lead = a.shape[:-1]
    h = a.shape[-1] // axis_len_true
    a = a.reshape(lead + (h, axis_len_true))
    a = jnp.pad(a, [(0, 0)] * len(lead) + [(0, 0), (0, axis_len_pad - axis_len_true)])
    return a.reshape(lead + (h * axis_len_pad,))


def _unpad_heads(a, axis_len_true, axis_len_pad):
    lead = a.shape[:-1]
    h = a.shape[-1] // axis_len_pad
    return a.reshape(lead + (h, axis_len_pad))[..., :axis_len_true].reshape(lead + (h * axis_len_true,))


def _weight_grad(G, name, layer, a, dy, W, tag):
    layers, rows, cols = W[name].shape
    if name in SLOT_DIRECT:
        G[name] = matmul([(a, dy)], "tn", F32, tag, slot=Slot(name, layers, layer, rows, cols, 0, G.get(name)))
    else:
        G.setdefault(name, [None] * layers)[layer] = matmul([(a, dy)], "tn", F32, tag)


def _mixer_fwd(kind, slot, hn, W, S, tabs, tag):
    T = hn.shape[0]
    if kind == 0:
        cfg = _attn_a(T)
        qkv = matmul([(hn, W["a_w_in"][slot])], "nn", BF, tag + "_a_in")
        q = headnorm_fwd(qkv, W["a_q_norm"][slot], tabs["hd"], tag + "_a_qn", A_HEADS, 0, HEAD_DIM, HEAD_DIM)
        k = headnorm_fwd(qkv, W["a_k_norm"][slot], tabs["hd"], tag + "_a_kn", A_KV_HEADS, A_HEADS, HEAD_DIM, HEAD_DIM)
        sink = jnp.broadcast_to(W["a_sink"][slot][:, None, None], (A_HEADS, SUBLANES, LANES)).astype(F32)
        o, lse = band_fwd(cfg, q, k, qkv, tag + "_a_att", BF, sink=sink)
        S.update(qkv=qkv, q=q, k=k, o=o, lse=lse, sink=sink)
        return o
    if kind == 1:
        cfg = _attn_b(T)
        lat = matmul([(hn, W["b_w_in"][slot])], "nn", BF, tag + "_b_in")
        qn = rmsnorm_fwd(lat, W["b_q_lat_norm"][slot], tag + "_b_qlat", 0, B_Q_RANK)
        kvn = rmsnorm_fwd(lat, W["b_kv_lat_norm"][slot], tag + "_b_kvlat", 1, B_KV_RANK)
        qp = matmul([(qn, W["b_w_q_up_pad"][slot])], "nn", BF, tag + "_b_qup")
        kv = matmul([(kvn, W["b_w_kv_up"][slot])], "nn", BF, tag + "_b_kvup")
        k_rope = lat[:, B_Q_RANK + B_KV_RANK:]
        kpre = jnp.concatenate(
            [kv.reshape(T, B_HEADS, 2 * B_NOPE)[:, :, :B_NOPE],
             jnp.broadcast_to(k_rope[:, None, :], (T, B_HEADS, B_ROPE)),
             jnp.zeros((T, B_HEADS, B_PAD - B_QK), BF)], axis=-1).reshape(T, B_HEADS * B_PAD)
        q = headnorm_fwd(qp, W["b_q_norm_pad"][slot], tabs["b"], tag + "_b_qn", B_HEADS, 0, B_PAD, B_QK)
        k = headnorm_fwd(kpre, W["b_k_norm_pad"][slot], tabs["b"], tag + "_b_kn", B_HEADS, 0, B_PAD, B_QK)
        o, lse = flash_fwd(cfg, q, k, kv, tag + "_b_att", BF)
        S.update(lat=lat, qn=qn, kvn=kvn, qp=qp, kv=kv, kpre=kpre, q=q, k=k, o=o, lse=lse)
        return o
    qkv = matmul([(hn, W["c_w_in"][slot])], "nn", BF, tag + "_c_in")
    nq = C_GROUPS * C_HEADS
    qs = [headnorm_fwd(qkv, W["c_q_norm"][slot], tabs["hd"], f"{tag}_c_qn{g}", C_HEADS, g * C_HEADS, HEAD_DIM, HEAD_DIM)
          for g in range(C_GROUPS)]
    k = headnorm_fwd(qkv, W["c_k_norm"][slot], tabs["hd"], tag + "_c_kn", C_HEADS, nq, HEAD_DIM, HEAD_DIM)
    outs, lses = [], []
    v = qkv[:, (C_GROUPS + 1) * C_HEADS * HEAD_DIM:]
    for g in range(C_GROUPS):
        og, lg = band_fwd(_attn_c(T, g), qs[g], k, v, f"{tag}_c_att{g}", F32)
        outs.append(og)
        lses.append(lg)
    o, lse = rowwise(_merge_fn, outs + lses, [BF, F32], tag + "_c_merge")
    S.update(qkv=qkv, qs=qs, v=v, k=k, o=o, lse=lse)
    return o


def _mixer_bwd(kind, slot, hn, do, W, S, tabs, tag, G):
    T = hn.shape[0]
    if kind == 0:
        cfg = _attn_a(T)
        qkv = S["qkv"]
        dq, dsink = band_dq(cfg, S["q"], S["k"], qkv, do, S["o"], S["lse"], tag + "_a_dq", sink=S["sink"])
        blocks = A_HEADS + 2 * A_KV_HEADS
        dqkv, dgq = headnorm_bwd(qkv, W["a_q_norm"][slot], tabs["hd"], dq, tag + "_a_dqn", A_HEADS, 0, HEAD_DIM, HEAD_DIM,
                                 into=(None, blocks, 0))
        dk, dqkv = band_dkv(cfg, S["q"], S["k"], qkv, do, S["o"], S["lse"], tag + "_a_dkv", BF,
                            dv_into=(dqkv, blocks, A_HEADS + A_KV_HEADS))
        dqkv, dgk = headnorm_bwd(qkv, W["a_k_norm"][slot], tabs["hd"], dk, tag + "_a_dkn", A_KV_HEADS, A_HEADS,
                                 HEAD_DIM, HEAD_DIM, into=(dqkv, blocks, A_HEADS))
        _weight_grad(G, "a_w_in", slot, hn, dqkv, W, tag + "_a_dwin")
        G["a_q_norm"][slot], G["a_k_norm"][slot] = dgq, dgk
        parts = dsink.reshape(A_HEADS // cfg.per, cfg.nb, cfg.per, SUBLANES, LANES)[:, :, :, 0, 0]
        G["a_sink"][slot] = jnp.sum(parts, axis=1).reshape(A_HEADS)
        return matmul([(dqkv, W["a_w_in"][slot])], "nt", F32, tag + "_a_dhn")
    if kind == 1:
        cfg = _attn_b(T)
        kv = S["kv"]
        dq = flash_dq(cfg, S["q"], S["k"], kv, do, S["o"], S["lse"], tag + "_b_dq")
        dk, dv = flash_dkv(cfg, S["q"], S["k"], kv, do, S["o"], S["lse"], tag + "_b_dkv", BF)
        dqp, dgq = headnorm_bwd(S["qp"], W["b_q_norm_pad"][slot], tabs["b"], dq, tag + "_b_dqn", B_HEADS, 0, B_PAD, B_QK)
        dkp, dgk, dksum = headnorm_bwd(S["kpre"], W["b_k_norm_pad"][slot], tabs["b"], dk, tag + "_b_dkn", B_HEADS, 0,
                                       B_PAD, B_QK, head_sum=True)
        dkv = jnp.concatenate([dkp.reshape(T, B_HEADS, B_PAD)[:, :, :B_NOPE], dv.reshape(T, B_HEADS, LANES)],
                              axis=-1).reshape(T, B_HEADS * 2 * B_NOPE)
        _weight_grad(G, "b_w_kv_up", slot, S["kvn"], dkv, W, tag + "_b_dwkv")
        G["b_w_q_up"][slot] = _unpad_heads(matmul([(S["qn"], dqp)], "tn", F32, tag + "_b_dwq"), B_QK, B_PAD)
        dqn = matmul([(dqp, W["b_w_q_up_pad"][slot])], "nt", F32, tag + "_b_dqnorm")
        dkvn = matmul([(dkv, W["b_w_kv_up"][slot])], "nt", F32, tag + "_b_dkvnorm")
        dql, dg_q = rmsnorm_bwd(S["lat"], W["b_q_lat_norm"][slot], dqn, tag + "_b_dqlat", [BF], None, 0, B_Q_RANK)
        dkvl, dg_kv = rmsnorm_bwd(S["lat"], W["b_kv_lat_norm"][slot], dkvn, tag + "_b_dkvlat", [BF], None, 1, B_KV_RANK)
        dlat = jnp.concatenate([dql, dkvl, dksum[:, B_NOPE:B_QK].astype(BF)], axis=1)
        _weight_grad(G, "b_w_in", slot, hn, dlat, W, tag + "_b_dwin")
        G["b_q_norm"][slot], G["b_k_norm"][slot] = dgq[:B_QK], dgk[:B_QK]
        G["b_q_lat_norm"][slot], G["b_kv_lat_norm"][slot] = dg_q, dg_kv
        return matmul([(dlat, W["b_w_in"][slot])], "nt", F32, tag + "_b_dhn")
    qkv = S["qkv"]
    nq = C_GROUPS * C_HEADS
    blocks = (C_GROUPS + 2) * C_HEADS
    dqkv, dgq = None, 0.0
    for g in range(C_GROUPS):
        dq = band_dq(_attn_c(T, g), S["qs"][g], S["k"], S["v"], do, S["o"], S["lse"], f"{tag}_c_dq{g}")
        dqkv, dg = headnorm_bwd(qkv, W["c_q_norm"][slot], tabs["hd"], dq, f"{tag}_c_dqn{g}", C_HEADS, g * C_HEADS,
                                HEAD_DIM, HEAD_DIM, into=(dqkv, blocks, g * C_HEADS))
        dgq = dgq + dg
    acc = None
    for g in reversed(range(C_GROUPS)):
        into = (dqkv, blocks, (C_GROUPS + 1) * C_HEADS) if g == 0 else None
        acc = band_dkv(_attn_c(T, g), S["qs"][g], S["k"], S["v"], do, S["o"], S["lse"], f"{tag}_c_dkv{g}", F32,
                       add=acc, dv_into=into)
    dk, dqkv = acc
    dqkv, dgk = headnorm_bwd(qkv, W["c_k_norm"][slot], tabs["hd"], dk, tag + "_c_dkn", C_HEADS, nq, HEAD_DIM, HEAD_DIM,
                             into=(dqkv, blocks, nq))
    _weight_grad(G, "c_w_in", slot, hn, dqkv, W, tag + "_c_dwin")
    G["c_q_norm"][slot], G["c_k_norm"][slot] = dgq, dgk
    return matmul([(dqkv, W["c_w_in"][slot])], "nt", F32, tag + "_c_dhn")


MIXER_OUT = ("a_w_o", "b_w_o", "c_w_o")


def local_step(x, p, positions, loss_target, W):
    T = x.shape[0]
    tabs = {"hd": rope_tables(positions, HEAD_DIM, 0, PARTIAL_ROT), "b": rope_tables(positions, B_PAD, B_NOPE, B_ROPE)}
    W = dict(W)
    W["b_w_q_up_pad"] = _pad_heads(W["b_w_q_up"], B_QK, B_PAD)
    W["b_q_norm_pad"] = _pad_heads(W["b_q_norm"], B_QK, B_PAD)
    W["b_k_norm_pad"] = _pad_heads(W["b_k_norm"], B_QK, B_PAD)
    saved = []
    h = x
    for i in range(DEPTH):
        kind, slot = i % 3, i // 3
        tag = f"l{i}"
        S = {"h0": h}
        hn = rmsnorm_fwd(h, W["g_mix"][i], tag + "_mixnorm")
        o = _mixer_fwd(kind, slot, hn, W, S, tabs, tag)
        h1 = matmul([(o, W[MIXER_OUT[kind]][slot])], "nn", F32, tag + "_mixout", res=h)
        hn2 = rmsnorm_fwd(h1, W["g_ffn"][i], tag + "_ffnnorm")
        a, b, c = matmul_swiglu(hn2, W["w_ffn_gate"][i], W["w_ffn_up"][i], tag + "_gateup")
        h2 = matmul([(c, W["w_ffn_down"][i])], "nn", F32, tag + "_down", res=h1)
        hn3 = rmsnorm_fwd(h2, W["g_ple"][i], tag + "_plenorm")
        p_i = p[i].astype(BF)
        pp = matmul([(p_i, W["w_ple_proj"][i])], "nn", BF, tag + "_pleproj")
        z, h3 = matmul([(hn3, W["w_ple_gate"][i])], "nn", BF, tag + "_plegate", ple=(h2, pp))
        S.update(hn=hn, h1=h1, hn2=hn2, a=a, b=b, c=c, h2=h2, hn3=hn3, z=z, pp=pp, p=p_i)
        saved.append(S)
        h = h3

    loss, dh = loss_and_grad(h, loss_target, "loss")
    G = {n: [None] * W[n].shape[0] for n in SMALL + ("b_w_q_up",)}
    for i in reversed(range(DEPTH)):
        kind, slot = i % 3, i // 3
        tag = f"l{i}"
        S = saved[i]
        dz, dpp = rowwise(_ple_bwd_fn, [dh, S["z"], S["pp"]], [BF, BF], tag + "_dple")
        _weight_grad(G, "w_ple_proj", i, S["p"], dpp, W, tag + "_dwpleproj")
        _weight_grad(G, "w_ple_gate", i, S["hn3"], dz, W, tag + "_dwplegate")
        dhn3 = matmul([(dz, W["w_ple_gate"][i])], "nt", F32, tag + "_dplenorm")
        dh2, dh2b, G["g_ple"][i] = rmsnorm_bwd(S["h2"], W["g_ple"][i], dhn3, tag + "_dple_norm", [F32, BF], dres=dh)
        da, db = matmul([(dh2b, W["w_ffn_down"][i])], "nt", BF, tag + "_dswiglu", swiglu=(S["a"], S["b"]))
        _weight_grad(G, "w_ffn_down", i, S["c"], dh2b, W, tag + "_dwdown")
        _weight_grad(G, "w_ffn_gate", i, S["hn2"], da, W, tag + "_dwgate")
        _weight_grad(G, "w_ffn_up", i, S["hn2"], db, W, tag + "_dwup")
        dhn2 = matmul([(da, W["w_ffn_gate"][i]), (db, W["w_ffn_up"][i])], "nt", F32, tag + "_dffnnorm")
        dh1, dh1b, G["g_ffn"][i] = rmsnorm_bwd(S["h1"], W["g_ffn"][i], dhn2, tag + "_dffn_norm", [F32, BF], dres=dh2)
        wo = W[MIXER_OUT[kind]][slot]
        do = matmul([(dh1b, wo)], "nt", BF, tag + "_dmixout")
        _weight_grad(G, MIXER_OUT[kind], slot, S["o"], dh1b, W, tag + "_dwmixout")
        dhn = _mixer_bwd(kind, slot, S["hn"], do, W, S, tabs, tag, G)
        dh, G["g_mix"][i] = rmsnorm_bwd(S["h0"], W["g_mix"][i], dhn, tag + "_dmix_norm", [F32], dres=dh1)
    return loss, dh, G


def _pack_small(vals):
    flat = jnp.concatenate([vals[n].reshape(-1).astype(F32) for n in SMALL])
    rows = -(-flat.shape[0] // SMALL_COLS)
    rows = -(-rows // SUBLANES) * SUBLANES
    return jnp.pad(flat, (0, rows * SMALL_COLS - flat.shape[0])).reshape(rows, SMALL_COLS)


def _unpack_small(packed, like):
    flat = packed.reshape(-1)
    out, off = {}, 0
    for n in SMALL:
        size = like[n].size
        out[n] = flat[off:off + size].reshape(like[n].shape)
        off += size
    return out


def kernel(x, p, positions, g_mix, g_ffn, g_ple, w_ple_gate, w_ple_proj, w_ffn_gate, w_ffn_up, w_ffn_down, a_w_in, a_q_norm, a_k_norm, a_sink, a_w_o, b_w_in, b_q_lat_norm, b_kv_lat_norm, b_w_q_up, b_w_kv_up, b_q_norm, b_k_norm, b_w_o, c_w_in, c_q_norm, c_k_norm, c_w_o, loss_target, m_g_mix, m_g_ffn, m_g_ple, m_w_ple_gate, m_w_ple_proj, m_w_ffn_gate, m_w_ffn_up, m_w_ffn_down, m_a_w_in, m_a_q_norm, m_a_k_norm, m_a_sink, m_a_w_o, m_b_w_in, m_b_q_lat_norm, m_b_kv_lat_norm, m_b_w_q_up, m_b_w_kv_up, m_b_q_norm, m_b_k_norm, m_b_w_o, m_c_w_in, m_c_q_norm, m_c_k_norm, m_c_w_o, v_g_mix, v_g_ffn, v_g_ple, v_w_ple_gate, v_w_ple_proj, v_w_ffn_gate, v_w_ffn_up, v_w_ffn_down, v_a_w_in, v_a_q_norm, v_a_k_norm, v_a_sink, v_a_w_o, v_b_w_in, v_b_q_lat_norm, v_b_kv_lat_norm, v_b_w_q_up, v_b_w_kv_up, v_b_q_norm, v_b_k_norm, v_b_w_o, v_c_w_in, v_c_q_norm, v_c_k_norm, v_c_w_o):
    args = dict(locals())
    w_loc = {n: args[n] for n in WEIGHTS}
    m_loc = {n: args["m_" + n] for n in WEIGHTS}
    v_loc = {n: args["v_" + n] for n in WEIGHTS}

    def halves(a):
        rows = a.shape[0] * a.shape[1]
        return a.reshape(2, rows // 2, a.shape[2])

    gathered = gather_weights([halves(w_loc[n].astype(BF)) for n in BIG])
    W = {n: _full_weight(n, g, w_loc[n].shape) for n, g in zip(BIG, gathered)}
    for n in SMALL:
        W[n] = w_loc[n]

    loss, dx, G = local_step(x[0], p[:, 0], positions[0], loss_target[0], W)
    loss = lax.psum(loss, ("x", "y", "c"))

    core = _core_index()
    landed, ready = [], None
    for n in BIG:
        s = _grad_slots(n, jnp.stack(G[n])) if isinstance(G[n], list) else G[n]
        g2 = s.reshape(2, 4 * s.shape[2], s.shape[3])
        if ready is None:
            part = presum_core_halves(g2, core, "presum_" + n)
        else:
            part, got = presum_core_halves(g2, core, "presum_" + n, ship=ready)
            landed.append(got)
        ready = part.reshape(s.shape[1:])
    landed += scatter_partials([ready])
    halves = [sum_and_swap(a, "sum_" + n) for n, a in zip(BIG, landed)]

    small = gather_small(_pack_small({n: jnp.stack(G[n]) for n in SMALL}))
    small_sum = sum_slots(small, F32, "sum_small")
    grads = _unpack_small(small_sum, w_loc)

    delta, new_m, new_v = {}, {}, {}
    for n, (mine, theirs) in zip(BIG, halves):
        shape = w_loc[n].shape
        two_d = (shape[0] * shape[1], shape[2])
        g, d, m, v = adamw_halves(w_loc[n].reshape(two_d), mine, theirs, m_loc[n].reshape(two_d),
                                  v_loc[n].reshape(two_d), core, "adamw_" + n)
        grads[n], delta[n], new_m[n], new_v[n] = g.reshape(shape), d.reshape(shape), m.reshape(shape), v.reshape(shape)
    d, m, v = adamw(_pack_small(w_loc), small_sum, _pack_small(m_loc), _pack_small(v_loc), "adamw_small")
    delta.update(_unpack_small(d, w_loc))
    new_m.update(_unpack_small(m, w_loc))
    new_v.update(_unpack_small(v, w_loc))

    return (loss, dx[None], *[grads[n] for n in WEIGHTS], *[delta[n] for n in WEIGHTS],
            *[new_m[n] for n in WEIGHTS], *[new_v[n] for n in WEIGHTS])
```

```python
import functools
import math

import numpy as np
import jax
import jax.numpy as jnp
from jax import lax
from jax.experimental import pallas as pl
from jax.experimental.pallas import tpu as pltpu

F32 = jnp.float32
BF = jnp.bfloat16

D_MODEL = 2048
DEPTH = 4
HEAD_DIM = 128
ROPE_THETA = 500000.0
PARTIAL_ROT = HEAD_DIM // 4
NORM_EPS = 1e-6
NEG = -1e30
A_HEADS = 16
A_KV_HEADS = 4
A_HALF_WINDOW = 128
B_HEADS = 16
B_Q_RANK = 512
B_KV_RANK = 512
B_NOPE = 128
B_ROPE = 64
B_QK = B_NOPE + B_ROPE
B_PAD = 256
C_PATTERNS = ((128, 1), (512, 4), (2048, 16))
C_HEADS = 16
C_GROUPS = 3
ADAM_LR = 0.001
ADAM_B1 = 0.9
ADAM_B2 = 0.999
ADAM_EPS = 1e-08
ADAM_WD = 0.01
ADAM_STEP = 10

LANES = 128
SUBLANES = 8
VMEM_LIMIT_BYTES = 56 * 1024 * 1024
MATMUL_VMEM_BYTES = 46 * 1024 * 1024
MIN_M_TILE = 512
SINGLE_STEP_MAX_K = 2048
BAND_BLOCK = 256
BAND_HEADS_PER_STEP = 4
DENSE_BLOCK = 1024
DENSE_OTHER_BLOCK = 8192
DENSE_SUB = 256
MESH = pl.DeviceIdType.MESH

BIG = ("w_ple_gate", "w_ple_proj", "w_ffn_gate", "w_ffn_up", "w_ffn_down", "a_w_in", "a_w_o",
       "b_w_in", "b_w_q_up", "b_w_kv_up", "b_w_o", "c_w_in", "c_w_o")
SHARD_AXIS = {"w_ple_gate": 1, "w_ple_proj": 2, "w_ffn_gate": 2, "w_ffn_up": 2, "w_ffn_down": 1,
              "a_w_in": 2, "a_w_o": 1, "b_w_in": 1, "b_w_q_up": 2, "b_w_kv_up": 2, "b_w_o": 1,
              "c_w_in": 2, "c_w_o": 1}
SMALL = ("g_mix", "g_ffn", "g_ple", "a_q_norm", "a_k_norm", "a_sink", "b_q_lat_norm",
         "b_kv_lat_norm", "b_q_norm", "b_k_norm", "c_q_norm", "c_k_norm")
WEIGHTS = ("g_mix", "g_ffn", "g_ple", "w_ple_gate", "w_ple_proj", "w_ffn_gate", "w_ffn_up",
           "w_ffn_down", "a_w_in", "a_q_norm", "a_k_norm", "a_sink", "a_w_o", "b_w_in",
           "b_q_lat_norm", "b_kv_lat_norm", "b_w_q_up", "b_w_kv_up", "b_q_norm", "b_k_norm",
           "b_w_o", "c_w_in", "c_q_norm", "c_k_norm", "c_w_o")
SMALL_COLS = 1024
EXCHANGE_ORDER = ("c_w_in", "w_ffn_gate", "w_ffn_up", "w_ffn_down", "w_ple_gate", "a_w_in", "a_w_o", "b_w_o",
                  "c_w_o", "b_w_in", "w_ple_proj", "b_w_kv_up", "b_w_q_up")
SLOT_DIRECT = ("w_ple_gate", "w_ple_proj", "w_ffn_gate", "w_ffn_up", "w_ffn_down")


def _params(semantics):
    return pltpu.CompilerParams(dimension_semantics=semantics, vmem_limit_bytes=VMEM_LIMIT_BYTES)


def _tile(dim, cands=(1024, 1408, 512, 256, 128)):
    for c in cands:
        if dim % c == 0:
            return c
    return dim


def _k_tile(K, bytes_per_k, fixed_bytes):
    for t in (4096, 2816, 2048, 1408, 1024, 512, 256, 128):
        if K % t == 0 and 2 * bytes_per_k * t + fixed_bytes <= MATMUL_VMEM_BYTES:
            return t
    return _tile(K, (128,))


def _row_tile(rows, cols, target_elems=1 << 19):
    best = None
    for t in range(16, rows + 1, 16):
        if rows % t == 0 and t * cols <= target_elems:
            best = t
    return best if best is not None else rows


def _sigmoid(x):
    return 1.0 / (1.0 + jnp.exp(-x))


class Slot:
    def __init__(self, name, layers, layer, rows, cols, col0=0, buf=None):
        self.axis, self.layers, self.layer, self.rows, self.cols, self.col0, self.buf = (
            SHARD_AXIS[name], layers, layer, rows, cols, col0, buf)
        self.srows = rows // 4 if self.axis == 1 else rows
        self.scols = cols if self.axis == 1 else cols // 4
        self.half = layers * self.srows // 2

    def tiles(self, ncols):
        tm = _tile(math.gcd(self.srows, self.half))
        tn = _tile(math.gcd(self.scols, math.gcd(self.col0, ncols)))
        return tm, tn

    def spec(self, tm, tn):
        def index(i, j, k):
            row, col = i * tm, self.col0 + j * tn
            chip = row // self.srows if self.axis == 1 else col // self.scols
            flat = self.layer * self.srows + (row % self.srows if self.axis == 1 else row)
            cb = col // tn if self.axis == 1 else (col % self.scols) // tn
            return flat // self.half, chip, (flat % self.half) // tm, cb

        return pl.BlockSpec((None, None, tm, tn), index)

    def shape(self):
        return jax.ShapeDtypeStruct((2, 4, self.half, self.scols), F32)


def matmul(pairs, mode, out_dtype, name, res=None, swiglu=None, ple=None, slot=None):
    a0, b0 = pairs[0]
    if mode == "nn":
        (M, K), N = a0.shape, b0.shape[1]
    elif mode == "nt":
        (M, K), N = a0.shape, b0.shape[0]
    else:
        (K, M), N = a0.shape, b0.shape[1]
    tm, tn = (_tile(M), _tile(N)) if slot is None else slot.tiles(N)
    n_mn = 2 + (0 if res is None else 2) + (0 if swiglu is None else 2) + (0 if ple is None else 4)

    def k_tile(rows):
        return _k_tile(K, sum(rows * a.dtype.itemsize + tn * b.dtype.itemsize for a, b in pairs),
                       4 * rows * tn * (1 + n_mn))

    tk = k_tile(tm)
    if (slot is None and tk < K <= SINGLE_STEP_MAX_K and tm > MIN_M_TILE and M % MIN_M_TILE == 0
            and k_tile(MIN_M_TILE) == K):
        tm, tk = MIN_M_TILE, K
    nk = K // tk
    if mode == "nn":
        a_spec = pl.BlockSpec((tm, tk), lambda i, j, k: (i, k))
        b_spec = pl.BlockSpec((tk, tn), lambda i, j, k: (k, j))
        dims = (((1,), (0,)), ((), ()))
    elif mode == "nt":
        a_spec = pl.BlockSpec((tm, tk), lambda i, j, k: (i, k))
        b_spec = pl.BlockSpec((tn, tk), lambda i, j, k: (j, k))
        dims = (((1,), (1,)), ((), ()))
    else:
        a_spec = pl.BlockSpec((tk, tm), lambda i, j, k: (k, i))
        b_spec = pl.BlockSpec((tk, tn), lambda i, j, k: (k, j))
        dims = (((0,), (0,)), ((), ()))
    mn_spec = pl.BlockSpec((tm, tn), lambda i, j, k: (i, j))
    npairs = len(pairs)
    extras = [] if res is None else [res]
    if swiglu is not None:
        extras = list(swiglu)
    if ple is not None:
        extras = list(ple)
    nex = len(extras)
    nout = 2 if (swiglu is not None or ple is not None) else 1
    carried = slot is not None and slot.buf is not None

    def body(*refs):
        ins = refs[:2 * npairs]
        ex = refs[2 * npairs:2 * npairs + nex]
        first_out = 2 * npairs + nex + (1 if carried else 0)
        outs = refs[first_out:first_out + nout]
        k = pl.program_id(2)

        def product():
            part = None
            for p in range(npairs):
                d = lax.dot_general(ins[2 * p][...].astype(BF), ins[2 * p + 1][...].astype(BF), dims,
                                    preferred_element_type=F32)
                part = d if part is None else part + d
            return part

        def finish(r):
            if swiglu is not None:
                a = ex[0][...].astype(F32)
                b = ex[1][...].astype(F32)
                sg = _sigmoid(a)
                outs[0][...] = (r * b * (sg * (1.0 + a * (1.0 - sg)))).astype(out_dtype)
                outs[1][...] = (r * (a * sg)).astype(out_dtype)
            elif ple is not None:
                outs[0][...] = r.astype(out_dtype)
                outs[1][...] = ex[0][...] + _sigmoid(r) * ex[1][...].astype(F32)
            elif res is not None:
                outs[0][...] = (ex[0][...] + r).astype(out_dtype)
            else:
                outs[0][...] = r.astype(outs[0].dtype)

        if nk == 1:
            finish(product())
        else:
            acc = refs[-1]

            @pl.when(k == 0)
            def _():
                acc[...] = jnp.zeros_like(acc)

            acc[...] += product()

            @pl.when(k == nk - 1)
            def _():
                finish(acc[...])

    in_specs = []
    operands = []
    for a, b in pairs:
        in_specs += [a_spec, b_spec]
        operands += [a, b]
    in_specs += [mn_spec] * nex
    operands += extras
    out_shape = [jax.ShapeDtypeStruct((M, N), out_dtype)] * nout
    out_specs = [mn_spec] * nout
    aliases = {}
    if ple is not None:
        out_shape[1] = jax.ShapeDtypeStruct((M, N), F32)
    if slot is not None:
        out_shape, out_specs = [slot.shape()], [slot.spec(tm, tn)]
        if carried:
            aliases = {len(operands): 0}
            in_specs.append(pl.BlockSpec(memory_space=pl.ANY))
            operands.append(slot.buf)
    outs = pl.pallas_call(
        body, out_shape=out_shape, grid=(M // tm, N // tn, nk), in_specs=in_specs,
        out_specs=out_specs, scratch_shapes=[pltpu.VMEM((tm, tn), F32)] if nk > 1 else [],
        input_output_aliases=aliases, compiler_params=_params(("parallel", "parallel", "arbitrary")), name=name,
    )(*operands)
    return outs if nout > 1 else outs[0]


def matmul_swiglu(x, wg, wu, name):
    (M, K), N = x.shape, wg.shape[1]
    tm, tn = _tile(M), _tile(N)

    def k_tile(rows):
        return _k_tile(K, rows * x.dtype.itemsize + 2 * tn * wg.dtype.itemsize, 4 * rows * tn * (2 + 3))

    tk = k_tile(tm)
    if tk < K <= SINGLE_STEP_MAX_K and tm > MIN_M_TILE and M % MIN_M_TILE == 0 and k_tile(MIN_M_TILE) == K:
        tm, tk = MIN_M_TILE, K
    nk = K // tk

    def body(x_ref, g_ref, u_ref, a_ref, b_ref, c_ref, *accs):
        k = pl.program_id(2)
        xv = x_ref[...].astype(BF)

        def products():
            return (jnp.dot(xv, g_ref[...].astype(BF), preferred_element_type=F32),
                    jnp.dot(xv, u_ref[...].astype(BF), preferred_element_type=F32))

        def finish(a, b):
            a_ref[...] = a.astype(a_ref.dtype)
            b_ref[...] = b.astype(b_ref.dtype)
            c_ref[...] = (a * _sigmoid(a) * b).astype(c_ref.dtype)

        if nk == 1:
            finish(*products())
        else:
            acc_g, acc_u = accs

            @pl.when(k == 0)
            def _():
                acc_g[...] = jnp.zeros_like(acc_g)
                acc_u[...] = jnp.zeros_like(acc_u)

            pg, pu = products()
            acc_g[...] += pg
            acc_u[...] += pu

            @pl.when(k == nk - 1)
            def _():
                finish(acc_g[...], acc_u[...])

    w_spec = pl.BlockSpec((tk, tn), lambda i, j, k: (k, j))
    mn_spec = pl.BlockSpec((tm, tn), lambda i, j, k: (i, j))
    return pl.pallas_call(
        body, out_shape=[jax.ShapeDtypeStruct((M, N), BF)] * 3, grid=(M // tm, N // tn, nk),
        in_specs=[pl.BlockSpec((tm, tk), lambda i, j, k: (i, k)), w_spec, w_spec], out_specs=[mn_spec] * 3,
        scratch_shapes=[pltpu.VMEM((tm, tn), F32)] * 2 if nk > 1 else [],
        compiler_params=_params(("parallel", "parallel", "arbitrary")), name=name,
    )(x, wg, wu)


def rowwise(fn, ins, out_dtypes, name):
    rows, cols = ins[0].shape
    tr = _row_tile(rows, cols)
    nin = len(ins)

    def body(*refs):
        vals = fn(*[r[...] for r in refs[:nin]])
        for o, v in zip(refs[nin:], vals):
            o[...] = v.astype(o.dtype)

    spec = pl.BlockSpec((tr, cols), lambda i: (i, 0))
    outs = pl.pallas_call(
        body, out_shape=[jax.ShapeDtypeStruct((rows, cols), d) for d in out_dtypes],
        grid=(rows // tr,), in_specs=[spec] * nin, out_specs=[spec] * len(out_dtypes),
        compiler_params=_params(("parallel",)), name=name,
    )(*ins)
    return outs


def _ple_bwd_fn(dh, z, pp):
    gate = _sigmoid(z.astype(F32))
    return (dh * pp.astype(F32) * gate * (1.0 - gate), dh * gate)


def _merge_fn(o0, o1, o2, l0, l1, l2):
    m = jnp.maximum(jnp.maximum(l0, l1), l2)
    e0, e1, e2 = jnp.exp(l0 - m), jnp.exp(l1 - m), jnp.exp(l2 - m)
    den = e0 + e1 + e2
    return ((e0 * o0 + e1 * o1 + e2 * o2) / den, m + jnp.log(den))


def rmsnorm_fwd(x, g, name, col_block=0, width=None):
    T = x.shape[0]
    W = x.shape[1] if width is None else width
    tt = _row_tile(T, W)

    def body(x_ref, g_ref, y_ref):
        xf = x_ref[...].astype(F32)
        ms = jnp.mean(xf * xf, axis=-1, keepdims=True)
        y_ref[...] = (xf * lax.rsqrt(ms + NORM_EPS) * g_ref[...]).astype(y_ref.dtype)

    return pl.pallas_call(
        body, out_shape=jax.ShapeDtypeStruct((T, W), BF), grid=(T // tt,),
        in_specs=[pl.BlockSpec((tt, W), lambda i: (i, col_block)), pl.BlockSpec((1, W), lambda i: (0, 0))],
        out_specs=pl.BlockSpec((tt, W), lambda i: (i, 0)),
        compiler_params=_params(("parallel",)), name=name,
    )(x, g.reshape(1, W).astype(F32))


def rmsnorm_bwd(x, g, dy, name, out_dtypes, dres=None, col_block=0, width=None):
    T = x.shape[0]
    W = x.shape[1] if width is None else width
    tt = _row_tile(T, W, 1 << 18)
    nout = len(out_dtypes)
    has_res = dres is not None

    def body(*refs):
        x_ref, g_ref, dy_ref = refs[:3]
        pos = 3
        res_ref = None
        if has_res:
            res_ref = refs[3]
            pos = 4
        dx_refs = refs[pos:pos + nout]
        dg_ref = refs[pos + nout]
        xf = x_ref[...].astype(F32)
        rstd = lax.rsqrt(jnp.mean(xf * xf, axis=-1, keepdims=True) + NORM_EPS)
        xhat = xf * rstd
        dyf = dy_ref[...].astype(F32)
        dn = dyf * g_ref[...]
        dx = rstd * (dn - xhat * jnp.mean(dn * xhat, axis=-1, keepdims=True))
        if has_res:
            dx = dx + res_ref[...]
        for o in dx_refs:
            o[...] = dx.astype(o.dtype)

        @pl.when(pl.program_id(0) == 0)
        def _():
            dg_ref[...] = jnp.zeros_like(dg_ref)

        dg_ref[...] += jnp.broadcast_to(jnp.sum(dyf * xhat, axis=0, keepdims=True), dg_ref.shape)

    row = pl.BlockSpec((tt, W), lambda i: (i, 0))
    in_specs = [pl.BlockSpec((tt, W), lambda i: (i, col_block)), pl.BlockSpec((1, W), lambda i: (0, 0)), row]
    operands = [x, g.reshape(1, W).astype(F32), dy]
    if has_res:
        in_specs.append(row)
        operands.append(dres)
    outs = pl.pallas_call(
        body,
        out_shape=[jax.ShapeDtypeStruct((T, W), d) for d in out_dtypes] + [jax.ShapeDtypeStruct((SUBLANES, W), F32)],
        grid=(T // tt,), in_specs=in_specs,
        out_specs=[row] * nout + [pl.BlockSpec((SUBLANES, W), lambda i: (0, 0))],
        compiler_params=_params(("arbitrary",)), name=name,
    )(*operands)
    return tuple(outs[:nout]) + (outs[nout][0],)


def loss_and_grad(y, target, name):
    T, D = y.shape
    tt = _row_tile(T, D)

    def body(y_ref, t_ref, loss_ref, dy_ref):
        d = y_ref[...] - t_ref[...]
        dy_ref[...] = d * (1.0 / D)

        @pl.when(pl.program_id(0) == 0)
        def _():
            loss_ref[...] = jnp.zeros_like(loss_ref)

        loss_ref[...] += jnp.full(loss_ref.shape, 0.5 / D, F32) * jnp.sum(d * d)

    row = pl.BlockSpec((tt, D), lambda i: (i, 0))
    loss, dy = pl.pallas_call(
        body, out_shape=[jax.ShapeDtypeStruct((SUBLANES, LANES), F32), jax.ShapeDtypeStruct((T, D), F32)],
        grid=(T // tt,), in_specs=[row, row],
        out_specs=[pl.BlockSpec((SUBLANES, LANES), lambda i: (0, 0)), row],
        compiler_params=_params(("arbitrary",)), name=name,
    )(y, target)
    return loss[0, 0], dy


def rope_tables(pos, width, r0, rot_dim):
    half = rot_dim // 2
    inv = ROPE_THETA ** (-jnp.arange(half, dtype=F32) * 2.0 / rot_dim)
    ang = pos.astype(F32)[:, None] * inv
    cos, sin = jnp.cos(ang), jnp.sin(ang)
    T = pos.shape[0]
    ones_l, ones_r = jnp.ones((T, r0), F32), jnp.ones((T, width - r0 - rot_dim), F32)
    c_tab = jnp.concatenate([ones_l, cos, cos, ones_r], axis=1)
    s_tab = jnp.concatenate([0 * ones_l, -sin, sin, 0 * ones_r], axis=1)
    perm = np.zeros((width, width), np.float32)
    for j in range(half):
        perm[r0 + j + half, r0 + j] = 1.0
        perm[r0 + j, r0 + j + half] = 1.0
    return c_tab, s_tab, jnp.asarray(perm, BF)


def _lane_permute(v, perm):
    hi = v.astype(BF)
    lo = (v - hi.astype(F32)).astype(BF)
    return (jnp.dot(hi, perm, preferred_element_type=F32) + jnp.dot(lo, perm, preferred_element_type=F32))


def headnorm_fwd(x, g, tabs, name, heads, col0, width, n_true):
    c_tab, s_tab, perm = tabs
    T = x.shape[0]
    tt = _tile(T, (1024, 512, 256, 128))
    inv_n = 1.0 / n_true

    def body(x_ref, g_ref, c_ref, s_ref, p_ref, y_ref):
        xf = x_ref[...].astype(F32)
        rstd = lax.rsqrt(jnp.sum(xf * xf, axis=-1, keepdims=True) * inv_n + NORM_EPS)
        n = xf * rstd * g_ref[...]
        y_ref[...] = (n * c_ref[...] + _lane_permute(n, p_ref[...]) * s_ref[...]).astype(y_ref.dtype)

    tab = pl.BlockSpec((tt, width), lambda i, h: (i, 0))
    return pl.pallas_call(
        body, out_shape=jax.ShapeDtypeStruct((T, heads * width), BF), grid=(T // tt, heads),
        in_specs=[pl.BlockSpec((tt, width), lambda i, h: (i, col0 + h)),
                  pl.BlockSpec((1, width), lambda i, h: (0, 0)), tab, tab,
                  pl.BlockSpec((width, width), lambda i, h: (0, 0))],
        out_specs=pl.BlockSpec((tt, width), lambda i, h: (i, h)),
        compiler_params=_params(("parallel", "parallel")), name=name,
    )(x, g.reshape(1, width).astype(F32), c_tab, s_tab, perm)


def headnorm_bwd(x, g, tabs, dy, name, heads, col0, width, n_true, head_sum=False, into=None):
    c_tab, s_tab, perm = tabs
    T = x.shape[0]
    tt = _tile(T, (1024, 512, 256, 128))
    inv_n = 1.0 / n_true
    buf, blocks, block0 = into if into is not None else (None, heads, 0)
    carried = buf is not None

    def body(*refs):
        x_ref, g_ref, c_ref, s_ref, p_ref, dy_ref = refs[:6]
        dx_ref, dg_ref = refs[7:9] if carried else refs[6:8]
        i, h = pl.program_id(0), pl.program_id(1)
        xf = x_ref[...].astype(F32)
        rstd = lax.rsqrt(jnp.sum(xf * xf, axis=-1, keepdims=True) * inv_n + NORM_EPS)
        xhat = xf * rstd
        dyf = dy_ref[...].astype(F32)
        dn = dyf * c_ref[...] + _lane_permute(dyf * s_ref[...], p_ref[...])
        dxh = dn * g_ref[...]
        dx = rstd * (dxh - xhat * (jnp.sum(dxh * xhat, axis=-1, keepdims=True) * inv_n))
        dx_ref[...] = dx.astype(dx_ref.dtype)

        @pl.when(jnp.logical_and(i == 0, h == 0))
        def _():
            dg_ref[...] = jnp.zeros_like(dg_ref)

        dg_ref[...] += jnp.broadcast_to(jnp.sum(dn * xhat, axis=0, keepdims=True), dg_ref.shape)
        if head_sum:
            sum_ref = refs[-1]

            @pl.when(h == 0)
            def _():
                sum_ref[...] = jnp.zeros_like(sum_ref)

            sum_ref[...] += dx

    tab = pl.BlockSpec((tt, width), lambda i, h: (i, 0))
    out_shape = [jax.ShapeDtypeStruct((T, blocks * width), BF), jax.ShapeDtypeStruct((SUBLANES, width), F32)]
    out_specs = [pl.BlockSpec((tt, width), lambda i, h: (i, block0 + h)),
                 pl.BlockSpec((SUBLANES, width), lambda i, h: (0, 0))]
    if head_sum:
        out_shape.append(jax.ShapeDtypeStruct((T, width), F32))
        out_specs.append(tab)
    in_specs = [pl.BlockSpec((tt, width), lambda i, h: (i, col0 + h)),
                pl.BlockSpec((1, width), lambda i, h: (0, 0)), tab, tab,
                pl.BlockSpec((width, width), lambda i, h: (0, 0)),
                pl.BlockSpec((tt, width), lambda i, h: (i, h))]
    operands = [x, g.reshape(1, width).astype(F32), c_tab, s_tab, perm, dy]
    if carried:
        in_specs.append(pl.BlockSpec(memory_space=pl.ANY))
        operands.append(buf)
    outs = pl.pallas_call(
        body, out_shape=out_shape, grid=(T // tt, heads), in_specs=in_specs, out_specs=out_specs,
        input_output_aliases={6: 0} if carried else {},
        compiler_params=_params(("arbitrary", "arbitrary")), name=name,
    )(*operands)
    return (outs[0], outs[1][0]) + ((outs[2],) if head_sum else ())


class Attn:
    def __init__(self, T, dil, hq, group, qc, q0, kc, k0, vc, v0, vstride, dqk, scale, half_window, blk, oblk=None):
        self.T, self.dil, self.hq, self.group = T, dil, hq, group
        self.hkv = hq // group
        self.qc, self.q0, self.kc, self.k0, self.vc, self.v0, self.vstride = qc, q0, kc, k0, vc, v0, vstride
        self.dqk, self.scale, self.hw = dqk, scale, half_window
        self.len = T // dil
        self.blk = min(blk, self.len)
        self.nb = self.len // self.blk
        self.band = half_window is not None
        self.oblk = self.blk if self.band or oblk is None else min(oblk, self.len)
        self.steps = 3 if self.band else self.len // self.oblk

    def other(self, i, s):
        if self.band:
            nom = i - 1 + s
            return jnp.minimum(jnp.maximum(nom, 0), self.nb - 1), nom
        return s, s

    def chains(self, a):
        return a.reshape(self.len, self.dil * a.shape[1])

    def row_chunks(self, rows):
        assert not self.band
        sub = min(DENSE_SUB, rows)
        return [slice(c * sub, (c + 1) * sub) for c in range(rows // sub)]

    def unchain(self, a, cols):
        return a.reshape(self.T, cols)

    def mask(self, q_nom, k_nom):
        if not self.band:
            return None
        qpos = q_nom * self.blk + lax.broadcasted_iota(jnp.int32, (self.blk, self.blk), 0)
        kpos = k_nom * self.blk + lax.broadcasted_iota(jnp.int32, (self.blk, self.blk), 1)
        ok = jnp.abs(qpos - kpos) <= self.hw
        for pos in (qpos, kpos):
            ok = jnp.logical_and(ok, jnp.logical_and(pos >= 0, pos < self.len))
        return ok


def _scores(cfg, q, k, q_nom, k_nom):
    s = lax.dot_general(q, k, (((1,), (1,)), ((), ())), preferred_element_type=F32) * cfg.scale
    ok = cfg.mask(q_nom, k_nom)
    return s if ok is None else jnp.where(ok, s, NEG)


def flash_fwd(cfg, q, k, v, name, out_dtype, sink=None):
    blk, dqk = cfg.blk, cfg.dqk
    has_sink = sink is not None

    def body(*refs):
        if has_sink:
            sink_ref, refs = refs[0], refs[1:]
        q_ref, k_ref, v_ref, o_ref, lse_ref, m_sc, l_sc, acc_sc = refs
        i, s = pl.program_id(2), pl.program_id(3)

        @pl.when(s == 0)
        def _():
            if has_sink:
                m_sc[...] = jnp.broadcast_to(sink_ref[0, :1, :], m_sc.shape)
                l_sc[...] = jnp.ones_like(l_sc)
            else:
                m_sc[...] = jnp.full(m_sc.shape, NEG, F32)
                l_sc[...] = jnp.zeros_like(l_sc)
            acc_sc[...] = jnp.zeros_like(acc_sc)

        _, k_nom = cfg.other(i, s)
        k, v = k_ref[...], v_ref[...]
        for rows in cfg.row_chunks(blk):
            sc = _scores(cfg, q_ref[rows, :], k, i, k_nom)
            m_prev = m_sc[rows, :]
            m_new = jnp.maximum(m_prev, jnp.max(sc, axis=-1, keepdims=True))
            p = jnp.exp(sc - m_new[:, :1])
            alpha = jnp.exp(m_prev - m_new)
            l_sc[rows, :] = alpha * l_sc[rows, :] + jnp.sum(p, axis=-1, keepdims=True)
            acc_sc[rows, :] = alpha * acc_sc[rows, :] + jnp.dot(p.astype(BF), v, preferred_element_type=F32)
            m_sc[rows, :] = m_new

        @pl.when(s == cfg.steps - 1)
        def _():
            o_ref[...] = (acc_sc[...] / l_sc[...]).astype(o_ref.dtype)
            lse_ref[...] = m_sc[...] + jnp.log(l_sc[...])

    g = cfg.group
    q_spec = pl.BlockSpec((blk, dqk), lambda r, h, i, s: (i, r * cfg.qc + cfg.q0 + h))
    k_spec = pl.BlockSpec((cfg.oblk, dqk), lambda r, h, i, s: (cfg.other(i, s)[0], r * cfg.kc + cfg.k0 + h // g))
    v_spec = pl.BlockSpec((cfg.oblk, LANES),
                          lambda r, h, i, s: (cfg.other(i, s)[0], r * cfg.vc + cfg.v0 + cfg.vstride * (h // g)))
    o_spec = pl.BlockSpec((blk, LANES), lambda r, h, i, s: (i, r * cfg.hq + h))
    in_specs = [q_spec, k_spec, v_spec]
    operands = [cfg.chains(q), cfg.chains(k), cfg.chains(v)]
    if has_sink:
        in_specs.insert(0, pl.BlockSpec((1, SUBLANES, LANES), lambda r, h, i, s: (h, 0, 0)))
        operands.insert(0, sink)
    cols = cfg.dil * cfg.hq * LANES
    o, lse = pl.pallas_call(
        body, out_shape=[jax.ShapeDtypeStruct((cfg.len, cols), out_dtype), jax.ShapeDtypeStruct((cfg.len, cols), F32)],
        grid=(cfg.dil, cfg.hq, cfg.nb, cfg.steps), in_specs=in_specs, out_specs=[o_spec, o_spec],
        scratch_shapes=[pltpu.VMEM((blk, LANES), F32)] * 3,
        compiler_params=_params(("parallel", "parallel", "parallel", "arbitrary")), name=name,
    )(*operands)
    return cfg.unchain(o, cfg.hq * LANES), cfg.unchain(lse, cfg.hq * LANES)


def flash_dq(cfg, q, k, v, do, o, lse, name, sink=None):
    blk, dqk = cfg.blk, cfg.dqk
    has_sink = sink is not None

    def body(*refs):
        if has_sink:
            sink_ref, refs = refs[0], refs[1:]
        q_ref, k_ref, v_ref, do_ref, o_ref, lse_ref = refs[:6]
        dq_ref = refs[6]
        dq_sc, delta_sc = refs[-2:]
        i, s = pl.program_id(2), pl.program_id(3)

        @pl.when(s == 0)
        def _():
            dq_sc[...] = jnp.zeros_like(dq_sc)
            delta = jnp.sum(do_ref[...].astype(F32) * o_ref[...].astype(F32), axis=-1, keepdims=True)
            delta_sc[...] = jnp.broadcast_to(delta, delta_sc.shape)

        _, k_nom = cfg.other(i, s)
        k, v = k_ref[...], v_ref[...]
        for rows in cfg.row_chunks(blk):
            sc = _scores(cfg, q_ref[rows, :], k, i, k_nom)
            p = jnp.exp(sc - lse_ref[rows, :1])
            dp = lax.dot_general(do_ref[rows, :], v, (((1,), (1,)), ((), ())), preferred_element_type=F32)
            ds = p * (dp - delta_sc[rows, :1]) * cfg.scale
            dq_sc[rows, :] += jnp.dot(ds.astype(BF), k, preferred_element_type=F32)

        @pl.when(s == cfg.steps - 1)
        def _():
            dq_ref[...] = dq_sc[...].astype(dq_ref.dtype)
            if has_sink:
                ps = jnp.exp(sink_ref[0, :1, :] - lse_ref[...])
                part = -jnp.sum(ps * delta_sc[...], axis=0, keepdims=True)
                refs[7][...] = jnp.broadcast_to(part, refs[7].shape)

    g = cfg.group
    q_spec = pl.BlockSpec((blk, dqk), lambda r, h, i, s: (i, r * cfg.qc + cfg.q0 + h))
    k_spec = pl.BlockSpec((cfg.oblk, dqk), lambda r, h, i, s: (cfg.other(i, s)[0], r * cfg.kc + cfg.k0 + h // g))
    v_spec = pl.BlockSpec((cfg.oblk, LANES),
                          lambda r, h, i, s: (cfg.other(i, s)[0], r * cfg.vc + cfg.v0 + cfg.vstride * (h // g)))
    o_spec = pl.BlockSpec((blk, LANES), lambda r, h, i, s: (i, r * cfg.hq + h))
    dq_spec = pl.BlockSpec((blk, dqk), lambda r, h, i, s: (i, r * cfg.hq + h))
    in_specs = [q_spec, k_spec, v_spec, o_spec, o_spec, o_spec]
    operands = [cfg.chains(q), cfg.chains(k), cfg.chains(v), cfg.chains(do), cfg.chains(o), cfg.chains(lse)]
    out_shape = [jax.ShapeDtypeStruct((cfg.len, cfg.dil * cfg.hq * dqk), BF)]
    out_specs = [dq_spec]
    if has_sink:
        in_specs.insert(0, pl.BlockSpec((1, SUBLANES, LANES), lambda r, h, i, s: (h, 0, 0)))
        operands.insert(0, sink)
        out_shape.append(jax.ShapeDtypeStruct((cfg.hq * cfg.nb * SUBLANES, LANES), F32))
        out_specs.append(pl.BlockSpec((SUBLANES, LANES), lambda r, h, i, s: (h * cfg.nb + i, 0)))
    outs = pl.pallas_call(
        body, out_shape=out_shape, grid=(cfg.dil, cfg.hq, cfg.nb, cfg.steps), in_specs=in_specs,
        out_specs=out_specs, scratch_shapes=[pltpu.VMEM((blk, dqk), F32), pltpu.VMEM((blk, LANES), F32)],
        compiler_params=_params(("parallel", "parallel", "parallel", "arbitrary")), name=name,
    )(*operands)
    dq = cfg.unchain(outs[0], cfg.hq * dqk)
    if has_sink:
        return dq, outs[1].reshape(cfg.hq, cfg.nb, SUBLANES, LANES)[:, :, 0, :]
    return dq


def flash_dkv(cfg, q, k, v, do, o, lse, name, out_dtype, add=None):
    blk, dqk, g, nw = cfg.blk, cfg.dqk, cfg.group, cfg.steps
    has_add = add is not None

    def body(*refs):
        k_ref, v_ref, q_ref, do_ref, o_ref, lse_ref = refs[:6]
        pos = 8 if has_add else 6
        dk_ref, dv_ref = refs[pos:pos + 2]
        dk_sc, dv_sc = refs[-2:]
        i, j = pl.program_id(2), pl.program_id(3)

        @pl.when(j == 0)
        def _():
            dk_sc[...] = jnp.zeros_like(dk_sc)
            dv_sc[...] = jnp.zeros_like(dv_sc)

        _, q_nom = cfg.other(i, j % nw)
        q, do = q_ref[...], do_ref[...]
        lse = lse_ref[:, :1]
        delta = jnp.sum(do.astype(F32) * o_ref[...].astype(F32), axis=-1, keepdims=True)
        for rows in cfg.row_chunks(blk):
            sc = _scores(cfg, q, k_ref[rows, :], q_nom, i)
            p = jnp.exp(sc - lse)
            dv_sc[rows, :] += lax.dot_general(p.astype(BF), do, (((0,), (0,)), ((), ())), preferred_element_type=F32)
            dp = lax.dot_general(do, v_ref[rows, :], (((1,), (1,)), ((), ())), preferred_element_type=F32)
            ds = p * (dp - delta) * cfg.scale
            dk_sc[rows, :] += lax.dot_general(ds.astype(BF), q, (((0,), (0,)), ((), ())), preferred_element_type=F32)

        @pl.when(j == g * nw - 1)
        def _():
            dk, dv = dk_sc[...], dv_sc[...]
            if has_add:
                dk, dv = dk + refs[6][...].astype(F32), dv + refs[7][...].astype(F32)
            dk_ref[...] = dk.astype(dk_ref.dtype)
            dv_ref[...] = dv.astype(dv_ref.dtype)

    def qrow(i, j):
        return cfg.other(i, j % nw)[0]

    k_spec = pl.BlockSpec((blk, dqk), lambda r, h, i, j: (i, r * cfg.kc + cfg.k0 + h))
    v_spec = pl.BlockSpec((blk, LANES), lambda r, h, i, j: (i, r * cfg.vc + cfg.v0 + cfg.vstride * h))
    q_spec = pl.BlockSpec((cfg.oblk, dqk), lambda r, h, i, j: (qrow(i, j), r * cfg.qc + cfg.q0 + h * g + j // nw))
    o_spec = pl.BlockSpec((cfg.oblk, LANES), lambda r, h, i, j: (qrow(i, j), r * cfg.hq + h * g + j // nw))
    dk_spec = pl.BlockSpec((blk, dqk), lambda r, h, i, j: (i, r * cfg.hkv + h))
    dv_spec = pl.BlockSpec((blk, LANES), lambda r, h, i, j: (i, r * cfg.hkv + h))
    in_specs = [k_spec, v_spec, q_spec, o_spec, o_spec, o_spec]
    operands = [cfg.chains(k), cfg.chains(v), cfg.chains(q), cfg.chains(do), cfg.chains(o), cfg.chains(lse)]
    if has_add:
        in_specs += [dk_spec, dv_spec]
        operands += [cfg.chains(add[0]), cfg.chains(add[1])]
    dk, dv = pl.pallas_call(
        body,
        out_shape=[jax.ShapeDtypeStruct((cfg.len, cfg.dil * cfg.hkv * dqk), out_dtype),
                   jax.ShapeDtypeStruct((cfg.len, cfg.dil * cfg.hkv * LANES), out_dtype)],
        grid=(cfg.dil, cfg.hkv, cfg.nb, g * nw), in_specs=in_specs, out_specs=[dk_spec, dv_spec],
        scratch_shapes=[pltpu.VMEM((blk, dqk), F32), pltpu.VMEM((blk, LANES), F32)],
        compiler_params=_params(("parallel", "parallel", "parallel", "arbitrary")), name=name,
    )(*operands)
    return cfg.unchain(dk, cfg.hkv * dqk), cfg.unchain(dv, cfg.hkv * LANES)


class Band:
    def __init__(self, T, dil, hq, group, per, qc, q0, kc, k0, vc, v0, scale, hw, blk):
        self.T, self.dil, self.hq, self.group, self.per = T, dil, hq, group, per
        self.pk = per // group
        self.hkv = hq // group
        self.scale, self.hw = scale, hw
        self.len = T // dil
        self.blk = min(blk, self.len)
        self.nb = self.len // self.blk
        self.win = self.blk + 2 * hw
        self.qcol = lambda r: (r * qc + q0) // per
        self.kcol = lambda r: (r * kc + k0) // self.pk
        self.vcol = lambda r: (r * vc + v0) // self.pk
        self.ocol = lambda r: (r * hq) // per
        self.dkcol = lambda r: (r * self.hkv) // self.pk
        assert hw <= self.blk and qc % per == 0 and q0 % per == 0 and kc % self.pk == 0 and k0 % self.pk == 0
        assert vc % self.pk == 0 and v0 % self.pk == 0

    def chains(self, a):
        return a.reshape(self.len, self.dil * a.shape[1])

    def rows3(self, width, col):
        nb = self.nb
        return [pl.BlockSpec((self.blk, width), lambda r, h, i: (jnp.maximum(i - 1, 0), col(r) + h)),
                pl.BlockSpec((self.blk, width), lambda r, h, i: (i, col(r) + h)),
                pl.BlockSpec((self.blk, width), lambda r, h, i: (jnp.minimum(i + 1, nb - 1), col(r) + h))]

    def window(self, prev, cur, nxt, j):
        cols = slice(j * LANES, (j + 1) * LANES)
        return jnp.concatenate([prev[self.blk - self.hw:, cols], cur[:, cols], nxt[:self.hw, cols]], axis=0)

    def valid(self, i, window_is_rows):
        shape = (self.win, self.blk) if window_is_rows else (self.blk, self.win)
        wdim = 0 if window_is_rows else 1
        bpos = i * self.blk + lax.broadcasted_iota(jnp.int32, shape, 1 - wdim)
        wpos = i * self.blk - self.hw + lax.broadcasted_iota(jnp.int32, shape, wdim)
        ok = jnp.abs(bpos - wpos) <= self.hw
        return jnp.logical_and(ok, jnp.logical_and(wpos >= 0, wpos < self.len))


def band_fwd(cfg, q, k, v, name, out_dtype, sink=None):
    blk, per, pk = cfg.blk, cfg.per, cfg.pk
    has_sink = sink is not None

    def body(*refs):
        if has_sink:
            sink_ref, refs = refs[0], refs[1:]
        q_ref, kp, kc, kn, vp, vc, vn, o_ref, lse_ref = refs
        ok = cfg.valid(pl.program_id(2), False)
        for j in range(per):
            jk = j // cfg.group
            if j % cfg.group == 0:
                kw = cfg.window(kp, kc, kn, jk)
                vw = cfg.window(vp, vc, vn, jk)
            cols = slice(j * LANES, (j + 1) * LANES)
            s = lax.dot_general(q_ref[:, cols], kw, (((1,), (1,)), ((), ())), preferred_element_type=F32) * cfg.scale
            s = jnp.where(ok, s, NEG)
            m = jnp.max(s, axis=-1, keepdims=True)
            if has_sink:
                sk = sink_ref[j, :1, :1]
                m = jnp.maximum(m, sk)
            e = jnp.exp(s - m)
            den = jnp.sum(e, axis=-1, keepdims=True)
            if has_sink:
                den = den + jnp.exp(sk - m)
            o = jnp.dot(e.astype(BF), vw, preferred_element_type=F32) / den
            o_ref[:, cols] = o.astype(o_ref.dtype)
            lse_ref[:, cols] = jnp.broadcast_to(m + jnp.log(den), (blk, LANES))

    q_spec = pl.BlockSpec((blk, per * LANES), lambda r, h, i: (i, cfg.qcol(r) + h))
    o_spec = pl.BlockSpec((blk, per * LANES), lambda r, h, i: (i, cfg.ocol(r) + h))
    in_specs = [q_spec] + cfg.rows3(pk * LANES, cfg.kcol) + cfg.rows3(pk * LANES, cfg.vcol)
    kc_, vc_ = cfg.chains(k), cfg.chains(v)
    operands = [cfg.chains(q), kc_, kc_, kc_, vc_, vc_, vc_]
    if has_sink:
        in_specs.insert(0, pl.BlockSpec((per, SUBLANES, LANES), lambda r, h, i: (h, 0, 0)))
        operands.insert(0, sink)
    cols = cfg.dil * cfg.hq * LANES
    o, lse = pl.pallas_call(
        body, out_shape=[jax.ShapeDtypeStruct((cfg.len, cols), out_dtype), jax.ShapeDtypeStruct((cfg.len, cols), F32)],
        grid=(cfg.dil, cfg.hq // per, cfg.nb), in_specs=in_specs, out_specs=[o_spec, o_spec],
        compiler_params=_params(("parallel", "parallel", "parallel")), name=name,
    )(*operands)
    return o.reshape(cfg.T, cfg.hq * LANES), lse.reshape(cfg.T, cfg.hq * LANES)


def band_dq(cfg, q, k, v, do, o, lse, name, sink=None):
    blk, per, pk = cfg.blk, cfg.per, cfg.pk
    has_sink = sink is not None

    def body(*refs):
        if has_sink:
            sink_ref, refs = refs[0], refs[1:]
        q_ref, kp, kc, kn, vp, vc, vn, do_ref, o_ref, lse_ref, dq_ref = refs[:11]
        ok = cfg.valid(pl.program_id(2), False)
        for j in range(per):
            jk = j // cfg.group
            if j % cfg.group == 0:
                kw = cfg.window(kp, kc, kn, jk)
                vw = cfg.window(vp, vc, vn, jk)
            cols = slice(j * LANES, (j + 1) * LANES)
            do = do_ref[:, cols]
            lse = lse_ref[:, j * LANES:j * LANES + 1]
            delta = jnp.sum(do.astype(F32) * o_ref[:, cols].astype(F32), axis=-1, keepdims=True)
            s = lax.dot_general(q_ref[:, cols], kw, (((1,), (1,)), ((), ())), preferred_element_type=F32) * cfg.scale
            p = jnp.exp(jnp.where(ok, s, NEG) - lse)
            dp = lax.dot_general(do, vw, (((1,), (1,)), ((), ())), preferred_element_type=F32)
            ds = p * (dp - delta) * cfg.scale
            dq_ref[:, cols] = jnp.dot(ds.astype(BF), kw, preferred_element_type=F32).astype(dq_ref.dtype)
            if has_sink:
                part = -jnp.sum(jnp.exp(sink_ref[j, :1, :1] - lse) * delta, axis=0, keepdims=True)
                refs[11][j * SUBLANES:(j + 1) * SUBLANES, :] = jnp.broadcast_to(part, (SUBLANES, LANES))

    q_spec = pl.BlockSpec((blk, per * LANES), lambda r, h, i: (i, cfg.qcol(r) + h))
    o_spec = pl.BlockSpec((blk, per * LANES), lambda r, h, i: (i, cfg.ocol(r) + h))
    in_specs = [q_spec] + cfg.rows3(pk * LANES, cfg.kcol) + cfg.rows3(pk * LANES, cfg.vcol) + [o_spec] * 3
    kc_, vc_ = cfg.chains(k), cfg.chains(v)
    operands = [cfg.chains(q), kc_, kc_, kc_, vc_, vc_, vc_, cfg.chains(do), cfg.chains(o), cfg.chains(lse)]
    out_shape = [jax.ShapeDtypeStruct((cfg.len, cfg.dil * cfg.hq * LANES), BF)]
    out_specs = [o_spec]
    if has_sink:
        in_specs.insert(0, pl.BlockSpec((per, SUBLANES, LANES), lambda r, h, i: (h, 0, 0)))
        operands.insert(0, sink)
        out_shape.append(jax.ShapeDtypeStruct((cfg.hq // per, cfg.nb, per * SUBLANES, LANES), F32))
        out_specs.append(pl.BlockSpec((None, None, per * SUBLANES, LANES), lambda r, h, i: (h, i, 0, 0)))
    outs = pl.pallas_call(
        body, out_shape=out_shape, grid=(cfg.dil, cfg.hq // per, cfg.nb), in_specs=in_specs, out_specs=out_specs,
        compiler_params=_params(("parallel", "parallel", "parallel")), name=name,
    )(*operands)
    dq = outs[0].reshape(cfg.T, cfg.hq * LANES)
    return (dq, outs[1]) if has_sink else dq


def band_dkv(cfg, q, k, v, do, o, lse, name, out_dtype, add=None, dv_into=None):
    blk, per, pk, group = cfg.blk, cfg.per, cfg.pk, cfg.group
    has_add = add is not None
    carried = dv_into is not None
    assert not carried or cfg.dil == 1

    def body(*refs):
        k_ref, v_ref = refs[:2]
        qs, dos, os_, lses = refs[2:5], refs[5:8], refs[8:11], refs[11:14]
        pos = 14 + (2 if has_add else 0) + (1 if carried else 0)
        dk_ref, dv_ref = refs[pos:pos + 2]
        ok = cfg.valid(pl.program_id(2), True)
        for jk in range(pk):
            kcols = slice(jk * LANES, (jk + 1) * LANES)
            kt, vt = k_ref[:, kcols], v_ref[:, kcols]
            dk = jnp.zeros((blk, LANES), F32)
            dv = jnp.zeros((blk, LANES), F32)
            for g in range(group):
                j = jk * group + g
                qw = cfg.window(*qs, j)
                dow = cfg.window(*dos, j)
                lse = cfg.window(*lses, j)[:, :1]
                delta = jnp.sum(dow.astype(F32) * cfg.window(*os_, j).astype(F32), axis=-1, keepdims=True)
                s = lax.dot_general(qw, kt, (((1,), (1,)), ((), ())), preferred_element_type=F32) * cfg.scale
                p = jnp.exp(jnp.where(ok, s, NEG) - lse)
                dv = dv + lax.dot_general(p.astype(BF), dow, (((0,), (0,)), ((), ())), preferred_element_type=F32)
                dp = lax.dot_general(dow, vt, (((1,), (1,)), ((), ())), preferred_element_type=F32)
                ds = p * (dp - delta) * cfg.scale
                dk = dk + lax.dot_general(ds.astype(BF), qw, (((0,), (0,)), ((), ())), preferred_element_type=F32)
            if has_add:
                dk, dv = dk + refs[14][:, kcols].astype(F32), dv + refs[15][:, kcols].astype(F32)
            dk_ref[:, kcols] = dk.astype(dk_ref.dtype)
            dv_ref[:, kcols] = dv.astype(dv_ref.dtype)

    k_spec = pl.BlockSpec((blk, pk * LANES), lambda r, h, i: (i, cfg.kcol(r) + h))
    v_spec = pl.BlockSpec((blk, pk * LANES), lambda r, h, i: (i, cfg.vcol(r) + h))
    d_spec = pl.BlockSpec((blk, pk * LANES), lambda r, h, i: (i, cfg.dkcol(r) + h))
    in_specs = [k_spec, v_spec] + cfg.rows3(per * LANES, cfg.qcol) + cfg.rows3(per * LANES, cfg.ocol) * 3
    qc_, doc, oc, lc = cfg.chains(q), cfg.chains(do), cfg.chains(o), cfg.chains(lse)
    operands = [cfg.chains(k), cfg.chains(v), qc_, qc_, qc_, doc, doc, doc, oc, oc, oc, lc, lc, lc]
    if has_add:
        in_specs += [d_spec, d_spec]
        operands += [cfg.chains(add[0]), cfg.chains(add[1])]
    cols = cfg.dil * cfg.hkv * LANES
    out_shape = [jax.ShapeDtypeStruct((cfg.len, cols), out_dtype)] * 2
    out_specs = [d_spec, d_spec]
    aliases = {}
    if carried:
        buf, blocks, block0 = dv_into
        out_shape[1] = jax.ShapeDtypeStruct((cfg.T, blocks * LANES), BF)
        out_specs[1] = pl.BlockSpec((blk, pk * LANES), lambda r, h, i: (i, block0 // pk + h))
        aliases = {len(operands): 1}
        in_specs.append(pl.BlockSpec(memory_space=pl.ANY))
        operands.append(buf)
    dk, dv = pl.pallas_call(
        body, out_shape=out_shape, grid=(cfg.dil, cfg.hq // per, cfg.nb), in_specs=in_specs, out_specs=out_specs,
        input_output_aliases=aliases, compiler_params=_params(("parallel", "parallel", "parallel")), name=name,
    )(*operands)
    return dk.reshape(cfg.T, cfg.hkv * LANES), (dv if carried else dv.reshape(cfg.T, cfg.hkv * LANES))


HBM_SPEC = pl.BlockSpec(memory_space=pltpu.HBM)


def _place():
    x, y, c = lax.axis_index("x"), lax.axis_index("y"), lax.axis_index("c")
    chips = [(1 - x, y), (x, 1 - y), (1 - x, 1 - y)]
    return x, y, c, chips


def gather_weights(shards):
    n = len(shards)

    def body(*refs):
        ins, outs = refs[:n], refs[n:2 * n]
        send_sems, recv_sems, local_sems = refs[2 * n:]
        x, y, c, chips = _place()
        me = 2 * x + y
        sibling = (x, y, 1 - c)

        def copy(w, k, src, chip_of_block, half, to):
            return pltpu.make_async_remote_copy(
                src_ref=src, dst_ref=outs[w].at[chip_of_block, half], send_sem=send_sems.at[6 * w + k],
                recv_sem=recv_sems.at[6 * w + k], device_id=to, device_id_type=MESH)

        started = []
        local = []
        for w in range(n):
            own = pltpu.make_async_copy(ins[w], outs[w].at[me], local_sems.at[w])
            own.start()
            local.append(own)
            for j, chip in enumerate(chips):
                cp = copy(w, j, ins[w].at[c], me, c, (*chip, c))
                cp.start()
                started.append(cp)
        for w in range(n):
            for j, (cx, cy) in enumerate(chips):
                them = 2 * cx + cy
                copy(w, j, ins[w].at[c], them, c, (cx, cy, c)).wait_recv()
                fwd = copy(w, 3 + j, outs[w].at[them, c], them, c, sibling)
                fwd.start()
                started.append(fwd)
        for w in range(n):
            for j, (cx, cy) in enumerate(chips):
                copy(w, 3 + j, ins[w].at[c], 2 * cx + cy, 1 - c, sibling).wait_recv()
        for cp in started:
            cp.wait_send()
        for own in local:
            own.wait()

    return pl.pallas_call(
        body, out_shape=[jax.ShapeDtypeStruct((4,) + s.shape, s.dtype) for s in shards],
        in_specs=[HBM_SPEC] * n, out_specs=[HBM_SPEC] * n,
        scratch_shapes=[pltpu.SemaphoreType.DMA((6 * n,)), pltpu.SemaphoreType.DMA((6 * n,)),
                        pltpu.SemaphoreType.DMA((n,))],
        name="gather_weights",
    )(*shards)


def _core_index():
    return lax.axis_index("c").astype(jnp.int32).reshape(1)


def presum_core_halves(g2, core, name, ship=None):
    _, rows, cols = g2.shape
    tr = _row_tile(rows, cols, 1 << 20)
    nb = rows // tr
    g2 = g2.reshape(2 * rows, cols)
    shipping = ship is not None

    def body(*refs):
        core_ref, mine_ref, other_ref = refs[:3]
        if shipping:
            ship_ref, out_ref, landed_ref, land, send_sems, recv_sems, ici_send, ici_recv, ici_local = refs[3:]
        else:
            out_ref, land, send_sems, recv_sems = refs[3:]
        x, y, c, chips = _place()
        i = pl.program_id(0)
        if shipping:
            me = 2 * x + y

            def own():
                return pltpu.make_async_copy(ship_ref.at[me], landed_ref.at[me], ici_local.at[0])

            def to_chip(j, cx, cy):
                return pltpu.make_async_remote_copy(
                    src_ref=ship_ref.at[2 * cx + cy], dst_ref=landed_ref.at[me], send_sem=ici_send.at[j],
                    recv_sem=ici_recv.at[j], device_id=(cx, cy, c), device_id_type=MESH)

            def from_chip(j, cx, cy):
                return pltpu.make_async_remote_copy(
                    src_ref=ship_ref.at[me], dst_ref=landed_ref.at[2 * cx + cy], send_sem=ici_send.at[j],
                    recv_sem=ici_recv.at[j], device_id=(cx, cy, c), device_id_type=MESH)

            @pl.when(i == 0)
            def _():
                own().start()
                for j, (cx, cy) in enumerate(chips):
                    to_chip(j, cx, cy).start()

        slot = i % 2
        cp = pltpu.make_async_remote_copy(
            src_ref=other_ref, dst_ref=land.at[slot], send_sem=send_sems.at[slot], recv_sem=recv_sems.at[slot],
            device_id=(x, y, 1 - c), device_id_type=MESH)
        cp.start()
        cp.wait_recv()
        out_ref[...] = (mine_ref[...] + land[slot]).astype(out_ref.dtype)
        cp.wait_send()
        if shipping:
            @pl.when(i == nb - 1)
            def _():
                for j, (cx, cy) in enumerate(chips):
                    from_chip(j, cx, cy).wait_recv()
                for j, (cx, cy) in enumerate(chips):
                    to_chip(j, cx, cy).wait_send()
                own().wait()

    in_specs = [pl.BlockSpec((tr, cols), lambda i, core: (core[0] * nb + i, 0)),
                pl.BlockSpec((tr, cols), lambda i, core: ((1 - core[0]) * nb + i, 0))]
    out_specs = [pl.BlockSpec((tr, cols), lambda i, core: (i, 0))]
    out_shape = [jax.ShapeDtypeStruct((rows, cols), BF)]
    scratch = [pltpu.VMEM((2, tr, cols), F32), pltpu.SemaphoreType.DMA((2,)), pltpu.SemaphoreType.DMA((2,))]
    operands = [core, g2, g2]
    if shipping:
        in_specs.append(pl.BlockSpec(memory_space=pl.ANY))
        out_specs.append(pl.BlockSpec(memory_space=pl.ANY))
        out_shape.append(jax.ShapeDtypeStruct(ship.shape, ship.dtype))
        scratch += [pltpu.SemaphoreType.DMA((3,)), pltpu.SemaphoreType.DMA((3,)), pltpu.SemaphoreType.DMA((1,))]
        operands.append(ship)
    grid_spec = pltpu.PrefetchScalarGridSpec(
        num_scalar_prefetch=1, grid=(nb,), in_specs=in_specs, out_specs=out_specs, scratch_shapes=scratch)
    outs = pl.pallas_call(
        body, out_shape=out_shape, grid_spec=grid_spec, compiler_params=_params(("arbitrary",)), name=name,
    )(*operands)
    return (outs[0], outs[1]) if shipping else outs[0]


def sum_and_swap(landed, name):
    n, rows, cols = landed.shape
    tr = _row_tile(rows, cols)

    def body(*refs):
        slots = refs[:n]
        mine_ref, theirs_ref, out_buf, land, send_sems, recv_sems = refs[n:]
        x, y, c, _ = _place()
        slot = pl.program_id(0) % 2
        tot = slots[0][...].astype(F32)
        for r in slots[1:]:
            tot = tot + r[...].astype(F32)
        mine_ref[...] = tot
        out_buf[slot] = tot
        cp = pltpu.make_async_remote_copy(
            src_ref=out_buf.at[slot], dst_ref=land.at[slot], send_sem=send_sems.at[slot], recv_sem=recv_sems.at[slot],
            device_id=(x, y, 1 - c), device_id_type=MESH)
        cp.start()
        cp.wait_recv()
        theirs_ref[...] = land[slot]
        cp.wait_send()

    specs = [pl.BlockSpec((None, tr, cols), functools.partial(lambda s, i: (s, i, 0), s)) for s in range(n)]
    row = pl.BlockSpec((tr, cols), lambda i: (i, 0))
    return pl.pallas_call(
        body, out_shape=[jax.ShapeDtypeStruct((rows, cols), F32)] * 2, grid=(rows // tr,), in_specs=specs,
        out_specs=[row, row],
        scratch_shapes=[pltpu.VMEM((2, tr, cols), F32), pltpu.VMEM((2, tr, cols), F32),
                        pltpu.SemaphoreType.DMA((2,)), pltpu.SemaphoreType.DMA((2,))],
        compiler_params=_params(("arbitrary",)), name=name,
    )(*([landed] * n))


def scatter_partials(parts):
    n = len(parts)

    def body(*refs):
        ins, outs = refs[:n], refs[n:2 * n]
        send_sems, recv_sems, local_sems = refs[2 * n:]
        x, y, c, chips = _place()
        me = 2 * x + y
        started = []
        for w in range(n):
            own = pltpu.make_async_copy(ins[w].at[me], outs[w].at[me], local_sems.at[w])
            own.start()
            started.append(own)
        sends = []
        for w in range(n):
            for j, (cx, cy) in enumerate(chips):
                cp = pltpu.make_async_remote_copy(
                    src_ref=ins[w].at[2 * cx + cy], dst_ref=outs[w].at[me], send_sem=send_sems.at[3 * w + j],
                    recv_sem=recv_sems.at[3 * w + j], device_id=(cx, cy, c), device_id_type=MESH)
                cp.start()
                sends.append(cp)
        for w in range(n):
            for j, (cx, cy) in enumerate(chips):
                pltpu.make_async_remote_copy(
                    src_ref=ins[w].at[me], dst_ref=outs[w].at[2 * cx + cy], send_sem=send_sems.at[3 * w + j],
                    recv_sem=recv_sems.at[3 * w + j], device_id=(cx, cy, c), device_id_type=MESH).wait_recv()
        for cp in sends:
            cp.wait_send()
        for own in started:
            own.wait()

    return pl.pallas_call(
        body, out_shape=[jax.ShapeDtypeStruct(p.shape, p.dtype) for p in parts],
        in_specs=[HBM_SPEC] * n, out_specs=[HBM_SPEC] * n,
        scratch_shapes=[pltpu.SemaphoreType.DMA((3 * n,)), pltpu.SemaphoreType.DMA((3 * n,)),
                        pltpu.SemaphoreType.DMA((n,))],
        name="scatter_partials",
    )(*parts)


def adamw_halves(w, mine, theirs, m, v, core, name):
    rows, cols = w.shape
    tr = _row_tile(rows // 2, cols, 1 << 18)
    nh = rows // 2 // tr

    def body(core_ref, w_ref, a_ref, b_ref, m_ref, v_ref, g_out, d_out, m_out, v_out):
        g = jnp.where(pl.program_id(0) // nh == core_ref[0], a_ref[...], b_ref[...])
        d_out[...], m_out[...], v_out[...] = _adam_fn(w_ref[...], g, m_ref[...], v_ref[...])
        g_out[...] = g

    full = pl.BlockSpec((tr, cols), lambda i, core: (i, 0))
    half = pl.BlockSpec((tr, cols), lambda i, core: (i % nh, 0))
    grid_spec = pltpu.PrefetchScalarGridSpec(
        num_scalar_prefetch=1, grid=(rows // tr,), in_specs=[full, half, half, full, full], out_specs=[full] * 4)
    return pl.pallas_call(
        body, out_shape=[jax.ShapeDtypeStruct((rows, cols), F32)] * 4, grid_spec=grid_spec,
        compiler_params=_params(("parallel",)), name=name,
    )(core, w, mine, theirs, m, v)


def gather_small(vec):
    rows = vec.shape[0]

    def body(v_ref, out_ref, send_sems, recv_sems):
        x, y, c, _ = _place()
        me = 4 * x + 2 * y + c
        out_ref[me] = v_ref[...]
        flips = [(dx, dy, dc) for dx in (0, 1) for dy in (0, 1) for dc in (0, 1)][1:]

        def peer(f):
            return tuple(1 - a if d else a for a, d in zip((x, y, c), f))

        def copy(k, block, to):
            return pltpu.make_async_remote_copy(
                src_ref=v_ref, dst_ref=out_ref.at[block], send_sem=send_sems.at[k], recv_sem=recv_sems.at[k],
                device_id=to, device_id_type=MESH)

        sends = [copy(k, me, peer(f)) for k, f in enumerate(flips)]
        for cp in sends:
            cp.start()
        for k, f in enumerate(flips):
            px, py, pc = peer(f)
            copy(k, 4 * px + 2 * py + pc, peer(f)).wait_recv()
        for cp in sends:
            cp.wait_send()

    vm = pl.BlockSpec(memory_space=pltpu.VMEM)
    return pl.pallas_call(
        body, out_shape=jax.ShapeDtypeStruct((8, rows, SMALL_COLS), F32), in_specs=[vm], out_specs=vm,
        scratch_shapes=[pltpu.SemaphoreType.DMA((7,)), pltpu.SemaphoreType.DMA((7,))], name="gather_small",
    )(vec)


def sum_slots(a, out_dtype, name):
    n, rows, cols = a.shape
    tr = _row_tile(rows, cols)

    def body(*refs):
        tot = refs[0][...].astype(F32)
        for r in refs[1:n]:
            tot = tot + r[...].astype(F32)
        refs[n][...] = tot.astype(out_dtype)

    specs = [pl.BlockSpec((None, tr, cols), functools.partial(lambda s, i: (s, i, 0), s)) for s in range(n)]
    return pl.pallas_call(
        body, out_shape=jax.ShapeDtypeStruct((rows, cols), out_dtype), grid=(rows // tr,), in_specs=specs,
        out_specs=pl.BlockSpec((tr, cols), lambda i: (i, 0)), compiler_params=_params(("parallel",)), name=name,
    )(*([a] * n))


def _adam_fn(w, g, m, v):
    m = ADAM_B1 * m + (1.0 - ADAM_B1) * g
    v = ADAM_B2 * v + (1.0 - ADAM_B2) * (g * g)
    m_hat = m / (1.0 - ADAM_B1 ** ADAM_STEP)
    v_hat = v / (1.0 - ADAM_B2 ** ADAM_STEP)
    delta = -ADAM_LR * (m_hat / (jnp.sqrt(v_hat) + ADAM_EPS) + ADAM_WD * w)
    return delta, m, v


def adamw(w, g, m, v, name):
    return rowwise(_adam_fn, [w, g, m, v], [F32, F32, F32], name)


def _full_weight(name, gathered, local_shape):
    L, a, b = local_shape
    g = gathered.reshape((4, L, a, b))
    if SHARD_AXIS[name] == 1:
        return g.transpose(1, 0, 2, 3).reshape(L, 4 * a, b)
    return g.transpose(1, 2, 0, 3).reshape(L, a, 4 * b)


def _grad_slots(name, dw):
    L, a, b = dw.shape
    if SHARD_AXIS[name] == 1:
        s = dw.reshape(L, 4, a // 4, b).transpose(1, 0, 2, 3)
        rows, cols = L * (a // 4), b
    else:
        s = dw.reshape(L, a, 4, b // 4).transpose(2, 0, 1, 3)
        rows, cols = L * a, b // 4
    return s.reshape(4, 2, rows // 2, cols).transpose(1, 0, 2, 3)


def _attn_a(T):
    group = A_HEADS // A_KV_HEADS
    return Band(T, 1, A_HEADS, group, group, A_HEADS, 0, A_KV_HEADS, 0, A_HEADS + 2 * A_KV_HEADS,
                A_HEADS + A_KV_HEADS, 1.0 / math.sqrt(HEAD_DIM), A_HALF_WINDOW, BAND_BLOCK)


def _attn_b(T):
    return Attn(T, 1, B_HEADS, 1, B_HEADS, 0, B_HEADS, 0, 2 * B_HEADS, 1, 2, B_PAD, 1.0 / math.sqrt(B_QK), None,
                DENSE_BLOCK, DENSE_OTHER_BLOCK)


def _attn_c(T, group):
    window, dil = C_PATTERNS[group]
    return Band(T, dil, C_HEADS, 1, BAND_HEADS_PER_STEP, C_HEADS, 0, C_HEADS, 0, C_HEADS, 0,
                1.0 / math.sqrt(HEAD_DIM), window // 2 // dil, BAND_BLOCK)


def _pad_heads(a, axis_len_true, axis_len_pad):
    lead = a.shape[:-1]
    h = a.shape[-1] // axis_len_true
    a = a.reshape(lead + (h, axis_len_true))
    a = jnp.pad(a, [(0, 0)] * len(lead) + [(0, 0), (0, axis_len_pad - axis_len_true)])
    return a.reshape(lead + (h * axis_len_pad,))


def _unpad_heads(a, axis_len_true, axis_len_pad):
    lead = a.shape[:-1]
    h = a.shape[-1] // axis_len_pad
    return a.reshape(lead + (h, axis_len_pad))[..., :axis_len_true].reshape(lead + (h * axis_len_true,))


def _weight_grad(G, name, layer, a, dy, W, tag):
    layers, rows, cols = W[name].shape
    if name in SLOT_DIRECT:
        G[name] = matmul([(a, dy)], "tn", F32, tag, slot=Slot(name, layers, layer, rows, cols, 0, G.get(name)))
    else:
        G.setdefault(name, [None] * layers)[layer] = matmul([(a, dy)], "tn", F32, tag)


def _mixer_fwd(kind, slot, hn, W, S, tabs, tag):
    T = hn.shape[0]
    if kind == 0:
        cfg = _attn_a(T)
        qkv = matmul([(hn, W["a_w_in"][slot])], "nn", BF, tag + "_a_in")
        q = headnorm_fwd(qkv, W["a_q_norm"][slot], tabs["hd"], tag + "_a_qn", A_HEADS, 0, HEAD_DIM, HEAD_DIM)
        k = headnorm_fwd(qkv, W["a_k_norm"][slot], tabs["hd"], tag + "_a_kn", A_KV_HEADS, A_HEADS, HEAD_DIM, HEAD_DIM)
        sink = jnp.broadcast_to(W["a_sink"][slot][:, None, None], (A_HEADS, SUBLANES, LANES)).astype(F32)
        o, lse = band_fwd(cfg, q, k, qkv, tag + "_a_att", BF, sink=sink)
        S.update(qkv=qkv, q=q, k=k, o=o, lse=lse, sink=sink)
        return o
    if kind == 1:
        cfg = _attn_b(T)
        lat = matmul([(hn, W["b_w_in"][slot])], "nn", BF, tag + "_b_in")
        qn = rmsnorm_fwd(lat, W["b_q_lat_norm"][slot], tag + "_b_qlat", 0, B_Q_RANK)
        kvn = rmsnorm_fwd(lat, W["b_kv_lat_norm"][slot], tag + "_b_kvlat", 1, B_KV_RANK)
        qp = matmul([(qn, W["b_w_q_up_pad"][slot])], "nn", BF, tag + "_b_qup")
        kv = matmul([(kvn, W["b_w_kv_up"][slot])], "nn", BF, tag + "_b_kvup")
        k_rope = lat[:, B_Q_RANK + B_KV_RANK:]
        kpre = jnp.concatenate(
            [kv.reshape(T, B_HEADS, 2 * B_NOPE)[:, :, :B_NOPE],
             jnp.broadcast_to(k_rope[:, None, :], (T, B_HEADS, B_ROPE)),
             jnp.zeros((T, B_HEADS, B_PAD - B_QK), BF)], axis=-1).reshape(T, B_HEADS * B_PAD)
        q = headnorm_fwd(qp, W["b_q_norm_pad"][slot], tabs["b"], tag + "_b_qn", B_HEADS, 0, B_PAD, B_QK)
        k = headnorm_fwd(kpre, W["b_k_norm_pad"][slot], tabs["b"], tag + "_b_kn", B_HEADS, 0, B_PAD, B_QK)
        o, lse = flash_fwd(cfg, q, k, kv, tag + "_b_att", BF)
        S.update(lat=lat, qn=qn, kvn=kvn, qp=qp, kv=kv, kpre=kpre, q=q, k=k, o=o, lse=lse)
        return o
    qkv = matmul([(hn, W["c_w_in"][slot])], "nn", BF, tag + "_c_in")
    nq = C_GROUPS * C_HEADS
    qs = [headnorm_fwd(qkv, W["c_q_norm"][slot], tabs["hd"], f"{tag}_c_qn{g}", C_HEADS, g * C_HEADS, HEAD_DIM, HEAD_DIM)
          for g in range(C_GROUPS)]
    k = headnorm_fwd(qkv, W["c_k_norm"][slot], tabs["hd"], tag + "_c_kn", C_HEADS, nq, HEAD_DIM, HEAD_DIM)
    outs, lses = [], []
    v = qkv[:, (C_GROUPS + 1) * C_HEADS * HEAD_DIM:]
    for g in range(C_GROUPS):
        og, lg = band_fwd(_attn_c(T, g), qs[g], k, v, f"{tag}_c_att{g}", F32)
        outs.append(og)
        lses.append(lg)
    o, lse = rowwise(_merge_fn, outs + lses, [BF, F32], tag + "_c_merge")
    S.update(qkv=qkv, qs=qs, v=v, k=k, o=o, lse=lse)
    return o


def _mixer_bwd(kind, slot, hn, do, W, S, tabs, tag, G):
    T = hn.shape[0]
    if kind == 0:
        cfg = _attn_a(T)
        qkv = S["qkv"]
        dq, dsink = band_dq(cfg, S["q"], S["k"], qkv, do, S["o"], S["lse"], tag + "_a_dq", sink=S["sink"])
        blocks = A_HEADS + 2 * A_KV_HEADS
        dqkv, dgq = headnorm_bwd(qkv, W["a_q_norm"][slot], tabs["hd"], dq, tag + "_a_dqn", A_HEADS, 0, HEAD_DIM, HEAD_DIM,
                                 into=(None, blocks, 0))
        dk, dqkv = band_dkv(cfg, S["q"], S["k"], qkv, do, S["o"], S["lse"], tag + "_a_dkv", BF,
                            dv_into=(dqkv, blocks, A_HEADS + A_KV_HEADS))
        dqkv, dgk = headnorm_bwd(qkv, W["a_k_norm"][slot], tabs["hd"], dk, tag + "_a_dkn", A_KV_HEADS, A_HEADS,
                                 HEAD_DIM, HEAD_DIM, into=(dqkv, blocks, A_HEADS))
        _weight_grad(G, "a_w_in", slot, hn, dqkv, W, tag + "_a_dwin")
        G["a_q_norm"][slot], G["a_k_norm"][slot] = dgq, dgk
        parts = dsink.reshape(A_HEADS // cfg.per, cfg.nb, cfg.per, SUBLANES, LANES)[:, :, :, 0, 0]
        G["a_sink"][slot] = jnp.sum(parts, axis=1).reshape(A_HEADS)
        return matmul([(dqkv, W["a_w_in"][slot])], "nt", F32, tag + "_a_dhn")
    if kind == 1:
        cfg = _attn_b(T)
        kv = S["kv"]
        dq = flash_dq(cfg, S["q"], S["k"], kv, do, S["o"], S["lse"], tag + "_b_dq")
        dk, dv = flash_dkv(cfg, S["q"], S["k"], kv, do, S["o"], S["lse"], tag + "_b_dkv", BF)
        dqp, dgq = headnorm_bwd(S["qp"], W["b_q_norm_pad"][slot], tabs["b"], dq, tag + "_b_dqn", B_HEADS, 0, B_PAD, B_QK)
        dkp, dgk, dksum = headnorm_bwd(S["kpre"], W["b_k_norm_pad"][slot], tabs["b"], dk, tag + "_b_dkn", B_HEADS, 0,
                                       B_PAD, B_QK, head_sum=True)
        dkv = jnp.concatenate([dkp.reshape(T, B_HEADS, B_PAD)[:, :, :B_NOPE], dv.reshape(T, B_HEADS, LANES)],
                              axis=-1).reshape(T, B_HEADS * 2 * B_NOPE)
        _weight_grad(G, "b_w_kv_up", slot, S["kvn"], dkv, W, tag + "_b_dwkv")
        G["b_w_q_up"][slot] = _unpad_heads(matmul([(S["qn"], dqp)], "tn", F32, tag + "_b_dwq"), B_QK, B_PAD)
        dqn = matmul([(dqp, W["b_w_q_up_pad"][slot])], "nt", F32, tag + "_b_dqnorm")
        dkvn = matmul([(dkv, W["b_w_kv_up"][slot])], "nt", F32, tag + "_b_dkvnorm")
        dql, dg_q = rmsnorm_bwd(S["lat"], W["b_q_lat_norm"][slot], dqn, tag + "_b_dqlat", [BF], None, 0, B_Q_RANK)
        dkvl, dg_kv = rmsnorm_bwd(S["lat"], W["b_kv_lat_norm"][slot], dkvn, tag + "_b_dkvlat", [BF], None, 1, B_KV_RANK)
        dlat = jnp.concatenate([dql, dkvl, dksum[:, B_NOPE:B_QK].astype(BF)], axis=1)
        _weight_grad(G, "b_w_in", slot, hn, dlat, W, tag + "_b_dwin")
        G["b_q_norm"][slot], G["b_k_norm"][slot] = dgq[:B_QK], dgk[:B_QK]
        G["b_q_lat_norm"][slot], G["b_kv_lat_norm"][slot] = dg_q, dg_kv
        return matmul([(dlat, W["b_w_in"][slot])], "nt", F32, tag + "_b_dhn")
    qkv = S["qkv"]
    nq = C_GROUPS * C_HEADS
    blocks = (C_GROUPS + 2) * C_HEADS
    dqkv, dgq = None, 0.0
    for g in range(C_GROUPS):
        dq = band_dq(_attn_c(T, g), S["qs"][g], S["k"], S["v"], do, S["o"], S["lse"], f"{tag}_c_dq{g}")
        dqkv, dg = headnorm_bwd(qkv, W["c_q_norm"][slot], tabs["hd"], dq, f"{tag}_c_dqn{g}", C_HEADS, g * C_HEADS,
                                HEAD_DIM, HEAD_DIM, into=(dqkv, blocks, g * C_HEADS))
        dgq = dgq + dg
    acc = None
    for g in reversed(range(C_GROUPS)):
        into = (dqkv, blocks, (C_GROUPS + 1) * C_HEADS) if g == 0 else None
        acc = band_dkv(_attn_c(T, g), S["qs"][g], S["k"], S["v"], do, S["o"], S["lse"], f"{tag}_c_dkv{g}", F32,
                       add=acc, dv_into=into)
    dk, dqkv = acc
    dqkv, dgk = headnorm_bwd(qkv, W["c_k_norm"][slot], tabs["hd"], dk, tag + "_c_dkn", C_HEADS, nq, HEAD_DIM, HEAD_DIM,
                             into=(dqkv, blocks, nq))
    _weight_grad(G, "c_w_in", slot, hn, dqkv, W, tag + "_c_dwin")
    G["c_q_norm"][slot], G["c_k_norm"][slot] = dgq, dgk
    return matmul([(dqkv, W["c_w_in"][slot])], "nt", F32, tag + "_c_dhn")


MIXER_OUT = ("a_w_o", "b_w_o", "c_w_o")


def local_step(x, p, positions, loss_target, W):
    T = x.shape[0]
    tabs = {"hd": rope_tables(positions, HEAD_DIM, 0, PARTIAL_ROT), "b": rope_tables(positions, B_PAD, B_NOPE, B_ROPE)}
    W = dict(W)
    W["b_w_q_up_pad"] = _pad_heads(W["b_w_q_up"], B_QK, B_PAD)
    W["b_q_norm_pad"] = _pad_heads(W["b_q_norm"], B_QK, B_PAD)
    W["b_k_norm_pad"] = _pad_heads(W["b_k_norm"], B_QK, B_PAD)
    saved = []
    h = x
    for i in range(DEPTH):
        kind, slot = i % 3, i // 3
        tag = f"l{i}"
        S = {"h0": h}
        hn = rmsnorm_fwd(h, W["g_mix"][i], tag + "_mixnorm")
        o = _mixer_fwd(kind, slot, hn, W, S, tabs, tag)
        h1 = matmul([(o, W[MIXER_OUT[kind]][slot])], "nn", F32, tag + "_mixout", res=h)
        hn2 = rmsnorm_fwd(h1, W["g_ffn"][i], tag + "_ffnnorm")
        a, b, c = matmul_swiglu(hn2, W["w_ffn_gate"][i], W["w_ffn_up"][i], tag + "_gateup")
        h2 = matmul([(c, W["w_ffn_down"][i])], "nn", F32, tag + "_down", res=h1)
        hn3 = rmsnorm_fwd(h2, W["g_ple"][i], tag + "_plenorm")
        p_i = p[i].astype(BF)
        pp = matmul([(p_i, W["w_ple_proj"][i])], "nn", BF, tag + "_pleproj")
        z, h3 = matmul([(hn3, W["w_ple_gate"][i])], "nn", BF, tag + "_plegate", ple=(h2, pp))
        S.update(hn=hn, h1=h1, hn2=hn2, a=a, b=b, c=c, h2=h2, hn3=hn3, z=z, pp=pp, p=p_i)
        saved.append(S)
        h = h3

    loss, dh = loss_and_grad(h, loss_target, "loss")
    G = {n: [None] * W[n].shape[0] for n in SMALL + ("b_w_q_up",)}
    for i in reversed(range(DEPTH)):
        kind, slot = i % 3, i // 3
        tag = f"l{i}"
        S = saved[i]
        dz, dpp = rowwise(_ple_bwd_fn, [dh, S["z"], S["pp"]], [BF, BF], tag + "_dple")
        _weight_grad(G, "w_ple_proj", i, S["p"], dpp, W, tag + "_dwpleproj")
        _weight_grad(G, "w_ple_gate", i, S["hn3"], dz, W, tag + "_dwplegate")
        dhn3 = matmul([(dz, W["w_ple_gate"][i])], "nt", F32, tag + "_dplenorm")
        dh2, dh2b, G["g_ple"][i] = rmsnorm_bwd(S["h2"], W["g_ple"][i], dhn3, tag + "_dple_norm", [F32, BF], dres=dh)
        da, db = matmul([(dh2b, W["w_ffn_down"][i])], "nt", BF, tag + "_dswiglu", swiglu=(S["a"], S["b"]))
        _weight_grad(G, "w_ffn_down", i, S["c"], dh2b, W, tag + "_dwdown")
        _weight_grad(G, "w_ffn_gate", i, S["hn2"], da, W, tag + "_dwgate")
        _weight_grad(G, "w_ffn_up", i, S["hn2"], db, W, tag + "_dwup")
        dhn2 = matmul([(da, W["w_ffn_gate"][i]), (db, W["w_ffn_up"][i])], "nt", F32, tag + "_dffnnorm")
        dh1, dh1b, G["g_ffn"][i] = rmsnorm_bwd(S["h1"], W["g_ffn"][i], dhn2, tag + "_dffn_norm", [F32, BF], dres=dh2)
        wo = W[MIXER_OUT[kind]][slot]
        do = matmul([(dh1b, wo)], "nt", BF, tag + "_dmixout")
        _weight_grad(G, MIXER_OUT[kind], slot, S["o"], dh1b, W, tag + "_dwmixout")
        dhn = _mixer_bwd(kind, slot, S["hn"], do, W, S, tabs, tag, G)
        dh, G["g_mix"][i] = rmsnorm_bwd(S["h0"], W["g_mix"][i], dhn, tag + "_dmix_norm", [F32], dres=dh1)
    return loss, dh, G


def _pack_small(vals):
    flat = jnp.concatenate([vals[n].reshape(-1).astype(F32) for n in SMALL])
    rows = -(-flat.shape[0] // SMALL_COLS)
    rows = -(-rows // SUBLANES) * SUBLANES
    return jnp.pad(flat, (0, rows * SMALL_COLS - flat.shape[0])).reshape(rows, SMALL_COLS)


def _unpack_small(packed, like):
    flat = packed.reshape(-1)
    out, off = {}, 0
    for n in SMALL:
        size = like[n].size
        out[n] = flat[off:off + size].reshape(like[n].shape)
        off += size
    return out


def kernel(x, p, positions, g_mix, g_ffn, g_ple, w_ple_gate, w_ple_proj, w_ffn_gate, w_ffn_up, w_ffn_down, a_w_in, a_q_norm, a_k_norm, a_sink, a_w_o, b_w_in, b_q_lat_norm, b_kv_lat_norm, b_w_q_up, b_w_kv_up, b_q_norm, b_k_norm, b_w_o, c_w_in, c_q_norm, c_k_norm, c_w_o, loss_target, m_g_mix, m_g_ffn, m_g_ple, m_w_ple_gate, m_w_ple_proj, m_w_ffn_gate, m_w_ffn_up, m_w_ffn_down, m_a_w_in, m_a_q_norm, m_a_k_norm, m_a_sink, m_a_w_o, m_b_w_in, m_b_q_lat_norm, m_b_kv_lat_norm, m_b_w_q_up, m_b_w_kv_up, m_b_q_norm, m_b_k_norm, m_b_w_o, m_c_w_in, m_c_q_norm, m_c_k_norm, m_c_w_o, v_g_mix, v_g_ffn, v_g_ple, v_w_ple_gate, v_w_ple_proj, v_w_ffn_gate, v_w_ffn_up, v_w_ffn_down, v_a_w_in, v_a_q_norm, v_a_k_norm, v_a_sink, v_a_w_o, v_b_w_in, v_b_q_lat_norm, v_b_kv_lat_norm, v_b_w_q_up, v_b_w_kv_up, v_b_q_norm, v_b_k_norm, v_b_w_o, v_c_w_in, v_c_q_norm, v_c_k_norm, v_c_w_o):
    args = dict(locals())
    w_loc = {n: args[n] for n in WEIGHTS}
    m_loc = {n: args["m_" + n] for n in WEIGHTS}
    v_loc = {n: args["v_" + n] for n in WEIGHTS}

    def halves(a):
        rows = a.shape[0] * a.shape[1]
        return a.reshape(2, rows // 2, a.shape[2])

    gathered = gather_weights([halves(w_loc[n].astype(BF)) for n in BIG])
    W = {n: _full_weight(n, g, w_loc[n].shape) for n, g in zip(BIG, gathered)}
    for n in SMALL:
        W[n] = w_loc[n]

    loss, dx, G = local_step(x[0], p[:, 0], positions[0], loss_target[0], W)
    loss = lax.psum(loss, ("x", "y", "c"))

    core = _core_index()
    landed, ready = [], None
    for n in EXCHANGE_ORDER:
        s = _grad_slots(n, jnp.stack(G[n])) if isinstance(G[n], list) else G[n]
        g2 = s.reshape(2, 4 * s.shape[2], s.shape[3])
        if ready is None:
            part = presum_core_halves(g2, core, "presum_" + n)
        else:
            part, got = presum_core_halves(g2, core, "presum_" + n, ship=ready)
            landed.append(got)
        ready = part.reshape(s.shape[1:])
    landed += scatter_partials([ready])
    halves = [sum_and_swap(a, "sum_" + n) for n, a in zip(EXCHANGE_ORDER, landed)]

    small = gather_small(_pack_small({n: jnp.stack(G[n]) for n in SMALL}))
    small_sum = sum_slots(small, F32, "sum_small")
    grads = _unpack_small(small_sum, w_loc)

    delta, new_m, new_v = {}, {}, {}
    for n, (mine, theirs) in zip(EXCHANGE_ORDER, halves):
        shape = w_loc[n].shape
        two_d = (shape[0] * shape[1], shape[2])
        g, d, m, v = adamw_halves(w_loc[n].reshape(two_d), mine, theirs, m_loc[n].reshape(two_d),
                                  v_loc[n].reshape(two_d), core, "adamw_" + n)
        grads[n], delta[n], new_m[n], new_v[n] = g.reshape(shape), d.reshape(shape), m.reshape(shape), v.reshape(shape)
    d, m, v = adamw(_pack_small(w_loc), small_sum, _pack_small(m_loc), _pack_small(v_loc), "adamw_small")
    delta.update(_unpack_small(d, w_loc))
    new_m.update(_unpack_small(m, w_loc))
    new_v.update(_unpack_small(v, w_loc))

    return (loss, dx[None], *[grads[n] for n in WEIGHTS], *[delta[n] for n in WEIGHTS],
            *[new_m[n] for n in WEIGHTS], *[new_v[n] for n in WEIGHTS])
```

```python
import functools
import math

import numpy as np
import jax
import jax.numpy as jnp
from jax import lax
from jax.experimental import pallas as pl
from jax.experimental.pallas import tpu as pltpu

F32 = jnp.float32
BF = jnp.bfloat16

D_MODEL = 2048
DEPTH = 4
HEAD_DIM = 128
ROPE_THETA = 500000.0
PARTIAL_ROT = HEAD_DIM // 4
NORM_EPS = 1e-6
NEG = -1e30
A_HEADS = 16
A_KV_HEADS = 4
A_HALF_WINDOW = 128
B_HEADS = 16
B_Q_RANK = 512
B_KV_RANK = 512
B_NOPE = 128
B_ROPE = 64
B_QK = B_NOPE + B_ROPE
B_PAD = 256
C_PATTERNS = ((128, 1), (512, 4), (2048, 16))
C_HEADS = 16
C_GROUPS = 3
ADAM_LR = 0.001
ADAM_B1 = 0.9
ADAM_B2 = 0.999
ADAM_EPS = 1e-08
ADAM_WD = 0.01
ADAM_STEP = 10

LANES = 128
SUBLANES = 8
VMEM_LIMIT_BYTES = 56 * 1024 * 1024
MATMUL_VMEM_BYTES = 46 * 1024 * 1024
MIN_M_TILE = 512
SINGLE_STEP_MAX_K = 2048
BAND_BLOCK = 256
BAND_HEADS_PER_STEP = 4
DENSE_BLOCK = 1024
DENSE_OTHER_BLOCK = 8192
DENSE_SUB = 256
MESH = pl.DeviceIdType.MESH

BIG = ("w_ple_gate", "w_ple_proj", "w_ffn_gate", "w_ffn_up", "w_ffn_down", "a_w_in", "a_w_o",
       "b_w_in", "b_w_q_up", "b_w_kv_up", "b_w_o", "c_w_in", "c_w_o")
SHARD_AXIS = {"w_ple_gate": 1, "w_ple_proj": 2, "w_ffn_gate": 2, "w_ffn_up": 2, "w_ffn_down": 1,
              "a_w_in": 2, "a_w_o": 1, "b_w_in": 1, "b_w_q_up": 2, "b_w_kv_up": 2, "b_w_o": 1,
              "c_w_in": 2, "c_w_o": 1}
SMALL = ("g_mix", "g_ffn", "g_ple", "a_q_norm", "a_k_norm", "a_sink", "b_q_lat_norm",
         "b_kv_lat_norm", "b_q_norm", "b_k_norm", "c_q_norm", "c_k_norm")
WEIGHTS = ("g_mix", "g_ffn", "g_ple", "w_ple_gate", "w_ple_proj", "w_ffn_gate", "w_ffn_up",
           "w_ffn_down", "a_w_in", "a_q_norm", "a_k_norm", "a_sink", "a_w_o", "b_w_in",
           "b_q_lat_norm", "b_kv_lat_norm", "b_w_q_up", "b_w_kv_up", "b_q_norm", "b_k_norm",
           "b_w_o", "c_w_in", "c_q_norm", "c_k_norm", "c_w_o")
SMALL_COLS = 1024
EXCHANGE_ORDER = ("c_w_in", "w_ffn_gate", "w_ffn_up", "w_ffn_down", "w_ple_gate", "a_w_in", "a_w_o", "b_w_o",
                  "c_w_o", "b_w_in", "w_ple_proj", "b_w_kv_up", "b_w_q_up")
SLOT_DIRECT = ("w_ple_gate", "w_ple_proj", "w_ffn_gate", "w_ffn_up", "w_ffn_down")


def _params(semantics):
    return pltpu.CompilerParams(dimension_semantics=semantics, vmem_limit_bytes=VMEM_LIMIT_BYTES)


def _tile(dim, cands=(1024, 1408, 512, 256, 128)):
    for c in cands:
        if dim % c == 0:
            return c
    return dim


def _k_tile(K, bytes_per_k, fixed_bytes):
    for t in (4096, 2816, 2048, 1408, 1024, 512, 256, 128):
        if K % t == 0 and 2 * bytes_per_k * t + fixed_bytes <= MATMUL_VMEM_BYTES:
            return t
    return _tile(K, (128,))


def _row_tile(rows, cols, target_elems=1 << 19):
    best = None
    for t in range(16, rows + 1, 16):
        if rows % t == 0 and t * cols <= target_elems:
            best = t
    return best if best is not None else rows


def _sigmoid(x):
    return 1.0 / (1.0 + jnp.exp(-x))


class Slot:
    def __init__(self, name, layers, layer, rows, cols, col0=0, buf=None):
        self.axis, self.layers, self.layer, self.rows, self.cols, self.col0, self.buf = (
            SHARD_AXIS[name], layers, layer, rows, cols, col0, buf)
        self.srows = rows // 4 if self.axis == 1 else rows
        self.scols = cols if self.axis == 1 else cols // 4
        self.half = layers * self.srows // 2

    def tiles(self, ncols):
        tm = _tile(math.gcd(self.srows, self.half))
        tn = _tile(math.gcd(self.scols, math.gcd(self.col0, ncols)))
        return tm, tn

    def spec(self, tm, tn):
        def index(i, j, k):
            row, col = i * tm, self.col0 + j * tn
            chip = row // self.srows if self.axis == 1 else col // self.scols
            flat = self.layer * self.srows + (row % self.srows if self.axis == 1 else row)
            cb = col // tn if self.axis == 1 else (col % self.scols) // tn
            return flat // self.half, chip, (flat % self.half) // tm, cb

        return pl.BlockSpec((None, None, tm, tn), index)

    def shape(self):
        return jax.ShapeDtypeStruct((2, 4, self.half, self.scols), F32)


def matmul(pairs, mode, out_dtype, name, res=None, swiglu=None, ple=None, slot=None):
    a0, b0 = pairs[0]
    if mode == "nn":
        (M, K), N = a0.shape, b0.shape[1]
    elif mode == "nt":
        (M, K), N = a0.shape, b0.shape[0]
    else:
        (K, M), N = a0.shape, b0.shape[1]
    tm, tn = (_tile(M), _tile(N)) if slot is None else slot.tiles(N)
    n_mn = 2 + (0 if res is None else 2) + (0 if swiglu is None else 2) + (0 if ple is None else 4)

    def k_tile(rows):
        return _k_tile(K, sum(rows * a.dtype.itemsize + tn * b.dtype.itemsize for a, b in pairs),
                       4 * rows * tn * (1 + n_mn))

    tk = k_tile(tm)
    if (slot is None and tk < K <= SINGLE_STEP_MAX_K and tm > MIN_M_TILE and M % MIN_M_TILE == 0
            and k_tile(MIN_M_TILE) == K):
        tm, tk = MIN_M_TILE, K
    nk = K // tk
    if mode == "nn":
        a_spec = pl.BlockSpec((tm, tk), lambda i, j, k: (i, k))
        b_spec = pl.BlockSpec((tk, tn), lambda i, j, k: (k, j))
        dims = (((1,), (0,)), ((), ()))
    elif mode == "nt":
        a_spec = pl.BlockSpec((tm, tk), lambda i, j, k: (i, k))
        b_spec = pl.BlockSpec((tn, tk), lambda i, j, k: (j, k))
        dims = (((1,), (1,)), ((), ()))
    else:
        a_spec = pl.BlockSpec((tk, tm), lambda i, j, k: (k, i))
        b_spec = pl.BlockSpec((tk, tn), lambda i, j, k: (k, j))
        dims = (((0,), (0,)), ((), ()))
    mn_spec = pl.BlockSpec((tm, tn), lambda i, j, k: (i, j))
    npairs = len(pairs)
    extras = [] if res is None else [res]
    if swiglu is not None:
        extras = list(swiglu)
    if ple is not None:
        extras = list(ple)
    nex = len(extras)
    nout = 2 if (swiglu is not None or ple is not None) else 1
    carried = slot is not None and slot.buf is not None

    def body(*refs):
        ins = refs[:2 * npairs]
        ex = refs[2 * npairs:2 * npairs + nex]
        first_out = 2 * npairs + nex + (1 if carried else 0)
        outs = refs[first_out:first_out + nout]
        k = pl.program_id(2)

        def product():
            part = None
            for p in range(npairs):
                d = lax.dot_general(ins[2 * p][...].astype(BF), ins[2 * p + 1][...].astype(BF), dims,
                                    preferred_element_type=F32)
                part = d if part is None else part + d
            return part

        def finish(r):
            if swiglu is not None:
                a = ex[0][...].astype(F32)
                b = ex[1][...].astype(F32)
                sg = _sigmoid(a)
                outs[0][...] = (r * b * (sg * (1.0 + a * (1.0 - sg)))).astype(out_dtype)
                outs[1][...] = (r * (a * sg)).astype(out_dtype)
            elif ple is not None:
                outs[0][...] = r.astype(out_dtype)
                outs[1][...] = ex[0][...] + _sigmoid(r) * ex[1][...].astype(F32)
            elif res is not None:
                outs[0][...] = (ex[0][...] + r).astype(out_dtype)
            else:
                outs[0][...] = r.astype(outs[0].dtype)

        if nk == 1:
            finish(product())
        else:
            acc = refs[-1]

            @pl.when(k == 0)
            def _():
                acc[...] = jnp.zeros_like(acc)

            acc[...] += product()

            @pl.when(k == nk - 1)
            def _():
                finish(acc[...])

    in_specs = []
    operands = []
    for a, b in pairs:
        in_specs += [a_spec, b_spec]
        operands += [a, b]
    in_specs += [mn_spec] * nex
    operands += extras
    out_shape = [jax.ShapeDtypeStruct((M, N), out_dtype)] * nout
    out_specs = [mn_spec] * nout
    aliases = {}
    if ple is not None:
        out_shape[1] = jax.ShapeDtypeStruct((M, N), F32)
    if slot is not None:
        out_shape, out_specs = [slot.shape()], [slot.spec(tm, tn)]
        if carried:
            aliases = {len(operands): 0}
            in_specs.append(pl.BlockSpec(memory_space=pl.ANY))
            operands.append(slot.buf)
    outs = pl.pallas_call(
        body, out_shape=out_shape, grid=(M // tm, N // tn, nk), in_specs=in_specs,
        out_specs=out_specs, scratch_shapes=[pltpu.VMEM((tm, tn), F32)] if nk > 1 else [],
        input_output_aliases=aliases, compiler_params=_params(("parallel", "parallel", "arbitrary")), name=name,
    )(*operands)
    return outs if nout > 1 else outs[0]


def matmul_swiglu(x, wg, wu, name):
    (M, K), N = x.shape, wg.shape[1]
    tm, tn = _tile(M), _tile(N)

    def k_tile(rows):
        return _k_tile(K, rows * x.dtype.itemsize + 2 * tn * wg.dtype.itemsize, 4 * rows * tn * (2 + 3))

    tk = k_tile(tm)
    if tk < K <= SINGLE_STEP_MAX_K and tm > MIN_M_TILE and M % MIN_M_TILE == 0 and k_tile(MIN_M_TILE) == K:
        tm, tk = MIN_M_TILE, K
    nk = K // tk

    def body(x_ref, g_ref, u_ref, a_ref, b_ref, c_ref, *accs):
        k = pl.program_id(2)
        xv = x_ref[...].astype(BF)

        def products():
            return (jnp.dot(xv, g_ref[...].astype(BF), preferred_element_type=F32),
                    jnp.dot(xv, u_ref[...].astype(BF), preferred_element_type=F32))

        def finish(a, b):
            a_ref[...] = a.astype(a_ref.dtype)
            b_ref[...] = b.astype(b_ref.dtype)
            c_ref[...] = (a * _sigmoid(a) * b).astype(c_ref.dtype)

        if nk == 1:
            finish(*products())
        else:
            acc_g, acc_u = accs

            @pl.when(k == 0)
            def _():
                acc_g[...] = jnp.zeros_like(acc_g)
                acc_u[...] = jnp.zeros_like(acc_u)

            pg, pu = products()
            acc_g[...] += pg
            acc_u[...] += pu

            @pl.when(k == nk - 1)
            def _():
                finish(acc_g[...], acc_u[...])

    w_spec = pl.BlockSpec((tk, tn), lambda i, j, k: (k, j))
    mn_spec = pl.BlockSpec((tm, tn), lambda i, j, k: (i, j))
    return pl.pallas_call(
        body, out_shape=[jax.ShapeDtypeStruct((M, N), BF)] * 3, grid=(M // tm, N // tn, nk),
        in_specs=[pl.BlockSpec((tm, tk), lambda i, j, k: (i, k)), w_spec, w_spec], out_specs=[mn_spec] * 3,
        scratch_shapes=[pltpu.VMEM((tm, tn), F32)] * 2 if nk > 1 else [],
        compiler_params=_params(("parallel", "parallel", "arbitrary")), name=name,
    )(x, wg, wu)


def rowwise(fn, ins, out_dtypes, name):
    rows, cols = ins[0].shape
    tr = _row_tile(rows, cols)
    nin = len(ins)

    def body(*refs):
        vals = fn(*[r[...] for r in refs[:nin]])
        for o, v in zip(refs[nin:], vals):
            o[...] = v.astype(o.dtype)

    spec = pl.BlockSpec((tr, cols), lambda i: (i, 0))
    outs = pl.pallas_call(
        body, out_shape=[jax.ShapeDtypeStruct((rows, cols), d) for d in out_dtypes],
        grid=(rows // tr,), in_specs=[spec] * nin, out_specs=[spec] * len(out_dtypes),
        compiler_params=_params(("parallel",)), name=name,
    )(*ins)
    return outs


def _ple_bwd_fn(dh, z, pp):
    gate = _sigmoid(z.astype(F32))
    return (dh * pp.astype(F32) * gate * (1.0 - gate), dh * gate)


def _merge_fn(o0, o1, o2, l0, l1, l2):
    m = jnp.maximum(jnp.maximum(l0, l1), l2)
    e0, e1, e2 = jnp.exp(l0 - m), jnp.exp(l1 - m), jnp.exp(l2 - m)
    den = e0 + e1 + e2
    return ((e0 * o0 + e1 * o1 + e2 * o2) / den, m + jnp.log(den))


def rmsnorm_fwd(x, g, name, col_block=0, width=None):
    T = x.shape[0]
    W = x.shape[1] if width is None else width
    tt = _row_tile(T, W)

    def body(x_ref, g_ref, y_ref):
        xf = x_ref[...].astype(F32)
        ms = jnp.mean(xf * xf, axis=-1, keepdims=True)
        y_ref[...] = (xf * lax.rsqrt(ms + NORM_EPS) * g_ref[...]).astype(y_ref.dtype)

    return pl.pallas_call(
        body, out_shape=jax.ShapeDtypeStruct((T, W), BF), grid=(T // tt,),
        in_specs=[pl.BlockSpec((tt, W), lambda i: (i, col_block)), pl.BlockSpec((1, W), lambda i: (0, 0))],
        out_specs=pl.BlockSpec((tt, W), lambda i: (i, 0)),
        compiler_params=_params(("parallel",)), name=name,
    )(x, g.reshape(1, W).astype(F32))


def rmsnorm_bwd(x, g, dy, name, out_dtypes, dres=None, col_block=0, width=None):
    T = x.shape[0]
    W = x.shape[1] if width is None else width
    tt = _row_tile(T, W)
    nout = len(out_dtypes)
    has_res = dres is not None

    def body(*refs):
        x_ref, g_ref, dy_ref = refs[:3]
        pos = 3
        res_ref = None
        if has_res:
            res_ref = refs[3]
            pos = 4
        dx_refs = refs[pos:pos + nout]
        dg_ref = refs[pos + nout]
        xf = x_ref[...].astype(F32)
        rstd = lax.rsqrt(jnp.mean(xf * xf, axis=-1, keepdims=True) + NORM_EPS)
        xhat = xf * rstd
        dyf = dy_ref[...].astype(F32)
        dn = dyf * g_ref[...]
        dx = rstd * (dn - xhat * jnp.mean(dn * xhat, axis=-1, keepdims=True))
        if has_res:
            dx = dx + res_ref[...]
        for o in dx_refs:
            o[...] = dx.astype(o.dtype)

        @pl.when(pl.program_id(0) == 0)
        def _():
            dg_ref[...] = jnp.zeros_like(dg_ref)

        dg_ref[...] += jnp.broadcast_to(jnp.sum(dyf * xhat, axis=0, keepdims=True), dg_ref.shape)

    row = pl.BlockSpec((tt, W), lambda i: (i, 0))
    in_specs = [pl.BlockSpec((tt, W), lambda i: (i, col_block)), pl.BlockSpec((1, W), lambda i: (0, 0)), row]
    operands = [x, g.reshape(1, W).astype(F32), dy]
    if has_res:
        in_specs.append(row)
        operands.append(dres)
    outs = pl.pallas_call(
        body,
        out_shape=[jax.ShapeDtypeStruct((T, W), d) for d in out_dtypes] + [jax.ShapeDtypeStruct((SUBLANES, W), F32)],
        grid=(T // tt,), in_specs=in_specs,
        out_specs=[row] * nout + [pl.BlockSpec((SUBLANES, W), lambda i: (0, 0))],
        compiler_params=_params(("arbitrary",)), name=name,
    )(*operands)
    return tuple(outs[:nout]) + (outs[nout][0],)


def loss_and_grad(y, target, name):
    T, D = y.shape
    tt = _row_tile(T, D)

    def body(y_ref, t_ref, loss_ref, dy_ref):
        d = y_ref[...] - t_ref[...]
        dy_ref[...] = d * (1.0 / D)

        @pl.when(pl.program_id(0) == 0)
        def _():
            loss_ref[...] = jnp.zeros_like(loss_ref)

        loss_ref[...] += jnp.full(loss_ref.shape, 0.5 / D, F32) * jnp.sum(d * d)

    row = pl.BlockSpec((tt, D), lambda i: (i, 0))
    loss, dy = pl.pallas_call(
        body, out_shape=[jax.ShapeDtypeStruct((SUBLANES, LANES), F32), jax.ShapeDtypeStruct((T, D), F32)],
        grid=(T // tt,), in_specs=[row, row],
        out_specs=[pl.BlockSpec((SUBLANES, LANES), lambda i: (0, 0)), row],
        compiler_params=_params(("arbitrary",)), name=name,
    )(y, target)
    return loss[0, 0], dy


def rope_tables(pos, width, r0, rot_dim):
    half = rot_dim // 2
    inv = ROPE_THETA ** (-jnp.arange(half, dtype=F32) * 2.0 / rot_dim)
    ang = pos.astype(F32)[:, None] * inv
    cos, sin = jnp.cos(ang), jnp.sin(ang)
    T = pos.shape[0]
    ones_l, ones_r = jnp.ones((T, r0), F32), jnp.ones((T, width - r0 - rot_dim), F32)
    c_tab = jnp.concatenate([ones_l, cos, cos, ones_r], axis=1)
    s_tab = jnp.concatenate([0 * ones_l, -sin, sin, 0 * ones_r], axis=1)
    perm = np.zeros((width, width), np.float32)
    for j in range(half):
        perm[r0 + j + half, r0 + j] = 1.0
        perm[r0 + j, r0 + j + half] = 1.0
    return c_tab, s_tab, jnp.asarray(perm, BF)


def _lane_permute(v, perm):
    hi = v.astype(BF)
    lo = (v - hi.astype(F32)).astype(BF)
    return (jnp.dot(hi, perm, preferred_element_type=F32) + jnp.dot(lo, perm, preferred_element_type=F32))


def headnorm_fwd(x, g, tabs, name, heads, col0, width, n_true):
    c_tab, s_tab, perm = tabs
    T = x.shape[0]
    tt = _tile(T, (1024, 512, 256, 128))
    inv_n = 1.0 / n_true

    def body(x_ref, g_ref, c_ref, s_ref, p_ref, y_ref):
        xf = x_ref[...].astype(F32)
        rstd = lax.rsqrt(jnp.sum(xf * xf, axis=-1, keepdims=True) * inv_n + NORM_EPS)
        n = xf * rstd * g_ref[...]
        y_ref[...] = (n * c_ref[...] + _lane_permute(n, p_ref[...]) * s_ref[...]).astype(y_ref.dtype)

    tab = pl.BlockSpec((tt, width), lambda i, h: (i, 0))
    return pl.pallas_call(
        body, out_shape=jax.ShapeDtypeStruct((T, heads * width), BF), grid=(T // tt, heads),
        in_specs=[pl.BlockSpec((tt, width), lambda i, h: (i, col0 + h)),
                  pl.BlockSpec((1, width), lambda i, h: (0, 0)), tab, tab,
                  pl.BlockSpec((width, width), lambda i, h: (0, 0))],
        out_specs=pl.BlockSpec((tt, width), lambda i, h: (i, h)),
        compiler_params=_params(("parallel", "parallel")), name=name,
    )(x, g.reshape(1, width).astype(F32), c_tab, s_tab, perm)


def headnorm_bwd(x, g, tabs, dy, name, heads, col0, width, n_true, head_sum=False, into=None):
    c_tab, s_tab, perm = tabs
    T = x.shape[0]
    tt = _tile(T, (1024, 512, 256, 128))
    inv_n = 1.0 / n_true
    buf, blocks, block0 = into if into is not None else (None, heads, 0)
    carried = buf is not None

    def body(*refs):
        x_ref, g_ref, c_ref, s_ref, p_ref, dy_ref = refs[:6]
        dx_ref, dg_ref = refs[7:9] if carried else refs[6:8]
        i, h = pl.program_id(0), pl.program_id(1)
        xf = x_ref[...].astype(F32)
        rstd = lax.rsqrt(jnp.sum(xf * xf, axis=-1, keepdims=True) * inv_n + NORM_EPS)
        xhat = xf * rstd
        dyf = dy_ref[...].astype(F32)
        dn = dyf * c_ref[...] + _lane_permute(dyf * s_ref[...], p_ref[...])
        dxh = dn * g_ref[...]
        dx = rstd * (dxh - xhat * (jnp.sum(dxh * xhat, axis=-1, keepdims=True) * inv_n))
        dx_ref[...] = dx.astype(dx_ref.dtype)

        @pl.when(jnp.logical_and(i == 0, h == 0))
        def _():
            dg_ref[...] = jnp.zeros_like(dg_ref)

        dg_ref[...] += jnp.broadcast_to(jnp.sum(dn * xhat, axis=0, keepdims=True), dg_ref.shape)
        if head_sum:
            sum_ref = refs[-1]

            @pl.when(h == 0)
            def _():
                sum_ref[...] = jnp.zeros_like(sum_ref)

            sum_ref[...] += dx

    tab = pl.BlockSpec((tt, width), lambda i, h: (i, 0))
    out_shape = [jax.ShapeDtypeStruct((T, blocks * width), BF), jax.ShapeDtypeStruct((SUBLANES, width), F32)]
    out_specs = [pl.BlockSpec((tt, width), lambda i, h: (i, block0 + h)),
                 pl.BlockSpec((SUBLANES, width), lambda i, h: (0, 0))]
    if head_sum:
        out_shape.append(jax.ShapeDtypeStruct((T, width), F32))
        out_specs.append(tab)
    in_specs = [pl.BlockSpec((tt, width), lambda i, h: (i, col0 + h)),
                pl.BlockSpec((1, width), lambda i, h: (0, 0)), tab, tab,
                pl.BlockSpec((width, width), lambda i, h: (0, 0)),
                pl.BlockSpec((tt, width), lambda i, h: (i, h))]
    operands = [x, g.reshape(1, width).astype(F32), c_tab, s_tab, perm, dy]
    if carried:
        in_specs.append(pl.BlockSpec(memory_space=pl.ANY))
        operands.append(buf)
    outs = pl.pallas_call(
        body, out_shape=out_shape, grid=(T // tt, heads), in_specs=in_specs, out_specs=out_specs,
        input_output_aliases={6: 0} if carried else {},
        compiler_params=_params(("arbitrary", "arbitrary")), name=name,
    )(*operands)
    return (outs[0], outs[1][0]) + ((outs[2],) if head_sum else ())


class Attn:
    def __init__(self, T, dil, hq, group, qc, q0, kc, k0, vc, v0, vstride, dqk, scale, half_window, blk, oblk=None):
        self.T, self.dil, self.hq, self.group = T, dil, hq, group
        self.hkv = hq // group
        self.qc, self.q0, self.kc, self.k0, self.vc, self.v0, self.vstride = qc, q0, kc, k0, vc, v0, vstride
        self.dqk, self.scale, self.hw = dqk, scale, half_window
        self.len = T // dil
        self.blk = min(blk, self.len)
        self.nb = self.len // self.blk
        self.band = half_window is not None
        self.oblk = self.blk if self.band or oblk is None else min(oblk, self.len)
        self.steps = 3 if self.band else self.len // self.oblk

    def other(self, i, s):
        if self.band:
            nom = i - 1 + s
            return jnp.minimum(jnp.maximum(nom, 0), self.nb - 1), nom
        return s, s

    def chains(self, a):
        return a.reshape(self.len, self.dil * a.shape[1])

    def row_chunks(self, rows):
        assert not self.band
        sub = min(DENSE_SUB, rows)
        return [slice(c * sub, (c + 1) * sub) for c in range(rows // sub)]

    def unchain(self, a, cols):
        return a.reshape(self.T, cols)

    def mask(self, q_nom, k_nom):
        if not self.band:
            return None
        qpos = q_nom * self.blk + lax.broadcasted_iota(jnp.int32, (self.blk, self.blk), 0)
        kpos = k_nom * self.blk + lax.broadcasted_iota(jnp.int32, (self.blk, self.blk), 1)
        ok = jnp.abs(qpos - kpos) <= self.hw
        for pos in (qpos, kpos):
            ok = jnp.logical_and(ok, jnp.logical_and(pos >= 0, pos < self.len))
        return ok


def _scores(cfg, q, k, q_nom, k_nom):
    s = lax.dot_general(q, k, (((1,), (1,)), ((), ())), preferred_element_type=F32) * cfg.scale
    ok = cfg.mask(q_nom, k_nom)
    return s if ok is None else jnp.where(ok, s, NEG)


def flash_fwd(cfg, q, k, v, name, out_dtype, sink=None):
    blk, dqk = cfg.blk, cfg.dqk
    has_sink = sink is not None

    def body(*refs):
        if has_sink:
            sink_ref, refs = refs[0], refs[1:]
        q_ref, k_ref, v_ref, o_ref, lse_ref, m_sc, l_sc, acc_sc = refs
        i, s = pl.program_id(2), pl.program_id(3)

        @pl.when(s == 0)
        def _():
            if has_sink:
                m_sc[...] = jnp.broadcast_to(sink_ref[0, :1, :], m_sc.shape)
                l_sc[...] = jnp.ones_like(l_sc)
            else:
                m_sc[...] = jnp.full(m_sc.shape, NEG, F32)
                l_sc[...] = jnp.zeros_like(l_sc)
            acc_sc[...] = jnp.zeros_like(acc_sc)

        _, k_nom = cfg.other(i, s)
        k, v = k_ref[...], v_ref[...]
        for rows in cfg.row_chunks(blk):
            sc = _scores(cfg, q_ref[rows, :], k, i, k_nom)
            m_prev = m_sc[rows, :]
            m_new = jnp.maximum(m_prev, jnp.max(sc, axis=-1, keepdims=True))
            p = jnp.exp(sc - m_new[:, :1])
            alpha = jnp.exp(m_prev - m_new)
            l_sc[rows, :] = alpha * l_sc[rows, :] + jnp.sum(p, axis=-1, keepdims=True)
            acc_sc[rows, :] = alpha * acc_sc[rows, :] + jnp.dot(p.astype(BF), v, preferred_element_type=F32)
            m_sc[rows, :] = m_new

        @pl.when(s == cfg.steps - 1)
        def _():
            o_ref[...] = (acc_sc[...] / l_sc[...]).astype(o_ref.dtype)
            lse_ref[...] = m_sc[...] + jnp.log(l_sc[...])

    g = cfg.group
    q_spec = pl.BlockSpec((blk, dqk), lambda r, h, i, s: (i, r * cfg.qc + cfg.q0 + h))
    k_spec = pl.BlockSpec((cfg.oblk, dqk), lambda r, h, i, s: (cfg.other(i, s)[0], r * cfg.kc + cfg.k0 + h // g))
    v_spec = pl.BlockSpec((cfg.oblk, LANES),
                          lambda r, h, i, s: (cfg.other(i, s)[0], r * cfg.vc + cfg.v0 + cfg.vstride * (h // g)))
    o_spec = pl.BlockSpec((blk, LANES), lambda r, h, i, s: (i, r * cfg.hq + h))
    in_specs = [q_spec, k_spec, v_spec]
    operands = [cfg.chains(q), cfg.chains(k), cfg.chains(v)]
    if has_sink:
        in_specs.insert(0, pl.BlockSpec((1, SUBLANES, LANES), lambda r, h, i, s: (h, 0, 0)))
        operands.insert(0, sink)
    cols = cfg.dil * cfg.hq * LANES
    o, lse = pl.pallas_call(
        body, out_shape=[jax.ShapeDtypeStruct((cfg.len, cols), out_dtype), jax.ShapeDtypeStruct((cfg.len, cols), F32)],
        grid=(cfg.dil, cfg.hq, cfg.nb, cfg.steps), in_specs=in_specs, out_specs=[o_spec, o_spec],
        scratch_shapes=[pltpu.VMEM((blk, LANES), F32)] * 3,
        compiler_params=_params(("parallel", "parallel", "parallel", "arbitrary")), name=name,
    )(*operands)
    return cfg.unchain(o, cfg.hq * LANES), cfg.unchain(lse, cfg.hq * LANES)


def flash_dq(cfg, q, k, v, do, o, lse, name, sink=None):
    blk, dqk = cfg.blk, cfg.dqk
    has_sink = sink is not None

    def body(*refs):
        if has_sink:
            sink_ref, refs = refs[0], refs[1:]
        q_ref, k_ref, v_ref, do_ref, o_ref, lse_ref = refs[:6]
        dq_ref = refs[6]
        dq_sc, delta_sc = refs[-2:]
        i, s = pl.program_id(2), pl.program_id(3)

        @pl.when(s == 0)
        def _():
            dq_sc[...] = jnp.zeros_like(dq_sc)
            delta = jnp.sum(do_ref[...].astype(F32) * o_ref[...].astype(F32), axis=-1, keepdims=True)
            delta_sc[...] = jnp.broadcast_to(delta, delta_sc.shape)

        _, k_nom = cfg.other(i, s)
        k, v = k_ref[...], v_ref[...]
        for rows in cfg.row_chunks(blk):
            sc = _scores(cfg, q_ref[rows, :], k, i, k_nom)
            p = jnp.exp(sc - lse_ref[rows, :1])
            dp = lax.dot_general(do_ref[rows, :], v, (((1,), (1,)), ((), ())), preferred_element_type=F32)
            ds = p * (dp - delta_sc[rows, :1]) * cfg.scale
            dq_sc[rows, :] += jnp.dot(ds.astype(BF), k, preferred_element_type=F32)

        @pl.when(s == cfg.steps - 1)
        def _():
            dq_ref[...] = dq_sc[...].astype(dq_ref.dtype)
            if has_sink:
                ps = jnp.exp(sink_ref[0, :1, :] - lse_ref[...])
                part = -jnp.sum(ps * delta_sc[...], axis=0, keepdims=True)
                refs[7][...] = jnp.broadcast_to(part, refs[7].shape)

    g = cfg.group
    q_spec = pl.BlockSpec((blk, dqk), lambda r, h, i, s: (i, r * cfg.qc + cfg.q0 + h))
    k_spec = pl.BlockSpec((cfg.oblk, dqk), lambda r, h, i, s: (cfg.other(i, s)[0], r * cfg.kc + cfg.k0 + h // g))
    v_spec = pl.BlockSpec((cfg.oblk, LANES),
                          lambda r, h, i, s: (cfg.other(i, s)[0], r * cfg.vc + cfg.v0 + cfg.vstride * (h // g)))
    o_spec = pl.BlockSpec((blk, LANES), lambda r, h, i, s: (i, r * cfg.hq + h))
    dq_spec = pl.BlockSpec((blk, dqk), lambda r, h, i, s: (i, r * cfg.hq + h))
    in_specs = [q_spec, k_spec, v_spec, o_spec, o_spec, o_spec]
    operands = [cfg.chains(q), cfg.chains(k), cfg.chains(v), cfg.chains(do), cfg.chains(o), cfg.chains(lse)]
    out_shape = [jax.ShapeDtypeStruct((cfg.len, cfg.dil * cfg.hq * dqk), BF)]
    out_specs = [dq_spec]
    if has_sink:
        in_specs.insert(0, pl.BlockSpec((1, SUBLANES, LANES), lambda r, h, i, s: (h, 0, 0)))
        operands.insert(0, sink)
        out_shape.append(jax.ShapeDtypeStruct((cfg.hq * cfg.nb * SUBLANES, LANES), F32))
        out_specs.append(pl.BlockSpec((SUBLANES, LANES), lambda r, h, i, s: (h * cfg.nb + i, 0)))
    outs = pl.pallas_call(
        body, out_shape=out_shape, grid=(cfg.dil, cfg.hq, cfg.nb, cfg.steps), in_specs=in_specs,
        out_specs=out_specs, scratch_shapes=[pltpu.VMEM((blk, dqk), F32), pltpu.VMEM((blk, LANES), F32)],
        compiler_params=_params(("parallel", "parallel", "parallel", "arbitrary")), name=name,
    )(*operands)
    dq = cfg.unchain(outs[0], cfg.hq * dqk)
    if has_sink:
        return dq, outs[1].reshape(cfg.hq, cfg.nb, SUBLANES, LANES)[:, :, 0, :]
    return dq


def flash_dkv(cfg, q, k, v, do, o, lse, name, out_dtype, add=None):
    blk, dqk, g, nw = cfg.blk, cfg.dqk, cfg.group, cfg.steps
    has_add = add is not None

    def body(*refs):
        k_ref, v_ref, q_ref, do_ref, o_ref, lse_ref = refs[:6]
        pos = 8 if has_add else 6
        dk_ref, dv_ref = refs[pos:pos + 2]
        dk_sc, dv_sc = refs[-2:]
        i, j = pl.program_id(2), pl.program_id(3)

        @pl.when(j == 0)
        def _():
            dk_sc[...] = jnp.zeros_like(dk_sc)
            dv_sc[...] = jnp.zeros_like(dv_sc)

        _, q_nom = cfg.other(i, j % nw)
        q, do = q_ref[...], do_ref[...]
        lse = lse_ref[:, :1]
        delta = jnp.sum(do.astype(F32) * o_ref[...].astype(F32), axis=-1, keepdims=True)
        for rows in cfg.row_chunks(blk):
            sc = _scores(cfg, q, k_ref[rows, :], q_nom, i)
            p = jnp.exp(sc - lse)
            dv_sc[rows, :] += lax.dot_general(p.astype(BF), do, (((0,), (0,)), ((), ())), preferred_element_type=F32)
            dp = lax.dot_general(do, v_ref[rows, :], (((1,), (1,)), ((), ())), preferred_element_type=F32)
            ds = p * (dp - delta) * cfg.scale
            dk_sc[rows, :] += lax.dot_general(ds.astype(BF), q, (((0,), (0,)), ((), ())), preferred_element_type=F32)

        @pl.when(j == g * nw - 1)
        def _():
            dk, dv = dk_sc[...], dv_sc[...]
            if has_add:
                dk, dv = dk + refs[6][...].astype(F32), dv + refs[7][...].astype(F32)
            dk_ref[...] = dk.astype(dk_ref.dtype)
            dv_ref[...] = dv.astype(dv_ref.dtype)

    def qrow(i, j):
        return cfg.other(i, j % nw)[0]

    k_spec = pl.BlockSpec((blk, dqk), lambda r, h, i, j: (i, r * cfg.kc + cfg.k0 + h))
    v_spec = pl.BlockSpec((blk, LANES), lambda r, h, i, j: (i, r * cfg.vc + cfg.v0 + cfg.vstride * h))
    q_spec = pl.BlockSpec((cfg.oblk, dqk), lambda r, h, i, j: (qrow(i, j), r * cfg.qc + cfg.q0 + h * g + j // nw))
    o_spec = pl.BlockSpec((cfg.oblk, LANES), lambda r, h, i, j: (qrow(i, j), r * cfg.hq + h * g + j // nw))
    dk_spec = pl.BlockSpec((blk, dqk), lambda r, h, i, j: (i, r * cfg.hkv + h))
    dv_spec = pl.BlockSpec((blk, LANES), lambda r, h, i, j: (i, r * cfg.hkv + h))
    in_specs = [k_spec, v_spec, q_spec, o_spec, o_spec, o_spec]
    operands = [cfg.chains(k), cfg.chains(v), cfg.chains(q), cfg.chains(do), cfg.chains(o), cfg.chains(lse)]
    if has_add:
        in_specs += [dk_spec, dv_spec]
        operands += [cfg.chains(add[0]), cfg.chains(add[1])]
    dk, dv = pl.pallas_call(
        body,
        out_shape=[jax.ShapeDtypeStruct((cfg.len, cfg.dil * cfg.hkv * dqk), out_dtype),
                   jax.ShapeDtypeStruct((cfg.len, cfg.dil * cfg.hkv * LANES), out_dtype)],
        grid=(cfg.dil, cfg.hkv, cfg.nb, g * nw), in_specs=in_specs, out_specs=[dk_spec, dv_spec],
        scratch_shapes=[pltpu.VMEM((blk, dqk), F32), pltpu.VMEM((blk, LANES), F32)],
        compiler_params=_params(("parallel", "parallel", "parallel", "arbitrary")), name=name,
    )(*operands)
    return cfg.unchain(dk, cfg.hkv * dqk), cfg.unchain(dv, cfg.hkv * LANES)


class Band:
    def __init__(self, T, dil, hq, group, per, qc, q0, kc, k0, vc, v0, scale, hw, blk):
        self.T, self.dil, self.hq, self.group, self.per = T, dil, hq, group, per
        self.pk = per // group
        self.hkv = hq // group
        self.scale, self.hw = scale, hw
        self.len = T // dil
        self.blk = min(blk, self.len)
        self.nb = self.len // self.blk
        self.win = self.blk + 2 * hw
        self.qcol = lambda r: (r * qc + q0) // per
        self.kcol = lambda r: (r * kc + k0) // self.pk
        self.vcol = lambda r: (r * vc + v0) // self.pk
        self.ocol = lambda r: (r * hq) // per
        self.dkcol = lambda r: (r * self.hkv) // self.pk
        assert hw <= self.blk and qc % per == 0 and q0 % per == 0 and kc % self.pk == 0 and k0 % self.pk == 0
        assert vc % self.pk == 0 and v0 % self.pk == 0

    def chains(self, a):
        return a.reshape(self.len, self.dil * a.shape[1])

    def rows3(self, width, col):
        nb = self.nb
        return [pl.BlockSpec((self.blk, width), lambda r, h, i: (jnp.maximum(i - 1, 0), col(r) + h)),
                pl.BlockSpec((self.blk, width), lambda r, h, i: (i, col(r) + h)),
                pl.BlockSpec((self.blk, width), lambda r, h, i: (jnp.minimum(i + 1, nb - 1), col(r) + h))]

    def window(self, prev, cur, nxt, j):
        cols = slice(j * LANES, (j + 1) * LANES)
        return jnp.concatenate([prev[self.blk - self.hw:, cols], cur[:, cols], nxt[:self.hw, cols]], axis=0)

    def valid(self, i, window_is_rows):
        shape = (self.win, self.blk) if window_is_rows else (self.blk, self.win)
        wdim = 0 if window_is_rows else 1
        bpos = i * self.blk + lax.broadcasted_iota(jnp.int32, shape, 1 - wdim)
        wpos = i * self.blk - self.hw + lax.broadcasted_iota(jnp.int32, shape, wdim)
        ok = jnp.abs(bpos - wpos) <= self.hw
        return jnp.logical_and(ok, jnp.logical_and(wpos >= 0, wpos < self.len))


def band_fwd(cfg, q, k, v, name, out_dtype, sink=None):
    blk, per, pk = cfg.blk, cfg.per, cfg.pk
    has_sink = sink is not None

    def body(*refs):
        if has_sink:
            sink_ref, refs = refs[0], refs[1:]
        q_ref, kp, kc, kn, vp, vc, vn, o_ref, lse_ref = refs
        ok = cfg.valid(pl.program_id(2), False)
        for j in range(per):
            jk = j // cfg.group
            if j % cfg.group == 0:
                kw = cfg.window(kp, kc, kn, jk)
                vw = cfg.window(vp, vc, vn, jk)
            cols = slice(j * LANES, (j + 1) * LANES)
            s = lax.dot_general(q_ref[:, cols], kw, (((1,), (1,)), ((), ())), preferred_element_type=F32) * cfg.scale
            s = jnp.where(ok, s, NEG)
            m = jnp.max(s, axis=-1, keepdims=True)
            if has_sink:
                sk = sink_ref[j, :1, :1]
                m = jnp.maximum(m, sk)
            e = jnp.exp(s - m)
            den = jnp.sum(e, axis=-1, keepdims=True)
            if has_sink:
                den = den + jnp.exp(sk - m)
            o = jnp.dot(e.astype(BF), vw, preferred_element_type=F32) / den
            o_ref[:, cols] = o.astype(o_ref.dtype)
            lse_ref[:, cols] = jnp.broadcast_to(m + jnp.log(den), (blk, LANES))

    q_spec = pl.BlockSpec((blk, per * LANES), lambda r, h, i: (i, cfg.qcol(r) + h))
    o_spec = pl.BlockSpec((blk, per * LANES), lambda r, h, i: (i, cfg.ocol(r) + h))
    in_specs = [q_spec] + cfg.rows3(pk * LANES, cfg.kcol) + cfg.rows3(pk * LANES, cfg.vcol)
    kc_, vc_ = cfg.chains(k), cfg.chains(v)
    operands = [cfg.chains(q), kc_, kc_, kc_, vc_, vc_, vc_]
    if has_sink:
        in_specs.insert(0, pl.BlockSpec((per, SUBLANES, LANES), lambda r, h, i: (h, 0, 0)))
        operands.insert(0, sink)
    cols = cfg.dil * cfg.hq * LANES
    o, lse = pl.pallas_call(
        body, out_shape=[jax.ShapeDtypeStruct((cfg.len, cols), out_dtype), jax.ShapeDtypeStruct((cfg.len, cols), F32)],
        grid=(cfg.dil, cfg.hq // per, cfg.nb), in_specs=in_specs, out_specs=[o_spec, o_spec],
        compiler_params=_params(("parallel", "parallel", "parallel")), name=name,
    )(*operands)
    return o.reshape(cfg.T, cfg.hq * LANES), lse.reshape(cfg.T, cfg.hq * LANES)


def band_dq(cfg, q, k, v, do, o, lse, name, sink=None):
    blk, per, pk = cfg.blk, cfg.per, cfg.pk
    has_sink = sink is not None

    def body(*refs):
        if has_sink:
            sink_ref, refs = refs[0], refs[1:]
        q_ref, kp, kc, kn, vp, vc, vn, do_ref, o_ref, lse_ref, dq_ref = refs[:11]
        ok = cfg.valid(pl.program_id(2), False)
        for j in range(per):
            jk = j // cfg.group
            if j % cfg.group == 0:
                kw = cfg.window(kp, kc, kn, jk)
                vw = cfg.window(vp, vc, vn, jk)
            cols = slice(j * LANES, (j + 1) * LANES)
            do = do_ref[:, cols]
            lse = lse_ref[:, j * LANES:j * LANES + 1]
            delta = jnp.sum(do.astype(F32) * o_ref[:, cols].astype(F32), axis=-1, keepdims=True)
            s = lax.dot_general(q_ref[:, cols], kw, (((1,), (1,)), ((), ())), preferred_element_type=F32) * cfg.scale
            p = jnp.exp(jnp.where(ok, s, NEG) - lse)
            dp = lax.dot_general(do, vw, (((1,), (1,)), ((), ())), preferred_element_type=F32)
            ds = p * (dp - delta) * cfg.scale
            dq_ref[:, cols] = jnp.dot(ds.astype(BF), kw, preferred_element_type=F32).astype(dq_ref.dtype)
            if has_sink:
                part = -jnp.sum(jnp.exp(sink_ref[j, :1, :1] - lse) * delta, axis=0, keepdims=True)
                refs[11][j * SUBLANES:(j + 1) * SUBLANES, :] = jnp.broadcast_to(part, (SUBLANES, LANES))

    q_spec = pl.BlockSpec((blk, per * LANES), lambda r, h, i: (i, cfg.qcol(r) + h))
    o_spec = pl.BlockSpec((blk, per * LANES), lambda r, h, i: (i, cfg.ocol(r) + h))
    in_specs = [q_spec] + cfg.rows3(pk * LANES, cfg.kcol) + cfg.rows3(pk * LANES, cfg.vcol) + [o_spec] * 3
    kc_, vc_ = cfg.chains(k), cfg.chains(v)
    operands = [cfg.chains(q), kc_, kc_, kc_, vc_, vc_, vc_, cfg.chains(do), cfg.chains(o), cfg.chains(lse)]
    out_shape = [jax.ShapeDtypeStruct((cfg.len, cfg.dil * cfg.hq * LANES), BF)]
    out_specs = [o_spec]
    if has_sink:
        in_specs.insert(0, pl.BlockSpec((per, SUBLANES, LANES), lambda r, h, i: (h, 0, 0)))
        operands.insert(0, sink)
        out_shape.append(jax.ShapeDtypeStruct((cfg.hq // per, cfg.nb, per * SUBLANES, LANES), F32))
        out_specs.append(pl.BlockSpec((None, None, per * SUBLANES, LANES), lambda r, h, i: (h, i, 0, 0)))
    outs = pl.pallas_call(
        body, out_shape=out_shape, grid=(cfg.dil, cfg.hq // per, cfg.nb), in_specs=in_specs, out_specs=out_specs,
        compiler_params=_params(("parallel", "parallel", "parallel")), name=name,
    )(*operands)
    dq = outs[0].reshape(cfg.T, cfg.hq * LANES)
    return (dq, outs[1]) if has_sink else dq


def band_dkv(cfg, q, k, v, do, o, lse, name, out_dtype, add=None, dv_into=None):
    blk, per, pk, group = cfg.blk, cfg.per, cfg.pk, cfg.group
    has_add = add is not None
    carried = dv_into is not None
    assert not carried or cfg.dil == 1

    def body(*refs):
        k_ref, v_ref = refs[:2]
        qs, dos, os_, lses = refs[2:5], refs[5:8], refs[8:11], refs[11:14]
        pos = 14 + (2 if has_add else 0) + (1 if carried else 0)
        dk_ref, dv_ref = refs[pos:pos + 2]
        ok = cfg.valid(pl.program_id(2), True)
        for jk in range(pk):
            kcols = slice(jk * LANES, (jk + 1) * LANES)
            kt, vt = k_ref[:, kcols], v_ref[:, kcols]
            dk = jnp.zeros((blk, LANES), F32)
            dv = jnp.zeros((blk, LANES), F32)
            for g in range(group):
                j = jk * group + g
                qw = cfg.window(*qs, j)
                dow = cfg.window(*dos, j)
                lse = cfg.window(*lses, j)[:, :1]
                delta = jnp.sum(dow.astype(F32) * cfg.window(*os_, j).astype(F32), axis=-1, keepdims=True)
                s = lax.dot_general(qw, kt, (((1,), (1,)), ((), ())), preferred_element_type=F32) * cfg.scale
                p = jnp.exp(jnp.where(ok, s, NEG) - lse)
                dv = dv + lax.dot_general(p.astype(BF), dow, (((0,), (0,)), ((), ())), preferred_element_type=F32)
                dp = lax.dot_general(dow, vt, (((1,), (1,)), ((), ())), preferred_element_type=F32)
                ds = p * (dp - delta) * cfg.scale
                dk = dk + lax.dot_general(ds.astype(BF), qw, (((0,), (0,)), ((), ())), preferred_element_type=F32)
            if has_add:
                dk, dv = dk + refs[14][:, kcols].astype(F32), dv + refs[15][:, kcols].astype(F32)
            dk_ref[:, kcols] = dk.astype(dk_ref.dtype)
            dv_ref[:, kcols] = dv.astype(dv_ref.dtype)

    k_spec = pl.BlockSpec((blk, pk * LANES), lambda r, h, i: (i, cfg.kcol(r) + h))
    v_spec = pl.BlockSpec((blk, pk * LANES), lambda r, h, i: (i, cfg.vcol(r) + h))
    d_spec = pl.BlockSpec((blk, pk * LANES), lambda r, h, i: (i, cfg.dkcol(r) + h))
    in_specs = [k_spec, v_spec] + cfg.rows3(per * LANES, cfg.qcol) + cfg.rows3(per * LANES, cfg.ocol) * 3
    qc_, doc, oc, lc = cfg.chains(q), cfg.chains(do), cfg.chains(o), cfg.chains(lse)
    operands = [cfg.chains(k), cfg.chains(v), qc_, qc_, qc_, doc, doc, doc, oc, oc, oc, lc, lc, lc]
    if has_add:
        in_specs += [d_spec, d_spec]
        operands += [cfg.chains(add[0]), cfg.chains(add[1])]
    cols = cfg.dil * cfg.hkv * LANES
    out_shape = [jax.ShapeDtypeStruct((cfg.len, cols), out_dtype)] * 2
    out_specs = [d_spec, d_spec]
    aliases = {}
    if carried:
        buf, blocks, block0 = dv_into
        out_shape[1] = jax.ShapeDtypeStruct((cfg.T, blocks * LANES), BF)
        out_specs[1] = pl.BlockSpec((blk, pk * LANES), lambda r, h, i: (i, block0 // pk + h))
        aliases = {len(operands): 1}
        in_specs.append(pl.BlockSpec(memory_space=pl.ANY))
        operands.append(buf)
    dk, dv = pl.pallas_call(
        body, out_shape=out_shape, grid=(cfg.dil, cfg.hq // per, cfg.nb), in_specs=in_specs, out_specs=out_specs,
        input_output_aliases=aliases, compiler_params=_params(("parallel", "parallel", "parallel")), name=name,
    )(*operands)
    return dk.reshape(cfg.T, cfg.hkv * LANES), (dv if carried else dv.reshape(cfg.T, cfg.hkv * LANES))


HBM_SPEC = pl.BlockSpec(memory_space=pltpu.HBM)


def _place():
    x, y, c = lax.axis_index("x"), lax.axis_index("y"), lax.axis_index("c")
    chips = [(1 - x, y), (x, 1 - y), (1 - x, 1 - y)]
    return x, y, c, chips


def gather_weights(shards):
    n = len(shards)

    def body(*refs):
        ins, outs = refs[:n], refs[n:2 * n]
        send_sems, recv_sems, local_sems = refs[2 * n:]
        x, y, c, chips = _place()
        me = 2 * x + y
        sibling = (x, y, 1 - c)

        def copy(w, k, src, chip_of_block, half, to):
            return pltpu.make_async_remote_copy(
                src_ref=src, dst_ref=outs[w].at[chip_of_block, half], send_sem=send_sems.at[6 * w + k],
                recv_sem=recv_sems.at[6 * w + k], device_id=to, device_id_type=MESH)

        started = []
        local = []
        for w in range(n):
            own = pltpu.make_async_copy(ins[w], outs[w].at[me], local_sems.at[w])
            own.start()
            local.append(own)
            for j, chip in enumerate(chips):
                cp = copy(w, j, ins[w].at[c], me, c, (*chip, c))
                cp.start()
                started.append(cp)
        for w in range(n):
            for j, (cx, cy) in enumerate(chips):
                them = 2 * cx + cy
                copy(w, j, ins[w].at[c], them, c, (cx, cy, c)).wait_recv()
                fwd = copy(w, 3 + j, outs[w].at[them, c], them, c, sibling)
                fwd.start()
                started.append(fwd)
        for w in range(n):
            for j, (cx, cy) in enumerate(chips):
                copy(w, 3 + j, ins[w].at[c], 2 * cx + cy, 1 - c, sibling).wait_recv()
        for cp in started:
            cp.wait_send()
        for own in local:
            own.wait()

    return pl.pallas_call(
        body, out_shape=[jax.ShapeDtypeStruct((4,) + s.shape, s.dtype) for s in shards],
        in_specs=[HBM_SPEC] * n, out_specs=[HBM_SPEC] * n,
        scratch_shapes=[pltpu.SemaphoreType.DMA((6 * n,)), pltpu.SemaphoreType.DMA((6 * n,)),
                        pltpu.SemaphoreType.DMA((n,))],
        name="gather_weights",
    )(*shards)


def _core_index():
    return lax.axis_index("c").astype(jnp.int32).reshape(1)


def presum_core_halves(g2, core, name, ship=None):
    _, rows, cols = g2.shape
    tr = _row_tile(rows, cols, 1 << 20)
    nb = rows // tr
    g2 = g2.reshape(2 * rows, cols)
    shipping = ship is not None

    def body(*refs):
        core_ref, mine_ref, other_ref = refs[:3]
        if shipping:
            ship_ref, out_ref, landed_ref, land, send_sems, recv_sems, ici_send, ici_recv, ici_local = refs[3:]
        else:
            out_ref, land, send_sems, recv_sems = refs[3:]
        x, y, c, chips = _place()
        i = pl.program_id(0)
        if shipping:
            me = 2 * x + y

            def own():
                return pltpu.make_async_copy(ship_ref.at[me], landed_ref.at[me], ici_local.at[0])

            def to_chip(j, cx, cy):
                return pltpu.make_async_remote_copy(
                    src_ref=ship_ref.at[2 * cx + cy], dst_ref=landed_ref.at[me], send_sem=ici_send.at[j],
                    recv_sem=ici_recv.at[j], device_id=(cx, cy, c), device_id_type=MESH)

            def from_chip(j, cx, cy):
                return pltpu.make_async_remote_copy(
                    src_ref=ship_ref.at[me], dst_ref=landed_ref.at[2 * cx + cy], send_sem=ici_send.at[j],
                    recv_sem=ici_recv.at[j], device_id=(cx, cy, c), device_id_type=MESH)

            @pl.when(i == 0)
            def _():
                own().start()
                for j, (cx, cy) in enumerate(chips):
                    to_chip(j, cx, cy).start()

        slot = i % 2
        cp = pltpu.make_async_remote_copy(
            src_ref=other_ref, dst_ref=land.at[slot], send_sem=send_sems.at[slot], recv_sem=recv_sems.at[slot],
            device_id=(x, y, 1 - c), device_id_type=MESH)
        cp.start()
        cp.wait_recv()
        out_ref[...] = (mine_ref[...] + land[slot]).astype(out_ref.dtype)
        cp.wait_send()
        if shipping:
            @pl.when(i == nb - 1)
            def _():
                for j, (cx, cy) in enumerate(chips):
                    from_chip(j, cx, cy).wait_recv()
                for j, (cx, cy) in enumerate(chips):
                    to_chip(j, cx, cy).wait_send()
                own().wait()

    in_specs = [pl.BlockSpec((tr, cols), lambda i, core: (core[0] * nb + i, 0)),
                pl.BlockSpec((tr, cols), lambda i, core: ((1 - core[0]) * nb + i, 0))]
    out_specs = [pl.BlockSpec((tr, cols), lambda i, core: (i, 0))]
    out_shape = [jax.ShapeDtypeStruct((rows, cols), BF)]
    scratch = [pltpu.VMEM((2, tr, cols), F32), pltpu.SemaphoreType.DMA((2,)), pltpu.SemaphoreType.DMA((2,))]
    operands = [core, g2, g2]
    if shipping:
        in_specs.append(pl.BlockSpec(memory_space=pl.ANY))
        out_specs.append(pl.BlockSpec(memory_space=pl.ANY))
        out_shape.append(jax.ShapeDtypeStruct(ship.shape, ship.dtype))
        scratch += [pltpu.SemaphoreType.DMA((3,)), pltpu.SemaphoreType.DMA((3,)), pltpu.SemaphoreType.DMA((1,))]
        operands.append(ship)
    grid_spec = pltpu.PrefetchScalarGridSpec(
        num_scalar_prefetch=1, grid=(nb,), in_specs=in_specs, out_specs=out_specs, scratch_shapes=scratch)
    outs = pl.pallas_call(
        body, out_shape=out_shape, grid_spec=grid_spec, compiler_params=_params(("arbitrary",)), name=name,
    )(*operands)
    return (outs[0], outs[1]) if shipping else outs[0]


def sum_and_swap(landed, name):
    n, rows, cols = landed.shape
    tr = _row_tile(rows, cols)

    def body(*refs):
        slots = refs[:n]
        mine_ref, theirs_ref, out_buf, land, send_sems, recv_sems = refs[n:]
        x, y, c, _ = _place()
        slot = pl.program_id(0) % 2
        tot = slots[0][...].astype(F32)
        for r in slots[1:]:
            tot = tot + r[...].astype(F32)
        mine_ref[...] = tot
        out_buf[slot] = tot
        cp = pltpu.make_async_remote_copy(
            src_ref=out_buf.at[slot], dst_ref=land.at[slot], send_sem=send_sems.at[slot], recv_sem=recv_sems.at[slot],
            device_id=(x, y, 1 - c), device_id_type=MESH)
        cp.start()
        cp.wait_recv()
        theirs_ref[...] = land[slot]
        cp.wait_send()

    specs = [pl.BlockSpec((None, tr, cols), functools.partial(lambda s, i: (s, i, 0), s)) for s in range(n)]
    row = pl.BlockSpec((tr, cols), lambda i: (i, 0))
    return pl.pallas_call(
        body, out_shape=[jax.ShapeDtypeStruct((rows, cols), F32)] * 2, grid=(rows // tr,), in_specs=specs,
        out_specs=[row, row],
        scratch_shapes=[pltpu.VMEM((2, tr, cols), F32), pltpu.VMEM((2, tr, cols), F32),
                        pltpu.SemaphoreType.DMA((2,)), pltpu.SemaphoreType.DMA((2,))],
        compiler_params=_params(("arbitrary",)), name=name,
    )(*([landed] * n))


def scatter_partials(parts):
    n = len(parts)

    def body(*refs):
        ins, outs = refs[:n], refs[n:2 * n]
        send_sems, recv_sems, local_sems = refs[2 * n:]
        x, y, c, chips = _place()
        me = 2 * x + y
        started = []
        for w in range(n):
            own = pltpu.make_async_copy(ins[w].at[me], outs[w].at[me], local_sems.at[w])
            own.start()
            started.append(own)
        sends = []
        for w in range(n):
            for j, (cx, cy) in enumerate(chips):
                cp = pltpu.make_async_remote_copy(
                    src_ref=ins[w].at[2 * cx + cy], dst_ref=outs[w].at[me], send_sem=send_sems.at[3 * w + j],
                    recv_sem=recv_sems.at[3 * w + j], device_id=(cx, cy, c), device_id_type=MESH)
                cp.start()
                sends.append(cp)
        for w in range(n):
            for j, (cx, cy) in enumerate(chips):
                pltpu.make_async_remote_copy(
                    src_ref=ins[w].at[me], dst_ref=outs[w].at[2 * cx + cy], send_sem=send_sems.at[3 * w + j],
                    recv_sem=recv_sems.at[3 * w + j], device_id=(cx, cy, c), device_id_type=MESH).wait_recv()
        for cp in sends:
            cp.wait_send()
        for own in started:
            own.wait()

    return pl.pallas_call(
        body, out_shape=[jax.ShapeDtypeStruct(p.shape, p.dtype) for p in parts],
        in_specs=[HBM_SPEC] * n, out_specs=[HBM_SPEC] * n,
        scratch_shapes=[pltpu.SemaphoreType.DMA((3 * n,)), pltpu.SemaphoreType.DMA((3 * n,)),
                        pltpu.SemaphoreType.DMA((n,))],
        name="scatter_partials",
    )(*parts)


def adamw_halves(w, mine, theirs, m, v, core, name):
    rows, cols = w.shape
    tr = _row_tile(rows // 2, cols, 1 << 18)
    nh = rows // 2 // tr

    def body(core_ref, w_ref, a_ref, b_ref, m_ref, v_ref, g_out, d_out, m_out, v_out):
        g = jnp.where(pl.program_id(0) // nh == core_ref[0], a_ref[...], b_ref[...])
        d_out[...], m_out[...], v_out[...] = _adam_fn(w_ref[...], g, m_ref[...], v_ref[...])
        g_out[...] = g

    full = pl.BlockSpec((tr, cols), lambda i, core: (i, 0))
    half = pl.BlockSpec((tr, cols), lambda i, core: (i % nh, 0))
    grid_spec = pltpu.PrefetchScalarGridSpec(
        num_scalar_prefetch=1, grid=(rows // tr,), in_specs=[full, half, half, full, full], out_specs=[full] * 4)
    return pl.pallas_call(
        body, out_shape=[jax.ShapeDtypeStruct((rows, cols), F32)] * 4, grid_spec=grid_spec,
        compiler_params=_params(("parallel",)), name=name,
    )(core, w, mine, theirs, m, v)


def gather_small(vec):
    rows = vec.shape[0]

    def body(v_ref, out_ref, send_sems, recv_sems):
        x, y, c, _ = _place()
        me = 4 * x + 2 * y + c
        out_ref[me] = v_ref[...]
        flips = [(dx, dy, dc) for dx in (0, 1) for dy in (0, 1) for dc in (0, 1)][1:]

        def peer(f):
            return tuple(1 - a if d else a for a, d in zip((x, y, c), f))

        def copy(k, block, to):
            return pltpu.make_async_remote_copy(
                src_ref=v_ref, dst_ref=out_ref.at[block], send_sem=send_sems.at[k], recv_sem=recv_sems.at[k],
                device_id=to, device_id_type=MESH)

        sends = [copy(k, me, peer(f)) for k, f in enumerate(flips)]
        for cp in sends:
            cp.start()
        for k, f in enumerate(flips):
            px, py, pc = peer(f)
            copy(k, 4 * px + 2 * py + pc, peer(f)).wait_recv()
        for cp in sends:
            cp.wait_send()

    vm = pl.BlockSpec(memory_space=pltpu.VMEM)
    return pl.pallas_call(
        body, out_shape=jax.ShapeDtypeStruct((8, rows, SMALL_COLS), F32), in_specs=[vm], out_specs=vm,
        scratch_shapes=[pltpu.SemaphoreType.DMA((7,)), pltpu.SemaphoreType.DMA((7,))], name="gather_small",
    )(vec)


def sum_slots(a, out_dtype, name):
    n, rows, cols = a.shape
    tr = _row_tile(rows, cols)

    def body(*refs):
        tot = refs[0][...].astype(F32)
        for r in refs[1:n]:
            tot = tot + r[...].astype(F32)
        refs[n][...] = tot.astype(out_dtype)

    specs = [pl.BlockSpec((None, tr, cols), functools.partial(lambda s, i: (s, i, 0), s)) for s in range(n)]
    return pl.pallas_call(
        body, out_shape=jax.ShapeDtypeStruct((rows, cols), out_dtype), grid=(rows // tr,), in_specs=specs,
        out_specs=pl.BlockSpec((tr, cols), lambda i: (i, 0)), compiler_params=_params(("parallel",)), name=name,
    )(*([a] * n))


def _adam_fn(w, g, m, v):
    m = ADAM_B1 * m + (1.0 - ADAM_B1) * g
    v = ADAM_B2 * v + (1.0 - ADAM_B2) * (g * g)
    m_hat = m / (1.0 - ADAM_B1 ** ADAM_STEP)
    v_hat = v / (1.0 - ADAM_B2 ** ADAM_STEP)
    delta = -ADAM_LR * (m_hat / (jnp.sqrt(v_hat) + ADAM_EPS) + ADAM_WD * w)
    return delta, m, v


def adamw(w, g, m, v, name):
    return rowwise(_adam_fn, [w, g, m, v], [F32, F32, F32], name)


def _full_weight(name, gathered, local_shape):
    L, a, b = local_shape
    g = gathered.reshape((4, L, a, b))
    if SHARD_AXIS[name] == 1:
        return g.transpose(1, 0, 2, 3).reshape(L, 4 * a, b)
    return g.transpose(1, 2, 0, 3).reshape(L, a, 4 * b)


def _grad_slots(name, dw):
    L, a, b = dw.shape
    if SHARD_AXIS[name] == 1:
        s = dw.reshape(L, 4, a // 4, b).transpose(1, 0, 2, 3)
        rows, cols = L * (a // 4), b
    else:
        s = dw.reshape(L, a, 4, b // 4).transpose(2, 0, 1, 3)
        rows, cols = L * a, b // 4
    return s.reshape(4, 2, rows // 2, cols).transpose(1, 0, 2, 3)


def _attn_a(T):
    group = A_HEADS // A_KV_HEADS
    return Band(T, 1, A_HEADS, group, group, A_HEADS, 0, A_KV_HEADS, 0, A_HEADS + 2 * A_KV_HEADS,
                A_HEADS + A_KV_HEADS, 1.0 / math.sqrt(HEAD_DIM), A_HALF_WINDOW, BAND_BLOCK)


def _attn_b(T):
    return Attn(T, 1, B_HEADS, 1, B_HEADS, 0, B_HEADS, 0, 2 * B_HEADS, 1, 2, B_PAD, 1.0 / math.sqrt(B_QK), None,
                DENSE_BLOCK, DENSE_OTHER_BLOCK)


def _attn_c(T, group):
    window, dil = C_PATTERNS[group]
    return Band(T, dil, C_HEADS, 1, BAND_HEADS_PER_STEP, C_HEADS, 0, C_HEADS, 0, C_HEADS, 0,
                1.0 / math.sqrt(HEAD_DIM), window // 2 // dil, BAND_BLOCK)


def _pad_heads(a, axis_len_true, axis_len_pad):
    lead = a.shape[:-1]
    h = a.shape[-1] // axis_len_true
    a = a.reshape(lead + (h, axis_len_true))
    a = jnp.pad(a, [(0, 0)] * len(lead) + [(0, 0), (0, axis_len_pad - axis_len_true)])
    return a.reshape(lead + (h * axis_len_pad,))


def _unpad_heads(a, axis_len_true, axis_len_pad):
    lead = a.shape[:-1]
    h = a.shape[-1] // axis_len_pad
    return a.reshape(lead + (h, axis_len_pad))[..., :axis_len_true].reshape(lead + (h * axis_len_true,))


def _weight_grad(G, name, layer, a, dy, W, tag):
    layers, rows, cols = W[name].shape
    if name in SLOT_DIRECT:
        G[name] = matmul([(a, dy)], "tn", F32, tag, slot=Slot(name, layers, layer, rows, cols, 0, G.get(name)))
    else:
        G.setdefault(name, [None] * layers)[layer] = matmul([(a, dy)], "tn", F32, tag)


def _mixer_fwd(kind, slot, hn, W, S, tabs, tag):
    T = hn.shape[0]
    if kind == 0:
        cfg = _attn_a(T)
        qkv = matmul([(hn, W["a_w_in"][slot])], "nn", BF, tag + "_a_in")
        q = headnorm_fwd(qkv, W["a_q_norm"][slot], tabs["hd"], tag + "_a_qn", A_HEADS, 0, HEAD_DIM, HEAD_DIM)
        k = headnorm_fwd(qkv, W["a_k_norm"][slot], tabs["hd"], tag + "_a_kn", A_KV_HEADS, A_HEADS, HEAD_DIM, HEAD_DIM)
        sink = jnp.broadcast_to(W["a_sink"][slot][:, None, None], (A_HEADS, SUBLANES, LANES)).astype(F32)
        o, lse = band_fwd(cfg, q, k, qkv, tag + "_a_att", BF, sink=sink)
        S.update(qkv=qkv, q=q, k=k, o=o, lse=lse, sink=sink)
        return o
    if kind == 1:
        cfg = _attn_b(T)
        lat = matmul([(hn, W["b_w_in"][slot])], "nn", BF, tag + "_b_in")
        qn = rmsnorm_fwd(lat, W["b_q_lat_norm"][slot], tag + "_b_qlat", 0, B_Q_RANK)
        kvn = rmsnorm_fwd(lat, W["b_kv_lat_norm"][slot], tag + "_b_kvlat", 1, B_KV_RANK)
        qp = matmul([(qn, W["b_w_q_up_pad"][slot])], "nn", BF, tag + "_b_qup")
        kv = matmul([(kvn, W["b_w_kv_up"][slot])], "nn", BF, tag + "_b_kvup")
        k_rope = lat[:, B_Q_RANK + B_KV_RANK:]
        kpre = jnp.concatenate(
            [kv.reshape(T, B_HEADS, 2 * B_NOPE)[:, :, :B_NOPE],
             jnp.broadcast_to(k_rope[:, None, :], (T, B_HEADS, B_ROPE)),
             jnp.zeros((T, B_HEADS, B_PAD - B_QK), BF)], axis=-1).reshape(T, B_HEADS * B_PAD)
        q = headnorm_fwd(qp, W["b_q_norm_pad"][slot], tabs["b"], tag + "_b_qn", B_HEADS, 0, B_PAD, B_QK)
        k = headnorm_fwd(kpre, W["b_k_norm_pad"][slot], tabs["b"], tag + "_b_kn", B_HEADS, 0, B_PAD, B_QK)
        o, lse = flash_fwd(cfg, q, k, kv, tag + "_b_att", BF)
        S.update(lat=lat, qn=qn, kvn=kvn, qp=qp, kv=kv, kpre=kpre, q=q, k=k, o=o, lse=lse)
        return o
    qkv = matmul([(hn, W["c_w_in"][slot])], "nn", BF, tag + "_c_in")
    nq = C_GROUPS * C_HEADS
    qs = [headnorm_fwd(qkv, W["c_q_norm"][slot], tabs["hd"], f"{tag}_c_qn{g}", C_HEADS, g * C_HEADS, HEAD_DIM, HEAD_DIM)
          for g in range(C_GROUPS)]
    k = headnorm_fwd(qkv, W["c_k_norm"][slot], tabs["hd"], tag + "_c_kn", C_HEADS, nq, HEAD_DIM, HEAD_DIM)
    outs, lses = [], []
    v = qkv[:, (C_GROUPS + 1) * C_HEADS * HEAD_DIM:]
    for g in range(C_GROUPS):
        og, lg = band_fwd(_attn_c(T, g), qs[g], k, v, f"{tag}_c_att{g}", F32)
        outs.append(og)
        lses.append(lg)
    o, lse = rowwise(_merge_fn, outs + lses, [BF, F32], tag + "_c_merge")
    S.update(qkv=qkv, qs=qs, v=v, k=k, o=o, lse=lse)
    return o


def _mixer_bwd(kind, slot, hn, do, W, S, tabs, tag, G):
    T = hn.shape[0]
    if kind == 0:
        cfg = _attn_a(T)
        qkv = S["qkv"]
        dq, dsink = band_dq(cfg, S["q"], S["k"], qkv, do, S["o"], S["lse"], tag + "_a_dq", sink=S["sink"])
        blocks = A_HEADS + 2 * A_KV_HEADS
        dqkv, dgq = headnorm_bwd(qkv, W["a_q_norm"][slot], tabs["hd"], dq, tag + "_a_dqn", A_HEADS, 0, HEAD_DIM, HEAD_DIM,
                                 into=(None, blocks, 0))
        dk, dqkv = band_dkv(cfg, S["q"], S["k"], qkv, do, S["o"], S["lse"], tag + "_a_dkv", BF,
                            dv_into=(dqkv, blocks, A_HEADS + A_KV_HEADS))
        dqkv, dgk = headnorm_bwd(qkv, W["a_k_norm"][slot], tabs["hd"], dk, tag + "_a_dkn", A_KV_HEADS, A_HEADS,
                                 HEAD_DIM, HEAD_DIM, into=(dqkv, blocks, A_HEADS))
        _weight_grad(G, "a_w_in", slot, hn, dqkv, W, tag + "_a_dwin")
        G["a_q_norm"][slot], G["a_k_norm"][slot] = dgq, dgk
        parts = dsink.reshape(A_HEADS // cfg.per, cfg.nb, cfg.per, SUBLANES, LANES)[:, :, :, 0, 0]
        G["a_sink"][slot] = jnp.sum(parts, axis=1).reshape(A_HEADS)
        return matmul([(dqkv, W["a_w_in"][slot])], "nt", F32, tag + "_a_dhn")
    if kind == 1:
        cfg = _attn_b(T)
        kv = S["kv"]
        dq = flash_dq(cfg, S["q"], S["k"], kv, do, S["o"], S["lse"], tag + "_b_dq")
        dk, dv = flash_dkv(cfg, S["q"], S["k"], kv, do, S["o"], S["lse"], tag + "_b_dkv", BF)
        dqp, dgq = headnorm_bwd(S["qp"], W["b_q_norm_pad"][slot], tabs["b"], dq, tag + "_b_dqn", B_HEADS, 0, B_PAD, B_QK)
        dkp, dgk, dksum = headnorm_bwd(S["kpre"], W["b_k_norm_pad"][slot], tabs["b"], dk, tag + "_b_dkn", B_HEADS, 0,
                                       B_PAD, B_QK, head_sum=True)
        dkv = jnp.concatenate([dkp.reshape(T, B_HEADS, B_PAD)[:, :, :B_NOPE], dv.reshape(T, B_HEADS, LANES)],
                              axis=-1).reshape(T, B_HEADS * 2 * B_NOPE)
        _weight_grad(G, "b_w_kv_up", slot, S["kvn"], dkv, W, tag + "_b_dwkv")
        G["b_w_q_up"][slot] = _unpad_heads(matmul([(S["qn"], dqp)], "tn", F32, tag + "_b_dwq"), B_QK, B_PAD)
        dqn = matmul([(dqp, W["b_w_q_up_pad"][slot])], "nt", F32, tag + "_b_dqnorm")
        dkvn = matmul([(dkv, W["b_w_kv_up"][slot])], "nt", F32, tag + "_b_dkvnorm")
        dql, dg_q = rmsnorm_bwd(S["lat"], W["b_q_lat_norm"][slot], dqn, tag + "_b_dqlat", [BF], None, 0, B_Q_RANK)
        dkvl, dg_kv = rmsnorm_bwd(S["lat"], W["b_kv_lat_norm"][slot], dkvn, tag + "_b_dkvlat", [BF], None, 1, B_KV_RANK)
        dlat = jnp.concatenate([dql, dkvl, dksum[:, B_NOPE:B_QK].astype(BF)], axis=1)
        _weight_grad(G, "b_w_in", slot, hn, dlat, W, tag + "_b_dwin")
        G["b_q_norm"][slot], G["b_k_norm"][slot] = dgq[:B_QK], dgk[:B_QK]
        G["b_q_lat_norm"][slot], G["b_kv_lat_norm"][slot] = dg_q, dg_kv
        return matmul([(dlat, W["b_w_in"][slot])], "nt", F32, tag + "_b_dhn")
    qkv = S["qkv"]
    nq = C_GROUPS * C_HEADS
    blocks = (C_GROUPS + 2) * C_HEADS
    dqkv, dgq = None, 0.0
    for g in range(C_GROUPS):
        dq = band_dq(_attn_c(T, g), S["qs"][g], S["k"], S["v"], do, S["o"], S["lse"], f"{tag}_c_dq{g}")
        dqkv, dg = headnorm_bwd(qkv, W["c_q_norm"][slot], tabs["hd"], dq, f"{tag}_c_dqn{g}", C_HEADS, g * C_HEADS,
                                HEAD_DIM, HEAD_DIM, into=(dqkv, blocks, g * C_HEADS))
        dgq = dgq + dg
    acc = None
    for g in reversed(range(C_GROUPS)):
        into = (dqkv, blocks, (C_GROUPS + 1) * C_HEADS) if g == 0 else None
        acc = band_dkv(_attn_c(T, g), S["qs"][g], S["k"], S["v"], do, S["o"], S["lse"], f"{tag}_c_dkv{g}", F32,
                       add=acc, dv_into=into)
    dk, dqkv = acc
    dqkv, dgk = headnorm_bwd(qkv, W["c_k_norm"][slot], tabs["hd"], dk, tag + "_c_dkn", C_HEADS, nq, HEAD_DIM, HEAD_DIM,
                             into=(dqkv, blocks, nq))
    _weight_grad(G, "c_w_in", slot, hn, dqkv, W, tag + "_c_dwin")
    G["c_q_norm"][slot], G["c_k_norm"][slot] = dgq, dgk
    return matmul([(dqkv, W["c_w_in"][slot])], "nt", F32, tag + "_c_dhn")


MIXER_OUT = ("a_w_o", "b_w_o", "c_w_o")


def local_step(x, p, positions, loss_target, W):
    T = x.shape[0]
    tabs = {"hd": rope_tables(positions, HEAD_DIM, 0, PARTIAL_ROT), "b": rope_tables(positions, B_PAD, B_NOPE, B_ROPE)}
    W = dict(W)
    W["b_w_q_up_pad"] = _pad_heads(W["b_w_q_up"], B_QK, B_PAD)
    W["b_q_norm_pad"] = _pad_heads(W["b_q_norm"], B_QK, B_PAD)
    W["b_k_norm_pad"] = _pad_heads(W["b_k_norm"], B_QK, B_PAD)
    saved = []
    h = x
    for i in range(DEPTH):
        kind, slot = i % 3, i // 3
        tag = f"l{i}"
        S = {"h0": h}
        hn = rmsnorm_fwd(h, W["g_mix"][i], tag + "_mixnorm")
        o = _mixer_fwd(kind, slot, hn, W, S, tabs, tag)
        h1 = matmul([(o, W[MIXER_OUT[kind]][slot])], "nn", F32, tag + "_mixout", res=h)
        hn2 = rmsnorm_fwd(h1, W["g_ffn"][i], tag + "_ffnnorm")
        a, b, c = matmul_swiglu(hn2, W["w_ffn_gate"][i], W["w_ffn_up"][i], tag + "_gateup")
        h2 = matmul([(c, W["w_ffn_down"][i])], "nn", F32, tag + "_down", res=h1)
        hn3 = rmsnorm_fwd(h2, W["g_ple"][i], tag + "_plenorm")
        p_i = p[i].astype(BF)
        pp = matmul([(p_i, W["w_ple_proj"][i])], "nn", BF, tag + "_pleproj")
        z, h3 = matmul([(hn3, W["w_ple_gate"][i])], "nn", BF, tag + "_plegate", ple=(h2, pp))
        S.update(hn=hn, h1=h1, hn2=hn2, a=a, b=b, c=c, h2=h2, hn3=hn3, z=z, pp=pp, p=p_i)
        saved.append(S)
        h = h3

    loss, dh = loss_and_grad(h, loss_target, "loss")
    G = {n: [None] * W[n].shape[0] for n in SMALL + ("b_w_q_up",)}
    for i in reversed(range(DEPTH)):
        kind, slot = i % 3, i // 3
        tag = f"l{i}"
        S = saved[i]
        dz, dpp = rowwise(_ple_bwd_fn, [dh, S["z"], S["pp"]], [BF, BF], tag + "_dple")
        _weight_grad(G, "w_ple_proj", i, S["p"], dpp, W, tag + "_dwpleproj")
        _weight_grad(G, "w_ple_gate", i, S["hn3"], dz, W, tag + "_dwplegate")
        dhn3 = matmul([(dz, W["w_ple_gate"][i])], "nt", F32, tag + "_dplenorm")
        dh2, dh2b, G["g_ple"][i] = rmsnorm_bwd(S["h2"], W["g_ple"][i], dhn3, tag + "_dple_norm", [F32, BF], dres=dh)
        da, db = matmul([(dh2b, W["w_ffn_down"][i])], "nt", BF, tag + "_dswiglu", swiglu=(S["a"], S["b"]))
        _weight_grad(G, "w_ffn_down", i, S["c"], dh2b, W, tag + "_dwdown")
        _weight_grad(G, "w_ffn_gate", i, S["hn2"], da, W, tag + "_dwgate")
        _weight_grad(G, "w_ffn_up", i, S["hn2"], db, W, tag + "_dwup")
        dhn2 = matmul([(da, W["w_ffn_gate"][i]), (db, W["w_ffn_up"][i])], "nt", F32, tag + "_dffnnorm")
        dh1, dh1b, G["g_ffn"][i] = rmsnorm_bwd(S["h1"], W["g_ffn"][i], dhn2, tag + "_dffn_norm", [F32, BF], dres=dh2)
        wo = W[MIXER_OUT[kind]][slot]
        do = matmul([(dh1b, wo)], "nt", BF, tag + "_dmixout")
        _weight_grad(G, MIXER_OUT[kind], slot, S["o"], dh1b, W, tag + "_dwmixout")
        dhn = _mixer_bwd(kind, slot, S["hn"], do, W, S, tabs, tag, G)
        dh, G["g_mix"][i] = rmsnorm_bwd(S["h0"], W["g_mix"][i], dhn, tag + "_dmix_norm", [F32], dres=dh1)
    return loss, dh, G


def _pack_small(vals):
    flat = jnp.concatenate([vals[n].reshape(-1).astype(F32) for n in SMALL])
    rows = -(-flat.shape[0] // SMALL_COLS)
    rows = -(-rows // SUBLANES) * SUBLANES
    return jnp.pad(flat, (0, rows * SMALL_COLS - flat.shape[0])).reshape(rows, SMALL_COLS)


def _unpack_small(packed, like):
    flat = packed.reshape(-1)
    out, off = {}, 0
    for n in SMALL:
        size = like[n].size
        out[n] = flat[off:off + size].reshape(like[n].shape)
        off += size
    return out


def kernel(x, p, positions, g_mix, g_ffn, g_ple, w_ple_gate, w_ple_proj, w_ffn_gate, w_ffn_up, w_ffn_down, a_w_in, a_q_norm, a_k_norm, a_sink, a_w_o, b_w_in, b_q_lat_norm, b_kv_lat_norm, b_w_q_up, b_w_kv_up, b_q_norm, b_k_norm, b_w_o, c_w_in, c_q_norm, c_k_norm, c_w_o, loss_target, m_g_mix, m_g_ffn, m_g_ple, m_w_ple_gate, m_w_ple_proj, m_w_ffn_gate, m_w_ffn_up, m_w_ffn_down, m_a_w_in, m_a_q_norm, m_a_k_norm, m_a_sink, m_a_w_o, m_b_w_in, m_b_q_lat_norm, m_b_kv_lat_norm, m_b_w_q_up, m_b_w_kv_up, m_b_q_norm, m_b_k_norm, m_b_w_o, m_c_w_in, m_c_q_norm, m_c_k_norm, m_c_w_o, v_g_mix, v_g_ffn, v_g_ple, v_w_ple_gate, v_w_ple_proj, v_w_ffn_gate, v_w_ffn_up, v_w_ffn_down, v_a_w_in, v_a_q_norm, v_a_k_norm, v_a_sink, v_a_w_o, v_b_w_in, v_b_q_lat_norm, v_b_kv_lat_norm, v_b_w_q_up, v_b_w_kv_up, v_b_q_norm, v_b_k_norm, v_b_w_o, v_c_w_in, v_c_q_norm, v_c_k_norm, v_c_w_o):
    args = dict(locals())
    w_loc = {n: args[n] for n in WEIGHTS}
    m_loc = {n: args["m_" + n] for n in WEIGHTS}
    v_loc = {n: args["v_" + n] for n in WEIGHTS}

    def halves(a):
        rows = a.shape[0] * a.shape[1]
        return a.reshape(2, rows // 2, a.shape[2])

    gathered = gather_weights([halves(w_loc[n].astype(BF)) for n in BIG])
    W = {n: _full_weight(n, g, w_loc[n].shape) for n, g in zip(BIG, gathered)}
    for n in SMALL:
        W[n] = w_loc[n]

    loss, dx, G = local_step(x[0], p[:, 0], positions[0], loss_target[0], W)
    loss = lax.psum(loss, ("x", "y", "c"))

    core = _core_index()
    landed, ready = [], None
    for n in EXCHANGE_ORDER:
        s = _grad_slots(n, jnp.stack(G[n])) if isinstance(G[n], list) else G[n]
        g2 = s.reshape(2, 4 * s.shape[2], s.shape[3])
        if ready is None:
            part = presum_core_halves(g2, core, "presum_" + n)
        else:
            part, got = presum_core_halves(g2, core, "presum_" + n, ship=ready)
            landed.append(got)
        ready = part.reshape(s.shape[1:])
    landed += scatter_partials([ready])
    halves = [sum_and_swap(a, "sum_" + n) for n, a in zip(EXCHANGE_ORDER, landed)]

    small = gather_small(_pack_small({n: jnp.stack(G[n]) for n in SMALL}))
    small_sum = sum_slots(small, F32, "sum_small")
    grads = _unpack_small(small_sum, w_loc)

    delta, new_m, new_v = {}, {}, {}
    for n, (mine, theirs) in zip(EXCHANGE_ORDER, halves):
        shape = w_loc[n].shape
        two_d = (shape[0] * shape[1], shape[2])
        g, d, m, v = adamw_halves(w_loc[n].reshape(two_d), mine, theirs, m_loc[n].reshape(two_d),
                                  v_loc[n].reshape(two_d), core, "adamw_" + n)
        grads[n], delta[n], new_m[n], new_v[n] = g.reshape(shape), d.reshape(shape), m.reshape(shape), v.reshape(shape)
    d, m, v = adamw(_pack_small(w_loc), small_sum, _pack_small(m_loc), _pack_small(v_loc), "adamw_small")
    delta.update(_unpack_small(d, w_loc))
    new_m.update(_unpack_small(m, w_loc))
    new_v.update(_unpack_small(v, w_loc))

    return (loss, dx[None], *[grads[n] for n in WEIGHTS], *[delta[n] for n in WEIGHTS],
            *[new_m[n] for n in WEIGHTS], *[new_v[n] for n in WEIGHTS])
```

```python
import functools
import math

import numpy as np
import jax
import jax.numpy as jnp
from jax import lax
from jax.experimental import pallas as pl
from jax.experimental.pallas import tpu as pltpu

F32 = jnp.float32
BF = jnp.bfloat16

D_MODEL = 2048
DEPTH = 4
HEAD_DIM = 128
ROPE_THETA = 500000.0
PARTIAL_ROT = HEAD_DIM // 4
NORM_EPS = 1e-6
NEG = -1e30
A_HEADS = 16
A_KV_HEADS = 4
A_HALF_WINDOW = 128
B_HEADS = 16
B_Q_RANK = 512
B_KV_RANK = 512
B_NOPE = 128
B_ROPE = 64
B_QK = B_NOPE + B_ROPE
B_PAD = 256
C_PATTERNS = ((128, 1), (512, 4), (2048, 16))
C_HEADS = 16
C_GROUPS = 3
ADAM_LR = 0.001
ADAM_B1 = 0.9
ADAM_B2 = 0.999
ADAM_EPS = 1e-08
ADAM_WD = 0.01
ADAM_STEP = 10

LANES = 128
SUBLANES = 8
VMEM_LIMIT_BYTES = 56 * 1024 * 1024
MATMUL_VMEM_BYTES = 46 * 1024 * 1024
MIN_M_TILE = 512
SINGLE_STEP_MAX_K = 2048
BAND_BLOCK = 256
BAND_HEADS_PER_STEP = 8
DENSE_BLOCK = 1024
DENSE_OTHER_BLOCK = 8192
DENSE_SUB = 256
MESH = pl.DeviceIdType.MESH

BIG = ("w_ple_gate", "w_ple_proj", "w_ffn_gate", "w_ffn_up", "w_ffn_down", "a_w_in", "a_w_o",
       "b_w_in", "b_w_q_up", "b_w_kv_up", "b_w_o", "c_w_in", "c_w_o")
SHARD_AXIS = {"w_ple_gate": 1, "w_ple_proj": 2, "w_ffn_gate": 2, "w_ffn_up": 2, "w_ffn_down": 1,
              "a_w_in": 2, "a_w_o": 1, "b_w_in": 1, "b_w_q_up": 2, "b_w_kv_up": 2, "b_w_o": 1,
              "c_w_in": 2, "c_w_o": 1}
SMALL = ("g_mix", "g_ffn", "g_ple", "a_q_norm", "a_k_norm", "a_sink", "b_q_lat_norm",
         "b_kv_lat_norm", "b_q_norm", "b_k_norm", "c_q_norm", "c_k_norm")
WEIGHTS = ("g_mix", "g_ffn", "g_ple", "w_ple_gate", "w_ple_proj", "w_ffn_gate", "w_ffn_up",
           "w_ffn_down", "a_w_in", "a_q_norm", "a_k_norm", "a_sink", "a_w_o", "b_w_in",
           "b_q_lat_norm", "b_kv_lat_norm", "b_w_q_up", "b_w_kv_up", "b_q_norm", "b_k_norm",
           "b_w_o", "c_w_in", "c_q_norm", "c_k_norm", "c_w_o")
SMALL_COLS = 1024
EXCHANGE_ORDER = ("c_w_in", "w_ffn_gate", "w_ffn_up", "w_ffn_down", "w_ple_gate", "a_w_in", "a_w_o", "b_w_o",
                  "c_w_o", "b_w_in", "w_ple_proj", "b_w_kv_up", "b_w_q_up")
SLOT_DIRECT = ("w_ple_gate", "w_ple_proj", "w_ffn_gate", "w_ffn_up", "w_ffn_down")


def _params(semantics):
    return pltpu.CompilerParams(dimension_semantics=semantics, vmem_limit_bytes=VMEM_LIMIT_BYTES)


def _tile(dim, cands=(1024, 1408, 512, 256, 128)):
    for c in cands:
        if dim % c == 0:
            return c
    return dim


def _k_tile(K, bytes_per_k, fixed_bytes):
    for t in (4096, 2816, 2048, 1408, 1024, 512, 256, 128):
        if K % t == 0 and 2 * bytes_per_k * t + fixed_bytes <= MATMUL_VMEM_BYTES:
            return t
    return _tile(K, (128,))


def _row_tile(rows, cols, target_elems=1 << 19):
    best = None
    for t in range(16, rows + 1, 16):
        if rows % t == 0 and t * cols <= target_elems:
            best = t
    return best if best is not None else rows


def _sigmoid(x):
    return 1.0 / (1.0 + jnp.exp(-x))


class Slot:
    def __init__(self, name, layers, layer, rows, cols, col0=0, buf=None):
        self.axis, self.layers, self.layer, self.rows, self.cols, self.col0, self.buf = (
            SHARD_AXIS[name], layers, layer, rows, cols, col0, buf)
        self.srows = rows // 4 if self.axis == 1 else rows
        self.scols = cols if self.axis == 1 else cols // 4
        self.half = layers * self.srows // 2

    def tiles(self, ncols):
        tm = _tile(math.gcd(self.srows, self.half))
        tn = _tile(math.gcd(self.scols, math.gcd(self.col0, ncols)))
        return tm, tn

    def spec(self, tm, tn):
        def index(i, j, k):
            row, col = i * tm, self.col0 + j * tn
            chip = row // self.srows if self.axis == 1 else col // self.scols
            flat = self.layer * self.srows + (row % self.srows if self.axis == 1 else row)
            cb = col // tn if self.axis == 1 else (col % self.scols) // tn
            return flat // self.half, chip, (flat % self.half) // tm, cb

        return pl.BlockSpec((None, None, tm, tn), index)

    def shape(self):
        return jax.ShapeDtypeStruct((2, 4, self.half, self.scols), F32)


def matmul(pairs, mode, out_dtype, name, res=None, swiglu=None, ple=None, slot=None):
    a0, b0 = pairs[0]
    if mode == "nn":
        (M, K), N = a0.shape, b0.shape[1]
    elif mode == "nt":
        (M, K), N = a0.shape, b0.shape[0]
    else:
        (K, M), N = a0.shape, b0.shape[1]
    tm, tn = (_tile(M), _tile(N)) if slot is None else slot.tiles(N)
    n_mn = 2 + (0 if res is None else 2) + (0 if swiglu is None else 2) + (0 if ple is None else 4)

    def k_tile(rows):
        return _k_tile(K, sum(rows * a.dtype.itemsize + tn * b.dtype.itemsize for a, b in pairs),
                       4 * rows * tn * (1 + n_mn))

    tk = k_tile(tm)
    if (slot is None and tk < K <= SINGLE_STEP_MAX_K and tm > MIN_M_TILE and M % MIN_M_TILE == 0
            and k_tile(MIN_M_TILE) == K):
        tm, tk = MIN_M_TILE, K
    nk = K // tk
    if mode == "nn":
        a_spec = pl.BlockSpec((tm, tk), lambda i, j, k: (i, k))
        b_spec = pl.BlockSpec((tk, tn), lambda i, j, k: (k, j))
        dims = (((1,), (0,)), ((), ()))
    elif mode == "nt":
        a_spec = pl.BlockSpec((tm, tk), lambda i, j, k: (i, k))
        b_spec = pl.BlockSpec((tn, tk), lambda i, j, k: (j, k))
        dims = (((1,), (1,)), ((), ()))
    else:
        a_spec = pl.BlockSpec((tk, tm), lambda i, j, k: (k, i))
        b_spec = pl.BlockSpec((tk, tn), lambda i, j, k: (k, j))
        dims = (((0,), (0,)), ((), ()))
    mn_spec = pl.BlockSpec((tm, tn), lambda i, j, k: (i, j))
    npairs = len(pairs)
    extras = [] if res is None else [res]
    if swiglu is not None:
        extras = list(swiglu)
    if ple is not None:
        extras = list(ple)
    nex = len(extras)
    nout = 2 if (swiglu is not None or ple is not None) else 1
    carried = slot is not None and slot.buf is not None

    def body(*refs):
        ins = refs[:2 * npairs]
        ex = refs[2 * npairs:2 * npairs + nex]
        first_out = 2 * npairs + nex + (1 if carried else 0)
        outs = refs[first_out:first_out + nout]
        k = pl.program_id(2)

        def product():
            part = None
            for p in range(npairs):
                d = lax.dot_general(ins[2 * p][...].astype(BF), ins[2 * p + 1][...].astype(BF), dims,
                                    preferred_element_type=F32)
                part = d if part is None else part + d
            return part

        def finish(r):
            if swiglu is not None:
                a = ex[0][...].astype(F32)
                b = ex[1][...].astype(F32)
                sg = _sigmoid(a)
                outs[0][...] = (r * b * (sg * (1.0 + a * (1.0 - sg)))).astype(out_dtype)
                outs[1][...] = (r * (a * sg)).astype(out_dtype)
            elif ple is not None:
                outs[0][...] = r.astype(out_dtype)
                outs[1][...] = ex[0][...] + _sigmoid(r) * ex[1][...].astype(F32)
            elif res is not None:
                outs[0][...] = (ex[0][...] + r).astype(out_dtype)
            else:
                outs[0][...] = r.astype(outs[0].dtype)

        if nk == 1:
            finish(product())
        else:
            acc = refs[-1]

            @pl.when(k == 0)
            def _():
                acc[...] = jnp.zeros_like(acc)

            acc[...] += product()

            @pl.when(k == nk - 1)
            def _():
                finish(acc[...])

    in_specs = []
    operands = []
    for a, b in pairs:
        in_specs += [a_spec, b_spec]
        operands += [a, b]
    in_specs += [mn_spec] * nex
    operands += extras
    out_shape = [jax.ShapeDtypeStruct((M, N), out_dtype)] * nout
    out_specs = [mn_spec] * nout
    aliases = {}
    if ple is not None:
        out_shape[1] = jax.ShapeDtypeStruct((M, N), F32)
    if slot is not None:
        out_shape, out_specs = [slot.shape()], [slot.spec(tm, tn)]
        if carried:
            aliases = {len(operands): 0}
            in_specs.append(pl.BlockSpec(memory_space=pl.ANY))
            operands.append(slot.buf)
    outs = pl.pallas_call(
        body, out_shape=out_shape, grid=(M // tm, N // tn, nk), in_specs=in_specs,
        out_specs=out_specs, scratch_shapes=[pltpu.VMEM((tm, tn), F32)] if nk > 1 else [],
        input_output_aliases=aliases, compiler_params=_params(("parallel", "parallel", "arbitrary")), name=name,
    )(*operands)
    return outs if nout > 1 else outs[0]


def matmul_swiglu(x, wg, wu, name):
    (M, K), N = x.shape, wg.shape[1]
    tm, tn = _tile(M), _tile(N)

    def k_tile(rows):
        return _k_tile(K, rows * x.dtype.itemsize + 2 * tn * wg.dtype.itemsize, 4 * rows * tn * (2 + 3))

    tk = k_tile(tm)
    if tk < K <= SINGLE_STEP_MAX_K and tm > MIN_M_TILE and M % MIN_M_TILE == 0 and k_tile(MIN_M_TILE) == K:
        tm, tk = MIN_M_TILE, K
    nk = K // tk

    def body(x_ref, g_ref, u_ref, a_ref, b_ref, c_ref, *accs):
        k = pl.program_id(2)
        xv = x_ref[...].astype(BF)

        def products():
            return (jnp.dot(xv, g_ref[...].astype(BF), preferred_element_type=F32),
                    jnp.dot(xv, u_ref[...].astype(BF), preferred_element_type=F32))

        def finish(a, b):
            a_ref[...] = a.astype(a_ref.dtype)
            b_ref[...] = b.astype(b_ref.dtype)
            c_ref[...] = (a * _sigmoid(a) * b).astype(c_ref.dtype)

        if nk == 1:
            finish(*products())
        else:
            acc_g, acc_u = accs

            @pl.when(k == 0)
            def _():
                acc_g[...] = jnp.zeros_like(acc_g)
                acc_u[...] = jnp.zeros_like(acc_u)

            pg, pu = products()
            acc_g[...] += pg
            acc_u[...] += pu

            @pl.when(k == nk - 1)
            def _():
                finish(acc_g[...], acc_u[...])

    w_spec = pl.BlockSpec((tk, tn), lambda i, j, k: (k, j))
    mn_spec = pl.BlockSpec((tm, tn), lambda i, j, k: (i, j))
    return pl.pallas_call(
        body, out_shape=[jax.ShapeDtypeStruct((M, N), BF)] * 3, grid=(M // tm, N // tn, nk),
        in_specs=[pl.BlockSpec((tm, tk), lambda i, j, k: (i, k)), w_spec, w_spec], out_specs=[mn_spec] * 3,
        scratch_shapes=[pltpu.VMEM((tm, tn), F32)] * 2 if nk > 1 else [],
        compiler_params=_params(("parallel", "parallel", "arbitrary")), name=name,
    )(x, wg, wu)


def rowwise(fn, ins, out_dtypes, name):
    rows, cols = ins[0].shape
    tr = _row_tile(rows, cols)
    nin = len(ins)

    def body(*refs):
        vals = fn(*[r[...] for r in refs[:nin]])
        for o, v in zip(refs[nin:], vals):
            o[...] = v.astype(o.dtype)

    spec = pl.BlockSpec((tr, cols), lambda i: (i, 0))
    outs = pl.pallas_call(
        body, out_shape=[jax.ShapeDtypeStruct((rows, cols), d) for d in out_dtypes],
        grid=(rows // tr,), in_specs=[spec] * nin, out_specs=[spec] * len(out_dtypes),
        compiler_params=_params(("parallel",)), name=name,
    )(*ins)
    return outs


def _ple_bwd_fn(dh, z, pp):
    gate = _sigmoid(z.astype(F32))
    return (dh * pp.astype(F32) * gate * (1.0 - gate), dh * gate)


def _merge_fn(o0, o1, o2, l0, l1, l2):
    m = jnp.maximum(jnp.maximum(l0, l1), l2)
    e0, e1, e2 = jnp.exp(l0 - m), jnp.exp(l1 - m), jnp.exp(l2 - m)
    den = e0 + e1 + e2
    return ((e0 * o0 + e1 * o1 + e2 * o2) / den, m + jnp.log(den))


def rmsnorm_fwd(x, g, name, col_block=0, width=None):
    T = x.shape[0]
    W = x.shape[1] if width is None else width
    tt = _row_tile(T, W)

    def body(x_ref, g_ref, y_ref):
        xf = x_ref[...].astype(F32)
        ms = jnp.mean(xf * xf, axis=-1, keepdims=True)
        y_ref[...] = (xf * lax.rsqrt(ms + NORM_EPS) * g_ref[...]).astype(y_ref.dtype)

    return pl.pallas_call(
        body, out_shape=jax.ShapeDtypeStruct((T, W), BF), grid=(T // tt,),
        in_specs=[pl.BlockSpec((tt, W), lambda i: (i, col_block)), pl.BlockSpec((1, W), lambda i: (0, 0))],
        out_specs=pl.BlockSpec((tt, W), lambda i: (i, 0)),
        compiler_params=_params(("parallel",)), name=name,
    )(x, g.reshape(1, W).astype(F32))


def rmsnorm_bwd(x, g, dy, name, out_dtypes, dres=None, col_block=0, width=None):
    T = x.shape[0]
    W = x.shape[1] if width is None else width
    tt = _row_tile(T, W)
    nout = len(out_dtypes)
    has_res = dres is not None

    def body(*refs):
        x_ref, g_ref, dy_ref = refs[:3]
        pos = 3
        res_ref = None
        if has_res:
            res_ref = refs[3]
            pos = 4
        dx_refs = refs[pos:pos + nout]
        dg_ref = refs[pos + nout]
        xf = x_ref[...].astype(F32)
        rstd = lax.rsqrt(jnp.mean(xf * xf, axis=-1, keepdims=True) + NORM_EPS)
        xhat = xf * rstd
        dyf = dy_ref[...].astype(F32)
        dn = dyf * g_ref[...]
        dx = rstd * (dn - xhat * jnp.mean(dn * xhat, axis=-1, keepdims=True))
        if has_res:
            dx = dx + res_ref[...]
        for o in dx_refs:
            o[...] = dx.astype(o.dtype)

        @pl.when(pl.program_id(0) == 0)
        def _():
            dg_ref[...] = jnp.zeros_like(dg_ref)

        dg_ref[...] += jnp.broadcast_to(jnp.sum(dyf * xhat, axis=0, keepdims=True), dg_ref.shape)

    row = pl.BlockSpec((tt, W), lambda i: (i, 0))
    in_specs = [pl.BlockSpec((tt, W), lambda i: (i, col_block)), pl.BlockSpec((1, W), lambda i: (0, 0)), row]
    operands = [x, g.reshape(1, W).astype(F32), dy]
    if has_res:
        in_specs.append(row)
        operands.append(dres)
    outs = pl.pallas_call(
        body,
        out_shape=[jax.ShapeDtypeStruct((T, W), d) for d in out_dtypes] + [jax.ShapeDtypeStruct((SUBLANES, W), F32)],
        grid=(T // tt,), in_specs=in_specs,
        out_specs=[row] * nout + [pl.BlockSpec((SUBLANES, W), lambda i: (0, 0))],
        compiler_params=_params(("arbitrary",)), name=name,
    )(*operands)
    return tuple(outs[:nout]) + (outs[nout][0],)


def loss_and_grad(y, target, name):
    T, D = y.shape
    tt = _row_tile(T, D)

    def body(y_ref, t_ref, loss_ref, dy_ref):
        d = y_ref[...] - t_ref[...]
        dy_ref[...] = d * (1.0 / D)

        @pl.when(pl.program_id(0) == 0)
        def _():
            loss_ref[...] = jnp.zeros_like(loss_ref)

        loss_ref[...] += jnp.full(loss_ref.shape, 0.5 / D, F32) * jnp.sum(d * d)

    row = pl.BlockSpec((tt, D), lambda i: (i, 0))
    loss, dy = pl.pallas_call(
        body, out_shape=[jax.ShapeDtypeStruct((SUBLANES, LANES), F32), jax.ShapeDtypeStruct((T, D), F32)],
        grid=(T // tt,), in_specs=[row, row],
        out_specs=[pl.BlockSpec((SUBLANES, LANES), lambda i: (0, 0)), row],
        compiler_params=_params(("arbitrary",)), name=name,
    )(y, target)
    return loss[0, 0], dy


def rope_tables(pos, width, r0, rot_dim):
    half = rot_dim // 2
    inv = ROPE_THETA ** (-jnp.arange(half, dtype=F32) * 2.0 / rot_dim)
    ang = pos.astype(F32)[:, None] * inv
    cos, sin = jnp.cos(ang), jnp.sin(ang)
    T = pos.shape[0]
    ones_l, ones_r = jnp.ones((T, r0), F32), jnp.ones((T, width - r0 - rot_dim), F32)
    c_tab = jnp.concatenate([ones_l, cos, cos, ones_r], axis=1)
    s_tab = jnp.concatenate([0 * ones_l, -sin, sin, 0 * ones_r], axis=1)
    perm = np.zeros((width, width), np.float32)
    for j in range(half):
        perm[r0 + j + half, r0 + j] = 1.0
        perm[r0 + j, r0 + j + half] = 1.0
    return c_tab, s_tab, jnp.asarray(perm, BF)


def _lane_permute(v, perm):
    hi = v.astype(BF)
    lo = (v - hi.astype(F32)).astype(BF)
    return (jnp.dot(hi, perm, preferred_element_type=F32) + jnp.dot(lo, perm, preferred_element_type=F32))


def headnorm_fwd(x, g, tabs, name, heads, col0, width, n_true):
    c_tab, s_tab, perm = tabs
    T = x.shape[0]
    tt = _tile(T, (1024, 512, 256, 128))
    inv_n = 1.0 / n_true

    def body(x_ref, g_ref, c_ref, s_ref, p_ref, y_ref):
        xf = x_ref[...].astype(F32)
        rstd = lax.rsqrt(jnp.sum(xf * xf, axis=-1, keepdims=True) * inv_n + NORM_EPS)
        n = xf * rstd * g_ref[...]
        y_ref[...] = (n * c_ref[...] + _lane_permute(n, p_ref[...]) * s_ref[...]).astype(y_ref.dtype)

    tab = pl.BlockSpec((tt, width), lambda i, h: (i, 0))
    return pl.pallas_call(
        body, out_shape=jax.ShapeDtypeStruct((T, heads * width), BF), grid=(T // tt, heads),
        in_specs=[pl.BlockSpec((tt, width), lambda i, h: (i, col0 + h)),
                  pl.BlockSpec((1, width), lambda i, h: (0, 0)), tab, tab,
                  pl.BlockSpec((width, width), lambda i, h: (0, 0))],
        out_specs=pl.BlockSpec((tt, width), lambda i, h: (i, h)),
        compiler_params=_params(("parallel", "parallel")), name=name,
    )(x, g.reshape(1, width).astype(F32), c_tab, s_tab, perm)


def headnorm_bwd(x, g, tabs, dy, name, heads, col0, width, n_true, head_sum=False, into=None):
    c_tab, s_tab, perm = tabs
    T = x.shape[0]
    tt = _tile(T, (1024, 512, 256, 128))
    inv_n = 1.0 / n_true
    buf, blocks, block0 = into if into is not None else (None, heads, 0)
    carried = buf is not None

    def body(*refs):
        x_ref, g_ref, c_ref, s_ref, p_ref, dy_ref = refs[:6]
        dx_ref, dg_ref = refs[7:9] if carried else refs[6:8]
        i, h = pl.program_id(0), pl.program_id(1)
        xf = x_ref[...].astype(F32)
        rstd = lax.rsqrt(jnp.sum(xf * xf, axis=-1, keepdims=True) * inv_n + NORM_EPS)
        xhat = xf * rstd
        dyf = dy_ref[...].astype(F32)
        dn = dyf * c_ref[...] + _lane_permute(dyf * s_ref[...], p_ref[...])
        dxh = dn * g_ref[...]
        dx = rstd * (dxh - xhat * (jnp.sum(dxh * xhat, axis=-1, keepdims=True) * inv_n))
        dx_ref[...] = dx.astype(dx_ref.dtype)

        @pl.when(jnp.logical_and(i == 0, h == 0))
        def _():
            dg_ref[...] = jnp.zeros_like(dg_ref)

        dg_ref[...] += jnp.broadcast_to(jnp.sum(dn * xhat, axis=0, keepdims=True), dg_ref.shape)
        if head_sum:
            sum_ref = refs[-1]

            @pl.when(h == 0)
            def _():
                sum_ref[...] = jnp.zeros_like(sum_ref)

            sum_ref[...] += dx

    tab = pl.BlockSpec((tt, width), lambda i, h: (i, 0))
    out_shape = [jax.ShapeDtypeStruct((T, blocks * width), BF), jax.ShapeDtypeStruct((SUBLANES, width), F32)]
    out_specs = [pl.BlockSpec((tt, width), lambda i, h: (i, block0 + h)),
                 pl.BlockSpec((SUBLANES, width), lambda i, h: (0, 0))]
    if head_sum:
        out_shape.append(jax.ShapeDtypeStruct((T, width), F32))
        out_specs.append(tab)
    in_specs = [pl.BlockSpec((tt, width), lambda i, h: (i, col0 + h)),
                pl.BlockSpec((1, width), lambda i, h: (0, 0)), tab, tab,
                pl.BlockSpec((width, width), lambda i, h: (0, 0)),
                pl.BlockSpec((tt, width), lambda i, h: (i, h))]
    operands = [x, g.reshape(1, width).astype(F32), c_tab, s_tab, perm, dy]
    if carried:
        in_specs.append(pl.BlockSpec(memory_space=pl.ANY))
        operands.append(buf)
    outs = pl.pallas_call(
        body, out_shape=out_shape, grid=(T // tt, heads), in_specs=in_specs, out_specs=out_specs,
        input_output_aliases={6: 0} if carried else {},
        compiler_params=_params(("arbitrary", "arbitrary")), name=name,
    )(*operands)
    return (outs[0], outs[1][0]) + ((outs[2],) if head_sum else ())


class Attn:
    def __init__(self, T, dil, hq, group, qc, q0, kc, k0, vc, v0, vstride, dqk, scale, half_window, blk, oblk=None):
        self.T, self.dil, self.hq, self.group = T, dil, hq, group
        self.hkv = hq // group
        self.qc, self.q0, self.kc, self.k0, self.vc, self.v0, self.vstride = qc, q0, kc, k0, vc, v0, vstride
        self.dqk, self.scale, self.hw = dqk, scale, half_window
        self.len = T // dil
        self.blk = min(blk, self.len)
        self.nb = self.len // self.blk
        self.band = half_window is not None
        self.oblk = self.blk if self.band or oblk is None else min(oblk, self.len)
        self.steps = 3 if self.band else self.len // self.oblk

    def other(self, i, s):
        if self.band:
            nom = i - 1 + s
            return jnp.minimum(jnp.maximum(nom, 0), self.nb - 1), nom
        return s, s

    def chains(self, a):
        return a.reshape(self.len, self.dil * a.shape[1])

    def row_chunks(self, rows):
        assert not self.band
        sub = min(DENSE_SUB, rows)
        return [slice(c * sub, (c + 1) * sub) for c in range(rows // sub)]

    def unchain(self, a, cols):
        return a.reshape(self.T, cols)

    def mask(self, q_nom, k_nom):
        if not self.band:
            return None
        qpos = q_nom * self.blk + lax.broadcasted_iota(jnp.int32, (self.blk, self.blk), 0)
        kpos = k_nom * self.blk + lax.broadcasted_iota(jnp.int32, (self.blk, self.blk), 1)
        ok = jnp.abs(qpos - kpos) <= self.hw
        for pos in (qpos, kpos):
            ok = jnp.logical_and(ok, jnp.logical_and(pos >= 0, pos < self.len))
        return ok


def _scores(cfg, q, k, q_nom, k_nom):
    s = lax.dot_general(q, k, (((1,), (1,)), ((), ())), preferred_element_type=F32) * cfg.scale
    ok = cfg.mask(q_nom, k_nom)
    return s if ok is None else jnp.where(ok, s, NEG)


def flash_fwd(cfg, q, k, v, name, out_dtype, sink=None):
    blk, dqk = cfg.blk, cfg.dqk
    has_sink = sink is not None

    def body(*refs):
        if has_sink:
            sink_ref, refs = refs[0], refs[1:]
        q_ref, k_ref, v_ref, o_ref, lse_ref, m_sc, l_sc, acc_sc = refs
        i, s = pl.program_id(2), pl.program_id(3)

        @pl.when(s == 0)
        def _():
            if has_sink:
                m_sc[...] = jnp.broadcast_to(sink_ref[0, :1, :], m_sc.shape)
                l_sc[...] = jnp.ones_like(l_sc)
            else:
                m_sc[...] = jnp.full(m_sc.shape, NEG, F32)
                l_sc[...] = jnp.zeros_like(l_sc)
            acc_sc[...] = jnp.zeros_like(acc_sc)

        _, k_nom = cfg.other(i, s)
        k, v = k_ref[...], v_ref[...]
        for rows in cfg.row_chunks(blk):
            sc = _scores(cfg, q_ref[rows, :], k, i, k_nom)
            m_prev = m_sc[rows, :]
            m_new = jnp.maximum(m_prev, jnp.max(sc, axis=-1, keepdims=True))
            p = jnp.exp(sc - m_new[:, :1])
            alpha = jnp.exp(m_prev - m_new)
            l_sc[rows, :] = alpha * l_sc[rows, :] + jnp.sum(p, axis=-1, keepdims=True)
            acc_sc[rows, :] = alpha * acc_sc[rows, :] + jnp.dot(p.astype(BF), v, preferred_element_type=F32)
            m_sc[rows, :] = m_new

        @pl.when(s == cfg.steps - 1)
        def _():
            o_ref[...] = (acc_sc[...] / l_sc[...]).astype(o_ref.dtype)
            lse_ref[...] = m_sc[...] + jnp.log(l_sc[...])

    g = cfg.group
    q_spec = pl.BlockSpec((blk, dqk), lambda r, h, i, s: (i, r * cfg.qc + cfg.q0 + h))
    k_spec = pl.BlockSpec((cfg.oblk, dqk), lambda r, h, i, s: (cfg.other(i, s)[0], r * cfg.kc + cfg.k0 + h // g))
    v_spec = pl.BlockSpec((cfg.oblk, LANES),
                          lambda r, h, i, s: (cfg.other(i, s)[0], r * cfg.vc + cfg.v0 + cfg.vstride * (h // g)))
    o_spec = pl.BlockSpec((blk, LANES), lambda r, h, i, s: (i, r * cfg.hq + h))
    in_specs = [q_spec, k_spec, v_spec]
    operands = [cfg.chains(q), cfg.chains(k), cfg.chains(v)]
    if has_sink:
        in_specs.insert(0, pl.BlockSpec((1, SUBLANES, LANES), lambda r, h, i, s: (h, 0, 0)))
        operands.insert(0, sink)
    cols = cfg.dil * cfg.hq * LANES
    o, lse = pl.pallas_call(
        body, out_shape=[jax.ShapeDtypeStruct((cfg.len, cols), out_dtype), jax.ShapeDtypeStruct((cfg.len, cols), F32)],
        grid=(cfg.dil, cfg.hq, cfg.nb, cfg.steps), in_specs=in_specs, out_specs=[o_spec, o_spec],
        scratch_shapes=[pltpu.VMEM((blk, LANES), F32)] * 3,
        compiler_params=_params(("parallel", "parallel", "parallel", "arbitrary")), name=name,
    )(*operands)
    return cfg.unchain(o, cfg.hq * LANES), cfg.unchain(lse, cfg.hq * LANES)


def flash_dq(cfg, q, k, v, do, o, lse, name, sink=None):
    blk, dqk = cfg.blk, cfg.dqk
    has_sink = sink is not None

    def body(*refs):
        if has_sink:
            sink_ref, refs = refs[0], refs[1:]
        q_ref, k_ref, v_ref, do_ref, o_ref, lse_ref = refs[:6]
        dq_ref = refs[6]
        dq_sc, delta_sc = refs[-2:]
        i, s = pl.program_id(2), pl.program_id(3)

        @pl.when(s == 0)
        def _():
            dq_sc[...] = jnp.zeros_like(dq_sc)
            delta = jnp.sum(do_ref[...].astype(F32) * o_ref[...].astype(F32), axis=-1, keepdims=True)
            delta_sc[...] = jnp.broadcast_to(delta, delta_sc.shape)

        _, k_nom = cfg.other(i, s)
        k, v = k_ref[...], v_ref[...]
        for rows in cfg.row_chunks(blk):
            sc = _scores(cfg, q_ref[rows, :], k, i, k_nom)
            p = jnp.exp(sc - lse_ref[rows, :1])
            dp = lax.dot_general(do_ref[rows, :], v, (((1,), (1,)), ((), ())), preferred_element_type=F32)
            ds = p * (dp - delta_sc[rows, :1]) * cfg.scale
            dq_sc[rows, :] += jnp.dot(ds.astype(BF), k, preferred_element_type=F32)

        @pl.when(s == cfg.steps - 1)
        def _():
            dq_ref[...] = dq_sc[...].astype(dq_ref.dtype)
            if has_sink:
                ps = jnp.exp(sink_ref[0, :1, :] - lse_ref[...])
                part = -jnp.sum(ps * delta_sc[...], axis=0, keepdims=True)
                refs[7][...] = jnp.broadcast_to(part, refs[7].shape)

    g = cfg.group
    q_spec = pl.BlockSpec((blk, dqk), lambda r, h, i, s: (i, r * cfg.qc + cfg.q0 + h))
    k_spec = pl.BlockSpec((cfg.oblk, dqk), lambda r, h, i, s: (cfg.other(i, s)[0], r * cfg.kc + cfg.k0 + h // g))
    v_spec = pl.BlockSpec((cfg.oblk, LANES),
                          lambda r, h, i, s: (cfg.other(i, s)[0], r * cfg.vc + cfg.v0 + cfg.vstride * (h // g)))
    o_spec = pl.BlockSpec((blk, LANES), lambda r, h, i, s: (i, r * cfg.hq + h))
    dq_spec = pl.BlockSpec((blk, dqk), lambda r, h, i, s: (i, r * cfg.hq + h))
    in_specs = [q_spec, k_spec, v_spec, o_spec, o_spec, o_spec]
    operands = [cfg.chains(q), cfg.chains(k), cfg.chains(v), cfg.chains(do), cfg.chains(o), cfg.chains(lse)]
    out_shape = [jax.ShapeDtypeStruct((cfg.len, cfg.dil * cfg.hq * dqk), BF)]
    out_specs = [dq_spec]
    if has_sink:
        in_specs.insert(0, pl.BlockSpec((1, SUBLANES, LANES), lambda r, h, i, s: (h, 0, 0)))
        operands.insert(0, sink)
        out_shape.append(jax.ShapeDtypeStruct((cfg.hq * cfg.nb * SUBLANES, LANES), F32))
        out_specs.append(pl.BlockSpec((SUBLANES, LANES), lambda r, h, i, s: (h * cfg.nb + i, 0)))
    outs = pl.pallas_call(
        body, out_shape=out_shape, grid=(cfg.dil, cfg.hq, cfg.nb, cfg.steps), in_specs=in_specs,
        out_specs=out_specs, scratch_shapes=[pltpu.VMEM((blk, dqk), F32), pltpu.VMEM((blk, LANES), F32)],
        compiler_params=_params(("parallel", "parallel", "parallel", "arbitrary")), name=name,
    )(*operands)
    dq = cfg.unchain(outs[0], cfg.hq * dqk)
    if has_sink:
        return dq, outs[1].reshape(cfg.hq, cfg.nb, SUBLANES, LANES)[:, :, 0, :]
    return dq


def flash_dkv(cfg, q, k, v, do, o, lse, name, out_dtype, add=None):
    blk, dqk, g, nw = cfg.blk, cfg.dqk, cfg.group, cfg.steps
    has_add = add is not None

    def body(*refs):
        k_ref, v_ref, q_ref, do_ref, o_ref, lse_ref = refs[:6]
        pos = 8 if has_add else 6
        dk_ref, dv_ref = refs[pos:pos + 2]
        dk_sc, dv_sc = refs[-2:]
        i, j = pl.program_id(2), pl.program_id(3)

        @pl.when(j == 0)
        def _():
            dk_sc[...] = jnp.zeros_like(dk_sc)
            dv_sc[...] = jnp.zeros_like(dv_sc)

        _, q_nom = cfg.other(i, j % nw)
        q, do = q_ref[...], do_ref[...]
        lse = lse_ref[:, :1]
        delta = jnp.sum(do.astype(F32) * o_ref[...].astype(F32), axis=-1, keepdims=True)
        for rows in cfg.row_chunks(blk):
            sc = _scores(cfg, q, k_ref[rows, :], q_nom, i)
            p = jnp.exp(sc - lse)
            dv_sc[rows, :] += lax.dot_general(p.astype(BF), do, (((0,), (0,)), ((), ())), preferred_element_type=F32)
            dp = lax.dot_general(do, v_ref[rows, :], (((1,), (1,)), ((), ())), preferred_element_type=F32)
            ds = p * (dp - delta) * cfg.scale
            dk_sc[rows, :] += lax.dot_general(ds.astype(BF), q, (((0,), (0,)), ((), ())), preferred_element_type=F32)

        @pl.when(j == g * nw - 1)
        def _():
            dk, dv = dk_sc[...], dv_sc[...]
            if has_add:
                dk, dv = dk + refs[6][...].astype(F32), dv + refs[7][...].astype(F32)
            dk_ref[...] = dk.astype(dk_ref.dtype)
            dv_ref[...] = dv.astype(dv_ref.dtype)

    def qrow(i, j):
        return cfg.other(i, j % nw)[0]

    k_spec = pl.BlockSpec((blk, dqk), lambda r, h, i, j: (i, r * cfg.kc + cfg.k0 + h))
    v_spec = pl.BlockSpec((blk, LANES), lambda r, h, i, j: (i, r * cfg.vc + cfg.v0 + cfg.vstride * h))
    q_spec = pl.BlockSpec((cfg.oblk, dqk), lambda r, h, i, j: (qrow(i, j), r * cfg.qc + cfg.q0 + h * g + j // nw))
    o_spec = pl.BlockSpec((cfg.oblk, LANES), lambda r, h, i, j: (qrow(i, j), r * cfg.hq + h * g + j // nw))
    dk_spec = pl.BlockSpec((blk, dqk), lambda r, h, i, j: (i, r * cfg.hkv + h))
    dv_spec = pl.BlockSpec((blk, LANES), lambda r, h, i, j: (i, r * cfg.hkv + h))
    in_specs = [k_spec, v_spec, q_spec, o_spec, o_spec, o_spec]
    operands = [cfg.chains(k), cfg.chains(v), cfg.chains(q), cfg.chains(do), cfg.chains(o), cfg.chains(lse)]
    if has_add:
        in_specs += [dk_spec, dv_spec]
        operands += [cfg.chains(add[0]), cfg.chains(add[1])]
    dk, dv = pl.pallas_call(
        body,
        out_shape=[jax.ShapeDtypeStruct((cfg.len, cfg.dil * cfg.hkv * dqk), out_dtype),
                   jax.ShapeDtypeStruct((cfg.len, cfg.dil * cfg.hkv * LANES), out_dtype)],
        grid=(cfg.dil, cfg.hkv, cfg.nb, g * nw), in_specs=in_specs, out_specs=[dk_spec, dv_spec],
        scratch_shapes=[pltpu.VMEM((blk, dqk), F32), pltpu.VMEM((blk, LANES), F32)],
        compiler_params=_params(("parallel", "parallel", "parallel", "arbitrary")), name=name,
    )(*operands)
    return cfg.unchain(dk, cfg.hkv * dqk), cfg.unchain(dv, cfg.hkv * LANES)


class Band:
    def __init__(self, T, dil, hq, group, per, qc, q0, kc, k0, vc, v0, scale, hw, blk):
        self.T, self.dil, self.hq, self.group, self.per = T, dil, hq, group, per
        self.pk = per // group
        self.hkv = hq // group
        self.scale, self.hw = scale, hw
        self.len = T // dil
        self.blk = min(blk, self.len)
        self.nb = self.len // self.blk
        self.win = self.blk + 2 * hw
        self.qcol = lambda r: (r * qc + q0) // per
        self.kcol = lambda r: (r * kc + k0) // self.pk
        self.vcol = lambda r: (r * vc + v0) // self.pk
        self.ocol = lambda r: (r * hq) // per
        self.dkcol = lambda r: (r * self.hkv) // self.pk
        assert hw <= self.blk and qc % per == 0 and q0 % per == 0 and kc % self.pk == 0 and k0 % self.pk == 0
        assert vc % self.pk == 0 and v0 % self.pk == 0

    def chains(self, a):
        return a.reshape(self.len, self.dil * a.shape[1])

    def rows3(self, width, col):
        nb = self.nb
        return [pl.BlockSpec((self.blk, width), lambda r, h, i: (jnp.maximum(i - 1, 0), col(r) + h)),
                pl.BlockSpec((self.blk, width), lambda r, h, i: (i, col(r) + h)),
                pl.BlockSpec((self.blk, width), lambda r, h, i: (jnp.minimum(i + 1, nb - 1), col(r) + h))]

    def window(self, prev, cur, nxt, j):
        cols = slice(j * LANES, (j + 1) * LANES)
        return jnp.concatenate([prev[self.blk - self.hw:, cols], cur[:, cols], nxt[:self.hw, cols]], axis=0)

    def valid(self, i, window_is_rows):
        shape = (self.win, self.blk) if window_is_rows else (self.blk, self.win)
        wdim = 0 if window_is_rows else 1
        bpos = i * self.blk + lax.broadcasted_iota(jnp.int32, shape, 1 - wdim)
        wpos = i * self.blk - self.hw + lax.broadcasted_iota(jnp.int32, shape, wdim)
        ok = jnp.abs(bpos - wpos) <= self.hw
        return jnp.logical_and(ok, jnp.logical_and(wpos >= 0, wpos < self.len))


def band_fwd(cfg, q, k, v, name, out_dtype, sink=None):
    blk, per, pk = cfg.blk, cfg.per, cfg.pk
    has_sink = sink is not None

    def body(*refs):
        if has_sink:
            sink_ref, refs = refs[0], refs[1:]
        q_ref, kp, kc, kn, vp, vc, vn, o_ref, lse_ref = refs
        ok = cfg.valid(pl.program_id(2), False)
        for j in range(per):
            jk = j // cfg.group
            if j % cfg.group == 0:
                kw = cfg.window(kp, kc, kn, jk)
                vw = cfg.window(vp, vc, vn, jk)
            cols = slice(j * LANES, (j + 1) * LANES)
            s = lax.dot_general(q_ref[:, cols], kw, (((1,), (1,)), ((), ())), preferred_element_type=F32) * cfg.scale
            s = jnp.where(ok, s, NEG)
            m = jnp.max(s, axis=-1, keepdims=True)
            if has_sink:
                sk = sink_ref[j, :1, :1]
                m = jnp.maximum(m, sk)
            e = jnp.exp(s - m)
            den = jnp.sum(e, axis=-1, keepdims=True)
            if has_sink:
                den = den + jnp.exp(sk - m)
            o = jnp.dot(e.astype(BF), vw, preferred_element_type=F32) / den
            o_ref[:, cols] = o.astype(o_ref.dtype)
            lse_ref[:, cols] = jnp.broadcast_to(m + jnp.log(den), (blk, LANES))

    q_spec = pl.BlockSpec((blk, per * LANES), lambda r, h, i: (i, cfg.qcol(r) + h))
    o_spec = pl.BlockSpec((blk, per * LANES), lambda r, h, i: (i, cfg.ocol(r) + h))
    in_specs = [q_spec] + cfg.rows3(pk * LANES, cfg.kcol) + cfg.rows3(pk * LANES, cfg.vcol)
    kc_, vc_ = cfg.chains(k), cfg.chains(v)
    operands = [cfg.chains(q), kc_, kc_, kc_, vc_, vc_, vc_]
    if has_sink:
        in_specs.insert(0, pl.BlockSpec((per, SUBLANES, LANES), lambda r, h, i: (h, 0, 0)))
        operands.insert(0, sink)
    cols = cfg.dil * cfg.hq * LANES
    o, lse = pl.pallas_call(
        body, out_shape=[jax.ShapeDtypeStruct((cfg.len, cols), out_dtype), jax.ShapeDtypeStruct((cfg.len, cols), F32)],
        grid=(cfg.dil, cfg.hq // per, cfg.nb), in_specs=in_specs, out_specs=[o_spec, o_spec],
        compiler_params=_params(("parallel", "parallel", "parallel")), name=name,
    )(*operands)
    return o.reshape(cfg.T, cfg.hq * LANES), lse.reshape(cfg.T, cfg.hq * LANES)


def band_dq(cfg, q, k, v, do, o, lse, name, sink=None):
    blk, per, pk = cfg.blk, cfg.per, cfg.pk
    has_sink = sink is not None

    def body(*refs):
        if has_sink:
            sink_ref, refs = refs[0], refs[1:]
        q_ref, kp, kc, kn, vp, vc, vn, do_ref, o_ref, lse_ref, dq_ref = refs[:11]
        ok = cfg.valid(pl.program_id(2), False)
        for j in range(per):
            jk = j // cfg.group
            if j % cfg.group == 0:
                kw = cfg.window(kp, kc, kn, jk)
                vw = cfg.window(vp, vc, vn, jk)
            cols = slice(j * LANES, (j + 1) * LANES)
            do = do_ref[:, cols]
            lse = lse_ref[:, j * LANES:j * LANES + 1]
            delta = jnp.sum(do.astype(F32) * o_ref[:, cols].astype(F32), axis=-1, keepdims=True)
            s = lax.dot_general(q_ref[:, cols], kw, (((1,), (1,)), ((), ())), preferred_element_type=F32) * cfg.scale
            p = jnp.exp(jnp.where(ok, s, NEG) - lse)
            dp = lax.dot_general(do, vw, (((1,), (1,)), ((), ())), preferred_element_type=F32)
            ds = p * (dp - delta) * cfg.scale
            dq_ref[:, cols] = jnp.dot(ds.astype(BF), kw, preferred_element_type=F32).astype(dq_ref.dtype)
            if has_sink:
                part = -jnp.sum(jnp.exp(sink_ref[j, :1, :1] - lse) * delta, axis=0, keepdims=True)
                refs[11][j * SUBLANES:(j + 1) * SUBLANES, :] = jnp.broadcast_to(part, (SUBLANES, LANES))

    q_spec = pl.BlockSpec((blk, per * LANES), lambda r, h, i: (i, cfg.qcol(r) + h))
    o_spec = pl.BlockSpec((blk, per * LANES), lambda r, h, i: (i, cfg.ocol(r) + h))
    in_specs = [q_spec] + cfg.rows3(pk * LANES, cfg.kcol) + cfg.rows3(pk * LANES, cfg.vcol) + [o_spec] * 3
    kc_, vc_ = cfg.chains(k), cfg.chains(v)
    operands = [cfg.chains(q), kc_, kc_, kc_, vc_, vc_, vc_, cfg.chains(do), cfg.chains(o), cfg.chains(lse)]
    out_shape = [jax.ShapeDtypeStruct((cfg.len, cfg.dil * cfg.hq * LANES), BF)]
    out_specs = [o_spec]
    if has_sink:
        in_specs.insert(0, pl.BlockSpec((per, SUBLANES, LANES), lambda r, h, i: (h, 0, 0)))
        operands.insert(0, sink)
        out_shape.append(jax.ShapeDtypeStruct((cfg.hq // per, cfg.nb, per * SUBLANES, LANES), F32))
        out_specs.append(pl.BlockSpec((None, None, per * SUBLANES, LANES), lambda r, h, i: (h, i, 0, 0)))
    outs = pl.pallas_call(
        body, out_shape=out_shape, grid=(cfg.dil, cfg.hq // per, cfg.nb), in_specs=in_specs, out_specs=out_specs,
        compiler_params=_params(("parallel", "parallel", "parallel")), name=name,
    )(*operands)
    dq = outs[0].reshape(cfg.T, cfg.hq * LANES)
    return (dq, outs[1]) if has_sink else dq


def band_dkv(cfg, q, k, v, do, o, lse, name, out_dtype, add=None, dv_into=None):
    blk, per, pk, group = cfg.blk, cfg.per, cfg.pk, cfg.group
    has_add = add is not None
    carried = dv_into is not None
    assert not carried or cfg.dil == 1

    def body(*refs):
        k_ref, v_ref = refs[:2]
        qs, dos, os_, lses = refs[2:5], refs[5:8], refs[8:11], refs[11:14]
        pos = 14 + (2 if has_add else 0) + (1 if carried else 0)
        dk_ref, dv_ref = refs[pos:pos + 2]
        ok = cfg.valid(pl.program_id(2), True)
        for jk in range(pk):
            kcols = slice(jk * LANES, (jk + 1) * LANES)
            kt, vt = k_ref[:, kcols], v_ref[:, kcols]
            dk = jnp.zeros((blk, LANES), F32)
            dv = jnp.zeros((blk, LANES), F32)
            for g in range(group):
                j = jk * group + g
                qw = cfg.window(*qs, j)
                dow = cfg.window(*dos, j)
                lse = cfg.window(*lses, j)[:, :1]
                delta = jnp.sum(dow.astype(F32) * cfg.window(*os_, j).astype(F32), axis=-1, keepdims=True)
                s = lax.dot_general(qw, kt, (((1,), (1,)), ((), ())), preferred_element_type=F32) * cfg.scale
                p = jnp.exp(jnp.where(ok, s, NEG) - lse)
                dv = dv + lax.dot_general(p.astype(BF), dow, (((0,), (0,)), ((), ())), preferred_element_type=F32)
                dp = lax.dot_general(dow, vt, (((1,), (1,)), ((), ())), preferred_element_type=F32)
                ds = p * (dp - delta) * cfg.scale
                dk = dk + lax.dot_general(ds.astype(BF), qw, (((0,), (0,)), ((), ())), preferred_element_type=F32)
            if has_add:
                dk, dv = dk + refs[14][:, kcols].astype(F32), dv + refs[15][:, kcols].astype(F32)
            dk_ref[:, kcols] = dk.astype(dk_ref.dtype)
            dv_ref[:, kcols] = dv.astype(dv_ref.dtype)

    k_spec = pl.BlockSpec((blk, pk * LANES), lambda r, h, i: (i, cfg.kcol(r) + h))
    v_spec = pl.BlockSpec((blk, pk * LANES), lambda r, h, i: (i, cfg.vcol(r) + h))
    d_spec = pl.BlockSpec((blk, pk * LANES), lambda r, h, i: (i, cfg.dkcol(r) + h))
    in_specs = [k_spec, v_spec] + cfg.rows3(per * LANES, cfg.qcol) + cfg.rows3(per * LANES, cfg.ocol) * 3
    qc_, doc, oc, lc = cfg.chains(q), cfg.chains(do), cfg.chains(o), cfg.chains(lse)
    operands = [cfg.chains(k), cfg.chains(v), qc_, qc_, qc_, doc, doc, doc, oc, oc, oc, lc, lc, lc]
    if has_add:
        in_specs += [d_spec, d_spec]
        operands += [cfg.chains(add[0]), cfg.chains(add[1])]
    cols = cfg.dil * cfg.hkv * LANES
    out_shape = [jax.ShapeDtypeStruct((cfg.len, cols), out_dtype)] * 2
    out_specs = [d_spec, d_spec]
    aliases = {}
    if carried:
        buf, blocks, block0 = dv_into
        out_shape[1] = jax.ShapeDtypeStruct((cfg.T, blocks * LANES), BF)
        out_specs[1] = pl.BlockSpec((blk, pk * LANES), lambda r, h, i: (i, block0 // pk + h))
        aliases = {len(operands): 1}
        in_specs.append(pl.BlockSpec(memory_space=pl.ANY))
        operands.append(buf)
    dk, dv = pl.pallas_call(
        body, out_shape=out_shape, grid=(cfg.dil, cfg.hq // per, cfg.nb), in_specs=in_specs, out_specs=out_specs,
        input_output_aliases=aliases, compiler_params=_params(("parallel", "parallel", "parallel")), name=name,
    )(*operands)
    return dk.reshape(cfg.T, cfg.hkv * LANES), (dv if carried else dv.reshape(cfg.T, cfg.hkv * LANES))


HBM_SPEC = pl.BlockSpec(memory_space=pltpu.HBM)


def _place():
    x, y, c = lax.axis_index("x"), lax.axis_index("y"), lax.axis_index("c")
    chips = [(1 - x, y), (x, 1 - y), (1 - x, 1 - y)]
    return x, y, c, chips


def gather_weights(shards):
    n = len(shards)

    def body(*refs):
        ins, outs = refs[:n], refs[n:2 * n]
        send_sems, recv_sems, local_sems = refs[2 * n:]
        x, y, c, chips = _place()
        me = 2 * x + y
        sibling = (x, y, 1 - c)

        def copy(w, k, src, chip_of_block, half, to):
            return pltpu.make_async_remote_copy(
                src_ref=src, dst_ref=outs[w].at[chip_of_block, half], send_sem=send_sems.at[6 * w + k],
                recv_sem=recv_sems.at[6 * w + k], device_id=to, device_id_type=MESH)

        started = []
        local = []
        for w in range(n):
            own = pltpu.make_async_copy(ins[w], outs[w].at[me], local_sems.at[w])
            own.start()
            local.append(own)
            for j, chip in enumerate(chips):
                cp = copy(w, j, ins[w].at[c], me, c, (*chip, c))
                cp.start()
                started.append(cp)
        for w in range(n):
            for j, (cx, cy) in enumerate(chips):
                them = 2 * cx + cy
                copy(w, j, ins[w].at[c], them, c, (cx, cy, c)).wait_recv()
                fwd = copy(w, 3 + j, outs[w].at[them, c], them, c, sibling)
                fwd.start()
                started.append(fwd)
        for w in range(n):
            for j, (cx, cy) in enumerate(chips):
                copy(w, 3 + j, ins[w].at[c], 2 * cx + cy, 1 - c, sibling).wait_recv()
        for cp in started:
            cp.wait_send()
        for own in local:
            own.wait()

    return pl.pallas_call(
        body, out_shape=[jax.ShapeDtypeStruct((4,) + s.shape, s.dtype) for s in shards],
        in_specs=[HBM_SPEC] * n, out_specs=[HBM_SPEC] * n,
        scratch_shapes=[pltpu.SemaphoreType.DMA((6 * n,)), pltpu.SemaphoreType.DMA((6 * n,)),
                        pltpu.SemaphoreType.DMA((n,))],
        name="gather_weights",
    )(*shards)


def _core_index():
    return lax.axis_index("c").astype(jnp.int32).reshape(1)


def presum_core_halves(g2, core, name, ship=None):
    _, rows, cols = g2.shape
    tr = _row_tile(rows, cols, 1 << 20)
    nb = rows // tr
    g2 = g2.reshape(2 * rows, cols)
    shipping = ship is not None

    def body(*refs):
        core_ref, mine_ref, other_ref = refs[:3]
        if shipping:
            ship_ref, out_ref, landed_ref, land, send_sems, recv_sems, ici_send, ici_recv, ici_local = refs[3:]
        else:
            out_ref, land, send_sems, recv_sems = refs[3:]
        x, y, c, chips = _place()
        i = pl.program_id(0)
        if shipping:
            me = 2 * x + y

            def own():
                return pltpu.make_async_copy(ship_ref.at[me], landed_ref.at[me], ici_local.at[0])

            def to_chip(j, cx, cy):
                return pltpu.make_async_remote_copy(
                    src_ref=ship_ref.at[2 * cx + cy], dst_ref=landed_ref.at[me], send_sem=ici_send.at[j],
                    recv_sem=ici_recv.at[j], device_id=(cx, cy, c), device_id_type=MESH)

            def from_chip(j, cx, cy):
                return pltpu.make_async_remote_copy(
                    src_ref=ship_ref.at[me], dst_ref=landed_ref.at[2 * cx + cy], send_sem=ici_send.at[j],
                    recv_sem=ici_recv.at[j], device_id=(cx, cy, c), device_id_type=MESH)

            @pl.when(i == 0)
            def _():
                own().start()
                for j, (cx, cy) in enumerate(chips):
                    to_chip(j, cx, cy).start()

        slot = i % 2
        cp = pltpu.make_async_remote_copy(
            src_ref=other_ref, dst_ref=land.at[slot], send_sem=send_sems.at[slot], recv_sem=recv_sems.at[slot],
            device_id=(x, y, 1 - c), device_id_type=MESH)
        cp.start()
        cp.wait_recv()
        out_ref[...] = (mine_ref[...] + land[slot]).astype(out_ref.dtype)
        cp.wait_send()
        if shipping:
            @pl.when(i == nb - 1)
            def _():
                for j, (cx, cy) in enumerate(chips):
                    from_chip(j, cx, cy).wait_recv()
                for j, (cx, cy) in enumerate(chips):
                    to_chip(j, cx, cy).wait_send()
                own().wait()

    in_specs = [pl.BlockSpec((tr, cols), lambda i, core: (core[0] * nb + i, 0)),
                pl.BlockSpec((tr, cols), lambda i, core: ((1 - core[0]) * nb + i, 0))]
    out_specs = [pl.BlockSpec((tr, cols), lambda i, core: (i, 0))]
    out_shape = [jax.ShapeDtypeStruct((rows, cols), BF)]
    scratch = [pltpu.VMEM((2, tr, cols), F32), pltpu.SemaphoreType.DMA((2,)), pltpu.SemaphoreType.DMA((2,))]
    operands = [core, g2, g2]
    if shipping:
        in_specs.append(pl.BlockSpec(memory_space=pl.ANY))
        out_specs.append(pl.BlockSpec(memory_space=pl.ANY))
        out_shape.append(jax.ShapeDtypeStruct(ship.shape, ship.dtype))
        scratch += [pltpu.SemaphoreType.DMA((3,)), pltpu.SemaphoreType.DMA((3,)), pltpu.SemaphoreType.DMA((1,))]
        operands.append(ship)
    grid_spec = pltpu.PrefetchScalarGridSpec(
        num_scalar_prefetch=1, grid=(nb,), in_specs=in_specs, out_specs=out_specs, scratch_shapes=scratch)
    outs = pl.pallas_call(
        body, out_shape=out_shape, grid_spec=grid_spec, compiler_params=_params(("arbitrary",)), name=name,
    )(*operands)
    return (outs[0], outs[1]) if shipping else outs[0]


def sum_and_swap(landed, name):
    n, rows, cols = landed.shape
    tr = _row_tile(rows, cols)

    def body(*refs):
        slots = refs[:n]
        mine_ref, theirs_ref, out_buf, land, send_sems, recv_sems = refs[n:]
        x, y, c, _ = _place()
        slot = pl.program_id(0) % 2
        tot = slots[0][...].astype(F32)
        for r in slots[1:]:
            tot = tot + r[...].astype(F32)
        mine_ref[...] = tot
        out_buf[slot] = tot
        cp = pltpu.make_async_remote_copy(
            src_ref=out_buf.at[slot], dst_ref=land.at[slot], send_sem=send_sems.at[slot], recv_sem=recv_sems.at[slot],
            device_id=(x, y, 1 - c), device_id_type=MESH)
        cp.start()
        cp.wait_recv()
        theirs_ref[...] = land[slot]
        cp.wait_send()

    specs = [pl.BlockSpec((None, tr, cols), functools.partial(lambda s, i: (s, i, 0), s)) for s in range(n)]
    row = pl.BlockSpec((tr, cols), lambda i: (i, 0))
    return pl.pallas_call(
        body, out_shape=[jax.ShapeDtypeStruct((rows, cols), F32)] * 2, grid=(rows // tr,), in_specs=specs,
        out_specs=[row, row],
        scratch_shapes=[pltpu.VMEM((2, tr, cols), F32), pltpu.VMEM((2, tr, cols), F32),
                        pltpu.SemaphoreType.DMA((2,)), pltpu.SemaphoreType.DMA((2,))],
        compiler_params=_params(("arbitrary",)), name=name,
    )(*([landed] * n))


def scatter_partials(parts):
    n = len(parts)

    def body(*refs):
        ins, outs = refs[:n], refs[n:2 * n]
        send_sems, recv_sems, local_sems = refs[2 * n:]
        x, y, c, chips = _place()
        me = 2 * x + y
        started = []
        for w in range(n):
            own = pltpu.make_async_copy(ins[w].at[me], outs[w].at[me], local_sems.at[w])
            own.start()
            started.append(own)
        sends = []
        for w in range(n):
            for j, (cx, cy) in enumerate(chips):
                cp = pltpu.make_async_remote_copy(
                    src_ref=ins[w].at[2 * cx + cy], dst_ref=outs[w].at[me], send_sem=send_sems.at[3 * w + j],
                    recv_sem=recv_sems.at[3 * w + j], device_id=(cx, cy, c), device_id_type=MESH)
                cp.start()
                sends.append(cp)
        for w in range(n):
            for j, (cx, cy) in enumerate(chips):
                pltpu.make_async_remote_copy(
                    src_ref=ins[w].at[me], dst_ref=outs[w].at[2 * cx + cy], send_sem=send_sems.at[3 * w + j],
                    recv_sem=recv_sems.at[3 * w + j], device_id=(cx, cy, c), device_id_type=MESH).wait_recv()
        for cp in sends:
            cp.wait_send()
        for own in started:
            own.wait()

    return pl.pallas_call(
        body, out_shape=[jax.ShapeDtypeStruct(p.shape, p.dtype) for p in parts],
        in_specs=[HBM_SPEC] * n, out_specs=[HBM_SPEC] * n,
        scratch_shapes=[pltpu.SemaphoreType.DMA((3 * n,)), pltpu.SemaphoreType.DMA((3 * n,)),
                        pltpu.SemaphoreType.DMA((n,))],
        name="scatter_partials",
    )(*parts)


def adamw_halves(w, mine, theirs, m, v, core, name):
    rows, cols = w.shape
    tr = _row_tile(rows // 2, cols, 1 << 18)
    nh = rows // 2 // tr

    def body(core_ref, w_ref, a_ref, b_ref, m_ref, v_ref, g_out, d_out, m_out, v_out):
        g = jnp.where(pl.program_id(0) // nh == core_ref[0], a_ref[...], b_ref[...])
        d_out[...], m_out[...], v_out[...] = _adam_fn(w_ref[...], g, m_ref[...], v_ref[...])
        g_out[...] = g

    full = pl.BlockSpec((tr, cols), lambda i, core: (i, 0))
    half = pl.BlockSpec((tr, cols), lambda i, core: (i % nh, 0))
    grid_spec = pltpu.PrefetchScalarGridSpec(
        num_scalar_prefetch=1, grid=(rows // tr,), in_specs=[full, half, half, full, full], out_specs=[full] * 4)
    return pl.pallas_call(
        body, out_shape=[jax.ShapeDtypeStruct((rows, cols), F32)] * 4, grid_spec=grid_spec,
        compiler_params=_params(("parallel",)), name=name,
    )(core, w, mine, theirs, m, v)


def gather_small(vec):
    rows = vec.shape[0]

    def body(v_ref, out_ref, send_sems, recv_sems):
        x, y, c, _ = _place()
        me = 4 * x + 2 * y + c
        out_ref[me] = v_ref[...]
        flips = [(dx, dy, dc) for dx in (0, 1) for dy in (0, 1) for dc in (0, 1)][1:]

        def peer(f):
            return tuple(1 - a if d else a for a, d in zip((x, y, c), f))

        def copy(k, block, to):
            return pltpu.make_async_remote_copy(
                src_ref=v_ref, dst_ref=out_ref.at[block], send_sem=send_sems.at[k], recv_sem=recv_sems.at[k],
                device_id=to, device_id_type=MESH)

        sends = [copy(k, me, peer(f)) for k, f in enumerate(flips)]
        for cp in sends:
            cp.start()
        for k, f in enumerate(flips):
            px, py, pc = peer(f)
            copy(k, 4 * px + 2 * py + pc, peer(f)).wait_recv()
        for cp in sends:
            cp.wait_send()

    vm = pl.BlockSpec(memory_space=pltpu.VMEM)
    return pl.pallas_call(
        body, out_shape=jax.ShapeDtypeStruct((8, rows, SMALL_COLS), F32), in_specs=[vm], out_specs=vm,
        scratch_shapes=[pltpu.SemaphoreType.DMA((7,)), pltpu.SemaphoreType.DMA((7,))], name="gather_small",
    )(vec)


def sum_slots(a, out_dtype, name):
    n, rows, cols = a.shape
    tr = _row_tile(rows, cols)

    def body(*refs):
        tot = refs[0][...].astype(F32)
        for r in refs[1:n]:
            tot = tot + r[...].astype(F32)
        refs[n][...] = tot.astype(out_dtype)

    specs = [pl.BlockSpec((None, tr, cols), functools.partial(lambda s, i: (s, i, 0), s)) for s in range(n)]
    return pl.pallas_call(
        body, out_shape=jax.ShapeDtypeStruct((rows, cols), out_dtype), grid=(rows // tr,), in_specs=specs,
        out_specs=pl.BlockSpec((tr, cols), lambda i: (i, 0)), compiler_params=_params(("parallel",)), name=name,
    )(*([a] * n))


def _adam_fn(w, g, m, v):
    m = ADAM_B1 * m + (1.0 - ADAM_B1) * g
    v = ADAM_B2 * v + (1.0 - ADAM_B2) * (g * g)
    m_hat = m / (1.0 - ADAM_B1 ** ADAM_STEP)
    v_hat = v / (1.0 - ADAM_B2 ** ADAM_STEP)
    delta = -ADAM_LR * (m_hat / (jnp.sqrt(v_hat) + ADAM_EPS) + ADAM_WD * w)
    return delta, m, v


def adamw(w, g, m, v, name):
    return rowwise(_adam_fn, [w, g, m, v], [F32, F32, F32], name)


def _full_weight(name, gathered, local_shape):
    L, a, b = local_shape
    g = gathered.reshape((4, L, a, b))
    if SHARD_AXIS[name] == 1:
        return g.transpose(1, 0, 2, 3).reshape(L, 4 * a, b)
    return g.transpose(1, 2, 0, 3).reshape(L, a, 4 * b)


def _grad_slots(name, dw):
    L, a, b = dw.shape
    if SHARD_AXIS[name] == 1:
        s = dw.reshape(L, 4, a // 4, b).transpose(1, 0, 2, 3)
        rows, cols = L * (a // 4), b
    else:
        s = dw.reshape(L, a, 4, b // 4).transpose(2, 0, 1, 3)
        rows, cols = L * a, b // 4
    return s.reshape(4, 2, rows // 2, cols).transpose(1, 0, 2, 3)


def _attn_a(T):
    group = A_HEADS // A_KV_HEADS
    return Band(T, 1, A_HEADS, group, group, A_HEADS, 0, A_KV_HEADS, 0, A_HEADS + 2 * A_KV_HEADS,
                A_HEADS + A_KV_HEADS, 1.0 / math.sqrt(HEAD_DIM), A_HALF_WINDOW, BAND_BLOCK)


def _attn_b(T):
    return Attn(T, 1, B_HEADS, 1, B_HEADS, 0, B_HEADS, 0, 2 * B_HEADS, 1, 2, B_PAD, 1.0 / math.sqrt(B_QK), None,
                DENSE_BLOCK, DENSE_OTHER_BLOCK)


def _attn_c(T, group):
    window, dil = C_PATTERNS[group]
    return Band(T, dil, C_HEADS, 1, BAND_HEADS_PER_STEP, C_HEADS, 0, C_HEADS, 0, C_HEADS, 0,
                1.0 / math.sqrt(HEAD_DIM), window // 2 // dil, BAND_BLOCK)


def _pad_heads(a, axis_len_true, axis_len_pad):
    lead = a.shape[:-1]
    h = a.shape[-1] // axis_len_true
    a = a.reshape(lead + (h, axis_len_true))
    a = jnp.pad(a, [(0, 0)] * len(lead) + [(0, 0), (0, axis_len_pad - axis_len_true)])
    return a.reshape(lead + (h * axis_len_pad,))


def _unpad_heads(a, axis_len_true, axis_len_pad):
    lead = a.shape[:-1]
    h = a.shape[-1] // axis_len_pad
    return a.reshape(lead + (h, axis_len_pad))[..., :axis_len_true].reshape(lead + (h * axis_len_true,))


def _weight_grad(G, name, layer, a, dy, W, tag):
    layers, rows, cols = W[name].shape
    if name in SLOT_DIRECT:
        G[name] = matmul([(a, dy)], "tn", F32, tag, slot=Slot(name, layers, layer, rows, cols, 0, G.get(name)))
    else:
        G.setdefault(name, [None] * layers)[layer] = matmul([(a, dy)], "tn", F32, tag)


def _mixer_fwd(kind, slot, hn, W, S, tabs, tag):
    T = hn.shape[0]
    if kind == 0:
        cfg = _attn_a(T)
        qkv = matmul([(hn, W["a_w_in"][slot])], "nn", BF, tag + "_a_in")
        q = headnorm_fwd(qkv, W["a_q_norm"][slot], tabs["hd"], tag + "_a_qn", A_HEADS, 0, HEAD_DIM, HEAD_DIM)
        k = headnorm_fwd(qkv, W["a_k_norm"][slot], tabs["hd"], tag + "_a_kn", A_KV_HEADS, A_HEADS, HEAD_DIM, HEAD_DIM)
        sink = jnp.broadcast_to(W["a_sink"][slot][:, None, None], (A_HEADS, SUBLANES, LANES)).astype(F32)
        o, lse = band_fwd(cfg, q, k, qkv, tag + "_a_att", BF, sink=sink)
        S.update(qkv=qkv, q=q, k=k, o=o, lse=lse, sink=sink)
        return o
    if kind == 1:
        cfg = _attn_b(T)
        lat = matmul([(hn, W["b_w_in"][slot])], "nn", BF, tag + "_b_in")
        qn = rmsnorm_fwd(lat, W["b_q_lat_norm"][slot], tag + "_b_qlat", 0, B_Q_RANK)
        kvn = rmsnorm_fwd(lat, W["b_kv_lat_norm"][slot], tag + "_b_kvlat", 1, B_KV_RANK)
        qp = matmul([(qn, W["b_w_q_up_pad"][slot])], "nn", BF, tag + "_b_qup")
        kv = matmul([(kvn, W["b_w_kv_up"][slot])], "nn", BF, tag + "_b_kvup")
        k_rope = lat[:, B_Q_RANK + B_KV_RANK:]
        kpre = jnp.concatenate(
            [kv.reshape(T, B_HEADS, 2 * B_NOPE)[:, :, :B_NOPE],
             jnp.broadcast_to(k_rope[:, None, :], (T, B_HEADS, B_ROPE)),
             jnp.zeros((T, B_HEADS, B_PAD - B_QK), BF)], axis=-1).reshape(T, B_HEADS * B_PAD)
        q = headnorm_fwd(qp, W["b_q_norm_pad"][slot], tabs["b"], tag + "_b_qn", B_HEADS, 0, B_PAD, B_QK)
        k = headnorm_fwd(kpre, W["b_k_norm_pad"][slot], tabs["b"], tag + "_b_kn", B_HEADS, 0, B_PAD, B_QK)
        o, lse = flash_fwd(cfg, q, k, kv, tag + "_b_att", BF)
        S.update(lat=lat, qn=qn, kvn=kvn, qp=qp, kv=kv, kpre=kpre, q=q, k=k, o=o, lse=lse)
        return o
    qkv = matmul([(hn, W["c_w_in"][slot])], "nn", BF, tag + "_c_in")
    nq = C_GROUPS * C_HEADS
    qs = [headnorm_fwd(qkv, W["c_q_norm"][slot], tabs["hd"], f"{tag}_c_qn{g}", C_HEADS, g * C_HEADS, HEAD_DIM, HEAD_DIM)
          for g in range(C_GROUPS)]
    k = headnorm_fwd(qkv, W["c_k_norm"][slot], tabs["hd"], tag + "_c_kn", C_HEADS, nq, HEAD_DIM, HEAD_DIM)
    outs, lses = [], []
    v = qkv[:, (C_GROUPS + 1) * C_HEADS * HEAD_DIM:]
    for g in range(C_GROUPS):
        og, lg = band_fwd(_attn_c(T, g), qs[g], k, v, f"{tag}_c_att{g}", F32)
        outs.append(og)
        lses.append(lg)
    o, lse = rowwise(_merge_fn, outs + lses, [BF, F32], tag + "_c_merge")
    S.update(qkv=qkv, qs=qs, v=v, k=k, o=o, lse=lse)
    return o


def _mixer_bwd(kind, slot, hn, do, W, S, tabs, tag, G):
    T = hn.shape[0]
    if kind == 0:
        cfg = _attn_a(T)
        qkv = S["qkv"]
        dq, dsink = band_dq(cfg, S["q"], S["k"], qkv, do, S["o"], S["lse"], tag + "_a_dq", sink=S["sink"])
        blocks = A_HEADS + 2 * A_KV_HEADS
        dqkv, dgq = headnorm_bwd(qkv, W["a_q_norm"][slot], tabs["hd"], dq, tag + "_a_dqn", A_HEADS, 0, HEAD_DIM, HEAD_DIM,
                                 into=(None, blocks, 0))
        dk, dqkv = band_dkv(cfg, S["q"], S["k"], qkv, do, S["o"], S["lse"], tag + "_a_dkv", BF,
                            dv_into=(dqkv, blocks, A_HEADS + A_KV_HEADS))
        dqkv, dgk = headnorm_bwd(qkv, W["a_k_norm"][slot], tabs["hd"], dk, tag + "_a_dkn", A_KV_HEADS, A_HEADS,
                                 HEAD_DIM, HEAD_DIM, into=(dqkv, blocks, A_HEADS))
        _weight_grad(G, "a_w_in", slot, hn, dqkv, W, tag + "_a_dwin")
        G["a_q_norm"][slot], G["a_k_norm"][slot] = dgq, dgk
        parts = dsink.reshape(A_HEADS // cfg.per, cfg.nb, cfg.per, SUBLANES, LANES)[:, :, :, 0, 0]
        G["a_sink"][slot] = jnp.sum(parts, axis=1).reshape(A_HEADS)
        return matmul([(dqkv, W["a_w_in"][slot])], "nt", F32, tag + "_a_dhn")
    if kind == 1:
        cfg = _attn_b(T)
        kv = S["kv"]
        dq = flash_dq(cfg, S["q"], S["k"], kv, do, S["o"], S["lse"], tag + "_b_dq")
        dk, dv = flash_dkv(cfg, S["q"], S["k"], kv, do, S["o"], S["lse"], tag + "_b_dkv", BF)
        dqp, dgq = headnorm_bwd(S["qp"], W["b_q_norm_pad"][slot], tabs["b"], dq, tag + "_b_dqn", B_HEADS, 0, B_PAD, B_QK)
        dkp, dgk, dksum = headnorm_bwd(S["kpre"], W["b_k_norm_pad"][slot], tabs["b"], dk, tag + "_b_dkn", B_HEADS, 0,
                                       B_PAD, B_QK, head_sum=True)
        dkv = jnp.concatenate([dkp.reshape(T, B_HEADS, B_PAD)[:, :, :B_NOPE], dv.reshape(T, B_HEADS, LANES)],
                              axis=-1).reshape(T, B_HEADS * 2 * B_NOPE)
        _weight_grad(G, "b_w_kv_up", slot, S["kvn"], dkv, W, tag + "_b_dwkv")
        G["b_w_q_up"][slot] = _unpad_heads(matmul([(S["qn"], dqp)], "tn", F32, tag + "_b_dwq"), B_QK, B_PAD)
        dqn = matmul([(dqp, W["b_w_q_up_pad"][slot])], "nt", F32, tag + "_b_dqnorm")
        dkvn = matmul([(dkv, W["b_w_kv_up"][slot])], "nt", F32, tag + "_b_dkvnorm")
        dql, dg_q = rmsnorm_bwd(S["lat"], W["b_q_lat_norm"][slot], dqn, tag + "_b_dqlat", [BF], None, 0, B_Q_RANK)
        dkvl, dg_kv = rmsnorm_bwd(S["lat"], W["b_kv_lat_norm"][slot], dkvn, tag + "_b_dkvlat", [BF], None, 1, B_KV_RANK)
        dlat = jnp.concatenate([dql, dkvl, dksum[:, B_NOPE:B_QK].astype(BF)], axis=1)
        _weight_grad(G, "b_w_in", slot, hn, dlat, W, tag + "_b_dwin")
        G["b_q_norm"][slot], G["b_k_norm"][slot] = dgq[:B_QK], dgk[:B_QK]
        G["b_q_lat_norm"][slot], G["b_kv_lat_norm"][slot] = dg_q, dg_kv
        return matmul([(dlat, W["b_w_in"][slot])], "nt", F32, tag + "_b_dhn")
    qkv = S["qkv"]
    nq = C_GROUPS * C_HEADS
    blocks = (C_GROUPS + 2) * C_HEADS
    dqkv, dgq = None, 0.0
    for g in range(C_GROUPS):
        dq = band_dq(_attn_c(T, g), S["qs"][g], S["k"], S["v"], do, S["o"], S["lse"], f"{tag}_c_dq{g}")
        dqkv, dg = headnorm_bwd(qkv, W["c_q_norm"][slot], tabs["hd"], dq, f"{tag}_c_dqn{g}", C_HEADS, g * C_HEADS,
                                HEAD_DIM, HEAD_DIM, into=(dqkv, blocks, g * C_HEADS))
        dgq = dgq + dg
    acc = None
    for g in reversed(range(C_GROUPS)):
        into = (dqkv, blocks, (C_GROUPS + 1) * C_HEADS) if g == 0 else None
        acc = band_dkv(_attn_c(T, g), S["qs"][g], S["k"], S["v"], do, S["o"], S["lse"], f"{tag}_c_dkv{g}", F32,
                       add=acc, dv_into=into)
    dk, dqkv = acc
    dqkv, dgk = headnorm_bwd(qkv, W["c_k_norm"][slot], tabs["hd"], dk, tag + "_c_dkn", C_HEADS, nq, HEAD_DIM, HEAD_DIM,
                             into=(dqkv, blocks, nq))
    _weight_grad(G, "c_w_in", slot, hn, dqkv, W, tag + "_c_dwin")
    G["c_q_norm"][slot], G["c_k_norm"][slot] = dgq, dgk
    return matmul([(dqkv, W["c_w_in"][slot])], "nt", F32, tag + "_c_dhn")


MIXER_OUT = ("a_w_o", "b_w_o", "c_w_o")


def local_step(x, p, positions, loss_target, W):
    T = x.shape[0]
    tabs = {"hd": rope_tables(positions, HEAD_DIM, 0, PARTIAL_ROT), "b": rope_tables(positions, B_PAD, B_NOPE, B_ROPE)}
    W = dict(W)
    W["b_w_q_up_pad"] = _pad_heads(W["b_w_q_up"], B_QK, B_PAD)
    W["b_q_norm_pad"] = _pad_heads(W["b_q_norm"], B_QK, B_PAD)
    W["b_k_norm_pad"] = _pad_heads(W["b_k_norm"], B_QK, B_PAD)
    saved = []
    h = x
    for i in range(DEPTH):
        kind, slot = i % 3, i // 3
        tag = f"l{i}"
        S = {"h0": h}
        hn = rmsnorm_fwd(h, W["g_mix"][i], tag + "_mixnorm")
        o = _mixer_fwd(kind, slot, hn, W, S, tabs, tag)
        h1 = matmul([(o, W[MIXER_OUT[kind]][slot])], "nn", F32, tag + "_mixout", res=h)
        hn2 = rmsnorm_fwd(h1, W["g_ffn"][i], tag + "_ffnnorm")
        a, b, c = matmul_swiglu(hn2, W["w_ffn_gate"][i], W["w_ffn_up"][i], tag + "_gateup")
        h2 = matmul([(c, W["w_ffn_down"][i])], "nn", F32, tag + "_down", res=h1)
        hn3 = rmsnorm_fwd(h2, W["g_ple"][i], tag + "_plenorm")
        p_i = p[i].astype(BF)
        pp = matmul([(p_i, W["w_ple_proj"][i])], "nn", BF, tag + "_pleproj")
        z, h3 = matmul([(hn3, W["w_ple_gate"][i])], "nn", BF, tag + "_plegate", ple=(h2, pp))
        S.update(hn=hn, h1=h1, hn2=hn2, a=a, b=b, c=c, h2=h2, hn3=hn3, z=z, pp=pp, p=p_i)
        saved.append(S)
        h = h3

    loss, dh = loss_and_grad(h, loss_target, "loss")
    G = {n: [None] * W[n].shape[0] for n in SMALL + ("b_w_q_up",)}
    for i in reversed(range(DEPTH)):
        kind, slot = i % 3, i // 3
        tag = f"l{i}"
        S = saved[i]
        dz, dpp = rowwise(_ple_bwd_fn, [dh, S["z"], S["pp"]], [BF, BF], tag + "_dple")
        _weight_grad(G, "w_ple_proj", i, S["p"], dpp, W, tag + "_dwpleproj")
        _weight_grad(G, "w_ple_gate", i, S["hn3"], dz, W, tag + "_dwplegate")
        dhn3 = matmul([(dz, W["w_ple_gate"][i])], "nt", F32, tag + "_dplenorm")
        dh2, dh2b, G["g_ple"][i] = rmsnorm_bwd(S["h2"], W["g_ple"][i], dhn3, tag + "_dple_norm", [F32, BF], dres=dh)
        da, db = matmul([(dh2b, W["w_ffn_down"][i])], "nt", BF, tag + "_dswiglu", swiglu=(S["a"], S["b"]))
        _weight_grad(G, "w_ffn_down", i, S["c"], dh2b, W, tag + "_dwdown")
        _weight_grad(G, "w_ffn_gate", i, S["hn2"], da, W, tag + "_dwgate")
        _weight_grad(G, "w_ffn_up", i, S["hn2"], db, W, tag + "_dwup")
        dhn2 = matmul([(da, W["w_ffn_gate"][i]), (db, W["w_ffn_up"][i])], "nt", F32, tag + "_dffnnorm")
        dh1, dh1b, G["g_ffn"][i] = rmsnorm_bwd(S["h1"], W["g_ffn"][i], dhn2, tag + "_dffn_norm", [F32, BF], dres=dh2)
        wo = W[MIXER_OUT[kind]][slot]
        do = matmul([(dh1b, wo)], "nt", BF, tag + "_dmixout")
        _weight_grad(G, MIXER_OUT[kind], slot, S["o"], dh1b, W, tag + "_dwmixout")
        dhn = _mixer_bwd(kind, slot, S["hn"], do, W, S, tabs, tag, G)
        dh, G["g_mix"][i] = rmsnorm_bwd(S["h0"], W["g_mix"][i], dhn, tag + "_dmix_norm", [F32], dres=dh1)
    return loss, dh, G


def _pack_small(vals):
    flat = jnp.concatenate([vals[n].reshape(-1).astype(F32) for n in SMALL])
    rows = -(-flat.shape[0] // SMALL_COLS)
    rows = -(-rows // SUBLANES) * SUBLANES
    return jnp.pad(flat, (0, rows * SMALL_COLS - flat.shape[0])).reshape(rows, SMALL_COLS)


def _unpack_small(packed, like):
    flat = packed.reshape(-1)
    out, off = {}, 0
    for n in SMALL:
        size = like[n].size
        out[n] = flat[off:off + size].reshape(like[n].shape)
        off += size
    return out


def kernel(x, p, positions, g_mix, g_ffn, g_ple, w_ple_gate, w_ple_proj, w_ffn_gate, w_ffn_up, w_ffn_down, a_w_in, a_q_norm, a_k_norm, a_sink, a_w_o, b_w_in, b_q_lat_norm, b_kv_lat_norm, b_w_q_up, b_w_kv_up, b_q_norm, b_k_norm, b_w_o, c_w_in, c_q_norm, c_k_norm, c_w_o, loss_target, m_g_mix, m_g_ffn, m_g_ple, m_w_ple_gate, m_w_ple_proj, m_w_ffn_gate, m_w_ffn_up, m_w_ffn_down, m_a_w_in, m_a_q_norm, m_a_k_norm, m_a_sink, m_a_w_o, m_b_w_in, m_b_q_lat_norm, m_b_kv_lat_norm, m_b_w_q_up, m_b_w_kv_up, m_b_q_norm, m_b_k_norm, m_b_w_o, m_c_w_in, m_c_q_norm, m_c_k_norm, m_c_w_o, v_g_mix, v_g_ffn, v_g_ple, v_w_ple_gate, v_w_ple_proj, v_w_ffn_gate, v_w_ffn_up, v_w_ffn_down, v_a_w_in, v_a_q_norm, v_a_k_norm, v_a_sink, v_a_w_o, v_b_w_in, v_b_q_lat_norm, v_b_kv_lat_norm, v_b_w_q_up, v_b_w_kv_up, v_b_q_norm, v_b_k_norm, v_b_w_o, v_c_w_in, v_c_q_norm, v_c_k_norm, v_c_w_o):
    args = dict(locals())
    w_loc = {n: args[n] for n in WEIGHTS}
    m_loc = {n: args["m_" + n] for n in WEIGHTS}
    v_loc = {n: args["v_" + n] for n in WEIGHTS}

    def halves(a):
        rows = a.shape[0] * a.shape[1]
        return a.reshape(2, rows // 2, a.shape[2])

    gathered = gather_weights([halves(w_loc[n].astype(BF)) for n in BIG])
    W = {n: _full_weight(n, g, w_loc[n].shape) for n, g in zip(BIG, gathered)}
    for n in SMALL:
        W[n] = w_loc[n]

    loss, dx, G = local_step(x[0], p[:, 0], positions[0], loss_target[0], W)
    loss = lax.psum(loss, ("x", "y", "c"))

    core = _core_index()
    landed, ready = [], None
    for n in EXCHANGE_ORDER:
        s = _grad_slots(n, jnp.stack(G[n])) if isinstance(G[n], list) else G[n]
        g2 = s.reshape(2, 4 * s.shape[2], s.shape[3])
        if ready is None:
            part = presum_core_halves(g2, core, "presum_" + n)
        else:
            part, got = presum_core_halves(g2, core, "presum_" + n, ship=ready)
            landed.append(got)
        ready = part.reshape(s.shape[1:])
    landed += scatter_partials([ready])
    halves = [sum_and_swap(a, "sum_" + n) for n, a in zip(EXCHANGE_ORDER, landed)]

    small = gather_small(_pack_small({n: jnp.stack(G[n]) for n in SMALL}))
    small_sum = sum_slots(small, F32, "sum_small")
    grads = _unpack_small(small_sum, w_loc)

    delta, new_m, new_v = {}, {}, {}
    for n, (mine, theirs) in zip(EXCHANGE_ORDER, halves):
        shape = w_loc[n].shape
        two_d = (shape[0] * shape[1], shape[2])
        g, d, m, v = adamw_halves(w_loc[n].reshape(two_d), mine, theirs, m_loc[n].reshape(two_d),
                                  v_loc[n].reshape(two_d), core, "adamw_" + n)
        grads[n], delta[n], new_m[n], new_v[n] = g.reshape(shape), d.reshape(shape), m.reshape(shape), v.reshape(shape)
    d, m, v = adamw(_pack_small(w_loc), small_sum, _pack_small(m_loc), _pack_small(v_loc), "adamw_small")
    delta.update(_unpack_small(d, w_loc))
    new_m.update(_unpack_small(m, w_loc))
    new_v.update(_unpack_small(v, w_loc))

    return (loss, dx[None], *[grads[n] for n in WEIGHTS], *[delta[n] for n in WEIGHTS],
            *[new_m[n] for n in WEIGHTS], *[new_v[n] for n in WEIGHTS])
```
